```python
import jax, jax.numpy as jnp
from jax import lax
import numpy as np

D_MODEL = 1024
BATCH = 8
SEQ = 4096
DEPTH = 2

CHUNK = 64
N_MIXERS = 2
N_A = (DEPTH + 1) // 2
N_B = DEPTH // 2
SHORT_CONV_WIDTH = 3
CONFORMER_CONV_WIDTH = 31
D_FF = ((8 * D_MODEL // 3 + 255) // 256) * 256
RMS_EPS = 1e-6
LN_EPS = 1e-5

kernel_name = "hybrid_shortconv_conformer_conv_encoder"


def rms_norm(x, g):
    xf = x.astype(jnp.float32)
    y = xf * lax.rsqrt(jnp.mean(xf * xf, axis=-1, keepdims=True) + RMS_EPS)
    return (y * g.astype(jnp.float32)).astype(x.dtype)


def layer_norm(x, g, b):
    xf = x.astype(jnp.float32)
    mu = jnp.mean(xf, axis=-1, keepdims=True)
    var = jnp.mean(jnp.square(xf - mu), axis=-1, keepdims=True)
    y = (xf - mu) * lax.rsqrt(var + LN_EPS)
    return (y * g.astype(jnp.float32) + b.astype(jnp.float32)).astype(x.dtype)


def causal_depthwise_conv(x, w):
    k = w.shape[0]
    return lax.conv_general_dilated(
        x, w[:, None, :].astype(x.dtype), window_strides=(1,),
        padding=[(k - 1, 0)], dimension_numbers=("NWC", "WIO", "NWC"),
        feature_group_count=x.shape[-1])


def short_gated_conv(h, w_in, w_conv, w_out):
    bcv = jnp.einsum("bsd,de->bse", h, w_in)
    gate_b, gate_c, v = jnp.split(bcv, 3, axis=-1)
    y = gate_b * causal_depthwise_conv(gate_c * v, w_conv)
    return jnp.einsum("bsd,de->bse", y, w_out)


def conformer_conv_module(h, w_pw1, b_pw1, w_dw, b_dw, ln_g, ln_b, w_pw2, b_pw2):
    u = jnp.einsum("bsd,de->bse", h, w_pw1) + b_pw1
    a, g = jnp.split(u, 2, axis=-1)
    u = a * jax.nn.sigmoid(g)
    u = causal_depthwise_conv(u, w_dw) + b_dw
    u = jax.nn.silu(layer_norm(u, ln_g, ln_b))
    return jnp.einsum("bsd,de->bse", u, w_pw2) + b_pw2


def swiglu(h, w_gate, w_up, w_down):
    gu = jax.nn.silu(jnp.einsum("bsd,df->bsf", h, w_gate)) * jnp.einsum("bsd,df->bsf", h, w_up)
    return jnp.einsum("bsf,fd->bsd", gu, w_down)


def _fwd_setup_inputs(seed: int = 0) -> dict:
    key = jax.random.key(seed)
    ks = jax.random.split(key, 24)
    D, F = D_MODEL, D_FF
    nrm = lambda k, shape, fan_in: jax.random.normal(k, shape, jnp.float32) * (fan_in ** -0.5)
    gain = lambda k, shape: 1.0 + 0.02 * jax.random.normal(k, shape, jnp.float32)
    small = lambda k, shape: 0.02 * jax.random.normal(k, shape, jnp.float32)
    return {
        "x": jax.random.normal(ks[0], (BATCH, SEQ, D), jnp.float32),
        "a_norm": gain(ks[1], (N_A, D)),
        "a_w_in": nrm(ks[2], (N_A, D, 3 * D), D),
        "a_conv": nrm(ks[3], (N_A, SHORT_CONV_WIDTH, D), SHORT_CONV_WIDTH),
        "a_w_out": nrm(ks[4], (N_A, D, D), D),
        "b_norm": gain(ks[5], (N_B, D)),
        "b_w_pw1": nrm(ks[6], (N_B, D, 2 * D), D),
        "b_b_pw1": small(ks[7], (N_B, 2 * D)),
        "b_conv": nrm(ks[8], (N_B, CONFORMER_CONV_WIDTH, D), CONFORMER_CONV_WIDTH),
        "b_b_conv": small(ks[9], (N_B, D)),
        "b_ln_g": gain(ks[10], (N_B, D)),
        "b_ln_b": small(ks[11], (N_B, D)),
        "b_w_pw2": nrm(ks[12], (N_B, D, D), D),
        "b_b_pw2": small(ks[13], (N_B, D)),
        "ffn_norm": gain(ks[14], (DEPTH, D)),
        "ffn_w_gate": nrm(ks[15], (DEPTH, D, F), D),
        "ffn_w_up": nrm(ks[16], (DEPTH, D, F), D),
        "ffn_w_down": nrm(ks[17], (DEPTH, F, D), F),
        "final_norm": gain(ks[18], (D,)),
    }


def _fwd_reference(x, a_norm, a_w_in, a_conv, a_w_out,
              b_norm, b_w_pw1, b_b_pw1, b_conv, b_b_conv, b_ln_g, b_ln_b, b_w_pw2, b_b_pw2,
              ffn_norm, ffn_w_gate, ffn_w_up, ffn_w_down, final_norm):
    h = x
    for i in range(DEPTH):
        j = i // N_MIXERS
        if i % N_MIXERS == 0:
            h = h + short_gated_conv(rms_norm(h, a_norm[j]), a_w_in[j], a_conv[j], a_w_out[j])
        else:
            h = h + conformer_conv_module(
                rms_norm(h, b_norm[j]), b_w_pw1[j], b_b_pw1[j], b_conv[j], b_b_conv[j],
                b_ln_g[j], b_ln_b[j], b_w_pw2[j], b_b_pw2[j])
        h = h + swiglu(rms_norm(h, ffn_norm[i]), ffn_w_gate[i], ffn_w_up[i], ffn_w_down[i])
    return rms_norm(h, final_norm)


import jax as _jax
import jax.numpy as _jnp

TWIN_FORMAT = 'train_step'
FWD_PARAMS = ['x', 'a_norm', 'a_w_in', 'a_conv', 'a_w_out', 'b_norm', 'b_w_pw1', 'b_b_pw1', 'b_conv', 'b_b_conv', 'b_ln_g', 'b_ln_b', 'b_w_pw2', 'b_b_pw2', 'ffn_norm', 'ffn_w_gate', 'ffn_w_up', 'ffn_w_down', 'final_norm']
TWIN_WEIGHTS = ['a_norm', 'a_w_in', 'a_conv', 'a_w_out', 'b_norm', 'b_w_pw1', 'b_b_pw1', 'b_conv', 'b_b_conv', 'b_ln_g', 'b_ln_b', 'b_w_pw2', 'b_b_pw2', 'ffn_norm', 'ffn_w_gate', 'ffn_w_up', 'ffn_w_down', 'final_norm']
TWIN_DIFF_INPUT = 'x'
TWIN_INPUTS = ['x', 'a_norm', 'a_w_in', 'a_conv', 'a_w_out', 'b_norm', 'b_w_pw1', 'b_b_pw1', 'b_conv', 'b_b_conv', 'b_ln_g', 'b_ln_b', 'b_w_pw2', 'b_b_pw2', 'ffn_norm', 'ffn_w_gate', 'ffn_w_up', 'ffn_w_down', 'final_norm', 'loss_target', 'm_a_norm', 'm_a_w_in', 'm_a_conv', 'm_a_w_out', 'm_b_norm', 'm_b_w_pw1', 'm_b_b_pw1', 'm_b_conv', 'm_b_b_conv', 'm_b_ln_g', 'm_b_ln_b', 'm_b_w_pw2', 'm_b_b_pw2', 'm_ffn_norm', 'm_ffn_w_gate', 'm_ffn_w_up', 'm_ffn_w_down', 'm_final_norm', 'v_a_norm', 'v_a_w_in', 'v_a_conv', 'v_a_w_out', 'v_b_norm', 'v_b_w_pw1', 'v_b_b_pw1', 'v_b_conv', 'v_b_b_conv', 'v_b_ln_g', 'v_b_ln_b', 'v_b_w_pw2', 'v_b_b_pw2', 'v_ffn_norm', 'v_ffn_w_gate', 'v_ffn_w_up', 'v_ffn_w_down', 'v_final_norm']
TWIN_OUTPUTS = ['loss', 'grad_x', 'grad_a_norm', 'grad_a_w_in', 'grad_a_conv', 'grad_a_w_out', 'grad_b_norm', 'grad_b_w_pw1', 'grad_b_b_pw1', 'grad_b_conv', 'grad_b_b_conv', 'grad_b_ln_g', 'grad_b_ln_b', 'grad_b_w_pw2', 'grad_b_b_pw2', 'grad_ffn_norm', 'grad_ffn_w_gate', 'grad_ffn_w_up', 'grad_ffn_w_down', 'grad_final_norm', 'delta_a_norm', 'delta_a_w_in', 'delta_a_conv', 'delta_a_w_out', 'delta_b_norm', 'delta_b_w_pw1', 'delta_b_b_pw1', 'delta_b_conv', 'delta_b_b_conv', 'delta_b_ln_g', 'delta_b_ln_b', 'delta_b_w_pw2', 'delta_b_b_pw2', 'delta_ffn_norm', 'delta_ffn_w_gate', 'delta_ffn_w_up', 'delta_ffn_w_down', 'delta_final_norm', 'new_m_a_norm', 'new_m_a_w_in', 'new_m_a_conv', 'new_m_a_w_out', 'new_m_b_norm', 'new_m_b_w_pw1', 'new_m_b_b_pw1', 'new_m_b_conv', 'new_m_b_b_conv', 'new_m_b_ln_g', 'new_m_b_ln_b', 'new_m_b_w_pw2', 'new_m_b_b_pw2', 'new_m_ffn_norm', 'new_m_ffn_w_gate', 'new_m_ffn_w_up', 'new_m_ffn_w_down', 'new_m_final_norm', 'new_v_a_norm', 'new_v_a_w_in', 'new_v_a_conv', 'new_v_a_w_out', 'new_v_b_norm', 'new_v_b_w_pw1', 'new_v_b_b_pw1', 'new_v_b_conv', 'new_v_b_b_conv', 'new_v_b_ln_g', 'new_v_b_ln_b', 'new_v_b_w_pw2', 'new_v_b_b_pw2', 'new_v_ffn_norm', 'new_v_ffn_w_gate', 'new_v_ffn_w_up', 'new_v_ffn_w_down', 'new_v_final_norm']
TWIN_LEAF_KINDS = {'loss': 'loss', 'grad_x': 'grad_x', 'grad_a_norm': 'grad_w', 'grad_a_w_in': 'grad_w', 'grad_a_conv': 'grad_w', 'grad_a_w_out': 'grad_w', 'grad_b_norm': 'grad_w', 'grad_b_w_pw1': 'grad_w', 'grad_b_b_pw1': 'grad_w', 'grad_b_conv': 'grad_w', 'grad_b_b_conv': 'grad_w', 'grad_b_ln_g': 'grad_w', 'grad_b_ln_b': 'grad_w', 'grad_b_w_pw2': 'grad_w', 'grad_b_b_pw2': 'grad_w', 'grad_ffn_norm': 'grad_w', 'grad_ffn_w_gate': 'grad_w', 'grad_ffn_w_up': 'grad_w', 'grad_ffn_w_down': 'grad_w', 'grad_final_norm': 'grad_w', 'delta_a_norm': 'delta_w', 'delta_a_w_in': 'delta_w', 'delta_a_conv': 'delta_w', 'delta_a_w_out': 'delta_w', 'delta_b_norm': 'delta_w', 'delta_b_w_pw1': 'delta_w', 'delta_b_b_pw1': 'delta_w', 'delta_b_conv': 'delta_w', 'delta_b_b_conv': 'delta_w', 'delta_b_ln_g': 'delta_w', 'delta_b_ln_b': 'delta_w', 'delta_b_w_pw2': 'delta_w', 'delta_b_b_pw2': 'delta_w', 'delta_ffn_norm': 'delta_w', 'delta_ffn_w_gate': 'delta_w', 'delta_ffn_w_up': 'delta_w', 'delta_ffn_w_down': 'delta_w', 'delta_final_norm': 'delta_w', 'new_m_a_norm': 'new_m', 'new_m_a_w_in': 'new_m', 'new_m_a_conv': 'new_m', 'new_m_a_w_out': 'new_m', 'new_m_b_norm': 'new_m', 'new_m_b_w_pw1': 'new_m', 'new_m_b_b_pw1': 'new_m', 'new_m_b_conv': 'new_m', 'new_m_b_b_conv': 'new_m', 'new_m_b_ln_g': 'new_m', 'new_m_b_ln_b': 'new_m', 'new_m_b_w_pw2': 'new_m', 'new_m_b_b_pw2': 'new_m', 'new_m_ffn_norm': 'new_m', 'new_m_ffn_w_gate': 'new_m', 'new_m_ffn_w_up': 'new_m', 'new_m_ffn_w_down': 'new_m', 'new_m_final_norm': 'new_m', 'new_v_a_norm': 'new_v', 'new_v_a_w_in': 'new_v', 'new_v_a_conv': 'new_v', 'new_v_a_w_out': 'new_v', 'new_v_b_norm': 'new_v', 'new_v_b_w_pw1': 'new_v', 'new_v_b_b_pw1': 'new_v', 'new_v_b_conv': 'new_v', 'new_v_b_b_conv': 'new_v', 'new_v_b_ln_g': 'new_v', 'new_v_b_ln_b': 'new_v', 'new_v_b_w_pw2': 'new_v', 'new_v_b_b_pw2': 'new_v', 'new_v_ffn_norm': 'new_v', 'new_v_ffn_w_gate': 'new_v', 'new_v_ffn_w_up': 'new_v', 'new_v_ffn_w_down': 'new_v', 'new_v_final_norm': 'new_v'}


def _forward(args):
    return _fwd_reference(*[args[k] for k in FWD_PARAMS])


def _output_shape():
    out = _jax.eval_shape(lambda: _forward(_fwd_setup_inputs(0)))
    return out.shape, out.dtype

N_MICROBATCH = 1
ADAM_LR = 0.001
ADAM_B1 = 0.9
ADAM_B2 = 0.999
ADAM_EPS = 1e-08
ADAM_WD = 0.01
ADAM_STEP = 10
PER_EXAMPLE_BATCH_AXIS = {'x': 0, 'loss_target': 0}
SHARED_INPUTS = []
_WEIGHT_DTYPES = {'a_norm': _jnp.float32, 'a_w_in': _jnp.float32, 'a_conv': _jnp.float32, 'a_w_out': _jnp.float32, 'b_norm': _jnp.float32, 'b_w_pw1': _jnp.float32, 'b_b_pw1': _jnp.float32, 'b_conv': _jnp.float32, 'b_b_conv': _jnp.float32, 'b_ln_g': _jnp.float32, 'b_ln_b': _jnp.float32, 'b_w_pw2': _jnp.float32, 'b_b_pw2': _jnp.float32, 'ffn_norm': _jnp.float32, 'ffn_w_gate': _jnp.float32, 'ffn_w_up': _jnp.float32, 'ffn_w_down': _jnp.float32, 'final_norm': _jnp.float32}
MOMENT_SCALE = {'a_norm': 2.682000e-01, 'a_w_in': 1.446571e-01, 'a_conv': 1.509712e-01, 'a_w_out': 1.444591e-01, 'b_norm': 7.773838e-02, 'b_w_pw1': 5.438086e-02, 'b_b_pw1': 5.719230e-02, 'b_conv': 7.193784e-02, 'b_b_conv': 1.568332e-01, 'b_ln_g': 8.438982e-02, 'b_ln_b': 7.187203e-02, 'b_w_pw2': 6.976892e-02, 'b_b_pw2': 1.250777e-01, 'ffn_norm': 1.034968e-01, 'ffn_w_gate': 4.254013e-02, 'ffn_w_up': 4.121685e-02, 'ffn_w_down': 6.840249e-02, 'final_norm': 3.194128e+01}


def _to_microbatches(a, axis):
    t = _jnp.moveaxis(a, axis, 0)
    t = t.reshape((N_MICROBATCH, t.shape[0] // N_MICROBATCH) + t.shape[1:])
    return _jnp.moveaxis(t, 1, axis + 1)


def setup_inputs(seed: int = 0) -> dict:
    inp = _fwd_setup_inputs(seed)
    key = _jax.random.fold_in(_jax.random.key(seed), 7919)
    shape, _ = _output_shape()
    out = dict(inp)
    out["loss_target"] = _jax.random.normal(_jax.random.fold_in(key, 0), shape, _jnp.float32)
    for i, name in enumerate(TWIN_WEIGHTS):
        w = inp[name].astype(_jnp.float32)
        if MOMENT_SCALE is None:
            s = _jnp.sqrt(_jnp.mean(_jnp.square(w)) + 1e-30)
        else:
            s = MOMENT_SCALE[name]
        km, kv = _jax.random.split(_jax.random.fold_in(key, i + 1))
        out[name] = w
        out["m_" + name] = s * _jax.random.normal(km, w.shape, _jnp.float32)
        out["v_" + name] = (s * s) * _jax.random.uniform(kv, w.shape, _jnp.float32, 0.5, 1.5)
    if N_MICROBATCH > 1:
        for name, axis in PER_EXAMPLE_BATCH_AXIS.items():
            out[name] = _to_microbatches(out[name], axis)
    return {'x': out['x'], 'a_norm': out['a_norm'], 'a_w_in': out['a_w_in'], 'a_conv': out['a_conv'], 'a_w_out': out['a_w_out'], 'b_norm': out['b_norm'], 'b_w_pw1': out['b_w_pw1'], 'b_b_pw1': out['b_b_pw1'], 'b_conv': out['b_conv'], 'b_b_conv': out['b_b_conv'], 'b_ln_g': out['b_ln_g'], 'b_ln_b': out['b_ln_b'], 'b_w_pw2': out['b_w_pw2'], 'b_b_pw2': out['b_b_pw2'], 'ffn_norm': out['ffn_norm'], 'ffn_w_gate': out['ffn_w_gate'], 'ffn_w_up': out['ffn_w_up'], 'ffn_w_down': out['ffn_w_down'], 'final_norm': out['final_norm'], 'loss_target': out['loss_target'], 'm_a_norm': out['m_a_norm'], 'm_a_w_in': out['m_a_w_in'], 'm_a_conv': out['m_a_conv'], 'm_a_w_out': out['m_a_w_out'], 'm_b_norm': out['m_b_norm'], 'm_b_w_pw1': out['m_b_w_pw1'], 'm_b_b_pw1': out['m_b_b_pw1'], 'm_b_conv': out['m_b_conv'], 'm_b_b_conv': out['m_b_b_conv'], 'm_b_ln_g': out['m_b_ln_g'], 'm_b_ln_b': out['m_b_ln_b'], 'm_b_w_pw2': out['m_b_w_pw2'], 'm_b_b_pw2': out['m_b_b_pw2'], 'm_ffn_norm': out['m_ffn_norm'], 'm_ffn_w_gate': out['m_ffn_w_gate'], 'm_ffn_w_up': out['m_ffn_w_up'], 'm_ffn_w_down': out['m_ffn_w_down'], 'm_final_norm': out['m_final_norm'], 'v_a_norm': out['v_a_norm'], 'v_a_w_in': out['v_a_w_in'], 'v_a_conv': out['v_a_conv'], 'v_a_w_out': out['v_a_w_out'], 'v_b_norm': out['v_b_norm'], 'v_b_w_pw1': out['v_b_w_pw1'], 'v_b_b_pw1': out['v_b_b_pw1'], 'v_b_conv': out['v_b_conv'], 'v_b_b_conv': out['v_b_b_conv'], 'v_b_ln_g': out['v_b_ln_g'], 'v_b_ln_b': out['v_b_ln_b'], 'v_b_w_pw2': out['v_b_w_pw2'], 'v_b_b_pw2': out['v_b_b_pw2'], 'v_ffn_norm': out['v_ffn_norm'], 'v_ffn_w_gate': out['v_ffn_w_gate'], 'v_ffn_w_up': out['v_ffn_w_up'], 'v_ffn_w_down': out['v_ffn_w_down'], 'v_final_norm': out['v_final_norm']}


def _loss(weights, diff, rest, loss_target):
    with _jax.named_scope("forward"):
        args = {**rest, TWIN_DIFF_INPUT: diff, **{k: w.astype(_WEIGHT_DTYPES[k]) for k, w in weights.items()}}
        y = _forward(args)
    with _jax.named_scope("loss_head"):
        err = _jnp.square(y.astype(_jnp.float32) - loss_target)
        return 0.5 * _jnp.sum(_jnp.mean(err, axis=-1)) if err.ndim else 0.5 * err


def _adamw(w, g, m, v):
    m = ADAM_B1 * m + (1.0 - ADAM_B1) * g
    v = ADAM_B2 * v + (1.0 - ADAM_B2) * _jnp.square(g)
    m_hat = m / (1.0 - ADAM_B1 ** ADAM_STEP)
    v_hat = v / (1.0 - ADAM_B2 ** ADAM_STEP)
    delta = -ADAM_LR * (m_hat / (_jnp.sqrt(v_hat) + ADAM_EPS) + ADAM_WD * w)
    return delta, m, v


def reference(x, a_norm, a_w_in, a_conv, a_w_out, b_norm, b_w_pw1, b_b_pw1, b_conv, b_b_conv, b_ln_g, b_ln_b, b_w_pw2, b_b_pw2, ffn_norm, ffn_w_gate, ffn_w_up, ffn_w_down, final_norm, loss_target, m_a_norm, m_a_w_in, m_a_conv, m_a_w_out, m_b_norm, m_b_w_pw1, m_b_b_pw1, m_b_conv, m_b_b_conv, m_b_ln_g, m_b_ln_b, m_b_w_pw2, m_b_b_pw2, m_ffn_norm, m_ffn_w_gate, m_ffn_w_up, m_ffn_w_down, m_final_norm, v_a_norm, v_a_w_in, v_a_conv, v_a_w_out, v_b_norm, v_b_w_pw1, v_b_b_pw1, v_b_conv, v_b_b_conv, v_b_ln_g, v_b_ln_b, v_b_w_pw2, v_b_b_pw2, v_ffn_norm, v_ffn_w_gate, v_ffn_w_up, v_ffn_w_down, v_final_norm):
    given = dict(x=x, a_norm=a_norm, a_w_in=a_w_in, a_conv=a_conv, a_w_out=a_w_out, b_norm=b_norm, b_w_pw1=b_w_pw1, b_b_pw1=b_b_pw1, b_conv=b_conv, b_b_conv=b_b_conv, b_ln_g=b_ln_g, b_ln_b=b_ln_b, b_w_pw2=b_w_pw2, b_b_pw2=b_b_pw2, ffn_norm=ffn_norm, ffn_w_gate=ffn_w_gate, ffn_w_up=ffn_w_up, ffn_w_down=ffn_w_down, final_norm=final_norm, loss_target=loss_target, m_a_norm=m_a_norm, m_a_w_in=m_a_w_in, m_a_conv=m_a_conv, m_a_w_out=m_a_w_out, m_b_norm=m_b_norm, m_b_w_pw1=m_b_w_pw1, m_b_b_pw1=m_b_b_pw1, m_b_conv=m_b_conv, m_b_b_conv=m_b_b_conv, m_b_ln_g=m_b_ln_g, m_b_ln_b=m_b_ln_b, m_b_w_pw2=m_b_w_pw2, m_b_b_pw2=m_b_b_pw2, m_ffn_norm=m_ffn_norm, m_ffn_w_gate=m_ffn_w_gate, m_ffn_w_up=m_ffn_w_up, m_ffn_w_down=m_ffn_w_down, m_final_norm=m_final_norm, v_a_norm=v_a_norm, v_a_w_in=v_a_w_in, v_a_conv=v_a_conv, v_a_w_out=v_a_w_out, v_b_norm=v_b_norm, v_b_w_pw1=v_b_w_pw1, v_b_b_pw1=v_b_b_pw1, v_b_conv=v_b_conv, v_b_b_conv=v_b_b_conv, v_b_ln_g=v_b_ln_g, v_b_ln_b=v_b_ln_b, v_b_w_pw2=v_b_w_pw2, v_b_b_pw2=v_b_b_pw2, v_ffn_norm=v_ffn_norm, v_ffn_w_gate=v_ffn_w_gate, v_ffn_w_up=v_ffn_w_up, v_ffn_w_down=v_ffn_w_down, v_final_norm=v_final_norm)
    weights = {n: given[n] for n in TWIN_WEIGHTS}
    shared = {n: given[n] for n in SHARED_INPUTS}
    per_example = {n: given[n] for n in ['x']}
    grad_fn = _jax.value_and_grad(_loss, argnums=(0, 1))

    def one_microbatch(ex, loss_target):
        ex = dict(ex)
        diff = ex.pop(TWIN_DIFF_INPUT)
        return grad_fn(weights, diff, {**shared, **ex}, loss_target)

    if N_MICROBATCH == 1:
        loss, (grad_w, grad_x) = one_microbatch(per_example, given["loss_target"])
    else:
        def body(carry, xs):
            loss_sum, grad_sum = carry
            l_k, (gw_k, gx_k) = one_microbatch(xs[0], xs[1])
            with _jax.named_scope("update"):
                return (loss_sum + l_k, _jax.tree.map(_jnp.add, grad_sum, gw_k)), gx_k

        init = (_jnp.zeros((), _jnp.float32), _jax.tree.map(_jnp.zeros_like, weights))
        (loss, grad_w), grad_x = _jax.lax.scan(body, init, (per_example, given["loss_target"]))
    with _jax.named_scope("update"):
        delta_w, new_m, new_v = {}, {}, {}
        for n in TWIN_WEIGHTS:
            delta_w[n], new_m[n], new_v[n] = _adamw(weights[n], grad_w[n], given["m_" + n], given["v_" + n])
    return (loss, grad_x, *[grad_w[n] for n in TWIN_WEIGHTS], *[delta_w[n] for n in TWIN_WEIGHTS],
            *[new_m[n] for n in TWIN_WEIGHTS], *[new_v[n] for n in TWIN_WEIGHTS])
```

```python
import functools

import jax
import jax.numpy as jnp
from jax import lax
from jax.experimental import pallas as pl
from jax.experimental.pallas import tpu as pltpu

F32 = jnp.float32
BF16 = jnp.bfloat16
MESH = pl.DeviceIdType.MESH

D = 1024
FF = 2816
FH = FF // 2
NDEV = 8
KA = 3
KB = 31
HALO_A = 8
HALO_B = 32
RMS_EPS = 1e-6
LN_EPS = 1e-5
LR, B1, B2, ADAM_EPS, WD, STEP = 0.001, 0.9, 0.999, 1e-08, 0.01, 10

TM = 256
TM_DW = 512
VMEM_LIMIT = 56 * 1024 * 1024

W_ROWS = (("a_in", 384), ("a_out", 128), ("b_pw1", 256), ("b_pw2", 128),
          ("g0", 352), ("u0", 352), ("d0", 352), ("g1", 352), ("u1", 352), ("d1", 352))
W_N = dict(W_ROWS)
W_OFF = {}
_o = 0
for _k, _n in W_ROWS:
    W_OFF[_k] = _o
    _o += _n
N_WROWS = _o
SM_ROWS = 16
RP_ROWS = 16
PK_ROWS = N_WROWS + SM_ROWS
LD_ROWS = N_WROWS + SM_ROWS + RP_ROWS

SM_B_NORM, SM_B_PW1, SM_B_CONV, SM_B_BCONV, SM_LN_G, SM_LN_B, SM_B_PW2, SM_A_CONV = 0, 1, 3, 34, 35, 36, 37, 38
SM_USED = 41
SM_F32_ROWS = 64


def _pcall(body, **kw):
    return pl.pallas_call(body, **kw)


def _cparams(sem=None):
    return pltpu.CompilerParams(dimension_semantics=sem, vmem_limit_bytes=VMEM_LIMIT)


def _dot(a, b):
    return jnp.dot(a, b, preferred_element_type=F32)


def _dot_nt(a, b):
    return lax.dot_general(a, b, (((1,), (1,)), ((), ())), preferred_element_type=F32)


def _dot_tn(a, b):
    return lax.dot_general(a, b, (((0,), (0,)), ((), ())), preferred_element_type=F32)


def _sigmoid(v):
    return 1.0 / (1.0 + jnp.exp(-v))


def _rms_stat(x):
    return lax.rsqrt(jnp.mean(x * x, axis=-1, keepdims=True) + RMS_EPS)


def _rms_bwd(dn, x, r, gamma):
    dng = dn * gamma
    dx = r * dng - x * (r * r * r) * jnp.mean(dng * x, axis=-1, keepdims=True)
    return dx, jnp.sum(dn * x * r, axis=0, keepdims=True)


def _load_weights(wg_ref, plan, sems):
    copies = []
    for j, (dst, off, n) in enumerate(plan):
        for d in range(NDEV):
            copies.append(pltpu.make_async_copy(
                wg_ref.at[d, pl.ds(off, n), :], dst.at[pl.ds(d * n, n), :], sems.at[j * NDEV + d]))
    for cp in copies:
        cp.start()
    for cp in copies:
        cp.wait()


_ANY = pl.BlockSpec(memory_space=pl.ANY)


def _row_spec(tm, width, rev_nt=None):
    if rev_nt is None:
        return pl.BlockSpec((tm, width), lambda i: (i, 0))
    return pl.BlockSpec((tm, width), lambda i: (rev_nt - 1 - i, 0))


def _full_spec(shape):
    return pl.BlockSpec(shape, lambda *_: (0,) * len(shape))


def _sds(shape, dtype):
    return jax.ShapeDtypeStruct(shape, dtype)


def _pack_shard(parts, small_bf16):
    def body(*refs):
        out = refs[-1]
        for (name, n), ref in zip(W_ROWS, refs[:-2]):
            out[pl.ds(W_OFF[name], n), :] = ref[...].astype(BF16)
        out[pl.ds(N_WROWS, SM_ROWS), :] = refs[-2][...]

    return _pcall(
        body, name="pack_shard",
        out_shape=_sds((PK_ROWS, D), BF16),
        in_specs=[pl.BlockSpec(memory_space=pltpu.VMEM)] * (len(parts) + 1),
        out_specs=pl.BlockSpec(memory_space=pltpu.VMEM),
        compiler_params=_cparams(),
    )(*parts, small_bf16)


def _all_gather(shard):
    def body(x_ref, out_ref, send_sems, recv_sems, local_sem):
        x, y, c = lax.axis_index("x"), lax.axis_index("y"), lax.axis_index("c")
        me, sibling = (x, y, c), (x, y, 1 - c)
        chips = [(1 - x, y), (x, 1 - y), (1 - x, 1 - y)]

        def slot(px, py, pc):
            return out_ref.at[4 * px + 2 * py + pc]

        def copy(k, block, to, src=None):
            return pltpu.make_async_remote_copy(
                src_ref=slot(*block) if src is None else src, dst_ref=slot(*block),
                send_sem=send_sems.at[k], recv_sem=recv_sems.at[k],
                device_id=to, device_id_type=MESH)

        mine = pltpu.make_async_copy(x_ref, slot(*me), local_sem)
        mine.start()
        first = [copy(0, me, sibling, src=x_ref)]
        first += [copy(1 + j, me, (*chip, c), src=x_ref) for j, chip in enumerate(chips)]
        for cp in first:
            cp.start()
        passed = [copy(4 + j, (*chip, c), sibling) for j, chip in enumerate(chips)]
        for j, chip in enumerate(chips):
            copy(1 + j, (*chip, c), me).wait_recv()
            passed[j].start()
        copy(0, sibling, me).wait_recv()
        for j, chip in enumerate(chips):
            copy(4 + j, (*chip, 1 - c), me).wait_recv()
        for cp in first + passed:
            cp.wait_send()
        mine.wait()

    return _pcall(
        body, name="all_gather_weights",
        out_shape=_sds((NDEV, PK_ROWS, D), BF16),
        in_specs=[_ANY], out_specs=_ANY,
        scratch_shapes=[pltpu.SemaphoreType.DMA((7,)), pltpu.SemaphoreType.DMA((7,)),
                        pltpu.SemaphoreType.DMA],
    )(shard)


def _exchange_grads(grads, small, repl):
    def body(*refs):
        g_refs = refs[:len(W_ROWS)]
        sm_ref, rp_ref, land, send_sems, recv_sems = refs[len(W_ROWS):]
        x, y, c = lax.axis_index("x"), lax.axis_index("y"), lax.axis_index("c")
        my_lin = 4 * x + 2 * y + c
        peers = []
        for j in range(NDEV):
            to = ((x + ((j >> 2) & 1)) % 2, (y + ((j >> 1) & 1)) % 2, (c + (j & 1)) % 2)
            t_lin = 4 * to[0] + 2 * to[1] + to[2]
            peers.append(to)

            def send(src, dst, j=j, to=to):
                if j == 0:
                    pltpu.make_async_copy(src, dst, send_sems.at[0]).start()
                else:
                    pltpu.make_async_remote_copy(
                        src_ref=src, dst_ref=dst, send_sem=send_sems.at[j], recv_sem=recv_sems.at[j],
                        device_id=to, device_id_type=MESH).start()

            for (name, n), g in zip(W_ROWS, g_refs):
                send(g.at[t_lin], land.at[my_lin, pl.ds(W_OFF[name], n), :])
            send(sm_ref.at[t_lin], land.at[my_lin, pl.ds(N_WROWS, SM_ROWS), :])
            send(rp_ref, land.at[my_lin, pl.ds(N_WROWS + SM_ROWS, RP_ROWS), :])
        pltpu.make_async_copy(land.at[0], land.at[0], send_sems.at[0]).wait()
        for j in range(1, NDEV):
            pltpu.make_async_remote_copy(
                src_ref=land.at[0], dst_ref=land.at[0], send_sem=send_sems.at[j], recv_sem=recv_sems.at[j],
                device_id=peers[j], device_id_type=MESH).wait()

    return _pcall(
        body, name="exchange_grads",
        out_shape=_sds((NDEV, LD_ROWS, D), BF16),
        in_specs=[_ANY] * (len(grads) + 2), out_specs=_ANY,
        scratch_shapes=[pltpu.SemaphoreType.DMA((NDEV,)), pltpu.SemaphoreType.DMA((NDEV,))],
    )(*grads, small, repl)


def _mixa_fwd(x, gam, cw, wg, tm):
    T = x.shape[0]

    def body(x_ref, gam_ref, cw_ref, wg_ref, h_ref, n_ref, b_ref, c_ref, v_ref, cc_ref, y_ref,
             win, wout, buf, sems):
        @pl.when(pl.program_id(0) == 0)
        def _():
            _load_weights(wg_ref, [(win, W_OFF["a_in"], 384), (wout, W_OFF["a_out"], 128)], sems)
            buf[pl.ds(0, HALO_A), :] = jnp.zeros((HALO_A, D), F32)

        xv = x_ref[...]
        nb = (xv * _rms_stat(xv) * gam_ref[...]).astype(BF16)
        n_ref[...] = nb
        bv = _dot_nt(nb, win[pl.ds(0, D), :])
        cval = _dot_nt(nb, win[pl.ds(D, D), :])
        vval = _dot_nt(nb, win[pl.ds(2 * D, D), :])
        cv = cval * vval
        buf[pl.ds(HALO_A, tm), :] = cv
        cc = cw_ref[pl.ds(KA - 1, 1), :] * cv
        for k in range(KA - 1):
            cc = cc + cw_ref[pl.ds(k, 1), :] * buf[pl.ds(HALO_A - (KA - 1) + k, tm), :]
        buf[pl.ds(0, HALO_A), :] = buf[pl.ds(tm, HALO_A), :]
        yb = (bv * cc).astype(BF16)
        b_ref[...] = bv.astype(BF16)
        c_ref[...] = cval.astype(BF16)
        v_ref[...] = vval.astype(BF16)
        cc_ref[...] = cc.astype(BF16)
        y_ref[...] = yb
        h_ref[...] = xv + _dot(yb, wout[...])

    act = _sds((T, D), BF16)
    return _pcall(
        body, name="mixa_fwd", grid=(T // tm,),
        out_shape=(_sds((T, D), F32),) + (act,) * 6,
        in_specs=[_row_spec(tm, D), _full_spec((1, D)), _full_spec((KA, D)), _ANY],
        out_specs=tuple(_row_spec(tm, D) for _ in range(7)),
        scratch_shapes=[pltpu.VMEM((3 * D, D), BF16), pltpu.VMEM((D, D), BF16),
                        pltpu.VMEM((HALO_A + tm, D), F32), pltpu.SemaphoreType.DMA((2 * NDEV,))],
        compiler_params=_cparams(("arbitrary",)),
    )(x, gam, cw, wg)


def _ffn_fwd(h, gam, wg, layer, tm):
    T = h.shape[0]
    kg, ku, kd = "g%d" % layer, "u%d" % layer, "d%d" % layer

    def body(h_ref, gam_ref, wg_ref, o_ref, n_ref, g_ref, u_ref, gu_ref, wgt, wut, wd, sems):
        @pl.when(pl.program_id(0) == 0)
        def _():
            _load_weights(wg_ref, [(wgt, W_OFF[kg], 352), (wut, W_OFF[ku], 352), (wd, W_OFF[kd], 352)], sems)

        hv = h_ref[...]
        nb = (hv * _rms_stat(hv) * gam_ref[...]).astype(BF16)
        n_ref[...] = nb
        out = hv
        for f in range(2):
            cols = pl.ds(f * FH, FH)
            g = _dot_nt(nb, wgt[cols, :])
            u = _dot_nt(nb, wut[cols, :])
            gu = (g * _sigmoid(g) * u).astype(BF16)
            g_ref[:, cols] = g.astype(BF16)
            u_ref[:, cols] = u.astype(BF16)
            gu_ref[:, cols] = gu
            out = out + _dot(gu, wd[cols, :])
        o_ref[...] = out

    pre = _sds((T, FF), BF16)
    return _pcall(
        body, name="ffn%d_fwd" % layer, grid=(T // tm,),
        out_shape=(_sds((T, D), F32), _sds((T, D), BF16), pre, pre, pre),
        in_specs=[_row_spec(tm, D), _full_spec((1, D)), _ANY],
        out_specs=(_row_spec(tm, D), _row_spec(tm, D), _row_spec(tm, FF), _row_spec(tm, FF), _row_spec(tm, FF)),
        scratch_shapes=[pltpu.VMEM((FF, D), BF16)] * 3 + [pltpu.SemaphoreType.DMA((3 * NDEV,))],
        compiler_params=_cparams(("arbitrary",)),
    )(h, gam, wg)


def _ln_stats(dc):
    mu = jnp.mean(dc, axis=-1, keepdims=True)
    xc = dc - mu
    rstd = lax.rsqrt(jnp.mean(xc * xc, axis=-1, keepdims=True) + LN_EPS)
    return xc * rstd, rstd


def _mixb_fwd(h, vecs, bias1, wg, tm):
    T = h.shape[0]

    def body(h_ref, vec_ref, b1_ref, wg_ref, o_ref, n_ref, a_ref, g_ref, dc_ref, s_ref, w1, w2, buf, sems):
        @pl.when(pl.program_id(0) == 0)
        def _():
            _load_weights(wg_ref, [(w1, W_OFF["b_pw1"], 256), (w2, W_OFF["b_pw2"], 128)], sems)
            buf[pl.ds(0, HALO_B), :] = jnp.zeros((HALO_B, D), F32)

        hv = h_ref[...]
        nb = (hv * _rms_stat(hv) * vec_ref[pl.ds(SM_B_NORM, 1), :]).astype(BF16)
        n_ref[...] = nb
        a = _dot_nt(nb, w1[pl.ds(0, D), :]) + b1_ref[:, pl.ds(0, D)]
        g = _dot_nt(nb, w1[pl.ds(D, D), :]) + b1_ref[:, pl.ds(D, D)]
        a_ref[...] = a.astype(BF16)
        g_ref[...] = g.astype(BF16)
        buf[pl.ds(HALO_B, tm), :] = a * _sigmoid(g)
        dc = jnp.zeros((tm, D), F32) + vec_ref[pl.ds(SM_B_BCONV, 1), :]
        for k in range(KB):
            dc = dc + vec_ref[pl.ds(SM_B_CONV + k, 1), :] * buf[pl.ds(HALO_B - (KB - 1) + k, tm), :]
        buf[pl.ds(0, HALO_B), :] = buf[pl.ds(tm, HALO_B), :]
        dc_ref[...] = dc
        xhat, _ = _ln_stats(dc)
        ln = xhat * vec_ref[pl.ds(SM_LN_G, 1), :] + vec_ref[pl.ds(SM_LN_B, 1), :]
        s = (ln * _sigmoid(ln)).astype(BF16)
        s_ref[...] = s
        o_ref[...] = hv + _dot(s, w2[...]) + vec_ref[pl.ds(SM_B_PW2, 1), :]

    act = _sds((T, D), BF16)
    return _pcall(
        body, name="mixb_fwd", grid=(T // tm,),
        out_shape=(_sds((T, D), F32), act, act, act, _sds((T, D), F32), act),
        in_specs=[_row_spec(tm, D), _full_spec((SM_F32_ROWS, D)), _full_spec((1, 2 * D)), _ANY],
        out_specs=tuple(_row_spec(tm, D) for _ in range(6)),
        scratch_shapes=[pltpu.VMEM((2 * D, D), BF16), pltpu.VMEM((D, D), BF16),
                        pltpu.VMEM((HALO_B + tm, D), F32), pltpu.SemaphoreType.DMA((2 * NDEV,))],
        compiler_params=_cparams(("arbitrary",)),
    )(h, vecs, bias1, wg)


def _loss_head(h, tgt, gam, tm):
    T = h.shape[0]

    def body(h_ref, t_ref, gam_ref, dh_ref, st_ref):
        @pl.when(pl.program_id(0) == 0)
        def _():
            st_ref[...] = jnp.zeros((8, D), F32)

        hv = h_ref[...]
        gamma = gam_ref[...]
        r = _rms_stat(hv)
        err = hv * r * gamma - t_ref[...]
        dx, dgam = _rms_bwd(err * (1.0 / D), hv, r, gamma)
        dh_ref[...] = dx
        st_ref[pl.ds(0, 1), :] += dgam
        st_ref[pl.ds(1, 1), :] += (0.5 / D) * jnp.sum(err * err, axis=0, keepdims=True)

    return _pcall(
        body, name="loss_head", grid=(T // tm,),
        out_shape=(_sds((T, D), F32), _sds((8, D), F32)),
        in_specs=[_row_spec(tm, D), _row_spec(tm, D), _full_spec((1, D))],
        out_specs=(_row_spec(tm, D), _full_spec((8, D))),
        compiler_params=_cparams(("arbitrary",)),
    )(h, tgt, gam)


def _ffn_bwd_dx(dh, h, g, u, gam, wg, layer, tm):
    T = h.shape[0]
    kg, ku, kd = "g%d" % layer, "u%d" % layer, "d%d" % layer

    def body(dh_ref, h_ref, g_ref, u_ref, gam_ref, wg_ref, o_ref, dg_ref, du_ref, st_ref, wgt, wut, wd, sems):
        @pl.when(pl.program_id(0) == 0)
        def _():
            _load_weights(wg_ref, [(wgt, W_OFF[kg], 352), (wut, W_OFF[ku], 352), (wd, W_OFF[kd], 352)], sems)
            st_ref[...] = jnp.zeros((8, D), F32)

        dhv = dh_ref[...]
        dhb = dhv.astype(BF16)
        dn = jnp.zeros_like(dhv)
        for f in range(2):
            cols = pl.ds(f * FH, FH)
            dgu = _dot_nt(dhb, wd[cols, :])
            gv = g_ref[:, cols].astype(F32)
            sg = _sigmoid(gv)
            du = (dgu * gv * sg).astype(BF16)
            dg = (dgu * u_ref[:, cols].astype(F32) * (sg * (1.0 + gv * (1.0 - sg)))).astype(BF16)
            dg_ref[:, cols] = dg
            du_ref[:, cols] = du
            dn = dn + _dot(dg, wgt[cols, :]) + _dot(du, wut[cols, :])
        hv = h_ref[...]
        dx, dgam = _rms_bwd(dn, hv, _rms_stat(hv), gam_ref[...])
        o_ref[...] = dhv + dx
        st_ref[pl.ds(0, 1), :] += dgam

    pre = _sds((T, FF), BF16)
    return _pcall(
        body, name="ffn%d_bwd_dx" % layer, grid=(T // tm,),
        out_shape=(_sds((T, D), F32), pre, pre, _sds((8, D), F32)),
        in_specs=[_row_spec(tm, D), _row_spec(tm, D), _row_spec(tm, FF), _row_spec(tm, FF),
                  _full_spec((1, D)), _ANY],
        out_specs=(_row_spec(tm, D), _row_spec(tm, FF), _row_spec(tm, FF), _full_spec((8, D))),
        scratch_shapes=[pltpu.VMEM((FF, D), BF16)] * 3 + [pltpu.SemaphoreType.DMA((3 * NDEV,))],
        compiler_params=_cparams(("arbitrary",)),
    )(dh, h, g, u, gam, wg)


def _grad_w(lhs, rhs, mc, name, tm):
    T, M = lhs.shape
    nt = T // tm

    def body(l_ref, r_ref, o_ref, acc):
        i = pl.program_id(1)

        @pl.when(i == 0)
        def _():
            acc[...] = jnp.zeros((mc, D), F32)

        acc[...] += _dot_tn(l_ref[...], r_ref[...].astype(BF16))

        @pl.when(i == nt - 1)
        def _():
            o_ref[...] = acc[...].astype(BF16)

    return _pcall(
        body, name=name, grid=(M // mc, nt),
        out_shape=_sds((M, D), BF16),
        in_specs=[pl.BlockSpec((tm, mc), lambda j, i: (i, j)), pl.BlockSpec((tm, D), lambda j, i: (i, 0))],
        out_specs=pl.BlockSpec((mc, D), lambda j, i: (j, 0)),
        scratch_shapes=[pltpu.VMEM((mc, D), F32)],
        compiler_params=_cparams(("arbitrary", "arbitrary")),
    )(lhs, rhs)


def _mixb_bwd(dh, h, a, g, dc, vecs, wg, tm):
    T = h.shape[0]
    nt = T // tm

    def body(dh_ref, h_ref, a_ref, g_ref, dc_ref, vec_ref, wg_ref, o_ref, du_ref, st_ref, sb_ref, w1, w2, buf, sems):
        @pl.when(pl.program_id(0) == 0)
        def _():
            _load_weights(wg_ref, [(w1, W_OFF["b_pw1"], 256), (w2, W_OFF["b_pw2"], 128)], sems)
            buf[pl.ds(tm, HALO_B), :] = jnp.zeros((HALO_B, D), F32)
            st_ref[...] = jnp.zeros((SM_F32_ROWS, D), F32)
            sb_ref[...] = jnp.zeros((8, 2 * D), F32)

        def acc(row, val):
            st_ref[pl.ds(row, 1), :] += jnp.sum(val, axis=0, keepdims=True)

        dhv = dh_ref[...]
        acc(SM_B_PW2, dhv)
        ds = _dot_nt(dhv.astype(BF16), w2[...])
        xhat, rstd = _ln_stats(dc_ref[...])
        ln_g = vec_ref[pl.ds(SM_LN_G, 1), :]
        ln = xhat * ln_g + vec_ref[pl.ds(SM_LN_B, 1), :]
        sl = _sigmoid(ln)
        dln = ds * (sl * (1.0 + ln * (1.0 - sl)))
        acc(SM_LN_G, dln * xhat)
        acc(SM_LN_B, dln)
        dxh = dln * ln_g
        ddc = rstd * (dxh - jnp.mean(dxh, axis=-1, keepdims=True)
                      - xhat * jnp.mean(dxh * xhat, axis=-1, keepdims=True))
        acc(SM_B_BCONV, ddc)
        buf[pl.ds(0, tm), :] = ddc
        av = a_ref[...].astype(F32)
        sg = _sigmoid(g_ref[...].astype(F32))
        glu = av * sg
        dglu = jnp.zeros((tm, D), F32)
        for j in range(KB):
            sh = buf[pl.ds(j, tm), :]
            k = KB - 1 - j
            dglu = dglu + vec_ref[pl.ds(SM_B_CONV + k, 1), :] * sh
            acc(SM_B_CONV + k, glu * sh)
        buf[pl.ds(tm, HALO_B), :] = buf[pl.ds(0, HALO_B), :]
        da = dglu * sg
        dg = dglu * av * sg * (1.0 - sg)
        sb_ref[pl.ds(0, 1), pl.ds(0, D)] += jnp.sum(da, axis=0, keepdims=True)
        sb_ref[pl.ds(0, 1), pl.ds(D, D)] += jnp.sum(dg, axis=0, keepdims=True)
        dab, dgb = da.astype(BF16), dg.astype(BF16)
        du_ref[:, pl.ds(0, D)] = dab
        du_ref[:, pl.ds(D, D)] = dgb
        dn = _dot(dab, w1[pl.ds(0, D), :]) + _dot(dgb, w1[pl.ds(D, D), :])
        hv = h_ref[...]
        dx, dgam = _rms_bwd(dn, hv, _rms_stat(hv), vec_ref[pl.ds(SM_B_NORM, 1), :])
        o_ref[...] = dhv + dx
        st_ref[pl.ds(SM_B_NORM, 1), :] += dgam

    rs = functools.partial(_row_spec, rev_nt=nt)
    return _pcall(
        body, name="mixb_bwd", grid=(nt,),
        out_shape=(_sds((T, D), F32), _sds((T, 2 * D), BF16), _sds((SM_F32_ROWS, D), F32), _sds((8, 2 * D), F32)),
        in_specs=[rs(tm, D), rs(tm, D), rs(tm, D), rs(tm, D), rs(tm, D), _full_spec((SM_F32_ROWS, D)), _ANY],
        out_specs=(rs(tm, D), rs(tm, 2 * D), _full_spec((SM_F32_ROWS, D)), _full_spec((8, 2 * D))),
        scratch_shapes=[pltpu.VMEM((2 * D, D), BF16), pltpu.VMEM((D, D), BF16),
                        pltpu.VMEM((tm + HALO_B, D), F32), pltpu.SemaphoreType.DMA((2 * NDEV,))],
        compiler_params=_cparams(("arbitrary",)),
    )(dh, h, a, g, dc, vecs, wg)


def _mixa_bwd(dh, x, b, c, v, cc, gam, cw, wg, tm):
    T = x.shape[0]
    nt = T // tm

    def body(dh_ref, x_ref, b_ref, c_ref, v_ref, cc_ref, gam_ref, cw_ref, wg_ref,
             o_ref, db_ref, st_ref, win, wout, buf, sems):
        @pl.when(pl.program_id(0) == 0)
        def _():
            _load_weights(wg_ref, [(win, W_OFF["a_in"], 384), (wout, W_OFF["a_out"], 128)], sems)
            buf[pl.ds(tm, HALO_A), :] = jnp.zeros((HALO_A, D), F32)
            st_ref[...] = jnp.zeros((8, D), F32)

        dhv = dh_ref[...]
        dy = _dot_nt(dhv.astype(BF16), wout[...])
        cval = c_ref[...].astype(F32)
        vval = v_ref[...].astype(F32)
        d_b = (dy * cc_ref[...].astype(F32)).astype(BF16)
        buf[pl.ds(0, tm), :] = dy * b_ref[...].astype(F32)
        cv = cval * vval
        dcv = jnp.zeros((tm, D), F32)
        for j in range(KA):
            sh = buf[pl.ds(j, tm), :]
            k = KA - 1 - j
            dcv = dcv + cw_ref[pl.ds(k, 1), :] * sh
            st_ref[pl.ds(1 + k, 1), :] += jnp.sum(cv * sh, axis=0, keepdims=True)
        buf[pl.ds(tm, HALO_A), :] = buf[pl.ds(0, HALO_A), :]
        d_c = (dcv * vval).astype(BF16)
        d_v = (dcv * cval).astype(BF16)
        db_ref[:, pl.ds(0, D)] = d_b
        db_ref[:, pl.ds(D, D)] = d_c
        db_ref[:, pl.ds(2 * D, D)] = d_v
        dn = _dot(d_b, win[pl.ds(0, D), :]) + _dot(d_c, win[pl.ds(D, D), :]) + _dot(d_v, win[pl.ds(2 * D, D), :])
        xv = x_ref[...]
        dx, dgam = _rms_bwd(dn, xv, _rms_stat(xv), gam_ref[...])
        o_ref[...] = dhv + dx
        st_ref[pl.ds(0, 1), :] += dgam

    rs = functools.partial(_row_spec, rev_nt=nt)
    return _pcall(
        body, name="mixa_bwd", grid=(nt,),
        out_shape=(_sds((T, D), F32), _sds((T, 3 * D), BF16), _sds((8, D), F32)),
        in_specs=[rs(tm, D) for _ in range(6)] + [_full_spec((1, D)), _full_spec((KA, D)), _ANY],
        out_specs=(rs(tm, D), rs(tm, 3 * D), _full_spec((8, D))),
        scratch_shapes=[pltpu.VMEM((3 * D, D), BF16), pltpu.VMEM((D, D), BF16),
                        pltpu.VMEM((tm + HALO_A, D), F32), pltpu.SemaphoreType.DMA((2 * NDEV,))],
        compiler_params=_cparams(("arbitrary",)),
    )(dh, x, b, c, v, cc, gam, cw, wg)


def _sum_weights(land):
    rows = N_WROWS // 4

    def body(l_ref, o_ref):
        acc = l_ref[0].astype(F32)
        for s in range(1, NDEV):
            acc = acc + l_ref[s].astype(F32)
        o_ref[...] = acc

    return _pcall(
        body, name="sum_weight_grads", grid=(4,),
        out_shape=_sds((N_WROWS, D), F32),
        in_specs=[pl.BlockSpec((NDEV, rows, D), lambda i: (0, i, 0))],
        out_specs=pl.BlockSpec((rows, D), lambda i: (i, 0)),
        compiler_params=_cparams(("arbitrary",)),
    )(land)


def _sum_small(sm, rp):
    def body(sm_ref, rp_ref, osm, orp, oloss):
        a = sm_ref[0]
        b = rp_ref[0]
        for s in range(1, NDEV):
            a = a + sm_ref[s]
            b = b + rp_ref[s]
        osm[...] = a
        orp[...] = b
        oloss[...] = jnp.zeros((8, 128), F32) + jnp.sum(b[4:5, :], axis=-1, keepdims=True)

    return _pcall(
        body, name="sum_small_grads",
        out_shape=(_sds((SM_F32_ROWS, 128), F32), _sds((8, D), F32), _sds((8, 128), F32)),
        in_specs=[pl.BlockSpec(memory_space=pltpu.VMEM)] * 2,
        out_specs=tuple(pl.BlockSpec(memory_space=pltpu.VMEM) for _ in range(3)),
        compiler_params=_cparams(),
    )(sm, rp)


def _adamw(w, g, m, v, name):
    rows, cols = w.shape
    tr = rows
    for cand in (256, 128, 64, 32, 16, 8):
        if rows % cand == 0:
            tr = cand
            break

    def body(w_ref, g_ref, m_ref, v_ref, od, om, ov):
        gv = g_ref[...]
        mn = B1 * m_ref[...] + (1.0 - B1) * gv
        vn = B2 * v_ref[...] + (1.0 - B2) * (gv * gv)
        m_hat = mn / (1.0 - B1 ** STEP)
        v_hat = vn / (1.0 - B2 ** STEP)
        od[...] = -LR * (m_hat / (jnp.sqrt(v_hat) + ADAM_EPS) + WD * w_ref[...])
        om[...] = mn
        ov[...] = vn

    spec = pl.BlockSpec((tr, cols), lambda i: (i, 0))
    shp = _sds((rows, cols), F32)
    return _pcall(
        body, name="adamw_" + name, grid=(rows // tr,),
        out_shape=(shp, shp, shp), in_specs=[spec] * 4, out_specs=(spec, spec, spec),
        compiler_params=_cparams(("arbitrary",)),
    )(w, g, m, v)


def _f32_as_bf16_rows(a):
    b = lax.bitcast_convert_type(a, BF16)
    return b.reshape(a.shape[:-1] + (2 * a.shape[-1],))


def _bf16_rows_as_f32(a):
    return lax.bitcast_convert_type(a.reshape(a.shape[:-1] + (a.shape[-1] // 2, 2)), F32)


def _pack_small(b_norm, b_b_pw1, b_conv, b_b_conv, b_ln_g, b_ln_b, b_b_pw2, a_conv, pad):
    rows = [b_norm.reshape(1, 128), b_b_pw1.reshape(2, 128), b_conv.reshape(KB, 128), b_b_conv.reshape(1, 128),
            b_ln_g.reshape(1, 128), b_ln_b.reshape(1, 128), b_b_pw2.reshape(1, 128), a_conv.reshape(KA, 128)]
    return jnp.concatenate(rows + [jnp.full((SM_F32_ROWS - SM_USED, 128), pad, F32)], axis=0)


def _unpack_small(p):
    return (p[SM_B_NORM:SM_B_NORM + 1], p[SM_B_PW1:SM_B_PW1 + 2].reshape(1, 256),
            p[SM_B_CONV:SM_B_CONV + KB].reshape(1, KB, 128), p[SM_B_BCONV:SM_B_BCONV + 1],
            p[SM_LN_G:SM_LN_G + 1], p[SM_LN_B:SM_LN_B + 1], p[SM_B_PW2:SM_B_PW2 + 1],
            p[SM_A_CONV:SM_A_CONV + KA].reshape(1, KA, 128))


def kernel(x, a_norm, a_w_in, a_conv, a_w_out, b_norm, b_w_pw1, b_b_pw1, b_conv, b_b_conv, b_ln_g, b_ln_b, b_w_pw2, b_b_pw2, ffn_norm, ffn_w_gate, ffn_w_up, ffn_w_down, final_norm, loss_target, m_a_norm, m_a_w_in, m_a_conv, m_a_w_out, m_b_norm, m_b_w_pw1, m_b_b_pw1, m_b_conv, m_b_b_conv, m_b_ln_g, m_b_ln_b, m_b_w_pw2, m_b_b_pw2, m_ffn_norm, m_ffn_w_gate, m_ffn_w_up, m_ffn_w_down, m_final_norm, v_a_norm, v_a_w_in, v_a_conv, v_a_w_out, v_b_norm, v_b_w_pw1, v_b_b_pw1, v_b_conv, v_b_b_conv, v_b_ln_g, v_b_ln_b, v_b_w_pw2, v_b_b_pw2, v_ffn_norm, v_ffn_w_gate, v_ffn_w_up, v_ffn_w_down, v_final_norm):
    T = x.shape[1]
    tm = min(TM, T)
    tw = min(TM_DW, T)
    xs = x.reshape(T, D)
    tgt = loss_target.reshape(T, D)

    parts = [a_w_in[0].T, a_w_out[0], b_w_pw1[0].T, b_w_pw2[0],
             ffn_w_gate[0].T, ffn_w_up[0].T, ffn_w_down[0],
             ffn_w_gate[1].T, ffn_w_up[1].T, ffn_w_down[1]]
    small = _pack_small(b_norm, b_b_pw1, b_conv, b_b_conv, b_ln_g, b_ln_b, b_b_pw2, a_conv, 0.0)
    shard = _pack_shard(parts, _f32_as_bf16_rows(small).reshape(SM_ROWS, D))
    wg = _all_gather(shard)
    sm_all = _bf16_rows_as_f32(wg[:, N_WROWS:, :]).reshape(NDEV, SM_F32_ROWS, 128)
    vecs = sm_all.transpose(1, 0, 2).reshape(SM_F32_ROWS, D)
    bias1 = sm_all[:, SM_B_PW1:SM_B_PW1 + 2, :].reshape(1, 2 * D)
    cw_a = vecs[SM_A_CONV:SM_A_CONV + KA]
    fn0, fn1 = ffn_norm[0:1], ffn_norm[1:2]
    fin = final_norm.reshape(1, D)

    h1, n0, bq, cq, vq, ccq, yq = _mixa_fwd(xs, a_norm, cw_a, wg, tm)
    h2, n1, g0, u0, gu0 = _ffn_fwd(h1, fn0, wg, 0, tm)
    h3, n2, aq, gq, dcq, sq = _mixb_fwd(h2, vecs, bias1, wg, tm)
    h4, n3, g1, u1, gu1 = _ffn_fwd(h3, fn1, wg, 1, tm)
    dh4, st_fin = _loss_head(h4, tgt, fin, tm)

    dh3, dg1, du1, st_f1 = _ffn_bwd_dx(dh4, h3, g1, u1, fn1, wg, 1, tm)
    gw_g1 = _grad_w(dg1, n3, FH, "grad_gate1", tw)
    gw_u1 = _grad_w(du1, n3, FH, "grad_up1", tw)
    gw_d1 = _grad_w(gu1, dh4, FH, "grad_down1", tw)
    dh2, dub, st_b, st_b1 = _mixb_bwd(dh3, h2, aq, gq, dcq, vecs, wg, tm)
    gw_pw1 = _grad_w(dub, n2, D, "grad_pw1", tw)
    gw_pw2 = _grad_w(sq, dh3, D, "grad_pw2", tw)
    dh1, dg0, du0, st_f0 = _ffn_bwd_dx(dh2, h1, g0, u0, fn0, wg, 0, tm)
    gw_g0 = _grad_w(dg0, n1, FH, "grad_gate0", tw)
    gw_u0 = _grad_w(du0, n1, FH, "grad_up0", tw)
    gw_d0 = _grad_w(gu0, dh2, FH, "grad_down0", tw)
    dx, dbcv, st_a = _mixa_bwd(dh1, xs, bq, cq, vq, ccq, a_norm, cw_a, wg, tm)
    gw_in = _grad_w(dbcv, n0, D, "grad_in", tw)
    gw_out = _grad_w(yq, dh1, D, "grad_out", tw)

    gws = {"a_in": gw_in, "a_out": gw_out, "b_pw1": gw_pw1, "b_pw2": gw_pw2,
           "g0": gw_g0, "u0": gw_u0, "d0": gw_d0, "g1": gw_g1, "u1": gw_u1, "d1": gw_d1}
    grads = [gws[k].reshape(NDEV, n, D) for k, n in W_ROWS]
    st_small = st_b.at[SM_A_CONV:SM_A_CONV + KA].set(st_a[1:1 + KA])
    sm_dest = st_small.reshape(SM_F32_ROWS, NDEV, 128).transpose(1, 0, 2)
    sm_dest = sm_dest.at[:, SM_B_PW1:SM_B_PW1 + 2, :].set(st_b1[0].reshape(NDEV, 2, 128))
    sm_send = _f32_as_bf16_rows(sm_dest).reshape(NDEV, SM_ROWS, D)
    repl = jnp.concatenate([st_a[0:1], st_f0[0:1], st_f1[0:1], st_fin[0:1], st_fin[1:2],
                            jnp.zeros((3, D), F32)], axis=0)
    rp_send = _f32_as_bf16_rows(repl).reshape(RP_ROWS, D)
    land = _exchange_grads(grads, sm_send, rp_send)

    gsum = _sum_weights(land)
    sm_land = _bf16_rows_as_f32(land[:, N_WROWS:N_WROWS + SM_ROWS, :]).reshape(NDEV, SM_F32_ROWS, 128)
    rp_land = _bf16_rows_as_f32(land[:, N_WROWS + SM_ROWS:, :]).reshape(NDEV, 8, D)
    g_small, g_repl, loss8 = _sum_small(sm_land, rp_land)
    loss = loss8[0, 0]

    def rows_of(k):
        return gsum[W_OFF[k]:W_OFF[k] + W_N[k]]

    g_a_w_in = rows_of("a_in").T[None]
    g_a_w_out = rows_of("a_out")[None]
    g_b_w_pw1 = rows_of("b_pw1").T[None]
    g_b_w_pw2 = rows_of("b_pw2")[None]
    g_gate = jnp.stack([rows_of("g0").T, rows_of("g1").T])
    g_up = jnp.stack([rows_of("u0").T, rows_of("u1").T])
    g_down = jnp.stack([rows_of("d0"), rows_of("d1")])
    (g_b_norm, g_b_b_pw1, g_b_conv, g_b_b_conv, g_b_ln_g, g_b_ln_b, g_b_b_pw2, g_a_conv) = _unpack_small(g_small)
    g_a_norm = g_repl[0:1]
    g_ffn_norm = g_repl[1:3]
    g_final = g_repl[3]

    def adam(name, w, g, m, v):
        d, nm, nv = _adamw(w.reshape(-1, w.shape[-1]), g.reshape(-1, w.shape[-1]),
                           m.reshape(-1, w.shape[-1]), v.reshape(-1, w.shape[-1]), name)
        return d.reshape(w.shape), nm.reshape(w.shape), nv.reshape(w.shape)

    big = [("a_w_in", a_w_in, g_a_w_in, m_a_w_in, v_a_w_in),
           ("a_w_out", a_w_out, g_a_w_out, m_a_w_out, v_a_w_out),
           ("b_w_pw1", b_w_pw1, g_b_w_pw1, m_b_w_pw1, v_b_w_pw1),
           ("b_w_pw2", b_w_pw2, g_b_w_pw2, m_b_w_pw2, v_b_w_pw2),
           ("ffn_w_gate", ffn_w_gate, g_gate, m_ffn_w_gate, v_ffn_w_gate),
           ("ffn_w_up", ffn_w_up, g_up, m_ffn_w_up, v_ffn_w_up),
           ("ffn_w_down", ffn_w_down, g_down, m_ffn_w_down, v_ffn_w_down)]
    res = {name: (g,) + adam(name, w, g, m, v) for name, w, g, m, v in big}

    sm_m = _pack_small(m_b_norm, m_b_b_pw1, m_b_conv, m_b_b_conv, m_b_ln_g, m_b_ln_b, m_b_b_pw2, m_a_conv, 0.0)
    sm_v = _pack_small(v_b_norm, v_b_b_pw1, v_b_conv, v_b_b_conv, v_b_ln_g, v_b_ln_b, v_b_b_pw2, v_a_conv, 1.0)
    sd, snm, snv = _adamw(small, g_small, sm_m, sm_v, "small")
    small_names = ["b_norm", "b_b_pw1", "b_conv", "b_b_conv", "b_ln_g", "b_ln_b", "b_b_pw2", "a_conv"]
    small_g = (g_b_norm, g_b_b_pw1, g_b_conv, g_b_b_conv, g_b_ln_g, g_b_ln_b, g_b_b_pw2, g_a_conv)
    for name, g, d, nm, nv in zip(small_names, small_g, _unpack_small(sd), _unpack_small(snm), _unpack_small(snv)):
        res[name] = (g, d, nm, nv)

    def rep_pack(an, fn, fi):
        return jnp.concatenate([an, fn, fi.reshape(1, D), jnp.ones((4, D), F32)], axis=0)

    rd, rnm, rnv = _adamw(rep_pack(a_norm, ffn_norm, final_norm), g_repl.at[4:].set(0.0),
                          rep_pack(m_a_norm, m_ffn_norm, m_final_norm),
                          rep_pack(v_a_norm, v_ffn_norm, v_final_norm), "replicated")
    res["a_norm"] = (g_a_norm, rd[0:1], rnm[0:1], rnv[0:1])
    res["ffn_norm"] = (g_ffn_norm, rd[1:3], rnm[1:3], rnv[1:3])
    res["final_norm"] = (g_final, rd[3], rnm[3], rnv[3])

    order = ["a_norm", "a_w_in", "a_conv", "a_w_out", "b_norm", "b_w_pw1", "b_b_pw1", "b_conv", "b_b_conv",
             "b_ln_g", "b_ln_b", "b_w_pw2", "b_b_pw2", "ffn_norm", "ffn_w_gate", "ffn_w_up", "ffn_w_down",
             "final_norm"]
    out = [loss, dx.reshape(1, T, D)]
    for j in range(4):
        out += [res[k][j] for k in order]
    return tuple(out)
```

```python
import functools

import jax
import jax.numpy as jnp
from jax import lax
from jax.experimental import pallas as pl
from jax.experimental.pallas import tpu as pltpu

F32 = jnp.float32
BF16 = jnp.bfloat16
MESH = pl.DeviceIdType.MESH

D = 1024
FF = 2816
FH = FF // 2
NDEV = 8
KA = 3
KB = 31
HALO_A = 8
HALO_B = 32
RMS_EPS = 1e-6
LN_EPS = 1e-5
LR, B1, B2, ADAM_EPS, WD, STEP = 0.001, 0.9, 0.999, 1e-08, 0.01, 10

TM = 256
TM_DW = 512
VMEM_LIMIT = 56 * 1024 * 1024

N_ROWS = {"a_in": 384, "a_out": 128, "b_pw1": 256, "b_pw2": 128, "g0": 352, "u0": 352, "d0": 352,
          "g1": 352, "u1": 352, "d1": 352, "small": 16, "repl": 16}
WEIGHT_KEYS = ("a_in", "a_out", "b_pw1", "b_pw2", "g0", "u0", "d0", "g1", "u1", "d1")


def _offsets(order):
    off, o = {}, 0
    for k in order:
        off[k] = o
        o += N_ROWS[k]
    return off, o


PK_ORDER = ("a_in", "a_out", "small", "g0", "u0", "d0", "b_pw1", "b_pw2", "g1", "u1", "d1")
PK_OFF, PK_ROWS = _offsets(PK_ORDER)
LD_ORDER = ("g1", "u1", "d1", "b_pw1", "b_pw2", "d0", "g0", "u0", "a_in", "a_out", "small", "repl")
LD_OFF, LD_ROWS = _offsets(LD_ORDER)
N_WROWS = LD_OFF["small"]


def _span(off, keys):
    return off[keys[0]], sum(N_ROWS[k] for k in keys)


SM_B_NORM, SM_B_PW1, SM_B_CONV, SM_B_BCONV, SM_LN_G, SM_LN_B, SM_B_PW2, SM_A_CONV = 0, 1, 3, 34, 35, 36, 37, 38
SM_USED = 41
SM_F32_ROWS = 64


def _pcall(body, **kw):
    return pl.pallas_call(body, **kw)


def _cparams(sem=None):
    return pltpu.CompilerParams(dimension_semantics=sem, vmem_limit_bytes=VMEM_LIMIT)


def _dot(a, b):
    return jnp.dot(a, b, preferred_element_type=F32)


def _dot_nt(a, b):
    return lax.dot_general(a, b, (((1,), (1,)), ((), ())), preferred_element_type=F32)


def _dot_tn(a, b):
    return lax.dot_general(a, b, (((0,), (0,)), ((), ())), preferred_element_type=F32)


def _sigmoid(v):
    return 1.0 / (1.0 + jnp.exp(-v))


def _rms_stat(x):
    return lax.rsqrt(jnp.mean(x * x, axis=-1, keepdims=True) + RMS_EPS)


def _rms_bwd(dn, x, r, gamma):
    dng = dn * gamma
    dx = r * dng - x * (r * r * r) * jnp.mean(dng * x, axis=-1, keepdims=True)
    return dx, jnp.sum(dn * x * r, axis=0, keepdims=True)


def _load_weights(wg_ref, plan, sems):
    copies = []
    for j, (dst, key) in enumerate(plan):
        off, n = PK_OFF[key], N_ROWS[key]
        for d in range(NDEV):
            copies.append(pltpu.make_async_copy(
                wg_ref.at[d, pl.ds(off, n), :], dst.at[pl.ds(d * n, n), :], sems.at[j * NDEV + d]))
    for cp in copies:
        cp.start()
    for cp in copies:
        cp.wait()


_ANY = pl.BlockSpec(memory_space=pl.ANY)


def _row_spec(tm, width, rev_nt=None):
    if rev_nt is None:
        return pl.BlockSpec((tm, width), lambda i: (i, 0))
    return pl.BlockSpec((tm, width), lambda i: (rev_nt - 1 - i, 0))


def _full_spec(shape):
    return pl.BlockSpec(shape, lambda *_: (0,) * len(shape))


def _sds(shape, dtype):
    return jax.ShapeDtypeStruct(shape, dtype)


def _mesh_pos():
    return lax.axis_index("x"), lax.axis_index("y"), lax.axis_index("c")


def _ag_exchange(shard_ref, wg_ref, send_sems, recv_sems, local_sem, r0, nr):
    x, y, c = _mesh_pos()
    me, sibling = (x, y, c), (x, y, 1 - c)
    chips = [(1 - x, y), (x, 1 - y), (1 - x, 1 - y)]
    src = shard_ref.at[pl.ds(r0, nr), :]

    def slot(p):
        return wg_ref.at[4 * p[0] + 2 * p[1] + p[2], pl.ds(r0, nr), :]

    def copy(k, block, to, own=False):
        return pltpu.make_async_remote_copy(
            src_ref=src if own else slot(block), dst_ref=slot(block),
            send_sem=send_sems.at[k], recv_sem=recv_sems.at[k], device_id=to, device_id_type=MESH)

    mine = pltpu.make_async_copy(src, slot(me), local_sem)
    first = [copy(0, me, sibling, own=True)]
    first += [copy(1 + j, me, (*chip, c), own=True) for j, chip in enumerate(chips)]
    passed = [copy(4 + j, (*chip, c), sibling) for j, chip in enumerate(chips)]

    def start():
        mine.start()
        for cp in first:
            cp.start()

    def finish():
        for j, chip in enumerate(chips):
            copy(1 + j, (*chip, c), me).wait_recv()
            passed[j].start()
        copy(0, sibling, me).wait_recv()
        for j, chip in enumerate(chips):
            copy(4 + j, (*chip, 1 - c), me).wait_recv()
        for cp in first + passed:
            cp.wait_send()
        mine.wait()

    return start, finish


def _rs_exchange(pieces, land_ref, send_sems, recv_sems, keys):
    x, y, c = _mesh_pos()
    my_lin = 4 * x + 2 * y + c
    l0, total = _span(LD_OFF, keys)
    peers = [((x + ((j >> 2) & 1)) % 2, (y + ((j >> 1) & 1)) % 2, (c + (j & 1)) % 2) for j in range(NDEV)]

    def start():
        for j, to in enumerate(peers):
            t_lin = 4 * to[0] + 2 * to[1] + to[2]
            for ref, key in zip(pieces, keys):
                src = ref if key == "repl" else ref.at[t_lin]
                dst = land_ref.at[my_lin, pl.ds(LD_OFF[key], N_ROWS[key]), :]
                if j == 0:
                    pltpu.make_async_copy(src, dst, send_sems.at[0]).start()
                else:
                    pltpu.make_async_remote_copy(
                        src_ref=src, dst_ref=dst, send_sem=send_sems.at[j], recv_sem=recv_sems.at[j],
                        device_id=to, device_id_type=MESH).start()

    def finish():
        whole = land_ref.at[0, pl.ds(l0, total), :]
        pltpu.make_async_copy(whole, whole, send_sems.at[0]).wait()
        for j in range(1, NDEV):
            pltpu.make_async_remote_copy(
                src_ref=whole, dst_ref=whole, send_sem=send_sems.at[j], recv_sem=recv_sems.at[j],
                device_id=peers[j], device_id_type=MESH).wait()

    return start, finish


class _Comm:
    def __init__(self, ins, alias_in, out_shape, scratch, make, gives_wg):
        self.ins, self.alias_in, self.out_shape = ins, alias_in, out_shape
        self.scratch, self.make, self.gives_wg = scratch, make, gives_wg


def _ag_comm(shard, wg, keys):
    r0, nr = _span(PK_OFF, keys)

    def make(c_ins, c_out, sc):
        return _ag_exchange(c_ins[0], c_out, sc[0], sc[1], sc[2], r0, nr)

    return _Comm([shard, wg], 1, _sds(wg.shape, BF16),
                 [pltpu.SemaphoreType.DMA((7,)), pltpu.SemaphoreType.DMA((7,)), pltpu.SemaphoreType.DMA],
                 make, True)


def _rs_comm(arrays, keys, land):
    def make(c_ins, c_out, sc):
        return _rs_exchange(c_ins[:len(keys)], c_out, sc[0], sc[1], keys)

    ins = list(arrays) + ([] if land is None else [land])
    return _Comm(ins, None if land is None else len(arrays), _sds((NDEV, LD_ROWS, D), BF16),
                 [pltpu.SemaphoreType.DMA((NDEV,)), pltpu.SemaphoreType.DMA((NDEV,))], make, False)


def _hosted_call(body, name, nt, in_specs, out_specs, out_shape, scratch, args, comm):
    if comm is None:
        return _pcall(body, name=name, grid=(nt,), in_specs=in_specs, out_specs=tuple(out_specs),
                      out_shape=tuple(out_shape), scratch_shapes=scratch,
                      compiler_params=_cparams(("arbitrary",)))(*args)
    n_in, n_out, n_sc, n_cin = len(in_specs), len(out_specs), len(scratch), len(comm.ins)

    def wrapped(*refs):
        ins = refs[:n_in]
        c_ins = refs[n_in:n_in + n_cin]
        outs = refs[n_in + n_cin:n_in + n_cin + n_out]
        c_out = refs[n_in + n_cin + n_out]
        sc = refs[n_in + n_cin + n_out + 1:n_in + n_cin + n_out + 1 + n_sc]
        c_sc = refs[n_in + n_cin + n_out + 1 + n_sc:]
        start, finish = comm.make(c_ins, c_out, c_sc)

        @pl.when(pl.program_id(0) == 0)
        def _():
            start()

        if comm.gives_wg:
            body(*ins, c_out, *outs, *sc)
        else:
            body(*ins, *outs, *sc)

        @pl.when(pl.program_id(0) == nt - 1)
        def _():
            finish()

    aliases = {} if comm.alias_in is None else {n_in + comm.alias_in: n_out}
    res = _pcall(wrapped, name=name, grid=(nt,),
                 in_specs=list(in_specs) + [_ANY] * n_cin, out_specs=tuple(out_specs) + (_ANY,),
                 out_shape=tuple(out_shape) + (comm.out_shape,),
                 scratch_shapes=list(scratch) + list(comm.scratch),
                 input_output_aliases=aliases,
                 compiler_params=_cparams(("arbitrary",)))(*args, *comm.ins)
    return res


def _pack_shard(parts, small_bf16):
    def body(*refs):
        out = refs[-1]
        for key, ref in zip(WEIGHT_KEYS, refs[:-2]):
            out[pl.ds(PK_OFF[key], N_ROWS[key]), :] = ref[...].astype(BF16)
        out[pl.ds(PK_OFF["small"], N_ROWS["small"]), :] = refs[-2][...]

    return _pcall(
        body, name="pack_shard",
        out_shape=_sds((PK_ROWS, D), BF16),
        in_specs=[pl.BlockSpec(memory_space=pltpu.VMEM)] * (len(parts) + 1),
        out_specs=pl.BlockSpec(memory_space=pltpu.VMEM),
        compiler_params=_cparams(),
    )(*parts, small_bf16)


def _all_gather_first(shard):
    keys = ("a_in", "a_out", "small")
    r0, nr = _span(PK_OFF, keys)

    def body(x_ref, wg_ref, sm_ref, send_sems, recv_sems, local_sem, sm_sem):
        start, finish = _ag_exchange(x_ref, wg_ref, send_sems, recv_sems, local_sem, r0, nr)
        start()
        finish()
        cp = pltpu.make_async_copy(wg_ref.at[:, pl.ds(PK_OFF["small"], N_ROWS["small"]), :], sm_ref, sm_sem)
        cp.start()
        cp.wait()

    return _pcall(
        body, name="all_gather_first",
        out_shape=(_sds((NDEV, PK_ROWS, D), BF16), _sds((NDEV, N_ROWS["small"], D), BF16)),
        in_specs=[_ANY], out_specs=(_ANY, _ANY),
        scratch_shapes=[pltpu.SemaphoreType.DMA((7,)), pltpu.SemaphoreType.DMA((7,)),
                        pltpu.SemaphoreType.DMA, pltpu.SemaphoreType.DMA],
    )(shard)


def _exchange_last(arrays, keys, land):
    def body(*refs):
        land_ref, send_sems, recv_sems = refs[len(arrays) + 1:]
        start, finish = _rs_exchange(refs[:len(arrays)], land_ref, send_sems, recv_sems, keys)
        start()
        finish()

    return _pcall(
        body, name="exchange_last",
        out_shape=_sds((NDEV, LD_ROWS, D), BF16),
        in_specs=[_ANY] * (len(arrays) + 1), out_specs=_ANY,
        input_output_aliases={len(arrays): 0},
        scratch_shapes=[pltpu.SemaphoreType.DMA((NDEV,)), pltpu.SemaphoreType.DMA((NDEV,))],
    )(*arrays, land)


def _mixa_fwd(x, gam, cw, wg, tm, comm):
    T = x.shape[0]

    def body(x_ref, gam_ref, cw_ref, wg_ref, h_ref, n_ref, b_ref, c_ref, v_ref, cc_ref, y_ref,
             win, wout, buf, sems):
        @pl.when(pl.program_id(0) == 0)
        def _():
            _load_weights(wg_ref, [(win, "a_in"), (wout, "a_out")], sems)
            buf[pl.ds(0, HALO_A), :] = jnp.zeros((HALO_A, D), F32)

        xv = x_ref[...]
        nb = (xv * _rms_stat(xv) * gam_ref[...]).astype(BF16)
        n_ref[...] = nb
        bv = _dot_nt(nb, win[pl.ds(0, D), :])
        cval = _dot_nt(nb, win[pl.ds(D, D), :])
        vval = _dot_nt(nb, win[pl.ds(2 * D, D), :])
        cv = cval * vval
        buf[pl.ds(HALO_A, tm), :] = cv
        cc = cw_ref[pl.ds(KA - 1, 1), :] * cv
        for k in range(KA - 1):
            cc = cc + cw_ref[pl.ds(k, 1), :] * buf[pl.ds(HALO_A - (KA - 1) + k, tm), :]
        buf[pl.ds(0, HALO_A), :] = buf[pl.ds(tm, HALO_A), :]
        yb = (bv * cc).astype(BF16)
        b_ref[...] = bv.astype(BF16)
        c_ref[...] = cval.astype(BF16)
        v_ref[...] = vval.astype(BF16)
        cc_ref[...] = cc.astype(BF16)
        y_ref[...] = yb
        h_ref[...] = xv + _dot(yb, wout[...])

    act = _sds((T, D), BF16)
    return _hosted_call(
        body, "mixa_fwd", T // tm,
        in_specs=[_row_spec(tm, D), _full_spec((1, D)), _full_spec((KA, D))],
        out_specs=[_row_spec(tm, D) for _ in range(7)],
        out_shape=(_sds((T, D), F32),) + (act,) * 6,
        scratch=[pltpu.VMEM((3 * D, D), BF16), pltpu.VMEM((D, D), BF16),
                 pltpu.VMEM((HALO_A + tm, D), F32), pltpu.SemaphoreType.DMA((2 * NDEV,))],
        args=(x, gam, cw), comm=comm)


def _ffn_fwd(h, gam, wg, layer, tm, comm):
    T = h.shape[0]
    kg, ku, kd = "g%d" % layer, "u%d" % layer, "d%d" % layer

    def body(h_ref, gam_ref, wg_ref, o_ref, n_ref, g_ref, u_ref, gu_ref, wgt, wut, wd, sems):
        @pl.when(pl.program_id(0) == 0)
        def _():
            _load_weights(wg_ref, [(wgt, kg), (wut, ku), (wd, kd)], sems)

        hv = h_ref[...]
        nb = (hv * _rms_stat(hv) * gam_ref[...]).astype(BF16)
        n_ref[...] = nb
        out = hv
        for f in range(2):
            cols = pl.ds(f * FH, FH)
            g = _dot_nt(nb, wgt[cols, :])
            u = _dot_nt(nb, wut[cols, :])
            gu = (g * _sigmoid(g) * u).astype(BF16)
            g_ref[:, cols] = g.astype(BF16)
            u_ref[:, cols] = u.astype(BF16)
            gu_ref[:, cols] = gu
            out = out + _dot(gu, wd[cols, :])
        o_ref[...] = out

    pre = _sds((T, FF), BF16)
    in_specs = [_row_spec(tm, D), _full_spec((1, D))]
    args = (h, gam)
    if comm is None:
        in_specs, args = in_specs + [_ANY], args + (wg,)
    return _hosted_call(
        body, "ffn%d_fwd" % layer, T // tm,
        in_specs=in_specs,
        out_specs=[_row_spec(tm, D), _row_spec(tm, D), _row_spec(tm, FF), _row_spec(tm, FF), _row_spec(tm, FF)],
        out_shape=(_sds((T, D), F32), _sds((T, D), BF16), pre, pre, pre),
        scratch=[pltpu.VMEM((FF, D), BF16)] * 3 + [pltpu.SemaphoreType.DMA((3 * NDEV,))],
        args=args, comm=comm)


def _ln_stats(dc):
    mu = jnp.mean(dc, axis=-1, keepdims=True)
    xc = dc - mu
    rstd = lax.rsqrt(jnp.mean(xc * xc, axis=-1, keepdims=True) + LN_EPS)
    return xc * rstd, rstd


def _mixb_fwd(h, vecs, bias1, wg, tm, comm):
    T = h.shape[0]

    def body(h_ref, vec_ref, b1_ref, wg_ref, o_ref, n_ref, a_ref, g_ref, dc_ref, s_ref, w1, w2, buf, sems):
        @pl.when(pl.program_id(0) == 0)
        def _():
            _load_weights(wg_ref, [(w1, "b_pw1"), (w2, "b_pw2")], sems)
            buf[pl.ds(0, HALO_B), :] = jnp.zeros((HALO_B, D), F32)

        hv = h_ref[...]
        nb = (hv * _rms_stat(hv) * vec_ref[pl.ds(SM_B_NORM, 1), :]).astype(BF16)
        n_ref[...] = nb
        a = _dot_nt(nb, w1[pl.ds(0, D), :]) + b1_ref[:, pl.ds(0, D)]
        g = _dot_nt(nb, w1[pl.ds(D, D), :]) + b1_ref[:, pl.ds(D, D)]
        a_ref[...] = a.astype(BF16)
        g_ref[...] = g.astype(BF16)
        buf[pl.ds(HALO_B, tm), :] = a * _sigmoid(g)
        dc = jnp.zeros((tm, D), F32) + vec_ref[pl.ds(SM_B_BCONV, 1), :]
        for k in range(KB):
            dc = dc + vec_ref[pl.ds(SM_B_CONV + k, 1), :] * buf[pl.ds(HALO_B - (KB - 1) + k, tm), :]
        buf[pl.ds(0, HALO_B), :] = buf[pl.ds(tm, HALO_B), :]
        dc_ref[...] = dc
        xhat, _ = _ln_stats(dc)
        ln = xhat * vec_ref[pl.ds(SM_LN_G, 1), :] + vec_ref[pl.ds(SM_LN_B, 1), :]
        s = (ln * _sigmoid(ln)).astype(BF16)
        s_ref[...] = s
        o_ref[...] = hv + _dot(s, w2[...]) + vec_ref[pl.ds(SM_B_PW2, 1), :]

    act = _sds((T, D), BF16)
    return _hosted_call(
        body, "mixb_fwd", T // tm,
        in_specs=[_row_spec(tm, D), _full_spec((SM_F32_ROWS, D)), _full_spec((1, 2 * D))],
        out_specs=[_row_spec(tm, D) for _ in range(6)],
        out_shape=(_sds((T, D), F32), act, act, act, _sds((T, D), F32), act),
        scratch=[pltpu.VMEM((2 * D, D), BF16), pltpu.VMEM((D, D), BF16),
                 pltpu.VMEM((HALO_B + tm, D), F32), pltpu.SemaphoreType.DMA((2 * NDEV,))],
        args=(h, vecs, bias1), comm=comm)


def _loss_head(h, tgt, gam, tm):
    T = h.shape[0]

    def body(h_ref, t_ref, gam_ref, dh_ref, st_ref):
        @pl.when(pl.program_id(0) == 0)
        def _():
            st_ref[...] = jnp.zeros((8, D), F32)

        hv = h_ref[...]
        gamma = gam_ref[...]
        r = _rms_stat(hv)
        err = hv * r * gamma - t_ref[...]
        dx, dgam = _rms_bwd(err * (1.0 / D), hv, r, gamma)
        dh_ref[...] = dx
        st_ref[pl.ds(0, 1), :] += dgam
        st_ref[pl.ds(1, 1), :] += (0.5 / D) * jnp.sum(err * err, axis=0, keepdims=True)

    return _pcall(
        body, name="loss_head", grid=(T // tm,),
        out_shape=(_sds((T, D), F32), _sds((8, D), F32)),
        in_specs=[_row_spec(tm, D), _row_spec(tm, D), _full_spec((1, D))],
        out_specs=(_row_spec(tm, D), _full_spec((8, D))),
        compiler_params=_cparams(("arbitrary",)),
    )(h, tgt, gam)


def _ffn_bwd_dx(dh, h, g, u, gam, wg, layer, tm, comm):
    T = h.shape[0]
    kg, ku, kd = "g%d" % layer, "u%d" % layer, "d%d" % layer

    def body(dh_ref, h_ref, g_ref, u_ref, gam_ref, wg_ref, o_ref, dg_ref, du_ref, st_ref, wgt, wut, wd, sems):
        @pl.when(pl.program_id(0) == 0)
        def _():
            _load_weights(wg_ref, [(wgt, kg), (wut, ku), (wd, kd)], sems)
            st_ref[...] = jnp.zeros((8, D), F32)

        dhv = dh_ref[...]
        dhb = dhv.astype(BF16)
        dn = jnp.zeros_like(dhv)
        for f in range(2):
            cols = pl.ds(f * FH, FH)
            dgu = _dot_nt(dhb, wd[cols, :])
            gv = g_ref[:, cols].astype(F32)
            sg = _sigmoid(gv)
            du = (dgu * gv * sg).astype(BF16)
            dg = (dgu * u_ref[:, cols].astype(F32) * (sg * (1.0 + gv * (1.0 - sg)))).astype(BF16)
            dg_ref[:, cols] = dg
            du_ref[:, cols] = du
            dn = dn + _dot(dg, wgt[cols, :]) + _dot(du, wut[cols, :])
        hv = h_ref[...]
        dx, dgam = _rms_bwd(dn, hv, _rms_stat(hv), gam_ref[...])
        o_ref[...] = dhv + dx
        st_ref[pl.ds(0, 1), :] += dgam

    pre = _sds((T, FF), BF16)
    return _hosted_call(
        body, "ffn%d_bwd_dx" % layer, T // tm,
        in_specs=[_row_spec(tm, D), _row_spec(tm, D), _row_spec(tm, FF), _row_spec(tm, FF),
                  _full_spec((1, D)), _ANY],
        out_specs=[_row_spec(tm, D), _row_spec(tm, FF), _row_spec(tm, FF), _full_spec((8, D))],
        out_shape=(_sds((T, D), F32), pre, pre, _sds((8, D), F32)),
        scratch=[pltpu.VMEM((FF, D), BF16)] * 3 + [pltpu.SemaphoreType.DMA((3 * NDEV,))],
        args=(dh, h, g, u, gam, wg), comm=comm)


def _grad_w(lhs, rhs, mc, name, tm):
    T, M = lhs.shape
    nt = T // tm

    def body(l_ref, r_ref, o_ref, acc):
        i = pl.program_id(1)

        @pl.when(i == 0)
        def _():
            acc[...] = jnp.zeros((mc, D), F32)

        acc[...] += _dot_tn(l_ref[...], r_ref[...].astype(BF16))

        @pl.when(i == nt - 1)
        def _():
            o_ref[...] = acc[...].astype(BF16)

    return _pcall(
        body, name=name, grid=(M // mc, nt),
        out_shape=_sds((M, D), BF16),
        in_specs=[pl.BlockSpec((tm, mc), lambda j, i: (i, j)), pl.BlockSpec((tm, D), lambda j, i: (i, 0))],
        out_specs=pl.BlockSpec((mc, D), lambda j, i: (j, 0)),
        scratch_shapes=[pltpu.VMEM((mc, D), F32)],
        compiler_params=_cparams(("arbitrary", "arbitrary")),
    )(lhs, rhs)


def _mixb_bwd(dh, h, a, g, dc, vecs, wg, tm, comm):
    T = h.shape[0]
    nt = T // tm

    def body(dh_ref, h_ref, a_ref, g_ref, dc_ref, vec_ref, wg_ref, o_ref, du_ref, st_ref, sb_ref, w1, w2, buf, sems):
        @pl.when(pl.program_id(0) == 0)
        def _():
            _load_weights(wg_ref, [(w1, "b_pw1"), (w2, "b_pw2")], sems)
            buf[pl.ds(tm, HALO_B), :] = jnp.zeros((HALO_B, D), F32)
            st_ref[...] = jnp.zeros((SM_F32_ROWS, D), F32)
            sb_ref[...] = jnp.zeros((8, 2 * D), F32)

        def acc(row, val):
            st_ref[pl.ds(row, 1), :] += jnp.sum(val, axis=0, keepdims=True)

        dhv = dh_ref[...]
        acc(SM_B_PW2, dhv)
        ds = _dot_nt(dhv.astype(BF16), w2[...])
        xhat, rstd = _ln_stats(dc_ref[...])
        ln_g = vec_ref[pl.ds(SM_LN_G, 1), :]
        ln = xhat * ln_g + vec_ref[pl.ds(SM_LN_B, 1), :]
        sl = _sigmoid(ln)
        dln = ds * (sl * (1.0 + ln * (1.0 - sl)))
        acc(SM_LN_G, dln * xhat)
        acc(SM_LN_B, dln)
        dxh = dln * ln_g
        ddc = rstd * (dxh - jnp.mean(dxh, axis=-1, keepdims=True)
                      - xhat * jnp.mean(dxh * xhat, axis=-1, keepdims=True))
        acc(SM_B_BCONV, ddc)
        buf[pl.ds(0, tm), :] = ddc
        av = a_ref[...].astype(F32)
        sg = _sigmoid(g_ref[...].astype(F32))
        glu = av * sg
        dglu = jnp.zeros((tm, D), F32)
        for j in range(KB):
            sh = buf[pl.ds(j, tm), :]
            k = KB - 1 - j
            dglu = dglu + vec_ref[pl.ds(SM_B_CONV + k, 1), :] * sh
            acc(SM_B_CONV + k, glu * sh)
        buf[pl.ds(tm, HALO_B), :] = buf[pl.ds(0, HALO_B), :]
        da = dglu * sg
        dg = dglu * av * sg * (1.0 - sg)
        sb_ref[pl.ds(0, 1), pl.ds(0, D)] += jnp.sum(da, axis=0, keepdims=True)
        sb_ref[pl.ds(0, 1), pl.ds(D, D)] += jnp.sum(dg, axis=0, keepdims=True)
        dab, dgb = da.astype(BF16), dg.astype(BF16)
        du_ref[:, pl.ds(0, D)] = dab
        du_ref[:, pl.ds(D, D)] = dgb
        dn = _dot(dab, w1[pl.ds(0, D), :]) + _dot(dgb, w1[pl.ds(D, D), :])
        hv = h_ref[...]
        dx, dgam = _rms_bwd(dn, hv, _rms_stat(hv), vec_ref[pl.ds(SM_B_NORM, 1), :])
        o_ref[...] = dhv + dx
        st_ref[pl.ds(SM_B_NORM, 1), :] += dgam

    rs = functools.partial(_row_spec, rev_nt=nt)
    return _hosted_call(
        body, "mixb_bwd", nt,
        in_specs=[rs(tm, D), rs(tm, D), rs(tm, D), rs(tm, D), rs(tm, D), _full_spec((SM_F32_ROWS, D)), _ANY],
        out_specs=[rs(tm, D), rs(tm, 2 * D), _full_spec((SM_F32_ROWS, D)), _full_spec((8, 2 * D))],
        out_shape=(_sds((T, D), F32), _sds((T, 2 * D), BF16), _sds((SM_F32_ROWS, D), F32), _sds((8, 2 * D), F32)),
        scratch=[pltpu.VMEM((2 * D, D), BF16), pltpu.VMEM((D, D), BF16),
                 pltpu.VMEM((tm + HALO_B, D), F32), pltpu.SemaphoreType.DMA((2 * NDEV,))],
        args=(dh, h, a, g, dc, vecs, wg), comm=comm)


def _mixa_bwd(dh, x, b, c, v, cc, gam, cw, wg, tm, comm):
    T = x.shape[0]
    nt = T // tm

    def body(dh_ref, x_ref, b_ref, c_ref, v_ref, cc_ref, gam_ref, cw_ref, wg_ref,
             o_ref, db_ref, st_ref, win, wout, buf, sems):
        @pl.when(pl.program_id(0) == 0)
        def _():
            _load_weights(wg_ref, [(win, "a_in"), (wout, "a_out")], sems)
            buf[pl.ds(tm, HALO_A), :] = jnp.zeros((HALO_A, D), F32)
            st_ref[...] = jnp.zeros((8, D), F32)

        dhv = dh_ref[...]
        dy = _dot_nt(dhv.astype(BF16), wout[...])
        cval = c_ref[...].astype(F32)
        vval = v_ref[...].astype(F32)
        d_b = (dy * cc_ref[...].astype(F32)).astype(BF16)
        buf[pl.ds(0, tm), :] = dy * b_ref[...].astype(F32)
        cv = cval * vval
        dcv = jnp.zeros((tm, D), F32)
        for j in range(KA):
            sh = buf[pl.ds(j, tm), :]
            k = KA - 1 - j
            dcv = dcv + cw_ref[pl.ds(k, 1), :] * sh
            st_ref[pl.ds(1 + k, 1), :] += jnp.sum(cv * sh, axis=0, keepdims=True)
        buf[pl.ds(tm, HALO_A), :] = buf[pl.ds(0, HALO_A), :]
        d_c = (dcv * vval).astype(BF16)
        d_v = (dcv * cval).astype(BF16)
        db_ref[:, pl.ds(0, D)] = d_b
        db_ref[:, pl.ds(D, D)] = d_c
        db_ref[:, pl.ds(2 * D, D)] = d_v
        dn = _dot(d_b, win[pl.ds(0, D), :]) + _dot(d_c, win[pl.ds(D, D), :]) + _dot(d_v, win[pl.ds(2 * D, D), :])
        xv = x_ref[...]
        dx, dgam = _rms_bwd(dn, xv, _rms_stat(xv), gam_ref[...])
        o_ref[...] = dhv + dx
        st_ref[pl.ds(0, 1), :] += dgam

    rs = functools.partial(_row_spec, rev_nt=nt)
    return _hosted_call(
        body, "mixa_bwd", nt,
        in_specs=[rs(tm, D) for _ in range(6)] + [_full_spec((1, D)), _full_spec((KA, D)), _ANY],
        out_specs=[rs(tm, D), rs(tm, 3 * D), _full_spec((8, D))],
        out_shape=(_sds((T, D), F32), _sds((T, 3 * D), BF16), _sds((8, D), F32)),
        scratch=[pltpu.VMEM((3 * D, D), BF16), pltpu.VMEM((D, D), BF16),
                 pltpu.VMEM((tm + HALO_A, D), F32), pltpu.SemaphoreType.DMA((2 * NDEV,))],
        args=(dh, x, b, c, v, cc, gam, cw, wg), comm=comm)


def _sum_weights(land):
    rows = N_WROWS // 4

    def body(l_ref, o_ref):
        acc = l_ref[0].astype(F32)
        for s in range(1, NDEV):
            acc = acc + l_ref[s].astype(F32)
        o_ref[...] = acc

    return _pcall(
        body, name="sum_weight_grads", grid=(4,),
        out_shape=_sds((N_WROWS, D), F32),
        in_specs=[pl.BlockSpec((NDEV, rows, D), lambda i: (0, i, 0))],
        out_specs=pl.BlockSpec((rows, D), lambda i: (i, 0)),
        compiler_params=_cparams(("arbitrary",)),
    )(land)


def _sum_small(sm, rp):
    def body(sm_ref, rp_ref, osm, orp, oloss):
        a = sm_ref[0]
        b = rp_ref[0]
        for s in range(1, NDEV):
            a = a + sm_ref[s]
            b = b + rp_ref[s]
        osm[...] = a
        orp[...] = b
        oloss[...] = jnp.zeros((8, 128), F32) + jnp.sum(b[4:5, :], axis=-1, keepdims=True)

    return _pcall(
        body, name="sum_small_grads",
        out_shape=(_sds((SM_F32_ROWS, 128), F32), _sds((8, D), F32), _sds((8, 128), F32)),
        in_specs=[pl.BlockSpec(memory_space=pltpu.VMEM)] * 2,
        out_specs=tuple(pl.BlockSpec(memory_space=pltpu.VMEM) for _ in range(3)),
        compiler_params=_cparams(),
    )(sm, rp)


def _adamw(w, g, m, v, name):
    rows, cols = w.shape
    tr = rows
    for cand in (256, 128, 64, 32, 16, 8):
        if rows % cand == 0:
            tr = cand
            break

    def body(w_ref, g_ref, m_ref, v_ref, od, om, ov):
        gv = g_ref[...]
        mn = B1 * m_ref[...] + (1.0 - B1) * gv
        vn = B2 * v_ref[...] + (1.0 - B2) * (gv * gv)
        m_hat = mn / (1.0 - B1 ** STEP)
        v_hat = vn / (1.0 - B2 ** STEP)
        od[...] = -LR * (m_hat / (jnp.sqrt(v_hat) + ADAM_EPS) + WD * w_ref[...])
        om[...] = mn
        ov[...] = vn

    spec = pl.BlockSpec((tr, cols), lambda i: (i, 0))
    shp = _sds((rows, cols), F32)
    return _pcall(
        body, name="adamw_" + name, grid=(rows // tr,),
        out_shape=(shp, shp, shp), in_specs=[spec] * 4, out_specs=(spec, spec, spec),
        compiler_params=_cparams(("arbitrary",)),
    )(w, g, m, v)


def _f32_as_bf16_rows(a):
    b = lax.bitcast_convert_type(a, BF16)
    return b.reshape(a.shape[:-1] + (2 * a.shape[-1],))


def _bf16_rows_as_f32(a):
    return lax.bitcast_convert_type(a.reshape(a.shape[:-1] + (a.shape[-1] // 2, 2)), F32)


def _pack_small(b_norm, b_b_pw1, b_conv, b_b_conv, b_ln_g, b_ln_b, b_b_pw2, a_conv, pad):
    rows = [b_norm.reshape(1, 128), b_b_pw1.reshape(2, 128), b_conv.reshape(KB, 128), b_b_conv.reshape(1, 128),
            b_ln_g.reshape(1, 128), b_ln_b.reshape(1, 128), b_b_pw2.reshape(1, 128), a_conv.reshape(KA, 128)]
    return jnp.concatenate(rows + [jnp.full((SM_F32_ROWS - SM_USED, 128), pad, F32)], axis=0)


def _unpack_small(p):
    return (p[SM_B_NORM:SM_B_NORM + 1], p[SM_B_PW1:SM_B_PW1 + 2].reshape(1, 256),
            p[SM_B_CONV:SM_B_CONV + KB].reshape(1, KB, 128), p[SM_B_BCONV:SM_B_BCONV + 1],
            p[SM_LN_G:SM_LN_G + 1], p[SM_LN_B:SM_LN_B + 1], p[SM_B_PW2:SM_B_PW2 + 1],
            p[SM_A_CONV:SM_A_CONV + KA].reshape(1, KA, 128))


def kernel(x, a_norm, a_w_in, a_conv, a_w_out, b_norm, b_w_pw1, b_b_pw1, b_conv, b_b_conv, b_ln_g, b_ln_b, b_w_pw2, b_b_pw2, ffn_norm, ffn_w_gate, ffn_w_up, ffn_w_down, final_norm, loss_target, m_a_norm, m_a_w_in, m_a_conv, m_a_w_out, m_b_norm, m_b_w_pw1, m_b_b_pw1, m_b_conv, m_b_b_conv, m_b_ln_g, m_b_ln_b, m_b_w_pw2, m_b_b_pw2, m_ffn_norm, m_ffn_w_gate, m_ffn_w_up, m_ffn_w_down, m_final_norm, v_a_norm, v_a_w_in, v_a_conv, v_a_w_out, v_b_norm, v_b_w_pw1, v_b_b_pw1, v_b_conv, v_b_b_conv, v_b_ln_g, v_b_ln_b, v_b_w_pw2, v_b_b_pw2, v_ffn_norm, v_ffn_w_gate, v_ffn_w_up, v_ffn_w_down, v_final_norm):
    T = x.shape[1]
    tm = min(TM, T)
    tw = min(TM_DW, T)
    xs = x.reshape(T, D)
    tgt = loss_target.reshape(T, D)

    parts = [a_w_in[0].T, a_w_out[0], b_w_pw1[0].T, b_w_pw2[0],
             ffn_w_gate[0].T, ffn_w_up[0].T, ffn_w_down[0],
             ffn_w_gate[1].T, ffn_w_up[1].T, ffn_w_down[1]]
    small = _pack_small(b_norm, b_b_pw1, b_conv, b_b_conv, b_ln_g, b_ln_b, b_b_pw2, a_conv, 0.0)
    shard = _pack_shard(parts, _f32_as_bf16_rows(small).reshape(N_ROWS["small"], D))
    wg, sm_rows = _all_gather_first(shard)
    sm_all = _bf16_rows_as_f32(sm_rows).reshape(NDEV, SM_F32_ROWS, 128)
    vecs = sm_all.transpose(1, 0, 2).reshape(SM_F32_ROWS, D)
    bias1 = sm_all[:, SM_B_PW1:SM_B_PW1 + 2, :].reshape(1, 2 * D)
    cw_a = vecs[SM_A_CONV:SM_A_CONV + KA]
    fn0, fn1 = ffn_norm[0:1], ffn_norm[1:2]
    fin = final_norm.reshape(1, D)

    h1, n0, bq, cq, vq, ccq, yq, wg = _mixa_fwd(xs, a_norm, cw_a, wg, tm, _ag_comm(shard, wg, ("g0", "u0", "d0")))
    h2, n1, g0, u0, gu0, wg = _ffn_fwd(h1, fn0, wg, 0, tm, _ag_comm(shard, wg, ("b_pw1", "b_pw2")))
    h3, n2, aq, gq, dcq, sq, wg = _mixb_fwd(h2, vecs, bias1, wg, tm, _ag_comm(shard, wg, ("g1", "u1", "d1")))
    h4, n3, g1, u1, gu1 = _ffn_fwd(h3, fn1, wg, 1, tm, None)
    dh4, st_fin = _loss_head(h4, tgt, fin, tm)

    def by_dest(gw, key):
        return gw.reshape(NDEV, N_ROWS[key], D)

    dh3, dg1, du1, st_f1 = _ffn_bwd_dx(dh4, h3, g1, u1, fn1, wg, 1, tm, None)
    gw_g1 = by_dest(_grad_w(dg1, n3, FH, "grad_gate1", tw), "g1")
    gw_u1 = by_dest(_grad_w(du1, n3, FH, "grad_up1", tw), "u1")
    gw_d1 = by_dest(_grad_w(gu1, dh4, FH, "grad_down1", tw), "d1")
    dh2, dub, st_b, st_b1, land = _mixb_bwd(
        dh3, h2, aq, gq, dcq, vecs, wg, tm, _rs_comm([gw_g1, gw_u1, gw_d1], ("g1", "u1", "d1"), None))
    gw_pw1 = by_dest(_grad_w(dub, n2, D, "grad_pw1", tw), "b_pw1")
    gw_pw2 = by_dest(_grad_w(sq, dh3, D, "grad_pw2", tw), "b_pw2")
    gw_d0 = by_dest(_grad_w(gu0, dh2, FH, "grad_down0", tw), "d0")
    dh1, dg0, du0, st_f0, land = _ffn_bwd_dx(
        dh2, h1, g0, u0, fn0, wg, 0, tm, _rs_comm([gw_pw1, gw_pw2, gw_d0], ("b_pw1", "b_pw2", "d0"), land))
    gw_g0 = by_dest(_grad_w(dg0, n1, FH, "grad_gate0", tw), "g0")
    gw_u0 = by_dest(_grad_w(du0, n1, FH, "grad_up0", tw), "u0")
    dx, dbcv, st_a, land = _mixa_bwd(
        dh1, xs, bq, cq, vq, ccq, a_norm, cw_a, wg, tm, _rs_comm([gw_g0, gw_u0], ("g0", "u0"), land))
    gw_in = by_dest(_grad_w(dbcv, n0, D, "grad_in", tw), "a_in")
    gw_out = by_dest(_grad_w(yq, dh1, D, "grad_out", tw), "a_out")

    st_small = st_b.at[SM_A_CONV:SM_A_CONV + KA].set(st_a[1:1 + KA])
    sm_dest = st_small.reshape(SM_F32_ROWS, NDEV, 128).transpose(1, 0, 2)
    sm_dest = sm_dest.at[:, SM_B_PW1:SM_B_PW1 + 2, :].set(st_b1[0].reshape(NDEV, 2, 128))
    sm_send = _f32_as_bf16_rows(sm_dest).reshape(NDEV, N_ROWS["small"], D)
    repl = jnp.concatenate([st_a[0:1], st_f0[0:1], st_f1[0:1], st_fin[0:1], st_fin[1:2],
                            jnp.zeros((3, D), F32)], axis=0)
    rp_send = _f32_as_bf16_rows(repl).reshape(N_ROWS["repl"], D)
    land = _exchange_last([gw_in, gw_out, sm_send, rp_send], ("a_in", "a_out", "small", "repl"), land)

    gsum = _sum_weights(land)
    sm_land = land[:, LD_OFF["small"]:LD_OFF["small"] + N_ROWS["small"], :]
    rp_land = land[:, LD_OFF["repl"]:LD_OFF["repl"] + N_ROWS["repl"], :]
    g_small, g_repl, loss8 = _sum_small(_bf16_rows_as_f32(sm_land).reshape(NDEV, SM_F32_ROWS, 128),
                                        _bf16_rows_as_f32(rp_land).reshape(NDEV, 8, D))
    loss = loss8[0, 0]

    def rows_of(k):
        return gsum[LD_OFF[k]:LD_OFF[k] + N_ROWS[k]]

    g_a_w_in = rows_of("a_in").T[None]
    g_a_w_out = rows_of("a_out")[None]
    g_b_w_pw1 = rows_of("b_pw1").T[None]
    g_b_w_pw2 = rows_of("b_pw2")[None]
    g_gate = jnp.stack([rows_of("g0").T, rows_of("g1").T])
    g_up = jnp.stack([rows_of("u0").T, rows_of("u1").T])
    g_down = jnp.stack([rows_of("d0"), rows_of("d1")])
    (g_b_norm, g_b_b_pw1, g_b_conv, g_b_b_conv, g_b_ln_g, g_b_ln_b, g_b_b_pw2, g_a_conv) = _unpack_small(g_small)
    g_a_norm = g_repl[0:1]
    g_ffn_norm = g_repl[1:3]
    g_final = g_repl[3]

    def adam(name, w, g, m, v):
        d, nm, nv = _adamw(w.reshape(-1, w.shape[-1]), g.reshape(-1, w.shape[-1]),
                           m.reshape(-1, w.shape[-1]), v.reshape(-1, w.shape[-1]), name)
        return d.reshape(w.shape), nm.reshape(w.shape), nv.reshape(w.shape)

    big = [("a_w_in", a_w_in, g_a_w_in, m_a_w_in, v_a_w_in),
           ("a_w_out", a_w_out, g_a_w_out, m_a_w_out, v_a_w_out),
           ("b_w_pw1", b_w_pw1, g_b_w_pw1, m_b_w_pw1, v_b_w_pw1),
           ("b_w_pw2", b_w_pw2, g_b_w_pw2, m_b_w_pw2, v_b_w_pw2),
           ("ffn_w_gate", ffn_w_gate, g_gate, m_ffn_w_gate, v_ffn_w_gate),
           ("ffn_w_up", ffn_w_up, g_up, m_ffn_w_up, v_ffn_w_up),
           ("ffn_w_down", ffn_w_down, g_down, m_ffn_w_down, v_ffn_w_down)]
    res = {name: (g,) + adam(name, w, g, m, v) for name, w, g, m, v in big}

    sm_m = _pack_small(m_b_norm, m_b_b_pw1, m_b_conv, m_b_b_conv, m_b_ln_g, m_b_ln_b, m_b_b_pw2, m_a_conv, 0.0)
    sm_v = _pack_small(v_b_norm, v_b_b_pw1, v_b_conv, v_b_b_conv, v_b_ln_g, v_b_ln_b, v_b_b_pw2, v_a_conv, 1.0)
    sd, snm, snv = _adamw(small, g_small, sm_m, sm_v, "small")
    small_names = ["b_norm", "b_b_pw1", "b_conv", "b_b_conv", "b_ln_g", "b_ln_b", "b_b_pw2", "a_conv"]
    small_g = (g_b_norm, g_b_b_pw1, g_b_conv, g_b_b_conv, g_b_ln_g, g_b_ln_b, g_b_b_pw2, g_a_conv)
    for name, g, d, nm, nv in zip(small_names, small_g, _unpack_small(sd), _unpack_small(snm), _unpack_small(snv)):
        res[name] = (g, d, nm, nv)

    def rep_pack(an, fn, fi):
        return jnp.concatenate([an, fn, fi.reshape(1, D), jnp.ones((4, D), F32)], axis=0)

    rd, rnm, rnv = _adamw(rep_pack(a_norm, ffn_norm, final_norm), g_repl.at[4:].set(0.0),
                          rep_pack(m_a_norm, m_ffn_norm, m_final_norm),
                          rep_pack(v_a_norm, v_ffn_norm, v_final_norm), "replicated")
    res["a_norm"] = (g_a_norm, rd[0:1], rnm[0:1], rnv[0:1])
    res["ffn_norm"] = (g_ffn_norm, rd[1:3], rnm[1:3], rnv[1:3])
    res["final_norm"] = (g_final, rd[3], rnm[3], rnv[3])

    order = ["a_norm", "a_w_in", "a_conv", "a_w_out", "b_norm", "b_w_pw1", "b_b_pw1", "b_conv", "b_b_conv",
             "b_ln_g", "b_ln_b", "b_w_pw2", "b_b_pw2", "ffn_norm", "ffn_w_gate", "ffn_w_up", "ffn_w_down",
             "final_norm"]
    out = [loss, dx.reshape(1, T, D)]
    for j in range(4):
        out += [res[k][j] for k in order]
    return tuple(out)
```

```python
import functools

import jax
import jax.numpy as jnp
from jax import lax
from jax.experimental import pallas as pl
from jax.experimental.pallas import tpu as pltpu

F32 = jnp.float32
BF16 = jnp.bfloat16
MESH = pl.DeviceIdType.MESH

D = 1024
FF = 2816
FH = FF // 2
NDEV = 8
KA = 3
KB = 31
HALO_A = 8
HALO_B = 32
CONV_ROWS = 64
RMS_EPS = 1e-6
LN_EPS = 1e-5
LR, B1, B2, ADAM_EPS, WD, STEP = 0.001, 0.9, 0.999, 1e-08, 0.01, 10

TM = 256
TM_DW = 512
VMEM_LIMIT = 56 * 1024 * 1024

N_ROWS = {"a_in": 384, "a_out": 128, "b_pw1": 256, "b_pw2": 128, "g0": 352, "u0": 352, "d0": 352,
          "g1": 352, "u1": 352, "d1": 352, "small": 16, "repl": 16}
WEIGHT_KEYS = ("a_in", "a_out", "b_pw1", "b_pw2", "g0", "u0", "d0", "g1", "u1", "d1")


def _offsets(order):
    off, o = {}, 0
    for k in order:
        off[k] = o
        o += N_ROWS[k]
    return off, o


PK_ORDER = ("a_in", "a_out", "small", "g0", "u0", "d0", "b_pw1", "b_pw2", "g1", "u1", "d1")
PK_OFF, PK_ROWS = _offsets(PK_ORDER)
LD_ORDER = ("g1", "u1", "d1", "b_pw1", "b_pw2", "d0", "g0", "u0", "a_in", "a_out", "small", "repl")
LD_OFF, LD_ROWS = _offsets(LD_ORDER)
N_WROWS = LD_OFF["small"]


def _span(off, keys):
    return off[keys[0]], sum(N_ROWS[k] for k in keys)


SM_B_NORM, SM_B_PW1, SM_B_CONV, SM_B_BCONV, SM_LN_G, SM_LN_B, SM_B_PW2, SM_A_CONV = 0, 1, 3, 34, 35, 36, 37, 38
SM_USED = 41
SM_F32_ROWS = 64


def _pcall(body, **kw):
    return pl.pallas_call(body, **kw)


def _cparams(sem=None):
    return pltpu.CompilerParams(dimension_semantics=sem, vmem_limit_bytes=VMEM_LIMIT)


def _dot(a, b):
    return jnp.dot(a, b, preferred_element_type=F32)


def _dot_nt(a, b):
    return lax.dot_general(a, b, (((1,), (1,)), ((), ())), preferred_element_type=F32)


def _dot_tn(a, b):
    return lax.dot_general(a, b, (((0,), (0,)), ((), ())), preferred_element_type=F32)


def _sigmoid(v):
    return 1.0 / (1.0 + jnp.exp(-v))


def _rms_stat(x):
    return lax.rsqrt(jnp.mean(x * x, axis=-1, keepdims=True) + RMS_EPS)


def _rms_bwd(dn, x, r, gamma):
    dng = dn * gamma
    dx = r * dng - x * (r * r * r) * jnp.mean(dng * x, axis=-1, keepdims=True)
    return dx, jnp.sum(dn * x * r, axis=0, keepdims=True)


def _load_weights(wg_ref, plan, sems):
    copies = []
    for j, (dst, key) in enumerate(plan):
        off, n = PK_OFF[key], N_ROWS[key]
        for d in range(NDEV):
            copies.append(pltpu.make_async_copy(
                wg_ref.at[d, pl.ds(off, n), :], dst.at[pl.ds(d * n, n), :], sems.at[j * NDEV + d]))
    for cp in copies:
        cp.start()
    for cp in copies:
        cp.wait()


_ANY = pl.BlockSpec(memory_space=pl.ANY)


def _row_spec(tm, width, rev_nt=None):
    if rev_nt is None:
        return pl.BlockSpec((tm, width), lambda i: (i, 0))
    return pl.BlockSpec((tm, width), lambda i: (rev_nt - 1 - i, 0))


def _full_spec(shape):
    return pl.BlockSpec(shape, lambda *_: (0,) * len(shape))


def _sds(shape, dtype):
    return jax.ShapeDtypeStruct(shape, dtype)


def _mesh_pos():
    return lax.axis_index("x"), lax.axis_index("y"), lax.axis_index("c")


def _ag_exchange(shard_ref, wg_ref, send_sems, recv_sems, local_sem, r0, nr):
    x, y, c = _mesh_pos()
    me, sibling = (x, y, c), (x, y, 1 - c)
    chips = [(1 - x, y), (x, 1 - y), (1 - x, 1 - y)]
    src = shard_ref.at[pl.ds(r0, nr), :]

    def slot(p):
        return wg_ref.at[4 * p[0] + 2 * p[1] + p[2], pl.ds(r0, nr), :]

    def copy(k, block, to, own=False):
        return pltpu.make_async_remote_copy(
            src_ref=src if own else slot(block), dst_ref=slot(block),
            send_sem=send_sems.at[k], recv_sem=recv_sems.at[k], device_id=to, device_id_type=MESH)

    mine = pltpu.make_async_copy(src, slot(me), local_sem)
    first = [copy(0, me, sibling, own=True)]
    first += [copy(1 + j, me, (*chip, c), own=True) for j, chip in enumerate(chips)]
    passed = [copy(4 + j, (*chip, c), sibling) for j, chip in enumerate(chips)]

    def start():
        mine.start()
        for cp in first:
            cp.start()

    def finish():
        for j, chip in enumerate(chips):
            copy(1 + j, (*chip, c), me).wait_recv()
            passed[j].start()
        copy(0, sibling, me).wait_recv()
        for j, chip in enumerate(chips):
            copy(4 + j, (*chip, 1 - c), me).wait_recv()
        for cp in first + passed:
            cp.wait_send()
        mine.wait()

    return start, finish


def _rs_exchange(pieces, land_ref, send_sems, recv_sems, keys):
    x, y, c = _mesh_pos()
    my_lin = 4 * x + 2 * y + c
    l0, total = _span(LD_OFF, keys)
    peers = [((x + ((j >> 2) & 1)) % 2, (y + ((j >> 1) & 1)) % 2, (c + (j & 1)) % 2) for j in range(NDEV)]

    def start():
        for j, to in enumerate(peers):
            t_lin = 4 * to[0] + 2 * to[1] + to[2]
            for ref, key in zip(pieces, keys):
                src = ref if key == "repl" else ref.at[t_lin]
                dst = land_ref.at[my_lin, pl.ds(LD_OFF[key], N_ROWS[key]), :]
                if j == 0:
                    pltpu.make_async_copy(src, dst, send_sems.at[0]).start()
                else:
                    pltpu.make_async_remote_copy(
                        src_ref=src, dst_ref=dst, send_sem=send_sems.at[j], recv_sem=recv_sems.at[j],
                        device_id=to, device_id_type=MESH).start()

    def finish():
        whole = land_ref.at[0, pl.ds(l0, total), :]
        pltpu.make_async_copy(whole, whole, send_sems.at[0]).wait()
        for j in range(1, NDEV):
            pltpu.make_async_remote_copy(
                src_ref=whole, dst_ref=whole, send_sem=send_sems.at[j], recv_sem=recv_sems.at[j],
                device_id=peers[j], device_id_type=MESH).wait()

    return start, finish


class _Comm:
    def __init__(self, ins, alias_in, out_shape, scratch, make, gives_wg):
        self.ins, self.alias_in, self.out_shape = ins, alias_in, out_shape
        self.scratch, self.make, self.gives_wg = scratch, make, gives_wg


def _ag_comm(shard, wg, keys):
    r0, nr = _span(PK_OFF, keys)

    def make(c_ins, c_out, sc):
        return _ag_exchange(c_ins[0], c_out, sc[0], sc[1], sc[2], r0, nr)

    return _Comm([shard, wg], 1, _sds(wg.shape, BF16),
                 [pltpu.SemaphoreType.DMA((7,)), pltpu.SemaphoreType.DMA((7,)), pltpu.SemaphoreType.DMA],
                 make, True)


def _rs_comm(arrays, keys, land):
    def make(c_ins, c_out, sc):
        return _rs_exchange(c_ins[:len(keys)], c_out, sc[0], sc[1], keys)

    ins = list(arrays) + ([] if land is None else [land])
    return _Comm(ins, None if land is None else len(arrays), _sds((NDEV, LD_ROWS, D), BF16),
                 [pltpu.SemaphoreType.DMA((NDEV,)), pltpu.SemaphoreType.DMA((NDEV,))], make, False)


def _hosted_call(body, name, nt, in_specs, out_specs, out_shape, scratch, args, comm):
    if comm is None:
        return _pcall(body, name=name, grid=(nt,), in_specs=in_specs, out_specs=tuple(out_specs),
                      out_shape=tuple(out_shape), scratch_shapes=scratch,
                      compiler_params=_cparams(("arbitrary",)))(*args)
    n_in, n_out, n_sc, n_cin = len(in_specs), len(out_specs), len(scratch), len(comm.ins)

    def wrapped(*refs):
        ins = refs[:n_in]
        c_ins = refs[n_in:n_in + n_cin]
        outs = refs[n_in + n_cin:n_in + n_cin + n_out]
        c_out = refs[n_in + n_cin + n_out]
        sc = refs[n_in + n_cin + n_out + 1:n_in + n_cin + n_out + 1 + n_sc]
        c_sc = refs[n_in + n_cin + n_out + 1 + n_sc:]
        start, finish = comm.make(c_ins, c_out, c_sc)

        @pl.when(pl.program_id(0) == 0)
        def _():
            start()

        if comm.gives_wg:
            body(*ins, c_out, *outs, *sc)
        else:
            body(*ins, *outs, *sc)

        @pl.when(pl.program_id(0) == nt - 1)
        def _():
            finish()

    aliases = {} if comm.alias_in is None else {n_in + comm.alias_in: n_out}
    res = _pcall(wrapped, name=name, grid=(nt,),
                 in_specs=list(in_specs) + [_ANY] * n_cin, out_specs=tuple(out_specs) + (_ANY,),
                 out_shape=tuple(out_shape) + (comm.out_shape,),
                 scratch_shapes=list(scratch) + list(comm.scratch),
                 input_output_aliases=aliases,
                 compiler_params=_cparams(("arbitrary",)))(*args, *comm.ins)
    return res


def _pack_shard(parts, small_bf16):
    def body(*refs):
        out = refs[-1]
        for key, ref in zip(WEIGHT_KEYS, refs[:-2]):
            out[pl.ds(PK_OFF[key], N_ROWS[key]), :] = ref[...].astype(BF16)
        out[pl.ds(PK_OFF["small"], N_ROWS["small"]), :] = refs[-2][...]

    return _pcall(
        body, name="pack_shard",
        out_shape=_sds((PK_ROWS, D), BF16),
        in_specs=[pl.BlockSpec(memory_space=pltpu.VMEM)] * (len(parts) + 1),
        out_specs=pl.BlockSpec(memory_space=pltpu.VMEM),
        compiler_params=_cparams(),
    )(*parts, small_bf16)


def _all_gather_first(shard):
    keys = ("a_in", "a_out", "small")
    r0, nr = _span(PK_OFF, keys)

    def body(x_ref, wg_ref, sm_ref, send_sems, recv_sems, local_sem, sm_sem):
        start, finish = _ag_exchange(x_ref, wg_ref, send_sems, recv_sems, local_sem, r0, nr)
        start()
        finish()
        cp = pltpu.make_async_copy(wg_ref.at[:, pl.ds(PK_OFF["small"], N_ROWS["small"]), :], sm_ref, sm_sem)
        cp.start()
        cp.wait()

    return _pcall(
        body, name="all_gather_first",
        out_shape=(_sds((NDEV, PK_ROWS, D), BF16), _sds((NDEV, N_ROWS["small"], D), BF16)),
        in_specs=[_ANY], out_specs=(_ANY, _ANY),
        scratch_shapes=[pltpu.SemaphoreType.DMA((7,)), pltpu.SemaphoreType.DMA((7,)),
                        pltpu.SemaphoreType.DMA, pltpu.SemaphoreType.DMA],
    )(shard)


def _exchange_last(arrays, keys, land):
    def body(*refs):
        land_ref, send_sems, recv_sems = refs[len(arrays) + 1:]
        start, finish = _rs_exchange(refs[:len(arrays)], land_ref, send_sems, recv_sems, keys)
        start()
        finish()

    return _pcall(
        body, name="exchange_last",
        out_shape=_sds((NDEV, LD_ROWS, D), BF16),
        in_specs=[_ANY] * (len(arrays) + 1), out_specs=_ANY,
        input_output_aliases={len(arrays): 0},
        scratch_shapes=[pltpu.SemaphoreType.DMA((NDEV,)), pltpu.SemaphoreType.DMA((NDEV,))],
    )(*arrays, land)


def _mixa_fwd(x, gam, cw, wg, tm, comm):
    T = x.shape[0]

    def body(x_ref, gam_ref, cw_ref, wg_ref, h_ref, n_ref, b_ref, c_ref, v_ref, cc_ref, y_ref,
             win, wout, buf, sems):
        @pl.when(pl.program_id(0) == 0)
        def _():
            _load_weights(wg_ref, [(win, "a_in"), (wout, "a_out")], sems)
            buf[pl.ds(0, HALO_A), :] = jnp.zeros((HALO_A, D), F32)

        xv = x_ref[...]
        nb = (xv * _rms_stat(xv) * gam_ref[...]).astype(BF16)
        n_ref[...] = nb
        bv = _dot_nt(nb, win[pl.ds(0, D), :])
        cval = _dot_nt(nb, win[pl.ds(D, D), :])
        vval = _dot_nt(nb, win[pl.ds(2 * D, D), :])
        cv = cval * vval
        buf[pl.ds(HALO_A, tm), :] = cv
        cc = cw_ref[pl.ds(KA - 1, 1), :] * cv
        for k in range(KA - 1):
            cc = cc + cw_ref[pl.ds(k, 1), :] * buf[pl.ds(HALO_A - (KA - 1) + k, tm), :]
        buf[pl.ds(0, HALO_A), :] = buf[pl.ds(tm, HALO_A), :]
        yb = (bv * cc).astype(BF16)
        b_ref[...] = bv.astype(BF16)
        c_ref[...] = cval.astype(BF16)
        v_ref[...] = vval.astype(BF16)
        cc_ref[...] = cc.astype(BF16)
        y_ref[...] = yb
        h_ref[...] = xv + _dot(yb, wout[...])

    act = _sds((T, D), BF16)
    return _hosted_call(
        body, "mixa_fwd", T // tm,
        in_specs=[_row_spec(tm, D), _full_spec((1, D)), _full_spec((KA, D))],
        out_specs=[_row_spec(tm, D) for _ in range(7)],
        out_shape=(_sds((T, D), F32),) + (act,) * 6,
        scratch=[pltpu.VMEM((3 * D, D), BF16), pltpu.VMEM((D, D), BF16),
                 pltpu.VMEM((HALO_A + tm, D), F32), pltpu.SemaphoreType.DMA((2 * NDEV,))],
        args=(x, gam, cw), comm=comm)


def _ffn_fwd(h, gam, wg, layer, tm, comm):
    T = h.shape[0]
    kg, ku, kd = "g%d" % layer, "u%d" % layer, "d%d" % layer

    def body(h_ref, gam_ref, wg_ref, o_ref, n_ref, g_ref, u_ref, gu_ref, wgt, wut, wd, sems):
        @pl.when(pl.program_id(0) == 0)
        def _():
            _load_weights(wg_ref, [(wgt, kg), (wut, ku), (wd, kd)], sems)

        hv = h_ref[...]
        nb = (hv * _rms_stat(hv) * gam_ref[...]).astype(BF16)
        n_ref[...] = nb
        out = hv
        for f in range(2):
            cols = pl.ds(f * FH, FH)
            g = _dot_nt(nb, wgt[cols, :])
            u = _dot_nt(nb, wut[cols, :])
            gu = (g * _sigmoid(g) * u).astype(BF16)
            g_ref[:, cols] = g.astype(BF16)
            u_ref[:, cols] = u.astype(BF16)
            gu_ref[:, cols] = gu
            out = out + _dot(gu, wd[cols, :])
        o_ref[...] = out

    pre = _sds((T, FF), BF16)
    in_specs = [_row_spec(tm, D), _full_spec((1, D))]
    args = (h, gam)
    if comm is None:
        in_specs, args = in_specs + [_ANY], args + (wg,)
    return _hosted_call(
        body, "ffn%d_fwd" % layer, T // tm,
        in_specs=in_specs,
        out_specs=[_row_spec(tm, D), _row_spec(tm, D), _row_spec(tm, FF), _row_spec(tm, FF), _row_spec(tm, FF)],
        out_shape=(_sds((T, D), F32), _sds((T, D), BF16), pre, pre, pre),
        scratch=[pltpu.VMEM((FF, D), BF16)] * 3 + [pltpu.SemaphoreType.DMA((3 * NDEV,))],
        args=args, comm=comm)


def _shifted_copies(buf, shf, tm):
    for r in range(1, 8):
        shf[r - 1] = buf[pl.ds(r, tm + HALO_B - 8), :]


def _broadcast_taps(vec_ref, wb):
    for k in range(KB):
        wb[k] = jnp.broadcast_to(vec_ref[pl.ds(SM_B_CONV + k, 1), :], (8, D))
    wb[KB] = jnp.broadcast_to(vec_ref[pl.ds(SM_B_BCONV, 1), :], (8, D))


def _taps_by_shift_residue(taps):
    groups = {}
    for k, shift in taps:
        q, r = divmod(shift, 8)
        groups.setdefault(r, []).append((k, q))
    return sorted(groups.items())


def _window(buf, shf, base, r, q0, n_groups, lanes):
    rows = pl.ds(base + 8 * q0, 8 * n_groups)
    v = buf[rows, lanes] if r == 0 else shf[r - 1, rows, lanes]
    return [v[8 * i:8 * i + 8] for i in range(n_groups)]


def _long_conv(buf, shf, wb, out_ref, tm, taps, bias_row):
    n_acc = CONV_ROWS // 8
    groups = _taps_by_shift_residue(taps)
    for col in range(D // 128):
        lanes = pl.ds(128 * col, 128)

        def rows(i, carry, lanes=lanes):
            base = i * CONV_ROWS
            init = jnp.zeros((8, 128), F32) if bias_row is None else wb[bias_row, :, lanes]
            accs = [init] * n_acc
            for r, lst in groups:
                q0, q1 = min(q for _, q in lst), max(q for _, q in lst)
                win = _window(buf, shf, base, r, q0, n_acc + q1 - q0, lanes)
                for k, q in lst:
                    wk = wb[k, :, lanes]
                    accs = [acc + wk * win[h + q - q0] for h, acc in enumerate(accs)]
            for h, acc in enumerate(accs):
                out_ref[pl.ds(base + 8 * h, 8), lanes] = acc
            return carry

        for i in range(tm // CONV_ROWS):
            rows(i, 0)


def _long_conv_grad_taps(buf, shf, x_ref, wacc, tm):
    n_acc = CONV_ROWS // 8
    groups = _taps_by_shift_residue([(KB - 1 - j, j) for j in range(KB)])
    for col in range(D // 128):
        lanes = pl.ds(128 * col, 128)

        def rows(i, carry, lanes=lanes):
            base = i * CONV_ROWS
            xv = x_ref[pl.ds(base, CONV_ROWS), lanes]
            xs = [xv[8 * h:8 * h + 8] for h in range(n_acc)]
            for r, lst in groups:
                q0, q1 = min(q for _, q in lst), max(q for _, q in lst)
                win = _window(buf, shf, base, r, q0, n_acc + q1 - q0, lanes)
                for k, q in lst:
                    prod = [x * win[h + q - q0] for h, x in enumerate(xs)]
                    wacc[k, :, lanes] += functools.reduce(lambda a, b: a + b, prod)
            return carry

        for i in range(tm // CONV_ROWS):
            rows(i, 0)


def _ln_stats(dc):
    mu = jnp.mean(dc, axis=-1, keepdims=True)
    xc = dc - mu
    rstd = lax.rsqrt(jnp.mean(xc * xc, axis=-1, keepdims=True) + LN_EPS)
    return xc * rstd, rstd


def _mixb_fwd(h, vecs, bias1, wg, tm, comm):
    T = h.shape[0]

    def body(h_ref, vec_ref, b1_ref, wg_ref, o_ref, n_ref, a_ref, g_ref, dc_ref, s_ref, w1, w2, buf, shf, wb, sems):
        @pl.when(pl.program_id(0) == 0)
        def _():
            _load_weights(wg_ref, [(w1, "b_pw1"), (w2, "b_pw2")], sems)
            buf[pl.ds(0, HALO_B), :] = jnp.zeros((HALO_B, D), F32)
            _broadcast_taps(vec_ref, wb)

        hv = h_ref[...]
        nb = (hv * _rms_stat(hv) * vec_ref[pl.ds(SM_B_NORM, 1), :]).astype(BF16)
        n_ref[...] = nb
        a = _dot_nt(nb, w1[pl.ds(0, D), :]) + b1_ref[:, pl.ds(0, D)]
        g = _dot_nt(nb, w1[pl.ds(D, D), :]) + b1_ref[:, pl.ds(D, D)]
        a_ref[...] = a.astype(BF16)
        g_ref[...] = g.astype(BF16)
        buf[pl.ds(HALO_B, tm), :] = a * _sigmoid(g)
        _shifted_copies(buf, shf, tm)

        _long_conv(buf, shf, wb, dc_ref, tm, [(k, HALO_B - (KB - 1) + k) for k in range(KB)], KB)
        buf[pl.ds(0, HALO_B), :] = buf[pl.ds(tm, HALO_B), :]
        xhat, _ = _ln_stats(dc_ref[...])
        ln = xhat * vec_ref[pl.ds(SM_LN_G, 1), :] + vec_ref[pl.ds(SM_LN_B, 1), :]
        s = (ln * _sigmoid(ln)).astype(BF16)
        s_ref[...] = s
        o_ref[...] = hv + _dot(s, w2[...]) + vec_ref[pl.ds(SM_B_PW2, 1), :]

    act = _sds((T, D), BF16)
    return _hosted_call(
        body, "mixb_fwd", T // tm,
        in_specs=[_row_spec(tm, D), _full_spec((SM_F32_ROWS, D)), _full_spec((1, 2 * D))],
        out_specs=[_row_spec(tm, D) for _ in range(6)],
        out_shape=(_sds((T, D), F32), act, act, act, _sds((T, D), F32), act),
        scratch=[pltpu.VMEM((2 * D, D), BF16), pltpu.VMEM((D, D), BF16),
                 pltpu.VMEM((HALO_B + tm, D), F32), pltpu.VMEM((7, HALO_B + tm - 8, D), F32),
                 pltpu.VMEM((KB + 1, 8, D), F32), pltpu.SemaphoreType.DMA((2 * NDEV,))],
        args=(h, vecs, bias1), comm=comm)


def _loss_head(h, tgt, gam, tm):
    T = h.shape[0]

    def body(h_ref, t_ref, gam_ref, dh_ref, st_ref):
        @pl.when(pl.program_id(0) == 0)
        def _():
            st_ref[...] = jnp.zeros((8, D), F32)

        hv = h_ref[...]
        gamma = gam_ref[...]
        r = _rms_stat(hv)
        err = hv * r * gamma - t_ref[...]
        dx, dgam = _rms_bwd(err * (1.0 / D), hv, r, gamma)
        dh_ref[...] = dx
        st_ref[pl.ds(0, 1), :] += dgam
        st_ref[pl.ds(1, 1), :] += (0.5 / D) * jnp.sum(err * err, axis=0, keepdims=True)

    return _pcall(
        body, name="loss_head", grid=(T // tm,),
        out_shape=(_sds((T, D), F32), _sds((8, D), F32)),
        in_specs=[_row_spec(tm, D), _row_spec(tm, D), _full_spec((1, D))],
        out_specs=(_row_spec(tm, D), _full_spec((8, D))),
        compiler_params=_cparams(("arbitrary",)),
    )(h, tgt, gam)


def _ffn_bwd_dx(dh, h, g, u, gam, wg, layer, tm, comm):
    T = h.shape[0]
    kg, ku, kd = "g%d" % layer, "u%d" % layer, "d%d" % layer

    def body(dh_ref, h_ref, g_ref, u_ref, gam_ref, wg_ref, o_ref, dg_ref, du_ref, st_ref, wgt, wut, wd, sems):
        @pl.when(pl.program_id(0) == 0)
        def _():
            _load_weights(wg_ref, [(wgt, kg), (wut, ku), (wd, kd)], sems)
            st_ref[...] = jnp.zeros((8, D), F32)

        dhv = dh_ref[...]
        dhb = dhv.astype(BF16)
        dn = jnp.zeros_like(dhv)
        for f in range(2):
            cols = pl.ds(f * FH, FH)
            dgu = _dot_nt(dhb, wd[cols, :])
            gv = g_ref[:, cols].astype(F32)
            sg = _sigmoid(gv)
            du = (dgu * gv * sg).astype(BF16)
            dg = (dgu * u_ref[:, cols].astype(F32) * (sg * (1.0 + gv * (1.0 - sg)))).astype(BF16)
            dg_ref[:, cols] = dg
            du_ref[:, cols] = du
            dn = dn + _dot(dg, wgt[cols, :]) + _dot(du, wut[cols, :])
        hv = h_ref[...]
        dx, dgam = _rms_bwd(dn, hv, _rms_stat(hv), gam_ref[...])
        o_ref[...] = dhv + dx
        st_ref[pl.ds(0, 1), :] += dgam

    pre = _sds((T, FF), BF16)
    return _hosted_call(
        body, "ffn%d_bwd_dx" % layer, T // tm,
        in_specs=[_row_spec(tm, D), _row_spec(tm, D), _row_spec(tm, FF), _row_spec(tm, FF),
                  _full_spec((1, D)), _ANY],
        out_specs=[_row_spec(tm, D), _row_spec(tm, FF), _row_spec(tm, FF), _full_spec((8, D))],
        out_shape=(_sds((T, D), F32), pre, pre, _sds((8, D), F32)),
        scratch=[pltpu.VMEM((FF, D), BF16)] * 3 + [pltpu.SemaphoreType.DMA((3 * NDEV,))],
        args=(dh, h, g, u, gam, wg), comm=comm)


def _grad_w(lhs, rhs, mc, name, tm):
    T, M = lhs.shape
    nt = T // tm

    def body(l_ref, r_ref, o_ref, acc):
        i = pl.program_id(1)

        @pl.when(i == 0)
        def _():
            acc[...] = jnp.zeros((mc, D), F32)

        acc[...] += _dot_tn(l_ref[...], r_ref[...].astype(BF16))

        @pl.when(i == nt - 1)
        def _():
            o_ref[...] = acc[...].astype(BF16)

    return _pcall(
        body, name=name, grid=(M // mc, nt),
        out_shape=_sds((M, D), BF16),
        in_specs=[pl.BlockSpec((tm, mc), lambda j, i: (i, j)), pl.BlockSpec((tm, D), lambda j, i: (i, 0))],
        out_specs=pl.BlockSpec((mc, D), lambda j, i: (j, 0)),
        scratch_shapes=[pltpu.VMEM((mc, D), F32)],
        compiler_params=_cparams(("arbitrary", "arbitrary")),
    )(lhs, rhs)


def _mixb_bwd(dh, h, a, g, dc, vecs, wg, tm, comm):
    T = h.shape[0]
    nt = T // tm

    def body(dh_ref, h_ref, a_ref, g_ref, dc_ref, vec_ref, wg_ref, o_ref, du_ref, st_ref, sb_ref,
             w1, w2, buf, shf, glu_s, dglu_s, wacc, wb, sems):
        @pl.when(pl.program_id(0) == 0)
        def _():
            _load_weights(wg_ref, [(w1, "b_pw1"), (w2, "b_pw2")], sems)
            buf[pl.ds(tm, HALO_B), :] = jnp.zeros((HALO_B, D), F32)
            wacc[...] = jnp.zeros((KB, 8, D), F32)
            _broadcast_taps(vec_ref, wb)
            st_ref[...] = jnp.zeros((SM_F32_ROWS, D), F32)
            sb_ref[...] = jnp.zeros((8, 2 * D), F32)

        def acc(row, val):
            st_ref[pl.ds(row, 1), :] += jnp.sum(val, axis=0, keepdims=True)

        dhv = dh_ref[...]
        acc(SM_B_PW2, dhv)
        ds = _dot_nt(dhv.astype(BF16), w2[...])
        xhat, rstd = _ln_stats(dc_ref[...])
        ln_g = vec_ref[pl.ds(SM_LN_G, 1), :]
        ln = xhat * ln_g + vec_ref[pl.ds(SM_LN_B, 1), :]
        sl = _sigmoid(ln)
        dln = ds * (sl * (1.0 + ln * (1.0 - sl)))
        acc(SM_LN_G, dln * xhat)
        acc(SM_LN_B, dln)
        dxh = dln * ln_g
        ddc = rstd * (dxh - jnp.mean(dxh, axis=-1, keepdims=True)
                      - xhat * jnp.mean(dxh * xhat, axis=-1, keepdims=True))
        acc(SM_B_BCONV, ddc)
        buf[pl.ds(0, tm), :] = ddc
        _shifted_copies(buf, shf, tm)
        av = a_ref[...].astype(F32)
        sg = _sigmoid(g_ref[...].astype(F32))
        glu_s[...] = av * sg

        _long_conv(buf, shf, wb, dglu_s, tm, [(KB - 1 - j, j) for j in range(KB)], None)
        _long_conv_grad_taps(buf, shf, glu_s, wacc, tm)
        buf[pl.ds(tm, HALO_B), :] = buf[pl.ds(0, HALO_B), :]
        dglu = dglu_s[...]
        da = dglu * sg
        dg = dglu * av * sg * (1.0 - sg)
        sb_ref[pl.ds(0, 1), pl.ds(0, D)] += jnp.sum(da, axis=0, keepdims=True)
        sb_ref[pl.ds(0, 1), pl.ds(D, D)] += jnp.sum(dg, axis=0, keepdims=True)
        dab, dgb = da.astype(BF16), dg.astype(BF16)
        du_ref[:, pl.ds(0, D)] = dab
        du_ref[:, pl.ds(D, D)] = dgb
        dn = _dot(dab, w1[pl.ds(0, D), :]) + _dot(dgb, w1[pl.ds(D, D), :])
        hv = h_ref[...]
        dx, dgam = _rms_bwd(dn, hv, _rms_stat(hv), vec_ref[pl.ds(SM_B_NORM, 1), :])
        o_ref[...] = dhv + dx
        st_ref[pl.ds(SM_B_NORM, 1), :] += dgam

        @pl.when(pl.program_id(0) == nt - 1)
        def _():
            st_ref[pl.ds(SM_B_CONV, KB), :] = jnp.sum(wacc[...], axis=1)

    rs = functools.partial(_row_spec, rev_nt=nt)
    return _hosted_call(
        body, "mixb_bwd", nt,
        in_specs=[rs(tm, D), rs(tm, D), rs(tm, D), rs(tm, D), rs(tm, D), _full_spec((SM_F32_ROWS, D)), _ANY],
        out_specs=[rs(tm, D), rs(tm, 2 * D), _full_spec((SM_F32_ROWS, D)), _full_spec((8, 2 * D))],
        out_shape=(_sds((T, D), F32), _sds((T, 2 * D), BF16), _sds((SM_F32_ROWS, D), F32), _sds((8, 2 * D), F32)),
        scratch=[pltpu.VMEM((2 * D, D), BF16), pltpu.VMEM((D, D), BF16),
                 pltpu.VMEM((tm + HALO_B, D), F32), pltpu.VMEM((7, tm + HALO_B - 8, D), F32),
                 pltpu.VMEM((tm, D), F32), pltpu.VMEM((tm, D), F32), pltpu.VMEM((KB, 8, D), F32),
                 pltpu.VMEM((KB + 1, 8, D), F32), pltpu.SemaphoreType.DMA((2 * NDEV,))],
        args=(dh, h, a, g, dc, vecs, wg), comm=comm)


def _mixa_bwd(dh, x, b, c, v, cc, gam, cw, wg, tm, comm):
    T = x.shape[0]
    nt = T // tm

    def body(dh_ref, x_ref, b_ref, c_ref, v_ref, cc_ref, gam_ref, cw_ref, wg_ref,
             o_ref, db_ref, st_ref, win, wout, buf, sems):
        @pl.when(pl.program_id(0) == 0)
        def _():
            _load_weights(wg_ref, [(win, "a_in"), (wout, "a_out")], sems)
            buf[pl.ds(tm, HALO_A), :] = jnp.zeros((HALO_A, D), F32)
            st_ref[...] = jnp.zeros((8, D), F32)

        dhv = dh_ref[...]
        dy = _dot_nt(dhv.astype(BF16), wout[...])
        cval = c_ref[...].astype(F32)
        vval = v_ref[...].astype(F32)
        d_b = (dy * cc_ref[...].astype(F32)).astype(BF16)
        buf[pl.ds(0, tm), :] = dy * b_ref[...].astype(F32)
        cv = cval * vval
        dcv = jnp.zeros((tm, D), F32)
        for j in range(KA):
            sh = buf[pl.ds(j, tm), :]
            k = KA - 1 - j
            dcv = dcv + cw_ref[pl.ds(k, 1), :] * sh
            st_ref[pl.ds(1 + k, 1), :] += jnp.sum(cv * sh, axis=0, keepdims=True)
        buf[pl.ds(tm, HALO_A), :] = buf[pl.ds(0, HALO_A), :]
        d_c = (dcv * vval).astype(BF16)
        d_v = (dcv * cval).astype(BF16)
        db_ref[:, pl.ds(0, D)] = d_b
        db_ref[:, pl.ds(D, D)] = d_c
        db_ref[:, pl.ds(2 * D, D)] = d_v
        dn = _dot(d_b, win[pl.ds(0, D), :]) + _dot(d_c, win[pl.ds(D, D), :]) + _dot(d_v, win[pl.ds(2 * D, D), :])
        xv = x_ref[...]
        dx, dgam = _rms_bwd(dn, xv, _rms_stat(xv), gam_ref[...])
        o_ref[...] = dhv + dx
        st_ref[pl.ds(0, 1), :] += dgam

    rs = functools.partial(_row_spec, rev_nt=nt)
    return _hosted_call(
        body, "mixa_bwd", nt,
        in_specs=[rs(tm, D) for _ in range(6)] + [_full_spec((1, D)), _full_spec((KA, D)), _ANY],
        out_specs=[rs(tm, D), rs(tm, 3 * D), _full_spec((8, D))],
        out_shape=(_sds((T, D), F32), _sds((T, 3 * D), BF16), _sds((8, D), F32)),
        scratch=[pltpu.VMEM((3 * D, D), BF16), pltpu.VMEM((D, D), BF16),
                 pltpu.VMEM((tm + HALO_A, D), F32), pltpu.SemaphoreType.DMA((2 * NDEV,))],
        args=(dh, x, b, c, v, cc, gam, cw, wg), comm=comm)


def _sum_weights(land):
    rows = N_WROWS // 4

    def body(l_ref, o_ref):
        acc = l_ref[0].astype(F32)
        for s in range(1, NDEV):
            acc = acc + l_ref[s].astype(F32)
        o_ref[...] = acc

    return _pcall(
        body, name="sum_weight_grads", grid=(4,),
        out_shape=_sds((N_WROWS, D), F32),
        in_specs=[pl.BlockSpec((NDEV, rows, D), lambda i: (0, i, 0))],
        out_specs=pl.BlockSpec((rows, D), lambda i: (i, 0)),
        compiler_params=_cparams(("arbitrary",)),
    )(land)


def _sum_small(sm, rp):
    def body(sm_ref, rp_ref, osm, orp, oloss):
        a = sm_ref[0]
        b = rp_ref[0]
        for s in range(1, NDEV):
            a = a + sm_ref[s]
            b = b + rp_ref[s]
        osm[...] = a
        orp[...] = b
        oloss[...] = jnp.zeros((8, 128), F32) + jnp.sum(b[4:5, :], axis=-1, keepdims=True)

    return _pcall(
        body, name="sum_small_grads",
        out_shape=(_sds((SM_F32_ROWS, 128), F32), _sds((8, D), F32), _sds((8, 128), F32)),
        in_specs=[pl.BlockSpec(memory_space=pltpu.VMEM)] * 2,
        out_specs=tuple(pl.BlockSpec(memory_space=pltpu.VMEM) for _ in range(3)),
        compiler_params=_cparams(),
    )(sm, rp)


def _adamw(w, g, m, v, name):
    rows, cols = w.shape
    tr = rows
    for cand in (256, 128, 64, 32, 16, 8):
        if rows % cand == 0:
            tr = cand
            break

    def body(w_ref, g_ref, m_ref, v_ref, od, om, ov):
        gv = g_ref[...]
        mn = B1 * m_ref[...] + (1.0 - B1) * gv
        vn = B2 * v_ref[...] + (1.0 - B2) * (gv * gv)
        m_hat = mn / (1.0 - B1 ** STEP)
        v_hat = vn / (1.0 - B2 ** STEP)
        od[...] = -LR * (m_hat / (jnp.sqrt(v_hat) + ADAM_EPS) + WD * w_ref[...])
        om[...] = mn
        ov[...] = vn

    spec = pl.BlockSpec((tr, cols), lambda i: (i, 0))
    shp = _sds((rows, cols), F32)
    return _pcall(
        body, name="adamw_" + name, grid=(rows // tr,),
        out_shape=(shp, shp, shp), in_specs=[spec] * 4, out_specs=(spec, spec, spec),
        compiler_params=_cparams(("arbitrary",)),
    )(w, g, m, v)


def _f32_as_bf16_rows(a):
    b = lax.bitcast_convert_type(a, BF16)
    return b.reshape(a.shape[:-1] + (2 * a.shape[-1],))


def _bf16_rows_as_f32(a):
    return lax.bitcast_convert_type(a.reshape(a.shape[:-1] + (a.shape[-1] // 2, 2)), F32)


def _pack_small(b_norm, b_b_pw1, b_conv, b_b_conv, b_ln_g, b_ln_b, b_b_pw2, a_conv, pad):
    rows = [b_norm.reshape(1, 128), b_b_pw1.reshape(2, 128), b_conv.reshape(KB, 128), b_b_conv.reshape(1, 128),
            b_ln_g.reshape(1, 128), b_ln_b.reshape(1, 128), b_b_pw2.reshape(1, 128), a_conv.reshape(KA, 128)]
    return jnp.concatenate(rows + [jnp.full((SM_F32_ROWS - SM_USED, 128), pad, F32)], axis=0)


def _unpack_small(p):
    return (p[SM_B_NORM:SM_B_NORM + 1], p[SM_B_PW1:SM_B_PW1 + 2].reshape(1, 256),
            p[SM_B_CONV:SM_B_CONV + KB].reshape(1, KB, 128), p[SM_B_BCONV:SM_B_BCONV + 1],
            p[SM_LN_G:SM_LN_G + 1], p[SM_LN_B:SM_LN_B + 1], p[SM_B_PW2:SM_B_PW2 + 1],
            p[SM_A_CONV:SM_A_CONV + KA].reshape(1, KA, 128))


def kernel(x, a_norm, a_w_in, a_conv, a_w_out, b_norm, b_w_pw1, b_b_pw1, b_conv, b_b_conv, b_ln_g, b_ln_b, b_w_pw2, b_b_pw2, ffn_norm, ffn_w_gate, ffn_w_up, ffn_w_down, final_norm, loss_target, m_a_norm, m_a_w_in, m_a_conv, m_a_w_out, m_b_norm, m_b_w_pw1, m_b_b_pw1, m_b_conv, m_b_b_conv, m_b_ln_g, m_b_ln_b, m_b_w_pw2, m_b_b_pw2, m_ffn_norm, m_ffn_w_gate, m_ffn_w_up, m_ffn_w_down, m_final_norm, v_a_norm, v_a_w_in, v_a_conv, v_a_w_out, v_b_norm, v_b_w_pw1, v_b_b_pw1, v_b_conv, v_b_b_conv, v_b_ln_g, v_b_ln_b, v_b_w_pw2, v_b_b_pw2, v_ffn_norm, v_ffn_w_gate, v_ffn_w_up, v_ffn_w_down, v_final_norm):
    T = x.shape[1]
    tm = min(TM, T)
    tw = min(TM_DW, T)
    xs = x.reshape(T, D)
    tgt = loss_target.reshape(T, D)

    parts = [a_w_in[0].T, a_w_out[0], b_w_pw1[0].T, b_w_pw2[0],
             ffn_w_gate[0].T, ffn_w_up[0].T, ffn_w_down[0],
             ffn_w_gate[1].T, ffn_w_up[1].T, ffn_w_down[1]]
    small = _pack_small(b_norm, b_b_pw1, b_conv, b_b_conv, b_ln_g, b_ln_b, b_b_pw2, a_conv, 0.0)
    shard = _pack_shard(parts, _f32_as_bf16_rows(small).reshape(N_ROWS["small"], D))
    wg, sm_rows = _all_gather_first(shard)
    sm_all = _bf16_rows_as_f32(sm_rows).reshape(NDEV, SM_F32_ROWS, 128)
    vecs = sm_all.transpose(1, 0, 2).reshape(SM_F32_ROWS, D)
    bias1 = sm_all[:, SM_B_PW1:SM_B_PW1 + 2, :].reshape(1, 2 * D)
    cw_a = vecs[SM_A_CONV:SM_A_CONV + KA]
    fn0, fn1 = ffn_norm[0:1], ffn_norm[1:2]
    fin = final_norm.reshape(1, D)

    h1, n0, bq, cq, vq, ccq, yq, wg = _mixa_fwd(xs, a_norm, cw_a, wg, tm, _ag_comm(shard, wg, ("g0", "u0", "d0")))
    h2, n1, g0, u0, gu0, wg = _ffn_fwd(h1, fn0, wg, 0, tm, _ag_comm(shard, wg, ("b_pw1", "b_pw2")))
    h3, n2, aq, gq, dcq, sq, wg = _mixb_fwd(h2, vecs, bias1, wg, tm, _ag_comm(shard, wg, ("g1", "u1", "d1")))
    h4, n3, g1, u1, gu1 = _ffn_fwd(h3, fn1, wg, 1, tm, None)
    dh4, st_fin = _loss_head(h4, tgt, fin, tm)

    def by_dest(gw, key):
        return gw.reshape(NDEV, N_ROWS[key], D)

    dh3, dg1, du1, st_f1 = _ffn_bwd_dx(dh4, h3, g1, u1, fn1, wg, 1, tm, None)
    gw_g1 = by_dest(_grad_w(dg1, n3, FH, "grad_gate1", tw), "g1")
    gw_u1 = by_dest(_grad_w(du1, n3, FH, "grad_up1", tw), "u1")
    gw_d1 = by_dest(_grad_w(gu1, dh4, FH, "grad_down1", tw), "d1")
    dh2, dub, st_b, st_b1, land = _mixb_bwd(
        dh3, h2, aq, gq, dcq, vecs, wg, tm, _rs_comm([gw_g1, gw_u1, gw_d1], ("g1", "u1", "d1"), None))
    gw_pw1 = by_dest(_grad_w(dub, n2, D, "grad_pw1", tw), "b_pw1")
    gw_pw2 = by_dest(_grad_w(sq, dh3, D, "grad_pw2", tw), "b_pw2")
    gw_d0 = by_dest(_grad_w(gu0, dh2, FH, "grad_down0", tw), "d0")
    dh1, dg0, du0, st_f0, land = _ffn_bwd_dx(
        dh2, h1, g0, u0, fn0, wg, 0, tm, _rs_comm([gw_pw1, gw_pw2, gw_d0], ("b_pw1", "b_pw2", "d0"), land))
    gw_g0 = by_dest(_grad_w(dg0, n1, FH, "grad_gate0", tw), "g0")
    gw_u0 = by_dest(_grad_w(du0, n1, FH, "grad_up0", tw), "u0")
    dx, dbcv, st_a, land = _mixa_bwd(
        dh1, xs, bq, cq, vq, ccq, a_norm, cw_a, wg, tm, _rs_comm([gw_g0, gw_u0], ("g0", "u0"), land))
    gw_in = by_dest(_grad_w(dbcv, n0, D, "grad_in", tw), "a_in")
    gw_out = by_dest(_grad_w(yq, dh1, D, "grad_out", tw), "a_out")

    st_small = st_b.at[SM_A_CONV:SM_A_CONV + KA].set(st_a[1:1 + KA])
    sm_dest = st_small.reshape(SM_F32_ROWS, NDEV, 128).transpose(1, 0, 2)
    sm_dest = sm_dest.at[:, SM_B_PW1:SM_B_PW1 + 2, :].set(st_b1[0].reshape(NDEV, 2, 128))
    sm_send = _f32_as_bf16_rows(sm_dest).reshape(NDEV, N_ROWS["small"], D)
    repl = jnp.concatenate([st_a[0:1], st_f0[0:1], st_f1[0:1], st_fin[0:1], st_fin[1:2],
                            jnp.zeros((3, D), F32)], axis=0)
    rp_send = _f32_as_bf16_rows(repl).reshape(N_ROWS["repl"], D)
    land = _exchange_last([gw_in, gw_out, sm_send, rp_send], ("a_in", "a_out", "small", "repl"), land)

    gsum = _sum_weights(land)
    sm_land = land[:, LD_OFF["small"]:LD_OFF["small"] + N_ROWS["small"], :]
    rp_land = land[:, LD_OFF["repl"]:LD_OFF["repl"] + N_ROWS["repl"], :]
    g_small, g_repl, loss8 = _sum_small(_bf16_rows_as_f32(sm_land).reshape(NDEV, SM_F32_ROWS, 128),
                                        _bf16_rows_as_f32(rp_land).reshape(NDEV, 8, D))
    loss = loss8[0, 0]

    def rows_of(k):
        return gsum[LD_OFF[k]:LD_OFF[k] + N_ROWS[k]]

    g_a_w_in = rows_of("a_in").T[None]
    g_a_w_out = rows_of("a_out")[None]
    g_b_w_pw1 = rows_of("b_pw1").T[None]
    g_b_w_pw2 = rows_of("b_pw2")[None]
    g_gate = jnp.stack([rows_of("g0").T, rows_of("g1").T])
    g_up = jnp.stack([rows_of("u0").T, rows_of("u1").T])
    g_down = jnp.stack([rows_of("d0"), rows_of("d1")])
    (g_b_norm, g_b_b_pw1, g_b_conv, g_b_b_conv, g_b_ln_g, g_b_ln_b, g_b_b_pw2, g_a_conv) = _unpack_small(g_small)
    g_a_norm = g_repl[0:1]
    g_ffn_norm = g_repl[1:3]
    g_final = g_repl[3]

    def adam(name, w, g, m, v):
        d, nm, nv = _adamw(w.reshape(-1, w.shape[-1]), g.reshape(-1, w.shape[-1]),
                           m.reshape(-1, w.shape[-1]), v.reshape(-1, w.shape[-1]), name)
        return d.reshape(w.shape), nm.reshape(w.shape), nv.reshape(w.shape)

    big = [("a_w_in", a_w_in, g_a_w_in, m_a_w_in, v_a_w_in),
           ("a_w_out", a_w_out, g_a_w_out, m_a_w_out, v_a_w_out),
           ("b_w_pw1", b_w_pw1, g_b_w_pw1, m_b_w_pw1, v_b_w_pw1),
           ("b_w_pw2", b_w_pw2, g_b_w_pw2, m_b_w_pw2, v_b_w_pw2),
           ("ffn_w_gate", ffn_w_gate, g_gate, m_ffn_w_gate, v_ffn_w_gate),
           ("ffn_w_up", ffn_w_up, g_up, m_ffn_w_up, v_ffn_w_up),
           ("ffn_w_down", ffn_w_down, g_down, m_ffn_w_down, v_ffn_w_down)]
    res = {name: (g,) + adam(name, w, g, m, v) for name, w, g, m, v in big}

    sm_m = _pack_small(m_b_norm, m_b_b_pw1, m_b_conv, m_b_b_conv, m_b_ln_g, m_b_ln_b, m_b_b_pw2, m_a_conv, 0.0)
    sm_v = _pack_small(v_b_norm, v_b_b_pw1, v_b_conv, v_b_b_conv, v_b_ln_g, v_b_ln_b, v_b_b_pw2, v_a_conv, 1.0)
    sd, snm, snv = _adamw(small, g_small, sm_m, sm_v, "small")
    small_names = ["b_norm", "b_b_pw1", "b_conv", "b_b_conv", "b_ln_g", "b_ln_b", "b_b_pw2", "a_conv"]
    small_g = (g_b_norm, g_b_b_pw1, g_b_conv, g_b_b_conv, g_b_ln_g, g_b_ln_b, g_b_b_pw2, g_a_conv)
    for name, g, d, nm, nv in zip(small_names, small_g, _unpack_small(sd), _unpack_small(snm), _unpack_small(snv)):
        res[name] = (g, d, nm, nv)

    def rep_pack(an, fn, fi):
        return jnp.concatenate([an, fn, fi.reshape(1, D), jnp.ones((4, D), F32)], axis=0)

    rd, rnm, rnv = _adamw(rep_pack(a_norm, ffn_norm, final_norm), g_repl.at[4:].set(0.0),
                          rep_pack(m_a_norm, m_ffn_norm, m_final_norm),
                          rep_pack(v_a_norm, v_ffn_norm, v_final_norm), "replicated")
    res["a_norm"] = (g_a_norm, rd[0:1], rnm[0:1], rnv[0:1])
    res["ffn_norm"] = (g_ffn_norm, rd[1:3], rnm[1:3], rnv[1:3])
    res["final_norm"] = (g_final, rd[3], rnm[3], rnv[3])

    order = ["a_norm", "a_w_in", "a_conv", "a_w_out", "b_norm", "b_w_pw1", "b_b_pw1", "b_conv", "b_b_conv",
             "b_ln_g", "b_ln_b", "b_w_pw2", "b_b_pw2", "ffn_norm", "ffn_w_gate", "ffn_w_up", "ffn_w_down",
             "final_norm"]
    out = [loss, dx.reshape(1, T, D)]
    for j in range(4):
        out += [res[k][j] for k in order]
    return tuple(out)
```

```python
import functools

import jax
import jax.numpy as jnp
from jax import lax
from jax.experimental import pallas as pl
from jax.experimental.pallas import tpu as pltpu

F32 = jnp.float32
BF16 = jnp.bfloat16
MESH = pl.DeviceIdType.MESH

D = 1024
FF = 2816
FH = FF // 2
NDEV = 8
KA = 3
KB = 31
HALO_A = 8
HALO_B = 32
CONV_ROWS = 64
RMS_EPS = 1e-6
LN_EPS = 1e-5
LR, B1, B2, ADAM_EPS, WD, STEP = 0.001, 0.9, 0.999, 1e-08, 0.01, 10

TM = 256
TM_DW = 512
VMEM_LIMIT = 56 * 1024 * 1024

N_ROWS = {"a_in": 384, "a_out": 128, "b_pw1": 256, "b_pw2": 128, "g0": 352, "u0": 352, "d0": 352,
          "g1": 352, "u1": 352, "d1": 352}
WEIGHT_KEYS = ("a_in", "a_out", "b_pw1", "b_pw2", "g0", "u0", "d0", "g1", "u1", "d1")
NCHIP = 4


def _offsets(order):
    off, o = {}, 0
    for k in order:
        off[k] = o
        o += N_ROWS[k]
    return off, o


PK_ORDER = ("a_in", "a_out", "g0", "u0", "d0", "b_pw1", "b_pw2", "g1", "u1", "d1")
PK_OFF, PK_ROWS = _offsets(PK_ORDER)
LD_ORDER = ("g1", "u1", "d1", "b_pw1", "b_pw2", "d0", "g0", "u0", "a_out", "a_in")
LD_OFF, N_WROWS = _offsets(LD_ORDER)


def _span(off, keys):
    return off[keys[0]], sum(N_ROWS[k] for k in keys)


SM_B_NORM, SM_B_PW1, SM_B_CONV, SM_B_BCONV, SM_LN_G, SM_LN_B, SM_B_PW2, SM_A_CONV = 0, 1, 3, 34, 35, 36, 37, 38
SM_USED = 41
SM_F32_ROWS = 64


def _pcall(body, **kw):
    return pl.pallas_call(body, **kw)


def _cparams(sem=None):
    return pltpu.CompilerParams(dimension_semantics=sem, vmem_limit_bytes=VMEM_LIMIT)


def _dot(a, b):
    return jnp.dot(a, b, preferred_element_type=F32)


def _dot_nt(a, b):
    return lax.dot_general(a, b, (((1,), (1,)), ((), ())), preferred_element_type=F32)


def _dot_tn(a, b):
    return lax.dot_general(a, b, (((0,), (0,)), ((), ())), preferred_element_type=F32)


def _sigmoid(v):
    return 1.0 / (1.0 + jnp.exp(-v))


def _rms_stat(x):
    return lax.rsqrt(jnp.mean(x * x, axis=-1, keepdims=True) + RMS_EPS)


def _rms_bwd(dn, x, r, gamma):
    dng = dn * gamma
    dx = r * dng - x * (r * r * r) * jnp.mean(dng * x, axis=-1, keepdims=True)
    return dx, jnp.sum(dn * x * r, axis=0, keepdims=True)


def _load_weights(wg_ref, plan, sems):
    copies = []
    for j, (dst, key) in enumerate(plan):
        off, n = PK_OFF[key], N_ROWS[key]
        for d in range(NDEV):
            copies.append(pltpu.make_async_copy(
                wg_ref.at[d, pl.ds(off, n), :], dst.at[pl.ds(d * n, n), :], sems.at[j * NDEV + d]))
    for cp in copies:
        cp.start()
    for cp in copies:
        cp.wait()


_ANY = pl.BlockSpec(memory_space=pl.ANY)


def _row_spec(tm, width, rev_nt=None):
    if rev_nt is None:
        return pl.BlockSpec((tm, width), lambda i: (i, 0))
    return pl.BlockSpec((tm, width), lambda i: (rev_nt - 1 - i, 0))


def _full_spec(shape):
    return pl.BlockSpec(shape, lambda *_: (0,) * len(shape))


def _sds(shape, dtype):
    return jax.ShapeDtypeStruct(shape, dtype)


def _mesh_pos():
    return lax.axis_index("x"), lax.axis_index("y"), lax.axis_index("c")


def _lin(p):
    return 4 * p[0] + 2 * p[1] + p[2]


def _ag_exchange(src, slot, send_sems, recv_sems, local_sem):
    x, y, c = _mesh_pos()
    me, sibling = (x, y, c), (x, y, 1 - c)
    chips = [(1 - x, y), (x, 1 - y), (1 - x, 1 - y)]

    def copy(k, block, to, own=False):
        return pltpu.make_async_remote_copy(
            src_ref=src if own else slot(block), dst_ref=slot(block),
            send_sem=send_sems.at[k], recv_sem=recv_sems.at[k], device_id=to, device_id_type=MESH)

    mine = pltpu.make_async_copy(src, slot(me), local_sem)
    first = [copy(0, me, sibling, own=True)]
    first += [copy(1 + j, me, (*chip, c), own=True) for j, chip in enumerate(chips)]
    passed = [copy(4 + j, (*chip, c), sibling) for j, chip in enumerate(chips)]

    def start():
        mine.start()
        for cp in first:
            cp.start()

    def finish():
        for j, chip in enumerate(chips):
            copy(1 + j, (*chip, c), me).wait_recv()
            passed[j].start()
        copy(0, sibling, me).wait_recv()
        for j, chip in enumerate(chips):
            copy(4 + j, (*chip, 1 - c), me).wait_recv()
        for cp in first + passed:
            cp.wait_send()
        mine.wait()

    return start, finish


def _chip_exchange(p_ref, land_ref, send_sems, recv_sems, keys):
    x, y, c = _mesh_pos()
    l0, rows = _span(LD_OFF, keys)
    dst = land_ref.at[2 * x + y, pl.ds(l0, rows), :]
    peers = [((x + (j >> 1)) % 2, (y + (j & 1)) % 2) for j in range(NCHIP)]

    def copy(j):
        tx, ty = peers[j]
        if j == 0:
            return pltpu.make_async_copy(p_ref.at[2 * tx + ty], dst, send_sems.at[0])
        return pltpu.make_async_remote_copy(
            src_ref=p_ref.at[2 * tx + ty], dst_ref=dst, send_sem=send_sems.at[j], recv_sem=recv_sems.at[j],
            device_id=(tx, ty, c), device_id_type=MESH)

    def start():
        for j in range(NCHIP):
            copy(j).start()

    def finish():
        for j in range(NCHIP):
            copy(j).wait()

    return start, finish


def _all_to_all_f32(src_for, dst_ref, send_sems, recv_sems):
    x, y, c = _mesh_pos()
    dst = dst_ref.at[_lin((x, y, c))]
    peers = [((x + ((j >> 2) & 1)) % 2, (y + ((j >> 1) & 1)) % 2, (c + (j & 1)) % 2) for j in range(NDEV)]

    def copy(j):
        if j == 0:
            return pltpu.make_async_copy(src_for(_lin(peers[0])), dst, send_sems.at[0])
        return pltpu.make_async_remote_copy(
            src_ref=src_for(_lin(peers[j])), dst_ref=dst, send_sem=send_sems.at[j], recv_sem=recv_sems.at[j],
            device_id=peers[j], device_id_type=MESH)

    def start():
        for j in range(NDEV):
            copy(j).start()

    def finish():
        for j in range(NDEV):
            copy(j).wait()

    return start, finish


class _Comm:
    def __init__(self, ins, alias_in, out_shape, scratch, make, gives_wg):
        self.ins, self.alias_in, self.out_shape = ins, alias_in, out_shape
        self.scratch, self.make, self.gives_wg = scratch, make, gives_wg


def _ag_comm(shard, wg, keys):
    def make(c_ins, c_out, sc):
        return _weights_exchange(c_ins[0], c_out, sc[0], sc[1], sc[2], keys)

    return _Comm([shard, wg], 1, _sds(wg.shape, BF16),
                 [pltpu.SemaphoreType.DMA((7,)), pltpu.SemaphoreType.DMA((7,)), pltpu.SemaphoreType.DMA],
                 make, True)


def _weights_exchange(shard_ref, wg_ref, send_sems, recv_sems, local_sem, keys):
    r0, nr = _span(PK_OFF, keys)
    return _ag_exchange(shard_ref.at[pl.ds(r0, nr), :], lambda p: wg_ref.at[_lin(p), pl.ds(r0, nr), :],
                        send_sems, recv_sems, local_sem)


def _rs_comm(psum, keys, land):
    def make(c_ins, c_out, sc):
        return _chip_exchange(c_ins[0], c_out, sc[0], sc[1], keys)

    ins = [psum] + ([] if land is None else [land])
    return _Comm(ins, None if land is None else 1, _sds((NCHIP, N_WROWS, D), BF16),
                 [pltpu.SemaphoreType.DMA((NCHIP,)), pltpu.SemaphoreType.DMA((NCHIP,))], make, False)


def _hosted_call(body, name, nt, in_specs, out_specs, out_shape, scratch, args, comm):
    if comm is None:
        return _pcall(body, name=name, grid=(nt,), in_specs=in_specs, out_specs=tuple(out_specs),
                      out_shape=tuple(out_shape), scratch_shapes=scratch,
                      compiler_params=_cparams(("arbitrary",)))(*args)
    n_in, n_out, n_sc, n_cin = len(in_specs), len(out_specs), len(scratch), len(comm.ins)

    def wrapped(*refs):
        ins = refs[:n_in]
        c_ins = refs[n_in:n_in + n_cin]
        outs = refs[n_in + n_cin:n_in + n_cin + n_out]
        c_out = refs[n_in + n_cin + n_out]
        sc = refs[n_in + n_cin + n_out + 1:n_in + n_cin + n_out + 1 + n_sc]
        c_sc = refs[n_in + n_cin + n_out + 1 + n_sc:]
        start, finish = comm.make(c_ins, c_out, c_sc)

        @pl.when(pl.program_id(0) == 0)
        def _():
            start()

        if comm.gives_wg:
            body(*ins, c_out, *outs, *sc)
        else:
            body(*ins, *outs, *sc)

        @pl.when(pl.program_id(0) == nt - 1)
        def _():
            finish()

    aliases = {} if comm.alias_in is None else {n_in + comm.alias_in: n_out}
    res = _pcall(wrapped, name=name, grid=(nt,),
                 in_specs=list(in_specs) + [_ANY] * n_cin, out_specs=tuple(out_specs) + (_ANY,),
                 out_shape=tuple(out_shape) + (comm.out_shape,),
                 scratch_shapes=list(scratch) + list(comm.scratch),
                 input_output_aliases=aliases,
                 compiler_params=_cparams(("arbitrary",)))(*args, *comm.ins)
    return res


def _pack_shard(parts):
    def body(*refs):
        out = refs[-1]
        for key, ref in zip(WEIGHT_KEYS, refs[:-1]):
            out[pl.ds(PK_OFF[key], N_ROWS[key]), :] = ref[...].astype(BF16)

    return _pcall(
        body, name="pack_shard",
        out_shape=_sds((PK_ROWS, D), BF16),
        in_specs=[pl.BlockSpec(memory_space=pltpu.VMEM)] * len(parts),
        out_specs=pl.BlockSpec(memory_space=pltpu.VMEM),
        compiler_params=_cparams(),
    )(*parts)


def _all_gather_first(shard, small):
    def body(x_ref, s_ref, wg_ref, sg_ref, send_w, recv_w, local_w, send_s, recv_s, local_s):
        start_w, finish_w = _weights_exchange(x_ref, wg_ref, send_w, recv_w, local_w, ("a_in", "a_out"))
        start_s, finish_s = _ag_exchange(s_ref, lambda p: sg_ref.at[_lin(p)], send_s, recv_s, local_s)
        start_s()
        start_w()
        finish_s()
        finish_w()

    sems = [pltpu.SemaphoreType.DMA((7,)), pltpu.SemaphoreType.DMA((7,)), pltpu.SemaphoreType.DMA]
    return _pcall(
        body, name="all_gather_first",
        out_shape=(_sds((NDEV, PK_ROWS, D), BF16), _sds((NDEV, SM_F32_ROWS, 128), F32)),
        in_specs=[_ANY, _ANY], out_specs=(_ANY, _ANY), scratch_shapes=sems + sems,
    )(shard, small)


def _pair_exchange(grads, keys):
    l0, rows = _span(LD_OFF, keys)

    def body(*refs):
        g_refs = refs[:len(keys)]
        got, send_sem, recv_sem = refs[len(keys):]
        x, y, c = _mesh_pos()
        sibling = (x, y, 1 - c)
        for ref, key in zip(g_refs, keys):
            for t in range(NCHIP):
                pltpu.make_async_remote_copy(
                    src_ref=ref.at[2 * t + (1 - c)], dst_ref=got.at[t, pl.ds(LD_OFF[key] - l0, N_ROWS[key]), :],
                    send_sem=send_sem, recv_sem=recv_sem, device_id=sibling, device_id_type=MESH).start()
        pltpu.make_async_remote_copy(src_ref=got, dst_ref=got, send_sem=send_sem, recv_sem=recv_sem,
                                     device_id=sibling, device_id_type=MESH).wait()

    return _pcall(
        body, name="pair_exchange_" + keys[0],
        out_shape=_sds((NCHIP, rows, D), BF16),
        in_specs=[_ANY] * len(keys), out_specs=_ANY,
        scratch_shapes=[pltpu.SemaphoreType.DMA, pltpu.SemaphoreType.DMA],
    )(*grads)


def _pair_add(grads, keys, got):
    l0, rows = _span(LD_OFF, keys)

    def body(c_ref, *refs):
        g_refs = refs[:len(keys)]
        got_ref, o_ref = refs[len(keys):]
        for ref, key in zip(g_refs, keys):
            sl = pl.ds(LD_OFF[key] - l0, N_ROWS[key])
            o_ref[sl, :] = (ref[...].astype(F32) + got_ref[sl, :].astype(F32)).astype(BF16)

    grid_spec = pltpu.PrefetchScalarGridSpec(
        num_scalar_prefetch=1, grid=(NCHIP,),
        in_specs=[pl.BlockSpec((None, N_ROWS[k], D), lambda t, c: (2 * t + c[0], 0, 0)) for k in keys]
        + [pl.BlockSpec((None, rows, D), lambda t, c: (t, 0, 0))],
        out_specs=pl.BlockSpec((None, rows, D), lambda t, c: (t, 0, 0)))
    core = lax.axis_index("c").astype(jnp.int32).reshape(1)
    return _pcall(
        body, name="pair_add_" + keys[0], grid_spec=grid_spec,
        out_shape=_sds((NCHIP, rows, D), BF16),
        compiler_params=_cparams(("arbitrary",)),
    )(core, *grads, got)


def _pair_reduce(grads, keys):
    return _pair_add(grads, keys, _pair_exchange(grads, keys))


def _exchange_last(psum, keys, land, small, repl):
    def body(p_ref, _, sm_ref, rp_ref, land_ref, sm_land, rp_land, s1, r1, s2, r2, s3, r3):
        start_w, finish_w = _chip_exchange(p_ref, land_ref, s1, r1, keys)
        start_s, finish_s = _all_to_all_f32(lambda d: sm_ref.at[d], sm_land, s2, r2)
        start_r, finish_r = _all_to_all_f32(lambda d: rp_ref.at[0], rp_land, s3, r3)
        start_w()
        start_s()
        start_r()
        finish_s()
        finish_r()
        finish_w()

    return _pcall(
        body, name="exchange_last",
        out_shape=(_sds((NCHIP, N_WROWS, D), BF16), _sds((NDEV, SM_F32_ROWS, 128), F32), _sds((NDEV, 8, D), F32)),
        in_specs=[_ANY] * 4, out_specs=(_ANY, _ANY, _ANY),
        input_output_aliases={1: 0},
        scratch_shapes=[pltpu.SemaphoreType.DMA((NCHIP,)), pltpu.SemaphoreType.DMA((NCHIP,))]
        + [pltpu.SemaphoreType.DMA((NDEV,))] * 4,
    )(psum, land, small, repl)


def _mixa_fwd(x, gam, cw, wg, tm, comm):
    T = x.shape[0]

    def body(x_ref, gam_ref, cw_ref, wg_ref, h_ref, n_ref, b_ref, c_ref, v_ref, cc_ref, y_ref,
             win, wout, buf, sems):
        @pl.when(pl.program_id(0) == 0)
        def _():
            _load_weights(wg_ref, [(win, "a_in"), (wout, "a_out")], sems)
            buf[pl.ds(0, HALO_A), :] = jnp.zeros((HALO_A, D), F32)

        xv = x_ref[...]
        nb = (xv * _rms_stat(xv) * gam_ref[...]).astype(BF16)
        n_ref[...] = nb
        bv = _dot_nt(nb, win[pl.ds(0, D), :])
        cval = _dot_nt(nb, win[pl.ds(D, D), :])
        vval = _dot_nt(nb, win[pl.ds(2 * D, D), :])
        cv = cval * vval
        buf[pl.ds(HALO_A, tm), :] = cv
        cc = cw_ref[pl.ds(KA - 1, 1), :] * cv
        for k in range(KA - 1):
            cc = cc + cw_ref[pl.ds(k, 1), :] * buf[pl.ds(HALO_A - (KA - 1) + k, tm), :]
        buf[pl.ds(0, HALO_A), :] = buf[pl.ds(tm, HALO_A), :]
        yb = (bv * cc).astype(BF16)
        b_ref[...] = bv.astype(BF16)
        c_ref[...] = cval.astype(BF16)
        v_ref[...] = vval.astype(BF16)
        cc_ref[...] = cc.astype(BF16)
        y_ref[...] = yb
        h_ref[...] = xv + _dot(yb, wout[...])

    act = _sds((T, D), BF16)
    return _hosted_call(
        body, "mixa_fwd", T // tm,
        in_specs=[_row_spec(tm, D), _full_spec((1, D)), _full_spec((KA, D))],
        out_specs=[_row_spec(tm, D) for _ in range(7)],
        out_shape=(_sds((T, D), F32),) + (act,) * 6,
        scratch=[pltpu.VMEM((3 * D, D), BF16), pltpu.VMEM((D, D), BF16),
                 pltpu.VMEM((HALO_A + tm, D), F32), pltpu.SemaphoreType.DMA((2 * NDEV,))],
        args=(x, gam, cw), comm=comm)


def _ffn_fwd(h, gam, wg, layer, tm, comm):
    T = h.shape[0]
    kg, ku, kd = "g%d" % layer, "u%d" % layer, "d%d" % layer

    def body(h_ref, gam_ref, wg_ref, o_ref, n_ref, g_ref, u_ref, gu_ref, wgt, wut, wd, sems):
        @pl.when(pl.program_id(0) == 0)
        def _():
            _load_weights(wg_ref, [(wgt, kg), (wut, ku), (wd, kd)], sems)

        hv = h_ref[...]
        nb = (hv * _rms_stat(hv) * gam_ref[...]).astype(BF16)
        n_ref[...] = nb
        out = hv
        for f in range(2):
            cols = pl.ds(f * FH, FH)
            g = _dot_nt(nb, wgt[cols, :])
            u = _dot_nt(nb, wut[cols, :])
            gu = (g * _sigmoid(g) * u).astype(BF16)
            g_ref[:, cols] = g.astype(BF16)
            u_ref[:, cols] = u.astype(BF16)
            gu_ref[:, cols] = gu
            out = out + _dot(gu, wd[cols, :])
        o_ref[...] = out

    pre = _sds((T, FF), BF16)
    in_specs = [_row_spec(tm, D), _full_spec((1, D))]
    args = (h, gam)
    if comm is None:
        in_specs, args = in_specs + [_ANY], args + (wg,)
    return _hosted_call(
        body, "ffn%d_fwd" % layer, T // tm,
        in_specs=in_specs,
        out_specs=[_row_spec(tm, D), _row_spec(tm, D), _row_spec(tm, FF), _row_spec(tm, FF), _row_spec(tm, FF)],
        out_shape=(_sds((T, D), F32), _sds((T, D), BF16), pre, pre, pre),
        scratch=[pltpu.VMEM((FF, D), BF16)] * 3 + [pltpu.SemaphoreType.DMA((3 * NDEV,))],
        args=args, comm=comm)


def _shifted_copies(buf, shf, tm):
    for r in range(1, 8):
        shf[r - 1] = buf[pl.ds(r, tm + HALO_B - 8), :]


def _broadcast_taps(vec_ref, wb):
    for k in range(KB):
        wb[k] = jnp.broadcast_to(vec_ref[pl.ds(SM_B_CONV + k, 1), :], (8, D))
    wb[KB] = jnp.broadcast_to(vec_ref[pl.ds(SM_B_BCONV, 1), :], (8, D))


def _taps_by_shift_residue(taps):
    groups = {}
    for k, shift in taps:
        q, r = divmod(shift, 8)
        groups.setdefault(r, []).append((k, q))
    return sorted(groups.items())


def _window(buf, shf, base, r, q0, n_groups, lanes):
    rows = pl.ds(base + 8 * q0, 8 * n_groups)
    v = buf[rows, lanes] if r == 0 else shf[r - 1, rows, lanes]
    return [v[8 * i:8 * i + 8] for i in range(n_groups)]


def _long_conv(buf, shf, wb, out_ref, tm, taps, bias_row):
    n_acc = CONV_ROWS // 8
    groups = _taps_by_shift_residue(taps)
    for col in range(D // 128):
        lanes = pl.ds(128 * col, 128)

        def rows(i, carry, lanes=lanes):
            base = i * CONV_ROWS
            init = jnp.zeros((8, 128), F32) if bias_row is None else wb[bias_row, :, lanes]
            accs = [init] * n_acc
            for r, lst in groups:
                q0, q1 = min(q for _, q in lst), max(q for _, q in lst)
                win = _window(buf, shf, base, r, q0, n_acc + q1 - q0, lanes)
                for k, q in lst:
                    wk = wb[k, :, lanes]
                    accs = [acc + wk * win[h + q - q0] for h, acc in enumerate(accs)]
            for h, acc in enumerate(accs):
                out_ref[pl.ds(base + 8 * h, 8), lanes] = acc
            return carry

        for i in range(tm // CONV_ROWS):
            rows(i, 0)


def _long_conv_grad_taps(buf, shf, x_ref, wacc, tm):
    n_acc = CONV_ROWS // 8
    groups = _taps_by_shift_residue([(KB - 1 - j, j) for j in range(KB)])
    for col in range(D // 128):
        lanes = pl.ds(128 * col, 128)

        def rows(i, carry, lanes=lanes):
            base = i * CONV_ROWS
            xv = x_ref[pl.ds(base, CONV_ROWS), lanes]
            xs = [xv[8 * h:8 * h + 8] for h in range(n_acc)]
            for r, lst in groups:
                q0, q1 = min(q for _, q in lst), max(q for _, q in lst)
                win = _window(buf, shf, base, r, q0, n_acc + q1 - q0, lanes)
                for k, q in lst:
                    prod = [x * win[h + q - q0] for h, x in enumerate(xs)]
                    wacc[k, :, lanes] += functools.reduce(lambda a, b: a + b, prod)
            return carry

        for i in range(tm // CONV_ROWS):
            rows(i, 0)


def _ln_stats(dc):
    mu = jnp.mean(dc, axis=-1, keepdims=True)
    xc = dc - mu
    rstd = lax.rsqrt(jnp.mean(xc * xc, axis=-1, keepdims=True) + LN_EPS)
    return xc * rstd, rstd


def _mixb_fwd(h, vecs, bias1, wg, tm, comm):
    T = h.shape[0]

    def body(h_ref, vec_ref, b1_ref, wg_ref, o_ref, n_ref, a_ref, g_ref, dc_ref, s_ref, w1, w2, buf, shf, wb, sems):
        @pl.when(pl.program_id(0) == 0)
        def _():
            _load_weights(wg_ref, [(w1, "b_pw1"), (w2, "b_pw2")], sems)
            buf[pl.ds(0, HALO_B), :] = jnp.zeros((HALO_B, D), F32)
            _broadcast_taps(vec_ref, wb)

        hv = h_ref[...]
        nb = (hv * _rms_stat(hv) * vec_ref[pl.ds(SM_B_NORM, 1), :]).astype(BF16)
        n_ref[...] = nb
        a = _dot_nt(nb, w1[pl.ds(0, D), :]) + b1_ref[:, pl.ds(0, D)]
        g = _dot_nt(nb, w1[pl.ds(D, D), :]) + b1_ref[:, pl.ds(D, D)]
        a_ref[...] = a.astype(BF16)
        g_ref[...] = g.astype(BF16)
        buf[pl.ds(HALO_B, tm), :] = a * _sigmoid(g)
        _shifted_copies(buf, shf, tm)

        _long_conv(buf, shf, wb, dc_ref, tm, [(k, HALO_B - (KB - 1) + k) for k in range(KB)], KB)
        buf[pl.ds(0, HALO_B), :] = buf[pl.ds(tm, HALO_B), :]
        xhat, _ = _ln_stats(dc_ref[...])
        ln = xhat * vec_ref[pl.ds(SM_LN_G, 1), :] + vec_ref[pl.ds(SM_LN_B, 1), :]
        s = (ln * _sigmoid(ln)).astype(BF16)
        s_ref[...] = s
        o_ref[...] = hv + _dot(s, w2[...]) + vec_ref[pl.ds(SM_B_PW2, 1), :]

    act = _sds((T, D), BF16)
    return _hosted_call(
        body, "mixb_fwd", T // tm,
        in_specs=[_row_spec(tm, D), _full_spec((SM_F32_ROWS, D)), _full_spec((1, 2 * D))],
        out_specs=[_row_spec(tm, D) for _ in range(6)],
        out_shape=(_sds((T, D), F32), act, act, act, _sds((T, D), F32), act),
        scratch=[pltpu.VMEM((2 * D, D), BF16), pltpu.VMEM((D, D), BF16),
                 pltpu.VMEM((HALO_B + tm, D), F32), pltpu.VMEM((7, HALO_B + tm - 8, D), F32),
                 pltpu.VMEM((KB + 1, 8, D), F32), pltpu.SemaphoreType.DMA((2 * NDEV,))],
        args=(h, vecs, bias1), comm=comm)


def _loss_head(h, tgt, gam, tm):
    T = h.shape[0]

    def body(h_ref, t_ref, gam_ref, dh_ref, st_ref):
        @pl.when(pl.program_id(0) == 0)
        def _():
            st_ref[...] = jnp.zeros((8, D), F32)

        hv = h_ref[...]
        gamma = gam_ref[...]
        r = _rms_stat(hv)
        err = hv * r * gamma - t_ref[...]
        dx, dgam = _rms_bwd(err * (1.0 / D), hv, r, gamma)
        dh_ref[...] = dx
        st_ref[pl.ds(0, 1), :] += dgam
        st_ref[pl.ds(1, 1), :] += (0.5 / D) * jnp.sum(err * err, axis=0, keepdims=True)

    return _pcall(
        body, name="loss_head", grid=(T // tm,),
        out_shape=(_sds((T, D), F32), _sds((8, D), F32)),
        in_specs=[_row_spec(tm, D), _row_spec(tm, D), _full_spec((1, D))],
        out_specs=(_row_spec(tm, D), _full_spec((8, D))),
        compiler_params=_cparams(("arbitrary",)),
    )(h, tgt, gam)


def _ffn_bwd_dx(dh, h, g, u, gam, wg, layer, tm, comm):
    T = h.shape[0]
    kg, ku, kd = "g%d" % layer, "u%d" % layer, "d%d" % layer

    def body(dh_ref, h_ref, g_ref, u_ref, gam_ref, wg_ref, o_ref, dg_ref, du_ref, st_ref, wgt, wut, wd, sems):
        @pl.when(pl.program_id(0) == 0)
        def _():
            _load_weights(wg_ref, [(wgt, kg), (wut, ku), (wd, kd)], sems)
            st_ref[...] = jnp.zeros((8, D), F32)

        dhv = dh_ref[...]
        dhb = dhv.astype(BF16)
        dn = jnp.zeros_like(dhv)
        for f in range(2):
            cols = pl.ds(f * FH, FH)
            dgu = _dot_nt(dhb, wd[cols, :])
            gv = g_ref[:, cols].astype(F32)
            sg = _sigmoid(gv)
            du = (dgu * gv * sg).astype(BF16)
            dg = (dgu * u_ref[:, cols].astype(F32) * (sg * (1.0 + gv * (1.0 - sg)))).astype(BF16)
            dg_ref[:, cols] = dg
            du_ref[:, cols] = du
            dn = dn + _dot(dg, wgt[cols, :]) + _dot(du, wut[cols, :])
        hv = h_ref[...]
        dx, dgam = _rms_bwd(dn, hv, _rms_stat(hv), gam_ref[...])
        o_ref[...] = dhv + dx
        st_ref[pl.ds(0, 1), :] += dgam

    pre = _sds((T, FF), BF16)
    return _hosted_call(
        body, "ffn%d_bwd_dx" % layer, T // tm,
        in_specs=[_row_spec(tm, D), _row_spec(tm, D), _row_spec(tm, FF), _row_spec(tm, FF),
                  _full_spec((1, D)), _ANY],
        out_specs=[_row_spec(tm, D), _row_spec(tm, FF), _row_spec(tm, FF), _full_spec((8, D))],
        out_shape=(_sds((T, D), F32), pre, pre, _sds((8, D), F32)),
        scratch=[pltpu.VMEM((FF, D), BF16)] * 3 + [pltpu.SemaphoreType.DMA((3 * NDEV,))],
        args=(dh, h, g, u, gam, wg), comm=comm)


def _grad_w(lhs, rhs, mc, name, tm):
    T, M = lhs.shape
    nt = T // tm

    def body(l_ref, r_ref, o_ref, acc):
        i = pl.program_id(1)

        @pl.when(i == 0)
        def _():
            acc[...] = jnp.zeros((mc, D), F32)

        acc[...] += _dot_tn(l_ref[...], r_ref[...].astype(BF16))

        @pl.when(i == nt - 1)
        def _():
            o_ref[...] = acc[...].astype(BF16)

    return _pcall(
        body, name=name, grid=(M // mc, nt),
        out_shape=_sds((M, D), BF16),
        in_specs=[pl.BlockSpec((tm, mc), lambda j, i: (i, j)), pl.BlockSpec((tm, D), lambda j, i: (i, 0))],
        out_specs=pl.BlockSpec((mc, D), lambda j, i: (j, 0)),
        scratch_shapes=[pltpu.VMEM((mc, D), F32)],
        compiler_params=_cparams(("arbitrary", "arbitrary")),
    )(lhs, rhs)


def _mixb_bwd(dh, h, a, g, dc, vecs, wg, tm, comm):
    T = h.shape[0]
    nt = T // tm

    def body(dh_ref, h_ref, a_ref, g_ref, dc_ref, vec_ref, wg_ref, o_ref, du_ref, st_ref, sb_ref,
             w1, w2, buf, shf, glu_s, dglu_s, wacc, wb, sems):
        @pl.when(pl.program_id(0) == 0)
        def _():
            _load_weights(wg_ref, [(w1, "b_pw1"), (w2, "b_pw2")], sems)
            buf[pl.ds(tm, HALO_B), :] = jnp.zeros((HALO_B, D), F32)
            wacc[...] = jnp.zeros((KB, 8, D), F32)
            _broadcast_taps(vec_ref, wb)
            st_ref[...] = jnp.zeros((SM_F32_ROWS, D), F32)
            sb_ref[...] = jnp.zeros((8, 2 * D), F32)

        def acc(row, val):
            st_ref[pl.ds(row, 1), :] += jnp.sum(val, axis=0, keepdims=True)

        dhv = dh_ref[...]
        acc(SM_B_PW2, dhv)
        ds = _dot_nt(dhv.astype(BF16), w2[...])
        xhat, rstd = _ln_stats(dc_ref[...])
        ln_g = vec_ref[pl.ds(SM_LN_G, 1), :]
        ln = xhat * ln_g + vec_ref[pl.ds(SM_LN_B, 1), :]
        sl = _sigmoid(ln)
        dln = ds * (sl * (1.0 + ln * (1.0 - sl)))
        acc(SM_LN_G, dln * xhat)
        acc(SM_LN_B, dln)
        dxh = dln * ln_g
        ddc = rstd * (dxh - jnp.mean(dxh, axis=-1, keepdims=True)
                      - xhat * jnp.mean(dxh * xhat, axis=-1, keepdims=True))
        acc(SM_B_BCONV, ddc)
        buf[pl.ds(0, tm), :] = ddc
        _shifted_copies(buf, shf, tm)
        av = a_ref[...].astype(F32)
        sg = _sigmoid(g_ref[...].astype(F32))
        glu_s[...] = av * sg

        _long_conv(buf, shf, wb, dglu_s, tm, [(KB - 1 - j, j) for j in range(KB)], None)
        _long_conv_grad_taps(buf, shf, glu_s, wacc, tm)
        buf[pl.ds(tm, HALO_B), :] = buf[pl.ds(0, HALO_B), :]
        dglu = dglu_s[...]
        da = dglu * sg
        dg = dglu * av * sg * (1.0 - sg)
        sb_ref[pl.ds(0, 1), pl.ds(0, D)] += jnp.sum(da, axis=0, keepdims=True)
        sb_ref[pl.ds(0, 1), pl.ds(D, D)] += jnp.sum(dg, axis=0, keepdims=True)
        dab, dgb = da.astype(BF16), dg.astype(BF16)
        du_ref[:, pl.ds(0, D)] = dab
        du_ref[:, pl.ds(D, D)] = dgb
        dn = _dot(dab, w1[pl.ds(0, D), :]) + _dot(dgb, w1[pl.ds(D, D), :])
        hv = h_ref[...]
        dx, dgam = _rms_bwd(dn, hv, _rms_stat(hv), vec_ref[pl.ds(SM_B_NORM, 1), :])
        o_ref[...] = dhv + dx
        st_ref[pl.ds(SM_B_NORM, 1), :] += dgam

        @pl.when(pl.program_id(0) == nt - 1)
        def _():
            st_ref[pl.ds(SM_B_CONV, KB), :] = jnp.sum(wacc[...], axis=1)

    rs = functools.partial(_row_spec, rev_nt=nt)
    return _hosted_call(
        body, "mixb_bwd", nt,
        in_specs=[rs(tm, D), rs(tm, D), rs(tm, D), rs(tm, D), rs(tm, D), _full_spec((SM_F32_ROWS, D)), _ANY],
        out_specs=[rs(tm, D), rs(tm, 2 * D), _full_spec((SM_F32_ROWS, D)), _full_spec((8, 2 * D))],
        out_shape=(_sds((T, D), F32), _sds((T, 2 * D), BF16), _sds((SM_F32_ROWS, D), F32), _sds((8, 2 * D), F32)),
        scratch=[pltpu.VMEM((2 * D, D), BF16), pltpu.VMEM((D, D), BF16),
                 pltpu.VMEM((tm + HALO_B, D), F32), pltpu.VMEM((7, tm + HALO_B - 8, D), F32),
                 pltpu.VMEM((tm, D), F32), pltpu.VMEM((tm, D), F32), pltpu.VMEM((KB, 8, D), F32),
                 pltpu.VMEM((KB + 1, 8, D), F32), pltpu.SemaphoreType.DMA((2 * NDEV,))],
        args=(dh, h, a, g, dc, vecs, wg), comm=comm)


def _mixa_bwd(dh, x, b, c, v, cc, gam, cw, wg, tm, comm):
    T = x.shape[0]
    nt = T // tm

    def body(dh_ref, x_ref, b_ref, c_ref, v_ref, cc_ref, gam_ref, cw_ref, wg_ref,
             o_ref, db_ref, st_ref, win, wout, buf, sems):
        @pl.when(pl.program_id(0) == 0)
        def _():
            _load_weights(wg_ref, [(win, "a_in"), (wout, "a_out")], sems)
            buf[pl.ds(tm, HALO_A), :] = jnp.zeros((HALO_A, D), F32)
            st_ref[...] = jnp.zeros((8, D), F32)

        dhv = dh_ref[...]
        dy = _dot_nt(dhv.astype(BF16), wout[...])
        cval = c_ref[...].astype(F32)
        vval = v_ref[...].astype(F32)
        d_b = (dy * cc_ref[...].astype(F32)).astype(BF16)
        buf[pl.ds(0, tm), :] = dy * b_ref[...].astype(F32)
        cv = cval * vval
        dcv = jnp.zeros((tm, D), F32)
        for j in range(KA):
            sh = buf[pl.ds(j, tm), :]
            k = KA - 1 - j
            dcv = dcv + cw_ref[pl.ds(k, 1), :] * sh
            st_ref[pl.ds(1 + k, 1), :] += jnp.sum(cv * sh, axis=0, keepdims=True)
        buf[pl.ds(tm, HALO_A), :] = buf[pl.ds(0, HALO_A), :]
        d_c = (dcv * vval).astype(BF16)
        d_v = (dcv * cval).astype(BF16)
        db_ref[:, pl.ds(0, D)] = d_b
        db_ref[:, pl.ds(D, D)] = d_c
        db_ref[:, pl.ds(2 * D, D)] = d_v
        dn = _dot(d_b, win[pl.ds(0, D), :]) + _dot(d_c, win[pl.ds(D, D), :]) + _dot(d_v, win[pl.ds(2 * D, D), :])
        xv = x_ref[...]
        dx, dgam = _rms_bwd(dn, xv, _rms_stat(xv), gam_ref[...])
        o_ref[...] = dhv + dx
        st_ref[pl.ds(0, 1), :] += dgam

    rs = functools.partial(_row_spec, rev_nt=nt)
    return _hosted_call(
        body, "mixa_bwd", nt,
        in_specs=[rs(tm, D) for _ in range(6)] + [_full_spec((1, D)), _full_spec((KA, D)), _ANY],
        out_specs=[rs(tm, D), rs(tm, 3 * D), _full_spec((8, D))],
        out_shape=(_sds((T, D), F32), _sds((T, 3 * D), BF16), _sds((8, D), F32)),
        scratch=[pltpu.VMEM((3 * D, D), BF16), pltpu.VMEM((D, D), BF16),
                 pltpu.VMEM((tm + HALO_A, D), F32), pltpu.SemaphoreType.DMA((2 * NDEV,))],
        args=(dh, x, b, c, v, cc, gam, cw, wg), comm=comm)


def _sum_weights(land):
    rows = N_WROWS // 4

    def body(l_ref, o_ref):
        acc = l_ref[0].astype(F32)
        for s in range(1, NCHIP):
            acc = acc + l_ref[s].astype(F32)
        o_ref[...] = acc

    return _pcall(
        body, name="sum_weight_grads", grid=(4,),
        out_shape=_sds((N_WROWS, D), F32),
        in_specs=[pl.BlockSpec((NCHIP, rows, D), lambda i: (0, i, 0))],
        out_specs=pl.BlockSpec((rows, D), lambda i: (i, 0)),
        compiler_params=_cparams(("arbitrary",)),
    )(land)


def _sum_small(sm, rp):
    def body(sm_ref, rp_ref, osm, orp, oloss):
        a = sm_ref[0]
        b = rp_ref[0]
        for s in range(1, NDEV):
            a = a + sm_ref[s]
            b = b + rp_ref[s]
        osm[...] = a
        orp[...] = b
        oloss[...] = jnp.zeros((8, 128), F32) + jnp.sum(b[4:5, :], axis=-1, keepdims=True)

    return _pcall(
        body, name="sum_small_grads",
        out_shape=(_sds((SM_F32_ROWS, 128), F32), _sds((8, D), F32), _sds((8, 128), F32)),
        in_specs=[pl.BlockSpec(memory_space=pltpu.VMEM)] * 2,
        out_specs=tuple(pl.BlockSpec(memory_space=pltpu.VMEM) for _ in range(3)),
        compiler_params=_cparams(),
    )(sm, rp)


def _adamw(w, g, m, v, name):
    rows, cols = w.shape
    tr = rows
    for cand in (256, 128, 64, 32, 16, 8):
        if rows % cand == 0:
            tr = cand
            break

    def body(w_ref, g_ref, m_ref, v_ref, od, om, ov):
        gv = g_ref[...]
        mn = B1 * m_ref[...] + (1.0 - B1) * gv
        vn = B2 * v_ref[...] + (1.0 - B2) * (gv * gv)
        m_hat = mn / (1.0 - B1 ** STEP)
        v_hat = vn / (1.0 - B2 ** STEP)
        od[...] = -LR * (m_hat / (jnp.sqrt(v_hat) + ADAM_EPS) + WD * w_ref[...])
        om[...] = mn
        ov[...] = vn

    spec = pl.BlockSpec((tr, cols), lambda i: (i, 0))
    shp = _sds((rows, cols), F32)
    return _pcall(
        body, name="adamw_" + name, grid=(rows // tr,),
        out_shape=(shp, shp, shp), in_specs=[spec] * 4, out_specs=(spec, spec, spec),
        compiler_params=_cparams(("arbitrary",)),
    )(w, g, m, v)


def _adam_math(w, g, m, v):
    mn = B1 * m + (1.0 - B1) * g
    vn = B2 * v + (1.0 - B2) * (g * g)
    m_hat = mn / (1.0 - B1 ** STEP)
    v_hat = vn / (1.0 - B2 ** STEP)
    return -LR * (m_hat / (jnp.sqrt(v_hat) + ADAM_EPS) + WD * w), mn, vn


SMALL_PARAMS = (("sm", SM_B_NORM, 1), ("sm", SM_B_PW1, 2), ("sm", SM_B_CONV, KB), ("sm", SM_B_BCONV, 1),
                ("sm", SM_LN_G, 1), ("sm", SM_LN_B, 1), ("sm", SM_B_PW2, 1), ("sm", SM_A_CONV, KA),
                ("rp", 0, 1), ("rp", 1, 2), ("rp", 3, 1))


def _adamw_small(g_small, g_repl, triples):
    n = len(SMALL_PARAMS)

    def body(*refs):
        gs_ref, gr_ref = refs[0], refs[1]
        ins = refs[2:2 + 3 * n]
        outs = refs[2 + 3 * n:]
        for i, (src, r0, nr) in enumerate(SMALL_PARAMS):
            w_ref, m_ref, v_ref = ins[3 * i:3 * i + 3]
            g = (gs_ref if src == "sm" else gr_ref)[pl.ds(r0, nr), :]
            if i == 1:
                g = jnp.concatenate([g[0:1], g[1:2]], axis=1)
            lead = (0,) if len(w_ref.shape) == 3 else ()
            idx = lead + (slice(None), slice(None))
            vals = (g,) + _adam_math(w_ref[idx], g, m_ref[idx], v_ref[idx])
            for o_ref, val in zip(outs[4 * i:4 * i + 4], vals):
                o_ref[idx] = val

    flat = [a for t in triples for a in t]
    out_shape = tuple(_sds(t[0].shape, F32) for t in triples for _ in range(4))
    vm = pl.BlockSpec(memory_space=pltpu.VMEM)
    res = _pcall(
        body, name="adamw_small", out_shape=out_shape,
        in_specs=[vm] * (2 + len(flat)), out_specs=tuple(vm for _ in out_shape),
        compiler_params=_cparams(),
    )(g_small, g_repl, *flat)
    return [tuple(res[4 * i:4 * i + 4]) for i in range(n)]


def _pack_small(b_norm, b_b_pw1, b_conv, b_b_conv, b_ln_g, b_ln_b, b_b_pw2, a_conv):
    rows = [b_norm.reshape(1, 128), b_b_pw1.reshape(2, 128), b_conv.reshape(KB, 128), b_b_conv.reshape(1, 128),
            b_ln_g.reshape(1, 128), b_ln_b.reshape(1, 128), b_b_pw2.reshape(1, 128), a_conv.reshape(KA, 128)]
    return jnp.concatenate(rows + [jnp.zeros((SM_F32_ROWS - SM_USED, 128), F32)], axis=0)


def kernel(x, a_norm, a_w_in, a_conv, a_w_out, b_norm, b_w_pw1, b_b_pw1, b_conv, b_b_conv, b_ln_g, b_ln_b, b_w_pw2, b_b_pw2, ffn_norm, ffn_w_gate, ffn_w_up, ffn_w_down, final_norm, loss_target, m_a_norm, m_a_w_in, m_a_conv, m_a_w_out, m_b_norm, m_b_w_pw1, m_b_b_pw1, m_b_conv, m_b_b_conv, m_b_ln_g, m_b_ln_b, m_b_w_pw2, m_b_b_pw2, m_ffn_norm, m_ffn_w_gate, m_ffn_w_up, m_ffn_w_down, m_final_norm, v_a_norm, v_a_w_in, v_a_conv, v_a_w_out, v_b_norm, v_b_w_pw1, v_b_b_pw1, v_b_conv, v_b_b_conv, v_b_ln_g, v_b_ln_b, v_b_w_pw2, v_b_b_pw2, v_ffn_norm, v_ffn_w_gate, v_ffn_w_up, v_ffn_w_down, v_final_norm):
    T = x.shape[1]
    tm = min(TM, T)
    tw = min(TM_DW, T)
    xs = x.reshape(T, D)
    tgt = loss_target.reshape(T, D)

    parts = [a_w_in[0].T, a_w_out[0], b_w_pw1[0].T, b_w_pw2[0],
             ffn_w_gate[0].T, ffn_w_up[0].T, ffn_w_down[0],
             ffn_w_gate[1].T, ffn_w_up[1].T, ffn_w_down[1]]
    small = _pack_small(b_norm, b_b_pw1, b_conv, b_b_conv, b_ln_g, b_ln_b, b_b_pw2, a_conv)
    shard = _pack_shard(parts)
    wg, sm_all = _all_gather_first(shard, small)
    vecs = sm_all.transpose(1, 0, 2).reshape(SM_F32_ROWS, D)
    bias1 = sm_all[:, SM_B_PW1:SM_B_PW1 + 2, :].reshape(1, 2 * D)
    cw_a = vecs[SM_A_CONV:SM_A_CONV + KA]
    fn0, fn1 = ffn_norm[0:1], ffn_norm[1:2]
    fin = final_norm.reshape(1, D)

    h1, n0, bq, cq, vq, ccq, yq, wg = _mixa_fwd(xs, a_norm, cw_a, wg, tm, _ag_comm(shard, wg, ("g0", "u0", "d0")))
    h2, n1, g0, u0, gu0, wg = _ffn_fwd(h1, fn0, wg, 0, tm, _ag_comm(shard, wg, ("b_pw1", "b_pw2")))
    h3, n2, aq, gq, dcq, sq, wg = _mixb_fwd(h2, vecs, bias1, wg, tm, _ag_comm(shard, wg, ("g1", "u1", "d1")))
    h4, n3, g1, u1, gu1 = _ffn_fwd(h3, fn1, wg, 1, tm, None)
    dh4, st_fin = _loss_head(h4, tgt, fin, tm)

    def by_dest(gw, key):
        return gw.reshape(NDEV, N_ROWS[key], D)

    gw_d1 = by_dest(_grad_w(gu1, dh4, FH, "grad_down1", tw), "d1")
    dh3, dg1, du1, st_f1 = _ffn_bwd_dx(dh4, h3, g1, u1, fn1, wg, 1, tm, None)
    gw_g1 = by_dest(_grad_w(dg1, n3, FH, "grad_gate1", tw), "g1")
    gw_u1 = by_dest(_grad_w(du1, n3, FH, "grad_up1", tw), "u1")
    keys = ("g1", "u1", "d1")
    dh2, dub, st_b, st_b1, land = _mixb_bwd(
        dh3, h2, aq, gq, dcq, vecs, wg, tm, _rs_comm(_pair_reduce([gw_g1, gw_u1, gw_d1], keys), keys, None))
    gw_pw1 = by_dest(_grad_w(dub, n2, D, "grad_pw1", tw), "b_pw1")
    gw_pw2 = by_dest(_grad_w(sq, dh3, D, "grad_pw2", tw), "b_pw2")
    gw_d0 = by_dest(_grad_w(gu0, dh2, FH, "grad_down0", tw), "d0")
    keys = ("b_pw1", "b_pw2", "d0")
    dh1, dg0, du0, st_f0, land = _ffn_bwd_dx(
        dh2, h1, g0, u0, fn0, wg, 0, tm, _rs_comm(_pair_reduce([gw_pw1, gw_pw2, gw_d0], keys), keys, land))
    gw_g0 = by_dest(_grad_w(dg0, n1, FH, "grad_gate0", tw), "g0")
    gw_u0 = by_dest(_grad_w(du0, n1, FH, "grad_up0", tw), "u0")
    gw_out = by_dest(_grad_w(yq, dh1, D, "grad_out", tw), "a_out")
    keys = ("g0", "u0", "a_out")
    dx, dbcv, st_a, land = _mixa_bwd(
        dh1, xs, bq, cq, vq, ccq, a_norm, cw_a, wg, tm,
        _rs_comm(_pair_reduce([gw_g0, gw_u0, gw_out], keys), keys, land))
    gw_in = by_dest(_grad_w(dbcv, n0, D, "grad_in", tw), "a_in")

    st_small = st_b.at[SM_A_CONV:SM_A_CONV + KA].set(st_a[1:1 + KA])
    sm_dest = st_small.reshape(SM_F32_ROWS, NDEV, 128).transpose(1, 0, 2)
    sm_dest = sm_dest.at[:, SM_B_PW1:SM_B_PW1 + 2, :].set(st_b1[0].reshape(NDEV, 2, 128))
    repl = jnp.concatenate([st_a[0:1], st_f0[0:1], st_f1[0:1], st_fin[0:1], st_fin[1:2],
                            jnp.zeros((3, D), F32)], axis=0)[None]
    land, sm_land, rp_land = _exchange_last(_pair_reduce([gw_in], ("a_in",)), ("a_in",), land, sm_dest, repl)

    gsum = _sum_weights(land)
    g_small, g_repl, loss8 = _sum_small(sm_land, rp_land)
    loss = loss8[0, 0]

    def rows_of(k):
        return gsum[LD_OFF[k]:LD_OFF[k] + N_ROWS[k]]

    g_a_w_in = rows_of("a_in").T[None]
    g_a_w_out = rows_of("a_out")[None]
    g_b_w_pw1 = rows_of("b_pw1").T[None]
    g_b_w_pw2 = rows_of("b_pw2")[None]
    g_gate = jnp.stack([rows_of("g0").T, rows_of("g1").T])
    g_up = jnp.stack([rows_of("u0").T, rows_of("u1").T])
    g_down = jnp.stack([rows_of("d0"), rows_of("d1")])

    def adam(name, w, g, m, v):
        d, nm, nv = _adamw(w.reshape(-1, w.shape[-1]), g.reshape(-1, w.shape[-1]),
                           m.reshape(-1, w.shape[-1]), v.reshape(-1, w.shape[-1]), name)
        return d.reshape(w.shape), nm.reshape(w.shape), nv.reshape(w.shape)

    big = [("a_w_in", a_w_in, g_a_w_in, m_a_w_in, v_a_w_in),
           ("a_w_out", a_w_out, g_a_w_out, m_a_w_out, v_a_w_out),
           ("b_w_pw1", b_w_pw1, g_b_w_pw1, m_b_w_pw1, v_b_w_pw1),
           ("b_w_pw2", b_w_pw2, g_b_w_pw2, m_b_w_pw2, v_b_w_pw2),
           ("ffn_w_gate", ffn_w_gate, g_gate, m_ffn_w_gate, v_ffn_w_gate),
           ("ffn_w_up", ffn_w_up, g_up, m_ffn_w_up, v_ffn_w_up),
           ("ffn_w_down", ffn_w_down, g_down, m_ffn_w_down, v_ffn_w_down)]
    res = {name: (g,) + adam(name, w, g, m, v) for name, w, g, m, v in big}

    small_names = ["b_norm", "b_b_pw1", "b_conv", "b_b_conv", "b_ln_g", "b_ln_b", "b_b_pw2", "a_conv",
                   "a_norm", "ffn_norm", "final_norm"]
    triples = [(b_norm, m_b_norm, v_b_norm), (b_b_pw1, m_b_b_pw1, v_b_b_pw1), (b_conv, m_b_conv, v_b_conv),
               (b_b_conv, m_b_b_conv, v_b_b_conv), (b_ln_g, m_b_ln_g, v_b_ln_g), (b_ln_b, m_b_ln_b, v_b_ln_b),
               (b_b_pw2, m_b_b_pw2, v_b_b_pw2), (a_conv, m_a_conv, v_a_conv), (a_norm, m_a_norm, v_a_norm),
               (ffn_norm, m_ffn_norm, v_ffn_norm),
               (fin, m_final_norm.reshape(1, D), v_final_norm.reshape(1, D))]
    for name, quad in zip(small_names, _adamw_small(g_small, g_repl, triples)):
        res[name] = quad
    res["final_norm"] = tuple(a.reshape(D) for a in res["final_norm"])

    order = ["a_norm", "a_w_in", "a_conv", "a_w_out", "b_norm", "b_w_pw1", "b_b_pw1", "b_conv", "b_b_conv",
             "b_ln_g", "b_ln_b", "b_w_pw2", "b_b_pw2", "ffn_norm", "ffn_w_gate", "ffn_w_up", "ffn_w_down",
             "final_norm"]
    out = [loss, dx.reshape(1, T, D)]
    for j in range(4):
        out += [res[k][j] for k in order]
    return tuple(out)
```

```python
import functools

import jax
import jax.numpy as jnp
from jax import lax
from jax.experimental import pallas as pl
from jax.experimental.pallas import tpu as pltpu

F32 = jnp.float32
BF16 = jnp.bfloat16
MESH = pl.DeviceIdType.MESH

D = 1024
FF = 2816
FH = FF // 2
NDEV = 8
KA = 3
KB = 31
HALO_A = 8
HALO_B = 32
CONV_ROWS = 64
RMS_EPS = 1e-6
LN_EPS = 1e-5
LR, B1, B2, ADAM_EPS, WD, STEP = 0.001, 0.9, 0.999, 1e-08, 0.01, 10

TM = 256
TM_DW = 1024
VMEM_LIMIT = 56 * 1024 * 1024

N_ROWS = {"a_in": 384, "a_out": 128, "b_pw1": 256, "b_pw2": 128, "g0": 352, "u0": 352, "d0": 352,
          "g1": 352, "u1": 352, "d1": 352}
WEIGHT_KEYS = ("a_in", "a_out", "b_pw1", "b_pw2", "g0", "u0", "d0", "g1", "u1", "d1")
NCHIP = 4


def _offsets(order):
    off, o = {}, 0
    for k in order:
        off[k] = o
        o += N_ROWS[k]
    return off, o


PK_ORDER = ("a_in", "a_out", "g0", "u0", "d0", "b_pw1", "b_pw2", "g1", "u1", "d1")
PK_OFF, PK_ROWS = _offsets(PK_ORDER)
LD_ORDER = ("g1", "u1", "d1", "b_pw1", "b_pw2", "d0", "g0", "u0", "a_out", "a_in")
LD_OFF, N_WROWS = _offsets(LD_ORDER)


def _span(off, keys):
    return off[keys[0]], sum(N_ROWS[k] for k in keys)


SM_B_NORM, SM_B_PW1, SM_B_CONV, SM_B_BCONV, SM_LN_G, SM_LN_B, SM_B_PW2, SM_A_CONV = 0, 1, 3, 34, 35, 36, 37, 38
SM_USED = 41
SM_F32_ROWS = 64


def _pcall(body, **kw):
    return pl.pallas_call(body, **kw)


def _cparams(sem=None):
    return pltpu.CompilerParams(dimension_semantics=sem, vmem_limit_bytes=VMEM_LIMIT)


def _dot(a, b):
    return jnp.dot(a, b, preferred_element_type=F32)


def _dot_nt(a, b):
    return lax.dot_general(a, b, (((1,), (1,)), ((), ())), preferred_element_type=F32)


def _dot_tn(a, b):
    return lax.dot_general(a, b, (((0,), (0,)), ((), ())), preferred_element_type=F32)


def _sigmoid(v):
    return 1.0 / (1.0 + jnp.exp(-v))


def _rms_stat(x):
    return lax.rsqrt(jnp.mean(x * x, axis=-1, keepdims=True) + RMS_EPS)


def _rms_bwd(dn, x, r, gamma):
    dng = dn * gamma
    dx = r * dng - x * (r * r * r) * jnp.mean(dng * x, axis=-1, keepdims=True)
    return dx, jnp.sum(dn * x * r, axis=0, keepdims=True)


def _load_weights(wg_ref, plan, sems):
    copies = []
    for j, (dst, key) in enumerate(plan):
        off, n = PK_OFF[key], N_ROWS[key]
        for d in range(NDEV):
            copies.append(pltpu.make_async_copy(
                wg_ref.at[d, pl.ds(off, n), :], dst.at[pl.ds(d * n, n), :], sems.at[j * NDEV + d]))
    for cp in copies:
        cp.start()
    for cp in copies:
        cp.wait()


_ANY = pl.BlockSpec(memory_space=pl.ANY)


def _row_spec(tm, width, rev_nt=None):
    if rev_nt is None:
        return pl.BlockSpec((tm, width), lambda i: (i, 0))
    return pl.BlockSpec((tm, width), lambda i: (rev_nt - 1 - i, 0))


def _full_spec(shape):
    return pl.BlockSpec(shape, lambda *_: (0,) * len(shape))


def _sds(shape, dtype):
    return jax.ShapeDtypeStruct(shape, dtype)


def _mesh_pos():
    return lax.axis_index("x"), lax.axis_index("y"), lax.axis_index("c")


def _lin(p):
    return 4 * p[0] + 2 * p[1] + p[2]


def _ag_exchange(src, slot, send_sems, recv_sems, local_sem):
    x, y, c = _mesh_pos()
    me, sibling = (x, y, c), (x, y, 1 - c)
    chips = [(1 - x, y), (x, 1 - y), (1 - x, 1 - y)]

    def copy(k, block, to, own=False):
        return pltpu.make_async_remote_copy(
            src_ref=src if own else slot(block), dst_ref=slot(block),
            send_sem=send_sems.at[k], recv_sem=recv_sems.at[k], device_id=to, device_id_type=MESH)

    mine = pltpu.make_async_copy(src, slot(me), local_sem)
    first = [copy(0, me, sibling, own=True)]
    first += [copy(1 + j, me, (*chip, c), own=True) for j, chip in enumerate(chips)]
    passed = [copy(4 + j, (*chip, c), sibling) for j, chip in enumerate(chips)]

    def start():
        mine.start()
        for cp in first:
            cp.start()

    def finish():
        for j, chip in enumerate(chips):
            copy(1 + j, (*chip, c), me).wait_recv()
            passed[j].start()
        copy(0, sibling, me).wait_recv()
        for j, chip in enumerate(chips):
            copy(4 + j, (*chip, 1 - c), me).wait_recv()
        for cp in first + passed:
            cp.wait_send()
        mine.wait()

    return start, finish


def _chip_exchange(p_ref, land_ref, send_sems, recv_sems, keys):
    x, y, c = _mesh_pos()
    l0, rows = _span(LD_OFF, keys)
    dst = land_ref.at[2 * x + y, pl.ds(l0, rows), :]
    peers = [((x + (j >> 1)) % 2, (y + (j & 1)) % 2) for j in range(NCHIP)]

    def copy(j):
        tx, ty = peers[j]
        if j == 0:
            return pltpu.make_async_copy(p_ref.at[2 * tx + ty], dst, send_sems.at[0])
        return pltpu.make_async_remote_copy(
            src_ref=p_ref.at[2 * tx + ty], dst_ref=dst, send_sem=send_sems.at[j], recv_sem=recv_sems.at[j],
            device_id=(tx, ty, c), device_id_type=MESH)

    def start():
        for j in range(NCHIP):
            copy(j).start()

    def finish():
        for j in range(NCHIP):
            copy(j).wait()

    return start, finish


def _all_to_all_f32(src_for, dst_ref, send_sems, recv_sems):
    x, y, c = _mesh_pos()
    dst = dst_ref.at[_lin((x, y, c))]
    peers = [((x + ((j >> 2) & 1)) % 2, (y + ((j >> 1) & 1)) % 2, (c + (j & 1)) % 2) for j in range(NDEV)]

    def copy(j):
        if j == 0:
            return pltpu.make_async_copy(src_for(_lin(peers[0])), dst, send_sems.at[0])
        return pltpu.make_async_remote_copy(
            src_ref=src_for(_lin(peers[j])), dst_ref=dst, send_sem=send_sems.at[j], recv_sem=recv_sems.at[j],
            device_id=peers[j], device_id_type=MESH)

    def start():
        for j in range(NDEV):
            copy(j).start()

    def finish():
        for j in range(NDEV):
            copy(j).wait()

    return start, finish


class _Comm:
    def __init__(self, ins, alias_in, out_shape, scratch, make, gives_wg):
        self.ins, self.alias_in, self.out_shape = ins, alias_in, out_shape
        self.scratch, self.make, self.gives_wg = scratch, make, gives_wg


def _ag_comm(shard, wg, keys):
    def make(c_ins, c_out, sc):
        return _weights_exchange(c_ins[0], c_out, sc[0], sc[1], sc[2], keys)

    return _Comm([shard, wg], 1, _sds(wg.shape, BF16),
                 [pltpu.SemaphoreType.DMA((7,)), pltpu.SemaphoreType.DMA((7,)), pltpu.SemaphoreType.DMA],
                 make, True)


def _weights_exchange(shard_ref, wg_ref, send_sems, recv_sems, local_sem, keys):
    r0, nr = _span(PK_OFF, keys)
    return _ag_exchange(shard_ref.at[pl.ds(r0, nr), :], lambda p: wg_ref.at[_lin(p), pl.ds(r0, nr), :],
                        send_sems, recv_sems, local_sem)


def _rs_comm(psum, keys, land):
    def make(c_ins, c_out, sc):
        return _chip_exchange(c_ins[0], c_out, sc[0], sc[1], keys)

    ins = [psum] + ([] if land is None else [land])
    return _Comm(ins, None if land is None else 1, _sds((NCHIP, N_WROWS, D), BF16),
                 [pltpu.SemaphoreType.DMA((NCHIP,)), pltpu.SemaphoreType.DMA((NCHIP,))], make, False)


def _hosted_call(body, name, nt, in_specs, out_specs, out_shape, scratch, args, comm):
    if comm is None:
        return _pcall(body, name=name, grid=(nt,), in_specs=in_specs, out_specs=tuple(out_specs),
                      out_shape=tuple(out_shape), scratch_shapes=scratch,
                      compiler_params=_cparams(("arbitrary",)))(*args)
    n_in, n_out, n_sc, n_cin = len(in_specs), len(out_specs), len(scratch), len(comm.ins)

    def wrapped(*refs):
        ins = refs[:n_in]
        c_ins = refs[n_in:n_in + n_cin]
        outs = refs[n_in + n_cin:n_in + n_cin + n_out]
        c_out = refs[n_in + n_cin + n_out]
        sc = refs[n_in + n_cin + n_out + 1:n_in + n_cin + n_out + 1 + n_sc]
        c_sc = refs[n_in + n_cin + n_out + 1 + n_sc:]
        start, finish = comm.make(c_ins, c_out, c_sc)

        @pl.when(pl.program_id(0) == 0)
        def _():
            start()

        if comm.gives_wg:
            body(*ins, c_out, *outs, *sc)
        else:
            body(*ins, *outs, *sc)

        @pl.when(pl.program_id(0) == nt - 1)
        def _():
            finish()

    aliases = {} if comm.alias_in is None else {n_in + comm.alias_in: n_out}
    res = _pcall(wrapped, name=name, grid=(nt,),
                 in_specs=list(in_specs) + [_ANY] * n_cin, out_specs=tuple(out_specs) + (_ANY,),
                 out_shape=tuple(out_shape) + (comm.out_shape,),
                 scratch_shapes=list(scratch) + list(comm.scratch),
                 input_output_aliases=aliases,
                 compiler_params=_cparams(("arbitrary",)))(*args, *comm.ins)
    return res


def _pack_shard(parts, smalls):
    def body(*refs):
        out, sm = refs[-2], refs[-1]
        for key, ref in zip(WEIGHT_KEYS, refs[:len(parts)]):
            out[pl.ds(PK_OFF[key], N_ROWS[key]), :] = ref[...].astype(BF16)
        b_norm, b_b_pw1, b_conv, b_b_conv, b_ln_g, b_ln_b, b_b_pw2, a_conv = refs[len(parts):-2]
        sm[...] = jnp.zeros((SM_F32_ROWS, 128), F32)
        sm[pl.ds(SM_B_PW1, 1), :] = b_b_pw1[:, pl.ds(0, 128)]
        sm[pl.ds(SM_B_PW1 + 1, 1), :] = b_b_pw1[:, pl.ds(128, 128)]
        sm[pl.ds(SM_B_CONV, KB), :] = b_conv[0]
        sm[pl.ds(SM_A_CONV, KA), :] = a_conv[0]
        for row, ref in ((SM_B_NORM, b_norm), (SM_B_BCONV, b_b_conv), (SM_LN_G, b_ln_g), (SM_LN_B, b_ln_b),
                         (SM_B_PW2, b_b_pw2)):
            sm[pl.ds(row, 1), :] = ref[...]

    vm = pl.BlockSpec(memory_space=pltpu.VMEM)
    return _pcall(
        body, name="pack_shard",
        out_shape=(_sds((PK_ROWS, D), BF16), _sds((SM_F32_ROWS, 128), F32)),
        in_specs=[vm] * (len(parts) + len(smalls)), out_specs=(vm, vm),
        compiler_params=_cparams(),
    )(*parts, *smalls)


def _all_gather_first(shard, small):
    def body(x_ref, s_ref, wg_ref, sg_ref, send_w, recv_w, local_w, send_s, recv_s, local_s):
        start_w, finish_w = _weights_exchange(x_ref, wg_ref, send_w, recv_w, local_w, ("a_in", "a_out"))
        start_s, finish_s = _ag_exchange(s_ref, lambda p: sg_ref.at[_lin(p)], send_s, recv_s, local_s)
        start_s()
        start_w()
        finish_s()
        finish_w()

    sems = [pltpu.SemaphoreType.DMA((7,)), pltpu.SemaphoreType.DMA((7,)), pltpu.SemaphoreType.DMA]
    return _pcall(
        body, name="all_gather_first",
        out_shape=(_sds((NDEV, PK_ROWS, D), BF16), _sds((NDEV, SM_F32_ROWS, 128), F32)),
        in_specs=[_ANY, _ANY], out_specs=(_ANY, _ANY), scratch_shapes=sems + sems,
    )(shard, small)


def _pair_exchange(grads, keys):
    l0, rows = _span(LD_OFF, keys)

    def body(*refs):
        g_refs = refs[:len(keys)]
        got, send_sem, recv_sem = refs[len(keys):]
        x, y, c = _mesh_pos()
        sibling = (x, y, 1 - c)
        for ref, key in zip(g_refs, keys):
            for t in range(NCHIP):
                pltpu.make_async_remote_copy(
                    src_ref=ref.at[2 * t + (1 - c)], dst_ref=got.at[t, pl.ds(LD_OFF[key] - l0, N_ROWS[key]), :],
                    send_sem=send_sem, recv_sem=recv_sem, device_id=sibling, device_id_type=MESH).start()
        pltpu.make_async_remote_copy(src_ref=got, dst_ref=got, send_sem=send_sem, recv_sem=recv_sem,
                                     device_id=sibling, device_id_type=MESH).wait()

    return _pcall(
        body, name="pair_exchange_" + keys[0],
        out_shape=_sds((NCHIP, rows, D), BF16),
        in_specs=[_ANY] * len(keys), out_specs=_ANY,
        scratch_shapes=[pltpu.SemaphoreType.DMA, pltpu.SemaphoreType.DMA],
    )(*grads)


def _pair_add(grads, keys, got):
    l0, rows = _span(LD_OFF, keys)

    def body(c_ref, *refs):
        g_refs = refs[:len(keys)]
        got_ref, o_ref = refs[len(keys):]
        for ref, key in zip(g_refs, keys):
            sl = pl.ds(LD_OFF[key] - l0, N_ROWS[key])
            o_ref[sl, :] = (ref[...].astype(F32) + got_ref[sl, :].astype(F32)).astype(BF16)

    grid_spec = pltpu.PrefetchScalarGridSpec(
        num_scalar_prefetch=1, grid=(NCHIP,),
        in_specs=[pl.BlockSpec((None, N_ROWS[k], D), lambda t, c: (2 * t + c[0], 0, 0)) for k in keys]
        + [pl.BlockSpec((None, rows, D), lambda t, c: (t, 0, 0))],
        out_specs=pl.BlockSpec((None, rows, D), lambda t, c: (t, 0, 0)))
    core = lax.axis_index("c").astype(jnp.int32).reshape(1)
    return _pcall(
        body, name="pair_add_" + keys[0], grid_spec=grid_spec,
        out_shape=_sds((NCHIP, rows, D), BF16),
        compiler_params=_cparams(("arbitrary",)),
    )(core, *grads, got)


def _pair_reduce(grads, keys):
    return _pair_add(grads, keys, _pair_exchange(grads, keys))


def _exchange_last(psum, keys, land, small, repl):
    def body(p_ref, _, sm_ref, rp_ref, land_ref, sm_land, rp_land, s1, r1, s2, r2, s3, r3):
        start_w, finish_w = _chip_exchange(p_ref, land_ref, s1, r1, keys)
        start_s, finish_s = _all_to_all_f32(lambda d: sm_ref.at[d], sm_land, s2, r2)
        start_r, finish_r = _all_to_all_f32(lambda d: rp_ref.at[0], rp_land, s3, r3)
        start_w()
        start_s()
        start_r()
        finish_s()
        finish_r()
        finish_w()

    return _pcall(
        body, name="exchange_last",
        out_shape=(_sds((NCHIP, N_WROWS, D), BF16), _sds((NDEV, SM_F32_ROWS, 128), F32), _sds((NDEV, 8, D), F32)),
        in_specs=[_ANY] * 4, out_specs=(_ANY, _ANY, _ANY),
        input_output_aliases={1: 0},
        scratch_shapes=[pltpu.SemaphoreType.DMA((NCHIP,)), pltpu.SemaphoreType.DMA((NCHIP,))]
        + [pltpu.SemaphoreType.DMA((NDEV,))] * 4,
    )(psum, land, small, repl)


def _mixa_fwd(x, gam, cw, wg, tm, comm):
    T = x.shape[0]

    def body(x_ref, gam_ref, cw_ref, wg_ref, h_ref, n_ref, b_ref, c_ref, v_ref, cc_ref, y_ref,
             win, wout, buf, sems):
        @pl.when(pl.program_id(0) == 0)
        def _():
            _load_weights(wg_ref, [(win, "a_in"), (wout, "a_out")], sems)
            buf[pl.ds(0, HALO_A), :] = jnp.zeros((HALO_A, D), F32)

        xv = x_ref[...]
        nb = (xv * _rms_stat(xv) * gam_ref[...]).astype(BF16)
        n_ref[...] = nb
        bv = _dot_nt(nb, win[pl.ds(0, D), :])
        cval = _dot_nt(nb, win[pl.ds(D, D), :])
        vval = _dot_nt(nb, win[pl.ds(2 * D, D), :])
        cv = cval * vval
        buf[pl.ds(HALO_A, tm), :] = cv
        cc = cw_ref[pl.ds(KA - 1, 1), :] * cv
        for k in range(KA - 1):
            cc = cc + cw_ref[pl.ds(k, 1), :] * buf[pl.ds(HALO_A - (KA - 1) + k, tm), :]
        buf[pl.ds(0, HALO_A), :] = buf[pl.ds(tm, HALO_A), :]
        yb = (bv * cc).astype(BF16)
        b_ref[...] = bv.astype(BF16)
        c_ref[...] = cval.astype(BF16)
        v_ref[...] = vval.astype(BF16)
        cc_ref[...] = cc.astype(BF16)
        y_ref[...] = yb
        h_ref[...] = xv + _dot(yb, wout[...])

    act = _sds((T, D), BF16)
    return _hosted_call(
        body, "mixa_fwd", T // tm,
        in_specs=[_row_spec(tm, D), _full_spec((1, D)), _full_spec((KA, D))],
        out_specs=[_row_spec(tm, D) for _ in range(7)],
        out_shape=(_sds((T, D), F32),) + (act,) * 6,
        scratch=[pltpu.VMEM((3 * D, D), BF16), pltpu.VMEM((D, D), BF16),
                 pltpu.VMEM((HALO_A + tm, D), F32), pltpu.SemaphoreType.DMA((2 * NDEV,))],
        args=(x, gam, cw), comm=comm)


def _ffn_fwd(h, gam, wg, layer, tm, comm):
    T = h.shape[0]
    kg, ku, kd = "g%d" % layer, "u%d" % layer, "d%d" % layer

    def body(h_ref, gam_ref, wg_ref, o_ref, n_ref, g_ref, u_ref, gu_ref, wgt, wut, wd, sems):
        @pl.when(pl.program_id(0) == 0)
        def _():
            _load_weights(wg_ref, [(wgt, kg), (wut, ku), (wd, kd)], sems)

        hv = h_ref[...]
        nb = (hv * _rms_stat(hv) * gam_ref[...]).astype(BF16)
        n_ref[...] = nb
        out = hv
        for f in range(2):
            cols = pl.ds(f * FH, FH)
            g = _dot_nt(nb, wgt[cols, :])
            u = _dot_nt(nb, wut[cols, :])
            gu = (g * _sigmoid(g) * u).astype(BF16)
            g_ref[:, cols] = g.astype(BF16)
            u_ref[:, cols] = u.astype(BF16)
            gu_ref[:, cols] = gu
            out = out + _dot(gu, wd[cols, :])
        o_ref[...] = out

    pre = _sds((T, FF), BF16)
    in_specs = [_row_spec(tm, D), _full_spec((1, D))]
    args = (h, gam)
    if comm is None:
        in_specs, args = in_specs + [_ANY], args + (wg,)
    return _hosted_call(
        body, "ffn%d_fwd" % layer, T // tm,
        in_specs=in_specs,
        out_specs=[_row_spec(tm, D), _row_spec(tm, D), _row_spec(tm, FF), _row_spec(tm, FF), _row_spec(tm, FF)],
        out_shape=(_sds((T, D), F32), _sds((T, D), BF16), pre, pre, pre),
        scratch=[pltpu.VMEM((FF, D), BF16)] * 3 + [pltpu.SemaphoreType.DMA((3 * NDEV,))],
        args=args, comm=comm)


def _shifted_copies(buf, shf, tm):
    for r in range(1, 8):
        shf[r - 1] = buf[pl.ds(r, tm + HALO_B - 8), :]


def _broadcast_taps(vec_ref, wb):
    for k in range(KB):
        wb[k] = jnp.broadcast_to(vec_ref[pl.ds(SM_B_CONV + k, 1), :], (8, D))
    wb[KB] = jnp.broadcast_to(vec_ref[pl.ds(SM_B_BCONV, 1), :], (8, D))


def _taps_by_shift_residue(taps):
    groups = {}
    for k, shift in taps:
        q, r = divmod(shift, 8)
        groups.setdefault(r, []).append((k, q))
    return sorted(groups.items())


def _window(buf, shf, base, r, q0, n_groups, lanes):
    rows = pl.ds(base + 8 * q0, 8 * n_groups)
    v = buf[rows, lanes] if r == 0 else shf[r - 1, rows, lanes]
    return [v[8 * i:8 * i + 8] for i in range(n_groups)]


def _long_conv(buf, shf, wb, out_ref, tm, taps, bias_row):
    n_acc = CONV_ROWS // 8
    groups = _taps_by_shift_residue(taps)
    for col in range(D // 128):
        lanes = pl.ds(128 * col, 128)

        def rows(i, carry, lanes=lanes):
            base = i * CONV_ROWS
            init = jnp.zeros((8, 128), F32) if bias_row is None else wb[bias_row, :, lanes]
            accs = [init] * n_acc
            for r, lst in groups:
                q0, q1 = min(q for _, q in lst), max(q for _, q in lst)
                win = _window(buf, shf, base, r, q0, n_acc + q1 - q0, lanes)
                for k, q in lst:
                    wk = wb[k, :, lanes]
                    accs = [acc + wk * win[h + q - q0] for h, acc in enumerate(accs)]
            for h, acc in enumerate(accs):
                out_ref[pl.ds(base + 8 * h, 8), lanes] = acc
            return carry

        for i in range(tm // CONV_ROWS):
            rows(i, 0)


def _long_conv_grad_taps(buf, shf, x_ref, wacc, tm):
    n_acc = CONV_ROWS // 8
    groups = _taps_by_shift_residue([(KB - 1 - j, j) for j in range(KB)])
    for col in range(D // 128):
        lanes = pl.ds(128 * col, 128)

        def rows(i, carry, lanes=lanes):
            base = i * CONV_ROWS
            xv = x_ref[pl.ds(base, CONV_ROWS), lanes]
            xs = [xv[8 * h:8 * h + 8] for h in range(n_acc)]
            for r, lst in groups:
                q0, q1 = min(q for _, q in lst), max(q for _, q in lst)
                win = _window(buf, shf, base, r, q0, n_acc + q1 - q0, lanes)
                for k, q in lst:
                    prod = [x * win[h + q - q0] for h, x in enumerate(xs)]
                    wacc[k, :, lanes] += functools.reduce(lambda a, b: a + b, prod)
            return carry

        for i in range(tm // CONV_ROWS):
            rows(i, 0)


def _ln_stats(dc):
    mu = jnp.mean(dc, axis=-1, keepdims=True)
    xc = dc - mu
    rstd = lax.rsqrt(jnp.mean(xc * xc, axis=-1, keepdims=True) + LN_EPS)
    return xc * rstd, rstd


def _mixb_fwd(h, vecs, bias1, wg, tm, comm):
    T = h.shape[0]

    def body(h_ref, vec_ref, b1_ref, wg_ref, o_ref, n_ref, a_ref, g_ref, dc_ref, s_ref, w1, w2, buf, shf, wb, sems):
        @pl.when(pl.program_id(0) == 0)
        def _():
            _load_weights(wg_ref, [(w1, "b_pw1"), (w2, "b_pw2")], sems)
            buf[pl.ds(0, HALO_B), :] = jnp.zeros((HALO_B, D), F32)
            _broadcast_taps(vec_ref, wb)

        hv = h_ref[...]
        nb = (hv * _rms_stat(hv) * vec_ref[pl.ds(SM_B_NORM, 1), :]).astype(BF16)
        n_ref[...] = nb
        a = _dot_nt(nb, w1[pl.ds(0, D), :]) + b1_ref[:, pl.ds(0, D)]
        g = _dot_nt(nb, w1[pl.ds(D, D), :]) + b1_ref[:, pl.ds(D, D)]
        a_ref[...] = a.astype(BF16)
        g_ref[...] = g.astype(BF16)
        buf[pl.ds(HALO_B, tm), :] = a * _sigmoid(g)
        _shifted_copies(buf, shf, tm)

        _long_conv(buf, shf, wb, dc_ref, tm, [(k, HALO_B - (KB - 1) + k) for k in range(KB)], KB)
        buf[pl.ds(0, HALO_B), :] = buf[pl.ds(tm, HALO_B), :]
        xhat, _ = _ln_stats(dc_ref[...])
        ln = xhat * vec_ref[pl.ds(SM_LN_G, 1), :] + vec_ref[pl.ds(SM_LN_B, 1), :]
        s = (ln * _sigmoid(ln)).astype(BF16)
        s_ref[...] = s
        o_ref[...] = hv + _dot(s, w2[...]) + vec_ref[pl.ds(SM_B_PW2, 1), :]

    act = _sds((T, D), BF16)
    return _hosted_call(
        body, "mixb_fwd", T // tm,
        in_specs=[_row_spec(tm, D), _full_spec((SM_F32_ROWS, D)), _full_spec((1, 2 * D))],
        out_specs=[_row_spec(tm, D) for _ in range(6)],
        out_shape=(_sds((T, D), F32), act, act, act, _sds((T, D), F32), act),
        scratch=[pltpu.VMEM((2 * D, D), BF16), pltpu.VMEM((D, D), BF16),
                 pltpu.VMEM((HALO_B + tm, D), F32), pltpu.VMEM((7, HALO_B + tm - 8, D), F32),
                 pltpu.VMEM((KB + 1, 8, D), F32), pltpu.SemaphoreType.DMA((2 * NDEV,))],
        args=(h, vecs, bias1), comm=comm)


def _loss_head(h, tgt, gam, tm):
    T = h.shape[0]

    def body(h_ref, t_ref, gam_ref, dh_ref, st_ref):
        @pl.when(pl.program_id(0) == 0)
        def _():
            st_ref[...] = jnp.zeros((8, D), F32)

        hv = h_ref[...]
        gamma = gam_ref[...]
        r = _rms_stat(hv)
        err = hv * r * gamma - t_ref[...]
        dx, dgam = _rms_bwd(err * (1.0 / D), hv, r, gamma)
        dh_ref[...] = dx
        st_ref[pl.ds(0, 1), :] += dgam
        st_ref[pl.ds(1, 1), :] += (0.5 / D) * jnp.sum(err * err, axis=0, keepdims=True)

    return _pcall(
        body, name="loss_head", grid=(T // tm,),
        out_shape=(_sds((T, D), F32), _sds((8, D), F32)),
        in_specs=[_row_spec(tm, D), _row_spec(tm, D), _full_spec((1, D))],
        out_specs=(_row_spec(tm, D), _full_spec((8, D))),
        compiler_params=_cparams(("arbitrary",)),
    )(h, tgt, gam)


def _ffn_bwd_dx(dh, h, g, u, gam, wg, layer, tm, comm):
    T = h.shape[0]
    kg, ku, kd = "g%d" % layer, "u%d" % layer, "d%d" % layer

    def body(dh_ref, h_ref, g_ref, u_ref, gam_ref, wg_ref, o_ref, dg_ref, du_ref, st_ref, wgt, wut, wd, sems):
        @pl.when(pl.program_id(0) == 0)
        def _():
            _load_weights(wg_ref, [(wgt, kg), (wut, ku), (wd, kd)], sems)
            st_ref[...] = jnp.zeros((8, D), F32)

        dhv = dh_ref[...]
        dhb = dhv.astype(BF16)
        dn = jnp.zeros_like(dhv)
        for f in range(2):
            cols = pl.ds(f * FH, FH)
            dgu = _dot_nt(dhb, wd[cols, :])
            gv = g_ref[:, cols].astype(F32)
            sg = _sigmoid(gv)
            du = (dgu * gv * sg).astype(BF16)
            dg = (dgu * u_ref[:, cols].astype(F32) * (sg * (1.0 + gv * (1.0 - sg)))).astype(BF16)
            dg_ref[:, cols] = dg
            du_ref[:, cols] = du
            dn = dn + _dot(dg, wgt[cols, :]) + _dot(du, wut[cols, :])
        hv = h_ref[...]
        dx, dgam = _rms_bwd(dn, hv, _rms_stat(hv), gam_ref[...])
        o_ref[...] = dhv + dx
        st_ref[pl.ds(0, 1), :] += dgam

    pre = _sds((T, FF), BF16)
    return _hosted_call(
        body, "ffn%d_bwd_dx" % layer, T // tm,
        in_specs=[_row_spec(tm, D), _row_spec(tm, D), _row_spec(tm, FF), _row_spec(tm, FF),
                  _full_spec((1, D)), _ANY],
        out_specs=[_row_spec(tm, D), _row_spec(tm, FF), _row_spec(tm, FF), _full_spec((8, D))],
        out_shape=(_sds((T, D), F32), pre, pre, _sds((8, D), F32)),
        scratch=[pltpu.VMEM((FF, D), BF16)] * 3 + [pltpu.SemaphoreType.DMA((3 * NDEV,))],
        args=(dh, h, g, u, gam, wg), comm=comm)


def _grad_w(lhs, rhs, mc, name, tm):
    T, M = lhs.shape
    nt = T // tm

    def body(l_ref, r_ref, o_ref, acc):
        i = pl.program_id(1)

        @pl.when(i == 0)
        def _():
            acc[...] = jnp.zeros((mc, D), F32)

        acc[...] += _dot_tn(l_ref[...], r_ref[...].astype(BF16))

        @pl.when(i == nt - 1)
        def _():
            o_ref[...] = acc[...].astype(BF16)

    return _pcall(
        body, name=name, grid=(M // mc, nt),
        out_shape=_sds((M, D), BF16),
        in_specs=[pl.BlockSpec((tm, mc), lambda j, i: (i, j)), pl.BlockSpec((tm, D), lambda j, i: (i, 0))],
        out_specs=pl.BlockSpec((mc, D), lambda j, i: (j, 0)),
        scratch_shapes=[pltpu.VMEM((mc, D), F32)],
        compiler_params=_cparams(("arbitrary", "arbitrary")),
    )(lhs, rhs)


def _mixb_bwd(dh, h, a, g, dc, vecs, wg, tm, comm):
    T = h.shape[0]
    nt = T // tm

    def body(dh_ref, h_ref, a_ref, g_ref, dc_ref, vec_ref, wg_ref, o_ref, du_ref, st_ref, sb_ref,
             w1, w2, buf, shf, glu_s, dglu_s, wacc, wb, sems):
        @pl.when(pl.program_id(0) == 0)
        def _():
            _load_weights(wg_ref, [(w1, "b_pw1"), (w2, "b_pw2")], sems)
            buf[pl.ds(tm, HALO_B), :] = jnp.zeros((HALO_B, D), F32)
            wacc[...] = jnp.zeros((KB, 8, D), F32)
            _broadcast_taps(vec_ref, wb)
            st_ref[...] = jnp.zeros((SM_F32_ROWS, D), F32)
            sb_ref[...] = jnp.zeros((8, 2 * D), F32)

        def acc(row, val):
            st_ref[pl.ds(row, 1), :] += jnp.sum(val, axis=0, keepdims=True)

        dhv = dh_ref[...]
        acc(SM_B_PW2, dhv)
        ds = _dot_nt(dhv.astype(BF16), w2[...])
        xhat, rstd = _ln_stats(dc_ref[...])
        ln_g = vec_ref[pl.ds(SM_LN_G, 1), :]
        ln = xhat * ln_g + vec_ref[pl.ds(SM_LN_B, 1), :]
        sl = _sigmoid(ln)
        dln = ds * (sl * (1.0 + ln * (1.0 - sl)))
        acc(SM_LN_G, dln * xhat)
        acc(SM_LN_B, dln)
        dxh = dln * ln_g
        ddc = rstd * (dxh - jnp.mean(dxh, axis=-1, keepdims=True)
                      - xhat * jnp.mean(dxh * xhat, axis=-1, keepdims=True))
        acc(SM_B_BCONV, ddc)
        buf[pl.ds(0, tm), :] = ddc
        _shifted_copies(buf, shf, tm)
        av = a_ref[...].astype(F32)
        sg = _sigmoid(g_ref[...].astype(F32))
        glu_s[...] = av * sg

        _long_conv(buf, shf, wb, dglu_s, tm, [(KB - 1 - j, j) for j in range(KB)], None)
        _long_conv_grad_taps(buf, shf, glu_s, wacc, tm)
        buf[pl.ds(tm, HALO_B), :] = buf[pl.ds(0, HALO_B), :]
        dglu = dglu_s[...]
        da = dglu * sg
        dg = dglu * av * sg * (1.0 - sg)
        sb_ref[pl.ds(0, 1), pl.ds(0, D)] += jnp.sum(da, axis=0, keepdims=True)
        sb_ref[pl.ds(0, 1), pl.ds(D, D)] += jnp.sum(dg, axis=0, keepdims=True)
        dab, dgb = da.astype(BF16), dg.astype(BF16)
        du_ref[:, pl.ds(0, D)] = dab
        du_ref[:, pl.ds(D, D)] = dgb
        dn = _dot(dab, w1[pl.ds(0, D), :]) + _dot(dgb, w1[pl.ds(D, D), :])
        hv = h_ref[...]
        dx, dgam = _rms_bwd(dn, hv, _rms_stat(hv), vec_ref[pl.ds(SM_B_NORM, 1), :])
        o_ref[...] = dhv + dx
        st_ref[pl.ds(SM_B_NORM, 1), :] += dgam

        @pl.when(pl.program_id(0) == nt - 1)
        def _():
            st_ref[pl.ds(SM_B_CONV, KB), :] = jnp.sum(wacc[...], axis=1)

    rs = functools.partial(_row_spec, rev_nt=nt)
    return _hosted_call(
        body, "mixb_bwd", nt,
        in_specs=[rs(tm, D), rs(tm, D), rs(tm, D), rs(tm, D), rs(tm, D), _full_spec((SM_F32_ROWS, D)), _ANY],
        out_specs=[rs(tm, D), rs(tm, 2 * D), _full_spec((SM_F32_ROWS, D)), _full_spec((8, 2 * D))],
        out_shape=(_sds((T, D), F32), _sds((T, 2 * D), BF16), _sds((SM_F32_ROWS, D), F32), _sds((8, 2 * D), F32)),
        scratch=[pltpu.VMEM((2 * D, D), BF16), pltpu.VMEM((D, D), BF16),
                 pltpu.VMEM((tm + HALO_B, D), F32), pltpu.VMEM((7, tm + HALO_B - 8, D), F32),
                 pltpu.VMEM((tm, D), F32), pltpu.VMEM((tm, D), F32), pltpu.VMEM((KB, 8, D), F32),
                 pltpu.VMEM((KB + 1, 8, D), F32), pltpu.SemaphoreType.DMA((2 * NDEV,))],
        args=(dh, h, a, g, dc, vecs, wg), comm=comm)


def _mixa_bwd(dh, x, b, c, v, cc, gam, cw, wg, tm, comm):
    T = x.shape[0]
    nt = T // tm

    def body(dh_ref, x_ref, b_ref, c_ref, v_ref, cc_ref, gam_ref, cw_ref, wg_ref,
             o_ref, db_ref, st_ref, win, wout, buf, sems):
        @pl.when(pl.program_id(0) == 0)
        def _():
            _load_weights(wg_ref, [(win, "a_in"), (wout, "a_out")], sems)
            buf[pl.ds(tm, HALO_A), :] = jnp.zeros((HALO_A, D), F32)
            st_ref[...] = jnp.zeros((8, D), F32)

        dhv = dh_ref[...]
        dy = _dot_nt(dhv.astype(BF16), wout[...])
        cval = c_ref[...].astype(F32)
        vval = v_ref[...].astype(F32)
        d_b = (dy * cc_ref[...].astype(F32)).astype(BF16)
        buf[pl.ds(0, tm), :] = dy * b_ref[...].astype(F32)
        cv = cval * vval
        dcv = jnp.zeros((tm, D), F32)
        for j in range(KA):
            sh = buf[pl.ds(j, tm), :]
            k = KA - 1 - j
            dcv = dcv + cw_ref[pl.ds(k, 1), :] * sh
            st_ref[pl.ds(1 + k, 1), :] += jnp.sum(cv * sh, axis=0, keepdims=True)
        buf[pl.ds(tm, HALO_A), :] = buf[pl.ds(0, HALO_A), :]
        d_c = (dcv * vval).astype(BF16)
        d_v = (dcv * cval).astype(BF16)
        db_ref[:, pl.ds(0, D)] = d_b
        db_ref[:, pl.ds(D, D)] = d_c
        db_ref[:, pl.ds(2 * D, D)] = d_v
        dn = _dot(d_b, win[pl.ds(0, D), :]) + _dot(d_c, win[pl.ds(D, D), :]) + _dot(d_v, win[pl.ds(2 * D, D), :])
        xv = x_ref[...]
        dx, dgam = _rms_bwd(dn, xv, _rms_stat(xv), gam_ref[...])
        o_ref[...] = dhv + dx
        st_ref[pl.ds(0, 1), :] += dgam

    rs = functools.partial(_row_spec, rev_nt=nt)
    return _hosted_call(
        body, "mixa_bwd", nt,
        in_specs=[rs(tm, D) for _ in range(6)] + [_full_spec((1, D)), _full_spec((KA, D)), _ANY],
        out_specs=[rs(tm, D), rs(tm, 3 * D), _full_spec((8, D))],
        out_shape=(_sds((T, D), F32), _sds((T, 3 * D), BF16), _sds((8, D), F32)),
        scratch=[pltpu.VMEM((3 * D, D), BF16), pltpu.VMEM((D, D), BF16),
                 pltpu.VMEM((tm + HALO_A, D), F32), pltpu.SemaphoreType.DMA((2 * NDEV,))],
        args=(dh, x, b, c, v, cc, gam, cw, wg), comm=comm)


def _sum_small(sm, rp):
    def body(sm_ref, rp_ref, osm, orp, oloss):
        a = sm_ref[0]
        b = rp_ref[0]
        for s in range(1, NDEV):
            a = a + sm_ref[s]
            b = b + rp_ref[s]
        osm[...] = a
        orp[...] = b
        oloss[...] = jnp.zeros((8, 128), F32) + jnp.sum(b[4:5, :], axis=-1, keepdims=True)

    return _pcall(
        body, name="sum_small_grads",
        out_shape=(_sds((SM_F32_ROWS, 128), F32), _sds((8, D), F32), _sds((8, 128), F32)),
        in_specs=[pl.BlockSpec(memory_space=pltpu.VMEM)] * 2,
        out_specs=tuple(pl.BlockSpec(memory_space=pltpu.VMEM) for _ in range(3)),
        compiler_params=_cparams(),
    )(sm, rp)


def _adam_math(w, g, m, v):
    mn = B1 * m + (1.0 - B1) * g
    vn = B2 * v + (1.0 - B2) * (g * g)
    m_hat = mn / (1.0 - B1 ** STEP)
    v_hat = vn / (1.0 - B2 ** STEP)
    return -LR * (m_hat / (jnp.sqrt(v_hat) + ADAM_EPS) + WD * w), mn, vn


def _finish_weight(land, keys, transposed, w, m, v, name):
    layers, rows, cols = w.shape
    n = N_ROWS[keys[0]]
    n_pad = -(-n // 128) * 128 if transposed else n

    def body(land_ref, w_ref, m_ref, v_ref, og, od, om, ov, buf, sem):
        off = LD_OFF[keys[0]]
        if layers == 2:
            off = jnp.where(pl.program_id(0) == 0, off, LD_OFF[keys[1]])
        cp = pltpu.make_async_copy(land_ref.at[:, pl.ds(pl.multiple_of(off, 32), n), :], buf, sem)
        cp.start()
        cp.wait()
        g = buf[0].astype(F32)
        for s in range(1, NCHIP):
            g = g + buf[s].astype(F32)
        if transposed:
            if n_pad != n:
                g = jnp.concatenate([g, jnp.zeros((n_pad - n, D), F32)], axis=0)
            g = g.T[:, :n]
        d, mn, vn = _adam_math(w_ref[...], g, m_ref[...], v_ref[...])
        og[...] = g
        od[...] = d
        om[...] = mn
        ov[...] = vn

    spec = pl.BlockSpec((None, rows, cols), lambda l: (l, 0, 0))
    shp = _sds(w.shape, F32)
    return _pcall(
        body, name="finish_" + name, grid=(layers,),
        out_shape=(shp,) * 4, in_specs=[_ANY, spec, spec, spec], out_specs=(spec,) * 4,
        scratch_shapes=[pltpu.VMEM((NCHIP, n, D), BF16), pltpu.SemaphoreType.DMA],
        compiler_params=_cparams(("arbitrary",)),
    )(land, w, m, v)


SMALL_PARAMS = (("sm", SM_B_NORM, 1), ("sm", SM_B_PW1, 2), ("sm", SM_B_CONV, KB), ("sm", SM_B_BCONV, 1),
                ("sm", SM_LN_G, 1), ("sm", SM_LN_B, 1), ("sm", SM_B_PW2, 1), ("sm", SM_A_CONV, KA),
                ("rp", 0, 1), ("rp", 1, 2), ("rp", 3, 1))


def _adamw_small(g_small, g_repl, triples):
    n = len(SMALL_PARAMS)

    def body(*refs):
        gs_ref, gr_ref = refs[0], refs[1]
        ins = refs[2:2 + 3 * n]
        outs = refs[2 + 3 * n:]
        for i, (src, r0, nr) in enumerate(SMALL_PARAMS):
            w_ref, m_ref, v_ref = ins[3 * i:3 * i + 3]
            g = (gs_ref if src == "sm" else gr_ref)[pl.ds(r0, nr), :]
            if i == 1:
                g = jnp.concatenate([g[0:1], g[1:2]], axis=1)
            lead = (0,) if len(w_ref.shape) == 3 else ()
            idx = lead + (slice(None), slice(None))
            vals = (g,) + _adam_math(w_ref[idx], g, m_ref[idx], v_ref[idx])
            for o_ref, val in zip(outs[4 * i:4 * i + 4], vals):
                o_ref[idx] = val

    flat = [a for t in triples for a in t]
    out_shape = tuple(_sds(t[0].shape, F32) for t in triples for _ in range(4))
    vm = pl.BlockSpec(memory_space=pltpu.VMEM)
    res = _pcall(
        body, name="adamw_small", out_shape=out_shape,
        in_specs=[vm] * (2 + len(flat)), out_specs=tuple(vm for _ in out_shape),
        compiler_params=_cparams(),
    )(g_small, g_repl, *flat)
    return [tuple(res[4 * i:4 * i + 4]) for i in range(n)]


def kernel(x, a_norm, a_w_in, a_conv, a_w_out, b_norm, b_w_pw1, b_b_pw1, b_conv, b_b_conv, b_ln_g, b_ln_b, b_w_pw2, b_b_pw2, ffn_norm, ffn_w_gate, ffn_w_up, ffn_w_down, final_norm, loss_target, m_a_norm, m_a_w_in, m_a_conv, m_a_w_out, m_b_norm, m_b_w_pw1, m_b_b_pw1, m_b_conv, m_b_b_conv, m_b_ln_g, m_b_ln_b, m_b_w_pw2, m_b_b_pw2, m_ffn_norm, m_ffn_w_gate, m_ffn_w_up, m_ffn_w_down, m_final_norm, v_a_norm, v_a_w_in, v_a_conv, v_a_w_out, v_b_norm, v_b_w_pw1, v_b_b_pw1, v_b_conv, v_b_b_conv, v_b_ln_g, v_b_ln_b, v_b_w_pw2, v_b_b_pw2, v_ffn_norm, v_ffn_w_gate, v_ffn_w_up, v_ffn_w_down, v_final_norm):
    T = x.shape[1]
    tm = min(TM, T)
    tw = min(TM_DW, T)
    xs = x.reshape(T, D)
    tgt = loss_target.reshape(T, D)

    parts = [a_w_in[0].T, a_w_out[0], b_w_pw1[0].T, b_w_pw2[0],
             ffn_w_gate[0].T, ffn_w_up[0].T, ffn_w_down[0],
             ffn_w_gate[1].T, ffn_w_up[1].T, ffn_w_down[1]]
    shard, small = _pack_shard(parts, (b_norm, b_b_pw1, b_conv, b_b_conv, b_ln_g, b_ln_b, b_b_pw2, a_conv))
    wg, sm_all = _all_gather_first(shard, small)
    vecs = sm_all.transpose(1, 0, 2).reshape(SM_F32_ROWS, D)
    bias1 = sm_all[:, SM_B_PW1:SM_B_PW1 + 2, :].reshape(1, 2 * D)
    cw_a = vecs[SM_A_CONV:SM_A_CONV + KA]
    fn0, fn1 = ffn_norm[0:1], ffn_norm[1:2]
    fin = final_norm.reshape(1, D)

    h1, n0, bq, cq, vq, ccq, yq, wg = _mixa_fwd(xs, a_norm, cw_a, wg, tm, _ag_comm(shard, wg, ("g0", "u0", "d0")))
    h2, n1, g0, u0, gu0, wg = _ffn_fwd(h1, fn0, wg, 0, tm, _ag_comm(shard, wg, ("b_pw1", "b_pw2")))
    h3, n2, aq, gq, dcq, sq, wg = _mixb_fwd(h2, vecs, bias1, wg, tm, _ag_comm(shard, wg, ("g1", "u1", "d1")))
    h4, n3, g1, u1, gu1 = _ffn_fwd(h3, fn1, wg, 1, tm, None)
    dh4, st_fin = _loss_head(h4, tgt, fin, tm)

    def by_dest(gw, key):
        return gw.reshape(NDEV, N_ROWS[key], D)

    gw_d1 = by_dest(_grad_w(gu1, dh4, FH, "grad_down1", tw), "d1")
    dh3, dg1, du1, st_f1 = _ffn_bwd_dx(dh4, h3, g1, u1, fn1, wg, 1, tm, None)
    gw_g1 = by_dest(_grad_w(dg1, n3, FH, "grad_gate1", tw), "g1")
    gw_u1 = by_dest(_grad_w(du1, n3, FH, "grad_up1", tw), "u1")
    keys = ("g1", "u1", "d1")
    dh2, dub, st_b, st_b1, land = _mixb_bwd(
        dh3, h2, aq, gq, dcq, vecs, wg, tm, _rs_comm(_pair_reduce([gw_g1, gw_u1, gw_d1], keys), keys, None))
    gw_pw1 = by_dest(_grad_w(dub, n2, D, "grad_pw1", tw), "b_pw1")
    gw_pw2 = by_dest(_grad_w(sq, dh3, D, "grad_pw2", tw), "b_pw2")
    gw_d0 = by_dest(_grad_w(gu0, dh2, FH, "grad_down0", tw), "d0")
    keys = ("b_pw1", "b_pw2", "d0")
    dh1, dg0, du0, st_f0, land = _ffn_bwd_dx(
        dh2, h1, g0, u0, fn0, wg, 0, tm, _rs_comm(_pair_reduce([gw_pw1, gw_pw2, gw_d0], keys), keys, land))
    gw_g0 = by_dest(_grad_w(dg0, n1, FH, "grad_gate0", tw), "g0")
    gw_u0 = by_dest(_grad_w(du0, n1, FH, "grad_up0", tw), "u0")
    gw_out = by_dest(_grad_w(yq, dh1, D, "grad_out", tw), "a_out")
    keys = ("g0", "u0", "a_out")
    dx, dbcv, st_a, land = _mixa_bwd(
        dh1, xs, bq, cq, vq, ccq, a_norm, cw_a, wg, tm,
        _rs_comm(_pair_reduce([gw_g0, gw_u0, gw_out], keys), keys, land))
    gw_in = by_dest(_grad_w(dbcv, n0, D, "grad_in", tw), "a_in")

    st_small = st_b.at[SM_A_CONV:SM_A_CONV + KA].set(st_a[1:1 + KA])
    sm_dest = st_small.reshape(SM_F32_ROWS, NDEV, 128).transpose(1, 0, 2)
    sm_dest = sm_dest.at[:, SM_B_PW1:SM_B_PW1 + 2, :].set(st_b1[0].reshape(NDEV, 2, 128))
    repl = jnp.concatenate([st_a[0:1], st_f0[0:1], st_f1[0:1], st_fin[0:1], st_fin[1:2],
                            jnp.zeros((3, D), F32)], axis=0)[None]
    land, sm_land, rp_land = _exchange_last(_pair_reduce([gw_in], ("a_in",)), ("a_in",), land, sm_dest, repl)

    g_small, g_repl, loss8 = _sum_small(sm_land, rp_land)
    loss = loss8[0, 0]

    big = [("a_w_in", ("a_in",), True, a_w_in, m_a_w_in, v_a_w_in),
           ("a_w_out", ("a_out",), False, a_w_out, m_a_w_out, v_a_w_out),
           ("b_w_pw1", ("b_pw1",), True, b_w_pw1, m_b_w_pw1, v_b_w_pw1),
           ("b_w_pw2", ("b_pw2",), False, b_w_pw2, m_b_w_pw2, v_b_w_pw2),
           ("ffn_w_gate", ("g0", "g1"), True, ffn_w_gate, m_ffn_w_gate, v_ffn_w_gate),
           ("ffn_w_up", ("u0", "u1"), True, ffn_w_up, m_ffn_w_up, v_ffn_w_up),
           ("ffn_w_down", ("d0", "d1"), False, ffn_w_down, m_ffn_w_down, v_ffn_w_down)]
    res = {name: _finish_weight(land, keys, transposed, w, m, v, name) for name, keys, transposed, w, m, v in big}

    small_names = ["b_norm", "b_b_pw1", "b_conv", "b_b_conv", "b_ln_g", "b_ln_b", "b_b_pw2", "a_conv",
                   "a_norm", "ffn_norm", "final_norm"]
    triples = [(b_norm, m_b_norm, v_b_norm), (b_b_pw1, m_b_b_pw1, v_b_b_pw1), (b_conv, m_b_conv, v_b_conv),
               (b_b_conv, m_b_b_conv, v_b_b_conv), (b_ln_g, m_b_ln_g, v_b_ln_g), (b_ln_b, m_b_ln_b, v_b_ln_b),
               (b_b_pw2, m_b_b_pw2, v_b_b_pw2), (a_conv, m_a_conv, v_a_conv), (a_norm, m_a_norm, v_a_norm),
               (ffn_norm, m_ffn_norm, v_ffn_norm),
               (fin, m_final_norm.reshape(1, D), v_final_norm.reshape(1, D))]
    for name, quad in zip(small_names, _adamw_small(g_small, g_repl, triples)):
        res[name] = quad
    res["final_norm"] = tuple(a.reshape(D) for a in res["final_norm"])

    order = ["a_norm", "a_w_in", "a_conv", "a_w_out", "b_norm", "b_w_pw1", "b_b_pw1", "b_conv", "b_b_conv",
             "b_ln_g", "b_ln_b", "b_w_pw2", "b_b_pw2", "ffn_norm", "ffn_w_gate", "ffn_w_up", "ffn_w_down",
             "final_norm"]
    out = [loss, dx.reshape(1, T, D)]
    for j in range(4):
        out += [res[k][j] for k in order]
    return tuple(out)
```

```python
import functools

import jax
import jax.numpy as jnp
from jax import lax
from jax.experimental import pallas as pl
from jax.experimental.pallas import tpu as pltpu

F32 = jnp.float32
BF16 = jnp.bfloat16
MESH = pl.DeviceIdType.MESH

D = 1024
FF = 2816
FH = FF // 2
NDEV = 8
KA = 3
KB = 31
HALO_A = 8
HALO_B = 32
CONV_ROWS = 64
RMS_EPS = 1e-6
LN_EPS = 1e-5
LR, B1, B2, ADAM_EPS, WD, STEP = 0.001, 0.9, 0.999, 1e-08, 0.01, 10

TM = 256
TM_DW = 1024
VMEM_LIMIT = 56 * 1024 * 1024

N_ROWS = {"a_in": 384, "a_out": 128, "b_pw1": 256, "b_pw2": 128, "g0": 352, "u0": 352, "d0": 352,
          "g1": 352, "u1": 352, "d1": 352}
WEIGHT_KEYS = ("a_in", "a_out", "b_pw1", "b_pw2", "g0", "u0", "d0", "g1", "u1", "d1")
NCHIP = 4


def _offsets(order):
    off, o = {}, 0
    for k in order:
        off[k] = o
        o += N_ROWS[k]
    return off, o


PK_ORDER = ("a_in", "a_out", "g0", "u0", "d0", "b_pw1", "b_pw2", "g1", "u1", "d1")
PK_OFF, PK_ROWS = _offsets(PK_ORDER)
LD_ORDER = ("g1", "u1", "d1", "b_pw1", "b_pw2", "d0", "g0", "u0", "a_out", "a_in")
LD_OFF, N_WROWS = _offsets(LD_ORDER)


def _span(off, keys):
    return off[keys[0]], sum(N_ROWS[k] for k in keys)


SM_B_NORM, SM_B_PW1, SM_B_CONV, SM_B_BCONV, SM_LN_G, SM_LN_B, SM_B_PW2, SM_A_CONV = 0, 1, 3, 34, 35, 36, 37, 38
SM_USED = 41
SM_F32_ROWS = 64


def _pcall(body, **kw):
    return pl.pallas_call(body, **kw)


def _cparams(sem=None):
    return pltpu.CompilerParams(dimension_semantics=sem, vmem_limit_bytes=VMEM_LIMIT)


def _dot(a, b):
    return jnp.dot(a, b, preferred_element_type=F32)


def _dot_nt(a, b):
    return lax.dot_general(a, b, (((1,), (1,)), ((), ())), preferred_element_type=F32)


def _dot_tn(a, b):
    return lax.dot_general(a, b, (((0,), (0,)), ((), ())), preferred_element_type=F32)


def _sigmoid(v):
    return 1.0 / (1.0 + jnp.exp(-v))


def _rms_stat(x):
    return lax.rsqrt(jnp.mean(x * x, axis=-1, keepdims=True) + RMS_EPS)


def _rms_bwd(dn, x, r, gamma):
    dng = dn * gamma
    dx = r * dng - x * (r * r * r) * jnp.mean(dng * x, axis=-1, keepdims=True)
    return dx, jnp.sum(dn * x * r, axis=0, keepdims=True)


def _load_weights(wg_ref, plan, sems):
    copies = []
    for j, (dst, key) in enumerate(plan):
        off, n = PK_OFF[key], N_ROWS[key]
        for d in range(NDEV):
            copies.append(pltpu.make_async_copy(
                wg_ref.at[d, pl.ds(off, n), :], dst.at[pl.ds(d * n, n), :], sems.at[j * NDEV + d]))
    for cp in copies:
        cp.start()
    for cp in copies:
        cp.wait()


_ANY = pl.BlockSpec(memory_space=pl.ANY)


def _row_spec(tm, width, rev_nt=None):
    if rev_nt is None:
        return pl.BlockSpec((tm, width), lambda i: (i, 0))
    return pl.BlockSpec((tm, width), lambda i: (rev_nt - 1 - i, 0))


def _full_spec(shape):
    return pl.BlockSpec(shape, lambda *_: (0,) * len(shape))


def _sds(shape, dtype):
    return jax.ShapeDtypeStruct(shape, dtype)


def _mesh_pos():
    return lax.axis_index("x"), lax.axis_index("y"), lax.axis_index("c")


def _lin(p):
    return 4 * p[0] + 2 * p[1] + p[2]


def _ag_exchange(src, slot, send_sems, recv_sems, local_sem):
    x, y, c = _mesh_pos()
    me, sibling = (x, y, c), (x, y, 1 - c)
    chips = [(1 - x, y), (x, 1 - y), (1 - x, 1 - y)]

    def copy(k, block, to, own=False):
        return pltpu.make_async_remote_copy(
            src_ref=src if own else slot(block), dst_ref=slot(block),
            send_sem=send_sems.at[k], recv_sem=recv_sems.at[k], device_id=to, device_id_type=MESH)

    mine = pltpu.make_async_copy(src, slot(me), local_sem)
    first = [copy(0, me, sibling, own=True)]
    first += [copy(1 + j, me, (*chip, c), own=True) for j, chip in enumerate(chips)]
    passed = [copy(4 + j, (*chip, c), sibling) for j, chip in enumerate(chips)]

    def start():
        mine.start()
        for cp in first:
            cp.start()

    def finish():
        for j, chip in enumerate(chips):
            copy(1 + j, (*chip, c), me).wait_recv()
            passed[j].start()
        copy(0, sibling, me).wait_recv()
        for j, chip in enumerate(chips):
            copy(4 + j, (*chip, 1 - c), me).wait_recv()
        for cp in first + passed:
            cp.wait_send()
        mine.wait()

    return start, finish


def _chip_exchange(p_ref, land_ref, send_sems, recv_sems, keys, part):
    x, y, c = _mesh_pos()
    l0, rows = _span(LD_OFF, keys)
    rows = rows // part[1]
    p0 = part[0] * rows
    dst = land_ref.at[2 * x + y, pl.ds(l0 + p0, rows), :]
    peers = [((x + (j >> 1)) % 2, (y + (j & 1)) % 2) for j in range(NCHIP)]

    def copy(j):
        tx, ty = peers[j]
        src = p_ref.at[2 * tx + ty, pl.ds(p0, rows), :]
        if j == 0:
            return pltpu.make_async_copy(src, dst, send_sems.at[0])
        return pltpu.make_async_remote_copy(
            src_ref=src, dst_ref=dst, send_sem=send_sems.at[j], recv_sem=recv_sems.at[j],
            device_id=(tx, ty, c), device_id_type=MESH)

    def start():
        for j in range(NCHIP):
            copy(j).start()

    def finish():
        for j in range(NCHIP):
            copy(j).wait()

    return start, finish


def _all_to_all_f32(src_for, dst_ref, send_sems, recv_sems):
    x, y, c = _mesh_pos()
    dst = dst_ref.at[_lin((x, y, c))]
    peers = [((x + ((j >> 2) & 1)) % 2, (y + ((j >> 1) & 1)) % 2, (c + (j & 1)) % 2) for j in range(NDEV)]

    def copy(j):
        if j == 0:
            return pltpu.make_async_copy(src_for(_lin(peers[0])), dst, send_sems.at[0])
        return pltpu.make_async_remote_copy(
            src_ref=src_for(_lin(peers[j])), dst_ref=dst, send_sem=send_sems.at[j], recv_sem=recv_sems.at[j],
            device_id=peers[j], device_id_type=MESH)

    def start():
        for j in range(NDEV):
            copy(j).start()

    def finish():
        for j in range(NDEV):
            copy(j).wait()

    return start, finish


class _Comm:
    def __init__(self, ins, alias_in, out_shape, scratch, make, gives_wg):
        self.ins, self.alias_in, self.out_shape = ins, alias_in, out_shape
        self.scratch, self.make, self.gives_wg = scratch, make, gives_wg


def _ag_comm(shard, wg, keys):
    def make(c_ins, c_out, sc):
        return _weights_exchange(c_ins[0], c_out, sc[0], sc[1], sc[2], keys)

    return _Comm([shard, wg], 1, _sds(wg.shape, BF16),
                 [pltpu.SemaphoreType.DMA((7,)), pltpu.SemaphoreType.DMA((7,)), pltpu.SemaphoreType.DMA],
                 make, True)


def _weights_exchange(shard_ref, wg_ref, send_sems, recv_sems, local_sem, keys):
    r0, nr = _span(PK_OFF, keys)
    return _ag_exchange(shard_ref.at[pl.ds(r0, nr), :], lambda p: wg_ref.at[_lin(p), pl.ds(r0, nr), :],
                        send_sems, recv_sems, local_sem)


def _rs_comm(psum, keys, land, part=(0, 1), body_reads_land=False):
    def make(c_ins, c_out, sc):
        return _chip_exchange(c_ins[0], c_out, sc[0], sc[1], keys, part)

    ins = [psum] + ([] if land is None else [land])
    return _Comm(ins, None if land is None else 1, _sds((NCHIP, N_WROWS, D), BF16),
                 [pltpu.SemaphoreType.DMA((NCHIP,)), pltpu.SemaphoreType.DMA((NCHIP,))], make, body_reads_land)


def _hosted_call(body, name, nt, in_specs, out_specs, out_shape, scratch, args, comm):
    if comm is None:
        return _pcall(body, name=name, grid=(nt,), in_specs=in_specs, out_specs=tuple(out_specs),
                      out_shape=tuple(out_shape), scratch_shapes=scratch,
                      compiler_params=_cparams(("arbitrary",)))(*args)
    n_in, n_out, n_sc, n_cin = len(in_specs), len(out_specs), len(scratch), len(comm.ins)

    def wrapped(*refs):
        ins = refs[:n_in]
        c_ins = refs[n_in:n_in + n_cin]
        outs = refs[n_in + n_cin:n_in + n_cin + n_out]
        c_out = refs[n_in + n_cin + n_out]
        sc = refs[n_in + n_cin + n_out + 1:n_in + n_cin + n_out + 1 + n_sc]
        c_sc = refs[n_in + n_cin + n_out + 1 + n_sc:]
        start, finish = comm.make(c_ins, c_out, c_sc)

        @pl.when(pl.program_id(0) == 0)
        def _():
            start()

        if comm.gives_wg:
            body(*ins, c_out, *outs, *sc)
        else:
            body(*ins, *outs, *sc)

        @pl.when(pl.program_id(0) == nt - 1)
        def _():
            finish()

    aliases = {} if comm.alias_in is None else {n_in + comm.alias_in: n_out}
    res = _pcall(wrapped, name=name, grid=(nt,),
                 in_specs=list(in_specs) + [_ANY] * n_cin, out_specs=tuple(out_specs) + (_ANY,),
                 out_shape=tuple(out_shape) + (comm.out_shape,),
                 scratch_shapes=list(scratch) + list(comm.scratch),
                 input_output_aliases=aliases,
                 compiler_params=_cparams(("arbitrary",)))(*args, *comm.ins)
    return res


def _pack_shard(weights, smalls):
    plan = (("a_in", 0, 0, True), ("a_out", 1, 0, False), ("b_pw1", 2, 0, True), ("b_pw2", 3, 0, False),
            ("g0", 4, 0, True), ("u0", 5, 0, True), ("d0", 6, 0, False),
            ("g1", 4, 1, True), ("u1", 5, 1, True), ("d1", 6, 1, False))
    n_pad = 384

    def body(*refs):
        out, sm, pad = refs[-3:]
        for key, idx, layer, transposed in plan:
            n = N_ROWS[key]
            val = refs[idx][layer]
            if transposed:
                if n % 128:
                    pad[:, pl.ds(0, n)] = val
                    pad[:, pl.ds(n, n_pad - n)] = jnp.zeros((D, n_pad - n), F32)
                    val = pad[...]
                val = val.T[:n]
            out[pl.ds(PK_OFF[key], n), :] = val.astype(BF16)
        b_norm, b_b_pw1, b_conv, b_b_conv, b_ln_g, b_ln_b, b_b_pw2, a_conv = refs[len(weights):-3]
        sm[...] = jnp.zeros((SM_F32_ROWS, 128), F32)
        sm[pl.ds(SM_B_PW1, 1), :] = b_b_pw1[:, pl.ds(0, 128)]
        sm[pl.ds(SM_B_PW1 + 1, 1), :] = b_b_pw1[:, pl.ds(128, 128)]
        sm[pl.ds(SM_B_CONV, KB), :] = b_conv[0]
        sm[pl.ds(SM_A_CONV, KA), :] = a_conv[0]
        for row, ref in ((SM_B_NORM, b_norm), (SM_B_BCONV, b_b_conv), (SM_LN_G, b_ln_g), (SM_LN_B, b_ln_b),
                         (SM_B_PW2, b_b_pw2)):
            sm[pl.ds(row, 1), :] = ref[...]

    vm = pl.BlockSpec(memory_space=pltpu.VMEM)
    return _pcall(
        body, name="pack_shard",
        out_shape=(_sds((PK_ROWS, D), BF16), _sds((SM_F32_ROWS, 128), F32)),
        in_specs=[vm] * (len(weights) + len(smalls)), out_specs=(vm, vm),
        scratch_shapes=[pltpu.VMEM((D, n_pad), F32)],
        compiler_params=_cparams(),
    )(*weights, *smalls)


def _all_gather_first(shard, small):
    def body(x_ref, s_ref, wg_ref, sg_ref, send_w, recv_w, local_w, send_s, recv_s, local_s):
        start_w, finish_w = _weights_exchange(x_ref, wg_ref, send_w, recv_w, local_w, ("a_in", "a_out"))
        start_s, finish_s = _ag_exchange(s_ref, lambda p: sg_ref.at[_lin(p)], send_s, recv_s, local_s)
        start_s()
        start_w()
        finish_s()
        finish_w()

    sems = [pltpu.SemaphoreType.DMA((7,)), pltpu.SemaphoreType.DMA((7,)), pltpu.SemaphoreType.DMA]
    return _pcall(
        body, name="all_gather_first",
        out_shape=(_sds((NDEV, PK_ROWS, D), BF16), _sds((NDEV, SM_F32_ROWS, 128), F32)),
        in_specs=[_ANY, _ANY], out_specs=(_ANY, _ANY), scratch_shapes=sems + sems,
    )(shard, small)


def _pair_exchange(grads, keys, side=None):
    l0, rows = _span(LD_OFF, keys)
    n_side = 0 if side is None else 2

    def body(*refs):
        g_refs = refs[:len(keys)]
        got, send_sem, recv_sem = refs[len(keys) + n_side], refs[-2], refs[-1]
        if side is not None:
            sm_ref, rp_ref = refs[len(keys):len(keys) + 2]
            sm_land, rp_land = refs[len(keys) + 3:len(keys) + 5]
            s2, r2, s3, r3 = refs[len(keys) + 5:len(keys) + 9]
            start_s, finish_s = _all_to_all_f32(lambda d: sm_ref.at[d], sm_land, s2, r2)
            start_r, finish_r = _all_to_all_f32(lambda d: rp_ref.at[0], rp_land, s3, r3)
            start_s()
            start_r()
        x, y, c = _mesh_pos()
        sibling = (x, y, 1 - c)
        for ref, key in zip(g_refs, keys):
            for t in range(NCHIP):
                pltpu.make_async_remote_copy(
                    src_ref=ref.at[2 * t + (1 - c)], dst_ref=got.at[t, pl.ds(LD_OFF[key] - l0, N_ROWS[key]), :],
                    send_sem=send_sem, recv_sem=recv_sem, device_id=sibling, device_id_type=MESH).start()
        pltpu.make_async_remote_copy(src_ref=got, dst_ref=got, send_sem=send_sem, recv_sem=recv_sem,
                                     device_id=sibling, device_id_type=MESH).wait()
        if side is not None:
            finish_s()
            finish_r()

    out_shape = [_sds((NCHIP, rows, D), BF16)]
    scratch = []
    if side is not None:
        out_shape += [_sds((NDEV, SM_F32_ROWS, 128), F32), _sds((NDEV, 8, D), F32)]
        scratch += [pltpu.SemaphoreType.DMA((NDEV,))] * 4
    res = _pcall(
        body, name="pair_exchange_" + keys[0],
        out_shape=tuple(out_shape),
        in_specs=[_ANY] * (len(keys) + n_side), out_specs=tuple(_ANY for _ in out_shape),
        scratch_shapes=scratch + [pltpu.SemaphoreType.DMA, pltpu.SemaphoreType.DMA],
    )(*grads, *(side or ()))
    return res[0] if side is None else res


def _pair_add(grads, keys, got):
    l0, rows = _span(LD_OFF, keys)

    def body(c_ref, *refs):
        g_refs = refs[:len(keys)]
        got_ref, o_ref = refs[len(keys):]
        for ref, key in zip(g_refs, keys):
            sl = pl.ds(LD_OFF[key] - l0, N_ROWS[key])
            o_ref[sl, :] = (ref[...].astype(F32) + got_ref[sl, :].astype(F32)).astype(BF16)

    grid_spec = pltpu.PrefetchScalarGridSpec(
        num_scalar_prefetch=1, grid=(NCHIP,),
        in_specs=[pl.BlockSpec((None, N_ROWS[k], D), lambda t, c: (2 * t + c[0], 0, 0)) for k in keys]
        + [pl.BlockSpec((None, rows, D), lambda t, c: (t, 0, 0))],
        out_specs=pl.BlockSpec((None, rows, D), lambda t, c: (t, 0, 0)))
    core = lax.axis_index("c").astype(jnp.int32).reshape(1)
    return _pcall(
        body, name="pair_add_" + keys[0], grid_spec=grid_spec,
        out_shape=_sds((NCHIP, rows, D), BF16),
        compiler_params=_cparams(("arbitrary",)),
    )(core, *grads, got)


def _pair_reduce(grads, keys):
    return _pair_add(grads, keys, _pair_exchange(grads, keys))


def _mixa_fwd(x, gam, cw, wg, tm, comm):
    T = x.shape[0]

    def body(x_ref, gam_ref, cw_ref, wg_ref, h_ref, n_ref, b_ref, c_ref, v_ref, cc_ref, y_ref,
             win, wout, buf, sems):
        @pl.when(pl.program_id(0) == 0)
        def _():
            _load_weights(wg_ref, [(win, "a_in"), (wout, "a_out")], sems)
            buf[pl.ds(0, HALO_A), :] = jnp.zeros((HALO_A, D), F32)

        xv = x_ref[...]
        nb = (xv * _rms_stat(xv) * gam_ref[...]).astype(BF16)
        n_ref[...] = nb
        bv = _dot_nt(nb, win[pl.ds(0, D), :])
        cval = _dot_nt(nb, win[pl.ds(D, D), :])
        vval = _dot_nt(nb, win[pl.ds(2 * D, D), :])
        cv = cval * vval
        buf[pl.ds(HALO_A, tm), :] = cv
        cc = cw_ref[pl.ds(KA - 1, 1), :] * cv
        for k in range(KA - 1):
            cc = cc + cw_ref[pl.ds(k, 1), :] * buf[pl.ds(HALO_A - (KA - 1) + k, tm), :]
        buf[pl.ds(0, HALO_A), :] = buf[pl.ds(tm, HALO_A), :]
        yb = (bv * cc).astype(BF16)
        b_ref[...] = bv.astype(BF16)
        c_ref[...] = cval.astype(BF16)
        v_ref[...] = vval.astype(BF16)
        cc_ref[...] = cc.astype(BF16)
        y_ref[...] = yb
        h_ref[...] = xv + _dot(yb, wout[...])

    act = _sds((T, D), BF16)
    return _hosted_call(
        body, "mixa_fwd", T // tm,
        in_specs=[_row_spec(tm, D), _full_spec((1, D)), _full_spec((KA, D))],
        out_specs=[_row_spec(tm, D) for _ in range(7)],
        out_shape=(_sds((T, D), F32),) + (act,) * 6,
        scratch=[pltpu.VMEM((3 * D, D), BF16), pltpu.VMEM((D, D), BF16),
                 pltpu.VMEM((HALO_A + tm, D), F32), pltpu.SemaphoreType.DMA((2 * NDEV,))],
        args=(x, gam, cw), comm=comm)


def _ffn_fwd(h, gam, wg, layer, tm, comm, head=None):
    T = h.shape[0]
    kg, ku, kd = "g%d" % layer, "u%d" % layer, "d%d" % layer
    n_head = 0 if head is None else 2

    def body(*refs):
        h_ref, gam_ref = refs[:2]
        wg_ref = refs[2 + n_head]
        o_ref, n_ref, g_ref, u_ref, gu_ref = refs[3 + n_head:8 + n_head]
        wgt, wut, wd, sems = refs[-4:]

        @pl.when(pl.program_id(0) == 0)
        def _():
            _load_weights(wg_ref, [(wgt, kg), (wut, ku), (wd, kd)], sems)

        hv = h_ref[...]
        nb = (hv * _rms_stat(hv) * gam_ref[...]).astype(BF16)
        n_ref[...] = nb
        out = hv
        for f in range(2):
            cols = pl.ds(f * FH, FH)
            g = _dot_nt(nb, wgt[cols, :])
            u = _dot_nt(nb, wut[cols, :])
            gu = (g * _sigmoid(g) * u).astype(BF16)
            g_ref[:, cols] = g.astype(BF16)
            u_ref[:, cols] = u.astype(BF16)
            gu_ref[:, cols] = gu
            out = out + _dot(gu, wd[cols, :])
        if head is None:
            o_ref[...] = out
        else:
            t_ref, fin_ref, st_ref = refs[2], refs[3], refs[8 + n_head]

            @pl.when(pl.program_id(0) == 0)
            def _():
                st_ref[...] = jnp.zeros((8, D), F32)

            gamma = fin_ref[...]
            r = _rms_stat(out)
            err = out * r * gamma - t_ref[...]
            dx, dgam = _rms_bwd(err * (1.0 / D), out, r, gamma)
            o_ref[...] = dx
            st_ref[pl.ds(0, 1), :] += dgam
            st_ref[pl.ds(1, 1), :] += (0.5 / D) * jnp.sum(err * err, axis=0, keepdims=True)

    pre = _sds((T, FF), BF16)
    in_specs = [_row_spec(tm, D), _full_spec((1, D))]
    args = (h, gam)
    out_specs = [_row_spec(tm, D), _row_spec(tm, D), _row_spec(tm, FF), _row_spec(tm, FF), _row_spec(tm, FF)]
    out_shape = (_sds((T, D), F32), _sds((T, D), BF16), pre, pre, pre)
    if head is not None:
        in_specs, args = in_specs + [_row_spec(tm, D), _full_spec((1, D))], args + tuple(head)
        out_specs, out_shape = out_specs + [_full_spec((8, D))], out_shape + (_sds((8, D), F32),)
    if comm is None:
        in_specs, args = in_specs + [_ANY], args + (wg,)
    return _hosted_call(
        body, "ffn%d_fwd" % layer, T // tm,
        in_specs=in_specs, out_specs=out_specs, out_shape=out_shape,
        scratch=[pltpu.VMEM((FF, D), BF16)] * 3 + [pltpu.SemaphoreType.DMA((3 * NDEV,))],
        args=args, comm=comm)


def _shifted_copies(buf, shf, tm):
    for r in range(1, 8):
        shf[r - 1] = buf[pl.ds(r, tm + HALO_B - 8), :]


def _broadcast_taps(vec_ref, wb):
    for k in range(KB):
        wb[k] = jnp.broadcast_to(vec_ref[pl.ds(SM_B_CONV + k, 1), :], (8, D))
    wb[KB] = jnp.broadcast_to(vec_ref[pl.ds(SM_B_BCONV, 1), :], (8, D))


def _taps_by_shift_residue(taps):
    groups = {}
    for k, shift in taps:
        q, r = divmod(shift, 8)
        groups.setdefault(r, []).append((k, q))
    return sorted(groups.items())


def _window(buf, shf, base, r, q0, n_groups, lanes):
    rows = pl.ds(base + 8 * q0, 8 * n_groups)
    v = buf[rows, lanes] if r == 0 else shf[r - 1, rows, lanes]
    return [v[8 * i:8 * i + 8] for i in range(n_groups)]


def _long_conv(buf, shf, wb, out_ref, tm, taps, bias_row):
    n_acc = CONV_ROWS // 8
    groups = _taps_by_shift_residue(taps)
    for col in range(D // 128):
        lanes = pl.ds(128 * col, 128)

        def rows(i, carry, lanes=lanes):
            base = i * CONV_ROWS
            init = jnp.zeros((8, 128), F32) if bias_row is None else wb[bias_row, :, lanes]
            accs = [init] * n_acc
            for r, lst in groups:
                q0, q1 = min(q for _, q in lst), max(q for _, q in lst)
                win = _window(buf, shf, base, r, q0, n_acc + q1 - q0, lanes)
                for k, q in lst:
                    wk = wb[k, :, lanes]
                    accs = [acc + wk * win[h + q - q0] for h, acc in enumerate(accs)]
            for h, acc in enumerate(accs):
                out_ref[pl.ds(base + 8 * h, 8), lanes] = acc
            return carry

        for i in range(tm // CONV_ROWS):
            rows(i, 0)


def _long_conv_grad_taps(buf, shf, x_ref, wacc, tm):
    n_acc = CONV_ROWS // 8
    groups = _taps_by_shift_residue([(KB - 1 - j, j) for j in range(KB)])
    for col in range(D // 128):
        lanes = pl.ds(128 * col, 128)

        def rows(i, carry, lanes=lanes):
            base = i * CONV_ROWS
            xv = x_ref[pl.ds(base, CONV_ROWS), lanes]
            xs = [xv[8 * h:8 * h + 8] for h in range(n_acc)]
            for r, lst in groups:
                q0, q1 = min(q for _, q in lst), max(q for _, q in lst)
                win = _window(buf, shf, base, r, q0, n_acc + q1 - q0, lanes)
                for k, q in lst:
                    prod = [x * win[h + q - q0] for h, x in enumerate(xs)]
                    wacc[k, :, lanes] += functools.reduce(lambda a, b: a + b, prod)
            return carry

        for i in range(tm // CONV_ROWS):
            rows(i, 0)


def _ln_stats(dc):
    mu = jnp.mean(dc, axis=-1, keepdims=True)
    xc = dc - mu
    rstd = lax.rsqrt(jnp.mean(xc * xc, axis=-1, keepdims=True) + LN_EPS)
    return xc * rstd, rstd


def _mixb_fwd(h, vecs, bias1, wg, tm, comm):
    T = h.shape[0]

    def body(h_ref, vec_ref, b1_ref, wg_ref, o_ref, n_ref, a_ref, g_ref, dc_ref, s_ref, w1, w2, buf, shf, wb, sems):
        @pl.when(pl.program_id(0) == 0)
        def _():
            _load_weights(wg_ref, [(w1, "b_pw1"), (w2, "b_pw2")], sems)
            buf[pl.ds(0, HALO_B), :] = jnp.zeros((HALO_B, D), F32)
            _broadcast_taps(vec_ref, wb)

        hv = h_ref[...]
        nb = (hv * _rms_stat(hv) * vec_ref[pl.ds(SM_B_NORM, 1), :]).astype(BF16)
        n_ref[...] = nb
        a = _dot_nt(nb, w1[pl.ds(0, D), :]) + b1_ref[:, pl.ds(0, D)]
        g = _dot_nt(nb, w1[pl.ds(D, D), :]) + b1_ref[:, pl.ds(D, D)]
        a_ref[...] = a.astype(BF16)
        g_ref[...] = g.astype(BF16)
        buf[pl.ds(HALO_B, tm), :] = a * _sigmoid(g)
        _shifted_copies(buf, shf, tm)

        _long_conv(buf, shf, wb, dc_ref, tm, [(k, HALO_B - (KB - 1) + k) for k in range(KB)], KB)
        buf[pl.ds(0, HALO_B), :] = buf[pl.ds(tm, HALO_B), :]
        xhat, _ = _ln_stats(dc_ref[...])
        ln = xhat * vec_ref[pl.ds(SM_LN_G, 1), :] + vec_ref[pl.ds(SM_LN_B, 1), :]
        s = (ln * _sigmoid(ln)).astype(BF16)
        s_ref[...] = s
        o_ref[...] = hv + _dot(s, w2[...]) + vec_ref[pl.ds(SM_B_PW2, 1), :]

    act = _sds((T, D), BF16)
    return _hosted_call(
        body, "mixb_fwd", T // tm,
        in_specs=[_row_spec(tm, D), _full_spec((SM_F32_ROWS, D)), _full_spec((1, 2 * D))],
        out_specs=[_row_spec(tm, D) for _ in range(6)],
        out_shape=(_sds((T, D), F32), act, act, act, _sds((T, D), F32), act),
        scratch=[pltpu.VMEM((2 * D, D), BF16), pltpu.VMEM((D, D), BF16),
                 pltpu.VMEM((HALO_B + tm, D), F32), pltpu.VMEM((7, HALO_B + tm - 8, D), F32),
                 pltpu.VMEM((KB + 1, 8, D), F32), pltpu.SemaphoreType.DMA((2 * NDEV,))],
        args=(h, vecs, bias1), comm=comm)


def _ffn_bwd_dx(dh, h, g, u, gam, wg, layer, tm, comm):
    T = h.shape[0]
    kg, ku, kd = "g%d" % layer, "u%d" % layer, "d%d" % layer

    def body(dh_ref, h_ref, g_ref, u_ref, gam_ref, wg_ref, o_ref, dg_ref, du_ref, st_ref, wgt, wut, wd, sems):
        @pl.when(pl.program_id(0) == 0)
        def _():
            _load_weights(wg_ref, [(wgt, kg), (wut, ku), (wd, kd)], sems)
            st_ref[...] = jnp.zeros((8, D), F32)

        dhv = dh_ref[...]
        dhb = dhv.astype(BF16)
        dn = jnp.zeros_like(dhv)
        for f in range(2):
            cols = pl.ds(f * FH, FH)
            dgu = _dot_nt(dhb, wd[cols, :])
            gv = g_ref[:, cols].astype(F32)
            sg = _sigmoid(gv)
            du = (dgu * gv * sg).astype(BF16)
            dg = (dgu * u_ref[:, cols].astype(F32) * (sg * (1.0 + gv * (1.0 - sg)))).astype(BF16)
            dg_ref[:, cols] = dg
            du_ref[:, cols] = du
            dn = dn + _dot(dg, wgt[cols, :]) + _dot(du, wut[cols, :])
        hv = h_ref[...]
        dx, dgam = _rms_bwd(dn, hv, _rms_stat(hv), gam_ref[...])
        o_ref[...] = dhv + dx
        st_ref[pl.ds(0, 1), :] += dgam

    pre = _sds((T, FF), BF16)
    return _hosted_call(
        body, "ffn%d_bwd_dx" % layer, T // tm,
        in_specs=[_row_spec(tm, D), _row_spec(tm, D), _row_spec(tm, FF), _row_spec(tm, FF),
                  _full_spec((1, D)), _ANY],
        out_specs=[_row_spec(tm, D), _row_spec(tm, FF), _row_spec(tm, FF), _full_spec((8, D))],
        out_shape=(_sds((T, D), F32), pre, pre, _sds((8, D), F32)),
        scratch=[pltpu.VMEM((FF, D), BF16)] * 3 + [pltpu.SemaphoreType.DMA((3 * NDEV,))],
        args=(dh, h, g, u, gam, wg), comm=comm)


def _grad_w(lhs, rhs, mc, name, tm):
    T, M = lhs.shape
    nt = T // tm

    def body(l_ref, r_ref, o_ref, acc):
        i = pl.program_id(1)

        @pl.when(i == 0)
        def _():
            acc[...] = jnp.zeros((mc, D), F32)

        acc[...] += _dot_tn(l_ref[...], r_ref[...].astype(BF16))

        @pl.when(i == nt - 1)
        def _():
            o_ref[...] = acc[...].astype(BF16)

    return _pcall(
        body, name=name, grid=(M // mc, nt),
        out_shape=_sds((M, D), BF16),
        in_specs=[pl.BlockSpec((tm, mc), lambda j, i: (i, j)), pl.BlockSpec((tm, D), lambda j, i: (i, 0))],
        out_specs=pl.BlockSpec((mc, D), lambda j, i: (j, 0)),
        scratch_shapes=[pltpu.VMEM((mc, D), F32)],
        compiler_params=_cparams(("arbitrary", "arbitrary")),
    )(lhs, rhs)


def _mixb_bwd(dh, h, a, g, dc, vecs, wg, tm, comm):
    T = h.shape[0]
    nt = T // tm

    def body(dh_ref, h_ref, a_ref, g_ref, dc_ref, vec_ref, wg_ref, o_ref, du_ref, st_ref, sb_ref,
             w1, w2, buf, shf, glu_s, dglu_s, wacc, wb, sems):
        @pl.when(pl.program_id(0) == 0)
        def _():
            _load_weights(wg_ref, [(w1, "b_pw1"), (w2, "b_pw2")], sems)
            buf[pl.ds(tm, HALO_B), :] = jnp.zeros((HALO_B, D), F32)
            wacc[...] = jnp.zeros((KB, 8, D), F32)
            _broadcast_taps(vec_ref, wb)
            st_ref[...] = jnp.zeros((SM_F32_ROWS, D), F32)
            sb_ref[...] = jnp.zeros((8, 2 * D), F32)

        def acc(row, val):
            st_ref[pl.ds(row, 1), :] += jnp.sum(val, axis=0, keepdims=True)

        dhv = dh_ref[...]
        acc(SM_B_PW2, dhv)
        ds = _dot_nt(dhv.astype(BF16), w2[...])
        xhat, rstd = _ln_stats(dc_ref[...])
        ln_g = vec_ref[pl.ds(SM_LN_G, 1), :]
        ln = xhat * ln_g + vec_ref[pl.ds(SM_LN_B, 1), :]
        sl = _sigmoid(ln)
        dln = ds * (sl * (1.0 + ln * (1.0 - sl)))
        acc(SM_LN_G, dln * xhat)
        acc(SM_LN_B, dln)
        dxh = dln * ln_g
        ddc = rstd * (dxh - jnp.mean(dxh, axis=-1, keepdims=True)
                      - xhat * jnp.mean(dxh * xhat, axis=-1, keepdims=True))
        acc(SM_B_BCONV, ddc)
        buf[pl.ds(0, tm), :] = ddc
        _shifted_copies(buf, shf, tm)
        av = a_ref[...].astype(F32)
        sg = _sigmoid(g_ref[...].astype(F32))
        glu_s[...] = av * sg

        _long_conv(buf, shf, wb, dglu_s, tm, [(KB - 1 - j, j) for j in range(KB)], None)
        _long_conv_grad_taps(buf, shf, glu_s, wacc, tm)
        buf[pl.ds(tm, HALO_B), :] = buf[pl.ds(0, HALO_B), :]
        dglu = dglu_s[...]
        da = dglu * sg
        dg = dglu * av * sg * (1.0 - sg)
        sb_ref[pl.ds(0, 1), pl.ds(0, D)] += jnp.sum(da, axis=0, keepdims=True)
        sb_ref[pl.ds(0, 1), pl.ds(D, D)] += jnp.sum(dg, axis=0, keepdims=True)
        dab, dgb = da.astype(BF16), dg.astype(BF16)
        du_ref[:, pl.ds(0, D)] = dab
        du_ref[:, pl.ds(D, D)] = dgb
        dn = _dot(dab, w1[pl.ds(0, D), :]) + _dot(dgb, w1[pl.ds(D, D), :])
        hv = h_ref[...]
        dx, dgam = _rms_bwd(dn, hv, _rms_stat(hv), vec_ref[pl.ds(SM_B_NORM, 1), :])
        o_ref[...] = dhv + dx
        st_ref[pl.ds(SM_B_NORM, 1), :] += dgam

        @pl.when(pl.program_id(0) == nt - 1)
        def _():
            st_ref[pl.ds(SM_B_CONV, KB), :] = jnp.sum(wacc[...], axis=1)

    rs = functools.partial(_row_spec, rev_nt=nt)
    return _hosted_call(
        body, "mixb_bwd", nt,
        in_specs=[rs(tm, D), rs(tm, D), rs(tm, D), rs(tm, D), rs(tm, D), _full_spec((SM_F32_ROWS, D)), _ANY],
        out_specs=[rs(tm, D), rs(tm, 2 * D), _full_spec((SM_F32_ROWS, D)), _full_spec((8, 2 * D))],
        out_shape=(_sds((T, D), F32), _sds((T, 2 * D), BF16), _sds((SM_F32_ROWS, D), F32), _sds((8, 2 * D), F32)),
        scratch=[pltpu.VMEM((2 * D, D), BF16), pltpu.VMEM((D, D), BF16),
                 pltpu.VMEM((tm + HALO_B, D), F32), pltpu.VMEM((7, tm + HALO_B - 8, D), F32),
                 pltpu.VMEM((tm, D), F32), pltpu.VMEM((tm, D), F32), pltpu.VMEM((KB, 8, D), F32),
                 pltpu.VMEM((KB + 1, 8, D), F32), pltpu.SemaphoreType.DMA((2 * NDEV,))],
        args=(dh, h, a, g, dc, vecs, wg), comm=comm)


def _mixa_bwd(dh, x, b, c, v, cc, gam, cw, wg, tm, comm):
    T = x.shape[0]
    nt = T // tm

    def body(dh_ref, x_ref, b_ref, c_ref, v_ref, cc_ref, gam_ref, cw_ref, wg_ref,
             o_ref, db_ref, st_ref, win, wout, buf, sems):
        @pl.when(pl.program_id(0) == 0)
        def _():
            _load_weights(wg_ref, [(win, "a_in"), (wout, "a_out")], sems)
            buf[pl.ds(tm, HALO_A), :] = jnp.zeros((HALO_A, D), F32)
            st_ref[...] = jnp.zeros((8, D), F32)

        dhv = dh_ref[...]
        dy = _dot_nt(dhv.astype(BF16), wout[...])
        cval = c_ref[...].astype(F32)
        vval = v_ref[...].astype(F32)
        d_b = (dy * cc_ref[...].astype(F32)).astype(BF16)
        buf[pl.ds(0, tm), :] = dy * b_ref[...].astype(F32)
        cv = cval * vval
        dcv = jnp.zeros((tm, D), F32)
        for j in range(KA):
            sh = buf[pl.ds(j, tm), :]
            k = KA - 1 - j
            dcv = dcv + cw_ref[pl.ds(k, 1), :] * sh
            st_ref[pl.ds(1 + k, 1), :] += jnp.sum(cv * sh, axis=0, keepdims=True)
        buf[pl.ds(tm, HALO_A), :] = buf[pl.ds(0, HALO_A), :]
        d_c = (dcv * vval).astype(BF16)
        d_v = (dcv * cval).astype(BF16)
        db_ref[:, pl.ds(0, D)] = d_b
        db_ref[:, pl.ds(D, D)] = d_c
        db_ref[:, pl.ds(2 * D, D)] = d_v
        dn = _dot(d_b, win[pl.ds(0, D), :]) + _dot(d_c, win[pl.ds(D, D), :]) + _dot(d_v, win[pl.ds(2 * D, D), :])
        xv = x_ref[...]
        dx, dgam = _rms_bwd(dn, xv, _rms_stat(xv), gam_ref[...])
        o_ref[...] = dhv + dx
        st_ref[pl.ds(0, 1), :] += dgam

    rs = functools.partial(_row_spec, rev_nt=nt)
    return _hosted_call(
        body, "mixa_bwd", nt,
        in_specs=[rs(tm, D) for _ in range(6)] + [_full_spec((1, D)), _full_spec((KA, D)), _ANY],
        out_specs=[rs(tm, D), rs(tm, 3 * D), _full_spec((8, D))],
        out_shape=(_sds((T, D), F32), _sds((T, 3 * D), BF16), _sds((8, D), F32)),
        scratch=[pltpu.VMEM((3 * D, D), BF16), pltpu.VMEM((D, D), BF16),
                 pltpu.VMEM((tm + HALO_A, D), F32), pltpu.SemaphoreType.DMA((2 * NDEV,))],
        args=(dh, x, b, c, v, cc, gam, cw, wg), comm=comm)


def _sum_small(sm, rp):
    def body(sm_ref, rp_ref, osm, orp, oloss):
        a = sm_ref[0]
        b = rp_ref[0]
        for s in range(1, NDEV):
            a = a + sm_ref[s]
            b = b + rp_ref[s]
        osm[...] = a
        orp[...] = b
        oloss[...] = jnp.zeros((8, 128), F32) + jnp.sum(b[4:5, :], axis=-1, keepdims=True)

    return _pcall(
        body, name="sum_small_grads",
        out_shape=(_sds((SM_F32_ROWS, 128), F32), _sds((8, D), F32), _sds((8, 128), F32)),
        in_specs=[pl.BlockSpec(memory_space=pltpu.VMEM)] * 2,
        out_specs=tuple(pl.BlockSpec(memory_space=pltpu.VMEM) for _ in range(3)),
        compiler_params=_cparams(),
    )(sm, rp)


def _adam_math(w, g, m, v):
    mn = B1 * m + (1.0 - B1) * g
    vn = B2 * v + (1.0 - B2) * (g * g)
    m_hat = mn / (1.0 - B1 ** STEP)
    v_hat = vn / (1.0 - B2 ** STEP)
    return -LR * (m_hat / (jnp.sqrt(v_hat) + ADAM_EPS) + WD * w), mn, vn


def _finish_weight(land, keys, transposed, w, m, v, name, comm=None):
    layers, rows, cols = w.shape
    n = N_ROWS[keys[0]]
    n_pad = -(-n // 128) * 128 if transposed else n

    def body(w_ref, m_ref, v_ref, land_ref, og, od, om, ov, buf, sem):
        off = LD_OFF[keys[0]]
        if layers == 2:
            off = jnp.where(pl.program_id(0) == 0, off, LD_OFF[keys[1]])
        cp = pltpu.make_async_copy(land_ref.at[:, pl.ds(pl.multiple_of(off, 32), n), :], buf, sem)
        cp.start()
        cp.wait()
        g = buf[0].astype(F32)
        for s in range(1, NCHIP):
            g = g + buf[s].astype(F32)
        if transposed:
            if n_pad != n:
                g = jnp.concatenate([g, jnp.zeros((n_pad - n, D), F32)], axis=0)
            g = g.T[:, :n]
        d, mn, vn = _adam_math(w_ref[...], g, m_ref[...], v_ref[...])
        og[...] = g
        od[...] = d
        om[...] = mn
        ov[...] = vn

    spec = pl.BlockSpec((None, rows, cols), lambda l: (l, 0, 0))
    shp = _sds(w.shape, F32)
    in_specs, args = [spec, spec, spec], (w, m, v)
    if comm is None:
        in_specs, args = in_specs + [_ANY], args + (land,)
    return _hosted_call(
        body, "finish_" + name, layers,
        in_specs=in_specs, out_specs=[spec] * 4, out_shape=(shp,) * 4,
        scratch=[pltpu.VMEM((NCHIP, n, D), BF16), pltpu.SemaphoreType.DMA],
        args=args, comm=comm)


SMALL_PARAMS = (("sm", SM_B_NORM, 1), ("sm", SM_B_PW1, 2), ("sm", SM_B_CONV, KB), ("sm", SM_B_BCONV, 1),
                ("sm", SM_LN_G, 1), ("sm", SM_LN_B, 1), ("sm", SM_B_PW2, 1), ("sm", SM_A_CONV, KA),
                ("rp", 0, 1), ("rp", 1, 2), ("rp", 3, 1))


def _adamw_small(g_small, g_repl, triples):
    n = len(SMALL_PARAMS)

    def body(*refs):
        gs_ref, gr_ref = refs[0], refs[1]
        ins = refs[2:2 + 3 * n]
        outs = refs[2 + 3 * n:]
        for i, (src, r0, nr) in enumerate(SMALL_PARAMS):
            w_ref, m_ref, v_ref = ins[3 * i:3 * i + 3]
            g = (gs_ref if src == "sm" else gr_ref)[pl.ds(r0, nr), :]
            if i == 1:
                g = jnp.concatenate([g[0:1], g[1:2]], axis=1)
            lead = (0,) if len(w_ref.shape) == 3 else ()
            idx = lead + (slice(None), slice(None))
            vals = (g,) + _adam_math(w_ref[idx], g, m_ref[idx], v_ref[idx])
            for o_ref, val in zip(outs[4 * i:4 * i + 4], vals):
                o_ref[idx] = val

    flat = [a for t in triples for a in t]
    out_shape = tuple(_sds(t[0].shape, F32) for t in triples for _ in range(4))
    vm = pl.BlockSpec(memory_space=pltpu.VMEM)
    res = _pcall(
        body, name="adamw_small", out_shape=out_shape,
        in_specs=[vm] * (2 + len(flat)), out_specs=tuple(vm for _ in out_shape),
        compiler_params=_cparams(),
    )(g_small, g_repl, *flat)
    return [tuple(res[4 * i:4 * i + 4]) for i in range(n)]


def kernel(x, a_norm, a_w_in, a_conv, a_w_out, b_norm, b_w_pw1, b_b_pw1, b_conv, b_b_conv, b_ln_g, b_ln_b, b_w_pw2, b_b_pw2, ffn_norm, ffn_w_gate, ffn_w_up, ffn_w_down, final_norm, loss_target, m_a_norm, m_a_w_in, m_a_conv, m_a_w_out, m_b_norm, m_b_w_pw1, m_b_b_pw1, m_b_conv, m_b_b_conv, m_b_ln_g, m_b_ln_b, m_b_w_pw2, m_b_b_pw2, m_ffn_norm, m_ffn_w_gate, m_ffn_w_up, m_ffn_w_down, m_final_norm, v_a_norm, v_a_w_in, v_a_conv, v_a_w_out, v_b_norm, v_b_w_pw1, v_b_b_pw1, v_b_conv, v_b_b_conv, v_b_ln_g, v_b_ln_b, v_b_w_pw2, v_b_b_pw2, v_ffn_norm, v_ffn_w_gate, v_ffn_w_up, v_ffn_w_down, v_final_norm):
    T = x.shape[1]
    tm = min(TM, T)
    tw = min(TM_DW, T)
    xs = x.reshape(T, D)
    tgt = loss_target.reshape(T, D)

    shard, small = _pack_shard((a_w_in, a_w_out, b_w_pw1, b_w_pw2, ffn_w_gate, ffn_w_up, ffn_w_down),
                               (b_norm, b_b_pw1, b_conv, b_b_conv, b_ln_g, b_ln_b, b_b_pw2, a_conv))
    wg, sm_all = _all_gather_first(shard, small)
    vecs = sm_all.transpose(1, 0, 2).reshape(SM_F32_ROWS, D)
    bias1 = sm_all[:, SM_B_PW1:SM_B_PW1 + 2, :].reshape(1, 2 * D)
    cw_a = vecs[SM_A_CONV:SM_A_CONV + KA]
    fn0, fn1 = ffn_norm[0:1], ffn_norm[1:2]
    fin = final_norm.reshape(1, D)

    h1, n0, bq, cq, vq, ccq, yq, wg = _mixa_fwd(xs, a_norm, cw_a, wg, tm, _ag_comm(shard, wg, ("g0", "u0", "d0")))
    h2, n1, g0, u0, gu0, wg = _ffn_fwd(h1, fn0, wg, 0, tm, _ag_comm(shard, wg, ("b_pw1", "b_pw2")))
    h3, n2, aq, gq, dcq, sq, wg = _mixb_fwd(h2, vecs, bias1, wg, tm, _ag_comm(shard, wg, ("g1", "u1", "d1")))
    dh4, n3, g1, u1, gu1, st_fin = _ffn_fwd(h3, fn1, wg, 1, tm, None, head=(tgt, fin))

    def by_dest(gw, key):
        return gw.reshape(NDEV, N_ROWS[key], D)

    gw_d1 = by_dest(_grad_w(gu1, dh4, FH, "grad_down1", tw), "d1")
    dh3, dg1, du1, st_f1 = _ffn_bwd_dx(dh4, h3, g1, u1, fn1, wg, 1, tm, None)
    gw_g1 = by_dest(_grad_w(dg1, n3, FH, "grad_gate1", tw), "g1")
    gw_u1 = by_dest(_grad_w(du1, n3, FH, "grad_up1", tw), "u1")
    keys = ("g1", "u1", "d1")
    dh2, dub, st_b, st_b1, land = _mixb_bwd(
        dh3, h2, aq, gq, dcq, vecs, wg, tm, _rs_comm(_pair_reduce([gw_g1, gw_u1, gw_d1], keys), keys, None))
    gw_pw1 = by_dest(_grad_w(dub, n2, D, "grad_pw1", tw), "b_pw1")
    gw_pw2 = by_dest(_grad_w(sq, dh3, D, "grad_pw2", tw), "b_pw2")
    gw_d0 = by_dest(_grad_w(gu0, dh2, FH, "grad_down0", tw), "d0")
    keys = ("b_pw1", "b_pw2", "d0")
    dh1, dg0, du0, st_f0, land = _ffn_bwd_dx(
        dh2, h1, g0, u0, fn0, wg, 0, tm, _rs_comm(_pair_reduce([gw_pw1, gw_pw2, gw_d0], keys), keys, land))
    gw_g0 = by_dest(_grad_w(dg0, n1, FH, "grad_gate0", tw), "g0")
    gw_u0 = by_dest(_grad_w(du0, n1, FH, "grad_up0", tw), "u0")
    gw_out = by_dest(_grad_w(yq, dh1, D, "grad_out", tw), "a_out")
    keys = ("g0", "u0", "a_out")
    dx, dbcv, st_a, land = _mixa_bwd(
        dh1, xs, bq, cq, vq, ccq, a_norm, cw_a, wg, tm,
        _rs_comm(_pair_reduce([gw_g0, gw_u0, gw_out], keys), keys, land))
    gw_in = by_dest(_grad_w(dbcv, n0, D, "grad_in", tw), "a_in")

    st_small = st_b.at[SM_A_CONV:SM_A_CONV + KA].set(st_a[1:1 + KA])
    sm_dest = st_small.reshape(SM_F32_ROWS, NDEV, 128).transpose(1, 0, 2)
    sm_dest = sm_dest.at[:, SM_B_PW1:SM_B_PW1 + 2, :].set(st_b1[0].reshape(NDEV, 2, 128))
    repl = jnp.concatenate([st_a[0:1], st_f0[0:1], st_f1[0:1], st_fin[0:1], st_fin[1:2],
                            jnp.zeros((3, D), F32)], axis=0)[None]
    got_in, sm_land, rp_land = _pair_exchange([gw_in], ("a_in",), side=(sm_dest, repl))
    p_in = _pair_add([gw_in], ("a_in",), got_in)

    big = [("ffn_w_gate", ("g0", "g1"), True, ffn_w_gate, m_ffn_w_gate, v_ffn_w_gate),
           ("ffn_w_up", ("u0", "u1"), True, ffn_w_up, m_ffn_w_up, v_ffn_w_up),
           ("ffn_w_down", ("d0", "d1"), False, ffn_w_down, m_ffn_w_down, v_ffn_w_down),
           ("a_w_out", ("a_out",), False, a_w_out, m_a_w_out, v_a_w_out),
           ("b_w_pw1", ("b_pw1",), True, b_w_pw1, m_b_w_pw1, v_b_w_pw1),
           ("b_w_pw2", ("b_pw2",), False, b_w_pw2, m_b_w_pw2, v_b_w_pw2),
           ("a_w_in", ("a_in",), True, a_w_in, m_a_w_in, v_a_w_in)]
    res = {}
    for k, (name, keys, transposed, w, m, v) in enumerate(big):
        if k < 3:
            comm = _rs_comm(p_in, ("a_in",), land, part=(k, 3), body_reads_land=True)
            *res[name], land = _finish_weight(land, keys, transposed, w, m, v, name, comm)
        else:
            res[name] = _finish_weight(land, keys, transposed, w, m, v, name)

    g_small, g_repl, loss8 = _sum_small(sm_land, rp_land)
    loss = loss8[0, 0]

    small_names = ["b_norm", "b_b_pw1", "b_conv", "b_b_conv", "b_ln_g", "b_ln_b", "b_b_pw2", "a_conv",
                   "a_norm", "ffn_norm", "final_norm"]
    triples = [(b_norm, m_b_norm, v_b_norm), (b_b_pw1, m_b_b_pw1, v_b_b_pw1), (b_conv, m_b_conv, v_b_conv),
               (b_b_conv, m_b_b_conv, v_b_b_conv), (b_ln_g, m_b_ln_g, v_b_ln_g), (b_ln_b, m_b_ln_b, v_b_ln_b),
               (b_b_pw2, m_b_b_pw2, v_b_b_pw2), (a_conv, m_a_conv, v_a_conv), (a_norm, m_a_norm, v_a_norm),
               (ffn_norm, m_ffn_norm, v_ffn_norm),
               (fin, m_final_norm.reshape(1, D), v_final_norm.reshape(1, D))]
    for name, quad in zip(small_names, _adamw_small(g_small, g_repl, triples)):
        res[name] = quad
    res["final_norm"] = tuple(a.reshape(D) for a in res["final_norm"])

    order = ["a_norm", "a_w_in", "a_conv", "a_w_out", "b_norm", "b_w_pw1", "b_b_pw1", "b_conv", "b_b_conv",
             "b_ln_g", "b_ln_b", "b_w_pw2", "b_b_pw2", "ffn_norm", "ffn_w_gate", "ffn_w_up", "ffn_w_down",
             "final_norm"]
    out = [loss, dx.reshape(1, T, D)]
    for j in range(4):
        out += [res[k][j] for k in order]
    return tuple(out)
```

```python
import functools

import jax
import jax.numpy as jnp
from jax import lax
from jax.experimental import pallas as pl
from jax.experimental.pallas import tpu as pltpu

F32 = jnp.float32
BF16 = jnp.bfloat16
MESH = pl.DeviceIdType.MESH

D = 1024
FF = 2816
FH = FF // 2
NDEV = 8
KA = 3
KB = 31
HALO_A = 8
HALO_B = 32
CONV_ROWS = 64
RMS_EPS = 1e-6
LN_EPS = 1e-5
LR, B1, B2, ADAM_EPS, WD, STEP = 0.001, 0.9, 0.999, 1e-08, 0.01, 10

TM = 256
TM_A = 512
TM_DW = 1024
VMEM_LIMIT = 56 * 1024 * 1024

N_ROWS = {"a_in": 384, "a_out": 128, "b_pw1": 256, "b_pw2": 128, "g0": 352, "u0": 352, "d0": 352,
          "g1": 352, "u1": 352, "d1": 352}
WEIGHT_KEYS = ("a_in", "a_out", "b_pw1", "b_pw2", "g0", "u0", "d0", "g1", "u1", "d1")
NCHIP = 4


def _offsets(order):
    off, o = {}, 0
    for k in order:
        off[k] = o
        o += N_ROWS[k]
    return off, o


PK_ORDER = ("a_in", "a_out", "g0", "u0", "d0", "b_pw1", "b_pw2", "g1", "u1", "d1")
PK_OFF, PK_ROWS = _offsets(PK_ORDER)
LD_ORDER = ("g1", "u1", "d1", "b_pw1", "b_pw2", "d0", "g0", "u0", "a_out", "a_in")
LD_OFF, N_WROWS = _offsets(LD_ORDER)


def _span(off, keys):
    return off[keys[0]], sum(N_ROWS[k] for k in keys)


SM_B_NORM, SM_B_PW1, SM_B_CONV, SM_B_BCONV, SM_LN_G, SM_LN_B, SM_B_PW2, SM_A_CONV = 0, 1, 3, 34, 35, 36, 37, 38
SM_USED = 41
SM_F32_ROWS = 64


def _pcall(body, **kw):
    return pl.pallas_call(body, **kw)


def _cparams(sem=None):
    return pltpu.CompilerParams(dimension_semantics=sem, vmem_limit_bytes=VMEM_LIMIT)


def _dot(a, b):
    return jnp.dot(a, b, preferred_element_type=F32)


def _dot_nt(a, b):
    return lax.dot_general(a, b, (((1,), (1,)), ((), ())), preferred_element_type=F32)


def _dot_tn(a, b):
    return lax.dot_general(a, b, (((0,), (0,)), ((), ())), preferred_element_type=F32)


def _sigmoid(v):
    return 1.0 / (1.0 + jnp.exp(-v))


def _rms_stat(x):
    return lax.rsqrt(jnp.mean(x * x, axis=-1, keepdims=True) + RMS_EPS)


def _rms_bwd(dn, x, r, gamma):
    dng = dn * gamma
    dx = r * dng - x * (r * r * r) * jnp.mean(dng * x, axis=-1, keepdims=True)
    return dx, jnp.sum(dn * x * r, axis=0, keepdims=True)


def _load_weights(wg_ref, plan, sems):
    copies = []
    for j, (dst, key) in enumerate(plan):
        off, n = PK_OFF[key], N_ROWS[key]
        for d in range(NDEV):
            copies.append(pltpu.make_async_copy(
                wg_ref.at[d, pl.ds(off, n), :], dst.at[pl.ds(d * n, n), :], sems.at[j * NDEV + d]))
    for cp in copies:
        cp.start()
    for cp in copies:
        cp.wait()


_ANY = pl.BlockSpec(memory_space=pl.ANY)


def _row_spec(tm, width, rev_nt=None):
    if rev_nt is None:
        return pl.BlockSpec((tm, width), lambda i: (i, 0))
    return pl.BlockSpec((tm, width), lambda i: (rev_nt - 1 - i, 0))


def _full_spec(shape):
    return pl.BlockSpec(shape, lambda *_: (0,) * len(shape))


def _sds(shape, dtype):
    return jax.ShapeDtypeStruct(shape, dtype)


def _mesh_pos():
    return lax.axis_index("x"), lax.axis_index("y"), lax.axis_index("c")


def _lin(p):
    return 4 * p[0] + 2 * p[1] + p[2]


def _ag_exchange(src, slot, send_sems, recv_sems, local_sem):
    x, y, c = _mesh_pos()
    me, sibling = (x, y, c), (x, y, 1 - c)
    chips = [(1 - x, y), (x, 1 - y), (1 - x, 1 - y)]

    def copy(k, block, to, own=False):
        return pltpu.make_async_remote_copy(
            src_ref=src if own else slot(block), dst_ref=slot(block),
            send_sem=send_sems.at[k], recv_sem=recv_sems.at[k], device_id=to, device_id_type=MESH)

    mine = pltpu.make_async_copy(src, slot(me), local_sem)
    first = [copy(0, me, sibling, own=True)]
    first += [copy(1 + j, me, (*chip, c), own=True) for j, chip in enumerate(chips)]
    passed = [copy(4 + j, (*chip, c), sibling) for j, chip in enumerate(chips)]

    def start():
        mine.start()
        for cp in first:
            cp.start()

    def finish():
        for j, chip in enumerate(chips):
            copy(1 + j, (*chip, c), me).wait_recv()
            passed[j].start()
        copy(0, sibling, me).wait_recv()
        for j, chip in enumerate(chips):
            copy(4 + j, (*chip, 1 - c), me).wait_recv()
        for cp in first + passed:
            cp.wait_send()
        mine.wait()

    return start, finish


def _chip_exchange(p_ref, land_ref, send_sems, recv_sems, keys, part):
    x, y, c = _mesh_pos()
    l0, rows = _span(LD_OFF, keys)
    rows = rows // part[1]
    p0 = part[0] * rows
    dst = land_ref.at[2 * x + y, pl.ds(l0 + p0, rows), :]
    peers = [((x + (j >> 1)) % 2, (y + (j & 1)) % 2) for j in range(NCHIP)]

    def copy(j):
        tx, ty = peers[j]
        src = p_ref.at[2 * tx + ty, pl.ds(p0, rows), :]
        if j == 0:
            return pltpu.make_async_copy(src, dst, send_sems.at[0])
        return pltpu.make_async_remote_copy(
            src_ref=src, dst_ref=dst, send_sem=send_sems.at[j], recv_sem=recv_sems.at[j],
            device_id=(tx, ty, c), device_id_type=MESH)

    def start():
        for j in range(NCHIP):
            copy(j).start()

    def finish():
        for j in range(NCHIP):
            copy(j).wait()

    return start, finish


def _all_to_all_f32(src_for, dst_ref, send_sems, recv_sems):
    x, y, c = _mesh_pos()
    dst = dst_ref.at[_lin((x, y, c))]
    peers = [((x + ((j >> 2) & 1)) % 2, (y + ((j >> 1) & 1)) % 2, (c + (j & 1)) % 2) for j in range(NDEV)]

    def copy(j):
        if j == 0:
            return pltpu.make_async_copy(src_for(_lin(peers[0])), dst, send_sems.at[0])
        return pltpu.make_async_remote_copy(
            src_ref=src_for(_lin(peers[j])), dst_ref=dst, send_sem=send_sems.at[j], recv_sem=recv_sems.at[j],
            device_id=peers[j], device_id_type=MESH)

    def start():
        for j in range(NDEV):
            copy(j).start()

    def finish():
        for j in range(NDEV):
            copy(j).wait()

    return start, finish


class _Comm:
    def __init__(self, ins, alias_in, out_shape, scratch, make, gives_wg):
        self.ins, self.alias_in, self.out_shape = ins, alias_in, out_shape
        self.scratch, self.make, self.gives_wg = scratch, make, gives_wg


def _ag_comm(shard, wg, keys):
    def make(c_ins, c_out, sc):
        return _weights_exchange(c_ins[0], c_out, sc[0], sc[1], sc[2], keys)

    return _Comm([shard, wg], 1, _sds(wg.shape, BF16),
                 [pltpu.SemaphoreType.DMA((7,)), pltpu.SemaphoreType.DMA((7,)), pltpu.SemaphoreType.DMA],
                 make, True)


def _weights_exchange(shard_ref, wg_ref, send_sems, recv_sems, local_sem, keys):
    r0, nr = _span(PK_OFF, keys)
    return _ag_exchange(shard_ref.at[pl.ds(r0, nr), :], lambda p: wg_ref.at[_lin(p), pl.ds(r0, nr), :],
                        send_sems, recv_sems, local_sem)


def _rs_comm(psum, keys, land, part=(0, 1), body_reads_land=False):
    def make(c_ins, c_out, sc):
        return _chip_exchange(c_ins[0], c_out, sc[0], sc[1], keys, part)

    ins = [psum] + ([] if land is None else [land])
    return _Comm(ins, None if land is None else 1, _sds((NCHIP, N_WROWS, D), BF16),
                 [pltpu.SemaphoreType.DMA((NCHIP,)), pltpu.SemaphoreType.DMA((NCHIP,))], make, body_reads_land)


def _hosted_call(body, name, nt, in_specs, out_specs, out_shape, scratch, args, comm):
    if comm is None:
        return _pcall(body, name=name, grid=(nt,), in_specs=in_specs, out_specs=tuple(out_specs),
                      out_shape=tuple(out_shape), scratch_shapes=scratch,
                      compiler_params=_cparams(("arbitrary",)))(*args)
    n_in, n_out, n_sc, n_cin = len(in_specs), len(out_specs), len(scratch), len(comm.ins)

    def wrapped(*refs):
        ins = refs[:n_in]
        c_ins = refs[n_in:n_in + n_cin]
        outs = refs[n_in + n_cin:n_in + n_cin + n_out]
        c_out = refs[n_in + n_cin + n_out]
        sc = refs[n_in + n_cin + n_out + 1:n_in + n_cin + n_out + 1 + n_sc]
        c_sc = refs[n_in + n_cin + n_out + 1 + n_sc:]
        start, finish = comm.make(c_ins, c_out, c_sc)

        @pl.when(pl.program_id(0) == 0)
        def _():
            start()

        if comm.gives_wg:
            body(*ins, c_out, *outs, *sc)
        else:
            body(*ins, *outs, *sc)

        @pl.when(pl.program_id(0) == nt - 1)
        def _():
            finish()

    aliases = {} if comm.alias_in is None else {n_in + comm.alias_in: n_out}
    res = _pcall(wrapped, name=name, grid=(nt,),
                 in_specs=list(in_specs) + [_ANY] * n_cin, out_specs=tuple(out_specs) + (_ANY,),
                 out_shape=tuple(out_shape) + (comm.out_shape,),
                 scratch_shapes=list(scratch) + list(comm.scratch),
                 input_output_aliases=aliases,
                 compiler_params=_cparams(("arbitrary",)))(*args, *comm.ins)
    return res


def _pack_shard(weights, smalls):
    plan = (("a_in", 0, 0, True), ("a_out", 1, 0, False), ("b_pw1", 2, 0, True), ("b_pw2", 3, 0, False),
            ("g0", 4, 0, False), ("u0", 5, 0, False), ("d0", 6, 0, False),
            ("g1", 4, 1, False), ("u1", 5, 1, False), ("d1", 6, 1, False))
    n_pad = 384

    def body(*refs):
        out, sm, pad = refs[-3:]
        for key, idx, layer, transposed in plan:
            n = N_ROWS[key]
            val = refs[idx][layer]
            if transposed:
                if n % 128:
                    pad[:, pl.ds(0, n)] = val
                    pad[:, pl.ds(n, n_pad - n)] = jnp.zeros((D, n_pad - n), F32)
                    val = pad[...]
                val = val.T[:n]
            out[pl.ds(PK_OFF[key], n), :] = val.astype(BF16)
        b_norm, b_b_pw1, b_conv, b_b_conv, b_ln_g, b_ln_b, b_b_pw2, a_conv = refs[len(weights):-3]
        sm[...] = jnp.zeros((SM_F32_ROWS, 128), F32)
        sm[pl.ds(SM_B_PW1, 1), :] = b_b_pw1[:, pl.ds(0, 128)]
        sm[pl.ds(SM_B_PW1 + 1, 1), :] = b_b_pw1[:, pl.ds(128, 128)]
        sm[pl.ds(SM_B_CONV, KB), :] = b_conv[0]
        sm[pl.ds(SM_A_CONV, KA), :] = a_conv[0]
        for row, ref in ((SM_B_NORM, b_norm), (SM_B_BCONV, b_b_conv), (SM_LN_G, b_ln_g), (SM_LN_B, b_ln_b),
                         (SM_B_PW2, b_b_pw2)):
            sm[pl.ds(row, 1), :] = ref[...]

    vm = pl.BlockSpec(memory_space=pltpu.VMEM)
    return _pcall(
        body, name="pack_shard",
        out_shape=(_sds((PK_ROWS, D), BF16), _sds((SM_F32_ROWS, 128), F32)),
        in_specs=[vm] * (len(weights) + len(smalls)), out_specs=(vm, vm),
        scratch_shapes=[pltpu.VMEM((D, n_pad), F32)],
        compiler_params=_cparams(),
    )(*weights, *smalls)


def _all_gather_first(shard, small):
    def body(x_ref, s_ref, wg_ref, sg_ref, send_w, recv_w, local_w, send_s, recv_s, local_s):
        start_w, finish_w = _weights_exchange(x_ref, wg_ref, send_w, recv_w, local_w, ("a_in", "a_out"))
        start_s, finish_s = _ag_exchange(s_ref, lambda p: sg_ref.at[_lin(p)], send_s, recv_s, local_s)
        start_s()
        start_w()
        finish_s()
        finish_w()

    sems = [pltpu.SemaphoreType.DMA((7,)), pltpu.SemaphoreType.DMA((7,)), pltpu.SemaphoreType.DMA]
    return _pcall(
        body, name="all_gather_first",
        out_shape=(_sds((NDEV, PK_ROWS, D), BF16), _sds((NDEV, SM_F32_ROWS, 128), F32)),
        in_specs=[_ANY, _ANY], out_specs=(_ANY, _ANY), scratch_shapes=sems + sems,
    )(shard, small)


def _pair_exchange(grads, keys, side=None):
    l0, rows = _span(LD_OFF, keys)
    n_side = 0 if side is None else 2

    def body(*refs):
        g_refs = refs[:len(keys)]
        got, send_sem, recv_sem = refs[len(keys) + n_side], refs[-2], refs[-1]
        if side is not None:
            sm_ref, rp_ref = refs[len(keys):len(keys) + 2]
            sm_land, rp_land = refs[len(keys) + 3:len(keys) + 5]
            s2, r2, s3, r3 = refs[len(keys) + 5:len(keys) + 9]
            start_s, finish_s = _all_to_all_f32(lambda d: sm_ref.at[d], sm_land, s2, r2)
            start_r, finish_r = _all_to_all_f32(lambda d: rp_ref.at[0], rp_land, s3, r3)
            start_s()
            start_r()
        x, y, c = _mesh_pos()
        sibling = (x, y, 1 - c)
        for ref, key in zip(g_refs, keys):
            for t in range(NCHIP):
                pltpu.make_async_remote_copy(
                    src_ref=ref.at[2 * t + (1 - c)], dst_ref=got.at[t, pl.ds(LD_OFF[key] - l0, N_ROWS[key]), :],
                    send_sem=send_sem, recv_sem=recv_sem, device_id=sibling, device_id_type=MESH).start()
        pltpu.make_async_remote_copy(src_ref=got, dst_ref=got, send_sem=send_sem, recv_sem=recv_sem,
                                     device_id=sibling, device_id_type=MESH).wait()
        if side is not None:
            finish_s()
            finish_r()

    out_shape = [_sds((NCHIP, rows, D), BF16)]
    scratch = []
    if side is not None:
        out_shape += [_sds((NDEV, SM_F32_ROWS, 128), F32), _sds((NDEV, 8, D), F32)]
        scratch += [pltpu.SemaphoreType.DMA((NDEV,))] * 4
    res = _pcall(
        body, name="pair_exchange_" + keys[0],
        out_shape=tuple(out_shape),
        in_specs=[_ANY] * (len(keys) + n_side), out_specs=tuple(_ANY for _ in out_shape),
        scratch_shapes=scratch + [pltpu.SemaphoreType.DMA, pltpu.SemaphoreType.DMA],
    )(*grads, *(side or ()))
    return res[0] if side is None else res


def _pair_add(grads, keys, got):
    l0, rows = _span(LD_OFF, keys)

    def body(c_ref, *refs):
        g_refs = refs[:len(keys)]
        got_ref, o_ref = refs[len(keys):]
        for ref, key in zip(g_refs, keys):
            sl = pl.ds(LD_OFF[key] - l0, N_ROWS[key])
            o_ref[sl, :] = (ref[...].astype(F32) + got_ref[sl, :].astype(F32)).astype(BF16)

    grid_spec = pltpu.PrefetchScalarGridSpec(
        num_scalar_prefetch=1, grid=(NCHIP,),
        in_specs=[pl.BlockSpec((None, N_ROWS[k], D), lambda t, c: (2 * t + c[0], 0, 0)) for k in keys]
        + [pl.BlockSpec((None, rows, D), lambda t, c: (t, 0, 0))],
        out_specs=pl.BlockSpec((None, rows, D), lambda t, c: (t, 0, 0)))
    core = lax.axis_index("c").astype(jnp.int32).reshape(1)
    return _pcall(
        body, name="pair_add_" + keys[0], grid_spec=grid_spec,
        out_shape=_sds((NCHIP, rows, D), BF16),
        compiler_params=_cparams(("arbitrary",)),
    )(core, *grads, got)


def _pair_reduce(grads, keys):
    return _pair_add(grads, keys, _pair_exchange(grads, keys))


def _mixa_fwd(x, gam, cw, wg, tm, comm):
    T = x.shape[0]

    def body(x_ref, gam_ref, cw_ref, wg_ref, h_ref, n_ref, b_ref, c_ref, v_ref, cc_ref, y_ref,
             win, wout, buf, sems):
        @pl.when(pl.program_id(0) == 0)
        def _():
            _load_weights(wg_ref, [(win, "a_in"), (wout, "a_out")], sems)
            buf[pl.ds(0, HALO_A), :] = jnp.zeros((HALO_A, D), F32)

        xv = x_ref[...]
        nb = (xv * _rms_stat(xv) * gam_ref[...]).astype(BF16)
        n_ref[...] = nb
        bv = _dot_nt(nb, win[pl.ds(0, D), :])
        cval = _dot_nt(nb, win[pl.ds(D, D), :])
        vval = _dot_nt(nb, win[pl.ds(2 * D, D), :])
        cv = cval * vval
        buf[pl.ds(HALO_A, tm), :] = cv
        cc = cw_ref[pl.ds(KA - 1, 1), :] * cv
        for k in range(KA - 1):
            cc = cc + cw_ref[pl.ds(k, 1), :] * buf[pl.ds(HALO_A - (KA - 1) + k, tm), :]
        buf[pl.ds(0, HALO_A), :] = buf[pl.ds(tm, HALO_A), :]
        yb = (bv * cc).astype(BF16)
        b_ref[...] = bv.astype(BF16)
        c_ref[...] = cval.astype(BF16)
        v_ref[...] = vval.astype(BF16)
        cc_ref[...] = cc.astype(BF16)
        y_ref[...] = yb
        h_ref[...] = xv + _dot(yb, wout[...])

    act = _sds((T, D), BF16)
    return _hosted_call(
        body, "mixa_fwd", T // tm,
        in_specs=[_row_spec(tm, D), _full_spec((1, D)), _full_spec((KA, D))],
        out_specs=[_row_spec(tm, D) for _ in range(7)],
        out_shape=(_sds((T, D), F32),) + (act,) * 6,
        scratch=[pltpu.VMEM((3 * D, D), BF16), pltpu.VMEM((D, D), BF16),
                 pltpu.VMEM((HALO_A + tm, D), F32), pltpu.SemaphoreType.DMA((2 * NDEV,))],
        args=(x, gam, cw), comm=comm)


def _ffn_fwd(h, gam, wg, layer, tm, comm, head=None):
    T = h.shape[0]
    kg, ku, kd = "g%d" % layer, "u%d" % layer, "d%d" % layer
    n_head = 0 if head is None else 2

    def body(*refs):
        h_ref, gam_ref = refs[:2]
        wg_ref = refs[2 + n_head]
        o_ref, n_ref, g_ref, u_ref, gu_ref = refs[3 + n_head:8 + n_head]
        wgt, wut, wd, sems = refs[-4:]

        @pl.when(pl.program_id(0) == 0)
        def _():
            _load_weights(wg_ref, [(wgt, kg), (wut, ku), (wd, kd)], sems)

        hv = h_ref[...]
        nb = (hv * _rms_stat(hv) * gam_ref[...]).astype(BF16)
        n_ref[...] = nb
        out = hv
        for f in range(2):
            cols = pl.ds(f * FH, FH)
            g = _dot_nt(nb, wgt[cols, :])
            u = _dot_nt(nb, wut[cols, :])
            gu = (g * _sigmoid(g) * u).astype(BF16)
            g_ref[:, cols] = g.astype(BF16)
            u_ref[:, cols] = u.astype(BF16)
            gu_ref[:, cols] = gu
            out = out + _dot(gu, wd[cols, :])
        if head is None:
            o_ref[...] = out
        else:
            t_ref, fin_ref, st_ref = refs[2], refs[3], refs[8 + n_head]

            @pl.when(pl.program_id(0) == 0)
            def _():
                st_ref[...] = jnp.zeros((8, D), F32)

            gamma = fin_ref[...]
            r = _rms_stat(out)
            err = out * r * gamma - t_ref[...]
            dx, dgam = _rms_bwd(err * (1.0 / D), out, r, gamma)
            o_ref[...] = dx
            st_ref[pl.ds(0, 1), :] += dgam
            st_ref[pl.ds(1, 1), :] += (0.5 / D) * jnp.sum(err * err, axis=0, keepdims=True)

    pre = _sds((T, FF), BF16)
    in_specs = [_row_spec(tm, D), _full_spec((1, D))]
    args = (h, gam)
    out_specs = [_row_spec(tm, D), _row_spec(tm, D), _row_spec(tm, FF), _row_spec(tm, FF), _row_spec(tm, FF)]
    out_shape = (_sds((T, D), F32), _sds((T, D), BF16), pre, pre, pre)
    if head is not None:
        in_specs, args = in_specs + [_row_spec(tm, D), _full_spec((1, D))], args + tuple(head)
        out_specs, out_shape = out_specs + [_full_spec((8, D))], out_shape + (_sds((8, D), F32),)
    if comm is None:
        in_specs, args = in_specs + [_ANY], args + (wg,)
    return _hosted_call(
        body, "ffn%d_fwd" % layer, T // tm,
        in_specs=in_specs, out_specs=out_specs, out_shape=out_shape,
        scratch=[pltpu.VMEM((FF, D), BF16)] * 3 + [pltpu.SemaphoreType.DMA((3 * NDEV,))],
        args=args, comm=comm)


def _shifted_copies(buf, shf, tm):
    for r in range(1, 8):
        shf[r - 1] = buf[pl.ds(r, tm + HALO_B - 8), :]


def _broadcast_taps(vec_ref, wb):
    for k in range(KB):
        wb[k] = jnp.broadcast_to(vec_ref[pl.ds(SM_B_CONV + k, 1), :], (8, D))
    wb[KB] = jnp.broadcast_to(vec_ref[pl.ds(SM_B_BCONV, 1), :], (8, D))


def _taps_by_shift_residue(taps):
    groups = {}
    for k, shift in taps:
        q, r = divmod(shift, 8)
        groups.setdefault(r, []).append((k, q))
    return sorted(groups.items())


def _window(buf, shf, base, r, q0, n_groups, lanes):
    rows = pl.ds(base + 8 * q0, 8 * n_groups)
    v = buf[rows, lanes] if r == 0 else shf[r - 1, rows, lanes]
    return [v[8 * i:8 * i + 8] for i in range(n_groups)]


def _long_conv(buf, shf, wb, out_ref, tm, taps, bias_row):
    n_acc = CONV_ROWS // 8
    groups = _taps_by_shift_residue(taps)
    for col in range(D // 128):
        lanes = pl.ds(128 * col, 128)

        def rows(i, carry, lanes=lanes):
            base = i * CONV_ROWS
            init = jnp.zeros((8, 128), F32) if bias_row is None else wb[bias_row, :, lanes]
            accs = [init] * n_acc
            for r, lst in groups:
                q0, q1 = min(q for _, q in lst), max(q for _, q in lst)
                win = _window(buf, shf, base, r, q0, n_acc + q1 - q0, lanes)
                for k, q in lst:
                    wk = wb[k, :, lanes]
                    accs = [acc + wk * win[h + q - q0] for h, acc in enumerate(accs)]
            for h, acc in enumerate(accs):
                out_ref[pl.ds(base + 8 * h, 8), lanes] = acc
            return carry

        for i in range(tm // CONV_ROWS):
            rows(i, 0)


def _long_conv_grad_taps(buf, shf, x_ref, wacc, tm):
    n_acc = CONV_ROWS // 8
    groups = _taps_by_shift_residue([(KB - 1 - j, j) for j in range(KB)])
    for col in range(D // 128):
        lanes = pl.ds(128 * col, 128)

        def rows(i, carry, lanes=lanes):
            base = i * CONV_ROWS
            xv = x_ref[pl.ds(base, CONV_ROWS), lanes]
            xs = [xv[8 * h:8 * h + 8] for h in range(n_acc)]
            for r, lst in groups:
                q0, q1 = min(q for _, q in lst), max(q for _, q in lst)
                win = _window(buf, shf, base, r, q0, n_acc + q1 - q0, lanes)
                for k, q in lst:
                    prod = [x * win[h + q - q0] for h, x in enumerate(xs)]
                    wacc[k, :, lanes] += functools.reduce(lambda a, b: a + b, prod)
            return carry

        for i in range(tm // CONV_ROWS):
            rows(i, 0)


def _ln_stats(dc):
    mu = jnp.mean(dc, axis=-1, keepdims=True)
    xc = dc - mu
    rstd = lax.rsqrt(jnp.mean(xc * xc, axis=-1, keepdims=True) + LN_EPS)
    return xc * rstd, rstd


def _mixb_fwd(h, vecs, bias1, wg, tm, comm):
    T = h.shape[0]

    def body(h_ref, vec_ref, b1_ref, wg_ref, o_ref, n_ref, a_ref, g_ref, dc_ref, s_ref, w1, w2, buf, shf, wb, sems):
        @pl.when(pl.program_id(0) == 0)
        def _():
            _load_weights(wg_ref, [(w1, "b_pw1"), (w2, "b_pw2")], sems)
            buf[pl.ds(0, HALO_B), :] = jnp.zeros((HALO_B, D), F32)
            _broadcast_taps(vec_ref, wb)

        hv = h_ref[...]
        nb = (hv * _rms_stat(hv) * vec_ref[pl.ds(SM_B_NORM, 1), :]).astype(BF16)
        n_ref[...] = nb
        a = _dot_nt(nb, w1[pl.ds(0, D), :]) + b1_ref[:, pl.ds(0, D)]
        g = _dot_nt(nb, w1[pl.ds(D, D), :]) + b1_ref[:, pl.ds(D, D)]
        a_ref[...] = a.astype(BF16)
        g_ref[...] = g.astype(BF16)
        buf[pl.ds(HALO_B, tm), :] = a * _sigmoid(g)
        _shifted_copies(buf, shf, tm)

        _long_conv(buf, shf, wb, dc_ref, tm, [(k, HALO_B - (KB - 1) + k) for k in range(KB)], KB)
        buf[pl.ds(0, HALO_B), :] = buf[pl.ds(tm, HALO_B), :]
        xhat, _ = _ln_stats(dc_ref[...])
        ln = xhat * vec_ref[pl.ds(SM_LN_G, 1), :] + vec_ref[pl.ds(SM_LN_B, 1), :]
        s = (ln * _sigmoid(ln)).astype(BF16)
        s_ref[...] = s
        o_ref[...] = hv + _dot(s, w2[...]) + vec_ref[pl.ds(SM_B_PW2, 1), :]

    act = _sds((T, D), BF16)
    return _hosted_call(
        body, "mixb_fwd", T // tm,
        in_specs=[_row_spec(tm, D), _full_spec((SM_F32_ROWS, D)), _full_spec((1, 2 * D))],
        out_specs=[_row_spec(tm, D) for _ in range(6)],
        out_shape=(_sds((T, D), F32), act, act, act, _sds((T, D), F32), act),
        scratch=[pltpu.VMEM((2 * D, D), BF16), pltpu.VMEM((D, D), BF16),
                 pltpu.VMEM((HALO_B + tm, D), F32), pltpu.VMEM((7, HALO_B + tm - 8, D), F32),
                 pltpu.VMEM((KB + 1, 8, D), F32), pltpu.SemaphoreType.DMA((2 * NDEV,))],
        args=(h, vecs, bias1), comm=comm)


def _ffn_bwd_dx(dh, h, g, u, gam, wg, layer, tm, comm):
    T = h.shape[0]
    kg, ku, kd = "g%d" % layer, "u%d" % layer, "d%d" % layer

    def body(dh_ref, h_ref, g_ref, u_ref, gam_ref, wg_ref, o_ref, dg_ref, du_ref, st_ref, wgt, wut, wd, sems):
        @pl.when(pl.program_id(0) == 0)
        def _():
            _load_weights(wg_ref, [(wgt, kg), (wut, ku), (wd, kd)], sems)
            st_ref[...] = jnp.zeros((8, D), F32)

        dhv = dh_ref[...]
        dhb = dhv.astype(BF16)
        dn = jnp.zeros_like(dhv)
        for f in range(2):
            cols = pl.ds(f * FH, FH)
            dgu = _dot_nt(dhb, wd[cols, :])
            gv = g_ref[:, cols].astype(F32)
            sg = _sigmoid(gv)
            du = (dgu * gv * sg).astype(BF16)
            dg = (dgu * u_ref[:, cols].astype(F32) * (sg * (1.0 + gv * (1.0 - sg)))).astype(BF16)
            dg_ref[:, cols] = dg
            du_ref[:, cols] = du
            dn = dn + _dot(dg, wgt[cols, :]) + _dot(du, wut[cols, :])
        hv = h_ref[...]
        dx, dgam = _rms_bwd(dn, hv, _rms_stat(hv), gam_ref[...])
        o_ref[...] = dhv + dx
        st_ref[pl.ds(0, 1), :] += dgam

    pre = _sds((T, FF), BF16)
    return _hosted_call(
        body, "ffn%d_bwd_dx" % layer, T // tm,
        in_specs=[_row_spec(tm, D), _row_spec(tm, D), _row_spec(tm, FF), _row_spec(tm, FF),
                  _full_spec((1, D)), _ANY],
        out_specs=[_row_spec(tm, D), _row_spec(tm, FF), _row_spec(tm, FF), _full_spec((8, D))],
        out_shape=(_sds((T, D), F32), pre, pre, _sds((8, D), F32)),
        scratch=[pltpu.VMEM((FF, D), BF16)] * 3 + [pltpu.SemaphoreType.DMA((3 * NDEV,))],
        args=(dh, h, g, u, gam, wg), comm=comm)


def _grad_w(lhs, rhs, mc, name, tm):
    T, M = lhs.shape
    nt = T // tm

    def body(l_ref, r_ref, o_ref, acc):
        i = pl.program_id(1)

        @pl.when(i == 0)
        def _():
            acc[...] = jnp.zeros((mc, D), F32)

        acc[...] += _dot_tn(l_ref[...], r_ref[...].astype(BF16))

        @pl.when(i == nt - 1)
        def _():
            o_ref[...] = acc[...].astype(BF16)

    return _pcall(
        body, name=name, grid=(M // mc, nt),
        out_shape=_sds((M, D), BF16),
        in_specs=[pl.BlockSpec((tm, mc), lambda j, i: (i, j)), pl.BlockSpec((tm, D), lambda j, i: (i, 0))],
        out_specs=pl.BlockSpec((mc, D), lambda j, i: (j, 0)),
        scratch_shapes=[pltpu.VMEM((mc, D), F32)],
        compiler_params=_cparams(("arbitrary", "arbitrary")),
    )(lhs, rhs)


def _mixb_bwd(dh, h, a, g, dc, vecs, wg, tm, comm):
    T = h.shape[0]
    nt = T // tm

    def body(dh_ref, h_ref, a_ref, g_ref, dc_ref, vec_ref, wg_ref, o_ref, du_ref, st_ref, sb_ref,
             w1, w2, buf, shf, glu_s, dglu_s, wacc, wb, sems):
        @pl.when(pl.program_id(0) == 0)
        def _():
            _load_weights(wg_ref, [(w1, "b_pw1"), (w2, "b_pw2")], sems)
            buf[pl.ds(tm, HALO_B), :] = jnp.zeros((HALO_B, D), F32)
            wacc[...] = jnp.zeros((KB, 8, D), F32)
            _broadcast_taps(vec_ref, wb)
            st_ref[...] = jnp.zeros((SM_F32_ROWS, D), F32)
            sb_ref[...] = jnp.zeros((8, 2 * D), F32)

        def acc(row, val):
            st_ref[pl.ds(row, 1), :] += jnp.sum(val, axis=0, keepdims=True)

        dhv = dh_ref[...]
        acc(SM_B_PW2, dhv)
        ds = _dot_nt(dhv.astype(BF16), w2[...])
        xhat, rstd = _ln_stats(dc_ref[...])
        ln_g = vec_ref[pl.ds(SM_LN_G, 1), :]
        ln = xhat * ln_g + vec_ref[pl.ds(SM_LN_B, 1), :]
        sl = _sigmoid(ln)
        dln = ds * (sl * (1.0 + ln * (1.0 - sl)))
        acc(SM_LN_G, dln * xhat)
        acc(SM_LN_B, dln)
        dxh = dln * ln_g
        ddc = rstd * (dxh - jnp.mean(dxh, axis=-1, keepdims=True)
                      - xhat * jnp.mean(dxh * xhat, axis=-1, keepdims=True))
        acc(SM_B_BCONV, ddc)
        buf[pl.ds(0, tm), :] = ddc
        _shifted_copies(buf, shf, tm)
        av = a_ref[...].astype(F32)
        sg = _sigmoid(g_ref[...].astype(F32))
        glu_s[...] = av * sg

        _long_conv(buf, shf, wb, dglu_s, tm, [(KB - 1 - j, j) for j in range(KB)], None)
        _long_conv_grad_taps(buf, shf, glu_s, wacc, tm)
        buf[pl.ds(tm, HALO_B), :] = buf[pl.ds(0, HALO_B), :]
        dglu = dglu_s[...]
        da = dglu * sg
        dg = dglu * av * sg * (1.0 - sg)
        sb_ref[pl.ds(0, 1), pl.ds(0, D)] += jnp.sum(da, axis=0, keepdims=True)
        sb_ref[pl.ds(0, 1), pl.ds(D, D)] += jnp.sum(dg, axis=0, keepdims=True)
        dab, dgb = da.astype(BF16), dg.astype(BF16)
        du_ref[:, pl.ds(0, D)] = dab
        du_ref[:, pl.ds(D, D)] = dgb
        dn = _dot(dab, w1[pl.ds(0, D), :]) + _dot(dgb, w1[pl.ds(D, D), :])
        hv = h_ref[...]
        dx, dgam = _rms_bwd(dn, hv, _rms_stat(hv), vec_ref[pl.ds(SM_B_NORM, 1), :])
        o_ref[...] = dhv + dx
        st_ref[pl.ds(SM_B_NORM, 1), :] += dgam

        @pl.when(pl.program_id(0) == nt - 1)
        def _():
            st_ref[pl.ds(SM_B_CONV, KB), :] = jnp.sum(wacc[...], axis=1)

    rs = functools.partial(_row_spec, rev_nt=nt)
    return _hosted_call(
        body, "mixb_bwd", nt,
        in_specs=[rs(tm, D), rs(tm, D), rs(tm, D), rs(tm, D), rs(tm, D), _full_spec((SM_F32_ROWS, D)), _ANY],
        out_specs=[rs(tm, D), rs(tm, 2 * D), _full_spec((SM_F32_ROWS, D)), _full_spec((8, 2 * D))],
        out_shape=(_sds((T, D), F32), _sds((T, 2 * D), BF16), _sds((SM_F32_ROWS, D), F32), _sds((8, 2 * D), F32)),
        scratch=[pltpu.VMEM((2 * D, D), BF16), pltpu.VMEM((D, D), BF16),
                 pltpu.VMEM((tm + HALO_B, D), F32), pltpu.VMEM((7, tm + HALO_B - 8, D), F32),
                 pltpu.VMEM((tm, D), F32), pltpu.VMEM((tm, D), F32), pltpu.VMEM((KB, 8, D), F32),
                 pltpu.VMEM((KB + 1, 8, D), F32), pltpu.SemaphoreType.DMA((2 * NDEV,))],
        args=(dh, h, a, g, dc, vecs, wg), comm=comm)


def _mixa_bwd(dh, x, b, c, v, cc, gam, cw, wg, tm, comm):
    T = x.shape[0]
    nt = T // tm

    def body(dh_ref, x_ref, b_ref, c_ref, v_ref, cc_ref, gam_ref, cw_ref, wg_ref,
             o_ref, db_ref, st_ref, win, wout, buf, sems):
        @pl.when(pl.program_id(0) == 0)
        def _():
            _load_weights(wg_ref, [(win, "a_in"), (wout, "a_out")], sems)
            buf[pl.ds(tm, HALO_A), :] = jnp.zeros((HALO_A, D), F32)
            st_ref[...] = jnp.zeros((8, D), F32)

        dhv = dh_ref[...]
        dy = _dot_nt(dhv.astype(BF16), wout[...])
        cval = c_ref[...].astype(F32)
        vval = v_ref[...].astype(F32)
        d_b = (dy * cc_ref[...].astype(F32)).astype(BF16)
        buf[pl.ds(0, tm), :] = dy * b_ref[...].astype(F32)
        cv = cval * vval
        dcv = jnp.zeros((tm, D), F32)
        for j in range(KA):
            sh = buf[pl.ds(j, tm), :]
            k = KA - 1 - j
            dcv = dcv + cw_ref[pl.ds(k, 1), :] * sh
            st_ref[pl.ds(1 + k, 1), :] += jnp.sum(cv * sh, axis=0, keepdims=True)
        buf[pl.ds(tm, HALO_A), :] = buf[pl.ds(0, HALO_A), :]
        d_c = (dcv * vval).astype(BF16)
        d_v = (dcv * cval).astype(BF16)
        db_ref[:, pl.ds(0, D)] = d_b
        db_ref[:, pl.ds(D, D)] = d_c
        db_ref[:, pl.ds(2 * D, D)] = d_v
        dn = _dot(d_b, win[pl.ds(0, D), :]) + _dot(d_c, win[pl.ds(D, D), :]) + _dot(d_v, win[pl.ds(2 * D, D), :])
        xv = x_ref[...]
        dx, dgam = _rms_bwd(dn, xv, _rms_stat(xv), gam_ref[...])
        o_ref[...] = dhv + dx
        st_ref[pl.ds(0, 1), :] += dgam

    rs = functools.partial(_row_spec, rev_nt=nt)
    return _hosted_call(
        body, "mixa_bwd", nt,
        in_specs=[rs(tm, D) for _ in range(6)] + [_full_spec((1, D)), _full_spec((KA, D)), _ANY],
        out_specs=[rs(tm, D), rs(tm, 3 * D), _full_spec((8, D))],
        out_shape=(_sds((T, D), F32), _sds((T, 3 * D), BF16), _sds((8, D), F32)),
        scratch=[pltpu.VMEM((3 * D, D), BF16), pltpu.VMEM((D, D), BF16),
                 pltpu.VMEM((tm + HALO_A, D), F32), pltpu.SemaphoreType.DMA((2 * NDEV,))],
        args=(dh, x, b, c, v, cc, gam, cw, wg), comm=comm)


def _sum_small(sm, rp):
    def body(sm_ref, rp_ref, osm, orp, oloss):
        a = sm_ref[0]
        b = rp_ref[0]
        for s in range(1, NDEV):
            a = a + sm_ref[s]
            b = b + rp_ref[s]
        osm[...] = a
        orp[...] = b
        oloss[...] = jnp.zeros((8, 128), F32) + jnp.sum(b[4:5, :], axis=-1, keepdims=True)

    return _pcall(
        body, name="sum_small_grads",
        out_shape=(_sds((SM_F32_ROWS, 128), F32), _sds((8, D), F32), _sds((8, 128), F32)),
        in_specs=[pl.BlockSpec(memory_space=pltpu.VMEM)] * 2,
        out_specs=tuple(pl.BlockSpec(memory_space=pltpu.VMEM) for _ in range(3)),
        compiler_params=_cparams(),
    )(sm, rp)


def _adam_math(w, g, m, v):
    mn = B1 * m + (1.0 - B1) * g
    vn = B2 * v + (1.0 - B2) * (g * g)
    m_hat = mn / (1.0 - B1 ** STEP)
    v_hat = vn / (1.0 - B2 ** STEP)
    return -LR * (m_hat / (jnp.sqrt(v_hat) + ADAM_EPS) + WD * w), mn, vn


def _finish_weight(land, keys, transposed, w, m, v, name, comm=None):
    layers, rows, cols = w.shape
    n = N_ROWS[keys[0]]
    n_pad = -(-n // 128) * 128 if transposed else n

    def body(w_ref, m_ref, v_ref, land_ref, og, od, om, ov, buf, sem):
        off = LD_OFF[keys[0]]
        if layers == 2:
            off = jnp.where(pl.program_id(0) == 0, off, LD_OFF[keys[1]])
        cp = pltpu.make_async_copy(land_ref.at[:, pl.ds(pl.multiple_of(off, 32), n), :], buf, sem)
        cp.start()
        cp.wait()
        g = buf[0].astype(F32)
        for s in range(1, NCHIP):
            g = g + buf[s].astype(F32)
        if transposed:
            if n_pad != n:
                g = jnp.concatenate([g, jnp.zeros((n_pad - n, D), F32)], axis=0)
            g = g.T[:, :n]
        d, mn, vn = _adam_math(w_ref[...], g, m_ref[...], v_ref[...])
        og[...] = g
        od[...] = d
        om[...] = mn
        ov[...] = vn

    spec = pl.BlockSpec((None, rows, cols), lambda l: (l, 0, 0))
    shp = _sds(w.shape, F32)
    in_specs, args = [spec, spec, spec], (w, m, v)
    if comm is None:
        in_specs, args = in_specs + [_ANY], args + (land,)
    return _hosted_call(
        body, "finish_" + name, layers,
        in_specs=in_specs, out_specs=[spec] * 4, out_shape=(shp,) * 4,
        scratch=[pltpu.VMEM((NCHIP, n, D), BF16), pltpu.SemaphoreType.DMA],
        args=args, comm=comm)


SMALL_PARAMS = (("sm", SM_B_NORM, 1), ("sm", SM_B_PW1, 2), ("sm", SM_B_CONV, KB), ("sm", SM_B_BCONV, 1),
                ("sm", SM_LN_G, 1), ("sm", SM_LN_B, 1), ("sm", SM_B_PW2, 1), ("sm", SM_A_CONV, KA),
                ("rp", 0, 1), ("rp", 1, 2), ("rp", 3, 1))


def _adamw_small(g_small, g_repl, triples):
    n = len(SMALL_PARAMS)

    def body(*refs):
        gs_ref, gr_ref = refs[0], refs[1]
        ins = refs[2:2 + 3 * n]
        outs = refs[2 + 3 * n:]
        for i, (src, r0, nr) in enumerate(SMALL_PARAMS):
            w_ref, m_ref, v_ref = ins[3 * i:3 * i + 3]
            g = (gs_ref if src == "sm" else gr_ref)[pl.ds(r0, nr), :]
            if i == 1:
                g = jnp.concatenate([g[0:1], g[1:2]], axis=1)
            lead = (0,) if len(w_ref.shape) == 3 else ()
            idx = lead + (slice(None), slice(None))
            vals = (g,) + _adam_math(w_ref[idx], g, m_ref[idx], v_ref[idx])
            for o_ref, val in zip(outs[4 * i:4 * i + 4], vals):
                o_ref[idx] = val

    flat = [a for t in triples for a in t]
    out_shape = tuple(_sds(t[0].shape, F32) for t in triples for _ in range(4))
    vm = pl.BlockSpec(memory_space=pltpu.VMEM)
    res = _pcall(
        body, name="adamw_small", out_shape=out_shape,
        in_specs=[vm] * (2 + len(flat)), out_specs=tuple(vm for _ in out_shape),
        compiler_params=_cparams(),
    )(g_small, g_repl, *flat)
    return [tuple(res[4 * i:4 * i + 4]) for i in range(n)]


def kernel(x, a_norm, a_w_in, a_conv, a_w_out, b_norm, b_w_pw1, b_b_pw1, b_conv, b_b_conv, b_ln_g, b_ln_b, b_w_pw2, b_b_pw2, ffn_norm, ffn_w_gate, ffn_w_up, ffn_w_down, final_norm, loss_target, m_a_norm, m_a_w_in, m_a_conv, m_a_w_out, m_b_norm, m_b_w_pw1, m_b_b_pw1, m_b_conv, m_b_b_conv, m_b_ln_g, m_b_ln_b, m_b_w_pw2, m_b_b_pw2, m_ffn_norm, m_ffn_w_gate, m_ffn_w_up, m_ffn_w_down, m_final_norm, v_a_norm, v_a_w_in, v_a_conv, v_a_w_out, v_b_norm, v_b_w_pw1, v_b_b_pw1, v_b_conv, v_b_b_conv, v_b_ln_g, v_b_ln_b, v_b_w_pw2, v_b_b_pw2, v_ffn_norm, v_ffn_w_gate, v_ffn_w_up, v_ffn_w_down, v_final_norm):
    T = x.shape[1]
    tm = min(TM, T)
    tma = min(TM_A, T)
    tw = min(TM_DW, T)
    xs = x.reshape(T, D)
    tgt = loss_target.reshape(T, D)

    def rows_first(a):
        return jnp.swapaxes(a, 1, 2)

    shard, small = _pack_shard(
        (a_w_in, a_w_out, b_w_pw1, b_w_pw2, rows_first(ffn_w_gate), rows_first(ffn_w_up), ffn_w_down),
        (b_norm, b_b_pw1, b_conv, b_b_conv, b_ln_g, b_ln_b, b_b_pw2, a_conv))
    wg, sm_all = _all_gather_first(shard, small)
    vecs = sm_all.transpose(1, 0, 2).reshape(SM_F32_ROWS, D)
    bias1 = sm_all[:, SM_B_PW1:SM_B_PW1 + 2, :].reshape(1, 2 * D)
    cw_a = vecs[SM_A_CONV:SM_A_CONV + KA]
    fn0, fn1 = ffn_norm[0:1], ffn_norm[1:2]
    fin = final_norm.reshape(1, D)

    h1, n0, bq, cq, vq, ccq, yq, wg = _mixa_fwd(xs, a_norm, cw_a, wg, tma, _ag_comm(shard, wg, ("g0", "u0", "d0")))
    h2, n1, g0, u0, gu0, wg = _ffn_fwd(h1, fn0, wg, 0, tma, _ag_comm(shard, wg, ("b_pw1", "b_pw2")))
    h3, n2, aq, gq, dcq, sq, wg = _mixb_fwd(h2, vecs, bias1, wg, tm, _ag_comm(shard, wg, ("g1", "u1", "d1")))
    dh4, n3, g1, u1, gu1, st_fin = _ffn_fwd(h3, fn1, wg, 1, tma, None, head=(tgt, fin))

    def by_dest(gw, key):
        return gw.reshape(NDEV, N_ROWS[key], D)

    gw_d1 = by_dest(_grad_w(gu1, dh4, FH, "grad_down1", tw), "d1")
    dh3, dg1, du1, st_f1 = _ffn_bwd_dx(dh4, h3, g1, u1, fn1, wg, 1, tm, None)
    gw_g1 = by_dest(_grad_w(dg1, n3, FH, "grad_gate1", tw), "g1")
    gw_u1 = by_dest(_grad_w(du1, n3, FH, "grad_up1", tw), "u1")
    keys = ("g1", "u1", "d1")
    dh2, dub, st_b, st_b1, land = _mixb_bwd(
        dh3, h2, aq, gq, dcq, vecs, wg, tm, _rs_comm(_pair_reduce([gw_g1, gw_u1, gw_d1], keys), keys, None))
    gw_pw1 = by_dest(_grad_w(dub, n2, D, "grad_pw1", tw), "b_pw1")
    gw_pw2 = by_dest(_grad_w(sq, dh3, D, "grad_pw2", tw), "b_pw2")
    gw_d0 = by_dest(_grad_w(gu0, dh2, FH, "grad_down0", tw), "d0")
    keys = ("b_pw1", "b_pw2", "d0")
    dh1, dg0, du0, st_f0, land = _ffn_bwd_dx(
        dh2, h1, g0, u0, fn0, wg, 0, tm, _rs_comm(_pair_reduce([gw_pw1, gw_pw2, gw_d0], keys), keys, land))
    gw_g0 = by_dest(_grad_w(dg0, n1, FH, "grad_gate0", tw), "g0")
    gw_u0 = by_dest(_grad_w(du0, n1, FH, "grad_up0", tw), "u0")
    gw_out = by_dest(_grad_w(yq, dh1, D, "grad_out", tw), "a_out")
    keys = ("g0", "u0", "a_out")
    dx, dbcv, st_a, land = _mixa_bwd(
        dh1, xs, bq, cq, vq, ccq, a_norm, cw_a, wg, tma,
        _rs_comm(_pair_reduce([gw_g0, gw_u0, gw_out], keys), keys, land))
    gw_in = by_dest(_grad_w(dbcv, n0, D, "grad_in", tw), "a_in")

    st_small = st_b.at[SM_A_CONV:SM_A_CONV + KA].set(st_a[1:1 + KA])
    sm_dest = st_small.reshape(SM_F32_ROWS, NDEV, 128).transpose(1, 0, 2)
    sm_dest = sm_dest.at[:, SM_B_PW1:SM_B_PW1 + 2, :].set(st_b1[0].reshape(NDEV, 2, 128))
    repl = jnp.concatenate([st_a[0:1], st_f0[0:1], st_f1[0:1], st_fin[0:1], st_fin[1:2],
                            jnp.zeros((3, D), F32)], axis=0)[None]
    got_in, sm_land, rp_land = _pair_exchange([gw_in], ("a_in",), side=(sm_dest, repl))
    p_in = _pair_add([gw_in], ("a_in",), got_in)

    big = [("ffn_w_gate", ("g0", "g1"), False,
            rows_first(ffn_w_gate), rows_first(m_ffn_w_gate), rows_first(v_ffn_w_gate)),
           ("ffn_w_up", ("u0", "u1"), False, rows_first(ffn_w_up), rows_first(m_ffn_w_up), rows_first(v_ffn_w_up)),
           ("ffn_w_down", ("d0", "d1"), False, ffn_w_down, m_ffn_w_down, v_ffn_w_down),
           ("a_w_out", ("a_out",), False, a_w_out, m_a_w_out, v_a_w_out),
           ("b_w_pw1", ("b_pw1",), True, b_w_pw1, m_b_w_pw1, v_b_w_pw1),
           ("b_w_pw2", ("b_pw2",), False, b_w_pw2, m_b_w_pw2, v_b_w_pw2),
           ("a_w_in", ("a_in",), True, a_w_in, m_a_w_in, v_a_w_in)]
    res = {}
    for k, (name, keys, transposed, w, m, v) in enumerate(big):
        if k < 3:
            comm = _rs_comm(p_in, ("a_in",), land, part=(k, 3), body_reads_land=True)
            *res[name], land = _finish_weight(land, keys, transposed, w, m, v, name, comm)
        else:
            res[name] = _finish_weight(land, keys, transposed, w, m, v, name)
    for name in ("ffn_w_gate", "ffn_w_up"):
        res[name] = [rows_first(a) for a in res[name]]

    g_small, g_repl, loss8 = _sum_small(sm_land, rp_land)
    loss = loss8[0, 0]

    small_names = ["b_norm", "b_b_pw1", "b_conv", "b_b_conv", "b_ln_g", "b_ln_b", "b_b_pw2", "a_conv",
                   "a_norm", "ffn_norm", "final_norm"]
    triples = [(b_norm, m_b_norm, v_b_norm), (b_b_pw1, m_b_b_pw1, v_b_b_pw1), (b_conv, m_b_conv, v_b_conv),
               (b_b_conv, m_b_b_conv, v_b_b_conv), (b_ln_g, m_b_ln_g, v_b_ln_g), (b_ln_b, m_b_ln_b, v_b_ln_b),
               (b_b_pw2, m_b_b_pw2, v_b_b_pw2), (a_conv, m_a_conv, v_a_conv), (a_norm, m_a_norm, v_a_norm),
               (ffn_norm, m_ffn_norm, v_ffn_norm),
               (fin, m_final_norm.reshape(1, D), v_final_norm.reshape(1, D))]
    for name, quad in zip(small_names, _adamw_small(g_small, g_repl, triples)):
        res[name] = quad
    res["final_norm"] = tuple(a.reshape(D) for a in res["final_norm"])

    order = ["a_norm", "a_w_in", "a_conv", "a_w_out", "b_norm", "b_w_pw1", "b_b_pw1", "b_conv", "b_b_conv",
             "b_ln_g", "b_ln_b", "b_w_pw2", "b_b_pw2", "ffn_norm", "ffn_w_gate", "ffn_w_up", "ffn_w_down",
             "final_norm"]
    out = [loss, dx.reshape(1, T, D)]
    for j in range(4):
        out += [res[k][j] for k in order]
    return tuple(out)
```

```python
import functools

import jax
import jax.numpy as jnp
from jax import lax
from jax.experimental import pallas as pl
from jax.experimental.pallas import tpu as pltpu

F32 = jnp.float32
BF16 = jnp.bfloat16
MESH = pl.DeviceIdType.MESH

D = 1024
FF = 2816
FH = FF // 2
NDEV = 8
KA = 3
KB = 31
HALO_A = 8
HALO_B = 32
CONV_ROWS = 64
RMS_EPS = 1e-6
LN_EPS = 1e-5
LR, B1, B2, ADAM_EPS, WD, STEP = 0.001, 0.9, 0.999, 1e-08, 0.01, 10

TM = 256
TM_A = 512
TM_DW = 1024
VMEM_LIMIT = 56 * 1024 * 1024

N_ROWS = {"a_in": 384, "a_out": 128, "b_pw1": 256, "b_pw2": 128, "g0": 352, "u0": 352, "d0": 352,
          "g1": 352, "u1": 352, "d1": 352}
WEIGHT_KEYS = ("a_in", "a_out", "b_pw1", "b_pw2", "g0", "u0", "d0", "g1", "u1", "d1")
NCHIP = 4


def _offsets(order):
    off, o = {}, 0
    for k in order:
        off[k] = o
        o += N_ROWS[k]
    return off, o


PK_ORDER = ("a_in", "a_out", "g0", "u0", "d0", "b_pw1", "b_pw2", "g1", "u1", "d1")
PK_OFF, PK_ROWS = _offsets(PK_ORDER)
LD_ORDER = ("g1", "u1", "d1", "b_pw1", "b_pw2", "d0", "g0", "u0", "a_out", "a_in")
LD_OFF, N_WROWS = _offsets(LD_ORDER)


def _span(off, keys):
    return off[keys[0]], sum(N_ROWS[k] for k in keys)


SM_B_NORM, SM_B_PW1, SM_B_CONV, SM_B_BCONV, SM_LN_G, SM_LN_B, SM_B_PW2, SM_A_CONV = 0, 1, 3, 34, 35, 36, 37, 38
SM_USED = 41
SM_F32_ROWS = 64


def _pcall(body, **kw):
    return pl.pallas_call(body, **kw)


def _cparams(sem=None):
    return pltpu.CompilerParams(dimension_semantics=sem, vmem_limit_bytes=VMEM_LIMIT)


def _dot(a, b):
    return jnp.dot(a, b, preferred_element_type=F32)


def _dot_nt(a, b):
    return lax.dot_general(a, b, (((1,), (1,)), ((), ())), preferred_element_type=F32)


def _dot_tn(a, b):
    return lax.dot_general(a, b, (((0,), (0,)), ((), ())), preferred_element_type=F32)


def _sigmoid(v):
    return 1.0 / (1.0 + jnp.exp(-v))


def _rms_stat(x):
    return lax.rsqrt(jnp.mean(x * x, axis=-1, keepdims=True) + RMS_EPS)


def _rms_bwd(dn, x, r, gamma):
    dng = dn * gamma
    dx = r * dng - x * (r * r * r) * jnp.mean(dng * x, axis=-1, keepdims=True)
    return dx, jnp.sum(dn * x * r, axis=0, keepdims=True)


def _weight_loads(wg_ref, plan, sems):
    groups = []
    for j, (dst, key) in enumerate(plan):
        off, n = PK_OFF[key], N_ROWS[key]
        groups.append([pltpu.make_async_copy(
            wg_ref.at[d, pl.ds(off, n), :], dst.at[pl.ds(d * n, n), :], sems.at[j * NDEV + d])
            for d in range(NDEV)])

    def start():
        for group in groups:
            for cp in group:
                cp.start()

    def waiter(group):
        def wait():
            for cp in group:
                cp.wait()
        return wait

    return start, [waiter(group) for group in groups]


_ANY = pl.BlockSpec(memory_space=pl.ANY)


def _row_spec(tm, width, rev_nt=None):
    if rev_nt is None:
        return pl.BlockSpec((tm, width), lambda i: (i, 0))
    return pl.BlockSpec((tm, width), lambda i: (rev_nt - 1 - i, 0))


def _full_spec(shape):
    return pl.BlockSpec(shape, lambda *_: (0,) * len(shape))


def _sds(shape, dtype):
    return jax.ShapeDtypeStruct(shape, dtype)


def _mesh_pos():
    return lax.axis_index("x"), lax.axis_index("y"), lax.axis_index("c")


def _lin(p):
    return 4 * p[0] + 2 * p[1] + p[2]


def _ag_exchange(src, slot, send_sems, recv_sems, local_sem):
    x, y, c = _mesh_pos()
    me, sibling = (x, y, c), (x, y, 1 - c)
    chips = [(1 - x, y), (x, 1 - y), (1 - x, 1 - y)]

    def copy(k, block, to, own=False):
        return pltpu.make_async_remote_copy(
            src_ref=src if own else slot(block), dst_ref=slot(block),
            send_sem=send_sems.at[k], recv_sem=recv_sems.at[k], device_id=to, device_id_type=MESH)

    mine = pltpu.make_async_copy(src, slot(me), local_sem)
    first = [copy(0, me, sibling, own=True)]
    first += [copy(1 + j, me, (*chip, c), own=True) for j, chip in enumerate(chips)]
    passed = [copy(4 + j, (*chip, c), sibling) for j, chip in enumerate(chips)]

    def start():
        mine.start()
        for cp in first:
            cp.start()

    def forward():
        for j, chip in enumerate(chips):
            copy(1 + j, (*chip, c), me).wait_recv()
            passed[j].start()

    def finish():
        copy(0, sibling, me).wait_recv()
        for j, chip in enumerate(chips):
            copy(4 + j, (*chip, 1 - c), me).wait_recv()
        for cp in first + passed:
            cp.wait_send()
        mine.wait()

    return start, forward, finish


def _chip_exchange(p_ref, land_ref, send_sems, recv_sems, keys, part):
    x, y, c = _mesh_pos()
    l0, rows = _span(LD_OFF, keys)
    rows = rows // part[1]
    p0 = part[0] * rows
    dst = land_ref.at[2 * x + y, pl.ds(l0 + p0, rows), :]
    peers = [((x + (j >> 1)) % 2, (y + (j & 1)) % 2) for j in range(NCHIP)]

    def copy(j):
        tx, ty = peers[j]
        src = p_ref.at[2 * tx + ty, pl.ds(p0, rows), :]
        if j == 0:
            return pltpu.make_async_copy(src, dst, send_sems.at[0])
        return pltpu.make_async_remote_copy(
            src_ref=src, dst_ref=dst, send_sem=send_sems.at[j], recv_sem=recv_sems.at[j],
            device_id=(tx, ty, c), device_id_type=MESH)

    def start():
        for j in range(NCHIP):
            copy(j).start()

    def finish():
        for j in range(NCHIP):
            copy(j).wait()

    return start, finish


def _all_to_all_f32(src_for, dst_ref, send_sems, recv_sems):
    x, y, c = _mesh_pos()
    dst = dst_ref.at[_lin((x, y, c))]
    peers = [((x + ((j >> 2) & 1)) % 2, (y + ((j >> 1) & 1)) % 2, (c + (j & 1)) % 2) for j in range(NDEV)]

    def copy(j):
        if j == 0:
            return pltpu.make_async_copy(src_for(_lin(peers[0])), dst, send_sems.at[0])
        return pltpu.make_async_remote_copy(
            src_ref=src_for(_lin(peers[j])), dst_ref=dst, send_sem=send_sems.at[j], recv_sem=recv_sems.at[j],
            device_id=peers[j], device_id_type=MESH)

    def start():
        for j in range(NDEV):
            copy(j).start()

    def finish():
        for j in range(NDEV):
            copy(j).wait()

    return start, finish


class _Comm:
    def __init__(self, ins, alias_in, out_shape, scratch, make, gives_wg, middle_step=0):
        self.ins, self.alias_in, self.out_shape = ins, alias_in, out_shape
        self.scratch, self.make, self.gives_wg, self.middle_step = scratch, make, gives_wg, middle_step


def _ag_comm(shard, wg, keys, forward_step=0):
    def make(c_ins, c_out, sc):
        return _weights_exchange(c_ins[0], c_out, sc[0], sc[1], sc[2], keys)

    return _Comm([shard, wg], 1, _sds(wg.shape, BF16),
                 [pltpu.SemaphoreType.DMA((7,)), pltpu.SemaphoreType.DMA((7,)), pltpu.SemaphoreType.DMA],
                 make, True, forward_step)


def _weights_exchange(shard_ref, wg_ref, send_sems, recv_sems, local_sem, keys):
    r0, nr = _span(PK_OFF, keys)
    return _ag_exchange(shard_ref.at[pl.ds(r0, nr), :], lambda p: wg_ref.at[_lin(p), pl.ds(r0, nr), :],
                        send_sems, recv_sems, local_sem)


def _rs_comm(psum, keys, land, part=(0, 1), body_reads_land=False):
    def make(c_ins, c_out, sc):
        start, finish = _chip_exchange(c_ins[0], c_out, sc[0], sc[1], keys, part)
        return start, None, finish

    ins = [psum] + ([] if land is None else [land])
    return _Comm(ins, None if land is None else 1, _sds((NCHIP, N_WROWS, D), BF16),
                 [pltpu.SemaphoreType.DMA((NCHIP,)), pltpu.SemaphoreType.DMA((NCHIP,))], make, body_reads_land)


def _hosted_call(body, name, nt, in_specs, out_specs, out_shape, scratch, args, comm):
    if comm is None:
        return _pcall(body, name=name, grid=(nt,), in_specs=in_specs, out_specs=tuple(out_specs),
                      out_shape=tuple(out_shape), scratch_shapes=scratch,
                      compiler_params=_cparams(("arbitrary",)))(*args)
    n_in, n_out, n_sc, n_cin = len(in_specs), len(out_specs), len(scratch), len(comm.ins)

    def wrapped(*refs):
        ins = refs[:n_in]
        c_ins = refs[n_in:n_in + n_cin]
        outs = refs[n_in + n_cin:n_in + n_cin + n_out]
        c_out = refs[n_in + n_cin + n_out]
        sc = refs[n_in + n_cin + n_out + 1:n_in + n_cin + n_out + 1 + n_sc]
        c_sc = refs[n_in + n_cin + n_out + 1 + n_sc:]
        start, middle, finish = comm.make(c_ins, c_out, c_sc)
        pl.when(pl.program_id(0) == 0)(start)
        if comm.gives_wg:
            body(*ins, c_out, *outs, *sc)
        else:
            body(*ins, *outs, *sc)
        if middle is not None:
            pl.when(pl.program_id(0) == max(nt - 1 - comm.middle_step, 0))(middle)
        pl.when(pl.program_id(0) == nt - 1)(finish)

    aliases = {} if comm.alias_in is None else {n_in + comm.alias_in: n_out}
    res = _pcall(wrapped, name=name, grid=(nt,),
                 in_specs=list(in_specs) + [_ANY] * n_cin, out_specs=tuple(out_specs) + (_ANY,),
                 out_shape=tuple(out_shape) + (comm.out_shape,),
                 scratch_shapes=list(scratch) + list(comm.scratch),
                 input_output_aliases=aliases,
                 compiler_params=_cparams(("arbitrary",)))(*args, *comm.ins)
    return res


def _pack_shard(weights, smalls):
    plan = (("a_in", 0, 0, True), ("a_out", 1, 0, False), ("b_pw1", 2, 0, True), ("b_pw2", 3, 0, False),
            ("g0", 4, 0, False), ("u0", 5, 0, False), ("d0", 6, 0, False),
            ("g1", 4, 1, False), ("u1", 5, 1, False), ("d1", 6, 1, False))
    n_pad = 384

    def body(*refs):
        out, sm, pad = refs[-3:]
        for key, idx, layer, transposed in plan:
            n = N_ROWS[key]
            val = refs[idx][layer]
            if transposed:
                if n % 128:
                    pad[:, pl.ds(0, n)] = val
                    pad[:, pl.ds(n, n_pad - n)] = jnp.zeros((D, n_pad - n), F32)
                    val = pad[...]
                val = val.T[:n]
            out[pl.ds(PK_OFF[key], n), :] = val.astype(BF16)
        b_norm, b_b_pw1, b_conv, b_b_conv, b_ln_g, b_ln_b, b_b_pw2, a_conv = refs[len(weights):-3]
        sm[...] = jnp.zeros((SM_F32_ROWS, 128), F32)
        sm[pl.ds(SM_B_PW1, 1), :] = b_b_pw1[:, pl.ds(0, 128)]
        sm[pl.ds(SM_B_PW1 + 1, 1), :] = b_b_pw1[:, pl.ds(128, 128)]
        sm[pl.ds(SM_B_CONV, KB), :] = b_conv[0]
        sm[pl.ds(SM_A_CONV, KA), :] = a_conv[0]
        for row, ref in ((SM_B_NORM, b_norm), (SM_B_BCONV, b_b_conv), (SM_LN_G, b_ln_g), (SM_LN_B, b_ln_b),
                         (SM_B_PW2, b_b_pw2)):
            sm[pl.ds(row, 1), :] = ref[...]

    vm = pl.BlockSpec(memory_space=pltpu.VMEM)
    return _pcall(
        body, name="pack_shard",
        out_shape=(_sds((PK_ROWS, D), BF16), _sds((SM_F32_ROWS, 128), F32)),
        in_specs=[vm] * (len(weights) + len(smalls)), out_specs=(vm, vm),
        scratch_shapes=[pltpu.VMEM((D, n_pad), F32)],
        compiler_params=_cparams(),
    )(*weights, *smalls)


def _all_gather_first(shard, small):
    def body(x_ref, s_ref, wg_ref, sg_ref, send_w, recv_w, local_w, send_s, recv_s, local_s):
        start_w, forward_w, finish_w = _weights_exchange(x_ref, wg_ref, send_w, recv_w, local_w, ("a_in", "a_out"))
        start_s, forward_s, finish_s = _ag_exchange(s_ref, lambda p: sg_ref.at[_lin(p)], send_s, recv_s, local_s)
        start_s()
        start_w()
        forward_s()
        forward_w()
        finish_s()
        finish_w()

    sems = [pltpu.SemaphoreType.DMA((7,)), pltpu.SemaphoreType.DMA((7,)), pltpu.SemaphoreType.DMA]
    return _pcall(
        body, name="all_gather_first",
        out_shape=(_sds((NDEV, PK_ROWS, D), BF16), _sds((NDEV, SM_F32_ROWS, 128), F32)),
        in_specs=[_ANY, _ANY], out_specs=(_ANY, _ANY), scratch_shapes=sems + sems,
    )(shard, small)


def _pair_reduce(grads, keys, side=None):
    l0, rows = _span(LD_OFF, keys)
    nk = len(keys)
    n_side = 0 if side is None else 2

    def body(c_ref, *refs):
        mine = refs[:nk]
        whole = refs[nk:2 * nk]
        o_ref = refs[2 * nk + n_side]
        got, send_sem, recv_sems = refs[2 * nk + 2 * n_side + 1:2 * nk + 2 * n_side + 4]
        t = pl.program_id(0)
        x, y, c = _mesh_pos()
        sibling = (x, y, 1 - c)
        if side is not None:
            sm_ref, rp_ref = refs[2 * nk:2 * nk + 2]
            sm_land, rp_land = refs[2 * nk + 3:2 * nk + 5]
            s2, r2, s3, r3 = refs[-4:]
            start_s, finish_s = _all_to_all_f32(lambda d: sm_ref.at[d], sm_land, s2, r2)
            start_r, finish_r = _all_to_all_f32(lambda d: rp_ref.at[0], rp_land, s3, r3)

        @pl.when(t == 0)
        def _():
            if side is not None:
                start_s()
                start_r()
            for ref, key in zip(whole, keys):
                for d in range(NCHIP):
                    pltpu.make_async_remote_copy(
                        src_ref=ref.at[2 * d + (1 - c)], dst_ref=got.at[d, pl.ds(LD_OFF[key] - l0, N_ROWS[key]), :],
                        send_sem=send_sem, recv_sem=recv_sems.at[d], device_id=sibling, device_id_type=MESH).start()

        pltpu.make_async_remote_copy(src_ref=got.at[t], dst_ref=got.at[t], send_sem=send_sem,
                                     recv_sem=recv_sems.at[t], device_id=sibling, device_id_type=MESH).wait_recv()
        for ref, key in zip(mine, keys):
            sl = pl.ds(LD_OFF[key] - l0, N_ROWS[key])
            o_ref[sl, :] = (ref[...].astype(F32) + got[t, sl, :].astype(F32)).astype(BF16)

        @pl.when(t == NCHIP - 1)
        def _():
            pltpu.make_async_remote_copy(src_ref=got, dst_ref=got, send_sem=send_sem, recv_sem=recv_sems.at[0],
                                         device_id=sibling, device_id_type=MESH).wait_send()
            if side is not None:
                finish_s()
                finish_r()

    out_specs = [pl.BlockSpec((None, rows, D), lambda t, c: (t, 0, 0))]
    out_shape = [_sds((NCHIP, rows, D), BF16)]
    scratch = [pltpu.VMEM((NCHIP, rows, D), BF16), pltpu.SemaphoreType.DMA, pltpu.SemaphoreType.DMA((NCHIP,))]
    if side is not None:
        out_specs += [_ANY, _ANY]
        out_shape += [_sds((NDEV, SM_F32_ROWS, 128), F32), _sds((NDEV, 8, D), F32)]
        scratch += [pltpu.SemaphoreType.DMA((NDEV,))] * 4
    grid_spec = pltpu.PrefetchScalarGridSpec(
        num_scalar_prefetch=1, grid=(NCHIP,),
        in_specs=[pl.BlockSpec((None, N_ROWS[k], D), lambda t, c: (2 * t + c[0], 0, 0)) for k in keys]
        + [_ANY] * (nk + n_side),
        out_specs=tuple(out_specs), scratch_shapes=scratch)
    core = lax.axis_index("c").astype(jnp.int32).reshape(1)
    res = _pcall(
        body, name="pair_reduce_" + keys[0], grid_spec=grid_spec, out_shape=tuple(out_shape),
        compiler_params=_cparams(("arbitrary",)),
    )(core, *grads, *grads, *(side or ()))
    return res[0] if side is None else res


def _mixa_fwd(x, gam, cw, wg, tm, comm):
    T = x.shape[0]

    def body(x_ref, gam_ref, cw_ref, wg_ref, h_ref, n_ref, b_ref, c_ref, v_ref, cc_ref, y_ref,
             win, wout, buf, sems):
        first = pl.program_id(0) == 0
        start_w, wait_w = _weight_loads(wg_ref, [(win, "a_in"), (wout, "a_out")], sems)

        @pl.when(first)
        def _():
            start_w()
            buf[pl.ds(0, HALO_A), :] = jnp.zeros((HALO_A, D), F32)

        xv = x_ref[...]
        nb = (xv * _rms_stat(xv) * gam_ref[...]).astype(BF16)
        n_ref[...] = nb
        pl.when(first)(wait_w[0])
        bv = _dot_nt(nb, win[pl.ds(0, D), :])
        cval = _dot_nt(nb, win[pl.ds(D, D), :])
        vval = _dot_nt(nb, win[pl.ds(2 * D, D), :])
        cv = cval * vval
        buf[pl.ds(HALO_A, tm), :] = cv
        cc = cw_ref[pl.ds(KA - 1, 1), :] * cv
        for k in range(KA - 1):
            cc = cc + cw_ref[pl.ds(k, 1), :] * buf[pl.ds(HALO_A - (KA - 1) + k, tm), :]
        buf[pl.ds(0, HALO_A), :] = buf[pl.ds(tm, HALO_A), :]
        yb = (bv * cc).astype(BF16)
        b_ref[...] = bv.astype(BF16)
        c_ref[...] = cval.astype(BF16)
        v_ref[...] = vval.astype(BF16)
        cc_ref[...] = cc.astype(BF16)
        y_ref[...] = yb
        pl.when(first)(wait_w[1])
        h_ref[...] = xv + _dot(yb, wout[...])

    act = _sds((T, D), BF16)
    return _hosted_call(
        body, "mixa_fwd", T // tm,
        in_specs=[_row_spec(tm, D), _full_spec((1, D)), _full_spec((KA, D))],
        out_specs=[_row_spec(tm, D) for _ in range(7)],
        out_shape=(_sds((T, D), F32),) + (act,) * 6,
        scratch=[pltpu.VMEM((3 * D, D), BF16), pltpu.VMEM((D, D), BF16),
                 pltpu.VMEM((HALO_A + tm, D), F32), pltpu.SemaphoreType.DMA((2 * NDEV,))],
        args=(x, gam, cw), comm=comm)


def _ffn_fwd(h, gam, wg, layer, tm, comm, head=None):
    T = h.shape[0]
    kg, ku, kd = "g%d" % layer, "u%d" % layer, "d%d" % layer
    n_head = 0 if head is None else 2

    def body(*refs):
        h_ref, gam_ref = refs[:2]
        wg_ref = refs[2 + n_head]
        o_ref, n_ref, g_ref, u_ref, gu_ref = refs[3 + n_head:8 + n_head]
        wgt, wut, wd, sems = refs[-4:]

        first = pl.program_id(0) == 0
        start_w, wait_w = _weight_loads(wg_ref, [(wgt, kg), (wut, ku), (wd, kd)], sems)
        pl.when(first)(start_w)

        hv = h_ref[...]
        nb = (hv * _rms_stat(hv) * gam_ref[...]).astype(BF16)
        n_ref[...] = nb
        out = hv
        for f in range(2):
            cols = pl.ds(f * FH, FH)
            if f == 0:
                pl.when(first)(wait_w[0])
            g = _dot_nt(nb, wgt[cols, :])
            if f == 0:
                pl.when(first)(wait_w[1])
            u = _dot_nt(nb, wut[cols, :])
            gu = (g * _sigmoid(g) * u).astype(BF16)
            g_ref[:, cols] = g.astype(BF16)
            u_ref[:, cols] = u.astype(BF16)
            gu_ref[:, cols] = gu
            if f == 0:
                pl.when(first)(wait_w[2])
            out = out + _dot(gu, wd[cols, :])
        if head is None:
            o_ref[...] = out
        else:
            t_ref, fin_ref, st_ref = refs[2], refs[3], refs[8 + n_head]

            @pl.when(pl.program_id(0) == 0)
            def _():
                st_ref[...] = jnp.zeros((8, D), F32)

            gamma = fin_ref[...]
            r = _rms_stat(out)
            err = out * r * gamma - t_ref[...]
            dx, dgam = _rms_bwd(err * (1.0 / D), out, r, gamma)
            o_ref[...] = dx
            st_ref[pl.ds(0, 1), :] += dgam
            st_ref[pl.ds(1, 1), :] += (0.5 / D) * jnp.sum(err * err, axis=0, keepdims=True)

    pre = _sds((T, FF), BF16)
    in_specs = [_row_spec(tm, D), _full_spec((1, D))]
    args = (h, gam)
    out_specs = [_row_spec(tm, D), _row_spec(tm, D), _row_spec(tm, FF), _row_spec(tm, FF), _row_spec(tm, FF)]
    out_shape = (_sds((T, D), F32), _sds((T, D), BF16), pre, pre, pre)
    if head is not None:
        in_specs, args = in_specs + [_row_spec(tm, D), _full_spec((1, D))], args + tuple(head)
        out_specs, out_shape = out_specs + [_full_spec((8, D))], out_shape + (_sds((8, D), F32),)
    if comm is None:
        in_specs, args = in_specs + [_ANY], args + (wg,)
    return _hosted_call(
        body, "ffn%d_fwd" % layer, T // tm,
        in_specs=in_specs, out_specs=out_specs, out_shape=out_shape,
        scratch=[pltpu.VMEM((FF, D), BF16)] * 3 + [pltpu.SemaphoreType.DMA((3 * NDEV,))],
        args=args, comm=comm)


def _shifted_copies(buf, shf, tm):
    for r in range(1, 8):
        shf[r - 1] = buf[pl.ds(r, tm + HALO_B - 8), :]


def _broadcast_taps(vec_ref, wb):
    for k in range(KB):
        wb[k] = jnp.broadcast_to(vec_ref[pl.ds(SM_B_CONV + k, 1), :], (8, D))
    wb[KB] = jnp.broadcast_to(vec_ref[pl.ds(SM_B_BCONV, 1), :], (8, D))


def _taps_by_shift_residue(taps):
    groups = {}
    for k, shift in taps:
        q, r = divmod(shift, 8)
        groups.setdefault(r, []).append((k, q))
    return sorted(groups.items())


def _window(buf, shf, base, r, q0, n_groups, lanes):
    rows = pl.ds(base + 8 * q0, 8 * n_groups)
    v = buf[rows, lanes] if r == 0 else shf[r - 1, rows, lanes]
    return [v[8 * i:8 * i + 8] for i in range(n_groups)]


def _long_conv(buf, shf, wb, out_ref, tm, taps, bias_row):
    n_acc = CONV_ROWS // 8
    groups = _taps_by_shift_residue(taps)
    for col in range(D // 128):
        lanes = pl.ds(128 * col, 128)

        def rows(i, carry, lanes=lanes):
            base = i * CONV_ROWS
            init = jnp.zeros((8, 128), F32) if bias_row is None else wb[bias_row, :, lanes]
            accs = [init] * n_acc
            for r, lst in groups:
                q0, q1 = min(q for _, q in lst), max(q for _, q in lst)
                win = _window(buf, shf, base, r, q0, n_acc + q1 - q0, lanes)
                for k, q in lst:
                    wk = wb[k, :, lanes]
                    accs = [acc + wk * win[h + q - q0] for h, acc in enumerate(accs)]
            for h, acc in enumerate(accs):
                out_ref[pl.ds(base + 8 * h, 8), lanes] = acc
            return carry

        for i in range(tm // CONV_ROWS):
            rows(i, 0)


def _long_conv_grad_taps(buf, shf, x_ref, wacc, tm):
    n_acc = CONV_ROWS // 8
    groups = _taps_by_shift_residue([(KB - 1 - j, j) for j in range(KB)])
    for col in range(D // 128):
        lanes = pl.ds(128 * col, 128)

        def rows(i, carry, lanes=lanes):
            base = i * CONV_ROWS
            xv = x_ref[pl.ds(base, CONV_ROWS), lanes]
            xs = [xv[8 * h:8 * h + 8] for h in range(n_acc)]
            for r, lst in groups:
                q0, q1 = min(q for _, q in lst), max(q for _, q in lst)
                win = _window(buf, shf, base, r, q0, n_acc + q1 - q0, lanes)
                for k, q in lst:
                    prod = [x * win[h + q - q0] for h, x in enumerate(xs)]
                    wacc[k, :, lanes] += functools.reduce(lambda a, b: a + b, prod)
            return carry

        for i in range(tm // CONV_ROWS):
            rows(i, 0)


def _ln_stats(dc):
    mu = jnp.mean(dc, axis=-1, keepdims=True)
    xc = dc - mu
    rstd = lax.rsqrt(jnp.mean(xc * xc, axis=-1, keepdims=True) + LN_EPS)
    return xc * rstd, rstd


def _mixb_fwd(h, vecs, bias1, wg, tm, comm):
    T = h.shape[0]

    def body(h_ref, vec_ref, b1_ref, wg_ref, o_ref, n_ref, a_ref, g_ref, dc_ref, s_ref, w1, w2, buf, shf, wb, sems):
        first = pl.program_id(0) == 0
        start_w, wait_w = _weight_loads(wg_ref, [(w1, "b_pw1"), (w2, "b_pw2")], sems)

        @pl.when(first)
        def _():
            start_w()
            buf[pl.ds(0, HALO_B), :] = jnp.zeros((HALO_B, D), F32)
            _broadcast_taps(vec_ref, wb)

        hv = h_ref[...]
        nb = (hv * _rms_stat(hv) * vec_ref[pl.ds(SM_B_NORM, 1), :]).astype(BF16)
        n_ref[...] = nb
        pl.when(first)(wait_w[0])
        a = _dot_nt(nb, w1[pl.ds(0, D), :]) + b1_ref[:, pl.ds(0, D)]
        g = _dot_nt(nb, w1[pl.ds(D, D), :]) + b1_ref[:, pl.ds(D, D)]
        a_ref[...] = a.astype(BF16)
        g_ref[...] = g.astype(BF16)
        buf[pl.ds(HALO_B, tm), :] = a * _sigmoid(g)
        _shifted_copies(buf, shf, tm)

        _long_conv(buf, shf, wb, dc_ref, tm, [(k, HALO_B - (KB - 1) + k) for k in range(KB)], KB)
        buf[pl.ds(0, HALO_B), :] = buf[pl.ds(tm, HALO_B), :]
        xhat, _ = _ln_stats(dc_ref[...])
        ln = xhat * vec_ref[pl.ds(SM_LN_G, 1), :] + vec_ref[pl.ds(SM_LN_B, 1), :]
        s = (ln * _sigmoid(ln)).astype(BF16)
        s_ref[...] = s
        pl.when(first)(wait_w[1])
        o_ref[...] = hv + _dot(s, w2[...]) + vec_ref[pl.ds(SM_B_PW2, 1), :]

    act = _sds((T, D), BF16)
    return _hosted_call(
        body, "mixb_fwd", T // tm,
        in_specs=[_row_spec(tm, D), _full_spec((SM_F32_ROWS, D)), _full_spec((1, 2 * D))],
        out_specs=[_row_spec(tm, D) for _ in range(6)],
        out_shape=(_sds((T, D), F32), act, act, act, _sds((T, D), F32), act),
        scratch=[pltpu.VMEM((2 * D, D), BF16), pltpu.VMEM((D, D), BF16),
                 pltpu.VMEM((HALO_B + tm, D), F32), pltpu.VMEM((7, HALO_B + tm - 8, D), F32),
                 pltpu.VMEM((KB + 1, 8, D), F32), pltpu.SemaphoreType.DMA((2 * NDEV,))],
        args=(h, vecs, bias1), comm=comm)


def _ffn_bwd_dx(dh, h, g, u, gam, wg, layer, tm, comm):
    T = h.shape[0]
    kg, ku, kd = "g%d" % layer, "u%d" % layer, "d%d" % layer

    def body(dh_ref, h_ref, g_ref, u_ref, gam_ref, wg_ref, o_ref, dg_ref, du_ref, st_ref, wgt, wut, wd, sems):
        first = pl.program_id(0) == 0
        start_w, wait_w = _weight_loads(wg_ref, [(wd, kd), (wgt, kg), (wut, ku)], sems)

        @pl.when(first)
        def _():
            start_w()
            st_ref[...] = jnp.zeros((8, D), F32)

        dhv = dh_ref[...]
        dhb = dhv.astype(BF16)
        dn = jnp.zeros_like(dhv)
        for f in range(2):
            cols = pl.ds(f * FH, FH)
            if f == 0:
                pl.when(first)(wait_w[0])
            dgu = _dot_nt(dhb, wd[cols, :])
            gv = g_ref[:, cols].astype(F32)
            sg = _sigmoid(gv)
            du = (dgu * gv * sg).astype(BF16)
            dg = (dgu * u_ref[:, cols].astype(F32) * (sg * (1.0 + gv * (1.0 - sg)))).astype(BF16)
            dg_ref[:, cols] = dg
            du_ref[:, cols] = du
            if f == 0:
                pl.when(first)(wait_w[1])
                pl.when(first)(wait_w[2])
            dn = dn + _dot(dg, wgt[cols, :]) + _dot(du, wut[cols, :])
        hv = h_ref[...]
        dx, dgam = _rms_bwd(dn, hv, _rms_stat(hv), gam_ref[...])
        o_ref[...] = dhv + dx
        st_ref[pl.ds(0, 1), :] += dgam

    pre = _sds((T, FF), BF16)
    return _hosted_call(
        body, "ffn%d_bwd_dx" % layer, T // tm,
        in_specs=[_row_spec(tm, D), _row_spec(tm, D), _row_spec(tm, FF), _row_spec(tm, FF),
                  _full_spec((1, D)), _ANY],
        out_specs=[_row_spec(tm, D), _row_spec(tm, FF), _row_spec(tm, FF), _full_spec((8, D))],
        out_shape=(_sds((T, D), F32), pre, pre, _sds((8, D), F32)),
        scratch=[pltpu.VMEM((FF, D), BF16)] * 3 + [pltpu.SemaphoreType.DMA((3 * NDEV,))],
        args=(dh, h, g, u, gam, wg), comm=comm)


def _grad_w(lhs, rhs, mc, name, tm):
    T, M = lhs.shape
    nt = T // tm

    def body(l_ref, r_ref, o_ref, acc):
        i = pl.program_id(1)

        @pl.when(i == 0)
        def _():
            acc[...] = jnp.zeros((mc, D), F32)

        acc[...] += _dot_tn(l_ref[...], r_ref[...].astype(BF16))

        @pl.when(i == nt - 1)
        def _():
            o_ref[...] = acc[...].astype(BF16)

    return _pcall(
        body, name=name, grid=(M // mc, nt),
        out_shape=_sds((M, D), BF16),
        in_specs=[pl.BlockSpec((tm, mc), lambda j, i: (i, j)), pl.BlockSpec((tm, D), lambda j, i: (i, 0))],
        out_specs=pl.BlockSpec((mc, D), lambda j, i: (j, 0)),
        scratch_shapes=[pltpu.VMEM((mc, D), F32)],
        compiler_params=_cparams(("arbitrary", "arbitrary")),
    )(lhs, rhs)


def _mixb_bwd(dh, h, a, g, dc, vecs, wg, tm, comm):
    T = h.shape[0]
    nt = T // tm

    def body(dh_ref, h_ref, a_ref, g_ref, dc_ref, vec_ref, wg_ref, o_ref, du_ref, st_ref, sb_ref,
             w1, w2, buf, shf, glu_s, dglu_s, wacc, wb, sems):
        first = pl.program_id(0) == 0
        start_w, wait_w = _weight_loads(wg_ref, [(w2, "b_pw2"), (w1, "b_pw1")], sems)

        @pl.when(first)
        def _():
            start_w()
            buf[pl.ds(tm, HALO_B), :] = jnp.zeros((HALO_B, D), F32)
            wacc[...] = jnp.zeros((KB, 8, D), F32)
            _broadcast_taps(vec_ref, wb)
            st_ref[...] = jnp.zeros((SM_F32_ROWS, D), F32)
            sb_ref[...] = jnp.zeros((8, 2 * D), F32)

        def acc(row, val):
            st_ref[pl.ds(row, 1), :] += jnp.sum(val, axis=0, keepdims=True)

        dhv = dh_ref[...]
        acc(SM_B_PW2, dhv)
        pl.when(first)(wait_w[0])
        ds = _dot_nt(dhv.astype(BF16), w2[...])
        xhat, rstd = _ln_stats(dc_ref[...])
        ln_g = vec_ref[pl.ds(SM_LN_G, 1), :]
        ln = xhat * ln_g + vec_ref[pl.ds(SM_LN_B, 1), :]
        sl = _sigmoid(ln)
        dln = ds * (sl * (1.0 + ln * (1.0 - sl)))
        acc(SM_LN_G, dln * xhat)
        acc(SM_LN_B, dln)
        dxh = dln * ln_g
        ddc = rstd * (dxh - jnp.mean(dxh, axis=-1, keepdims=True)
                      - xhat * jnp.mean(dxh * xhat, axis=-1, keepdims=True))
        acc(SM_B_BCONV, ddc)
        buf[pl.ds(0, tm), :] = ddc
        _shifted_copies(buf, shf, tm)
        av = a_ref[...].astype(F32)
        sg = _sigmoid(g_ref[...].astype(F32))
        glu_s[...] = av * sg

        _long_conv(buf, shf, wb, dglu_s, tm, [(KB - 1 - j, j) for j in range(KB)], None)
        _long_conv_grad_taps(buf, shf, glu_s, wacc, tm)
        buf[pl.ds(tm, HALO_B), :] = buf[pl.ds(0, HALO_B), :]
        dglu = dglu_s[...]
        da = dglu * sg
        dg = dglu * av * sg * (1.0 - sg)
        sb_ref[pl.ds(0, 1), pl.ds(0, D)] += jnp.sum(da, axis=0, keepdims=True)
        sb_ref[pl.ds(0, 1), pl.ds(D, D)] += jnp.sum(dg, axis=0, keepdims=True)
        dab, dgb = da.astype(BF16), dg.astype(BF16)
        du_ref[:, pl.ds(0, D)] = dab
        du_ref[:, pl.ds(D, D)] = dgb
        pl.when(first)(wait_w[1])
        dn = _dot(dab, w1[pl.ds(0, D), :]) + _dot(dgb, w1[pl.ds(D, D), :])
        hv = h_ref[...]
        dx, dgam = _rms_bwd(dn, hv, _rms_stat(hv), vec_ref[pl.ds(SM_B_NORM, 1), :])
        o_ref[...] = dhv + dx
        st_ref[pl.ds(SM_B_NORM, 1), :] += dgam

        @pl.when(pl.program_id(0) == nt - 1)
        def _():
            st_ref[pl.ds(SM_B_CONV, KB), :] = jnp.sum(wacc[...], axis=1)

    rs = functools.partial(_row_spec, rev_nt=nt)
    return _hosted_call(
        body, "mixb_bwd", nt,
        in_specs=[rs(tm, D), rs(tm, D), rs(tm, D), rs(tm, D), rs(tm, D), _full_spec((SM_F32_ROWS, D)), _ANY],
        out_specs=[rs(tm, D), rs(tm, 2 * D), _full_spec((SM_F32_ROWS, D)), _full_spec((8, 2 * D))],
        out_shape=(_sds((T, D), F32), _sds((T, 2 * D), BF16), _sds((SM_F32_ROWS, D), F32), _sds((8, 2 * D), F32)),
        scratch=[pltpu.VMEM((2 * D, D), BF16), pltpu.VMEM((D, D), BF16),
                 pltpu.VMEM((tm + HALO_B, D), F32), pltpu.VMEM((7, tm + HALO_B - 8, D), F32),
                 pltpu.VMEM((tm, D), F32), pltpu.VMEM((tm, D), F32), pltpu.VMEM((KB, 8, D), F32),
                 pltpu.VMEM((KB + 1, 8, D), F32), pltpu.SemaphoreType.DMA((2 * NDEV,))],
        args=(dh, h, a, g, dc, vecs, wg), comm=comm)


def _mixa_bwd(dh, x, b, c, v, cc, gam, cw, wg, tm, comm):
    T = x.shape[0]
    nt = T // tm

    def body(dh_ref, x_ref, b_ref, c_ref, v_ref, cc_ref, gam_ref, cw_ref, wg_ref,
             o_ref, db_ref, st_ref, win, wout, buf, sems):
        first = pl.program_id(0) == 0
        start_w, wait_w = _weight_loads(wg_ref, [(wout, "a_out"), (win, "a_in")], sems)

        @pl.when(first)
        def _():
            start_w()
            buf[pl.ds(tm, HALO_A), :] = jnp.zeros((HALO_A, D), F32)
            st_ref[...] = jnp.zeros((8, D), F32)

        dhv = dh_ref[...]
        pl.when(first)(wait_w[0])
        dy = _dot_nt(dhv.astype(BF16), wout[...])
        cval = c_ref[...].astype(F32)
        vval = v_ref[...].astype(F32)
        d_b = (dy * cc_ref[...].astype(F32)).astype(BF16)
        buf[pl.ds(0, tm), :] = dy * b_ref[...].astype(F32)
        cv = cval * vval
        dcv = jnp.zeros((tm, D), F32)
        for j in range(KA):
            sh = buf[pl.ds(j, tm), :]
            k = KA - 1 - j
            dcv = dcv + cw_ref[pl.ds(k, 1), :] * sh
            st_ref[pl.ds(1 + k, 1), :] += jnp.sum(cv * sh, axis=0, keepdims=True)
        buf[pl.ds(tm, HALO_A), :] = buf[pl.ds(0, HALO_A), :]
        d_c = (dcv * vval).astype(BF16)
        d_v = (dcv * cval).astype(BF16)
        db_ref[:, pl.ds(0, D)] = d_b
        db_ref[:, pl.ds(D, D)] = d_c
        db_ref[:, pl.ds(2 * D, D)] = d_v
        pl.when(first)(wait_w[1])
        dn = _dot(d_b, win[pl.ds(0, D), :]) + _dot(d_c, win[pl.ds(D, D), :]) + _dot(d_v, win[pl.ds(2 * D, D), :])
        xv = x_ref[...]
        dx, dgam = _rms_bwd(dn, xv, _rms_stat(xv), gam_ref[...])
        o_ref[...] = dhv + dx
        st_ref[pl.ds(0, 1), :] += dgam

    rs = functools.partial(_row_spec, rev_nt=nt)
    return _hosted_call(
        body, "mixa_bwd", nt,
        in_specs=[rs(tm, D) for _ in range(6)] + [_full_spec((1, D)), _full_spec((KA, D)), _ANY],
        out_specs=[rs(tm, D), rs(tm, 3 * D), _full_spec((8, D))],
        out_shape=(_sds((T, D), F32), _sds((T, 3 * D), BF16), _sds((8, D), F32)),
        scratch=[pltpu.VMEM((3 * D, D), BF16), pltpu.VMEM((D, D), BF16),
                 pltpu.VMEM((tm + HALO_A, D), F32), pltpu.SemaphoreType.DMA((2 * NDEV,))],
        args=(dh, x, b, c, v, cc, gam, cw, wg), comm=comm)


def _sum_small(sm, rp):
    def body(sm_ref, rp_ref, osm, orp, oloss):
        a = sm_ref[0]
        b = rp_ref[0]
        for s in range(1, NDEV):
            a = a + sm_ref[s]
            b = b + rp_ref[s]
        osm[...] = a
        orp[...] = b
        oloss[...] = jnp.zeros((8, 128), F32) + jnp.sum(b[4:5, :], axis=-1, keepdims=True)

    return _pcall(
        body, name="sum_small_grads",
        out_shape=(_sds((SM_F32_ROWS, 128), F32), _sds((8, D), F32), _sds((8, 128), F32)),
        in_specs=[pl.BlockSpec(memory_space=pltpu.VMEM)] * 2,
        out_specs=tuple(pl.BlockSpec(memory_space=pltpu.VMEM) for _ in range(3)),
        compiler_params=_cparams(),
    )(sm, rp)


def _adam_math(w, g, m, v):
    mn = B1 * m + (1.0 - B1) * g
    vn = B2 * v + (1.0 - B2) * (g * g)
    m_hat = mn / (1.0 - B1 ** STEP)
    v_hat = vn / (1.0 - B2 ** STEP)
    return -LR * (m_hat / (jnp.sqrt(v_hat) + ADAM_EPS) + WD * w), mn, vn


def _finish_weight(land, keys, transposed, w, m, v, name, comm=None):
    layers, rows, cols = w.shape
    n = N_ROWS[keys[0]]
    n_pad = -(-n // 128) * 128 if transposed else n

    def body(w_ref, m_ref, v_ref, land_ref, og, od, om, ov, buf, sem):
        off = LD_OFF[keys[0]]
        if layers == 2:
            off = jnp.where(pl.program_id(0) == 0, off, LD_OFF[keys[1]])
        cp = pltpu.make_async_copy(land_ref.at[:, pl.ds(pl.multiple_of(off, 32), n), :], buf, sem)
        cp.start()
        cp.wait()
        g = buf[0].astype(F32)
        for s in range(1, NCHIP):
            g = g + buf[s].astype(F32)
        if transposed:
            if n_pad != n:
                g = jnp.concatenate([g, jnp.zeros((n_pad - n, D), F32)], axis=0)
            g = g.T[:, :n]
        d, mn, vn = _adam_math(w_ref[...], g, m_ref[...], v_ref[...])
        og[...] = g
        od[...] = d
        om[...] = mn
        ov[...] = vn

    spec = pl.BlockSpec((None, rows, cols), lambda l: (l, 0, 0))
    shp = _sds(w.shape, F32)
    in_specs, args = [spec, spec, spec], (w, m, v)
    if comm is None:
        in_specs, args = in_specs + [_ANY], args + (land,)
    return _hosted_call(
        body, "finish_" + name, layers,
        in_specs=in_specs, out_specs=[spec] * 4, out_shape=(shp,) * 4,
        scratch=[pltpu.VMEM((NCHIP, n, D), BF16), pltpu.SemaphoreType.DMA],
        args=args, comm=comm)


SMALL_PARAMS = (("sm", SM_B_NORM, 1), ("sm", SM_B_PW1, 2), ("sm", SM_B_CONV, KB), ("sm", SM_B_BCONV, 1),
                ("sm", SM_LN_G, 1), ("sm", SM_LN_B, 1), ("sm", SM_B_PW2, 1), ("sm", SM_A_CONV, KA),
                ("rp", 0, 1), ("rp", 1, 2), ("rp", 3, 1))


def _adamw_small(g_small, g_repl, triples):
    n = len(SMALL_PARAMS)

    def body(*refs):
        gs_ref, gr_ref = refs[0], refs[1]
        ins = refs[2:2 + 3 * n]
        outs = refs[2 + 3 * n:]
        for i, (src, r0, nr) in enumerate(SMALL_PARAMS):
            w_ref, m_ref, v_ref = ins[3 * i:3 * i + 3]
            g = (gs_ref if src == "sm" else gr_ref)[pl.ds(r0, nr), :]
            if i == 1:
                g = jnp.concatenate([g[0:1], g[1:2]], axis=1)
            lead = (0,) if len(w_ref.shape) == 3 else ()
            idx = lead + (slice(None), slice(None))
            vals = (g,) + _adam_math(w_ref[idx], g, m_ref[idx], v_ref[idx])
            for o_ref, val in zip(outs[4 * i:4 * i + 4], vals):
                o_ref[idx] = val

    flat = [a for t in triples for a in t]
    out_shape = tuple(_sds(t[0].shape, F32) for t in triples for _ in range(4))
    vm = pl.BlockSpec(memory_space=pltpu.VMEM)
    res = _pcall(
        body, name="adamw_small", out_shape=out_shape,
        in_specs=[vm] * (2 + len(flat)), out_specs=tuple(vm for _ in out_shape),
        compiler_params=_cparams(),
    )(g_small, g_repl, *flat)
    return [tuple(res[4 * i:4 * i + 4]) for i in range(n)]


def kernel(x, a_norm, a_w_in, a_conv, a_w_out, b_norm, b_w_pw1, b_b_pw1, b_conv, b_b_conv, b_ln_g, b_ln_b, b_w_pw2, b_b_pw2, ffn_norm, ffn_w_gate, ffn_w_up, ffn_w_down, final_norm, loss_target, m_a_norm, m_a_w_in, m_a_conv, m_a_w_out, m_b_norm, m_b_w_pw1, m_b_b_pw1, m_b_conv, m_b_b_conv, m_b_ln_g, m_b_ln_b, m_b_w_pw2, m_b_b_pw2, m_ffn_norm, m_ffn_w_gate, m_ffn_w_up, m_ffn_w_down, m_final_norm, v_a_norm, v_a_w_in, v_a_conv, v_a_w_out, v_b_norm, v_b_w_pw1, v_b_b_pw1, v_b_conv, v_b_b_conv, v_b_ln_g, v_b_ln_b, v_b_w_pw2, v_b_b_pw2, v_ffn_norm, v_ffn_w_gate, v_ffn_w_up, v_ffn_w_down, v_final_norm):
    T = x.shape[1]
    tm = min(TM, T)
    tma = min(TM_A, T)
    tw = min(TM_DW, T)
    xs = x.reshape(T, D)
    tgt = loss_target.reshape(T, D)

    def rows_first(a):
        return jnp.swapaxes(a, 1, 2)

    shard, small = _pack_shard(
        (a_w_in, a_w_out, b_w_pw1, b_w_pw2, rows_first(ffn_w_gate), rows_first(ffn_w_up), ffn_w_down),
        (b_norm, b_b_pw1, b_conv, b_b_conv, b_ln_g, b_ln_b, b_b_pw2, a_conv))
    wg, sm_all = _all_gather_first(shard, small)
    vecs = sm_all.transpose(1, 0, 2).reshape(SM_F32_ROWS, D)
    bias1 = sm_all[:, SM_B_PW1:SM_B_PW1 + 2, :].reshape(1, 2 * D)
    cw_a = vecs[SM_A_CONV:SM_A_CONV + KA]
    fn0, fn1 = ffn_norm[0:1], ffn_norm[1:2]
    fin = final_norm.reshape(1, D)

    h1, n0, bq, cq, vq, ccq, yq, wg = _mixa_fwd(xs, a_norm, cw_a, wg, tma, _ag_comm(shard, wg, ("g0", "u0", "d0")))
    h2, n1, g0, u0, gu0, wg = _ffn_fwd(h1, fn0, wg, 0, tma, _ag_comm(shard, wg, ("b_pw1", "b_pw2"), (T // tma) // 2))
    h3, n2, aq, gq, dcq, sq, wg = _mixb_fwd(h2, vecs, bias1, wg, tm, _ag_comm(shard, wg, ("g1", "u1", "d1"), 1))
    dh4, n3, g1, u1, gu1, st_fin = _ffn_fwd(h3, fn1, wg, 1, tma, None, head=(tgt, fin))

    def by_dest(gw, key):
        return gw.reshape(NDEV, N_ROWS[key], D)

    gw_d1 = by_dest(_grad_w(gu1, dh4, FH, "grad_down1", tw), "d1")
    dh3, dg1, du1, st_f1 = _ffn_bwd_dx(dh4, h3, g1, u1, fn1, wg, 1, tm, None)
    gw_g1 = by_dest(_grad_w(dg1, n3, FH, "grad_gate1", tw), "g1")
    gw_u1 = by_dest(_grad_w(du1, n3, FH, "grad_up1", tw), "u1")
    keys = ("g1", "u1", "d1")
    dh2, dub, st_b, st_b1, land = _mixb_bwd(
        dh3, h2, aq, gq, dcq, vecs, wg, tm, _rs_comm(_pair_reduce([gw_g1, gw_u1, gw_d1], keys), keys, None))
    gw_pw1 = by_dest(_grad_w(dub, n2, D, "grad_pw1", tw), "b_pw1")
    gw_pw2 = by_dest(_grad_w(sq, dh3, D, "grad_pw2", tw), "b_pw2")
    gw_d0 = by_dest(_grad_w(gu0, dh2, FH, "grad_down0", tw), "d0")
    keys = ("b_pw1", "b_pw2", "d0")
    dh1, dg0, du0, st_f0, land = _ffn_bwd_dx(
        dh2, h1, g0, u0, fn0, wg, 0, tm, _rs_comm(_pair_reduce([gw_pw1, gw_pw2, gw_d0], keys), keys, land))
    gw_g0 = by_dest(_grad_w(dg0, n1, FH, "grad_gate0", tw), "g0")
    gw_u0 = by_dest(_grad_w(du0, n1, FH, "grad_up0", tw), "u0")
    gw_out = by_dest(_grad_w(yq, dh1, D, "grad_out", tw), "a_out")
    keys = ("g0", "u0", "a_out")
    dx, dbcv, st_a, land = _mixa_bwd(
        dh1, xs, bq, cq, vq, ccq, a_norm, cw_a, wg, tma,
        _rs_comm(_pair_reduce([gw_g0, gw_u0, gw_out], keys), keys, land))
    gw_in = by_dest(_grad_w(dbcv, n0, D, "grad_in", tw), "a_in")

    st_small = st_b.at[SM_A_CONV:SM_A_CONV + KA].set(st_a[1:1 + KA])
    sm_dest = st_small.reshape(SM_F32_ROWS, NDEV, 128).transpose(1, 0, 2)
    sm_dest = sm_dest.at[:, SM_B_PW1:SM_B_PW1 + 2, :].set(st_b1[0].reshape(NDEV, 2, 128))
    repl = jnp.concatenate([st_a[0:1], st_f0[0:1], st_f1[0:1], st_fin[0:1], st_fin[1:2],
                            jnp.zeros((3, D), F32)], axis=0)[None]
    p_in, sm_land, rp_land = _pair_reduce([gw_in], ("a_in",), side=(sm_dest, repl))

    big = [("ffn_w_gate", ("g0", "g1"), False,
            rows_first(ffn_w_gate), rows_first(m_ffn_w_gate), rows_first(v_ffn_w_gate)),
           ("ffn_w_up", ("u0", "u1"), False, rows_first(ffn_w_up), rows_first(m_ffn_w_up), rows_first(v_ffn_w_up)),
           ("ffn_w_down", ("d0", "d1"), False, ffn_w_down, m_ffn_w_down, v_ffn_w_down),
           ("a_w_out", ("a_out",), False, a_w_out, m_a_w_out, v_a_w_out),
           ("b_w_pw1", ("b_pw1",), True, b_w_pw1, m_b_w_pw1, v_b_w_pw1),
           ("b_w_pw2", ("b_pw2",), False, b_w_pw2, m_b_w_pw2, v_b_w_pw2),
           ("a_w_in", ("a_in",), True, a_w_in, m_a_w_in, v_a_w_in)]
    res = {}
    for k, (name, keys, transposed, w, m, v) in enumerate(big):
        if k < 3:
            comm = _rs_comm(p_in, ("a_in",), land, part=(k, 3), body_reads_land=True)
            *res[name], land = _finish_weight(land, keys, transposed, w, m, v, name, comm)
        else:
            res[name] = _finish_weight(land, keys, transposed, w, m, v, name)
    for name in ("ffn_w_gate", "ffn_w_up"):
        res[name] = [rows_first(a) for a in res[name]]

    g_small, g_repl, loss8 = _sum_small(sm_land, rp_land)
    loss = loss8[0, 0]

    small_names = ["b_norm", "b_b_pw1", "b_conv", "b_b_conv", "b_ln_g", "b_ln_b", "b_b_pw2", "a_conv",
                   "a_norm", "ffn_norm", "final_norm"]
    triples = [(b_norm, m_b_norm, v_b_norm), (b_b_pw1, m_b_b_pw1, v_b_b_pw1), (b_conv, m_b_conv, v_b_conv),
               (b_b_conv, m_b_b_conv, v_b_b_conv), (b_ln_g, m_b_ln_g, v_b_ln_g), (b_ln_b, m_b_ln_b, v_b_ln_b),
               (b_b_pw2, m_b_b_pw2, v_b_b_pw2), (a_conv, m_a_conv, v_a_conv), (a_norm, m_a_norm, v_a_norm),
               (ffn_norm, m_ffn_norm, v_ffn_norm),
               (fin, m_final_norm.reshape(1, D), v_final_norm.reshape(1, D))]
    for name, quad in zip(small_names, _adamw_small(g_small, g_repl, triples)):
        res[name] = quad
    res["final_norm"] = tuple(a.reshape(D) for a in res["final_norm"])

    order = ["a_norm", "a_w_in", "a_conv", "a_w_out", "b_norm", "b_w_pw1", "b_b_pw1", "b_conv", "b_b_conv",
             "b_ln_g", "b_ln_b", "b_w_pw2", "b_b_pw2", "ffn_norm", "ffn_w_gate", "ffn_w_up", "ffn_w_down",
             "final_norm"]
    out = [loss, dx.reshape(1, T, D)]
    for j in range(4):
        out += [res[k][j] for k in order]
    return tuple(out)
```

```python
import functools

import jax
import jax.numpy as jnp
from jax import lax
from jax.experimental import pallas as pl
from jax.experimental.pallas import tpu as pltpu

F32 = jnp.float32
BF16 = jnp.bfloat16
MESH = pl.DeviceIdType.MESH

D = 1024
FF = 2816
FH = FF // 2
NDEV = 8
KA = 3
KB = 31
HALO_A = 8
HALO_B = 32
CONV_ROWS = 64
RMS_EPS = 1e-6
LN_EPS = 1e-5
LR, B1, B2, ADAM_EPS, WD, STEP = 0.001, 0.9, 0.999, 1e-08, 0.01, 10

TM = 256
TM_A = 512
TM_DW = 1024
VMEM_LIMIT = 56 * 1024 * 1024

N_ROWS = {"a_in": 384, "a_out": 128, "b_pw1": 256, "b_pw2": 128, "g0": 352, "u0": 352, "d0": 352,
          "g1": 352, "u1": 352, "d1": 352}
WEIGHT_KEYS = ("a_in", "a_out", "b_pw1", "b_pw2", "g0", "u0", "d0", "g1", "u1", "d1")
NCHIP = 4


def _offsets(order):
    off, o = {}, 0
    for k in order:
        off[k] = o
        o += N_ROWS[k]
    return off, o


PK_ORDER = ("a_in", "a_out", "g0", "u0", "d0", "b_pw1", "b_pw2", "g1", "u1", "d1")
PK_OFF, PK_ROWS = _offsets(PK_ORDER)
LD_ORDER = ("g1", "u1", "d1", "b_pw1", "b_pw2", "d0", "g0", "u0", "a_out", "a_in")
LD_OFF, N_WROWS = _offsets(LD_ORDER)


def _span(off, keys):
    return off[keys[0]], sum(N_ROWS[k] for k in keys)


SM_B_NORM, SM_B_PW1, SM_B_CONV, SM_B_BCONV, SM_LN_G, SM_LN_B, SM_B_PW2, SM_A_CONV = 0, 1, 3, 34, 35, 36, 37, 38
SM_USED = 41
SM_F32_ROWS = 64


def _pcall(body, **kw):
    return pl.pallas_call(body, **kw)


def _cparams(sem=None):
    return pltpu.CompilerParams(dimension_semantics=sem, vmem_limit_bytes=VMEM_LIMIT)


def _dot(a, b):
    return jnp.dot(a, b, preferred_element_type=F32)


def _dot_nt(a, b):
    return lax.dot_general(a, b, (((1,), (1,)), ((), ())), preferred_element_type=F32)


def _dot_tn(a, b):
    return lax.dot_general(a, b, (((0,), (0,)), ((), ())), preferred_element_type=F32)


def _sigmoid(v):
    return 1.0 / (1.0 + jnp.exp(-v))


def _rms_stat(x):
    return lax.rsqrt(jnp.mean(x * x, axis=-1, keepdims=True) + RMS_EPS)


def _rms_bwd(dn, x, r, gamma):
    dng = dn * gamma
    dx = r * dng - x * (r * r * r) * jnp.mean(dng * x, axis=-1, keepdims=True)
    return dx, jnp.sum(dn * x * r, axis=0, keepdims=True)


def _weight_loads(wg_ref, plan, sems):
    copies = []
    for j, (dst, key) in enumerate(plan):
        off, n = PK_OFF[key], N_ROWS[key]
        copies += [pltpu.make_async_copy(
            wg_ref.at[d, pl.ds(off, n), :], dst.at[pl.ds(d * n, n), :], sems.at[j * NDEV + d])
            for d in range(NDEV)]

    def load():
        for cp in copies:
            cp.start()
        for cp in copies:
            cp.wait()

    return load


_ANY = pl.BlockSpec(memory_space=pl.ANY)


def _row_spec(tm, width, rev_nt=None):
    if rev_nt is None:
        return pl.BlockSpec((tm, width), lambda i: (i, 0))
    return pl.BlockSpec((tm, width), lambda i: (rev_nt - 1 - i, 0))


def _full_spec(shape):
    return pl.BlockSpec(shape, lambda *_: (0,) * len(shape))


def _sds(shape, dtype):
    return jax.ShapeDtypeStruct(shape, dtype)


def _mesh_pos():
    return lax.axis_index("x"), lax.axis_index("y"), lax.axis_index("c")


def _lin(p):
    return 4 * p[0] + 2 * p[1] + p[2]


def _ag_exchange(src, slot, send_sems, recv_sems, local_sem):
    x, y, c = _mesh_pos()
    me, sibling = (x, y, c), (x, y, 1 - c)
    chips = [(1 - x, y), (x, 1 - y), (1 - x, 1 - y)]

    def copy(k, block, to, own=False):
        return pltpu.make_async_remote_copy(
            src_ref=src if own else slot(block), dst_ref=slot(block),
            send_sem=send_sems.at[k], recv_sem=recv_sems.at[k], device_id=to, device_id_type=MESH)

    mine = pltpu.make_async_copy(src, slot(me), local_sem)
    first = [copy(0, me, sibling, own=True)]
    first += [copy(1 + j, me, (*chip, c), own=True) for j, chip in enumerate(chips)]
    passed = [copy(4 + j, (*chip, c), sibling) for j, chip in enumerate(chips)]

    def start():
        mine.start()
        for cp in first:
            cp.start()

    def forward():
        for j, chip in enumerate(chips):
            copy(1 + j, (*chip, c), me).wait_recv()
            passed[j].start()

    def finish():
        copy(0, sibling, me).wait_recv()
        for j, chip in enumerate(chips):
            copy(4 + j, (*chip, 1 - c), me).wait_recv()
        for cp in first + passed:
            cp.wait_send()
        mine.wait()

    return start, forward, finish


def _chip_exchange(p_ref, land_ref, send_sems, recv_sems, keys, part):
    x, y, c = _mesh_pos()
    l0, rows = _span(LD_OFF, keys)
    rows = rows // part[1]
    p0 = part[0] * rows
    dst = land_ref.at[2 * x + y, pl.ds(l0 + p0, rows), :]
    peers = [((x + (j >> 1)) % 2, (y + (j & 1)) % 2) for j in range(NCHIP)]

    def copy(j):
        tx, ty = peers[j]
        src = p_ref.at[2 * tx + ty, pl.ds(p0, rows), :]
        if j == 0:
            return pltpu.make_async_copy(src, dst, send_sems.at[0])
        return pltpu.make_async_remote_copy(
            src_ref=src, dst_ref=dst, send_sem=send_sems.at[j], recv_sem=recv_sems.at[j],
            device_id=(tx, ty, c), device_id_type=MESH)

    def start():
        for j in range(NCHIP):
            copy(j).start()

    def finish():
        for j in range(NCHIP):
            copy(j).wait()

    return start, finish


def _all_to_all_f32(src_for, dst_ref, send_sems, recv_sems):
    x, y, c = _mesh_pos()
    dst = dst_ref.at[_lin((x, y, c))]
    peers = [((x + ((j >> 2) & 1)) % 2, (y + ((j >> 1) & 1)) % 2, (c + (j & 1)) % 2) for j in range(NDEV)]

    def copy(j):
        if j == 0:
            return pltpu.make_async_copy(src_for(_lin(peers[0])), dst, send_sems.at[0])
        return pltpu.make_async_remote_copy(
            src_ref=src_for(_lin(peers[j])), dst_ref=dst, send_sem=send_sems.at[j], recv_sem=recv_sems.at[j],
            device_id=peers[j], device_id_type=MESH)

    def start():
        for j in range(NDEV):
            copy(j).start()

    def finish():
        for j in range(NDEV):
            copy(j).wait()

    return start, finish


class _Comm:
    def __init__(self, ins, alias_in, out_shape, scratch, make, gives_wg, middle_step=0):
        self.ins, self.alias_in, self.out_shape = ins, alias_in, out_shape
        self.scratch, self.make, self.gives_wg, self.middle_step = scratch, make, gives_wg, middle_step


def _ag_comm(shard, wg, keys, forward_step=0):
    def make(c_ins, c_out, sc):
        return _weights_exchange(c_ins[0], c_out, sc[0], sc[1], sc[2], keys)

    return _Comm([shard, wg], 1, _sds(wg.shape, BF16),
                 [pltpu.SemaphoreType.DMA((7,)), pltpu.SemaphoreType.DMA((7,)), pltpu.SemaphoreType.DMA],
                 make, True, forward_step)


def _weights_exchange(shard_ref, wg_ref, send_sems, recv_sems, local_sem, keys):
    r0, nr = _span(PK_OFF, keys)
    return _ag_exchange(shard_ref.at[pl.ds(r0, nr), :], lambda p: wg_ref.at[_lin(p), pl.ds(r0, nr), :],
                        send_sems, recv_sems, local_sem)


def _rs_comm(psum, keys, land, part=(0, 1), body_reads_land=False):
    def make(c_ins, c_out, sc):
        start, finish = _chip_exchange(c_ins[0], c_out, sc[0], sc[1], keys, part)
        return start, None, finish

    ins = [psum] + ([] if land is None else [land])
    return _Comm(ins, None if land is None else 1, _sds((NCHIP, N_WROWS, D), BF16),
                 [pltpu.SemaphoreType.DMA((NCHIP,)), pltpu.SemaphoreType.DMA((NCHIP,))], make, body_reads_land)


def _hosted_call(body, name, nt, in_specs, out_specs, out_shape, scratch, args, comm):
    if comm is None:
        return _pcall(body, name=name, grid=(nt,), in_specs=in_specs, out_specs=tuple(out_specs),
                      out_shape=tuple(out_shape), scratch_shapes=scratch,
                      compiler_params=_cparams(("arbitrary",)))(*args)
    n_in, n_out, n_sc, n_cin = len(in_specs), len(out_specs), len(scratch), len(comm.ins)

    def wrapped(*refs):
        ins = refs[:n_in]
        c_ins = refs[n_in:n_in + n_cin]
        outs = refs[n_in + n_cin:n_in + n_cin + n_out]
        c_out = refs[n_in + n_cin + n_out]
        sc = refs[n_in + n_cin + n_out + 1:n_in + n_cin + n_out + 1 + n_sc]
        c_sc = refs[n_in + n_cin + n_out + 1 + n_sc:]
        start, middle, finish = comm.make(c_ins, c_out, c_sc)
        pl.when(pl.program_id(0) == 0)(start)
        if comm.gives_wg:
            body(*ins, c_out, *outs, *sc)
        else:
            body(*ins, *outs, *sc)
        if middle is not None:
            pl.when(pl.program_id(0) == max(nt - 1 - comm.middle_step, 0))(middle)
        pl.when(pl.program_id(0) == nt - 1)(finish)

    aliases = {} if comm.alias_in is None else {n_in + comm.alias_in: n_out}
    res = _pcall(wrapped, name=name, grid=(nt,),
                 in_specs=list(in_specs) + [_ANY] * n_cin, out_specs=tuple(out_specs) + (_ANY,),
                 out_shape=tuple(out_shape) + (comm.out_shape,),
                 scratch_shapes=list(scratch) + list(comm.scratch),
                 input_output_aliases=aliases,
                 compiler_params=_cparams(("arbitrary",)))(*args, *comm.ins)
    return res


def _pack_shard(weights, smalls):
    plan = (("a_in", 0, 0, True), ("a_out", 1, 0, False), ("b_pw1", 2, 0, True), ("b_pw2", 3, 0, False),
            ("g0", 4, 0, False), ("u0", 5, 0, False), ("d0", 6, 0, False),
            ("g1", 4, 1, False), ("u1", 5, 1, False), ("d1", 6, 1, False))
    n_pad = 384

    def body(*refs):
        out, sm, pad = refs[-3:]
        for key, idx, layer, transposed in plan:
            n = N_ROWS[key]
            val = refs[idx][layer]
            if transposed:
                if n % 128:
                    pad[:, pl.ds(0, n)] = val
                    pad[:, pl.ds(n, n_pad - n)] = jnp.zeros((D, n_pad - n), F32)
                    val = pad[...]
                val = val.T[:n]
            out[pl.ds(PK_OFF[key], n), :] = val.astype(BF16)
        b_norm, b_b_pw1, b_conv, b_b_conv, b_ln_g, b_ln_b, b_b_pw2, a_conv = refs[len(weights):-3]
        sm[...] = jnp.zeros((SM_F32_ROWS, 128), F32)
        sm[pl.ds(SM_B_PW1, 1), :] = b_b_pw1[:, pl.ds(0, 128)]
        sm[pl.ds(SM_B_PW1 + 1, 1), :] = b_b_pw1[:, pl.ds(128, 128)]
        sm[pl.ds(SM_B_CONV, KB), :] = b_conv[0]
        sm[pl.ds(SM_A_CONV, KA), :] = a_conv[0]
        for row, ref in ((SM_B_NORM, b_norm), (SM_B_BCONV, b_b_conv), (SM_LN_G, b_ln_g), (SM_LN_B, b_ln_b),
                         (SM_B_PW2, b_b_pw2)):
            sm[pl.ds(row, 1), :] = ref[...]

    vm = pl.BlockSpec(memory_space=pltpu.VMEM)
    return _pcall(
        body, name="pack_shard",
        out_shape=(_sds((PK_ROWS, D), BF16), _sds((SM_F32_ROWS, 128), F32)),
        in_specs=[vm] * (len(weights) + len(smalls)), out_specs=(vm, vm),
        scratch_shapes=[pltpu.VMEM((D, n_pad), F32)],
        compiler_params=_cparams(),
    )(*weights, *smalls)


def _all_gather_first(shard, small):
    def body(x_ref, s_ref, wg_ref, sg_ref, send_w, recv_w, local_w, send_s, recv_s, local_s):
        start_w, forward_w, finish_w = _weights_exchange(x_ref, wg_ref, send_w, recv_w, local_w, ("a_in", "a_out"))
        start_s, forward_s, finish_s = _ag_exchange(s_ref, lambda p: sg_ref.at[_lin(p)], send_s, recv_s, local_s)
        start_s()
        start_w()
        forward_s()
        forward_w()
        finish_s()
        finish_w()

    sems = [pltpu.SemaphoreType.DMA((7,)), pltpu.SemaphoreType.DMA((7,)), pltpu.SemaphoreType.DMA]
    return _pcall(
        body, name="all_gather_first",
        out_shape=(_sds((NDEV, PK_ROWS, D), BF16), _sds((NDEV, SM_F32_ROWS, 128), F32)),
        in_specs=[_ANY, _ANY], out_specs=(_ANY, _ANY), scratch_shapes=sems + sems,
    )(shard, small)


def _pair_reduce(grads, keys, side=None):
    l0, rows = _span(LD_OFF, keys)
    nk = len(keys)
    n_side = 0 if side is None else 2

    def body(c_ref, *refs):
        mine = refs[:nk]
        whole = refs[nk:2 * nk]
        o_ref = refs[2 * nk + n_side]
        got, send_sem, recv_sems = refs[2 * nk + 2 * n_side + 1:2 * nk + 2 * n_side + 4]
        t = pl.program_id(0)
        x, y, c = _mesh_pos()
        sibling = (x, y, 1 - c)
        if side is not None:
            sm_ref, rp_ref = refs[2 * nk:2 * nk + 2]
            sm_land, rp_land = refs[2 * nk + 3:2 * nk + 5]
            s2, r2, s3, r3 = refs[-4:]
            start_s, finish_s = _all_to_all_f32(lambda d: sm_ref.at[d], sm_land, s2, r2)
            start_r, finish_r = _all_to_all_f32(lambda d: rp_ref.at[0], rp_land, s3, r3)

        @pl.when(t == 0)
        def _():
            if side is not None:
                start_s()
                start_r()
            for ref, key in zip(whole, keys):
                for d in range(NCHIP):
                    pltpu.make_async_remote_copy(
                        src_ref=ref.at[2 * d + (1 - c)], dst_ref=got.at[d, pl.ds(LD_OFF[key] - l0, N_ROWS[key]), :],
                        send_sem=send_sem, recv_sem=recv_sems.at[d], device_id=sibling, device_id_type=MESH).start()

        pltpu.make_async_remote_copy(src_ref=got.at[t], dst_ref=got.at[t], send_sem=send_sem,
                                     recv_sem=recv_sems.at[t], device_id=sibling, device_id_type=MESH).wait_recv()
        for ref, key in zip(mine, keys):
            sl = pl.ds(LD_OFF[key] - l0, N_ROWS[key])
            o_ref[sl, :] = (ref[...].astype(F32) + got[t, sl, :].astype(F32)).astype(BF16)

        @pl.when(t == NCHIP - 1)
        def _():
            pltpu.make_async_remote_copy(src_ref=got, dst_ref=got, send_sem=send_sem, recv_sem=recv_sems.at[0],
                                         device_id=sibling, device_id_type=MESH).wait_send()
            if side is not None:
                finish_s()
                finish_r()

    out_specs = [pl.BlockSpec((None, rows, D), lambda t, c: (t, 0, 0))]
    out_shape = [_sds((NCHIP, rows, D), BF16)]
    scratch = [pltpu.VMEM((NCHIP, rows, D), BF16), pltpu.SemaphoreType.DMA, pltpu.SemaphoreType.DMA((NCHIP,))]
    if side is not None:
        out_specs += [_ANY, _ANY]
        out_shape += [_sds((NDEV, SM_F32_ROWS, 128), F32), _sds((NDEV, 8, D), F32)]
        scratch += [pltpu.SemaphoreType.DMA((NDEV,))] * 4
    grid_spec = pltpu.PrefetchScalarGridSpec(
        num_scalar_prefetch=1, grid=(NCHIP,),
        in_specs=[pl.BlockSpec((None, N_ROWS[k], D), lambda t, c: (2 * t + c[0], 0, 0)) for k in keys]
        + [_ANY] * (nk + n_side),
        out_specs=tuple(out_specs), scratch_shapes=scratch)
    core = lax.axis_index("c").astype(jnp.int32).reshape(1)
    res = _pcall(
        body, name="pair_reduce_" + keys[0], grid_spec=grid_spec, out_shape=tuple(out_shape),
        compiler_params=_cparams(("arbitrary",)),
    )(core, *grads, *grads, *(side or ()))
    return res[0] if side is None else res


def _mixa_fwd(x, gam, cw, wg, tm, comm):
    T = x.shape[0]

    def body(x_ref, gam_ref, cw_ref, wg_ref, h_ref, n_ref, b_ref, c_ref, v_ref, cc_ref, y_ref,
             win, wout, buf, sems):
        first = pl.program_id(0) == 0
        load_w = _weight_loads(wg_ref, [(win, "a_in"), (wout, "a_out")], sems)

        @pl.when(first)
        def _():
            load_w()
            buf[pl.ds(0, HALO_A), :] = jnp.zeros((HALO_A, D), F32)

        xv = x_ref[...]
        nb = (xv * _rms_stat(xv) * gam_ref[...]).astype(BF16)
        n_ref[...] = nb
        bv = _dot_nt(nb, win[pl.ds(0, D), :])
        cval = _dot_nt(nb, win[pl.ds(D, D), :])
        vval = _dot_nt(nb, win[pl.ds(2 * D, D), :])
        cv = cval * vval
        buf[pl.ds(HALO_A, tm), :] = cv
        cc = cw_ref[pl.ds(KA - 1, 1), :] * cv
        for k in range(KA - 1):
            cc = cc + cw_ref[pl.ds(k, 1), :] * buf[pl.ds(HALO_A - (KA - 1) + k, tm), :]
        buf[pl.ds(0, HALO_A), :] = buf[pl.ds(tm, HALO_A), :]
        yb = (bv * cc).astype(BF16)
        b_ref[...] = bv.astype(BF16)
        c_ref[...] = cval.astype(BF16)
        v_ref[...] = vval.astype(BF16)
        cc_ref[...] = cc.astype(BF16)
        y_ref[...] = yb
        h_ref[...] = xv + _dot(yb, wout[...])

    act = _sds((T, D), BF16)
    return _hosted_call(
        body, "mixa_fwd", T // tm,
        in_specs=[_row_spec(tm, D), _full_spec((1, D)), _full_spec((KA, D))],
        out_specs=[_row_spec(tm, D) for _ in range(7)],
        out_shape=(_sds((T, D), F32),) + (act,) * 6,
        scratch=[pltpu.VMEM((3 * D, D), BF16), pltpu.VMEM((D, D), BF16),
                 pltpu.VMEM((HALO_A + tm, D), F32), pltpu.SemaphoreType.DMA((2 * NDEV,))],
        args=(x, gam, cw), comm=comm)


def _ffn_fwd(h, gam, wg, layer, tm, comm, head=None):
    T = h.shape[0]
    kg, ku, kd = "g%d" % layer, "u%d" % layer, "d%d" % layer
    n_head = 0 if head is None else 2

    def body(*refs):
        h_ref, gam_ref = refs[:2]
        wg_ref = refs[2 + n_head]
        o_ref, n_ref, g_ref, u_ref, gu_ref = refs[3 + n_head:8 + n_head]
        wgt, wut, wd, sems = refs[-4:]

        first = pl.program_id(0) == 0
        load_w = _weight_loads(wg_ref, [(wgt, kg), (wut, ku), (wd, kd)], sems)
        pl.when(first)(load_w)

        hv = h_ref[...]
        nb = (hv * _rms_stat(hv) * gam_ref[...]).astype(BF16)
        n_ref[...] = nb
        out = hv
        for f in range(2):
            cols = pl.ds(f * FH, FH)
            g = _dot_nt(nb, wgt[cols, :])
            u = _dot_nt(nb, wut[cols, :])
            gu = (g * _sigmoid(g) * u).astype(BF16)
            g_ref[:, cols] = g.astype(BF16)
            u_ref[:, cols] = u.astype(BF16)
            gu_ref[:, cols] = gu
            out = out + _dot(gu, wd[cols, :])
        if head is None:
            o_ref[...] = out
        else:
            t_ref, fin_ref, st_ref = refs[2], refs[3], refs[8 + n_head]

            @pl.when(pl.program_id(0) == 0)
            def _():
                st_ref[...] = jnp.zeros((8, D), F32)

            gamma = fin_ref[...]
            r = _rms_stat(out)
            err = out * r * gamma - t_ref[...]
            dx, dgam = _rms_bwd(err * (1.0 / D), out, r, gamma)
            o_ref[...] = dx
            st_ref[pl.ds(0, 1), :] += dgam
            st_ref[pl.ds(1, 1), :] += (0.5 / D) * jnp.sum(err * err, axis=0, keepdims=True)

    pre = _sds((T, FF), BF16)
    in_specs = [_row_spec(tm, D), _full_spec((1, D))]
    args = (h, gam)
    out_specs = [_row_spec(tm, D), _row_spec(tm, D), _row_spec(tm, FF), _row_spec(tm, FF), _row_spec(tm, FF)]
    out_shape = (_sds((T, D), F32), _sds((T, D), BF16), pre, pre, pre)
    if head is not None:
        in_specs, args = in_specs + [_row_spec(tm, D), _full_spec((1, D))], args + tuple(head)
        out_specs, out_shape = out_specs + [_full_spec((8, D))], out_shape + (_sds((8, D), F32),)
    if comm is None:
        in_specs, args = in_specs + [_ANY], args + (wg,)
    return _hosted_call(
        body, "ffn%d_fwd" % layer, T // tm,
        in_specs=in_specs, out_specs=out_specs, out_shape=out_shape,
        scratch=[pltpu.VMEM((FF, D), BF16)] * 3 + [pltpu.SemaphoreType.DMA((3 * NDEV,))],
        args=args, comm=comm)


def _shifted_copies(buf, shf, tm):
    for r in range(1, 8):
        shf[r - 1] = buf[pl.ds(r, tm + HALO_B - 8), :]


def _broadcast_taps(vec_ref, wb):
    for k in range(KB):
        wb[k] = jnp.broadcast_to(vec_ref[pl.ds(SM_B_CONV + k, 1), :], (8, D))
    wb[KB] = jnp.broadcast_to(vec_ref[pl.ds(SM_B_BCONV, 1), :], (8, D))


def _taps_by_shift_residue(taps):
    groups = {}
    for k, shift in taps:
        q, r = divmod(shift, 8)
        groups.setdefault(r, []).append((k, q))
    return sorted(groups.items())


def _window(buf, shf, base, r, q0, n_groups, lanes):
    rows = pl.ds(base + 8 * q0, 8 * n_groups)
    v = buf[rows, lanes] if r == 0 else shf[r - 1, rows, lanes]
    return [v[8 * i:8 * i + 8] for i in range(n_groups)]


def _long_conv(buf, shf, wb, out_ref, tm, taps, bias_row):
    n_acc = CONV_ROWS // 8
    groups = _taps_by_shift_residue(taps)
    for col in range(D // 128):
        lanes = pl.ds(128 * col, 128)

        def rows(i, carry, lanes=lanes):
            base = i * CONV_ROWS
            init = jnp.zeros((8, 128), F32) if bias_row is None else wb[bias_row, :, lanes]
            accs = [init] * n_acc
            for r, lst in groups:
                q0, q1 = min(q for _, q in lst), max(q for _, q in lst)
                win = _window(buf, shf, base, r, q0, n_acc + q1 - q0, lanes)
                for k, q in lst:
                    wk = wb[k, :, lanes]
                    accs = [acc + wk * win[h + q - q0] for h, acc in enumerate(accs)]
            for h, acc in enumerate(accs):
                out_ref[pl.ds(base + 8 * h, 8), lanes] = acc
            return carry

        for i in range(tm // CONV_ROWS):
            rows(i, 0)


def _long_conv_grad_taps(buf, shf, x_ref, wacc, tm):
    n_acc = CONV_ROWS // 8
    groups = _taps_by_shift_residue([(KB - 1 - j, j) for j in range(KB)])
    for col in range(D // 128):
        lanes = pl.ds(128 * col, 128)

        def rows(i, carry, lanes=lanes):
            base = i * CONV_ROWS
            xv = x_ref[pl.ds(base, CONV_ROWS), lanes]
            xs = [xv[8 * h:8 * h + 8] for h in range(n_acc)]
            for r, lst in groups:
                q0, q1 = min(q for _, q in lst), max(q for _, q in lst)
                win = _window(buf, shf, base, r, q0, n_acc + q1 - q0, lanes)
                for k, q in lst:
                    prod = [x * win[h + q - q0] for h, x in enumerate(xs)]
                    wacc[k, :, lanes] += functools.reduce(lambda a, b: a + b, prod)
            return carry

        for i in range(tm // CONV_ROWS):
            rows(i, 0)


def _ln_stats(dc):
    mu = jnp.mean(dc, axis=-1, keepdims=True)
    xc = dc - mu
    rstd = lax.rsqrt(jnp.mean(xc * xc, axis=-1, keepdims=True) + LN_EPS)
    return xc * rstd, rstd


def _mixb_fwd(h, vecs, bias1, wg, tm, comm):
    T = h.shape[0]

    def body(h_ref, vec_ref, b1_ref, wg_ref, o_ref, n_ref, a_ref, g_ref, dc_ref, s_ref, w1, w2, buf, shf, wb, sems):
        first = pl.program_id(0) == 0
        load_w = _weight_loads(wg_ref, [(w1, "b_pw1"), (w2, "b_pw2")], sems)

        @pl.when(first)
        def _():
            load_w()
            buf[pl.ds(0, HALO_B), :] = jnp.zeros((HALO_B, D), F32)
            _broadcast_taps(vec_ref, wb)

        hv = h_ref[...]
        nb = (hv * _rms_stat(hv) * vec_ref[pl.ds(SM_B_NORM, 1), :]).astype(BF16)
        n_ref[...] = nb
        a = _dot_nt(nb, w1[pl.ds(0, D), :]) + b1_ref[:, pl.ds(0, D)]
        g = _dot_nt(nb, w1[pl.ds(D, D), :]) + b1_ref[:, pl.ds(D, D)]
        a_ref[...] = a.astype(BF16)
        g_ref[...] = g.astype(BF16)
        buf[pl.ds(HALO_B, tm), :] = a * _sigmoid(g)
        _shifted_copies(buf, shf, tm)

        _long_conv(buf, shf, wb, dc_ref, tm, [(k, HALO_B - (KB - 1) + k) for k in range(KB)], KB)
        buf[pl.ds(0, HALO_B), :] = buf[pl.ds(tm, HALO_B), :]
        xhat, _ = _ln_stats(dc_ref[...])
        ln = xhat * vec_ref[pl.ds(SM_LN_G, 1), :] + vec_ref[pl.ds(SM_LN_B, 1), :]
        s = (ln * _sigmoid(ln)).astype(BF16)
        s_ref[...] = s
        o_ref[...] = hv + _dot(s, w2[...]) + vec_ref[pl.ds(SM_B_PW2, 1), :]

    act = _sds((T, D), BF16)
    return _hosted_call(
        body, "mixb_fwd", T // tm,
        in_specs=[_row_spec(tm, D), _full_spec((SM_F32_ROWS, D)), _full_spec((1, 2 * D))],
        out_specs=[_row_spec(tm, D) for _ in range(6)],
        out_shape=(_sds((T, D), F32), act, act, act, _sds((T, D), F32), act),
        scratch=[pltpu.VMEM((2 * D, D), BF16), pltpu.VMEM((D, D), BF16),
                 pltpu.VMEM((HALO_B + tm, D), F32), pltpu.VMEM((7, HALO_B + tm - 8, D), F32),
                 pltpu.VMEM((KB + 1, 8, D), F32), pltpu.SemaphoreType.DMA((2 * NDEV,))],
        args=(h, vecs, bias1), comm=comm)


def _ffn_bwd_dx(dh, h, g, u, gam, wg, layer, tm, comm):
    T = h.shape[0]
    kg, ku, kd = "g%d" % layer, "u%d" % layer, "d%d" % layer

    def body(dh_ref, h_ref, g_ref, u_ref, gam_ref, wg_ref, o_ref, dg_ref, du_ref, st_ref, wgt, wut, wd, sems):
        first = pl.program_id(0) == 0
        load_w = _weight_loads(wg_ref, [(wd, kd), (wgt, kg), (wut, ku)], sems)

        @pl.when(first)
        def _():
            load_w()
            st_ref[...] = jnp.zeros((8, D), F32)

        dhv = dh_ref[...]
        dhb = dhv.astype(BF16)
        dn = jnp.zeros_like(dhv)
        for f in range(2):
            cols = pl.ds(f * FH, FH)
            dgu = _dot_nt(dhb, wd[cols, :])
            gv = g_ref[:, cols].astype(F32)
            sg = _sigmoid(gv)
            du = (dgu * gv * sg).astype(BF16)
            dg = (dgu * u_ref[:, cols].astype(F32) * (sg * (1.0 + gv * (1.0 - sg)))).astype(BF16)
            dg_ref[:, cols] = dg
            du_ref[:, cols] = du
            dn = dn + _dot(dg, wgt[cols, :]) + _dot(du, wut[cols, :])
        hv = h_ref[...]
        dx, dgam = _rms_bwd(dn, hv, _rms_stat(hv), gam_ref[...])
        o_ref[...] = dhv + dx
        st_ref[pl.ds(0, 1), :] += dgam

    pre = _sds((T, FF), BF16)
    return _hosted_call(
        body, "ffn%d_bwd_dx" % layer, T // tm,
        in_specs=[_row_spec(tm, D), _row_spec(tm, D), _row_spec(tm, FF), _row_spec(tm, FF),
                  _full_spec((1, D)), _ANY],
        out_specs=[_row_spec(tm, D), _row_spec(tm, FF), _row_spec(tm, FF), _full_spec((8, D))],
        out_shape=(_sds((T, D), F32), pre, pre, _sds((8, D), F32)),
        scratch=[pltpu.VMEM((FF, D), BF16)] * 3 + [pltpu.SemaphoreType.DMA((3 * NDEV,))],
        args=(dh, h, g, u, gam, wg), comm=comm)


def _grad_w(lhs, rhs, mc, name, tm):
    T, M = lhs.shape
    nt = T // tm

    def body(l_ref, r_ref, o_ref, acc):
        i = pl.program_id(1)

        @pl.when(i == 0)
        def _():
            acc[...] = jnp.zeros((mc, D), F32)

        acc[...] += _dot_tn(l_ref[...], r_ref[...].astype(BF16))

        @pl.when(i == nt - 1)
        def _():
            o_ref[...] = acc[...].astype(BF16)

    return _pcall(
        body, name=name, grid=(M // mc, nt),
        out_shape=_sds((M, D), BF16),
        in_specs=[pl.BlockSpec((tm, mc), lambda j, i: (i, j)), pl.BlockSpec((tm, D), lambda j, i: (i, 0))],
        out_specs=pl.BlockSpec((mc, D), lambda j, i: (j, 0)),
        scratch_shapes=[pltpu.VMEM((mc, D), F32)],
        compiler_params=_cparams(("arbitrary", "arbitrary")),
    )(lhs, rhs)


def _mixb_bwd(dh, h, a, g, dc, vecs, wg, tm, comm):
    T = h.shape[0]
    nt = T // tm

    def body(dh_ref, h_ref, a_ref, g_ref, dc_ref, vec_ref, wg_ref, o_ref, du_ref, st_ref, sb_ref,
             w1, w2, buf, shf, glu_s, dglu_s, wacc, wb, sems):
        first = pl.program_id(0) == 0
        load_w = _weight_loads(wg_ref, [(w2, "b_pw2"), (w1, "b_pw1")], sems)

        @pl.when(first)
        def _():
            load_w()
            buf[pl.ds(tm, HALO_B), :] = jnp.zeros((HALO_B, D), F32)
            wacc[...] = jnp.zeros((KB, 8, D), F32)
            _broadcast_taps(vec_ref, wb)
            st_ref[...] = jnp.zeros((SM_F32_ROWS, D), F32)
            sb_ref[...] = jnp.zeros((8, 2 * D), F32)

        def acc(row, val):
            st_ref[pl.ds(row, 1), :] += jnp.sum(val, axis=0, keepdims=True)

        dhv = dh_ref[...]
        acc(SM_B_PW2, dhv)
        ds = _dot_nt(dhv.astype(BF16), w2[...])
        xhat, rstd = _ln_stats(dc_ref[...])
        ln_g = vec_ref[pl.ds(SM_LN_G, 1), :]
        ln = xhat * ln_g + vec_ref[pl.ds(SM_LN_B, 1), :]
        sl = _sigmoid(ln)
        dln = ds * (sl * (1.0 + ln * (1.0 - sl)))
        acc(SM_LN_G, dln * xhat)
        acc(SM_LN_B, dln)
        dxh = dln * ln_g
        ddc = rstd * (dxh - jnp.mean(dxh, axis=-1, keepdims=True)
                      - xhat * jnp.mean(dxh * xhat, axis=-1, keepdims=True))
        acc(SM_B_BCONV, ddc)
        buf[pl.ds(0, tm), :] = ddc
        _shifted_copies(buf, shf, tm)
        av = a_ref[...].astype(F32)
        sg = _sigmoid(g_ref[...].astype(F32))
        glu_s[...] = av * sg

        _long_conv(buf, shf, wb, dglu_s, tm, [(KB - 1 - j, j) for j in range(KB)], None)
        _long_conv_grad_taps(buf, shf, glu_s, wacc, tm)
        buf[pl.ds(tm, HALO_B), :] = buf[pl.ds(0, HALO_B), :]
        dglu = dglu_s[...]
        da = dglu * sg
        dg = dglu * av * sg * (1.0 - sg)
        sb_ref[pl.ds(0, 1), pl.ds(0, D)] += jnp.sum(da, axis=0, keepdims=True)
        sb_ref[pl.ds(0, 1), pl.ds(D, D)] += jnp.sum(dg, axis=0, keepdims=True)
        dab, dgb = da.astype(BF16), dg.astype(BF16)
        du_ref[:, pl.ds(0, D)] = dab
        du_ref[:, pl.ds(D, D)] = dgb
        dn = _dot(dab, w1[pl.ds(0, D), :]) + _dot(dgb, w1[pl.ds(D, D), :])
        hv = h_ref[...]
        dx, dgam = _rms_bwd(dn, hv, _rms_stat(hv), vec_ref[pl.ds(SM_B_NORM, 1), :])
        o_ref[...] = dhv + dx
        st_ref[pl.ds(SM_B_NORM, 1), :] += dgam

        @pl.when(pl.program_id(0) == nt - 1)
        def _():
            st_ref[pl.ds(SM_B_CONV, KB), :] = jnp.sum(wacc[...], axis=1)

    rs = functools.partial(_row_spec, rev_nt=nt)
    return _hosted_call(
        body, "mixb_bwd", nt,
        in_specs=[rs(tm, D), rs(tm, D), rs(tm, D), rs(tm, D), rs(tm, D), _full_spec((SM_F32_ROWS, D)), _ANY],
        out_specs=[rs(tm, D), rs(tm, 2 * D), _full_spec((SM_F32_ROWS, D)), _full_spec((8, 2 * D))],
        out_shape=(_sds((T, D), F32), _sds((T, 2 * D), BF16), _sds((SM_F32_ROWS, D), F32), _sds((8, 2 * D), F32)),
        scratch=[pltpu.VMEM((2 * D, D), BF16), pltpu.VMEM((D, D), BF16),
                 pltpu.VMEM((tm + HALO_B, D), F32), pltpu.VMEM((7, tm + HALO_B - 8, D), F32),
                 pltpu.VMEM((tm, D), F32), pltpu.VMEM((tm, D), F32), pltpu.VMEM((KB, 8, D), F32),
                 pltpu.VMEM((KB + 1, 8, D), F32), pltpu.SemaphoreType.DMA((2 * NDEV,))],
        args=(dh, h, a, g, dc, vecs, wg), comm=comm)


def _mixa_bwd(dh, x, b, c, v, cc, gam, cw, wg, tm, comm):
    T = x.shape[0]
    nt = T // tm

    def body(dh_ref, x_ref, b_ref, c_ref, v_ref, cc_ref, gam_ref, cw_ref, wg_ref,
             o_ref, db_ref, st_ref, win, wout, buf, sems):
        first = pl.program_id(0) == 0
        load_w = _weight_loads(wg_ref, [(wout, "a_out"), (win, "a_in")], sems)

        @pl.when(first)
        def _():
            load_w()
            buf[pl.ds(tm, HALO_A), :] = jnp.zeros((HALO_A, D), F32)
            st_ref[...] = jnp.zeros((8, D), F32)

        dhv = dh_ref[...]
        dy = _dot_nt(dhv.astype(BF16), wout[...])
        cval = c_ref[...].astype(F32)
        vval = v_ref[...].astype(F32)
        d_b = (dy * cc_ref[...].astype(F32)).astype(BF16)
        buf[pl.ds(0, tm), :] = dy * b_ref[...].astype(F32)
        cv = cval * vval
        dcv = jnp.zeros((tm, D), F32)
        for j in range(KA):
            sh = buf[pl.ds(j, tm), :]
            k = KA - 1 - j
            dcv = dcv + cw_ref[pl.ds(k, 1), :] * sh
            st_ref[pl.ds(1 + k, 1), :] += jnp.sum(cv * sh, axis=0, keepdims=True)
        buf[pl.ds(tm, HALO_A), :] = buf[pl.ds(0, HALO_A), :]
        d_c = (dcv * vval).astype(BF16)
        d_v = (dcv * cval).astype(BF16)
        db_ref[:, pl.ds(0, D)] = d_b
        db_ref[:, pl.ds(D, D)] = d_c
        db_ref[:, pl.ds(2 * D, D)] = d_v
        dn = _dot(d_b, win[pl.ds(0, D), :]) + _dot(d_c, win[pl.ds(D, D), :]) + _dot(d_v, win[pl.ds(2 * D, D), :])
        xv = x_ref[...]
        dx, dgam = _rms_bwd(dn, xv, _rms_stat(xv), gam_ref[...])
        o_ref[...] = dhv + dx
        st_ref[pl.ds(0, 1), :] += dgam

    rs = functools.partial(_row_spec, rev_nt=nt)
    return _hosted_call(
        body, "mixa_bwd", nt,
        in_specs=[rs(tm, D) for _ in range(6)] + [_full_spec((1, D)), _full_spec((KA, D)), _ANY],
        out_specs=[rs(tm, D), rs(tm, 3 * D), _full_spec((8, D))],
        out_shape=(_sds((T, D), F32), _sds((T, 3 * D), BF16), _sds((8, D), F32)),
        scratch=[pltpu.VMEM((3 * D, D), BF16), pltpu.VMEM((D, D), BF16),
                 pltpu.VMEM((tm + HALO_A, D), F32), pltpu.SemaphoreType.DMA((2 * NDEV,))],
        args=(dh, x, b, c, v, cc, gam, cw, wg), comm=comm)


def _sum_small(sm, rp):
    def body(sm_ref, rp_ref, osm, orp, oloss):
        a = sm_ref[0]
        b = rp_ref[0]
        for s in range(1, NDEV):
            a = a + sm_ref[s]
            b = b + rp_ref[s]
        osm[...] = a
        orp[...] = b
        oloss[...] = jnp.zeros((8, 128), F32) + jnp.sum(b[4:5, :], axis=-1, keepdims=True)

    return _pcall(
        body, name="sum_small_grads",
        out_shape=(_sds((SM_F32_ROWS, 128), F32), _sds((8, D), F32), _sds((8, 128), F32)),
        in_specs=[pl.BlockSpec(memory_space=pltpu.VMEM)] * 2,
        out_specs=tuple(pl.BlockSpec(memory_space=pltpu.VMEM) for _ in range(3)),
        compiler_params=_cparams(),
    )(sm, rp)


def _adam_math(w, g, m, v):
    mn = B1 * m + (1.0 - B1) * g
    vn = B2 * v + (1.0 - B2) * (g * g)
    m_hat = mn / (1.0 - B1 ** STEP)
    v_hat = vn / (1.0 - B2 ** STEP)
    return -LR * (m_hat / (jnp.sqrt(v_hat) + ADAM_EPS) + WD * w), mn, vn


def _finish_weight(land, keys, transposed, w, m, v, name, comm=None):
    layers, rows, cols = w.shape
    n = N_ROWS[keys[0]]
    n_pad = -(-n // 128) * 128 if transposed else n

    def body(w_ref, m_ref, v_ref, land_ref, og, od, om, ov, buf, sem):
        off = LD_OFF[keys[0]]
        if layers == 2:
            off = jnp.where(pl.program_id(0) == 0, off, LD_OFF[keys[1]])
        cp = pltpu.make_async_copy(land_ref.at[:, pl.ds(pl.multiple_of(off, 32), n), :], buf, sem)
        cp.start()
        cp.wait()
        g = buf[0].astype(F32)
        for s in range(1, NCHIP):
            g = g + buf[s].astype(F32)
        if transposed:
            if n_pad != n:
                g = jnp.concatenate([g, jnp.zeros((n_pad - n, D), F32)], axis=0)
            g = g.T[:, :n]
        d, mn, vn = _adam_math(w_ref[...], g, m_ref[...], v_ref[...])
        og[...] = g
        od[...] = d
        om[...] = mn
        ov[...] = vn

    spec = pl.BlockSpec((None, rows, cols), lambda l: (l, 0, 0))
    shp = _sds(w.shape, F32)
    in_specs, args = [spec, spec, spec], (w, m, v)
    if comm is None:
        in_specs, args = in_specs + [_ANY], args + (land,)
    return _hosted_call(
        body, "finish_" + name, layers,
        in_specs=in_specs, out_specs=[spec] * 4, out_shape=(shp,) * 4,
        scratch=[pltpu.VMEM((NCHIP, n, D), BF16), pltpu.SemaphoreType.DMA],
        args=args, comm=comm)


SMALL_PARAMS = (("sm", SM_B_NORM, 1), ("sm", SM_B_PW1, 2), ("sm", SM_B_CONV, KB), ("sm", SM_B_BCONV, 1),
                ("sm", SM_LN_G, 1), ("sm", SM_LN_B, 1), ("sm", SM_B_PW2, 1), ("sm", SM_A_CONV, KA),
                ("rp", 0, 1), ("rp", 1, 2), ("rp", 3, 1))


def _adamw_small(g_small, g_repl, triples):
    n = len(SMALL_PARAMS)

    def body(*refs):
        gs_ref, gr_ref = refs[0], refs[1]
        ins = refs[2:2 + 3 * n]
        outs = refs[2 + 3 * n:]
        for i, (src, r0, nr) in enumerate(SMALL_PARAMS):
            w_ref, m_ref, v_ref = ins[3 * i:3 * i + 3]
            g = (gs_ref if src == "sm" else gr_ref)[pl.ds(r0, nr), :]
            if i == 1:
                g = jnp.concatenate([g[0:1], g[1:2]], axis=1)
            lead = (0,) if len(w_ref.shape) == 3 else ()
            idx = lead + (slice(None), slice(None))
            vals = (g,) + _adam_math(w_ref[idx], g, m_ref[idx], v_ref[idx])
            for o_ref, val in zip(outs[4 * i:4 * i + 4], vals):
                o_ref[idx] = val

    flat = [a for t in triples for a in t]
    out_shape = tuple(_sds(t[0].shape, F32) for t in triples for _ in range(4))
    vm = pl.BlockSpec(memory_space=pltpu.VMEM)
    res = _pcall(
        body, name="adamw_small", out_shape=out_shape,
        in_specs=[vm] * (2 + len(flat)), out_specs=tuple(vm for _ in out_shape),
        compiler_params=_cparams(),
    )(g_small, g_repl, *flat)
    return [tuple(res[4 * i:4 * i + 4]) for i in range(n)]


def kernel(x, a_norm, a_w_in, a_conv, a_w_out, b_norm, b_w_pw1, b_b_pw1, b_conv, b_b_conv, b_ln_g, b_ln_b, b_w_pw2, b_b_pw2, ffn_norm, ffn_w_gate, ffn_w_up, ffn_w_down, final_norm, loss_target, m_a_norm, m_a_w_in, m_a_conv, m_a_w_out, m_b_norm, m_b_w_pw1, m_b_b_pw1, m_b_conv, m_b_b_conv, m_b_ln_g, m_b_ln_b, m_b_w_pw2, m_b_b_pw2, m_ffn_norm, m_ffn_w_gate, m_ffn_w_up, m_ffn_w_down, m_final_norm, v_a_norm, v_a_w_in, v_a_conv, v_a_w_out, v_b_norm, v_b_w_pw1, v_b_b_pw1, v_b_conv, v_b_b_conv, v_b_ln_g, v_b_ln_b, v_b_w_pw2, v_b_b_pw2, v_ffn_norm, v_ffn_w_gate, v_ffn_w_up, v_ffn_w_down, v_final_norm):
    T = x.shape[1]
    tm = min(TM, T)
    tma = min(TM_A, T)
    tw = min(TM_DW, T)
    xs = x.reshape(T, D)
    tgt = loss_target.reshape(T, D)

    def rows_first(a):
        return jnp.swapaxes(a, 1, 2)

    shard, small = _pack_shard(
        (a_w_in, a_w_out, b_w_pw1, b_w_pw2, rows_first(ffn_w_gate), rows_first(ffn_w_up), ffn_w_down),
        (b_norm, b_b_pw1, b_conv, b_b_conv, b_ln_g, b_ln_b, b_b_pw2, a_conv))
    wg, sm_all = _all_gather_first(shard, small)
    vecs = sm_all.transpose(1, 0, 2).reshape(SM_F32_ROWS, D)
    bias1 = sm_all[:, SM_B_PW1:SM_B_PW1 + 2, :].reshape(1, 2 * D)
    cw_a = vecs[SM_A_CONV:SM_A_CONV + KA]
    fn0, fn1 = ffn_norm[0:1], ffn_norm[1:2]
    fin = final_norm.reshape(1, D)

    h1, n0, bq, cq, vq, ccq, yq, wg = _mixa_fwd(xs, a_norm, cw_a, wg, tma, _ag_comm(shard, wg, ("g0", "u0", "d0")))
    h2, n1, g0, u0, gu0, wg = _ffn_fwd(h1, fn0, wg, 0, tma, _ag_comm(shard, wg, ("b_pw1", "b_pw2"), (T // tma) // 2))
    h3, n2, aq, gq, dcq, sq, wg = _mixb_fwd(h2, vecs, bias1, wg, tm, _ag_comm(shard, wg, ("g1", "u1", "d1"), 1))
    dh4, n3, g1, u1, gu1, st_fin = _ffn_fwd(h3, fn1, wg, 1, tma, None, head=(tgt, fin))

    def by_dest(gw, key):
        return gw.reshape(NDEV, N_ROWS[key], D)

    gw_d1 = by_dest(_grad_w(gu1, dh4, FH, "grad_down1", tw), "d1")
    dh3, dg1, du1, st_f1 = _ffn_bwd_dx(dh4, h3, g1, u1, fn1, wg, 1, tm, None)
    gw_g1 = by_dest(_grad_w(dg1, n3, FH, "grad_gate1", tw), "g1")
    gw_u1 = by_dest(_grad_w(du1, n3, FH, "grad_up1", tw), "u1")
    keys = ("g1", "u1", "d1")
    dh2, dub, st_b, st_b1, land = _mixb_bwd(
        dh3, h2, aq, gq, dcq, vecs, wg, tm, _rs_comm(_pair_reduce([gw_g1, gw_u1, gw_d1], keys), keys, None))
    gw_pw1 = by_dest(_grad_w(dub, n2, D, "grad_pw1", tw), "b_pw1")
    gw_pw2 = by_dest(_grad_w(sq, dh3, D, "grad_pw2", tw), "b_pw2")
    gw_d0 = by_dest(_grad_w(gu0, dh2, FH, "grad_down0", tw), "d0")
    keys = ("b_pw1", "b_pw2", "d0")
    dh1, dg0, du0, st_f0, land = _ffn_bwd_dx(
        dh2, h1, g0, u0, fn0, wg, 0, tm, _rs_comm(_pair_reduce([gw_pw1, gw_pw2, gw_d0], keys), keys, land))
    gw_g0 = by_dest(_grad_w(dg0, n1, FH, "grad_gate0", tw), "g0")
    gw_u0 = by_dest(_grad_w(du0, n1, FH, "grad_up0", tw), "u0")
    gw_out = by_dest(_grad_w(yq, dh1, D, "grad_out", tw), "a_out")
    keys = ("g0", "u0", "a_out")
    dx, dbcv, st_a, land = _mixa_bwd(
        dh1, xs, bq, cq, vq, ccq, a_norm, cw_a, wg, tma,
        _rs_comm(_pair_reduce([gw_g0, gw_u0, gw_out], keys), keys, land))
    gw_in = by_dest(_grad_w(dbcv, n0, D, "grad_in", tw), "a_in")

    st_small = st_b.at[SM_A_CONV:SM_A_CONV + KA].set(st_a[1:1 + KA])
    sm_dest = st_small.reshape(SM_F32_ROWS, NDEV, 128).transpose(1, 0, 2)
    sm_dest = sm_dest.at[:, SM_B_PW1:SM_B_PW1 + 2, :].set(st_b1[0].reshape(NDEV, 2, 128))
    repl = jnp.concatenate([st_a[0:1], st_f0[0:1], st_f1[0:1], st_fin[0:1], st_fin[1:2],
                            jnp.zeros((3, D), F32)], axis=0)[None]
    p_in, sm_land, rp_land = _pair_reduce([gw_in], ("a_in",), side=(sm_dest, repl))

    big = [("ffn_w_gate", ("g0", "g1"), False,
            rows_first(ffn_w_gate), rows_first(m_ffn_w_gate), rows_first(v_ffn_w_gate)),
           ("ffn_w_up", ("u0", "u1"), False, rows_first(ffn_w_up), rows_first(m_ffn_w_up), rows_first(v_ffn_w_up)),
           ("ffn_w_down", ("d0", "d1"), False, ffn_w_down, m_ffn_w_down, v_ffn_w_down),
           ("a_w_out", ("a_out",), False, a_w_out, m_a_w_out, v_a_w_out),
           ("b_w_pw1", ("b_pw1",), True, b_w_pw1, m_b_w_pw1, v_b_w_pw1),
           ("b_w_pw2", ("b_pw2",), False, b_w_pw2, m_b_w_pw2, v_b_w_pw2),
           ("a_w_in", ("a_in",), True, a_w_in, m_a_w_in, v_a_w_in)]
    res = {}
    for k, (name, keys, transposed, w, m, v) in enumerate(big):
        if k < 3:
            comm = _rs_comm(p_in, ("a_in",), land, part=(k, 3), body_reads_land=True)
            *res[name], land = _finish_weight(land, keys, transposed, w, m, v, name, comm)
        else:
            res[name] = _finish_weight(land, keys, transposed, w, m, v, name)
    for name in ("ffn_w_gate", "ffn_w_up"):
        res[name] = [rows_first(a) for a in res[name]]

    g_small, g_repl, loss8 = _sum_small(sm_land, rp_land)
    loss = loss8[0, 0]

    small_names = ["b_norm", "b_b_pw1", "b_conv", "b_b_conv", "b_ln_g", "b_ln_b", "b_b_pw2", "a_conv",
                   "a_norm", "ffn_norm", "final_norm"]
    triples = [(b_norm, m_b_norm, v_b_norm), (b_b_pw1, m_b_b_pw1, v_b_b_pw1), (b_conv, m_b_conv, v_b_conv),
               (b_b_conv, m_b_b_conv, v_b_b_conv), (b_ln_g, m_b_ln_g, v_b_ln_g), (b_ln_b, m_b_ln_b, v_b_ln_b),
               (b_b_pw2, m_b_b_pw2, v_b_b_pw2), (a_conv, m_a_conv, v_a_conv), (a_norm, m_a_norm, v_a_norm),
               (ffn_norm, m_ffn_norm, v_ffn_norm),
               (fin, m_final_norm.reshape(1, D), v_final_norm.reshape(1, D))]
    for name, quad in zip(small_names, _adamw_small(g_small, g_repl, triples)):
        res[name] = quad
    res["final_norm"] = tuple(a.reshape(D) for a in res["final_norm"])

    order = ["a_norm", "a_w_in", "a_conv", "a_w_out", "b_norm", "b_w_pw1", "b_b_pw1", "b_conv", "b_b_conv",
             "b_ln_g", "b_ln_b", "b_w_pw2", "b_b_pw2", "ffn_norm", "ffn_w_gate", "ffn_w_up", "ffn_w_down",
             "final_norm"]
    out = [loss, dx.reshape(1, T, D)]
    for j in range(4):
        out += [res[k][j] for k in order]
    return tuple(out)
```

```python
import functools

import jax
import jax.numpy as jnp
from jax import lax
from jax.experimental import pallas as pl
from jax.experimental.pallas import tpu as pltpu

F32 = jnp.float32
BF16 = jnp.bfloat16
MESH = pl.DeviceIdType.MESH

D = 1024
FF = 2816
FH = FF // 2
NDEV = 8
KA = 3
KB = 31
HALO_A = 8
HALO_B = 32
CONV_ROWS = 64
RMS_EPS = 1e-6
LN_EPS = 1e-5
LR, B1, B2, ADAM_EPS, WD, STEP = 0.001, 0.9, 0.999, 1e-08, 0.01, 10

TM = 256
TM_A = 512
TM_DW = 1024
VMEM_LIMIT = 62 * 1024 * 1024

N_ROWS = {"a_in": 384, "a_out": 128, "b_pw1": 256, "b_pw2": 128, "g0": 352, "u0": 352, "d0": 352,
          "g1": 352, "u1": 352, "d1": 352}
WEIGHT_KEYS = ("a_in", "a_out", "b_pw1", "b_pw2", "g0", "u0", "d0", "g1", "u1", "d1")
NCHIP = 4


def _offsets(order):
    off, o = {}, 0
    for k in order:
        off[k] = o
        o += N_ROWS[k]
    return off, o


PK_ORDER = ("a_in", "a_out", "g0", "u0", "d0", "b_pw1", "b_pw2", "g1", "u1", "d1")
PK_OFF, PK_ROWS = _offsets(PK_ORDER)
LD_ORDER = ("g1", "u1", "d1", "b_pw1", "b_pw2", "d0", "g0", "u0", "a_out", "a_in")
LD_OFF, N_WROWS = _offsets(LD_ORDER)


def _span(off, keys):
    return off[keys[0]], sum(N_ROWS[k] for k in keys)


SM_B_NORM, SM_B_PW1, SM_B_CONV, SM_B_BCONV, SM_LN_G, SM_LN_B, SM_B_PW2, SM_A_CONV = 0, 1, 3, 34, 35, 36, 37, 38
SM_USED = 41
SM_F32_ROWS = 64


def _pcall(body, **kw):
    return pl.pallas_call(body, **kw)


def _cparams(sem=None):
    return pltpu.CompilerParams(dimension_semantics=sem, vmem_limit_bytes=VMEM_LIMIT)


def _dot(a, b):
    return jnp.dot(a, b, preferred_element_type=F32)


def _dot_nt(a, b):
    return lax.dot_general(a, b, (((1,), (1,)), ((), ())), preferred_element_type=F32)


def _dot_tn(a, b):
    return lax.dot_general(a, b, (((0,), (0,)), ((), ())), preferred_element_type=F32)


def _sigmoid(v):
    return 1.0 / (1.0 + jnp.exp(-v))


def _rms_stat(x):
    return lax.rsqrt(jnp.mean(x * x, axis=-1, keepdims=True) + RMS_EPS)


def _rms_bwd(dn, x, r, gamma):
    dng = dn * gamma
    dx = r * dng - x * (r * r * r) * jnp.mean(dng * x, axis=-1, keepdims=True)
    return dx, jnp.sum(dn * x * r, axis=0, keepdims=True)


def _weight_loads(wg_ref, plan, sems):
    copies = []
    for j, (dst, key) in enumerate(plan):
        off, n = PK_OFF[key], N_ROWS[key]
        copies += [pltpu.make_async_copy(
            wg_ref.at[d, pl.ds(off, n), :], dst.at[pl.ds(d * n, n), :], sems.at[j * NDEV + d])
            for d in range(NDEV)]

    def load():
        for cp in copies:
            cp.start()
        for cp in copies:
            cp.wait()

    return load


_ANY = pl.BlockSpec(memory_space=pl.ANY)


def _row_spec(tm, width, rev_nt=None):
    if rev_nt is None:
        return pl.BlockSpec((tm, width), lambda i: (i, 0))
    return pl.BlockSpec((tm, width), lambda i: (rev_nt - 1 - i, 0))


def _full_spec(shape):
    return pl.BlockSpec(shape, lambda *_: (0,) * len(shape))


def _sds(shape, dtype):
    return jax.ShapeDtypeStruct(shape, dtype)


def _mesh_pos():
    return lax.axis_index("x"), lax.axis_index("y"), lax.axis_index("c")


def _lin(p):
    return 4 * p[0] + 2 * p[1] + p[2]


def _ag_exchange(src, slot, send_sems, recv_sems, local_sem):
    x, y, c = _mesh_pos()
    me, sibling = (x, y, c), (x, y, 1 - c)
    chips = [(1 - x, y), (x, 1 - y), (1 - x, 1 - y)]

    def copy(k, block, to, own=False):
        return pltpu.make_async_remote_copy(
            src_ref=src if own else slot(block), dst_ref=slot(block),
            send_sem=send_sems.at[k], recv_sem=recv_sems.at[k], device_id=to, device_id_type=MESH)

    mine = pltpu.make_async_copy(src, slot(me), local_sem)
    first = [copy(0, me, sibling, own=True)]
    first += [copy(1 + j, me, (*chip, c), own=True) for j, chip in enumerate(chips)]
    passed = [copy(4 + j, (*chip, c), sibling) for j, chip in enumerate(chips)]

    def start():
        mine.start()
        for cp in first:
            cp.start()

    def forward():
        for j, chip in enumerate(chips):
            copy(1 + j, (*chip, c), me).wait_recv()
            passed[j].start()

    def finish():
        copy(0, sibling, me).wait_recv()
        for j, chip in enumerate(chips):
            copy(4 + j, (*chip, 1 - c), me).wait_recv()
        for cp in first + passed:
            cp.wait_send()
        mine.wait()

    return start, forward, finish


def _chip_exchange(p_ref, land_ref, send_sems, recv_sems, keys, part):
    x, y, c = _mesh_pos()
    l0, rows = _span(LD_OFF, keys)
    rows = rows // part[1]
    p0 = part[0] * rows
    dst = land_ref.at[2 * x + y, pl.ds(l0 + p0, rows), :]
    peers = [((x + (j >> 1)) % 2, (y + (j & 1)) % 2) for j in range(NCHIP)]

    def copy(j):
        tx, ty = peers[j]
        src = p_ref.at[2 * tx + ty, pl.ds(p0, rows), :]
        if j == 0:
            return pltpu.make_async_copy(src, dst, send_sems.at[0])
        return pltpu.make_async_remote_copy(
            src_ref=src, dst_ref=dst, send_sem=send_sems.at[j], recv_sem=recv_sems.at[j],
            device_id=(tx, ty, c), device_id_type=MESH)

    def start():
        for j in range(NCHIP):
            copy(j).start()

    def finish():
        for j in range(NCHIP):
            copy(j).wait()

    return start, finish


def _all_to_all_f32(src_for, dst_ref, send_sems, recv_sems):
    x, y, c = _mesh_pos()
    dst = dst_ref.at[_lin((x, y, c))]
    peers = [((x + ((j >> 2) & 1)) % 2, (y + ((j >> 1) & 1)) % 2, (c + (j & 1)) % 2) for j in range(NDEV)]

    def copy(j):
        if j == 0:
            return pltpu.make_async_copy(src_for(_lin(peers[0])), dst, send_sems.at[0])
        return pltpu.make_async_remote_copy(
            src_ref=src_for(_lin(peers[j])), dst_ref=dst, send_sem=send_sems.at[j], recv_sem=recv_sems.at[j],
            device_id=peers[j], device_id_type=MESH)

    def start():
        for j in range(NDEV):
            copy(j).start()

    def finish():
        for j in range(NDEV):
            copy(j).wait()

    return start, finish


class _Comm:
    def __init__(self, ins, alias_in, out_shape, scratch, make, gives_wg, middle_step=0):
        self.ins, self.alias_in, self.out_shape = ins, alias_in, out_shape
        self.scratch, self.make, self.gives_wg, self.middle_step = scratch, make, gives_wg, middle_step


def _ag_comm(shard, wg, keys, forward_step=0):
    def make(c_ins, c_out, sc):
        return _weights_exchange(c_ins[0], c_out, sc[0], sc[1], sc[2], keys)

    return _Comm([shard, wg], 1, _sds(wg.shape, BF16),
                 [pltpu.SemaphoreType.DMA((7,)), pltpu.SemaphoreType.DMA((7,)), pltpu.SemaphoreType.DMA],
                 make, True, forward_step)


def _weights_exchange(shard_ref, wg_ref, send_sems, recv_sems, local_sem, keys):
    r0, nr = _span(PK_OFF, keys)
    return _ag_exchange(shard_ref.at[pl.ds(r0, nr), :], lambda p: wg_ref.at[_lin(p), pl.ds(r0, nr), :],
                        send_sems, recv_sems, local_sem)


def _rs_comm(psum, keys, land, part=(0, 1), body_reads_land=False):
    def make(c_ins, c_out, sc):
        start, finish = _chip_exchange(c_ins[0], c_out, sc[0], sc[1], keys, part)
        return start, None, finish

    ins = [psum] + ([] if land is None else [land])
    return _Comm(ins, None if land is None else 1, _sds((NCHIP, N_WROWS, D), BF16),
                 [pltpu.SemaphoreType.DMA((NCHIP,)), pltpu.SemaphoreType.DMA((NCHIP,))], make, body_reads_land)


def _hosted_call(body, name, nt, in_specs, out_specs, out_shape, scratch, args, comm):
    if comm is None:
        return _pcall(body, name=name, grid=(nt,), in_specs=in_specs, out_specs=tuple(out_specs),
                      out_shape=tuple(out_shape), scratch_shapes=scratch,
                      compiler_params=_cparams(("arbitrary",)))(*args)
    n_in, n_out, n_sc, n_cin = len(in_specs), len(out_specs), len(scratch), len(comm.ins)

    def wrapped(*refs):
        ins = refs[:n_in]
        c_ins = refs[n_in:n_in + n_cin]
        outs = refs[n_in + n_cin:n_in + n_cin + n_out]
        c_out = refs[n_in + n_cin + n_out]
        sc = refs[n_in + n_cin + n_out + 1:n_in + n_cin + n_out + 1 + n_sc]
        c_sc = refs[n_in + n_cin + n_out + 1 + n_sc:]
        start, middle, finish = comm.make(c_ins, c_out, c_sc)
        pl.when(pl.program_id(0) == 0)(start)
        if comm.gives_wg:
            body(*ins, c_out, *outs, *sc)
        else:
            body(*ins, *outs, *sc)
        if middle is not None:
            pl.when(pl.program_id(0) == max(nt - 1 - comm.middle_step, 0))(middle)
        pl.when(pl.program_id(0) == nt - 1)(finish)

    aliases = {} if comm.alias_in is None else {n_in + comm.alias_in: n_out}
    res = _pcall(wrapped, name=name, grid=(nt,),
                 in_specs=list(in_specs) + [_ANY] * n_cin, out_specs=tuple(out_specs) + (_ANY,),
                 out_shape=tuple(out_shape) + (comm.out_shape,),
                 scratch_shapes=list(scratch) + list(comm.scratch),
                 input_output_aliases=aliases,
                 compiler_params=_cparams(("arbitrary",)))(*args, *comm.ins)
    return res


def _pack_shard(weights, smalls):
    plan = (("a_in", 0, 0, True), ("a_out", 1, 0, False), ("b_pw1", 2, 0, True), ("b_pw2", 3, 0, False),
            ("g0", 4, 0, False), ("u0", 5, 0, False), ("d0", 6, 0, False),
            ("g1", 4, 1, False), ("u1", 5, 1, False), ("d1", 6, 1, False))
    n_pad = 384

    def body(*refs):
        out, sm, pad = refs[-3:]
        for key, idx, layer, transposed in plan:
            n = N_ROWS[key]
            val = refs[idx][layer]
            if transposed:
                if n % 128:
                    pad[:, pl.ds(0, n)] = val
                    pad[:, pl.ds(n, n_pad - n)] = jnp.zeros((D, n_pad - n), F32)
                    val = pad[...]
                val = val.T[:n]
            out[pl.ds(PK_OFF[key], n), :] = val.astype(BF16)
        b_norm, b_b_pw1, b_conv, b_b_conv, b_ln_g, b_ln_b, b_b_pw2, a_conv = refs[len(weights):-3]
        sm[...] = jnp.zeros((SM_F32_ROWS, 128), F32)
        sm[pl.ds(SM_B_PW1, 1), :] = b_b_pw1[:, pl.ds(0, 128)]
        sm[pl.ds(SM_B_PW1 + 1, 1), :] = b_b_pw1[:, pl.ds(128, 128)]
        sm[pl.ds(SM_B_CONV, KB), :] = b_conv[0]
        sm[pl.ds(SM_A_CONV, KA), :] = a_conv[0]
        for row, ref in ((SM_B_NORM, b_norm), (SM_B_BCONV, b_b_conv), (SM_LN_G, b_ln_g), (SM_LN_B, b_ln_b),
                         (SM_B_PW2, b_b_pw2)):
            sm[pl.ds(row, 1), :] = ref[...]

    vm = pl.BlockSpec(memory_space=pltpu.VMEM)
    return _pcall(
        body, name="pack_shard",
        out_shape=(_sds((PK_ROWS, D), BF16), _sds((SM_F32_ROWS, 128), F32)),
        in_specs=[vm] * (len(weights) + len(smalls)), out_specs=(vm, vm),
        scratch_shapes=[pltpu.VMEM((D, n_pad), F32)],
        compiler_params=_cparams(),
    )(*weights, *smalls)


def _all_gather_first(shard, small):
    def body(x_ref, s_ref, wg_ref, sg_ref, send_w, recv_w, local_w, send_s, recv_s, local_s):
        start_w, forward_w, finish_w = _weights_exchange(x_ref, wg_ref, send_w, recv_w, local_w, ("a_in", "a_out"))
        start_s, forward_s, finish_s = _ag_exchange(s_ref, lambda p: sg_ref.at[_lin(p)], send_s, recv_s, local_s)
        start_s()
        start_w()
        forward_s()
        forward_w()
        finish_s()
        finish_w()

    sems = [pltpu.SemaphoreType.DMA((7,)), pltpu.SemaphoreType.DMA((7,)), pltpu.SemaphoreType.DMA]
    return _pcall(
        body, name="all_gather_first",
        out_shape=(_sds((NDEV, PK_ROWS, D), BF16), _sds((NDEV, SM_F32_ROWS, 128), F32)),
        in_specs=[_ANY, _ANY], out_specs=(_ANY, _ANY), scratch_shapes=sems + sems,
    )(shard, small)


def _pair_reduce(grads, keys, side=None):
    l0, rows = _span(LD_OFF, keys)
    nk = len(keys)
    n_side = 0 if side is None else 2

    def body(c_ref, *refs):
        mine = refs[:nk]
        whole = refs[nk:2 * nk]
        o_ref = refs[2 * nk + n_side]
        got, send_sem, recv_sems = refs[2 * nk + 2 * n_side + 1:2 * nk + 2 * n_side + 4]
        t = pl.program_id(0)
        x, y, c = _mesh_pos()
        sibling = (x, y, 1 - c)
        if side is not None:
            sm_ref, rp_ref = refs[2 * nk:2 * nk + 2]
            sm_land, rp_land = refs[2 * nk + 3:2 * nk + 5]
            s2, r2, s3, r3 = refs[-4:]
            start_s, finish_s = _all_to_all_f32(lambda d: sm_ref.at[d], sm_land, s2, r2)
            start_r, finish_r = _all_to_all_f32(lambda d: rp_ref.at[0], rp_land, s3, r3)

        @pl.when(t == 0)
        def _():
            if side is not None:
                start_s()
                start_r()
            for ref, key in zip(whole, keys):
                for d in range(NCHIP):
                    pltpu.make_async_remote_copy(
                        src_ref=ref.at[2 * d + (1 - c)], dst_ref=got.at[d, pl.ds(LD_OFF[key] - l0, N_ROWS[key]), :],
                        send_sem=send_sem, recv_sem=recv_sems.at[d], device_id=sibling, device_id_type=MESH).start()

        pltpu.make_async_remote_copy(src_ref=got.at[t], dst_ref=got.at[t], send_sem=send_sem,
                                     recv_sem=recv_sems.at[t], device_id=sibling, device_id_type=MESH).wait_recv()
        for ref, key in zip(mine, keys):
            sl = pl.ds(LD_OFF[key] - l0, N_ROWS[key])
            o_ref[sl, :] = (ref[...].astype(F32) + got[t, sl, :].astype(F32)).astype(BF16)

        @pl.when(t == NCHIP - 1)
        def _():
            pltpu.make_async_remote_copy(src_ref=got, dst_ref=got, send_sem=send_sem, recv_sem=recv_sems.at[0],
                                         device_id=sibling, device_id_type=MESH).wait_send()
            if side is not None:
                finish_s()
                finish_r()

    out_specs = [pl.BlockSpec((None, rows, D), lambda t, c: (t, 0, 0))]
    out_shape = [_sds((NCHIP, rows, D), BF16)]
    scratch = [pltpu.VMEM((NCHIP, rows, D), BF16), pltpu.SemaphoreType.DMA, pltpu.SemaphoreType.DMA((NCHIP,))]
    if side is not None:
        out_specs += [_ANY, _ANY]
        out_shape += [_sds((NDEV, SM_F32_ROWS, 128), F32), _sds((NDEV, 8, D), F32)]
        scratch += [pltpu.SemaphoreType.DMA((NDEV,))] * 4
    grid_spec = pltpu.PrefetchScalarGridSpec(
        num_scalar_prefetch=1, grid=(NCHIP,),
        in_specs=[pl.BlockSpec((None, N_ROWS[k], D), lambda t, c: (2 * t + c[0], 0, 0)) for k in keys]
        + [_ANY] * (nk + n_side),
        out_specs=tuple(out_specs), scratch_shapes=scratch)
    core = lax.axis_index("c").astype(jnp.int32).reshape(1)
    res = _pcall(
        body, name="pair_reduce_" + keys[0], grid_spec=grid_spec, out_shape=tuple(out_shape),
        compiler_params=_cparams(("arbitrary",)),
    )(core, *grads, *grads, *(side or ()))
    return res[0] if side is None else res


def _mixa_fwd(x, gam, cw, wg, tm, comm):
    T = x.shape[0]

    def body(x_ref, gam_ref, cw_ref, wg_ref, h_ref, n_ref, b_ref, c_ref, v_ref, cc_ref, y_ref,
             win, wout, buf, sems):
        first = pl.program_id(0) == 0
        load_w = _weight_loads(wg_ref, [(win, "a_in"), (wout, "a_out")], sems)

        @pl.when(first)
        def _():
            load_w()
            buf[pl.ds(0, HALO_A), :] = jnp.zeros((HALO_A, D), F32)

        xv = x_ref[...]
        nb = (xv * _rms_stat(xv) * gam_ref[...]).astype(BF16)
        n_ref[...] = nb
        bv = _dot_nt(nb, win[pl.ds(0, D), :])
        cval = _dot_nt(nb, win[pl.ds(D, D), :])
        vval = _dot_nt(nb, win[pl.ds(2 * D, D), :])
        cv = cval * vval
        buf[pl.ds(HALO_A, tm), :] = cv
        cc = cw_ref[pl.ds(KA - 1, 1), :] * cv
        for k in range(KA - 1):
            cc = cc + cw_ref[pl.ds(k, 1), :] * buf[pl.ds(HALO_A - (KA - 1) + k, tm), :]
        buf[pl.ds(0, HALO_A), :] = buf[pl.ds(tm, HALO_A), :]
        yb = (bv * cc).astype(BF16)
        b_ref[...] = bv.astype(BF16)
        c_ref[...] = cval.astype(BF16)
        v_ref[...] = vval.astype(BF16)
        cc_ref[...] = cc.astype(BF16)
        y_ref[...] = yb
        h_ref[...] = xv + _dot(yb, wout[...])

    act = _sds((T, D), BF16)
    return _hosted_call(
        body, "mixa_fwd", T // tm,
        in_specs=[_row_spec(tm, D), _full_spec((1, D)), _full_spec((KA, D))],
        out_specs=[_row_spec(tm, D) for _ in range(7)],
        out_shape=(_sds((T, D), F32),) + (act,) * 6,
        scratch=[pltpu.VMEM((3 * D, D), BF16), pltpu.VMEM((D, D), BF16),
                 pltpu.VMEM((HALO_A + tm, D), F32), pltpu.SemaphoreType.DMA((2 * NDEV,))],
        args=(x, gam, cw), comm=comm)


def _ffn_fwd(h, gam, wg, layer, tm, comm, head=None):
    T = h.shape[0]
    kg, ku, kd = "g%d" % layer, "u%d" % layer, "d%d" % layer
    n_head = 0 if head is None else 2

    def body(*refs):
        h_ref, gam_ref = refs[:2]
        wg_ref = refs[2 + n_head]
        o_ref, n_ref, g_ref, u_ref, gu_ref = refs[3 + n_head:8 + n_head]
        wgt, wut, wd, sems = refs[-4:]

        first = pl.program_id(0) == 0
        load_w = _weight_loads(wg_ref, [(wgt, kg), (wut, ku), (wd, kd)], sems)
        pl.when(first)(load_w)

        hv = h_ref[...]
        nb = (hv * _rms_stat(hv) * gam_ref[...]).astype(BF16)
        n_ref[...] = nb
        out = hv
        for f in range(2):
            cols = pl.ds(f * FH, FH)
            g = _dot_nt(nb, wgt[cols, :])
            u = _dot_nt(nb, wut[cols, :])
            gu = (g * _sigmoid(g) * u).astype(BF16)
            g_ref[:, cols] = g.astype(BF16)
            u_ref[:, cols] = u.astype(BF16)
            gu_ref[:, cols] = gu
            out = out + _dot(gu, wd[cols, :])
        if head is None:
            o_ref[...] = out
        else:
            t_ref, fin_ref, st_ref = refs[2], refs[3], refs[8 + n_head]

            @pl.when(pl.program_id(0) == 0)
            def _():
                st_ref[...] = jnp.zeros((8, D), F32)

            gamma = fin_ref[...]
            r = _rms_stat(out)
            err = out * r * gamma - t_ref[...]
            dx, dgam = _rms_bwd(err * (1.0 / D), out, r, gamma)
            o_ref[...] = dx
            st_ref[pl.ds(0, 1), :] += dgam
            st_ref[pl.ds(1, 1), :] += (0.5 / D) * jnp.sum(err * err, axis=0, keepdims=True)

    pre = _sds((T, FF), BF16)
    in_specs = [_row_spec(tm, D), _full_spec((1, D))]
    args = (h, gam)
    out_specs = [_row_spec(tm, D), _row_spec(tm, D), _row_spec(tm, FF), _row_spec(tm, FF), _row_spec(tm, FF)]
    out_shape = (_sds((T, D), F32), _sds((T, D), BF16), pre, pre, pre)
    if head is not None:
        in_specs, args = in_specs + [_row_spec(tm, D), _full_spec((1, D))], args + tuple(head)
        out_specs, out_shape = out_specs + [_full_spec((8, D))], out_shape + (_sds((8, D), F32),)
    if comm is None:
        in_specs, args = in_specs + [_ANY], args + (wg,)
    return _hosted_call(
        body, "ffn%d_fwd" % layer, T // tm,
        in_specs=in_specs, out_specs=out_specs, out_shape=out_shape,
        scratch=[pltpu.VMEM((FF, D), BF16)] * 3 + [pltpu.SemaphoreType.DMA((3 * NDEV,))],
        args=args, comm=comm)


def _shifted_copies(buf, shf, tm):
    for r in range(1, 8):
        shf[r - 1] = buf[pl.ds(r, tm + HALO_B - 8), :]


def _broadcast_taps(vec_ref, wb):
    for k in range(KB):
        wb[k] = jnp.broadcast_to(vec_ref[pl.ds(SM_B_CONV + k, 1), :], (8, D))
    wb[KB] = jnp.broadcast_to(vec_ref[pl.ds(SM_B_BCONV, 1), :], (8, D))


def _taps_by_shift_residue(taps):
    groups = {}
    for k, shift in taps:
        q, r = divmod(shift, 8)
        groups.setdefault(r, []).append((k, q))
    return sorted(groups.items())


def _window(buf, shf, base, r, q0, n_groups, lanes):
    rows = pl.ds(base + 8 * q0, 8 * n_groups)
    v = buf[rows, lanes] if r == 0 else shf[r - 1, rows, lanes]
    return [v[8 * i:8 * i + 8] for i in range(n_groups)]


def _long_conv(buf, shf, wb, out_ref, tm, taps, bias_row):
    n_acc = CONV_ROWS // 8
    groups = _taps_by_shift_residue(taps)
    for col in range(D // 128):
        lanes = pl.ds(128 * col, 128)

        def rows(i, carry, lanes=lanes):
            base = i * CONV_ROWS
            init = jnp.zeros((8, 128), F32) if bias_row is None else wb[bias_row, :, lanes]
            accs = [init] * n_acc
            for r, lst in groups:
                q0, q1 = min(q for _, q in lst), max(q for _, q in lst)
                win = _window(buf, shf, base, r, q0, n_acc + q1 - q0, lanes)
                for k, q in lst:
                    wk = wb[k, :, lanes]
                    accs = [acc + wk * win[h + q - q0] for h, acc in enumerate(accs)]
            for h, acc in enumerate(accs):
                out_ref[pl.ds(base + 8 * h, 8), lanes] = acc
            return carry

        for i in range(tm // CONV_ROWS):
            rows(i, 0)


def _long_conv_grad_taps(buf, shf, x_ref, wacc, tm):
    n_acc = CONV_ROWS // 8
    groups = _taps_by_shift_residue([(KB - 1 - j, j) for j in range(KB)])
    for col in range(D // 128):
        lanes = pl.ds(128 * col, 128)

        def rows(i, carry, lanes=lanes):
            base = i * CONV_ROWS
            xv = x_ref[pl.ds(base, CONV_ROWS), lanes]
            xs = [xv[8 * h:8 * h + 8] for h in range(n_acc)]
            for r, lst in groups:
                q0, q1 = min(q for _, q in lst), max(q for _, q in lst)
                win = _window(buf, shf, base, r, q0, n_acc + q1 - q0, lanes)
                for k, q in lst:
                    prod = [x * win[h + q - q0] for h, x in enumerate(xs)]
                    wacc[k, :, lanes] += functools.reduce(lambda a, b: a + b, prod)
            return carry

        for i in range(tm // CONV_ROWS):
            rows(i, 0)


def _ln_stats(dc):
    mu = jnp.mean(dc, axis=-1, keepdims=True)
    xc = dc - mu
    rstd = lax.rsqrt(jnp.mean(xc * xc, axis=-1, keepdims=True) + LN_EPS)
    return xc * rstd, rstd


def _mixb_fwd(h, vecs, bias1, wg, tm, comm):
    T = h.shape[0]

    def body(h_ref, vec_ref, b1_ref, wg_ref, o_ref, n_ref, a_ref, g_ref, dc_ref, s_ref, w1, w2, buf, shf, wb, sems):
        first = pl.program_id(0) == 0
        load_w = _weight_loads(wg_ref, [(w1, "b_pw1"), (w2, "b_pw2")], sems)

        @pl.when(first)
        def _():
            load_w()
            buf[pl.ds(0, HALO_B), :] = jnp.zeros((HALO_B, D), F32)
            _broadcast_taps(vec_ref, wb)

        hv = h_ref[...]
        nb = (hv * _rms_stat(hv) * vec_ref[pl.ds(SM_B_NORM, 1), :]).astype(BF16)
        n_ref[...] = nb
        a = _dot_nt(nb, w1[pl.ds(0, D), :]) + b1_ref[:, pl.ds(0, D)]
        g = _dot_nt(nb, w1[pl.ds(D, D), :]) + b1_ref[:, pl.ds(D, D)]
        a_ref[...] = a.astype(BF16)
        g_ref[...] = g.astype(BF16)
        buf[pl.ds(HALO_B, tm), :] = a * _sigmoid(g)
        _shifted_copies(buf, shf, tm)

        _long_conv(buf, shf, wb, dc_ref, tm, [(k, HALO_B - (KB - 1) + k) for k in range(KB)], KB)
        buf[pl.ds(0, HALO_B), :] = buf[pl.ds(tm, HALO_B), :]
        xhat, _ = _ln_stats(dc_ref[...])
        ln = xhat * vec_ref[pl.ds(SM_LN_G, 1), :] + vec_ref[pl.ds(SM_LN_B, 1), :]
        s = (ln * _sigmoid(ln)).astype(BF16)
        s_ref[...] = s
        o_ref[...] = hv + _dot(s, w2[...]) + vec_ref[pl.ds(SM_B_PW2, 1), :]

    act = _sds((T, D), BF16)
    return _hosted_call(
        body, "mixb_fwd", T // tm,
        in_specs=[_row_spec(tm, D), _full_spec((SM_F32_ROWS, D)), _full_spec((1, 2 * D))],
        out_specs=[_row_spec(tm, D) for _ in range(6)],
        out_shape=(_sds((T, D), F32), act, act, act, _sds((T, D), F32), act),
        scratch=[pltpu.VMEM((2 * D, D), BF16), pltpu.VMEM((D, D), BF16),
                 pltpu.VMEM((HALO_B + tm, D), F32), pltpu.VMEM((7, HALO_B + tm - 8, D), F32),
                 pltpu.VMEM((KB + 1, 8, D), F32), pltpu.SemaphoreType.DMA((2 * NDEV,))],
        args=(h, vecs, bias1), comm=comm)


def _ffn_bwd_dx(dh, h, g, u, gam, wg, layer, tm, comm):
    T = h.shape[0]
    kg, ku, kd = "g%d" % layer, "u%d" % layer, "d%d" % layer

    def body(dh_ref, h_ref, g_ref, u_ref, gam_ref, wg_ref, o_ref, dg_ref, du_ref, st_ref, wgt, wut, wd, sems):
        first = pl.program_id(0) == 0
        load_w = _weight_loads(wg_ref, [(wd, kd), (wgt, kg), (wut, ku)], sems)

        @pl.when(first)
        def _():
            load_w()
            st_ref[...] = jnp.zeros((8, D), F32)

        dhv = dh_ref[...]
        dhb = dhv.astype(BF16)
        dn = jnp.zeros_like(dhv)
        for f in range(2):
            cols = pl.ds(f * FH, FH)
            dgu = _dot_nt(dhb, wd[cols, :])
            gv = g_ref[:, cols].astype(F32)
            sg = _sigmoid(gv)
            du = (dgu * gv * sg).astype(BF16)
            dg = (dgu * u_ref[:, cols].astype(F32) * (sg * (1.0 + gv * (1.0 - sg)))).astype(BF16)
            dg_ref[:, cols] = dg
            du_ref[:, cols] = du
            dn = dn + _dot(dg, wgt[cols, :]) + _dot(du, wut[cols, :])
        hv = h_ref[...]
        dx, dgam = _rms_bwd(dn, hv, _rms_stat(hv), gam_ref[...])
        o_ref[...] = dhv + dx
        st_ref[pl.ds(0, 1), :] += dgam

    pre = _sds((T, FF), BF16)
    return _hosted_call(
        body, "ffn%d_bwd_dx" % layer, T // tm,
        in_specs=[_row_spec(tm, D), _row_spec(tm, D), _row_spec(tm, FF), _row_spec(tm, FF),
                  _full_spec((1, D)), _ANY],
        out_specs=[_row_spec(tm, D), _row_spec(tm, FF), _row_spec(tm, FF), _full_spec((8, D))],
        out_shape=(_sds((T, D), F32), pre, pre, _sds((8, D), F32)),
        scratch=[pltpu.VMEM((FF, D), BF16)] * 3 + [pltpu.SemaphoreType.DMA((3 * NDEV,))],
        args=(dh, h, g, u, gam, wg), comm=comm)


def _grad_w(lhs, rhs, mc, name, tm):
    T, M = lhs.shape
    nt = T // tm

    def body(l_ref, r_ref, o_ref, acc):
        i = pl.program_id(1)

        @pl.when(i == 0)
        def _():
            acc[...] = jnp.zeros((mc, D), F32)

        acc[...] += _dot_tn(l_ref[...], r_ref[...].astype(BF16))

        @pl.when(i == nt - 1)
        def _():
            o_ref[...] = acc[...].astype(BF16)

    return _pcall(
        body, name=name, grid=(M // mc, nt),
        out_shape=_sds((M, D), BF16),
        in_specs=[pl.BlockSpec((tm, mc), lambda j, i: (i, j)), pl.BlockSpec((tm, D), lambda j, i: (i, 0))],
        out_specs=pl.BlockSpec((mc, D), lambda j, i: (j, 0)),
        scratch_shapes=[pltpu.VMEM((mc, D), F32)],
        compiler_params=_cparams(("arbitrary", "arbitrary")),
    )(lhs, rhs)


def _mixb_bwd(dh, h, a, g, dc, vecs, wg, tm, comm):
    T = h.shape[0]
    nt = T // tm

    def body(dh_ref, h_ref, a_ref, g_ref, dc_ref, vec_ref, wg_ref, o_ref, du_ref, st_ref, sb_ref,
             w1, w2, buf, shf, glu_s, dglu_s, wacc, wb, sems):
        first = pl.program_id(0) == 0
        load_w = _weight_loads(wg_ref, [(w2, "b_pw2"), (w1, "b_pw1")], sems)

        @pl.when(first)
        def _():
            load_w()
            buf[pl.ds(tm, HALO_B), :] = jnp.zeros((HALO_B, D), F32)
            wacc[...] = jnp.zeros((KB, 8, D), F32)
            _broadcast_taps(vec_ref, wb)
            st_ref[...] = jnp.zeros((SM_F32_ROWS, D), F32)
            sb_ref[...] = jnp.zeros((8, 2 * D), F32)

        def acc(row, val):
            st_ref[pl.ds(row, 1), :] += jnp.sum(val, axis=0, keepdims=True)

        dhv = dh_ref[...]
        acc(SM_B_PW2, dhv)
        ds = _dot_nt(dhv.astype(BF16), w2[...])
        xhat, rstd = _ln_stats(dc_ref[...])
        ln_g = vec_ref[pl.ds(SM_LN_G, 1), :]
        ln = xhat * ln_g + vec_ref[pl.ds(SM_LN_B, 1), :]
        sl = _sigmoid(ln)
        dln = ds * (sl * (1.0 + ln * (1.0 - sl)))
        acc(SM_LN_G, dln * xhat)
        acc(SM_LN_B, dln)
        dxh = dln * ln_g
        ddc = rstd * (dxh - jnp.mean(dxh, axis=-1, keepdims=True)
                      - xhat * jnp.mean(dxh * xhat, axis=-1, keepdims=True))
        acc(SM_B_BCONV, ddc)
        buf[pl.ds(0, tm), :] = ddc
        _shifted_copies(buf, shf, tm)
        av = a_ref[...].astype(F32)
        sg = _sigmoid(g_ref[...].astype(F32))
        glu_s[...] = av * sg

        _long_conv(buf, shf, wb, dglu_s, tm, [(KB - 1 - j, j) for j in range(KB)], None)
        _long_conv_grad_taps(buf, shf, glu_s, wacc, tm)
        buf[pl.ds(tm, HALO_B), :] = buf[pl.ds(0, HALO_B), :]
        dglu = dglu_s[...]
        da = dglu * sg
        dg = dglu * av * sg * (1.0 - sg)
        sb_ref[pl.ds(0, 1), pl.ds(0, D)] += jnp.sum(da, axis=0, keepdims=True)
        sb_ref[pl.ds(0, 1), pl.ds(D, D)] += jnp.sum(dg, axis=0, keepdims=True)
        dab, dgb = da.astype(BF16), dg.astype(BF16)
        du_ref[:, pl.ds(0, D)] = dab
        du_ref[:, pl.ds(D, D)] = dgb
        dn = _dot(dab, w1[pl.ds(0, D), :]) + _dot(dgb, w1[pl.ds(D, D), :])
        hv = h_ref[...]
        dx, dgam = _rms_bwd(dn, hv, _rms_stat(hv), vec_ref[pl.ds(SM_B_NORM, 1), :])
        o_ref[...] = dhv + dx
        st_ref[pl.ds(SM_B_NORM, 1), :] += dgam

        @pl.when(pl.program_id(0) == nt - 1)
        def _():
            st_ref[pl.ds(SM_B_CONV, KB), :] = jnp.sum(wacc[...], axis=1)

    rs = functools.partial(_row_spec, rev_nt=nt)
    return _hosted_call(
        body, "mixb_bwd", nt,
        in_specs=[rs(tm, D), rs(tm, D), rs(tm, D), rs(tm, D), rs(tm, D), _full_spec((SM_F32_ROWS, D)), _ANY],
        out_specs=[rs(tm, D), rs(tm, 2 * D), _full_spec((SM_F32_ROWS, D)), _full_spec((8, 2 * D))],
        out_shape=(_sds((T, D), F32), _sds((T, 2 * D), BF16), _sds((SM_F32_ROWS, D), F32), _sds((8, 2 * D), F32)),
        scratch=[pltpu.VMEM((2 * D, D), BF16), pltpu.VMEM((D, D), BF16),
                 pltpu.VMEM((tm + HALO_B, D), F32), pltpu.VMEM((7, tm + HALO_B - 8, D), F32),
                 pltpu.VMEM((tm, D), F32), pltpu.VMEM((tm, D), F32), pltpu.VMEM((KB, 8, D), F32),
                 pltpu.VMEM((KB + 1, 8, D), F32), pltpu.SemaphoreType.DMA((2 * NDEV,))],
        args=(dh, h, a, g, dc, vecs, wg), comm=comm)


def _mixa_bwd(dh, x, b, c, v, cc, gam, cw, wg, tm, comm):
    T = x.shape[0]
    nt = T // tm

    def body(dh_ref, x_ref, b_ref, c_ref, v_ref, cc_ref, gam_ref, cw_ref, wg_ref,
             o_ref, db_ref, st_ref, win, wout, buf, sems):
        first = pl.program_id(0) == 0
        load_w = _weight_loads(wg_ref, [(wout, "a_out"), (win, "a_in")], sems)

        @pl.when(first)
        def _():
            load_w()
            buf[pl.ds(tm, HALO_A), :] = jnp.zeros((HALO_A, D), F32)
            st_ref[...] = jnp.zeros((8, D), F32)

        dhv = dh_ref[...]
        dy = _dot_nt(dhv.astype(BF16), wout[...])
        cval = c_ref[...].astype(F32)
        vval = v_ref[...].astype(F32)
        d_b = (dy * cc_ref[...].astype(F32)).astype(BF16)
        buf[pl.ds(0, tm), :] = dy * b_ref[...].astype(F32)
        cv = cval * vval
        dcv = jnp.zeros((tm, D), F32)
        for j in range(KA):
            sh = buf[pl.ds(j, tm), :]
            k = KA - 1 - j
            dcv = dcv + cw_ref[pl.ds(k, 1), :] * sh
            st_ref[pl.ds(1 + k, 1), :] += jnp.sum(cv * sh, axis=0, keepdims=True)
        buf[pl.ds(tm, HALO_A), :] = buf[pl.ds(0, HALO_A), :]
        d_c = (dcv * vval).astype(BF16)
        d_v = (dcv * cval).astype(BF16)
        db_ref[:, pl.ds(0, D)] = d_b
        db_ref[:, pl.ds(D, D)] = d_c
        db_ref[:, pl.ds(2 * D, D)] = d_v
        dn = _dot(d_b, win[pl.ds(0, D), :]) + _dot(d_c, win[pl.ds(D, D), :]) + _dot(d_v, win[pl.ds(2 * D, D), :])
        xv = x_ref[...]
        dx, dgam = _rms_bwd(dn, xv, _rms_stat(xv), gam_ref[...])
        o_ref[...] = dhv + dx
        st_ref[pl.ds(0, 1), :] += dgam

    rs = functools.partial(_row_spec, rev_nt=nt)
    return _hosted_call(
        body, "mixa_bwd", nt,
        in_specs=[rs(tm, D) for _ in range(6)] + [_full_spec((1, D)), _full_spec((KA, D)), _ANY],
        out_specs=[rs(tm, D), rs(tm, 3 * D), _full_spec((8, D))],
        out_shape=(_sds((T, D), F32), _sds((T, 3 * D), BF16), _sds((8, D), F32)),
        scratch=[pltpu.VMEM((3 * D, D), BF16), pltpu.VMEM((D, D), BF16),
                 pltpu.VMEM((tm + HALO_A, D), F32), pltpu.SemaphoreType.DMA((2 * NDEV,))],
        args=(dh, x, b, c, v, cc, gam, cw, wg), comm=comm)


def _sum_small(sm, rp):
    def body(sm_ref, rp_ref, osm, orp, oloss):
        a = sm_ref[0]
        b = rp_ref[0]
        for s in range(1, NDEV):
            a = a + sm_ref[s]
            b = b + rp_ref[s]
        osm[...] = a
        orp[...] = b
        oloss[...] = jnp.zeros((8, 128), F32) + jnp.sum(b[4:5, :], axis=-1, keepdims=True)

    return _pcall(
        body, name="sum_small_grads",
        out_shape=(_sds((SM_F32_ROWS, 128), F32), _sds((8, D), F32), _sds((8, 128), F32)),
        in_specs=[pl.BlockSpec(memory_space=pltpu.VMEM)] * 2,
        out_specs=tuple(pl.BlockSpec(memory_space=pltpu.VMEM) for _ in range(3)),
        compiler_params=_cparams(),
    )(sm, rp)


def _adam_math(w, g, m, v):
    mn = B1 * m + (1.0 - B1) * g
    vn = B2 * v + (1.0 - B2) * (g * g)
    m_hat = mn / (1.0 - B1 ** STEP)
    v_hat = vn / (1.0 - B2 ** STEP)
    return -LR * (m_hat / (jnp.sqrt(v_hat) + ADAM_EPS) + WD * w), mn, vn


def _finish_weight(land, keys, transposed, w, m, v, name, comm=None):
    layers, rows, cols = w.shape
    n = N_ROWS[keys[0]]
    n_pad = -(-n // 128) * 128 if transposed else n

    def body(w_ref, m_ref, v_ref, land_ref, og, od, om, ov, buf, sem):
        off = LD_OFF[keys[0]]
        if layers == 2:
            off = jnp.where(pl.program_id(0) == 0, off, LD_OFF[keys[1]])
        cp = pltpu.make_async_copy(land_ref.at[:, pl.ds(pl.multiple_of(off, 32), n), :], buf, sem)
        cp.start()
        cp.wait()
        g = buf[0].astype(F32)
        for s in range(1, NCHIP):
            g = g + buf[s].astype(F32)
        if transposed:
            if n_pad != n:
                g = jnp.concatenate([g, jnp.zeros((n_pad - n, D), F32)], axis=0)
            g = g.T[:, :n]
        d, mn, vn = _adam_math(w_ref[...], g, m_ref[...], v_ref[...])
        og[...] = g
        od[...] = d
        om[...] = mn
        ov[...] = vn

    spec = pl.BlockSpec((None, rows, cols), lambda l: (l, 0, 0))
    shp = _sds(w.shape, F32)
    in_specs, args = [spec, spec, spec], (w, m, v)
    if comm is None:
        in_specs, args = in_specs + [_ANY], args + (land,)
    return _hosted_call(
        body, "finish_" + name, layers,
        in_specs=in_specs, out_specs=[spec] * 4, out_shape=(shp,) * 4,
        scratch=[pltpu.VMEM((NCHIP, n, D), BF16), pltpu.SemaphoreType.DMA],
        args=args, comm=comm)


SMALL_PARAMS = (("sm", SM_B_NORM, 1), ("sm", SM_B_PW1, 2), ("sm", SM_B_CONV, KB), ("sm", SM_B_BCONV, 1),
                ("sm", SM_LN_G, 1), ("sm", SM_LN_B, 1), ("sm", SM_B_PW2, 1), ("sm", SM_A_CONV, KA),
                ("rp", 0, 1), ("rp", 1, 2), ("rp", 3, 1))


def _adamw_small(g_small, g_repl, triples):
    n = len(SMALL_PARAMS)

    def body(*refs):
        gs_ref, gr_ref = refs[0], refs[1]
        ins = refs[2:2 + 3 * n]
        outs = refs[2 + 3 * n:]
        for i, (src, r0, nr) in enumerate(SMALL_PARAMS):
            w_ref, m_ref, v_ref = ins[3 * i:3 * i + 3]
            g = (gs_ref if src == "sm" else gr_ref)[pl.ds(r0, nr), :]
            if i == 1:
                g = jnp.concatenate([g[0:1], g[1:2]], axis=1)
            lead = (0,) if len(w_ref.shape) == 3 else ()
            idx = lead + (slice(None), slice(None))
            vals = (g,) + _adam_math(w_ref[idx], g, m_ref[idx], v_ref[idx])
            for o_ref, val in zip(outs[4 * i:4 * i + 4], vals):
                o_ref[idx] = val

    flat = [a for t in triples for a in t]
    out_shape = tuple(_sds(t[0].shape, F32) for t in triples for _ in range(4))
    vm = pl.BlockSpec(memory_space=pltpu.VMEM)
    res = _pcall(
        body, name="adamw_small", out_shape=out_shape,
        in_specs=[vm] * (2 + len(flat)), out_specs=tuple(vm for _ in out_shape),
        compiler_params=_cparams(),
    )(g_small, g_repl, *flat)
    return [tuple(res[4 * i:4 * i + 4]) for i in range(n)]


def kernel(x, a_norm, a_w_in, a_conv, a_w_out, b_norm, b_w_pw1, b_b_pw1, b_conv, b_b_conv, b_ln_g, b_ln_b, b_w_pw2, b_b_pw2, ffn_norm, ffn_w_gate, ffn_w_up, ffn_w_down, final_norm, loss_target, m_a_norm, m_a_w_in, m_a_conv, m_a_w_out, m_b_norm, m_b_w_pw1, m_b_b_pw1, m_b_conv, m_b_b_conv, m_b_ln_g, m_b_ln_b, m_b_w_pw2, m_b_b_pw2, m_ffn_norm, m_ffn_w_gate, m_ffn_w_up, m_ffn_w_down, m_final_norm, v_a_norm, v_a_w_in, v_a_conv, v_a_w_out, v_b_norm, v_b_w_pw1, v_b_b_pw1, v_b_conv, v_b_b_conv, v_b_ln_g, v_b_ln_b, v_b_w_pw2, v_b_b_pw2, v_ffn_norm, v_ffn_w_gate, v_ffn_w_up, v_ffn_w_down, v_final_norm):
    T = x.shape[1]
    tm = min(TM, T)
    tma = min(TM_A, T)
    tw = min(TM_DW, T)
    xs = x.reshape(T, D)
    tgt = loss_target.reshape(T, D)

    def rows_first(a):
        return jnp.swapaxes(a, 1, 2)

    shard, small = _pack_shard(
        (a_w_in, a_w_out, b_w_pw1, b_w_pw2, rows_first(ffn_w_gate), rows_first(ffn_w_up), ffn_w_down),
        (b_norm, b_b_pw1, b_conv, b_b_conv, b_ln_g, b_ln_b, b_b_pw2, a_conv))
    wg, sm_all = _all_gather_first(shard, small)
    vecs = sm_all.transpose(1, 0, 2).reshape(SM_F32_ROWS, D)
    bias1 = sm_all[:, SM_B_PW1:SM_B_PW1 + 2, :].reshape(1, 2 * D)
    cw_a = vecs[SM_A_CONV:SM_A_CONV + KA]
    fn0, fn1 = ffn_norm[0:1], ffn_norm[1:2]
    fin = final_norm.reshape(1, D)

    h1, n0, bq, cq, vq, ccq, yq, wg = _mixa_fwd(xs, a_norm, cw_a, wg, tma, _ag_comm(shard, wg, ("g0", "u0", "d0")))
    h2, n1, g0, u0, gu0, wg = _ffn_fwd(h1, fn0, wg, 0, tma, _ag_comm(shard, wg, ("b_pw1", "b_pw2"), (T // tma) // 2))
    h3, n2, aq, gq, dcq, sq, wg = _mixb_fwd(h2, vecs, bias1, wg, tm, _ag_comm(shard, wg, ("g1", "u1", "d1"), 1))
    dh4, n3, g1, u1, gu1, st_fin = _ffn_fwd(h3, fn1, wg, 1, tma, None, head=(tgt, fin))

    def by_dest(gw, key):
        return gw.reshape(NDEV, N_ROWS[key], D)

    gw_d1 = by_dest(_grad_w(gu1, dh4, FH, "grad_down1", tw), "d1")
    dh3, dg1, du1, st_f1 = _ffn_bwd_dx(dh4, h3, g1, u1, fn1, wg, 1, tma, None)
    gw_g1 = by_dest(_grad_w(dg1, n3, FH, "grad_gate1", tw), "g1")
    gw_u1 = by_dest(_grad_w(du1, n3, FH, "grad_up1", tw), "u1")
    keys = ("g1", "u1", "d1")
    dh2, dub, st_b, st_b1, land = _mixb_bwd(
        dh3, h2, aq, gq, dcq, vecs, wg, tm, _rs_comm(_pair_reduce([gw_g1, gw_u1, gw_d1], keys), keys, None))
    gw_pw1 = by_dest(_grad_w(dub, n2, D, "grad_pw1", tw), "b_pw1")
    gw_pw2 = by_dest(_grad_w(sq, dh3, D, "grad_pw2", tw), "b_pw2")
    gw_d0 = by_dest(_grad_w(gu0, dh2, FH, "grad_down0", tw), "d0")
    keys = ("b_pw1", "b_pw2", "d0")
    dh1, dg0, du0, st_f0, land = _ffn_bwd_dx(
        dh2, h1, g0, u0, fn0, wg, 0, tma, _rs_comm(_pair_reduce([gw_pw1, gw_pw2, gw_d0], keys), keys, land))
    gw_g0 = by_dest(_grad_w(dg0, n1, FH, "grad_gate0", tw), "g0")
    gw_u0 = by_dest(_grad_w(du0, n1, FH, "grad_up0", tw), "u0")
    gw_out = by_dest(_grad_w(yq, dh1, D, "grad_out", tw), "a_out")
    keys = ("g0", "u0", "a_out")
    dx, dbcv, st_a, land = _mixa_bwd(
        dh1, xs, bq, cq, vq, ccq, a_norm, cw_a, wg, tma,
        _rs_comm(_pair_reduce([gw_g0, gw_u0, gw_out], keys), keys, land))
    gw_in = by_dest(_grad_w(dbcv, n0, D, "grad_in", tw), "a_in")

    st_small = st_b.at[SM_A_CONV:SM_A_CONV + KA].set(st_a[1:1 + KA])
    sm_dest = st_small.reshape(SM_F32_ROWS, NDEV, 128).transpose(1, 0, 2)
    sm_dest = sm_dest.at[:, SM_B_PW1:SM_B_PW1 + 2, :].set(st_b1[0].reshape(NDEV, 2, 128))
    repl = jnp.concatenate([st_a[0:1], st_f0[0:1], st_f1[0:1], st_fin[0:1], st_fin[1:2],
                            jnp.zeros((3, D), F32)], axis=0)[None]
    p_in, sm_land, rp_land = _pair_reduce([gw_in], ("a_in",), side=(sm_dest, repl))

    big = [("ffn_w_gate", ("g0", "g1"), False,
            rows_first(ffn_w_gate), rows_first(m_ffn_w_gate), rows_first(v_ffn_w_gate)),
           ("ffn_w_up", ("u0", "u1"), False, rows_first(ffn_w_up), rows_first(m_ffn_w_up), rows_first(v_ffn_w_up)),
           ("ffn_w_down", ("d0", "d1"), False, ffn_w_down, m_ffn_w_down, v_ffn_w_down),
           ("a_w_out", ("a_out",), False, a_w_out, m_a_w_out, v_a_w_out),
           ("b_w_pw1", ("b_pw1",), True, b_w_pw1, m_b_w_pw1, v_b_w_pw1),
           ("b_w_pw2", ("b_pw2",), False, b_w_pw2, m_b_w_pw2, v_b_w_pw2),
           ("a_w_in", ("a_in",), True, a_w_in, m_a_w_in, v_a_w_in)]
    res = {}
    for k, (name, keys, transposed, w, m, v) in enumerate(big):
        if k < 3:
            comm = _rs_comm(p_in, ("a_in",), land, part=(k, 3), body_reads_land=True)
            *res[name], land = _finish_weight(land, keys, transposed, w, m, v, name, comm)
        else:
            res[name] = _finish_weight(land, keys, transposed, w, m, v, name)
    for name in ("ffn_w_gate", "ffn_w_up"):
        res[name] = [rows_first(a) for a in res[name]]

    g_small, g_repl, loss8 = _sum_small(sm_land, rp_land)
    loss = loss8[0, 0]

    small_names = ["b_norm", "b_b_pw1", "b_conv", "b_b_conv", "b_ln_g", "b_ln_b", "b_b_pw2", "a_conv",
                   "a_norm", "ffn_norm", "final_norm"]
    triples = [(b_norm, m_b_norm, v_b_norm), (b_b_pw1, m_b_b_pw1, v_b_b_pw1), (b_conv, m_b_conv, v_b_conv),
               (b_b_conv, m_b_b_conv, v_b_b_conv), (b_ln_g, m_b_ln_g, v_b_ln_g), (b_ln_b, m_b_ln_b, v_b_ln_b),
               (b_b_pw2, m_b_b_pw2, v_b_b_pw2), (a_conv, m_a_conv, v_a_conv), (a_norm, m_a_norm, v_a_norm),
               (ffn_norm, m_ffn_norm, v_ffn_norm),
               (fin, m_final_norm.reshape(1, D), v_final_norm.reshape(1, D))]
    for name, quad in zip(small_names, _adamw_small(g_small, g_repl, triples)):
        res[name] = quad
    res["final_norm"] = tuple(a.reshape(D) for a in res["final_norm"])

    order = ["a_norm", "a_w_in", "a_conv", "a_w_out", "b_norm", "b_w_pw1", "b_b_pw1", "b_conv", "b_b_conv",
             "b_ln_g", "b_ln_b", "b_w_pw2", "b_b_pw2", "ffn_norm", "ffn_w_gate", "ffn_w_up", "ffn_w_down",
             "final_norm"]
    out = [loss, dx.reshape(1, T, D)]
    for j in range(4):
        out += [res[k][j] for k in order]
    return tuple(out)
```

```python
import functools

import jax
import jax.numpy as jnp
from jax import lax
from jax.experimental import pallas as pl
from jax.experimental.pallas import tpu as pltpu

F32 = jnp.float32
BF16 = jnp.bfloat16
MESH = pl.DeviceIdType.MESH

D = 1024
FF = 2816
FH = FF // 2
NDEV = 8
KA = 3
KB = 31
HALO_A = 8
HALO_B = 32
CONV_ROWS = 64
RMS_EPS = 1e-6
LN_EPS = 1e-5
LR, B1, B2, ADAM_EPS, WD, STEP = 0.001, 0.9, 0.999, 1e-08, 0.01, 10

TM = 256
TM_A = 512
TM_DW = 1024
VMEM_LIMIT = 62 * 1024 * 1024

N_ROWS = {"a_in": 384, "a_out": 128, "b_pw1": 256, "b_pw2": 128, "g0": 352, "u0": 352, "d0": 352,
          "g1": 352, "u1": 352, "d1": 352}
WEIGHT_KEYS = ("a_in", "a_out", "b_pw1", "b_pw2", "g0", "u0", "d0", "g1", "u1", "d1")
NCHIP = 4


def _offsets(order):
    off, o = {}, 0
    for k in order:
        off[k] = o
        o += N_ROWS[k]
    return off, o


PK_ORDER = ("a_in", "a_out", "g0", "u0", "d0", "b_pw1", "b_pw2", "g1", "u1", "d1")
PK_OFF, PK_ROWS = _offsets(PK_ORDER)
LD_ORDER = ("g1", "u1", "d1", "b_pw1", "b_pw2", "d0", "g0", "u0", "a_out", "a_in")
LD_OFF, N_WROWS = _offsets(LD_ORDER)


def _span(off, keys):
    return off[keys[0]], sum(N_ROWS[k] for k in keys)


SM_B_NORM, SM_B_PW1, SM_B_CONV, SM_B_BCONV, SM_LN_G, SM_LN_B, SM_B_PW2, SM_A_CONV = 0, 1, 3, 34, 35, 36, 37, 38
SM_USED = 41
SM_F32_ROWS = 64


def _pcall(body, **kw):
    return pl.pallas_call(body, **kw)


def _cparams(sem=None):
    return pltpu.CompilerParams(dimension_semantics=sem, vmem_limit_bytes=VMEM_LIMIT)


def _dot(a, b):
    return jnp.dot(a, b, preferred_element_type=F32)


def _dot_nt(a, b):
    return lax.dot_general(a, b, (((1,), (1,)), ((), ())), preferred_element_type=F32)


def _dot_tn(a, b):
    return lax.dot_general(a, b, (((0,), (0,)), ((), ())), preferred_element_type=F32)


def _sigmoid(v):
    return 1.0 / (1.0 + jnp.exp(-v))


def _rms_stat(x):
    return lax.rsqrt(jnp.mean(x * x, axis=-1, keepdims=True) + RMS_EPS)


def _rms_bwd(dn, x, r, gamma):
    dng = dn * gamma
    dx = r * dng - x * (r * r * r) * jnp.mean(dng * x, axis=-1, keepdims=True)
    return dx, jnp.sum(dn * x * r, axis=0, keepdims=True)


def _weight_loads(wg_ref, plan, sems):
    copies = []
    for j, (dst, key) in enumerate(plan):
        off, n = PK_OFF[key], N_ROWS[key]
        copies += [pltpu.make_async_copy(
            wg_ref.at[d, pl.ds(off, n), :], dst.at[pl.ds(d * n, n), :], sems.at[j * NDEV + d])
            for d in range(NDEV)]

    def load():
        for cp in copies:
            cp.start()
        for cp in copies:
            cp.wait()

    return load


_ANY = pl.BlockSpec(memory_space=pl.ANY)


def _row_spec(tm, width, rev_nt=None):
    if rev_nt is None:
        return pl.BlockSpec((tm, width), lambda i: (i, 0))
    return pl.BlockSpec((tm, width), lambda i: (rev_nt - 1 - i, 0))


def _full_spec(shape):
    return pl.BlockSpec(shape, lambda *_: (0,) * len(shape))


def _sds(shape, dtype):
    return jax.ShapeDtypeStruct(shape, dtype)


def _mesh_pos():
    return lax.axis_index("x"), lax.axis_index("y"), lax.axis_index("c")


def _lin(p):
    return 4 * p[0] + 2 * p[1] + p[2]


def _ag_exchange(src, slot, send_sems, recv_sems, local_sem):
    x, y, c = _mesh_pos()
    me, sibling = (x, y, c), (x, y, 1 - c)
    chips = [(1 - x, y), (x, 1 - y), (1 - x, 1 - y)]

    def copy(k, block, to, own=False):
        return pltpu.make_async_remote_copy(
            src_ref=src if own else slot(block), dst_ref=slot(block),
            send_sem=send_sems.at[k], recv_sem=recv_sems.at[k], device_id=to, device_id_type=MESH)

    mine = pltpu.make_async_copy(src, slot(me), local_sem)
    first = [copy(0, me, sibling, own=True)]
    first += [copy(1 + j, me, (*chip, c), own=True) for j, chip in enumerate(chips)]
    passed = [copy(4 + j, (*chip, c), sibling) for j, chip in enumerate(chips)]

    def start():
        mine.start()
        for cp in first:
            cp.start()

    def forward():
        for j, chip in enumerate(chips):
            copy(1 + j, (*chip, c), me).wait_recv()
            passed[j].start()

    def finish():
        copy(0, sibling, me).wait_recv()
        for j, chip in enumerate(chips):
            copy(4 + j, (*chip, 1 - c), me).wait_recv()
        for cp in first + passed:
            cp.wait_send()
        mine.wait()

    return start, forward, finish


def _chip_exchange(p_ref, land_ref, send_sems, recv_sems, keys, part):
    x, y, c = _mesh_pos()
    l0, rows = _span(LD_OFF, keys)
    p0, rows = (0, rows) if part is None else part
    dst = land_ref.at[2 * x + y, pl.ds(l0 + p0, rows), :]
    peers = [((x + (j >> 1)) % 2, (y + (j & 1)) % 2) for j in range(NCHIP)]

    def copy(j):
        tx, ty = peers[j]
        src = p_ref.at[2 * tx + ty, pl.ds(p0, rows), :]
        if j == 0:
            return pltpu.make_async_copy(src, dst, send_sems.at[0])
        return pltpu.make_async_remote_copy(
            src_ref=src, dst_ref=dst, send_sem=send_sems.at[j], recv_sem=recv_sems.at[j],
            device_id=(tx, ty, c), device_id_type=MESH)

    def start():
        for j in range(NCHIP):
            copy(j).start()

    def finish():
        for j in range(NCHIP):
            copy(j).wait()

    return start, finish


def _all_to_all_f32(src_for, dst_ref, send_sems, recv_sems):
    x, y, c = _mesh_pos()
    dst = dst_ref.at[_lin((x, y, c))]
    peers = [((x + ((j >> 2) & 1)) % 2, (y + ((j >> 1) & 1)) % 2, (c + (j & 1)) % 2) for j in range(NDEV)]

    def copy(j):
        if j == 0:
            return pltpu.make_async_copy(src_for(_lin(peers[0])), dst, send_sems.at[0])
        return pltpu.make_async_remote_copy(
            src_ref=src_for(_lin(peers[j])), dst_ref=dst, send_sem=send_sems.at[j], recv_sem=recv_sems.at[j],
            device_id=peers[j], device_id_type=MESH)

    def start():
        for j in range(NDEV):
            copy(j).start()

    def finish():
        for j in range(NDEV):
            copy(j).wait()

    return start, finish


class _Comm:
    def __init__(self, ins, alias_in, out_shape, scratch, make, gives_wg, middle_step=0):
        self.ins, self.alias_in, self.out_shape = ins, alias_in, out_shape
        self.scratch, self.make, self.gives_wg, self.middle_step = scratch, make, gives_wg, middle_step


def _ag_comm(shard, wg, keys, forward_step=0):
    def make(c_ins, c_out, sc):
        return _weights_exchange(c_ins[0], c_out, sc[0], sc[1], sc[2], keys)

    return _Comm([shard, wg], 1, _sds(wg.shape, BF16),
                 [pltpu.SemaphoreType.DMA((7,)), pltpu.SemaphoreType.DMA((7,)), pltpu.SemaphoreType.DMA],
                 make, True, forward_step)


def _weights_exchange(shard_ref, wg_ref, send_sems, recv_sems, local_sem, keys):
    r0, nr = _span(PK_OFF, keys)
    return _ag_exchange(shard_ref.at[pl.ds(r0, nr), :], lambda p: wg_ref.at[_lin(p), pl.ds(r0, nr), :],
                        send_sems, recv_sems, local_sem)


def _rs_comm(psum, keys, land, part=None, body_reads_land=False):
    def make(c_ins, c_out, sc):
        start, finish = _chip_exchange(c_ins[0], c_out, sc[0], sc[1], keys, part)
        return start, None, finish

    ins = [psum] + ([] if land is None else [land])
    return _Comm(ins, None if land is None else 1, _sds((NCHIP, N_WROWS, D), BF16),
                 [pltpu.SemaphoreType.DMA((NCHIP,)), pltpu.SemaphoreType.DMA((NCHIP,))], make, body_reads_land)


def _hosted_call(body, name, nt, in_specs, out_specs, out_shape, scratch, args, comm):
    if comm is None:
        return _pcall(body, name=name, grid=(nt,), in_specs=in_specs, out_specs=tuple(out_specs),
                      out_shape=tuple(out_shape), scratch_shapes=scratch,
                      compiler_params=_cparams(("arbitrary",)))(*args)
    n_in, n_out, n_sc, n_cin = len(in_specs), len(out_specs), len(scratch), len(comm.ins)

    def wrapped(*refs):
        ins = refs[:n_in]
        c_ins = refs[n_in:n_in + n_cin]
        outs = refs[n_in + n_cin:n_in + n_cin + n_out]
        c_out = refs[n_in + n_cin + n_out]
        sc = refs[n_in + n_cin + n_out + 1:n_in + n_cin + n_out + 1 + n_sc]
        c_sc = refs[n_in + n_cin + n_out + 1 + n_sc:]
        start, middle, finish = comm.make(c_ins, c_out, c_sc)
        pl.when(pl.program_id(0) == 0)(start)
        if comm.gives_wg:
            body(*ins, c_out, *outs, *sc)
        else:
            body(*ins, *outs, *sc)
        if middle is not None:
            pl.when(pl.program_id(0) == max(nt - 1 - comm.middle_step, 0))(middle)
        pl.when(pl.program_id(0) == nt - 1)(finish)

    aliases = {} if comm.alias_in is None else {n_in + comm.alias_in: n_out}
    res = _pcall(wrapped, name=name, grid=(nt,),
                 in_specs=list(in_specs) + [_ANY] * n_cin, out_specs=tuple(out_specs) + (_ANY,),
                 out_shape=tuple(out_shape) + (comm.out_shape,),
                 scratch_shapes=list(scratch) + list(comm.scratch),
                 input_output_aliases=aliases,
                 compiler_params=_cparams(("arbitrary",)))(*args, *comm.ins)
    return res


def _pack_shard(weights, smalls):
    plan = (("a_in", 0, 0, True), ("a_out", 1, 0, False), ("b_pw1", 2, 0, True), ("b_pw2", 3, 0, False),
            ("g0", 4, 0, False), ("u0", 5, 0, False), ("d0", 6, 0, False),
            ("g1", 4, 1, False), ("u1", 5, 1, False), ("d1", 6, 1, False))
    n_pad = 384

    def body(*refs):
        out, sm, pad = refs[-3:]
        for key, idx, layer, transposed in plan:
            n = N_ROWS[key]
            val = refs[idx][layer]
            if transposed:
                if n % 128:
                    pad[:, pl.ds(0, n)] = val
                    pad[:, pl.ds(n, n_pad - n)] = jnp.zeros((D, n_pad - n), F32)
                    val = pad[...]
                val = val.T[:n]
            out[pl.ds(PK_OFF[key], n), :] = val.astype(BF16)
        b_norm, b_b_pw1, b_conv, b_b_conv, b_ln_g, b_ln_b, b_b_pw2, a_conv = refs[len(weights):-3]
        sm[...] = jnp.zeros((SM_F32_ROWS, 128), F32)
        sm[pl.ds(SM_B_PW1, 1), :] = b_b_pw1[:, pl.ds(0, 128)]
        sm[pl.ds(SM_B_PW1 + 1, 1), :] = b_b_pw1[:, pl.ds(128, 128)]
        sm[pl.ds(SM_B_CONV, KB), :] = b_conv[0]
        sm[pl.ds(SM_A_CONV, KA), :] = a_conv[0]
        for row, ref in ((SM_B_NORM, b_norm), (SM_B_BCONV, b_b_conv), (SM_LN_G, b_ln_g), (SM_LN_B, b_ln_b),
                         (SM_B_PW2, b_b_pw2)):
            sm[pl.ds(row, 1), :] = ref[...]

    vm = pl.BlockSpec(memory_space=pltpu.VMEM)
    return _pcall(
        body, name="pack_shard",
        out_shape=(_sds((PK_ROWS, D), BF16), _sds((SM_F32_ROWS, 128), F32)),
        in_specs=[vm] * (len(weights) + len(smalls)), out_specs=(vm, vm),
        scratch_shapes=[pltpu.VMEM((D, n_pad), F32)],
        compiler_params=_cparams(),
    )(*weights, *smalls)


def _all_gather_first(shard, small):
    def body(x_ref, s_ref, wg_ref, sg_ref, send_w, recv_w, local_w, send_s, recv_s, local_s):
        start_w, forward_w, finish_w = _weights_exchange(x_ref, wg_ref, send_w, recv_w, local_w, ("a_in", "a_out"))
        start_s, forward_s, finish_s = _ag_exchange(s_ref, lambda p: sg_ref.at[_lin(p)], send_s, recv_s, local_s)
        start_s()
        start_w()
        forward_s()
        forward_w()
        finish_s()
        finish_w()

    sems = [pltpu.SemaphoreType.DMA((7,)), pltpu.SemaphoreType.DMA((7,)), pltpu.SemaphoreType.DMA]
    return _pcall(
        body, name="all_gather_first",
        out_shape=(_sds((NDEV, PK_ROWS, D), BF16), _sds((NDEV, SM_F32_ROWS, 128), F32)),
        in_specs=[_ANY, _ANY], out_specs=(_ANY, _ANY), scratch_shapes=sems + sems,
    )(shard, small)


def _pair_reduce(grads, keys, side=None):
    l0, rows = _span(LD_OFF, keys)
    nk = len(keys)
    n_side = 0 if side is None else 2

    def body(c_ref, *refs):
        mine = refs[:nk]
        whole = refs[nk:2 * nk]
        o_ref = refs[2 * nk + n_side]
        got, send_sem, recv_sems = refs[2 * nk + 2 * n_side + 1:2 * nk + 2 * n_side + 4]
        t = pl.program_id(0)
        x, y, c = _mesh_pos()
        sibling = (x, y, 1 - c)
        if side is not None:
            sm_ref, rp_ref = refs[2 * nk:2 * nk + 2]
            sm_land, rp_land = refs[2 * nk + 3:2 * nk + 5]
            s2, r2, s3, r3 = refs[-4:]
            start_s, finish_s = _all_to_all_f32(lambda d: sm_ref.at[d], sm_land, s2, r2)
            start_r, finish_r = _all_to_all_f32(lambda d: rp_ref.at[0], rp_land, s3, r3)

        @pl.when(t == 0)
        def _():
            if side is not None:
                start_s()
                start_r()
            for ref, key in zip(whole, keys):
                for d in range(NCHIP):
                    pltpu.make_async_remote_copy(
                        src_ref=ref.at[2 * d + (1 - c)], dst_ref=got.at[d, pl.ds(LD_OFF[key] - l0, N_ROWS[key]), :],
                        send_sem=send_sem, recv_sem=recv_sems.at[d], device_id=sibling, device_id_type=MESH).start()

        pltpu.make_async_remote_copy(src_ref=got.at[t], dst_ref=got.at[t], send_sem=send_sem,
                                     recv_sem=recv_sems.at[t], device_id=sibling, device_id_type=MESH).wait_recv()
        for ref, key in zip(mine, keys):
            sl = pl.ds(LD_OFF[key] - l0, N_ROWS[key])
            o_ref[sl, :] = (ref[...].astype(F32) + got[t, sl, :].astype(F32)).astype(BF16)

        @pl.when(t == NCHIP - 1)
        def _():
            pltpu.make_async_remote_copy(src_ref=got, dst_ref=got, send_sem=send_sem, recv_sem=recv_sems.at[0],
                                         device_id=sibling, device_id_type=MESH).wait_send()
            if side is not None:
                finish_s()
                finish_r()

    out_specs = [pl.BlockSpec((None, rows, D), lambda t, c: (t, 0, 0))]
    out_shape = [_sds((NCHIP, rows, D), BF16)]
    scratch = [pltpu.VMEM((NCHIP, rows, D), BF16), pltpu.SemaphoreType.DMA, pltpu.SemaphoreType.DMA((NCHIP,))]
    if side is not None:
        out_specs += [_ANY, _ANY]
        out_shape += [_sds((NDEV, SM_F32_ROWS, 128), F32), _sds((NDEV, 8, D), F32)]
        scratch += [pltpu.SemaphoreType.DMA((NDEV,))] * 4
    grid_spec = pltpu.PrefetchScalarGridSpec(
        num_scalar_prefetch=1, grid=(NCHIP,),
        in_specs=[pl.BlockSpec((None, N_ROWS[k], D), lambda t, c: (2 * t + c[0], 0, 0)) for k in keys]
        + [_ANY] * (nk + n_side),
        out_specs=tuple(out_specs), scratch_shapes=scratch)
    core = lax.axis_index("c").astype(jnp.int32).reshape(1)
    res = _pcall(
        body, name="pair_reduce_" + keys[0], grid_spec=grid_spec, out_shape=tuple(out_shape),
        compiler_params=_cparams(("arbitrary",)),
    )(core, *grads, *grads, *(side or ()))
    return res[0] if side is None else res


def _mixa_fwd(x, gam, cw, wg, tm, comm):
    T = x.shape[0]

    def body(x_ref, gam_ref, cw_ref, wg_ref, h_ref, n_ref, b_ref, c_ref, v_ref, cc_ref, y_ref,
             win, wout, buf, sems):
        first = pl.program_id(0) == 0
        load_w = _weight_loads(wg_ref, [(win, "a_in"), (wout, "a_out")], sems)

        @pl.when(first)
        def _():
            load_w()
            buf[pl.ds(0, HALO_A), :] = jnp.zeros((HALO_A, D), F32)

        xv = x_ref[...]
        nb = (xv * _rms_stat(xv) * gam_ref[...]).astype(BF16)
        n_ref[...] = nb
        bv = _dot_nt(nb, win[pl.ds(0, D), :])
        cval = _dot_nt(nb, win[pl.ds(D, D), :])
        vval = _dot_nt(nb, win[pl.ds(2 * D, D), :])
        cv = cval * vval
        buf[pl.ds(HALO_A, tm), :] = cv
        cc = cw_ref[pl.ds(KA - 1, 1), :] * cv
        for k in range(KA - 1):
            cc = cc + cw_ref[pl.ds(k, 1), :] * buf[pl.ds(HALO_A - (KA - 1) + k, tm), :]
        buf[pl.ds(0, HALO_A), :] = buf[pl.ds(tm, HALO_A), :]
        yb = (bv * cc).astype(BF16)
        b_ref[...] = bv.astype(BF16)
        c_ref[...] = cval.astype(BF16)
        v_ref[...] = vval.astype(BF16)
        cc_ref[...] = cc.astype(BF16)
        y_ref[...] = yb
        h_ref[...] = xv + _dot(yb, wout[...])

    act = _sds((T, D), BF16)
    return _hosted_call(
        body, "mixa_fwd", T // tm,
        in_specs=[_row_spec(tm, D), _full_spec((1, D)), _full_spec((KA, D))],
        out_specs=[_row_spec(tm, D) for _ in range(7)],
        out_shape=(_sds((T, D), F32),) + (act,) * 6,
        scratch=[pltpu.VMEM((3 * D, D), BF16), pltpu.VMEM((D, D), BF16),
                 pltpu.VMEM((HALO_A + tm, D), F32), pltpu.SemaphoreType.DMA((2 * NDEV,))],
        args=(x, gam, cw), comm=comm)


def _ffn_fwd(h, gam, wg, layer, tm, comm, head=None):
    T = h.shape[0]
    kg, ku, kd = "g%d" % layer, "u%d" % layer, "d%d" % layer
    n_head = 0 if head is None else 2

    def body(*refs):
        h_ref, gam_ref = refs[:2]
        wg_ref = refs[2 + n_head]
        o_ref, n_ref, g_ref, u_ref, gu_ref = refs[3 + n_head:8 + n_head]
        wgt, wut, wd, sems = refs[-4:]

        first = pl.program_id(0) == 0
        load_w = _weight_loads(wg_ref, [(wgt, kg), (wut, ku), (wd, kd)], sems)
        pl.when(first)(load_w)

        hv = h_ref[...]
        nb = (hv * _rms_stat(hv) * gam_ref[...]).astype(BF16)
        n_ref[...] = nb
        out = hv
        for f in range(2):
            cols = pl.ds(f * FH, FH)
            g = _dot_nt(nb, wgt[cols, :])
            u = _dot_nt(nb, wut[cols, :])
            gu = (g * _sigmoid(g) * u).astype(BF16)
            g_ref[:, cols] = g.astype(BF16)
            u_ref[:, cols] = u.astype(BF16)
            gu_ref[:, cols] = gu
            out = out + _dot(gu, wd[cols, :])
        if head is None:
            o_ref[...] = out
        else:
            t_ref, fin_ref, st_ref = refs[2], refs[3], refs[8 + n_head]

            @pl.when(pl.program_id(0) == 0)
            def _():
                st_ref[...] = jnp.zeros((8, D), F32)

            gamma = fin_ref[...]
            r = _rms_stat(out)
            err = out * r * gamma - t_ref[...]
            dx, dgam = _rms_bwd(err * (1.0 / D), out, r, gamma)
            o_ref[...] = dx
            st_ref[pl.ds(0, 1), :] += dgam
            st_ref[pl.ds(1, 1), :] += (0.5 / D) * jnp.sum(err * err, axis=0, keepdims=True)

    pre = _sds((T, FF), BF16)
    in_specs = [_row_spec(tm, D), _full_spec((1, D))]
    args = (h, gam)
    out_specs = [_row_spec(tm, D), _row_spec(tm, D), _row_spec(tm, FF), _row_spec(tm, FF), _row_spec(tm, FF)]
    out_shape = (_sds((T, D), F32), _sds((T, D), BF16), pre, pre, pre)
    if head is not None:
        in_specs, args = in_specs + [_row_spec(tm, D), _full_spec((1, D))], args + tuple(head)
        out_specs, out_shape = out_specs + [_full_spec((8, D))], out_shape + (_sds((8, D), F32),)
    if comm is None:
        in_specs, args = in_specs + [_ANY], args + (wg,)
    return _hosted_call(
        body, "ffn%d_fwd" % layer, T // tm,
        in_specs=in_specs, out_specs=out_specs, out_shape=out_shape,
        scratch=[pltpu.VMEM((FF, D), BF16)] * 3 + [pltpu.SemaphoreType.DMA((3 * NDEV,))],
        args=args, comm=comm)


def _shifted_copies(buf, shf, tm):
    for r in range(1, 8):
        shf[r - 1] = buf[pl.ds(r, tm + HALO_B - 8), :]


def _broadcast_taps(vec_ref, wb):
    for k in range(KB):
        wb[k] = jnp.broadcast_to(vec_ref[pl.ds(SM_B_CONV + k, 1), :], (8, D))
    wb[KB] = jnp.broadcast_to(vec_ref[pl.ds(SM_B_BCONV, 1), :], (8, D))


def _taps_by_shift_residue(taps):
    groups = {}
    for k, shift in taps:
        q, r = divmod(shift, 8)
        groups.setdefault(r, []).append((k, q))
    return sorted(groups.items())


def _window(buf, shf, base, r, q0, n_groups, lanes):
    rows = pl.ds(base + 8 * q0, 8 * n_groups)
    v = buf[rows, lanes] if r == 0 else shf[r - 1, rows, lanes]
    return [v[8 * i:8 * i + 8] for i in range(n_groups)]


def _long_conv(buf, shf, wb, out_ref, tm, taps, bias_row):
    n_acc = CONV_ROWS // 8
    groups = _taps_by_shift_residue(taps)
    for col in range(D // 128):
        lanes = pl.ds(128 * col, 128)

        def rows(i, carry, lanes=lanes):
            base = i * CONV_ROWS
            init = jnp.zeros((8, 128), F32) if bias_row is None else wb[bias_row, :, lanes]
            accs = [init] * n_acc
            for r, lst in groups:
                q0, q1 = min(q for _, q in lst), max(q for _, q in lst)
                win = _window(buf, shf, base, r, q0, n_acc + q1 - q0, lanes)
                for k, q in lst:
                    wk = wb[k, :, lanes]
                    accs = [acc + wk * win[h + q - q0] for h, acc in enumerate(accs)]
            for h, acc in enumerate(accs):
                out_ref[pl.ds(base + 8 * h, 8), lanes] = acc
            return carry

        for i in range(tm // CONV_ROWS):
            rows(i, 0)


def _long_conv_grad_taps(buf, shf, x_ref, wacc, tm):
    n_acc = CONV_ROWS // 8
    groups = _taps_by_shift_residue([(KB - 1 - j, j) for j in range(KB)])
    for col in range(D // 128):
        lanes = pl.ds(128 * col, 128)

        def rows(i, carry, lanes=lanes):
            base = i * CONV_ROWS
            xv = x_ref[pl.ds(base, CONV_ROWS), lanes]
            xs = [xv[8 * h:8 * h + 8] for h in range(n_acc)]
            for r, lst in groups:
                q0, q1 = min(q for _, q in lst), max(q for _, q in lst)
                win = _window(buf, shf, base, r, q0, n_acc + q1 - q0, lanes)
                for k, q in lst:
                    prod = [x * win[h + q - q0] for h, x in enumerate(xs)]
                    wacc[k, :, lanes] += functools.reduce(lambda a, b: a + b, prod)
            return carry

        for i in range(tm // CONV_ROWS):
            rows(i, 0)


def _ln_stats(dc):
    mu = jnp.mean(dc, axis=-1, keepdims=True)
    xc = dc - mu
    rstd = lax.rsqrt(jnp.mean(xc * xc, axis=-1, keepdims=True) + LN_EPS)
    return xc * rstd, rstd


def _mixb_fwd(h, vecs, bias1, wg, tm, comm):
    T = h.shape[0]

    def body(h_ref, vec_ref, b1_ref, wg_ref, o_ref, n_ref, a_ref, g_ref, dc_ref, s_ref, w1, w2, buf, shf, wb, sems):
        first = pl.program_id(0) == 0
        load_w = _weight_loads(wg_ref, [(w1, "b_pw1"), (w2, "b_pw2")], sems)

        @pl.when(first)
        def _():
            load_w()
            buf[pl.ds(0, HALO_B), :] = jnp.zeros((HALO_B, D), F32)
            _broadcast_taps(vec_ref, wb)

        hv = h_ref[...]
        nb = (hv * _rms_stat(hv) * vec_ref[pl.ds(SM_B_NORM, 1), :]).astype(BF16)
        n_ref[...] = nb
        a = _dot_nt(nb, w1[pl.ds(0, D), :]) + b1_ref[:, pl.ds(0, D)]
        g = _dot_nt(nb, w1[pl.ds(D, D), :]) + b1_ref[:, pl.ds(D, D)]
        a_ref[...] = a.astype(BF16)
        g_ref[...] = g.astype(BF16)
        buf[pl.ds(HALO_B, tm), :] = a * _sigmoid(g)
        _shifted_copies(buf, shf, tm)

        _long_conv(buf, shf, wb, dc_ref, tm, [(k, HALO_B - (KB - 1) + k) for k in range(KB)], KB)
        buf[pl.ds(0, HALO_B), :] = buf[pl.ds(tm, HALO_B), :]
        xhat, _ = _ln_stats(dc_ref[...])
        ln = xhat * vec_ref[pl.ds(SM_LN_G, 1), :] + vec_ref[pl.ds(SM_LN_B, 1), :]
        s = (ln * _sigmoid(ln)).astype(BF16)
        s_ref[...] = s
        o_ref[...] = hv + _dot(s, w2[...]) + vec_ref[pl.ds(SM_B_PW2, 1), :]

    act = _sds((T, D), BF16)
    return _hosted_call(
        body, "mixb_fwd", T // tm,
        in_specs=[_row_spec(tm, D), _full_spec((SM_F32_ROWS, D)), _full_spec((1, 2 * D))],
        out_specs=[_row_spec(tm, D) for _ in range(6)],
        out_shape=(_sds((T, D), F32), act, act, act, _sds((T, D), F32), act),
        scratch=[pltpu.VMEM((2 * D, D), BF16), pltpu.VMEM((D, D), BF16),
                 pltpu.VMEM((HALO_B + tm, D), F32), pltpu.VMEM((7, HALO_B + tm - 8, D), F32),
                 pltpu.VMEM((KB + 1, 8, D), F32), pltpu.SemaphoreType.DMA((2 * NDEV,))],
        args=(h, vecs, bias1), comm=comm)


def _ffn_bwd_dx(dh, h, g, u, gam, wg, layer, tm, comm):
    T = h.shape[0]
    kg, ku, kd = "g%d" % layer, "u%d" % layer, "d%d" % layer

    def body(dh_ref, h_ref, g_ref, u_ref, gam_ref, wg_ref, o_ref, dg_ref, du_ref, st_ref, wgt, wut, wd, sems):
        first = pl.program_id(0) == 0
        load_w = _weight_loads(wg_ref, [(wd, kd), (wgt, kg), (wut, ku)], sems)

        @pl.when(first)
        def _():
            load_w()
            st_ref[...] = jnp.zeros((8, D), F32)

        dhv = dh_ref[...]
        dhb = dhv.astype(BF16)
        dn = jnp.zeros_like(dhv)
        for f in range(2):
            cols = pl.ds(f * FH, FH)
            dgu = _dot_nt(dhb, wd[cols, :])
            gv = g_ref[:, cols].astype(F32)
            sg = _sigmoid(gv)
            du = (dgu * gv * sg).astype(BF16)
            dg = (dgu * u_ref[:, cols].astype(F32) * (sg * (1.0 + gv * (1.0 - sg)))).astype(BF16)
            dg_ref[:, cols] = dg
            du_ref[:, cols] = du
            dn = dn + _dot(dg, wgt[cols, :]) + _dot(du, wut[cols, :])
        hv = h_ref[...]
        dx, dgam = _rms_bwd(dn, hv, _rms_stat(hv), gam_ref[...])
        o_ref[...] = dhv + dx
        st_ref[pl.ds(0, 1), :] += dgam

    pre = _sds((T, FF), BF16)
    return _hosted_call(
        body, "ffn%d_bwd_dx" % layer, T // tm,
        in_specs=[_row_spec(tm, D), _row_spec(tm, D), _row_spec(tm, FF), _row_spec(tm, FF),
                  _full_spec((1, D)), _ANY],
        out_specs=[_row_spec(tm, D), _row_spec(tm, FF), _row_spec(tm, FF), _full_spec((8, D))],
        out_shape=(_sds((T, D), F32), pre, pre, _sds((8, D), F32)),
        scratch=[pltpu.VMEM((FF, D), BF16)] * 3 + [pltpu.SemaphoreType.DMA((3 * NDEV,))],
        args=(dh, h, g, u, gam, wg), comm=comm)


def _grad_w(lhs, rhs, mc, name, tm):
    T, M = lhs.shape
    nt = T // tm

    def body(l_ref, r_ref, o_ref, acc):
        i = pl.program_id(1)

        @pl.when(i == 0)
        def _():
            acc[...] = jnp.zeros((mc, D), F32)

        acc[...] += _dot_tn(l_ref[...], r_ref[...].astype(BF16))

        @pl.when(i == nt - 1)
        def _():
            o_ref[...] = acc[...].astype(BF16)

    return _pcall(
        body, name=name, grid=(M // mc, nt),
        out_shape=_sds((M, D), BF16),
        in_specs=[pl.BlockSpec((tm, mc), lambda j, i: (i, j)), pl.BlockSpec((tm, D), lambda j, i: (i, 0))],
        out_specs=pl.BlockSpec((mc, D), lambda j, i: (j, 0)),
        scratch_shapes=[pltpu.VMEM((mc, D), F32)],
        compiler_params=_cparams(("arbitrary", "arbitrary")),
    )(lhs, rhs)


def _mixb_bwd(dh, h, a, g, dc, vecs, wg, tm, comm):
    T = h.shape[0]
    nt = T // tm

    def body(dh_ref, h_ref, a_ref, g_ref, dc_ref, vec_ref, wg_ref, o_ref, du_ref, st_ref, sb_ref,
             w1, w2, buf, shf, glu_s, dglu_s, wacc, wb, sems):
        first = pl.program_id(0) == 0
        load_w = _weight_loads(wg_ref, [(w2, "b_pw2"), (w1, "b_pw1")], sems)

        @pl.when(first)
        def _():
            load_w()
            buf[pl.ds(tm, HALO_B), :] = jnp.zeros((HALO_B, D), F32)
            wacc[...] = jnp.zeros((KB, 8, D), F32)
            _broadcast_taps(vec_ref, wb)
            st_ref[...] = jnp.zeros((SM_F32_ROWS, D), F32)
            sb_ref[...] = jnp.zeros((8, 2 * D), F32)

        def acc(row, val):
            st_ref[pl.ds(row, 1), :] += jnp.sum(val, axis=0, keepdims=True)

        dhv = dh_ref[...]
        acc(SM_B_PW2, dhv)
        ds = _dot_nt(dhv.astype(BF16), w2[...])
        xhat, rstd = _ln_stats(dc_ref[...])
        ln_g = vec_ref[pl.ds(SM_LN_G, 1), :]
        ln = xhat * ln_g + vec_ref[pl.ds(SM_LN_B, 1), :]
        sl = _sigmoid(ln)
        dln = ds * (sl * (1.0 + ln * (1.0 - sl)))
        acc(SM_LN_G, dln * xhat)
        acc(SM_LN_B, dln)
        dxh = dln * ln_g
        ddc = rstd * (dxh - jnp.mean(dxh, axis=-1, keepdims=True)
                      - xhat * jnp.mean(dxh * xhat, axis=-1, keepdims=True))
        acc(SM_B_BCONV, ddc)
        buf[pl.ds(0, tm), :] = ddc
        _shifted_copies(buf, shf, tm)
        av = a_ref[...].astype(F32)
        sg = _sigmoid(g_ref[...].astype(F32))
        glu_s[...] = av * sg

        _long_conv(buf, shf, wb, dglu_s, tm, [(KB - 1 - j, j) for j in range(KB)], None)
        _long_conv_grad_taps(buf, shf, glu_s, wacc, tm)
        buf[pl.ds(tm, HALO_B), :] = buf[pl.ds(0, HALO_B), :]
        dglu = dglu_s[...]
        da = dglu * sg
        dg = dglu * av * sg * (1.0 - sg)
        sb_ref[pl.ds(0, 1), pl.ds(0, D)] += jnp.sum(da, axis=0, keepdims=True)
        sb_ref[pl.ds(0, 1), pl.ds(D, D)] += jnp.sum(dg, axis=0, keepdims=True)
        dab, dgb = da.astype(BF16), dg.astype(BF16)
        du_ref[:, pl.ds(0, D)] = dab
        du_ref[:, pl.ds(D, D)] = dgb
        dn = _dot(dab, w1[pl.ds(0, D), :]) + _dot(dgb, w1[pl.ds(D, D), :])
        hv = h_ref[...]
        dx, dgam = _rms_bwd(dn, hv, _rms_stat(hv), vec_ref[pl.ds(SM_B_NORM, 1), :])
        o_ref[...] = dhv + dx
        st_ref[pl.ds(SM_B_NORM, 1), :] += dgam

        @pl.when(pl.program_id(0) == nt - 1)
        def _():
            st_ref[pl.ds(SM_B_CONV, KB), :] = jnp.sum(wacc[...], axis=1)

    rs = functools.partial(_row_spec, rev_nt=nt)
    return _hosted_call(
        body, "mixb_bwd", nt,
        in_specs=[rs(tm, D), rs(tm, D), rs(tm, D), rs(tm, D), rs(tm, D), _full_spec((SM_F32_ROWS, D)), _ANY],
        out_specs=[rs(tm, D), rs(tm, 2 * D), _full_spec((SM_F32_ROWS, D)), _full_spec((8, 2 * D))],
        out_shape=(_sds((T, D), F32), _sds((T, 2 * D), BF16), _sds((SM_F32_ROWS, D), F32), _sds((8, 2 * D), F32)),
        scratch=[pltpu.VMEM((2 * D, D), BF16), pltpu.VMEM((D, D), BF16),
                 pltpu.VMEM((tm + HALO_B, D), F32), pltpu.VMEM((7, tm + HALO_B - 8, D), F32),
                 pltpu.VMEM((tm, D), F32), pltpu.VMEM((tm, D), F32), pltpu.VMEM((KB, 8, D), F32),
                 pltpu.VMEM((KB + 1, 8, D), F32), pltpu.SemaphoreType.DMA((2 * NDEV,))],
        args=(dh, h, a, g, dc, vecs, wg), comm=comm)


def _mixa_bwd(dh, x, b, c, v, cc, gam, cw, wg, tm, comm):
    T = x.shape[0]
    nt = T // tm

    def body(dh_ref, x_ref, b_ref, c_ref, v_ref, cc_ref, gam_ref, cw_ref, wg_ref,
             o_ref, db_ref, st_ref, win, wout, buf, sems):
        first = pl.program_id(0) == 0
        load_w = _weight_loads(wg_ref, [(wout, "a_out"), (win, "a_in")], sems)

        @pl.when(first)
        def _():
            load_w()
            buf[pl.ds(tm, HALO_A), :] = jnp.zeros((HALO_A, D), F32)
            st_ref[...] = jnp.zeros((8, D), F32)

        dhv = dh_ref[...]
        dy = _dot_nt(dhv.astype(BF16), wout[...])
        cval = c_ref[...].astype(F32)
        vval = v_ref[...].astype(F32)
        d_b = (dy * cc_ref[...].astype(F32)).astype(BF16)
        buf[pl.ds(0, tm), :] = dy * b_ref[...].astype(F32)
        cv = cval * vval
        dcv = jnp.zeros((tm, D), F32)
        for j in range(KA):
            sh = buf[pl.ds(j, tm), :]
            k = KA - 1 - j
            dcv = dcv + cw_ref[pl.ds(k, 1), :] * sh
            st_ref[pl.ds(1 + k, 1), :] += jnp.sum(cv * sh, axis=0, keepdims=True)
        buf[pl.ds(tm, HALO_A), :] = buf[pl.ds(0, HALO_A), :]
        d_c = (dcv * vval).astype(BF16)
        d_v = (dcv * cval).astype(BF16)
        db_ref[:, pl.ds(0, D)] = d_b
        db_ref[:, pl.ds(D, D)] = d_c
        db_ref[:, pl.ds(2 * D, D)] = d_v
        dn = _dot(d_b, win[pl.ds(0, D), :]) + _dot(d_c, win[pl.ds(D, D), :]) + _dot(d_v, win[pl.ds(2 * D, D), :])
        xv = x_ref[...]
        dx, dgam = _rms_bwd(dn, xv, _rms_stat(xv), gam_ref[...])
        o_ref[...] = dhv + dx
        st_ref[pl.ds(0, 1), :] += dgam

    rs = functools.partial(_row_spec, rev_nt=nt)
    return _hosted_call(
        body, "mixa_bwd", nt,
        in_specs=[rs(tm, D) for _ in range(6)] + [_full_spec((1, D)), _full_spec((KA, D)), _ANY],
        out_specs=[rs(tm, D), rs(tm, 3 * D), _full_spec((8, D))],
        out_shape=(_sds((T, D), F32), _sds((T, 3 * D), BF16), _sds((8, D), F32)),
        scratch=[pltpu.VMEM((3 * D, D), BF16), pltpu.VMEM((D, D), BF16),
                 pltpu.VMEM((tm + HALO_A, D), F32), pltpu.SemaphoreType.DMA((2 * NDEV,))],
        args=(dh, x, b, c, v, cc, gam, cw, wg), comm=comm)


def _sum_small(sm, rp):
    def body(sm_ref, rp_ref, osm, orp, oloss):
        a = sm_ref[0]
        b = rp_ref[0]
        for s in range(1, NDEV):
            a = a + sm_ref[s]
            b = b + rp_ref[s]
        osm[...] = a
        orp[...] = b
        oloss[...] = jnp.zeros((8, 128), F32) + jnp.sum(b[4:5, :], axis=-1, keepdims=True)

    return _pcall(
        body, name="sum_small_grads",
        out_shape=(_sds((SM_F32_ROWS, 128), F32), _sds((8, D), F32), _sds((8, 128), F32)),
        in_specs=[pl.BlockSpec(memory_space=pltpu.VMEM)] * 2,
        out_specs=tuple(pl.BlockSpec(memory_space=pltpu.VMEM) for _ in range(3)),
        compiler_params=_cparams(),
    )(sm, rp)


def _adam_math(w, g, m, v):
    mn = B1 * m + (1.0 - B1) * g
    vn = B2 * v + (1.0 - B2) * (g * g)
    m_hat = mn / (1.0 - B1 ** STEP)
    v_hat = vn / (1.0 - B2 ** STEP)
    return -LR * (m_hat / (jnp.sqrt(v_hat) + ADAM_EPS) + WD * w), mn, vn


def _finish_weight(land, keys, transposed, w, m, v, name, comm=None):
    layers, rows, cols = w.shape
    n = N_ROWS[keys[0]]
    n_pad = -(-n // 128) * 128 if transposed else n

    def body(w_ref, m_ref, v_ref, land_ref, og, od, om, ov, buf, sem):
        off = LD_OFF[keys[0]]
        if layers == 2:
            off = jnp.where(pl.program_id(0) == 0, off, LD_OFF[keys[1]])
        cp = pltpu.make_async_copy(land_ref.at[:, pl.ds(pl.multiple_of(off, 32), n), :], buf, sem)
        cp.start()
        cp.wait()
        g = buf[0].astype(F32)
        for s in range(1, NCHIP):
            g = g + buf[s].astype(F32)
        if transposed:
            if n_pad != n:
                g = jnp.concatenate([g, jnp.zeros((n_pad - n, D), F32)], axis=0)
            g = g.T[:, :n]
        d, mn, vn = _adam_math(w_ref[...], g, m_ref[...], v_ref[...])
        og[...] = g
        od[...] = d
        om[...] = mn
        ov[...] = vn

    spec = pl.BlockSpec((None, rows, cols), lambda l: (l, 0, 0))
    shp = _sds(w.shape, F32)
    in_specs, args = [spec, spec, spec], (w, m, v)
    if comm is None:
        in_specs, args = in_specs + [_ANY], args + (land,)
    return _hosted_call(
        body, "finish_" + name, layers,
        in_specs=in_specs, out_specs=[spec] * 4, out_shape=(shp,) * 4,
        scratch=[pltpu.VMEM((NCHIP, n, D), BF16), pltpu.SemaphoreType.DMA],
        args=args, comm=comm)


SMALL_PARAMS = (("sm", SM_B_NORM, 1), ("sm", SM_B_PW1, 2), ("sm", SM_B_CONV, KB), ("sm", SM_B_BCONV, 1),
                ("sm", SM_LN_G, 1), ("sm", SM_LN_B, 1), ("sm", SM_B_PW2, 1), ("sm", SM_A_CONV, KA),
                ("rp", 0, 1), ("rp", 1, 2), ("rp", 3, 1))


def _adamw_small(g_small, g_repl, triples):
    n = len(SMALL_PARAMS)

    def body(*refs):
        gs_ref, gr_ref = refs[0], refs[1]
        ins = refs[2:2 + 3 * n]
        outs = refs[2 + 3 * n:]
        for i, (src, r0, nr) in enumerate(SMALL_PARAMS):
            w_ref, m_ref, v_ref = ins[3 * i:3 * i + 3]
            g = (gs_ref if src == "sm" else gr_ref)[pl.ds(r0, nr), :]
            if i == 1:
                g = jnp.concatenate([g[0:1], g[1:2]], axis=1)
            lead = (0,) if len(w_ref.shape) == 3 else ()
            idx = lead + (slice(None), slice(None))
            vals = (g,) + _adam_math(w_ref[idx], g, m_ref[idx], v_ref[idx])
            for o_ref, val in zip(outs[4 * i:4 * i + 4], vals):
                o_ref[idx] = val

    flat = [a for t in triples for a in t]
    out_shape = tuple(_sds(t[0].shape, F32) for t in triples for _ in range(4))
    vm = pl.BlockSpec(memory_space=pltpu.VMEM)
    res = _pcall(
        body, name="adamw_small", out_shape=out_shape,
        in_specs=[vm] * (2 + len(flat)), out_specs=tuple(vm for _ in out_shape),
        compiler_params=_cparams(),
    )(g_small, g_repl, *flat)
    return [tuple(res[4 * i:4 * i + 4]) for i in range(n)]


def kernel(x, a_norm, a_w_in, a_conv, a_w_out, b_norm, b_w_pw1, b_b_pw1, b_conv, b_b_conv, b_ln_g, b_ln_b, b_w_pw2, b_b_pw2, ffn_norm, ffn_w_gate, ffn_w_up, ffn_w_down, final_norm, loss_target, m_a_norm, m_a_w_in, m_a_conv, m_a_w_out, m_b_norm, m_b_w_pw1, m_b_b_pw1, m_b_conv, m_b_b_conv, m_b_ln_g, m_b_ln_b, m_b_w_pw2, m_b_b_pw2, m_ffn_norm, m_ffn_w_gate, m_ffn_w_up, m_ffn_w_down, m_final_norm, v_a_norm, v_a_w_in, v_a_conv, v_a_w_out, v_b_norm, v_b_w_pw1, v_b_b_pw1, v_b_conv, v_b_b_conv, v_b_ln_g, v_b_ln_b, v_b_w_pw2, v_b_b_pw2, v_ffn_norm, v_ffn_w_gate, v_ffn_w_up, v_ffn_w_down, v_final_norm):
    T = x.shape[1]
    tm = min(TM, T)
    tma = min(TM_A, T)
    tw = min(TM_DW, T)
    xs = x.reshape(T, D)
    tgt = loss_target.reshape(T, D)

    def rows_first(a):
        return jnp.swapaxes(a, 1, 2)

    shard, small = _pack_shard(
        (a_w_in, a_w_out, b_w_pw1, b_w_pw2, rows_first(ffn_w_gate), rows_first(ffn_w_up), ffn_w_down),
        (b_norm, b_b_pw1, b_conv, b_b_conv, b_ln_g, b_ln_b, b_b_pw2, a_conv))
    wg, sm_all = _all_gather_first(shard, small)
    vecs = sm_all.transpose(1, 0, 2).reshape(SM_F32_ROWS, D)
    bias1 = sm_all[:, SM_B_PW1:SM_B_PW1 + 2, :].reshape(1, 2 * D)
    cw_a = vecs[SM_A_CONV:SM_A_CONV + KA]
    fn0, fn1 = ffn_norm[0:1], ffn_norm[1:2]
    fin = final_norm.reshape(1, D)

    h1, n0, bq, cq, vq, ccq, yq, wg = _mixa_fwd(xs, a_norm, cw_a, wg, tma, _ag_comm(shard, wg, ("g0", "u0", "d0")))
    h2, n1, g0, u0, gu0, wg = _ffn_fwd(h1, fn0, wg, 0, tma, _ag_comm(shard, wg, ("b_pw1", "b_pw2"), (T // tma) // 2))
    h3, n2, aq, gq, dcq, sq, wg = _mixb_fwd(h2, vecs, bias1, wg, tm, _ag_comm(shard, wg, ("g1", "u1", "d1"), 1))
    dh4, n3, g1, u1, gu1, st_fin = _ffn_fwd(h3, fn1, wg, 1, tma, None, head=(tgt, fin))

    def by_dest(gw, key):
        return gw.reshape(NDEV, N_ROWS[key], D)

    gw_d1 = by_dest(_grad_w(gu1, dh4, FH, "grad_down1", tw), "d1")
    dh3, dg1, du1, st_f1 = _ffn_bwd_dx(dh4, h3, g1, u1, fn1, wg, 1, tma, None)
    gw_g1 = by_dest(_grad_w(dg1, n3, FH, "grad_gate1", tw), "g1")
    gw_u1 = by_dest(_grad_w(du1, n3, FH, "grad_up1", tw), "u1")
    keys = ("g1", "u1", "d1")
    dh2, dub, st_b, st_b1, land = _mixb_bwd(
        dh3, h2, aq, gq, dcq, vecs, wg, tm, _rs_comm(_pair_reduce([gw_g1, gw_u1, gw_d1], keys), keys, None))
    gw_pw1 = by_dest(_grad_w(dub, n2, D, "grad_pw1", tw), "b_pw1")
    gw_pw2 = by_dest(_grad_w(sq, dh3, D, "grad_pw2", tw), "b_pw2")
    gw_d0 = by_dest(_grad_w(gu0, dh2, FH, "grad_down0", tw), "d0")
    keys = ("b_pw1", "b_pw2", "d0")
    dh1, dg0, du0, st_f0, land = _ffn_bwd_dx(
        dh2, h1, g0, u0, fn0, wg, 0, tma, _rs_comm(_pair_reduce([gw_pw1, gw_pw2, gw_d0], keys), keys, land))
    gw_g0 = by_dest(_grad_w(dg0, n1, FH, "grad_gate0", tw), "g0")
    gw_u0 = by_dest(_grad_w(du0, n1, FH, "grad_up0", tw), "u0")
    keys = ("g0", "u0")
    dx, dbcv, st_a, land = _mixa_bwd(
        dh1, xs, bq, cq, vq, ccq, a_norm, cw_a, wg, tma,
        _rs_comm(_pair_reduce([gw_g0, gw_u0], keys), keys, land))
    gw_out = by_dest(_grad_w(yq, dh1, D, "grad_out", tw), "a_out")
    gw_in = by_dest(_grad_w(dbcv, n0, D, "grad_in", tw), "a_in")

    st_small = st_b.at[SM_A_CONV:SM_A_CONV + KA].set(st_a[1:1 + KA])
    sm_dest = st_small.reshape(SM_F32_ROWS, NDEV, 128).transpose(1, 0, 2)
    sm_dest = sm_dest.at[:, SM_B_PW1:SM_B_PW1 + 2, :].set(st_b1[0].reshape(NDEV, 2, 128))
    repl = jnp.concatenate([st_a[0:1], st_f0[0:1], st_f1[0:1], st_fin[0:1], st_fin[1:2],
                            jnp.zeros((3, D), F32)], axis=0)[None]
    last = ("a_out", "a_in")
    p_last, sm_land, rp_land = _pair_reduce([gw_out, gw_in], last, side=(sm_dest, repl))

    big = [("ffn_w_gate", ("g0", "g1"), False,
            rows_first(ffn_w_gate), rows_first(m_ffn_w_gate), rows_first(v_ffn_w_gate)),
           ("ffn_w_up", ("u0", "u1"), False, rows_first(ffn_w_up), rows_first(m_ffn_w_up), rows_first(v_ffn_w_up)),
           ("ffn_w_down", ("d0", "d1"), False, ffn_w_down, m_ffn_w_down, v_ffn_w_down),
           ("a_w_out", ("a_out",), False, a_w_out, m_a_w_out, v_a_w_out),
           ("b_w_pw1", ("b_pw1",), True, b_w_pw1, m_b_w_pw1, v_b_w_pw1),
           ("b_w_pw2", ("b_pw2",), False, b_w_pw2, m_b_w_pw2, v_b_w_pw2),
           ("a_w_in", ("a_in",), True, a_w_in, m_a_w_in, v_a_w_in)]
    res = {}
    for k, (name, keys, transposed, w, m, v) in enumerate(big):
        if k < 3:
            comm = _rs_comm(p_last, last, land, part=((0, 176), (176, 176), (352, 160))[k], body_reads_land=True)
            *res[name], land = _finish_weight(land, keys, transposed, w, m, v, name, comm)
        else:
            res[name] = _finish_weight(land, keys, transposed, w, m, v, name)
    for name in ("ffn_w_gate", "ffn_w_up"):
        res[name] = [rows_first(a) for a in res[name]]

    g_small, g_repl, loss8 = _sum_small(sm_land, rp_land)
    loss = loss8[0, 0]

    small_names = ["b_norm", "b_b_pw1", "b_conv", "b_b_conv", "b_ln_g", "b_ln_b", "b_b_pw2", "a_conv",
                   "a_norm", "ffn_norm", "final_norm"]
    triples = [(b_norm, m_b_norm, v_b_norm), (b_b_pw1, m_b_b_pw1, v_b_b_pw1), (b_conv, m_b_conv, v_b_conv),
               (b_b_conv, m_b_b_conv, v_b_b_conv), (b_ln_g, m_b_ln_g, v_b_ln_g), (b_ln_b, m_b_ln_b, v_b_ln_b),
               (b_b_pw2, m_b_b_pw2, v_b_b_pw2), (a_conv, m_a_conv, v_a_conv), (a_norm, m_a_norm, v_a_norm),
               (ffn_norm, m_ffn_norm, v_ffn_norm),
               (fin, m_final_norm.reshape(1, D), v_final_norm.reshape(1, D))]
    for name, quad in zip(small_names, _adamw_small(g_small, g_repl, triples)):
        res[name] = quad
    res["final_norm"] = tuple(a.reshape(D) for a in res["final_norm"])

    order = ["a_norm", "a_w_in", "a_conv", "a_w_out", "b_norm", "b_w_pw1", "b_b_pw1", "b_conv", "b_b_conv",
             "b_ln_g", "b_ln_b", "b_w_pw2", "b_b_pw2", "ffn_norm", "ffn_w_gate", "ffn_w_up", "ffn_w_down",
             "final_norm"]
    out = [loss, dx.reshape(1, T, D)]
    for j in range(4):
        out += [res[k][j] for k in order]
    return tuple(out)
```

```python
import functools

import jax
import jax.numpy as jnp
from jax import lax
from jax.experimental import pallas as pl
from jax.experimental.pallas import tpu as pltpu

F32 = jnp.float32
BF16 = jnp.bfloat16
MESH = pl.DeviceIdType.MESH

D = 1024
FF = 2816
FH = FF // 2
NDEV = 8
KA = 3
KB = 31
HALO_A = 8
HALO_B = 32
CONV_ROWS = 64
RMS_EPS = 1e-6
LN_EPS = 1e-5
LR, B1, B2, ADAM_EPS, WD, STEP = 0.001, 0.9, 0.999, 1e-08, 0.01, 10

TM = 256
TM_A = 512
TM_DW = 1024
VMEM_LIMIT = 62 * 1024 * 1024

N_ROWS = {"a_in": 384, "a_out": 128, "b_pw1": 256, "b_pw2": 128, "g0": 352, "u0": 352, "d0": 352,
          "g1": 352, "u1": 352, "d1": 352}
WEIGHT_KEYS = ("a_in", "a_out", "b_pw1", "b_pw2", "g0", "u0", "d0", "g1", "u1", "d1")
NCHIP = 4


def _offsets(order):
    off, o = {}, 0
    for k in order:
        off[k] = o
        o += N_ROWS[k]
    return off, o


PK_ORDER = ("a_in", "a_out", "g0", "u0", "d0", "b_pw1", "b_pw2", "g1", "u1", "d1")
PK_OFF, PK_ROWS = _offsets(PK_ORDER)
LD_ORDER = ("g1", "u1", "d1", "b_pw1", "b_pw2", "d0", "g0", "u0", "a_out", "a_in")
LD_OFF, N_WROWS = _offsets(LD_ORDER)


def _span(off, keys):
    return off[keys[0]], sum(N_ROWS[k] for k in keys)


SM_B_NORM, SM_B_PW1, SM_B_CONV, SM_B_BCONV, SM_LN_G, SM_LN_B, SM_B_PW2, SM_A_CONV = 0, 1, 3, 34, 35, 36, 37, 38
SM_USED = 41
SM_F32_ROWS = 64


def _pcall(body, **kw):
    return pl.pallas_call(body, **kw)


def _cparams(sem=None):
    return pltpu.CompilerParams(dimension_semantics=sem, vmem_limit_bytes=VMEM_LIMIT)


def _dot(a, b):
    return jnp.dot(a, b, preferred_element_type=F32)


def _dot_nt(a, b):
    return lax.dot_general(a, b, (((1,), (1,)), ((), ())), preferred_element_type=F32)


def _dot_tn(a, b):
    return lax.dot_general(a, b, (((0,), (0,)), ((), ())), preferred_element_type=F32)


def _sigmoid(v):
    return 1.0 / (1.0 + jnp.exp(-v))


def _rms_stat(x):
    return lax.rsqrt(jnp.mean(x * x, axis=-1, keepdims=True) + RMS_EPS)


def _rms_bwd(dn, x, r, gamma):
    dng = dn * gamma
    dx = r * dng - x * (r * r * r) * jnp.mean(dng * x, axis=-1, keepdims=True)
    return dx, jnp.sum(dn * x * r, axis=0, keepdims=True)


def _weight_loads(wg_ref, plan, sems):
    copies = []
    for j, (dst, key) in enumerate(plan):
        off, n = PK_OFF[key], N_ROWS[key]
        copies += [pltpu.make_async_copy(
            wg_ref.at[d, pl.ds(off, n), :], dst.at[pl.ds(d * n, n), :], sems.at[j * NDEV + d])
            for d in range(NDEV)]

    def load():
        for cp in copies:
            cp.start()
        for cp in copies:
            cp.wait()

    return load


_ANY = pl.BlockSpec(memory_space=pl.ANY)


def _row_spec(tm, width, rev_nt=None):
    if rev_nt is None:
        return pl.BlockSpec((tm, width), lambda i: (i, 0))
    return pl.BlockSpec((tm, width), lambda i: (rev_nt - 1 - i, 0))


def _full_spec(shape):
    return pl.BlockSpec(shape, lambda *_: (0,) * len(shape))


def _sds(shape, dtype):
    return jax.ShapeDtypeStruct(shape, dtype)


def _mesh_pos():
    return lax.axis_index("x"), lax.axis_index("y"), lax.axis_index("c")


def _lin(p):
    return 4 * p[0] + 2 * p[1] + p[2]


def _ag_exchange(src, slot, send_sems, recv_sems, local_sem):
    x, y, c = _mesh_pos()
    me, sibling = (x, y, c), (x, y, 1 - c)
    chips = [(1 - x, y), (x, 1 - y), (1 - x, 1 - y)]

    def copy(k, block, to, own=False):
        return pltpu.make_async_remote_copy(
            src_ref=src if own else slot(block), dst_ref=slot(block),
            send_sem=send_sems.at[k], recv_sem=recv_sems.at[k], device_id=to, device_id_type=MESH)

    mine = pltpu.make_async_copy(src, slot(me), local_sem)
    first = [copy(0, me, sibling, own=True)]
    first += [copy(1 + j, me, (*chip, c), own=True) for j, chip in enumerate(chips)]
    passed = [copy(4 + j, (*chip, c), sibling) for j, chip in enumerate(chips)]

    def start():
        mine.start()
        for cp in first:
            cp.start()

    def forward():
        for j, chip in enumerate(chips):
            copy(1 + j, (*chip, c), me).wait_recv()
            passed[j].start()

    def finish():
        copy(0, sibling, me).wait_recv()
        for j, chip in enumerate(chips):
            copy(4 + j, (*chip, 1 - c), me).wait_recv()
        for cp in first + passed:
            cp.wait_send()
        mine.wait()

    return start, forward, finish


def _chip_exchange(p_ref, land_ref, send_sems, recv_sems, keys, part):
    x, y, c = _mesh_pos()
    l0, rows = _span(LD_OFF, keys)
    p0, rows = (0, rows) if part is None else part
    dst = land_ref.at[2 * x + y, pl.ds(l0 + p0, rows), :]
    peers = [((x + (j >> 1)) % 2, (y + (j & 1)) % 2) for j in range(NCHIP)]

    def copy(j):
        tx, ty = peers[j]
        src = p_ref.at[2 * tx + ty, pl.ds(p0, rows), :]
        if j == 0:
            return pltpu.make_async_copy(src, dst, send_sems.at[0])
        return pltpu.make_async_remote_copy(
            src_ref=src, dst_ref=dst, send_sem=send_sems.at[j], recv_sem=recv_sems.at[j],
            device_id=(tx, ty, c), device_id_type=MESH)

    def start():
        for j in range(NCHIP):
            copy(j).start()

    def finish():
        for j in range(NCHIP):
            copy(j).wait()

    return start, finish


def _all_to_all_f32(src_for, dst_ref, send_sems, recv_sems):
    x, y, c = _mesh_pos()
    dst = dst_ref.at[_lin((x, y, c))]
    peers = [((x + ((j >> 2) & 1)) % 2, (y + ((j >> 1) & 1)) % 2, (c + (j & 1)) % 2) for j in range(NDEV)]

    def copy(j):
        if j == 0:
            return pltpu.make_async_copy(src_for(_lin(peers[0])), dst, send_sems.at[0])
        return pltpu.make_async_remote_copy(
            src_ref=src_for(_lin(peers[j])), dst_ref=dst, send_sem=send_sems.at[j], recv_sem=recv_sems.at[j],
            device_id=peers[j], device_id_type=MESH)

    def start():
        for j in range(NDEV):
            copy(j).start()

    def finish():
        for j in range(NDEV):
            copy(j).wait()

    return start, finish


class _Comm:
    def __init__(self, ins, alias_in, out_shape, scratch, make, gives_wg, middle_step=0):
        self.ins, self.alias_in, self.out_shape = ins, alias_in, out_shape
        self.scratch, self.make, self.gives_wg, self.middle_step = scratch, make, gives_wg, middle_step


def _ag_comm(shard, wg, keys, forward_step=0):
    def make(c_ins, c_out, sc):
        return _weights_exchange(c_ins[0], c_out, sc[0], sc[1], sc[2], keys)

    return _Comm([shard, wg], 1, _sds(wg.shape, BF16),
                 [pltpu.SemaphoreType.DMA((7,)), pltpu.SemaphoreType.DMA((7,)), pltpu.SemaphoreType.DMA],
                 make, True, forward_step)


def _weights_exchange(shard_ref, wg_ref, send_sems, recv_sems, local_sem, keys):
    r0, nr = _span(PK_OFF, keys)
    return _ag_exchange(shard_ref.at[pl.ds(r0, nr), :], lambda p: wg_ref.at[_lin(p), pl.ds(r0, nr), :],
                        send_sems, recv_sems, local_sem)


def _rs_comm(psum, keys, land, part=None, body_reads_land=False):
    def make(c_ins, c_out, sc):
        start, finish = _chip_exchange(c_ins[0], c_out, sc[0], sc[1], keys, part)
        return start, None, finish

    ins = [psum] + ([] if land is None else [land])
    return _Comm(ins, None if land is None else 1, _sds((NCHIP, N_WROWS, D), BF16),
                 [pltpu.SemaphoreType.DMA((NCHIP,)), pltpu.SemaphoreType.DMA((NCHIP,))], make, body_reads_land)


def _hosted_call(body, name, nt, in_specs, out_specs, out_shape, scratch, args, comm):
    if comm is None:
        return _pcall(body, name=name, grid=(nt,), in_specs=in_specs, out_specs=tuple(out_specs),
                      out_shape=tuple(out_shape), scratch_shapes=scratch,
                      compiler_params=_cparams(("arbitrary",)))(*args)
    n_in, n_out, n_sc, n_cin = len(in_specs), len(out_specs), len(scratch), len(comm.ins)

    def wrapped(*refs):
        ins = refs[:n_in]
        c_ins = refs[n_in:n_in + n_cin]
        outs = refs[n_in + n_cin:n_in + n_cin + n_out]
        c_out = refs[n_in + n_cin + n_out]
        sc = refs[n_in + n_cin + n_out + 1:n_in + n_cin + n_out + 1 + n_sc]
        c_sc = refs[n_in + n_cin + n_out + 1 + n_sc:]
        start, middle, finish = comm.make(c_ins, c_out, c_sc)
        pl.when(pl.program_id(0) == 0)(start)
        if comm.gives_wg:
            body(*ins, c_out, *outs, *sc)
        else:
            body(*ins, *outs, *sc)
        if middle is not None:
            pl.when(pl.program_id(0) == max(nt - 1 - comm.middle_step, 0))(middle)
        pl.when(pl.program_id(0) == nt - 1)(finish)

    aliases = {} if comm.alias_in is None else {n_in + comm.alias_in: n_out}
    res = _pcall(wrapped, name=name, grid=(nt,),
                 in_specs=list(in_specs) + [_ANY] * n_cin, out_specs=tuple(out_specs) + (_ANY,),
                 out_shape=tuple(out_shape) + (comm.out_shape,),
                 scratch_shapes=list(scratch) + list(comm.scratch),
                 input_output_aliases=aliases,
                 compiler_params=_cparams(("arbitrary",)))(*args, *comm.ins)
    return res


def _pack_shard(weights, smalls):
    plan = (("a_in", 0, 0, True), ("a_out", 1, 0, False), ("b_pw1", 2, 0, True), ("b_pw2", 3, 0, False),
            ("g0", 4, 0, False), ("u0", 5, 0, False), ("d0", 6, 0, False),
            ("g1", 4, 1, False), ("u1", 5, 1, False), ("d1", 6, 1, False))
    n_pad = 384

    def body(*refs):
        out, sm, pad = refs[-3:]
        for key, idx, layer, transposed in plan:
            n = N_ROWS[key]
            val = refs[idx][layer]
            if transposed:
                if n % 128:
                    pad[:, pl.ds(0, n)] = val
                    pad[:, pl.ds(n, n_pad - n)] = jnp.zeros((D, n_pad - n), F32)
                    val = pad[...]
                val = val.T[:n]
            out[pl.ds(PK_OFF[key], n), :] = val.astype(BF16)
        b_norm, b_b_pw1, b_conv, b_b_conv, b_ln_g, b_ln_b, b_b_pw2, a_conv = refs[len(weights):-3]
        sm[...] = jnp.zeros((SM_F32_ROWS, 128), F32)
        sm[pl.ds(SM_B_PW1, 1), :] = b_b_pw1[:, pl.ds(0, 128)]
        sm[pl.ds(SM_B_PW1 + 1, 1), :] = b_b_pw1[:, pl.ds(128, 128)]
        sm[pl.ds(SM_B_CONV, KB), :] = b_conv[0]
        sm[pl.ds(SM_A_CONV, KA), :] = a_conv[0]
        for row, ref in ((SM_B_NORM, b_norm), (SM_B_BCONV, b_b_conv), (SM_LN_G, b_ln_g), (SM_LN_B, b_ln_b),
                         (SM_B_PW2, b_b_pw2)):
            sm[pl.ds(row, 1), :] = ref[...]

    vm = pl.BlockSpec(memory_space=pltpu.VMEM)
    return _pcall(
        body, name="pack_shard",
        out_shape=(_sds((PK_ROWS, D), BF16), _sds((SM_F32_ROWS, 128), F32)),
        in_specs=[vm] * (len(weights) + len(smalls)), out_specs=(vm, vm),
        scratch_shapes=[pltpu.VMEM((D, n_pad), F32)],
        compiler_params=_cparams(),
    )(*weights, *smalls)


def _all_gather_first(shard, small):
    def body(x_ref, s_ref, wg_ref, sg_ref, send_w, recv_w, local_w, send_s, recv_s, local_s):
        start_w, forward_w, finish_w = _weights_exchange(x_ref, wg_ref, send_w, recv_w, local_w, ("a_in", "a_out"))
        start_s, forward_s, finish_s = _ag_exchange(s_ref, lambda p: sg_ref.at[_lin(p)], send_s, recv_s, local_s)
        start_s()
        start_w()
        forward_s()
        forward_w()
        finish_s()
        finish_w()

    sems = [pltpu.SemaphoreType.DMA((7,)), pltpu.SemaphoreType.DMA((7,)), pltpu.SemaphoreType.DMA]
    return _pcall(
        body, name="all_gather_first",
        out_shape=(_sds((NDEV, PK_ROWS, D), BF16), _sds((NDEV, SM_F32_ROWS, 128), F32)),
        in_specs=[_ANY, _ANY], out_specs=(_ANY, _ANY), scratch_shapes=sems + sems,
    )(shard, small)


def _pair_reduce(grads, keys, side=None):
    l0, rows = _span(LD_OFF, keys)
    nk = len(keys)
    n_side = 0 if side is None else 2

    def body(c_ref, *refs):
        mine = refs[:nk]
        whole = refs[nk:2 * nk]
        o_ref = refs[2 * nk + n_side]
        got, send_sem, recv_sems = refs[2 * nk + 2 * n_side + 1:2 * nk + 2 * n_side + 4]
        t = pl.program_id(0)
        x, y, c = _mesh_pos()
        sibling = (x, y, 1 - c)
        if side is not None:
            sm_ref, rp_ref = refs[2 * nk:2 * nk + 2]
            sm_land, rp_land = refs[2 * nk + 3:2 * nk + 5]
            s2, r2, s3, r3 = refs[-4:]
            start_s, finish_s = _all_to_all_f32(lambda d: sm_ref.at[d], sm_land, s2, r2)
            start_r, finish_r = _all_to_all_f32(lambda d: rp_ref.at[0], rp_land, s3, r3)

        @pl.when(t == 0)
        def _():
            if side is not None:
                start_s()
                start_r()
            for ref, key in zip(whole, keys):
                for d in range(NCHIP):
                    pltpu.make_async_remote_copy(
                        src_ref=ref.at[2 * d + (1 - c)], dst_ref=got.at[d, pl.ds(LD_OFF[key] - l0, N_ROWS[key]), :],
                        send_sem=send_sem, recv_sem=recv_sems.at[d], device_id=sibling, device_id_type=MESH).start()

        pltpu.make_async_remote_copy(src_ref=got.at[t], dst_ref=got.at[t], send_sem=send_sem,
                                     recv_sem=recv_sems.at[t], device_id=sibling, device_id_type=MESH).wait_recv()
        for ref, key in zip(mine, keys):
            sl = pl.ds(LD_OFF[key] - l0, N_ROWS[key])
            o_ref[sl, :] = (ref[...].astype(F32) + got[t, sl, :].astype(F32)).astype(BF16)

        @pl.when(t == NCHIP - 1)
        def _():
            pltpu.make_async_remote_copy(src_ref=got, dst_ref=got, send_sem=send_sem, recv_sem=recv_sems.at[0],
                                         device_id=sibling, device_id_type=MESH).wait_send()
            if side is not None:
                finish_s()
                finish_r()

    out_specs = [pl.BlockSpec((None, rows, D), lambda t, c: (t, 0, 0))]
    out_shape = [_sds((NCHIP, rows, D), BF16)]
    scratch = [pltpu.VMEM((NCHIP, rows, D), BF16), pltpu.SemaphoreType.DMA, pltpu.SemaphoreType.DMA((NCHIP,))]
    if side is not None:
        out_specs += [_ANY, _ANY]
        out_shape += [_sds((NDEV, SM_F32_ROWS, 128), F32), _sds((NDEV, 8, D), F32)]
        scratch += [pltpu.SemaphoreType.DMA((NDEV,))] * 4
    grid_spec = pltpu.PrefetchScalarGridSpec(
        num_scalar_prefetch=1, grid=(NCHIP,),
        in_specs=[pl.BlockSpec((None, N_ROWS[k], D), lambda t, c: (2 * t + c[0], 0, 0)) for k in keys]
        + [_ANY] * (nk + n_side),
        out_specs=tuple(out_specs), scratch_shapes=scratch)
    core = lax.axis_index("c").astype(jnp.int32).reshape(1)
    res = _pcall(
        body, name="pair_reduce_" + keys[0], grid_spec=grid_spec, out_shape=tuple(out_shape),
        compiler_params=_cparams(("arbitrary",)),
    )(core, *grads, *grads, *(side or ()))
    return res[0] if side is None else res


def _mixa_fwd(x, gam, cw, wg, tm, comm):
    T = x.shape[0]

    def body(x_ref, gam_ref, cw_ref, wg_ref, h_ref, n_ref, b_ref, c_ref, v_ref, cc_ref, y_ref,
             win, wout, buf, sems):
        first = pl.program_id(0) == 0
        load_w = _weight_loads(wg_ref, [(win, "a_in"), (wout, "a_out")], sems)

        @pl.when(first)
        def _():
            load_w()
            buf[pl.ds(0, HALO_A), :] = jnp.zeros((HALO_A, D), F32)

        xv = x_ref[...]
        nb = (xv * _rms_stat(xv) * gam_ref[...]).astype(BF16)
        n_ref[...] = nb
        bv = _dot_nt(nb, win[pl.ds(0, D), :])
        cval = _dot_nt(nb, win[pl.ds(D, D), :])
        vval = _dot_nt(nb, win[pl.ds(2 * D, D), :])
        cv = cval * vval
        buf[pl.ds(HALO_A, tm), :] = cv
        cc = cw_ref[pl.ds(KA - 1, 1), :] * cv
        for k in range(KA - 1):
            cc = cc + cw_ref[pl.ds(k, 1), :] * buf[pl.ds(HALO_A - (KA - 1) + k, tm), :]
        buf[pl.ds(0, HALO_A), :] = buf[pl.ds(tm, HALO_A), :]
        yb = (bv * cc).astype(BF16)
        b_ref[...] = bv.astype(BF16)
        c_ref[...] = cval.astype(BF16)
        v_ref[...] = vval.astype(BF16)
        cc_ref[...] = cc.astype(BF16)
        y_ref[...] = yb
        h_ref[...] = xv + _dot(yb, wout[...])

    act = _sds((T, D), BF16)
    return _hosted_call(
        body, "mixa_fwd", T // tm,
        in_specs=[_row_spec(tm, D), _full_spec((1, D)), _full_spec((KA, D))],
        out_specs=[_row_spec(tm, D) for _ in range(7)],
        out_shape=(_sds((T, D), F32),) + (act,) * 6,
        scratch=[pltpu.VMEM((3 * D, D), BF16), pltpu.VMEM((D, D), BF16),
                 pltpu.VMEM((HALO_A + tm, D), F32), pltpu.SemaphoreType.DMA((2 * NDEV,))],
        args=(x, gam, cw), comm=comm)


def _ffn_fwd(h, gam, wg, layer, tm, comm, head=None):
    T = h.shape[0]
    kg, ku, kd = "g%d" % layer, "u%d" % layer, "d%d" % layer
    n_head = 0 if head is None else 2

    def body(*refs):
        h_ref, gam_ref = refs[:2]
        wg_ref = refs[2 + n_head]
        o_ref, n_ref, g_ref, u_ref, gu_ref = refs[3 + n_head:8 + n_head]
        wgt, wut, wd, sems = refs[-4:]

        first = pl.program_id(0) == 0
        load_w = _weight_loads(wg_ref, [(wgt, kg), (wut, ku), (wd, kd)], sems)
        pl.when(first)(load_w)

        hv = h_ref[...]
        nb = (hv * _rms_stat(hv) * gam_ref[...]).astype(BF16)
        n_ref[...] = nb
        out = hv
        for f in range(2):
            cols = pl.ds(f * FH, FH)
            g = _dot_nt(nb, wgt[cols, :])
            u = _dot_nt(nb, wut[cols, :])
            gu = (g * _sigmoid(g) * u).astype(BF16)
            g_ref[:, cols] = g.astype(BF16)
            u_ref[:, cols] = u.astype(BF16)
            gu_ref[:, cols] = gu
            out = out + _dot(gu, wd[cols, :])
        if head is None:
            o_ref[...] = out
        else:
            t_ref, fin_ref, st_ref = refs[2], refs[3], refs[8 + n_head]

            @pl.when(pl.program_id(0) == 0)
            def _():
                st_ref[...] = jnp.zeros((8, D), F32)

            gamma = fin_ref[...]
            r = _rms_stat(out)
            err = out * r * gamma - t_ref[...]
            dx, dgam = _rms_bwd(err * (1.0 / D), out, r, gamma)
            o_ref[...] = dx
            st_ref[pl.ds(0, 1), :] += dgam
            st_ref[pl.ds(1, 1), :] += (0.5 / D) * jnp.sum(err * err, axis=0, keepdims=True)

    pre = _sds((T, FF), BF16)
    in_specs = [_row_spec(tm, D), _full_spec((1, D))]
    args = (h, gam)
    out_specs = [_row_spec(tm, D), _row_spec(tm, D), _row_spec(tm, FF), _row_spec(tm, FF), _row_spec(tm, FF)]
    out_shape = (_sds((T, D), F32), _sds((T, D), BF16), pre, pre, pre)
    if head is not None:
        in_specs, args = in_specs + [_row_spec(tm, D), _full_spec((1, D))], args + tuple(head)
        out_specs, out_shape = out_specs + [_full_spec((8, D))], out_shape + (_sds((8, D), F32),)
    if comm is None:
        in_specs, args = in_specs + [_ANY], args + (wg,)
    return _hosted_call(
        body, "ffn%d_fwd" % layer, T // tm,
        in_specs=in_specs, out_specs=out_specs, out_shape=out_shape,
        scratch=[pltpu.VMEM((FF, D), BF16)] * 3 + [pltpu.SemaphoreType.DMA((3 * NDEV,))],
        args=args, comm=comm)


def _shifted_copies(buf, shf, tm):
    for r in range(1, 8):
        shf[r - 1] = buf[pl.ds(r, tm + HALO_B - 8), :]


def _broadcast_taps(vec_ref, wb):
    for k in range(KB):
        wb[k] = jnp.broadcast_to(vec_ref[pl.ds(SM_B_CONV + k, 1), :], (8, D))
    wb[KB] = jnp.broadcast_to(vec_ref[pl.ds(SM_B_BCONV, 1), :], (8, D))


def _taps_by_shift_residue(taps):
    groups = {}
    for k, shift in taps:
        q, r = divmod(shift, 8)
        groups.setdefault(r, []).append((k, q))
    return sorted(groups.items())


def _window(buf, shf, base, r, q0, n_groups, lanes):
    rows = pl.ds(base + 8 * q0, 8 * n_groups)
    v = buf[rows, lanes] if r == 0 else shf[r - 1, rows, lanes]
    return [v[8 * i:8 * i + 8] for i in range(n_groups)]


def _long_conv(buf, shf, wb, out_ref, tm, taps, bias_row):
    n_acc = CONV_ROWS // 8
    groups = _taps_by_shift_residue(taps)
    for col in range(D // 128):
        lanes = pl.ds(128 * col, 128)

        def rows(i, carry, lanes=lanes):
            base = i * CONV_ROWS
            init = jnp.zeros((8, 128), F32) if bias_row is None else wb[bias_row, :, lanes]
            accs = [init] * n_acc
            for r, lst in groups:
                q0, q1 = min(q for _, q in lst), max(q for _, q in lst)
                win = _window(buf, shf, base, r, q0, n_acc + q1 - q0, lanes)
                for k, q in lst:
                    wk = wb[k, :, lanes]
                    accs = [acc + wk * win[h + q - q0] for h, acc in enumerate(accs)]
            for h, acc in enumerate(accs):
                out_ref[pl.ds(base + 8 * h, 8), lanes] = acc
            return carry

        for i in range(tm // CONV_ROWS):
            rows(i, 0)


def _long_conv_grad_taps(buf, shf, x_ref, wacc, tm):
    n_acc = CONV_ROWS // 8
    groups = _taps_by_shift_residue([(KB - 1 - j, j) for j in range(KB)])
    for col in range(D // 128):
        lanes = pl.ds(128 * col, 128)

        def rows(i, carry, lanes=lanes):
            base = i * CONV_ROWS
            xv = x_ref[pl.ds(base, CONV_ROWS), lanes]
            xs = [xv[8 * h:8 * h + 8] for h in range(n_acc)]
            for r, lst in groups:
                q0, q1 = min(q for _, q in lst), max(q for _, q in lst)
                win = _window(buf, shf, base, r, q0, n_acc + q1 - q0, lanes)
                for k, q in lst:
                    prod = [x * win[h + q - q0] for h, x in enumerate(xs)]
                    wacc[k, :, lanes] += functools.reduce(lambda a, b: a + b, prod)
            return carry

        for i in range(tm // CONV_ROWS):
            rows(i, 0)


def _ln_stats(dc):
    mu = jnp.mean(dc, axis=-1, keepdims=True)
    xc = dc - mu
    rstd = lax.rsqrt(jnp.mean(xc * xc, axis=-1, keepdims=True) + LN_EPS)
    return xc * rstd, rstd


def _mixb_fwd(h, vecs, bias1, wg, tm, comm):
    T = h.shape[0]

    def body(h_ref, vec_ref, b1_ref, wg_ref, o_ref, n_ref, a_ref, g_ref, dc_ref, s_ref, w1, w2, buf, shf, wb, sems):
        first = pl.program_id(0) == 0
        load_w = _weight_loads(wg_ref, [(w1, "b_pw1"), (w2, "b_pw2")], sems)

        @pl.when(first)
        def _():
            load_w()
            buf[pl.ds(0, HALO_B), :] = jnp.zeros((HALO_B, D), F32)
            _broadcast_taps(vec_ref, wb)

        hv = h_ref[...]
        nb = (hv * _rms_stat(hv) * vec_ref[pl.ds(SM_B_NORM, 1), :]).astype(BF16)
        n_ref[...] = nb
        a = _dot_nt(nb, w1[pl.ds(0, D), :]) + b1_ref[:, pl.ds(0, D)]
        g = _dot_nt(nb, w1[pl.ds(D, D), :]) + b1_ref[:, pl.ds(D, D)]
        a_ref[...] = a.astype(BF16)
        g_ref[...] = g.astype(BF16)
        buf[pl.ds(HALO_B, tm), :] = a * _sigmoid(g)
        _shifted_copies(buf, shf, tm)

        _long_conv(buf, shf, wb, dc_ref, tm, [(k, HALO_B - (KB - 1) + k) for k in range(KB)], KB)
        buf[pl.ds(0, HALO_B), :] = buf[pl.ds(tm, HALO_B), :]
        xhat, _ = _ln_stats(dc_ref[...])
        ln = xhat * vec_ref[pl.ds(SM_LN_G, 1), :] + vec_ref[pl.ds(SM_LN_B, 1), :]
        s = (ln * _sigmoid(ln)).astype(BF16)
        s_ref[...] = s
        o_ref[...] = hv + _dot(s, w2[...]) + vec_ref[pl.ds(SM_B_PW2, 1), :]

    act = _sds((T, D), BF16)
    return _hosted_call(
        body, "mixb_fwd", T // tm,
        in_specs=[_row_spec(tm, D), _full_spec((SM_F32_ROWS, D)), _full_spec((1, 2 * D))],
        out_specs=[_row_spec(tm, D) for _ in range(6)],
        out_shape=(_sds((T, D), F32), act, act, act, _sds((T, D), F32), act),
        scratch=[pltpu.VMEM((2 * D, D), BF16), pltpu.VMEM((D, D), BF16),
                 pltpu.VMEM((HALO_B + tm, D), F32), pltpu.VMEM((7, HALO_B + tm - 8, D), F32),
                 pltpu.VMEM((KB + 1, 8, D), F32), pltpu.SemaphoreType.DMA((2 * NDEV,))],
        args=(h, vecs, bias1), comm=comm)


def _ffn_bwd_dx(dh, h, g, u, gam, wg, layer, tm, comm):
    T = h.shape[0]
    kg, ku, kd = "g%d" % layer, "u%d" % layer, "d%d" % layer

    def body(dh_ref, h_ref, g_ref, u_ref, gam_ref, wg_ref, o_ref, dg_ref, du_ref, st_ref, wgt, wut, wd, sems):
        first = pl.program_id(0) == 0
        load_w = _weight_loads(wg_ref, [(wd, kd), (wgt, kg), (wut, ku)], sems)

        @pl.when(first)
        def _():
            load_w()
            st_ref[...] = jnp.zeros((8, D), F32)

        dhv = dh_ref[...]
        dhb = dhv.astype(BF16)
        dn = jnp.zeros_like(dhv)
        for f in range(2):
            cols = pl.ds(f * FH, FH)
            dgu = _dot_nt(dhb, wd[cols, :])
            gv = g_ref[:, cols].astype(F32)
            sg = _sigmoid(gv)
            du = (dgu * gv * sg).astype(BF16)
            dg = (dgu * u_ref[:, cols].astype(F32) * (sg * (1.0 + gv * (1.0 - sg)))).astype(BF16)
            dg_ref[:, cols] = dg
            du_ref[:, cols] = du
            dn = dn + _dot(dg, wgt[cols, :]) + _dot(du, wut[cols, :])
        hv = h_ref[...]
        dx, dgam = _rms_bwd(dn, hv, _rms_stat(hv), gam_ref[...])
        o_ref[...] = dhv + dx
        st_ref[pl.ds(0, 1), :] += dgam

    pre = _sds((T, FF), BF16)
    return _hosted_call(
        body, "ffn%d_bwd_dx" % layer, T // tm,
        in_specs=[_row_spec(tm, D), _row_spec(tm, D), _row_spec(tm, FF), _row_spec(tm, FF),
                  _full_spec((1, D)), _ANY],
        out_specs=[_row_spec(tm, D), _row_spec(tm, FF), _row_spec(tm, FF), _full_spec((8, D))],
        out_shape=(_sds((T, D), F32), pre, pre, _sds((8, D), F32)),
        scratch=[pltpu.VMEM((FF, D), BF16)] * 3 + [pltpu.SemaphoreType.DMA((3 * NDEV,))],
        args=(dh, h, g, u, gam, wg), comm=comm)


def _grad_w(lhs, rhs, mc, name, tm):
    T, M = lhs.shape
    nt = T // tm

    def body(l_ref, r_ref, o_ref, acc):
        i = pl.program_id(1)

        @pl.when(i == 0)
        def _():
            acc[...] = jnp.zeros((mc, D), F32)

        acc[...] += _dot_tn(l_ref[...], r_ref[...].astype(BF16))

        @pl.when(i == nt - 1)
        def _():
            o_ref[...] = acc[...].astype(BF16)

    return _pcall(
        body, name=name, grid=(M // mc, nt),
        out_shape=_sds((M, D), BF16),
        in_specs=[pl.BlockSpec((tm, mc), lambda j, i: (i, j)), pl.BlockSpec((tm, D), lambda j, i: (i, 0))],
        out_specs=pl.BlockSpec((mc, D), lambda j, i: (j, 0)),
        scratch_shapes=[pltpu.VMEM((mc, D), F32)],
        compiler_params=_cparams(("arbitrary", "arbitrary")),
    )(lhs, rhs)


def _mixb_bwd(dh, h, a, g, dc, vecs, wg, tm, comm):
    T = h.shape[0]
    nt = T // tm

    def body(dh_ref, h_ref, a_ref, g_ref, dc_ref, vec_ref, wg_ref, o_ref, du_ref, st_ref, sb_ref,
             w1, w2, buf, shf, glu_s, dglu_s, wacc, wb, sems):
        first = pl.program_id(0) == 0
        load_w = _weight_loads(wg_ref, [(w2, "b_pw2"), (w1, "b_pw1")], sems)

        @pl.when(first)
        def _():
            load_w()
            buf[pl.ds(tm, HALO_B), :] = jnp.zeros((HALO_B, D), F32)
            wacc[...] = jnp.zeros((KB, 8, D), F32)
            _broadcast_taps(vec_ref, wb)
            st_ref[...] = jnp.zeros((SM_F32_ROWS, D), F32)
            sb_ref[...] = jnp.zeros((8, 2 * D), F32)

        def acc(row, val):
            st_ref[pl.ds(row, 1), :] += jnp.sum(val, axis=0, keepdims=True)

        dhv = dh_ref[...]
        acc(SM_B_PW2, dhv)
        ds = _dot_nt(dhv.astype(BF16), w2[...])
        xhat, rstd = _ln_stats(dc_ref[...])
        ln_g = vec_ref[pl.ds(SM_LN_G, 1), :]
        ln = xhat * ln_g + vec_ref[pl.ds(SM_LN_B, 1), :]
        sl = _sigmoid(ln)
        dln = ds * (sl * (1.0 + ln * (1.0 - sl)))
        acc(SM_LN_G, dln * xhat)
        acc(SM_LN_B, dln)
        dxh = dln * ln_g
        ddc = rstd * (dxh - jnp.mean(dxh, axis=-1, keepdims=True)
                      - xhat * jnp.mean(dxh * xhat, axis=-1, keepdims=True))
        acc(SM_B_BCONV, ddc)
        buf[pl.ds(0, tm), :] = ddc
        _shifted_copies(buf, shf, tm)
        av = a_ref[...].astype(F32)
        sg = _sigmoid(g_ref[...].astype(F32))
        glu_s[...] = av * sg

        _long_conv(buf, shf, wb, dglu_s, tm, [(KB - 1 - j, j) for j in range(KB)], None)
        _long_conv_grad_taps(buf, shf, glu_s, wacc, tm)
        buf[pl.ds(tm, HALO_B), :] = buf[pl.ds(0, HALO_B), :]
        dglu = dglu_s[...]
        da = dglu * sg
        dg = dglu * av * sg * (1.0 - sg)
        sb_ref[pl.ds(0, 1), pl.ds(0, D)] += jnp.sum(da, axis=0, keepdims=True)
        sb_ref[pl.ds(0, 1), pl.ds(D, D)] += jnp.sum(dg, axis=0, keepdims=True)
        dab, dgb = da.astype(BF16), dg.astype(BF16)
        du_ref[:, pl.ds(0, D)] = dab
        du_ref[:, pl.ds(D, D)] = dgb
        dn = _dot(dab, w1[pl.ds(0, D), :]) + _dot(dgb, w1[pl.ds(D, D), :])
        hv = h_ref[...]
        dx, dgam = _rms_bwd(dn, hv, _rms_stat(hv), vec_ref[pl.ds(SM_B_NORM, 1), :])
        o_ref[...] = dhv + dx
        st_ref[pl.ds(SM_B_NORM, 1), :] += dgam

        @pl.when(pl.program_id(0) == nt - 1)
        def _():
            st_ref[pl.ds(SM_B_CONV, KB), :] = jnp.sum(wacc[...], axis=1)

    rs = functools.partial(_row_spec, rev_nt=nt)
    return _hosted_call(
        body, "mixb_bwd", nt,
        in_specs=[rs(tm, D), rs(tm, D), rs(tm, D), rs(tm, D), rs(tm, D), _full_spec((SM_F32_ROWS, D)), _ANY],
        out_specs=[rs(tm, D), rs(tm, 2 * D), _full_spec((SM_F32_ROWS, D)), _full_spec((8, 2 * D))],
        out_shape=(_sds((T, D), F32), _sds((T, 2 * D), BF16), _sds((SM_F32_ROWS, D), F32), _sds((8, 2 * D), F32)),
        scratch=[pltpu.VMEM((2 * D, D), BF16), pltpu.VMEM((D, D), BF16),
                 pltpu.VMEM((tm + HALO_B, D), F32), pltpu.VMEM((7, tm + HALO_B - 8, D), F32),
                 pltpu.VMEM((tm, D), F32), pltpu.VMEM((tm, D), F32), pltpu.VMEM((KB, 8, D), F32),
                 pltpu.VMEM((KB + 1, 8, D), F32), pltpu.SemaphoreType.DMA((2 * NDEV,))],
        args=(dh, h, a, g, dc, vecs, wg), comm=comm)


def _mixa_bwd(dh, x, b, c, v, cc, y, gam, cw, wg, tm, comm):
    T = x.shape[0]
    nt = T // tm

    def body(dh_ref, x_ref, b_ref, c_ref, v_ref, cc_ref, y_ref, gam_ref, cw_ref, wg_ref,
             o_ref, st_ref, gin_ref, gout_ref, win, wout, buf, acc_in, acc_out, stage, sems, out_sems):
        first = pl.program_id(0) == 0
        load_w = _weight_loads(wg_ref, [(wout, "a_out"), (win, "a_in")], sems)

        @pl.when(first)
        def _():
            load_w()
            buf[pl.ds(tm, HALO_A), :] = jnp.zeros((HALO_A, D), F32)
            st_ref[...] = jnp.zeros((8, D), F32)
            acc_in[...] = jnp.zeros((3 * D, D), F32)
            acc_out[...] = jnp.zeros((D, D), F32)

        dhv = dh_ref[...]
        dhb = dhv.astype(BF16)
        dy = _dot_nt(dhb, wout[...])
        cval = c_ref[...].astype(F32)
        vval = v_ref[...].astype(F32)
        d_b = (dy * cc_ref[...].astype(F32)).astype(BF16)
        buf[pl.ds(0, tm), :] = dy * b_ref[...].astype(F32)
        cv = cval * vval
        dcv = jnp.zeros((tm, D), F32)
        for j in range(KA):
            sh = buf[pl.ds(j, tm), :]
            k = KA - 1 - j
            dcv = dcv + cw_ref[pl.ds(k, 1), :] * sh
            st_ref[pl.ds(1 + k, 1), :] += jnp.sum(cv * sh, axis=0, keepdims=True)
        buf[pl.ds(tm, HALO_A), :] = buf[pl.ds(0, HALO_A), :]
        d_c = (dcv * vval).astype(BF16)
        d_v = (dcv * cval).astype(BF16)
        xv = x_ref[...]
        r = _rms_stat(xv)
        gamma = gam_ref[...]
        nb = (xv * r * gamma).astype(BF16)
        for j, piece in enumerate((d_b, d_c, d_v)):
            acc_in[pl.ds(j * D, D), :] += _dot_tn(piece, nb)
        acc_out[...] += _dot_tn(y_ref[...], dhb)
        dn = _dot(d_b, win[pl.ds(0, D), :]) + _dot(d_c, win[pl.ds(D, D), :]) + _dot(d_v, win[pl.ds(2 * D, D), :])
        dx, dgam = _rms_bwd(dn, xv, r, gamma)
        o_ref[...] = dhv + dx
        st_ref[pl.ds(0, 1), :] += dgam

        @pl.when(pl.program_id(0) == nt - 1)
        def _():
            blocks = [(acc_in, j * D, gin_ref, j * D) for j in range(3)] + [(acc_out, 0, gout_ref, 0)]
            for k, (acc, a0, dst, d0) in enumerate(blocks):
                stage[...] = acc[pl.ds(a0, D), :].astype(BF16)
                cp = pltpu.make_async_copy(stage, dst.at[pl.ds(d0, D), :], out_sems.at[k])
                cp.start()
                cp.wait()

    rs = functools.partial(_row_spec, rev_nt=nt)
    return _hosted_call(
        body, "mixa_bwd", nt,
        in_specs=[rs(tm, D) for _ in range(7)] + [_full_spec((1, D)), _full_spec((KA, D)), _ANY],
        out_specs=[rs(tm, D), _full_spec((8, D)), _ANY, _ANY],
        out_shape=(_sds((T, D), F32), _sds((8, D), F32), _sds((3 * D, D), BF16), _sds((D, D), BF16)),
        scratch=[pltpu.VMEM((3 * D, D), BF16), pltpu.VMEM((D, D), BF16), pltpu.VMEM((tm + HALO_A, D), F32),
                 pltpu.VMEM((3 * D, D), F32), pltpu.VMEM((D, D), F32), pltpu.VMEM((D, D), BF16),
                 pltpu.SemaphoreType.DMA((2 * NDEV,)), pltpu.SemaphoreType.DMA((4,))],
        args=(dh, x, b, c, v, cc, y, gam, cw, wg), comm=comm)


def _sum_small(sm, rp):
    def body(sm_ref, rp_ref, osm, orp, oloss):
        a = sm_ref[0]
        b = rp_ref[0]
        for s in range(1, NDEV):
            a = a + sm_ref[s]
            b = b + rp_ref[s]
        osm[...] = a
        orp[...] = b
        oloss[...] = jnp.zeros((8, 128), F32) + jnp.sum(b[4:5, :], axis=-1, keepdims=True)

    return _pcall(
        body, name="sum_small_grads",
        out_shape=(_sds((SM_F32_ROWS, 128), F32), _sds((8, D), F32), _sds((8, 128), F32)),
        in_specs=[pl.BlockSpec(memory_space=pltpu.VMEM)] * 2,
        out_specs=tuple(pl.BlockSpec(memory_space=pltpu.VMEM) for _ in range(3)),
        compiler_params=_cparams(),
    )(sm, rp)


def _adam_math(w, g, m, v):
    mn = B1 * m + (1.0 - B1) * g
    vn = B2 * v + (1.0 - B2) * (g * g)
    m_hat = mn / (1.0 - B1 ** STEP)
    v_hat = vn / (1.0 - B2 ** STEP)
    return -LR * (m_hat / (jnp.sqrt(v_hat) + ADAM_EPS) + WD * w), mn, vn


def _finish_weight(land, keys, transposed, w, m, v, name, comm=None):
    layers, rows, cols = w.shape
    n = N_ROWS[keys[0]]
    n_pad = -(-n // 128) * 128 if transposed else n

    def body(w_ref, m_ref, v_ref, land_ref, og, od, om, ov, buf, sem):
        off = LD_OFF[keys[0]]
        if layers == 2:
            off = jnp.where(pl.program_id(0) == 0, off, LD_OFF[keys[1]])
        cp = pltpu.make_async_copy(land_ref.at[:, pl.ds(pl.multiple_of(off, 32), n), :], buf, sem)
        cp.start()
        cp.wait()
        g = buf[0].astype(F32)
        for s in range(1, NCHIP):
            g = g + buf[s].astype(F32)
        if transposed:
            if n_pad != n:
                g = jnp.concatenate([g, jnp.zeros((n_pad - n, D), F32)], axis=0)
            g = g.T[:, :n]
        d, mn, vn = _adam_math(w_ref[...], g, m_ref[...], v_ref[...])
        og[...] = g
        od[...] = d
        om[...] = mn
        ov[...] = vn

    spec = pl.BlockSpec((None, rows, cols), lambda l: (l, 0, 0))
    shp = _sds(w.shape, F32)
    in_specs, args = [spec, spec, spec], (w, m, v)
    if comm is None:
        in_specs, args = in_specs + [_ANY], args + (land,)
    return _hosted_call(
        body, "finish_" + name, layers,
        in_specs=in_specs, out_specs=[spec] * 4, out_shape=(shp,) * 4,
        scratch=[pltpu.VMEM((NCHIP, n, D), BF16), pltpu.SemaphoreType.DMA],
        args=args, comm=comm)


SMALL_PARAMS = (("sm", SM_B_NORM, 1), ("sm", SM_B_PW1, 2), ("sm", SM_B_CONV, KB), ("sm", SM_B_BCONV, 1),
                ("sm", SM_LN_G, 1), ("sm", SM_LN_B, 1), ("sm", SM_B_PW2, 1), ("sm", SM_A_CONV, KA),
                ("rp", 0, 1), ("rp", 1, 2), ("rp", 3, 1))


def _adamw_small(g_small, g_repl, triples):
    n = len(SMALL_PARAMS)

    def body(*refs):
        gs_ref, gr_ref = refs[0], refs[1]
        ins = refs[2:2 + 3 * n]
        outs = refs[2 + 3 * n:]
        for i, (src, r0, nr) in enumerate(SMALL_PARAMS):
            w_ref, m_ref, v_ref = ins[3 * i:3 * i + 3]
            g = (gs_ref if src == "sm" else gr_ref)[pl.ds(r0, nr), :]
            if i == 1:
                g = jnp.concatenate([g[0:1], g[1:2]], axis=1)
            lead = (0,) if len(w_ref.shape) == 3 else ()
            idx = lead + (slice(None), slice(None))
            vals = (g,) + _adam_math(w_ref[idx], g, m_ref[idx], v_ref[idx])
            for o_ref, val in zip(outs[4 * i:4 * i + 4], vals):
                o_ref[idx] = val

    flat = [a for t in triples for a in t]
    out_shape = tuple(_sds(t[0].shape, F32) for t in triples for _ in range(4))
    vm = pl.BlockSpec(memory_space=pltpu.VMEM)
    res = _pcall(
        body, name="adamw_small", out_shape=out_shape,
        in_specs=[vm] * (2 + len(flat)), out_specs=tuple(vm for _ in out_shape),
        compiler_params=_cparams(),
    )(g_small, g_repl, *flat)
    return [tuple(res[4 * i:4 * i + 4]) for i in range(n)]


def kernel(x, a_norm, a_w_in, a_conv, a_w_out, b_norm, b_w_pw1, b_b_pw1, b_conv, b_b_conv, b_ln_g, b_ln_b, b_w_pw2, b_b_pw2, ffn_norm, ffn_w_gate, ffn_w_up, ffn_w_down, final_norm, loss_target, m_a_norm, m_a_w_in, m_a_conv, m_a_w_out, m_b_norm, m_b_w_pw1, m_b_b_pw1, m_b_conv, m_b_b_conv, m_b_ln_g, m_b_ln_b, m_b_w_pw2, m_b_b_pw2, m_ffn_norm, m_ffn_w_gate, m_ffn_w_up, m_ffn_w_down, m_final_norm, v_a_norm, v_a_w_in, v_a_conv, v_a_w_out, v_b_norm, v_b_w_pw1, v_b_b_pw1, v_b_conv, v_b_b_conv, v_b_ln_g, v_b_ln_b, v_b_w_pw2, v_b_b_pw2, v_ffn_norm, v_ffn_w_gate, v_ffn_w_up, v_ffn_w_down, v_final_norm):
    T = x.shape[1]
    tm = min(TM, T)
    tma = min(TM_A, T)
    tw = min(TM_DW, T)
    xs = x.reshape(T, D)
    tgt = loss_target.reshape(T, D)

    def rows_first(a):
        return jnp.swapaxes(a, 1, 2)

    shard, small = _pack_shard(
        (a_w_in, a_w_out, b_w_pw1, b_w_pw2, rows_first(ffn_w_gate), rows_first(ffn_w_up), ffn_w_down),
        (b_norm, b_b_pw1, b_conv, b_b_conv, b_ln_g, b_ln_b, b_b_pw2, a_conv))
    wg, sm_all = _all_gather_first(shard, small)
    vecs = sm_all.transpose(1, 0, 2).reshape(SM_F32_ROWS, D)
    bias1 = sm_all[:, SM_B_PW1:SM_B_PW1 + 2, :].reshape(1, 2 * D)
    cw_a = vecs[SM_A_CONV:SM_A_CONV + KA]
    fn0, fn1 = ffn_norm[0:1], ffn_norm[1:2]
    fin = final_norm.reshape(1, D)

    h1, n0, bq, cq, vq, ccq, yq, wg = _mixa_fwd(xs, a_norm, cw_a, wg, tma, _ag_comm(shard, wg, ("g0", "u0", "d0")))
    h2, n1, g0, u0, gu0, wg = _ffn_fwd(h1, fn0, wg, 0, tma, _ag_comm(shard, wg, ("b_pw1", "b_pw2"), (T // tma) // 2))
    h3, n2, aq, gq, dcq, sq, wg = _mixb_fwd(h2, vecs, bias1, wg, tm, _ag_comm(shard, wg, ("g1", "u1", "d1"), 1))
    dh4, n3, g1, u1, gu1, st_fin = _ffn_fwd(h3, fn1, wg, 1, tma, None, head=(tgt, fin))

    def by_dest(gw, key):
        return gw.reshape(NDEV, N_ROWS[key], D)

    gw_d1 = by_dest(_grad_w(gu1, dh4, FH, "grad_down1", tw), "d1")
    dh3, dg1, du1, st_f1 = _ffn_bwd_dx(dh4, h3, g1, u1, fn1, wg, 1, tma, None)
    gw_g1 = by_dest(_grad_w(dg1, n3, FH, "grad_gate1", tw), "g1")
    gw_u1 = by_dest(_grad_w(du1, n3, FH, "grad_up1", tw), "u1")
    keys = ("g1", "u1", "d1")
    dh2, dub, st_b, st_b1, land = _mixb_bwd(
        dh3, h2, aq, gq, dcq, vecs, wg, tm, _rs_comm(_pair_reduce([gw_g1, gw_u1, gw_d1], keys), keys, None))
    gw_pw1 = by_dest(_grad_w(dub, n2, D, "grad_pw1", tw), "b_pw1")
    gw_pw2 = by_dest(_grad_w(sq, dh3, D, "grad_pw2", tw), "b_pw2")
    gw_d0 = by_dest(_grad_w(gu0, dh2, FH, "grad_down0", tw), "d0")
    keys = ("b_pw1", "b_pw2", "d0")
    dh1, dg0, du0, st_f0, land = _ffn_bwd_dx(
        dh2, h1, g0, u0, fn0, wg, 0, tma, _rs_comm(_pair_reduce([gw_pw1, gw_pw2, gw_d0], keys), keys, land))
    gw_g0 = by_dest(_grad_w(dg0, n1, FH, "grad_gate0", tw), "g0")
    gw_u0 = by_dest(_grad_w(du0, n1, FH, "grad_up0", tw), "u0")
    keys = ("g0", "u0")
    dx, st_a, gw_in, gw_out, land = _mixa_bwd(
        dh1, xs, bq, cq, vq, ccq, yq, a_norm, cw_a, wg, tm,
        _rs_comm(_pair_reduce([gw_g0, gw_u0], keys), keys, land))
    gw_in, gw_out = by_dest(gw_in, "a_in"), by_dest(gw_out, "a_out")

    st_small = st_b.at[SM_A_CONV:SM_A_CONV + KA].set(st_a[1:1 + KA])
    sm_dest = st_small.reshape(SM_F32_ROWS, NDEV, 128).transpose(1, 0, 2)
    sm_dest = sm_dest.at[:, SM_B_PW1:SM_B_PW1 + 2, :].set(st_b1[0].reshape(NDEV, 2, 128))
    repl = jnp.concatenate([st_a[0:1], st_f0[0:1], st_f1[0:1], st_fin[0:1], st_fin[1:2],
                            jnp.zeros((3, D), F32)], axis=0)[None]
    last = ("a_out", "a_in")
    p_last, sm_land, rp_land = _pair_reduce([gw_out, gw_in], last, side=(sm_dest, repl))

    big = [("ffn_w_gate", ("g0", "g1"), False,
            rows_first(ffn_w_gate), rows_first(m_ffn_w_gate), rows_first(v_ffn_w_gate)),
           ("ffn_w_up", ("u0", "u1"), False, rows_first(ffn_w_up), rows_first(m_ffn_w_up), rows_first(v_ffn_w_up)),
           ("ffn_w_down", ("d0", "d1"), False, ffn_w_down, m_ffn_w_down, v_ffn_w_down),
           ("a_w_out", ("a_out",), False, a_w_out, m_a_w_out, v_a_w_out),
           ("b_w_pw1", ("b_pw1",), True, b_w_pw1, m_b_w_pw1, v_b_w_pw1),
           ("b_w_pw2", ("b_pw2",), False, b_w_pw2, m_b_w_pw2, v_b_w_pw2),
           ("a_w_in", ("a_in",), True, a_w_in, m_a_w_in, v_a_w_in)]
    res = {}
    for k, (name, keys, transposed, w, m, v) in enumerate(big):
        if k < 3:
            comm = _rs_comm(p_last, last, land, part=((0, 176), (176, 176), (352, 160))[k], body_reads_land=True)
            *res[name], land = _finish_weight(land, keys, transposed, w, m, v, name, comm)
        else:
            res[name] = _finish_weight(land, keys, transposed, w, m, v, name)
    for name in ("ffn_w_gate", "ffn_w_up"):
        res[name] = [rows_first(a) for a in res[name]]

    g_small, g_repl, loss8 = _sum_small(sm_land, rp_land)
    loss = loss8[0, 0]

    small_names = ["b_norm", "b_b_pw1", "b_conv", "b_b_conv", "b_ln_g", "b_ln_b", "b_b_pw2", "a_conv",
                   "a_norm", "ffn_norm", "final_norm"]
    triples = [(b_norm, m_b_norm, v_b_norm), (b_b_pw1, m_b_b_pw1, v_b_b_pw1), (b_conv, m_b_conv, v_b_conv),
               (b_b_conv, m_b_b_conv, v_b_b_conv), (b_ln_g, m_b_ln_g, v_b_ln_g), (b_ln_b, m_b_ln_b, v_b_ln_b),
               (b_b_pw2, m_b_b_pw2, v_b_b_pw2), (a_conv, m_a_conv, v_a_conv), (a_norm, m_a_norm, v_a_norm),
               (ffn_norm, m_ffn_norm, v_ffn_norm),
               (fin, m_final_norm.reshape(1, D), v_final_norm.reshape(1, D))]
    for name, quad in zip(small_names, _adamw_small(g_small, g_repl, triples)):
        res[name] = quad
    res["final_norm"] = tuple(a.reshape(D) for a in res["final_norm"])

    order = ["a_norm", "a_w_in", "a_conv", "a_w_out", "b_norm", "b_w_pw1", "b_b_pw1", "b_conv", "b_b_conv",
             "b_ln_g", "b_ln_b", "b_w_pw2", "b_b_pw2", "ffn_norm", "ffn_w_gate", "ffn_w_up", "ffn_w_down",
             "final_norm"]
    out = [loss, dx.reshape(1, T, D)]
    for j in range(4):
        out += [res[k][j] for k in order]
    return tuple(out)
```

```python
import functools

import jax
import jax.numpy as jnp
from jax import lax
from jax.experimental import pallas as pl
from jax.experimental.pallas import tpu as pltpu

F32 = jnp.float32
BF16 = jnp.bfloat16
MESH = pl.DeviceIdType.MESH

D = 1024
FF = 2816
FH = FF // 2
NDEV = 8
KA = 3
KB = 31
HALO_A = 8
HALO_B = 32
CONV_ROWS = 64
RMS_EPS = 1e-6
LN_EPS = 1e-5
LR, B1, B2, ADAM_EPS, WD, STEP = 0.001, 0.9, 0.999, 1e-08, 0.01, 10

TM = 256
TM_A = 512
TM_DW = 1024
VMEM_LIMIT = 62 * 1024 * 1024

N_ROWS = {"a_in": 384, "a_out": 128, "b_pw1": 256, "b_pw2": 128, "g0": 352, "u0": 352, "d0": 352,
          "g1": 352, "u1": 352, "d1": 352}
WEIGHT_KEYS = ("a_in", "a_out", "b_pw1", "b_pw2", "g0", "u0", "d0", "g1", "u1", "d1")
NCHIP = 4


def _offsets(order):
    off, o = {}, 0
    for k in order:
        off[k] = o
        o += N_ROWS[k]
    return off, o


PK_ORDER = ("a_in", "a_out", "g0", "u0", "d0", "b_pw1", "b_pw2", "g1", "u1", "d1")
PK_OFF, PK_ROWS = _offsets(PK_ORDER)
LD_ORDER = ("g1", "u1", "d1", "b_pw1", "b_pw2", "d0", "g0", "u0", "a_out", "a_in")
LD_OFF, N_WROWS = _offsets(LD_ORDER)


def _span(off, keys):
    return off[keys[0]], sum(N_ROWS[k] for k in keys)


SM_B_NORM, SM_B_PW1, SM_B_CONV, SM_B_BCONV, SM_LN_G, SM_LN_B, SM_B_PW2, SM_A_CONV = 0, 1, 3, 34, 35, 36, 37, 38
SM_USED = 41
SM_F32_ROWS = 64


def _pcall(body, **kw):
    return pl.pallas_call(body, **kw)


def _cparams(sem=None):
    return pltpu.CompilerParams(dimension_semantics=sem, vmem_limit_bytes=VMEM_LIMIT)


def _dot(a, b):
    return jnp.dot(a, b, preferred_element_type=F32)


def _dot_nt(a, b):
    return lax.dot_general(a, b, (((1,), (1,)), ((), ())), preferred_element_type=F32)


def _dot_tn(a, b):
    return lax.dot_general(a, b, (((0,), (0,)), ((), ())), preferred_element_type=F32)


def _sigmoid(v):
    return 1.0 / (1.0 + jnp.exp(-v))


def _rms_stat(x):
    return lax.rsqrt(jnp.mean(x * x, axis=-1, keepdims=True) + RMS_EPS)


def _rms_bwd(dn, x, r, gamma):
    dng = dn * gamma
    dx = r * dng - x * (r * r * r) * jnp.mean(dng * x, axis=-1, keepdims=True)
    return dx, jnp.sum(dn * x * r, axis=0, keepdims=True)


def _weight_loads(wg_ref, plan, sems):
    copies = []
    for j, (dst, key) in enumerate(plan):
        off, n = PK_OFF[key], N_ROWS[key]
        copies += [pltpu.make_async_copy(
            wg_ref.at[d, pl.ds(off, n), :], dst.at[pl.ds(d * n, n), :], sems.at[j * NDEV + d])
            for d in range(NDEV)]

    def load():
        for cp in copies:
            cp.start()
        for cp in copies:
            cp.wait()

    return load


_ANY = pl.BlockSpec(memory_space=pl.ANY)


def _row_spec(tm, width, rev_nt=None):
    if rev_nt is None:
        return pl.BlockSpec((tm, width), lambda i: (i, 0))
    return pl.BlockSpec((tm, width), lambda i: (rev_nt - 1 - i, 0))


def _full_spec(shape):
    return pl.BlockSpec(shape, lambda *_: (0,) * len(shape))


def _sds(shape, dtype):
    return jax.ShapeDtypeStruct(shape, dtype)


def _mesh_pos():
    return lax.axis_index("x"), lax.axis_index("y"), lax.axis_index("c")


def _lin(p):
    return 4 * p[0] + 2 * p[1] + p[2]


def _ag_exchange(src, slot, send_sems, recv_sems, local_sem):
    x, y, c = _mesh_pos()
    me, sibling = (x, y, c), (x, y, 1 - c)
    chips = [(1 - x, y), (x, 1 - y), (1 - x, 1 - y)]

    def copy(k, block, to, own=False):
        return pltpu.make_async_remote_copy(
            src_ref=src if own else slot(block), dst_ref=slot(block),
            send_sem=send_sems.at[k], recv_sem=recv_sems.at[k], device_id=to, device_id_type=MESH)

    mine = pltpu.make_async_copy(src, slot(me), local_sem)
    first = [copy(0, me, sibling, own=True)]
    first += [copy(1 + j, me, (*chip, c), own=True) for j, chip in enumerate(chips)]
    passed = [copy(4 + j, (*chip, c), sibling) for j, chip in enumerate(chips)]

    def start():
        mine.start()
        for cp in first:
            cp.start()

    def forward():
        for j, chip in enumerate(chips):
            copy(1 + j, (*chip, c), me).wait_recv()
            passed[j].start()

    def finish():
        copy(0, sibling, me).wait_recv()
        for j, chip in enumerate(chips):
            copy(4 + j, (*chip, 1 - c), me).wait_recv()
        for cp in first + passed:
            cp.wait_send()
        mine.wait()

    return start, forward, finish


def _chip_exchange(p_ref, land_ref, send_sems, recv_sems, keys, part):
    x, y, c = _mesh_pos()
    l0, rows = _span(LD_OFF, keys)
    p0, rows = (0, rows) if part is None else part
    dst = land_ref.at[2 * x + y, pl.ds(l0 + p0, rows), :]
    peers = [((x + (j >> 1)) % 2, (y + (j & 1)) % 2) for j in range(NCHIP)]

    def copy(j):
        tx, ty = peers[j]
        src = p_ref.at[2 * tx + ty, pl.ds(p0, rows), :]
        if j == 0:
            return pltpu.make_async_copy(src, dst, send_sems.at[0])
        return pltpu.make_async_remote_copy(
            src_ref=src, dst_ref=dst, send_sem=send_sems.at[j], recv_sem=recv_sems.at[j],
            device_id=(tx, ty, c), device_id_type=MESH)

    def start():
        for j in range(NCHIP):
            copy(j).start()

    def finish():
        for j in range(NCHIP):
            copy(j).wait()

    return start, finish


def _all_to_all_f32(src_for, dst_ref, send_sems, recv_sems):
    x, y, c = _mesh_pos()
    dst = dst_ref.at[_lin((x, y, c))]
    peers = [((x + ((j >> 2) & 1)) % 2, (y + ((j >> 1) & 1)) % 2, (c + (j & 1)) % 2) for j in range(NDEV)]

    def copy(j):
        if j == 0:
            return pltpu.make_async_copy(src_for(_lin(peers[0])), dst, send_sems.at[0])
        return pltpu.make_async_remote_copy(
            src_ref=src_for(_lin(peers[j])), dst_ref=dst, send_sem=send_sems.at[j], recv_sem=recv_sems.at[j],
            device_id=peers[j], device_id_type=MESH)

    def start():
        for j in range(NDEV):
            copy(j).start()

    def finish():
        for j in range(NDEV):
            copy(j).wait()

    return start, finish


class _Comm:
    def __init__(self, ins, alias_in, out_shape, scratch, make, gives_wg, middle_step=0):
        self.ins, self.alias_in, self.out_shape = ins, alias_in, out_shape
        self.scratch, self.make, self.gives_wg, self.middle_step = scratch, make, gives_wg, middle_step


def _ag_comm(shard, wg, keys, forward_step=0):
    def make(c_ins, c_out, sc):
        return _weights_exchange(c_ins[0], c_out, sc[0], sc[1], sc[2], keys)

    return _Comm([shard, wg], 1, _sds(wg.shape, BF16),
                 [pltpu.SemaphoreType.DMA((7,)), pltpu.SemaphoreType.DMA((7,)), pltpu.SemaphoreType.DMA],
                 make, True, forward_step)


def _weights_exchange(shard_ref, wg_ref, send_sems, recv_sems, local_sem, keys):
    r0, nr = _span(PK_OFF, keys)
    return _ag_exchange(shard_ref.at[pl.ds(r0, nr), :], lambda p: wg_ref.at[_lin(p), pl.ds(r0, nr), :],
                        send_sems, recv_sems, local_sem)


def _rs_comm(psum, keys, land, part=None, body_reads_land=False):
    def make(c_ins, c_out, sc):
        start, finish = _chip_exchange(c_ins[0], c_out, sc[0], sc[1], keys, part)
        return start, None, finish

    ins = [psum] + ([] if land is None else [land])
    return _Comm(ins, None if land is None else 1, _sds((NCHIP, N_WROWS, D), BF16),
                 [pltpu.SemaphoreType.DMA((NCHIP,)), pltpu.SemaphoreType.DMA((NCHIP,))], make, body_reads_land)


def _hosted_call(body, name, nt, in_specs, out_specs, out_shape, scratch, args, comm):
    if comm is None:
        return _pcall(body, name=name, grid=(nt,), in_specs=in_specs, out_specs=tuple(out_specs),
                      out_shape=tuple(out_shape), scratch_shapes=scratch,
                      compiler_params=_cparams(("arbitrary",)))(*args)
    n_in, n_out, n_sc, n_cin = len(in_specs), len(out_specs), len(scratch), len(comm.ins)

    def wrapped(*refs):
        ins = refs[:n_in]
        c_ins = refs[n_in:n_in + n_cin]
        outs = refs[n_in + n_cin:n_in + n_cin + n_out]
        c_out = refs[n_in + n_cin + n_out]
        sc = refs[n_in + n_cin + n_out + 1:n_in + n_cin + n_out + 1 + n_sc]
        c_sc = refs[n_in + n_cin + n_out + 1 + n_sc:]
        start, middle, finish = comm.make(c_ins, c_out, c_sc)
        pl.when(pl.program_id(0) == 0)(start)
        if comm.gives_wg:
            body(*ins, c_out, *outs, *sc)
        else:
            body(*ins, *outs, *sc)
        if middle is not None:
            pl.when(pl.program_id(0) == max(nt - 1 - comm.middle_step, 0))(middle)
        pl.when(pl.program_id(0) == nt - 1)(finish)

    aliases = {} if comm.alias_in is None else {n_in + comm.alias_in: n_out}
    res = _pcall(wrapped, name=name, grid=(nt,),
                 in_specs=list(in_specs) + [_ANY] * n_cin, out_specs=tuple(out_specs) + (_ANY,),
                 out_shape=tuple(out_shape) + (comm.out_shape,),
                 scratch_shapes=list(scratch) + list(comm.scratch),
                 input_output_aliases=aliases,
                 compiler_params=_cparams(("arbitrary",)))(*args, *comm.ins)
    return res


def _pack_shard(weights, smalls):
    plan = (("a_in", 0, 0, True), ("a_out", 1, 0, False), ("b_pw1", 2, 0, True), ("b_pw2", 3, 0, False),
            ("g0", 4, 0, False), ("u0", 5, 0, False), ("d0", 6, 0, False),
            ("g1", 4, 1, False), ("u1", 5, 1, False), ("d1", 6, 1, False))
    n_pad = 384

    def body(*refs):
        out, sm, pad = refs[-3:]
        for key, idx, layer, transposed in plan:
            n = N_ROWS[key]
            val = refs[idx][layer]
            if transposed:
                if n % 128:
                    pad[:, pl.ds(0, n)] = val
                    pad[:, pl.ds(n, n_pad - n)] = jnp.zeros((D, n_pad - n), F32)
                    val = pad[...]
                val = val.T[:n]
            out[pl.ds(PK_OFF[key], n), :] = val.astype(BF16)
        b_norm, b_b_pw1, b_conv, b_b_conv, b_ln_g, b_ln_b, b_b_pw2, a_conv = refs[len(weights):-3]
        sm[...] = jnp.zeros((SM_F32_ROWS, 128), F32)
        sm[pl.ds(SM_B_PW1, 1), :] = b_b_pw1[:, pl.ds(0, 128)]
        sm[pl.ds(SM_B_PW1 + 1, 1), :] = b_b_pw1[:, pl.ds(128, 128)]
        sm[pl.ds(SM_B_CONV, KB), :] = b_conv[0]
        sm[pl.ds(SM_A_CONV, KA), :] = a_conv[0]
        for row, ref in ((SM_B_NORM, b_norm), (SM_B_BCONV, b_b_conv), (SM_LN_G, b_ln_g), (SM_LN_B, b_ln_b),
                         (SM_B_PW2, b_b_pw2)):
            sm[pl.ds(row, 1), :] = ref[...]

    vm = pl.BlockSpec(memory_space=pltpu.VMEM)
    return _pcall(
        body, name="pack_shard",
        out_shape=(_sds((PK_ROWS, D), BF16), _sds((SM_F32_ROWS, 128), F32)),
        in_specs=[vm] * (len(weights) + len(smalls)), out_specs=(vm, vm),
        scratch_shapes=[pltpu.VMEM((D, n_pad), F32)],
        compiler_params=_cparams(),
    )(*weights, *smalls)


def _all_gather_first(shard, small):
    def body(x_ref, s_ref, wg_ref, sg_ref, send_w, recv_w, local_w, send_s, recv_s, local_s):
        start_w, forward_w, finish_w = _weights_exchange(x_ref, wg_ref, send_w, recv_w, local_w, ("a_in", "a_out"))
        start_s, forward_s, finish_s = _ag_exchange(s_ref, lambda p: sg_ref.at[_lin(p)], send_s, recv_s, local_s)
        start_s()
        start_w()
        forward_s()
        forward_w()
        finish_s()
        finish_w()

    sems = [pltpu.SemaphoreType.DMA((7,)), pltpu.SemaphoreType.DMA((7,)), pltpu.SemaphoreType.DMA]
    return _pcall(
        body, name="all_gather_first",
        out_shape=(_sds((NDEV, PK_ROWS, D), BF16), _sds((NDEV, SM_F32_ROWS, 128), F32)),
        in_specs=[_ANY, _ANY], out_specs=(_ANY, _ANY), scratch_shapes=sems + sems,
    )(shard, small)


def _pair_reduce(grads, keys, side=None):
    l0, rows = _span(LD_OFF, keys)
    nk = len(keys)
    n_side = 0 if side is None else 2

    def body(c_ref, *refs):
        mine = refs[:nk]
        whole = refs[nk:2 * nk]
        o_ref = refs[2 * nk + n_side]
        got, send_sem, recv_sems = refs[2 * nk + 2 * n_side + 1:2 * nk + 2 * n_side + 4]
        t = pl.program_id(0)
        x, y, c = _mesh_pos()
        sibling = (x, y, 1 - c)
        if side is not None:
            sm_ref, rp_ref = refs[2 * nk:2 * nk + 2]
            sm_land, rp_land = refs[2 * nk + 3:2 * nk + 5]
            s2, r2, s3, r3 = refs[-4:]
            start_s, finish_s = _all_to_all_f32(lambda d: sm_ref.at[d], sm_land, s2, r2)
            start_r, finish_r = _all_to_all_f32(lambda d: rp_ref.at[0], rp_land, s3, r3)

        @pl.when(t == 0)
        def _():
            if side is not None:
                start_s()
                start_r()
            for ref, key in zip(whole, keys):
                for d in range(NCHIP):
                    pltpu.make_async_remote_copy(
                        src_ref=ref.at[2 * d + (1 - c)], dst_ref=got.at[d, pl.ds(LD_OFF[key] - l0, N_ROWS[key]), :],
                        send_sem=send_sem, recv_sem=recv_sems.at[d], device_id=sibling, device_id_type=MESH).start()

        pltpu.make_async_remote_copy(src_ref=got.at[t], dst_ref=got.at[t], send_sem=send_sem,
                                     recv_sem=recv_sems.at[t], device_id=sibling, device_id_type=MESH).wait_recv()
        for ref, key in zip(mine, keys):
            sl = pl.ds(LD_OFF[key] - l0, N_ROWS[key])
            o_ref[sl, :] = (ref[...].astype(F32) + got[t, sl, :].astype(F32)).astype(BF16)

        @pl.when(t == NCHIP - 1)
        def _():
            pltpu.make_async_remote_copy(src_ref=got, dst_ref=got, send_sem=send_sem, recv_sem=recv_sems.at[0],
                                         device_id=sibling, device_id_type=MESH).wait_send()
            if side is not None:
                finish_s()
                finish_r()

    out_specs = [pl.BlockSpec((None, rows, D), lambda t, c: (t, 0, 0))]
    out_shape = [_sds((NCHIP, rows, D), BF16)]
    scratch = [pltpu.VMEM((NCHIP, rows, D), BF16), pltpu.SemaphoreType.DMA, pltpu.SemaphoreType.DMA((NCHIP,))]
    if side is not None:
        out_specs += [_ANY, _ANY]
        out_shape += [_sds((NDEV, SM_F32_ROWS, 128), F32), _sds((NDEV, 8, D), F32)]
        scratch += [pltpu.SemaphoreType.DMA((NDEV,))] * 4
    grid_spec = pltpu.PrefetchScalarGridSpec(
        num_scalar_prefetch=1, grid=(NCHIP,),
        in_specs=[pl.BlockSpec((None, N_ROWS[k], D), lambda t, c: (2 * t + c[0], 0, 0)) for k in keys]
        + [_ANY] * (nk + n_side),
        out_specs=tuple(out_specs), scratch_shapes=scratch)
    core = lax.axis_index("c").astype(jnp.int32).reshape(1)
    res = _pcall(
        body, name="pair_reduce_" + keys[0], grid_spec=grid_spec, out_shape=tuple(out_shape),
        compiler_params=_cparams(("arbitrary",)),
    )(core, *grads, *grads, *(side or ()))
    return res[0] if side is None else res


def _mixa_fwd(x, gam, cw, wg, tm, comm):
    T = x.shape[0]

    def body(x_ref, gam_ref, cw_ref, wg_ref, h_ref, b_ref, c_ref, v_ref, cc_ref, y_ref,
             win, wout, buf, sems):
        first = pl.program_id(0) == 0
        load_w = _weight_loads(wg_ref, [(win, "a_in"), (wout, "a_out")], sems)

        @pl.when(first)
        def _():
            load_w()
            buf[pl.ds(0, HALO_A), :] = jnp.zeros((HALO_A, D), F32)

        xv = x_ref[...]
        nb = (xv * _rms_stat(xv) * gam_ref[...]).astype(BF16)
        bv = _dot_nt(nb, win[pl.ds(0, D), :])
        cval = _dot_nt(nb, win[pl.ds(D, D), :])
        vval = _dot_nt(nb, win[pl.ds(2 * D, D), :])
        cv = cval * vval
        buf[pl.ds(HALO_A, tm), :] = cv
        cc = cw_ref[pl.ds(KA - 1, 1), :] * cv
        for k in range(KA - 1):
            cc = cc + cw_ref[pl.ds(k, 1), :] * buf[pl.ds(HALO_A - (KA - 1) + k, tm), :]
        buf[pl.ds(0, HALO_A), :] = buf[pl.ds(tm, HALO_A), :]
        yb = (bv * cc).astype(BF16)
        b_ref[...] = bv.astype(BF16)
        c_ref[...] = cval.astype(BF16)
        v_ref[...] = vval.astype(BF16)
        cc_ref[...] = cc.astype(BF16)
        y_ref[...] = yb
        h_ref[...] = xv + _dot(yb, wout[...])

    act = _sds((T, D), BF16)
    return _hosted_call(
        body, "mixa_fwd", T // tm,
        in_specs=[_row_spec(tm, D), _full_spec((1, D)), _full_spec((KA, D))],
        out_specs=[_row_spec(tm, D) for _ in range(6)],
        out_shape=(_sds((T, D), F32),) + (act,) * 5,
        scratch=[pltpu.VMEM((3 * D, D), BF16), pltpu.VMEM((D, D), BF16),
                 pltpu.VMEM((HALO_A + tm, D), F32), pltpu.SemaphoreType.DMA((2 * NDEV,))],
        args=(x, gam, cw), comm=comm)


def _ffn_fwd(h, gam, wg, layer, tm, comm, head=None):
    T = h.shape[0]
    kg, ku, kd = "g%d" % layer, "u%d" % layer, "d%d" % layer
    n_head = 0 if head is None else 2

    def body(*refs):
        h_ref, gam_ref = refs[:2]
        wg_ref = refs[2 + n_head]
        o_ref, n_ref, g_ref, u_ref, gu_ref = refs[3 + n_head:8 + n_head]
        wgt, wut, wd, sems = refs[-4:]

        first = pl.program_id(0) == 0
        load_w = _weight_loads(wg_ref, [(wgt, kg), (wut, ku), (wd, kd)], sems)
        pl.when(first)(load_w)

        hv = h_ref[...]
        nb = (hv * _rms_stat(hv) * gam_ref[...]).astype(BF16)
        n_ref[...] = nb
        out = hv
        for f in range(2):
            cols = pl.ds(f * FH, FH)
            g = _dot_nt(nb, wgt[cols, :])
            u = _dot_nt(nb, wut[cols, :])
            gu = (g * _sigmoid(g) * u).astype(BF16)
            g_ref[:, cols] = g.astype(BF16)
            u_ref[:, cols] = u.astype(BF16)
            gu_ref[:, cols] = gu
            out = out + _dot(gu, wd[cols, :])
        if head is None:
            o_ref[...] = out
        else:
            t_ref, fin_ref, st_ref = refs[2], refs[3], refs[8 + n_head]

            @pl.when(pl.program_id(0) == 0)
            def _():
                st_ref[...] = jnp.zeros((8, D), F32)

            gamma = fin_ref[...]
            r = _rms_stat(out)
            err = out * r * gamma - t_ref[...]
            dx, dgam = _rms_bwd(err * (1.0 / D), out, r, gamma)
            o_ref[...] = dx
            st_ref[pl.ds(0, 1), :] += dgam
            st_ref[pl.ds(1, 1), :] += (0.5 / D) * jnp.sum(err * err, axis=0, keepdims=True)

    pre = _sds((T, FF), BF16)
    in_specs = [_row_spec(tm, D), _full_spec((1, D))]
    args = (h, gam)
    out_specs = [_row_spec(tm, D), _row_spec(tm, D), _row_spec(tm, FF), _row_spec(tm, FF), _row_spec(tm, FF)]
    out_shape = (_sds((T, D), F32), _sds((T, D), BF16), pre, pre, pre)
    if head is not None:
        in_specs, args = in_specs + [_row_spec(tm, D), _full_spec((1, D))], args + tuple(head)
        out_specs, out_shape = out_specs + [_full_spec((8, D))], out_shape + (_sds((8, D), F32),)
    if comm is None:
        in_specs, args = in_specs + [_ANY], args + (wg,)
    return _hosted_call(
        body, "ffn%d_fwd" % layer, T // tm,
        in_specs=in_specs, out_specs=out_specs, out_shape=out_shape,
        scratch=[pltpu.VMEM((FF, D), BF16)] * 3 + [pltpu.SemaphoreType.DMA((3 * NDEV,))],
        args=args, comm=comm)


def _shifted_copies(buf, shf, tm):
    for r in range(1, 8):
        shf[r - 1] = buf[pl.ds(r, tm + HALO_B - 8), :]


def _broadcast_taps(vec_ref, wb):
    for k in range(KB):
        wb[k] = jnp.broadcast_to(vec_ref[pl.ds(SM_B_CONV + k, 1), :], (8, D))
    wb[KB] = jnp.broadcast_to(vec_ref[pl.ds(SM_B_BCONV, 1), :], (8, D))


def _taps_by_shift_residue(taps):
    groups = {}
    for k, shift in taps:
        q, r = divmod(shift, 8)
        groups.setdefault(r, []).append((k, q))
    return sorted(groups.items())


def _window(buf, shf, base, r, q0, n_groups, lanes):
    rows = pl.ds(base + 8 * q0, 8 * n_groups)
    v = buf[rows, lanes] if r == 0 else shf[r - 1, rows, lanes]
    return [v[8 * i:8 * i + 8] for i in range(n_groups)]


def _long_conv(buf, shf, wb, out_ref, tm, taps, bias_row):
    n_acc = CONV_ROWS // 8
    groups = _taps_by_shift_residue(taps)
    for col in range(D // 128):
        lanes = pl.ds(128 * col, 128)

        def rows(i, carry, lanes=lanes):
            base = i * CONV_ROWS
            init = jnp.zeros((8, 128), F32) if bias_row is None else wb[bias_row, :, lanes]
            accs = [init] * n_acc
            for r, lst in groups:
                q0, q1 = min(q for _, q in lst), max(q for _, q in lst)
                win = _window(buf, shf, base, r, q0, n_acc + q1 - q0, lanes)
                for k, q in lst:
                    wk = wb[k, :, lanes]
                    accs = [acc + wk * win[h + q - q0] for h, acc in enumerate(accs)]
            for h, acc in enumerate(accs):
                out_ref[pl.ds(base + 8 * h, 8), lanes] = acc
            return carry

        for i in range(tm // CONV_ROWS):
            rows(i, 0)


def _long_conv_grad_taps(buf, shf, x_ref, wacc, tm):
    n_acc = CONV_ROWS // 8
    groups = _taps_by_shift_residue([(KB - 1 - j, j) for j in range(KB)])
    for col in range(D // 128):
        lanes = pl.ds(128 * col, 128)

        def rows(i, carry, lanes=lanes):
            base = i * CONV_ROWS
            xv = x_ref[pl.ds(base, CONV_ROWS), lanes]
            xs = [xv[8 * h:8 * h + 8] for h in range(n_acc)]
            for r, lst in groups:
                q0, q1 = min(q for _, q in lst), max(q for _, q in lst)
                win = _window(buf, shf, base, r, q0, n_acc + q1 - q0, lanes)
                for k, q in lst:
                    prod = [x * win[h + q - q0] for h, x in enumerate(xs)]
                    wacc[k, :, lanes] += functools.reduce(lambda a, b: a + b, prod)
            return carry

        for i in range(tm // CONV_ROWS):
            rows(i, 0)


def _ln_stats(dc):
    mu = jnp.mean(dc, axis=-1, keepdims=True)
    xc = dc - mu
    rstd = lax.rsqrt(jnp.mean(xc * xc, axis=-1, keepdims=True) + LN_EPS)
    return xc * rstd, rstd


def _mixb_fwd(h, vecs, bias1, wg, tm, comm):
    T = h.shape[0]

    def body(h_ref, vec_ref, b1_ref, wg_ref, o_ref, a_ref, g_ref, dc_ref, s_ref, w1, w2, buf, shf, wb, sems):
        first = pl.program_id(0) == 0
        load_w = _weight_loads(wg_ref, [(w1, "b_pw1"), (w2, "b_pw2")], sems)

        @pl.when(first)
        def _():
            load_w()
            buf[pl.ds(0, HALO_B), :] = jnp.zeros((HALO_B, D), F32)
            _broadcast_taps(vec_ref, wb)

        hv = h_ref[...]
        nb = (hv * _rms_stat(hv) * vec_ref[pl.ds(SM_B_NORM, 1), :]).astype(BF16)
        a = _dot_nt(nb, w1[pl.ds(0, D), :]) + b1_ref[:, pl.ds(0, D)]
        g = _dot_nt(nb, w1[pl.ds(D, D), :]) + b1_ref[:, pl.ds(D, D)]
        a_ref[...] = a.astype(BF16)
        g_ref[...] = g.astype(BF16)
        buf[pl.ds(HALO_B, tm), :] = a * _sigmoid(g)
        _shifted_copies(buf, shf, tm)

        _long_conv(buf, shf, wb, dc_ref, tm, [(k, HALO_B - (KB - 1) + k) for k in range(KB)], KB)
        buf[pl.ds(0, HALO_B), :] = buf[pl.ds(tm, HALO_B), :]
        xhat, _ = _ln_stats(dc_ref[...])
        ln = xhat * vec_ref[pl.ds(SM_LN_G, 1), :] + vec_ref[pl.ds(SM_LN_B, 1), :]
        s = (ln * _sigmoid(ln)).astype(BF16)
        s_ref[...] = s
        o_ref[...] = hv + _dot(s, w2[...]) + vec_ref[pl.ds(SM_B_PW2, 1), :]

    act = _sds((T, D), BF16)
    return _hosted_call(
        body, "mixb_fwd", T // tm,
        in_specs=[_row_spec(tm, D), _full_spec((SM_F32_ROWS, D)), _full_spec((1, 2 * D))],
        out_specs=[_row_spec(tm, D) for _ in range(5)],
        out_shape=(_sds((T, D), F32), act, act, _sds((T, D), F32), act),
        scratch=[pltpu.VMEM((2 * D, D), BF16), pltpu.VMEM((D, D), BF16),
                 pltpu.VMEM((HALO_B + tm, D), F32), pltpu.VMEM((7, HALO_B + tm - 8, D), F32),
                 pltpu.VMEM((KB + 1, 8, D), F32), pltpu.SemaphoreType.DMA((2 * NDEV,))],
        args=(h, vecs, bias1), comm=comm)


def _ffn_bwd_dx(dh, h, g, u, gam, wg, layer, tm, comm):
    T = h.shape[0]
    kg, ku, kd = "g%d" % layer, "u%d" % layer, "d%d" % layer

    def body(dh_ref, h_ref, g_ref, u_ref, gam_ref, wg_ref, o_ref, dg_ref, du_ref, st_ref, wgt, wut, wd, sems):
        first = pl.program_id(0) == 0
        load_w = _weight_loads(wg_ref, [(wd, kd), (wgt, kg), (wut, ku)], sems)

        @pl.when(first)
        def _():
            load_w()
            st_ref[...] = jnp.zeros((8, D), F32)

        dhv = dh_ref[...]
        dhb = dhv.astype(BF16)
        dn = jnp.zeros_like(dhv)
        for f in range(2):
            cols = pl.ds(f * FH, FH)
            dgu = _dot_nt(dhb, wd[cols, :])
            gv = g_ref[:, cols].astype(F32)
            sg = _sigmoid(gv)
            du = (dgu * gv * sg).astype(BF16)
            dg = (dgu * u_ref[:, cols].astype(F32) * (sg * (1.0 + gv * (1.0 - sg)))).astype(BF16)
            dg_ref[:, cols] = dg
            du_ref[:, cols] = du
            dn = dn + _dot(dg, wgt[cols, :]) + _dot(du, wut[cols, :])
        hv = h_ref[...]
        dx, dgam = _rms_bwd(dn, hv, _rms_stat(hv), gam_ref[...])
        o_ref[...] = dhv + dx
        st_ref[pl.ds(0, 1), :] += dgam

    pre = _sds((T, FF), BF16)
    return _hosted_call(
        body, "ffn%d_bwd_dx" % layer, T // tm,
        in_specs=[_row_spec(tm, D), _row_spec(tm, D), _row_spec(tm, FF), _row_spec(tm, FF),
                  _full_spec((1, D)), _ANY],
        out_specs=[_row_spec(tm, D), _row_spec(tm, FF), _row_spec(tm, FF), _full_spec((8, D))],
        out_shape=(_sds((T, D), F32), pre, pre, _sds((8, D), F32)),
        scratch=[pltpu.VMEM((FF, D), BF16)] * 3 + [pltpu.SemaphoreType.DMA((3 * NDEV,))],
        args=(dh, h, g, u, gam, wg), comm=comm)


def _grad_w(lhs, rhs, mc, name, tm):
    T, M = lhs.shape
    nt = T // tm

    def body(l_ref, r_ref, o_ref, acc):
        i = pl.program_id(1)

        @pl.when(i == 0)
        def _():
            acc[...] = jnp.zeros((mc, D), F32)

        acc[...] += _dot_tn(l_ref[...], r_ref[...].astype(BF16))

        @pl.when(i == nt - 1)
        def _():
            o_ref[...] = acc[...].astype(BF16)

    return _pcall(
        body, name=name, grid=(M // mc, nt),
        out_shape=_sds((M, D), BF16),
        in_specs=[pl.BlockSpec((tm, mc), lambda j, i: (i, j)), pl.BlockSpec((tm, D), lambda j, i: (i, 0))],
        out_specs=pl.BlockSpec((mc, D), lambda j, i: (j, 0)),
        scratch_shapes=[pltpu.VMEM((mc, D), F32)],
        compiler_params=_cparams(("arbitrary", "arbitrary")),
    )(lhs, rhs)


def _mixb_bwd(dh, h, a, g, dc, s, vecs, wg, tm, comm):
    T = h.shape[0]
    nt = T // tm

    def body(dh_ref, h_ref, a_ref, g_ref, dc_ref, s_ref, vec_ref, wg_ref, o_ref, st_ref, sb_ref, g1_ref, g2_ref,
             w1, w2, buf, shf, glu_s, dglu_s, wacc, wb, acc1, acc2, stage, sems, out_sems):
        first = pl.program_id(0) == 0
        load_w = _weight_loads(wg_ref, [(w2, "b_pw2"), (w1, "b_pw1")], sems)

        @pl.when(first)
        def _():
            load_w()
            buf[pl.ds(tm, HALO_B), :] = jnp.zeros((HALO_B, D), F32)
            wacc[...] = jnp.zeros((KB, 8, D), F32)
            _broadcast_taps(vec_ref, wb)
            st_ref[...] = jnp.zeros((SM_F32_ROWS, D), F32)
            sb_ref[...] = jnp.zeros((8, 2 * D), F32)
            acc1[...] = jnp.zeros((2 * D, D), F32)
            acc2[...] = jnp.zeros((D, D), F32)

        def acc(row, val):
            st_ref[pl.ds(row, 1), :] += jnp.sum(val, axis=0, keepdims=True)

        dhv = dh_ref[...]
        dhb = dhv.astype(BF16)
        acc(SM_B_PW2, dhv)
        acc2[...] += _dot_tn(s_ref[...], dhb)
        ds = _dot_nt(dhb, w2[...])
        xhat, rstd = _ln_stats(dc_ref[...])
        ln_g = vec_ref[pl.ds(SM_LN_G, 1), :]
        ln = xhat * ln_g + vec_ref[pl.ds(SM_LN_B, 1), :]
        sl = _sigmoid(ln)
        dln = ds * (sl * (1.0 + ln * (1.0 - sl)))
        acc(SM_LN_G, dln * xhat)
        acc(SM_LN_B, dln)
        dxh = dln * ln_g
        ddc = rstd * (dxh - jnp.mean(dxh, axis=-1, keepdims=True)
                      - xhat * jnp.mean(dxh * xhat, axis=-1, keepdims=True))
        acc(SM_B_BCONV, ddc)
        buf[pl.ds(0, tm), :] = ddc
        _shifted_copies(buf, shf, tm)
        av = a_ref[...].astype(F32)
        sg = _sigmoid(g_ref[...].astype(F32))
        glu_s[...] = av * sg

        _long_conv(buf, shf, wb, dglu_s, tm, [(KB - 1 - j, j) for j in range(KB)], None)
        _long_conv_grad_taps(buf, shf, glu_s, wacc, tm)
        buf[pl.ds(tm, HALO_B), :] = buf[pl.ds(0, HALO_B), :]
        dglu = dglu_s[...]
        da = dglu * sg
        dg = dglu * av * sg * (1.0 - sg)
        sb_ref[pl.ds(0, 1), pl.ds(0, D)] += jnp.sum(da, axis=0, keepdims=True)
        sb_ref[pl.ds(0, 1), pl.ds(D, D)] += jnp.sum(dg, axis=0, keepdims=True)
        dab, dgb = da.astype(BF16), dg.astype(BF16)
        dn = _dot(dab, w1[pl.ds(0, D), :]) + _dot(dgb, w1[pl.ds(D, D), :])
        hv = h_ref[...]
        r = _rms_stat(hv)
        gamma = vec_ref[pl.ds(SM_B_NORM, 1), :]
        nb = (hv * r * gamma).astype(BF16)
        acc1[pl.ds(0, D), :] += _dot_tn(dab, nb)
        acc1[pl.ds(D, D), :] += _dot_tn(dgb, nb)
        dx, dgam = _rms_bwd(dn, hv, r, gamma)
        o_ref[...] = dhv + dx
        st_ref[pl.ds(SM_B_NORM, 1), :] += dgam

        @pl.when(pl.program_id(0) == nt - 1)
        def _():
            st_ref[pl.ds(SM_B_CONV, KB), :] = jnp.sum(wacc[...], axis=1)
            blocks = [(acc1, 0, g1_ref, 0), (acc1, D, g1_ref, D), (acc2, 0, g2_ref, 0)]
            for k, (src, a0, dst, d0) in enumerate(blocks):
                stage[...] = src[pl.ds(a0, D), :].astype(BF16)
                cp = pltpu.make_async_copy(stage, dst.at[pl.ds(d0, D), :], out_sems.at[k])
                cp.start()
                cp.wait()

    rs = functools.partial(_row_spec, rev_nt=nt)
    return _hosted_call(
        body, "mixb_bwd", nt,
        in_specs=[rs(tm, D) for _ in range(6)] + [_full_spec((SM_F32_ROWS, D)), _ANY],
        out_specs=[rs(tm, D), _full_spec((SM_F32_ROWS, D)), _full_spec((8, 2 * D)), _ANY, _ANY],
        out_shape=(_sds((T, D), F32), _sds((SM_F32_ROWS, D), F32), _sds((8, 2 * D), F32),
                   _sds((2 * D, D), BF16), _sds((D, D), BF16)),
        scratch=[pltpu.VMEM((2 * D, D), BF16), pltpu.VMEM((D, D), BF16),
                 pltpu.VMEM((tm + HALO_B, D), F32), pltpu.VMEM((7, tm + HALO_B - 8, D), F32),
                 pltpu.VMEM((tm, D), F32), pltpu.VMEM((tm, D), F32), pltpu.VMEM((KB, 8, D), F32),
                 pltpu.VMEM((KB + 1, 8, D), F32), pltpu.VMEM((2 * D, D), F32), pltpu.VMEM((D, D), F32),
                 pltpu.VMEM((D, D), BF16), pltpu.SemaphoreType.DMA((2 * NDEV,)), pltpu.SemaphoreType.DMA((3,))],
        args=(dh, h, a, g, dc, s, vecs, wg), comm=comm)


def _mixa_bwd(dh, x, b, c, v, cc, y, gam, cw, wg, tm, comm):
    T = x.shape[0]
    nt = T // tm

    def body(dh_ref, x_ref, b_ref, c_ref, v_ref, cc_ref, y_ref, gam_ref, cw_ref, wg_ref,
             o_ref, st_ref, gin_ref, gout_ref, win, wout, buf, acc_in, acc_out, stage, sems, out_sems):
        first = pl.program_id(0) == 0
        load_w = _weight_loads(wg_ref, [(wout, "a_out"), (win, "a_in")], sems)

        @pl.when(first)
        def _():
            load_w()
            buf[pl.ds(tm, HALO_A), :] = jnp.zeros((HALO_A, D), F32)
            st_ref[...] = jnp.zeros((8, D), F32)
            acc_in[...] = jnp.zeros((3 * D, D), F32)
            acc_out[...] = jnp.zeros((D, D), F32)

        dhv = dh_ref[...]
        dhb = dhv.astype(BF16)
        dy = _dot_nt(dhb, wout[...])
        cval = c_ref[...].astype(F32)
        vval = v_ref[...].astype(F32)
        d_b = (dy * cc_ref[...].astype(F32)).astype(BF16)
        buf[pl.ds(0, tm), :] = dy * b_ref[...].astype(F32)
        cv = cval * vval
        dcv = jnp.zeros((tm, D), F32)
        for j in range(KA):
            sh = buf[pl.ds(j, tm), :]
            k = KA - 1 - j
            dcv = dcv + cw_ref[pl.ds(k, 1), :] * sh
            st_ref[pl.ds(1 + k, 1), :] += jnp.sum(cv * sh, axis=0, keepdims=True)
        buf[pl.ds(tm, HALO_A), :] = buf[pl.ds(0, HALO_A), :]
        d_c = (dcv * vval).astype(BF16)
        d_v = (dcv * cval).astype(BF16)
        xv = x_ref[...]
        r = _rms_stat(xv)
        gamma = gam_ref[...]
        nb = (xv * r * gamma).astype(BF16)
        for j, piece in enumerate((d_b, d_c, d_v)):
            acc_in[pl.ds(j * D, D), :] += _dot_tn(piece, nb)
        acc_out[...] += _dot_tn(y_ref[...], dhb)
        dn = _dot(d_b, win[pl.ds(0, D), :]) + _dot(d_c, win[pl.ds(D, D), :]) + _dot(d_v, win[pl.ds(2 * D, D), :])
        dx, dgam = _rms_bwd(dn, xv, r, gamma)
        o_ref[...] = dhv + dx
        st_ref[pl.ds(0, 1), :] += dgam

        @pl.when(pl.program_id(0) == nt - 1)
        def _():
            blocks = [(acc_in, j * D, gin_ref, j * D) for j in range(3)] + [(acc_out, 0, gout_ref, 0)]
            for k, (acc, a0, dst, d0) in enumerate(blocks):
                stage[...] = acc[pl.ds(a0, D), :].astype(BF16)
                cp = pltpu.make_async_copy(stage, dst.at[pl.ds(d0, D), :], out_sems.at[k])
                cp.start()
                cp.wait()

    rs = functools.partial(_row_spec, rev_nt=nt)
    return _hosted_call(
        body, "mixa_bwd", nt,
        in_specs=[rs(tm, D) for _ in range(7)] + [_full_spec((1, D)), _full_spec((KA, D)), _ANY],
        out_specs=[rs(tm, D), _full_spec((8, D)), _ANY, _ANY],
        out_shape=(_sds((T, D), F32), _sds((8, D), F32), _sds((3 * D, D), BF16), _sds((D, D), BF16)),
        scratch=[pltpu.VMEM((3 * D, D), BF16), pltpu.VMEM((D, D), BF16), pltpu.VMEM((tm + HALO_A, D), F32),
                 pltpu.VMEM((3 * D, D), F32), pltpu.VMEM((D, D), F32), pltpu.VMEM((D, D), BF16),
                 pltpu.SemaphoreType.DMA((2 * NDEV,)), pltpu.SemaphoreType.DMA((4,))],
        args=(dh, x, b, c, v, cc, y, gam, cw, wg), comm=comm)


def _sum_small(sm, rp):
    def body(sm_ref, rp_ref, osm, orp, oloss):
        a = sm_ref[0]
        b = rp_ref[0]
        for s in range(1, NDEV):
            a = a + sm_ref[s]
            b = b + rp_ref[s]
        osm[...] = a
        orp[...] = b
        oloss[...] = jnp.zeros((8, 128), F32) + jnp.sum(b[4:5, :], axis=-1, keepdims=True)

    return _pcall(
        body, name="sum_small_grads",
        out_shape=(_sds((SM_F32_ROWS, 128), F32), _sds((8, D), F32), _sds((8, 128), F32)),
        in_specs=[pl.BlockSpec(memory_space=pltpu.VMEM)] * 2,
        out_specs=tuple(pl.BlockSpec(memory_space=pltpu.VMEM) for _ in range(3)),
        compiler_params=_cparams(),
    )(sm, rp)


def _adam_math(w, g, m, v):
    mn = B1 * m + (1.0 - B1) * g
    vn = B2 * v + (1.0 - B2) * (g * g)
    m_hat = mn / (1.0 - B1 ** STEP)
    v_hat = vn / (1.0 - B2 ** STEP)
    return -LR * (m_hat / (jnp.sqrt(v_hat) + ADAM_EPS) + WD * w), mn, vn


def _finish_weight(land, keys, transposed, w, m, v, name, comm=None):
    layers, rows, cols = w.shape
    n = N_ROWS[keys[0]]
    n_pad = -(-n // 128) * 128 if transposed else n

    def body(w_ref, m_ref, v_ref, land_ref, og, od, om, ov, buf, sem):
        off = LD_OFF[keys[0]]
        if layers == 2:
            off = jnp.where(pl.program_id(0) == 0, off, LD_OFF[keys[1]])
        cp = pltpu.make_async_copy(land_ref.at[:, pl.ds(pl.multiple_of(off, 32), n), :], buf, sem)
        cp.start()
        cp.wait()
        g = buf[0].astype(F32)
        for s in range(1, NCHIP):
            g = g + buf[s].astype(F32)
        if transposed:
            if n_pad != n:
                g = jnp.concatenate([g, jnp.zeros((n_pad - n, D), F32)], axis=0)
            g = g.T[:, :n]
        d, mn, vn = _adam_math(w_ref[...], g, m_ref[...], v_ref[...])
        og[...] = g
        od[...] = d
        om[...] = mn
        ov[...] = vn

    spec = pl.BlockSpec((None, rows, cols), lambda l: (l, 0, 0))
    shp = _sds(w.shape, F32)
    in_specs, args = [spec, spec, spec], (w, m, v)
    if comm is None:
        in_specs, args = in_specs + [_ANY], args + (land,)
    return _hosted_call(
        body, "finish_" + name, layers,
        in_specs=in_specs, out_specs=[spec] * 4, out_shape=(shp,) * 4,
        scratch=[pltpu.VMEM((NCHIP, n, D), BF16), pltpu.SemaphoreType.DMA],
        args=args, comm=comm)


SMALL_PARAMS = (("sm", SM_B_NORM, 1), ("sm", SM_B_PW1, 2), ("sm", SM_B_CONV, KB), ("sm", SM_B_BCONV, 1),
                ("sm", SM_LN_G, 1), ("sm", SM_LN_B, 1), ("sm", SM_B_PW2, 1), ("sm", SM_A_CONV, KA),
                ("rp", 0, 1), ("rp", 1, 2), ("rp", 3, 1))


def _adamw_small(g_small, g_repl, triples):
    n = len(SMALL_PARAMS)

    def body(*refs):
        gs_ref, gr_ref = refs[0], refs[1]
        ins = refs[2:2 + 3 * n]
        outs = refs[2 + 3 * n:]
        for i, (src, r0, nr) in enumerate(SMALL_PARAMS):
            w_ref, m_ref, v_ref = ins[3 * i:3 * i + 3]
            g = (gs_ref if src == "sm" else gr_ref)[pl.ds(r0, nr), :]
            if i == 1:
                g = jnp.concatenate([g[0:1], g[1:2]], axis=1)
            lead = (0,) if len(w_ref.shape) == 3 else ()
            idx = lead + (slice(None), slice(None))
            vals = (g,) + _adam_math(w_ref[idx], g, m_ref[idx], v_ref[idx])
            for o_ref, val in zip(outs[4 * i:4 * i + 4], vals):
                o_ref[idx] = val

    flat = [a for t in triples for a in t]
    out_shape = tuple(_sds(t[0].shape, F32) for t in triples for _ in range(4))
    vm = pl.BlockSpec(memory_space=pltpu.VMEM)
    res = _pcall(
        body, name="adamw_small", out_shape=out_shape,
        in_specs=[vm] * (2 + len(flat)), out_specs=tuple(vm for _ in out_shape),
        compiler_params=_cparams(),
    )(g_small, g_repl, *flat)
    return [tuple(res[4 * i:4 * i + 4]) for i in range(n)]


def kernel(x, a_norm, a_w_in, a_conv, a_w_out, b_norm, b_w_pw1, b_b_pw1, b_conv, b_b_conv, b_ln_g, b_ln_b, b_w_pw2, b_b_pw2, ffn_norm, ffn_w_gate, ffn_w_up, ffn_w_down, final_norm, loss_target, m_a_norm, m_a_w_in, m_a_conv, m_a_w_out, m_b_norm, m_b_w_pw1, m_b_b_pw1, m_b_conv, m_b_b_conv, m_b_ln_g, m_b_ln_b, m_b_w_pw2, m_b_b_pw2, m_ffn_norm, m_ffn_w_gate, m_ffn_w_up, m_ffn_w_down, m_final_norm, v_a_norm, v_a_w_in, v_a_conv, v_a_w_out, v_b_norm, v_b_w_pw1, v_b_b_pw1, v_b_conv, v_b_b_conv, v_b_ln_g, v_b_ln_b, v_b_w_pw2, v_b_b_pw2, v_ffn_norm, v_ffn_w_gate, v_ffn_w_up, v_ffn_w_down, v_final_norm):
    T = x.shape[1]
    tm = min(TM, T)
    tma = min(TM_A, T)
    tw = min(TM_DW, T)
    xs = x.reshape(T, D)
    tgt = loss_target.reshape(T, D)

    def rows_first(a):
        return jnp.swapaxes(a, 1, 2)

    shard, small = _pack_shard(
        (a_w_in, a_w_out, b_w_pw1, b_w_pw2, rows_first(ffn_w_gate), rows_first(ffn_w_up), ffn_w_down),
        (b_norm, b_b_pw1, b_conv, b_b_conv, b_ln_g, b_ln_b, b_b_pw2, a_conv))
    wg, sm_all = _all_gather_first(shard, small)
    vecs = sm_all.transpose(1, 0, 2).reshape(SM_F32_ROWS, D)
    bias1 = sm_all[:, SM_B_PW1:SM_B_PW1 + 2, :].reshape(1, 2 * D)
    cw_a = vecs[SM_A_CONV:SM_A_CONV + KA]
    fn0, fn1 = ffn_norm[0:1], ffn_norm[1:2]
    fin = final_norm.reshape(1, D)

    h1, bq, cq, vq, ccq, yq, wg = _mixa_fwd(xs, a_norm, cw_a, wg, tma, _ag_comm(shard, wg, ("g0", "u0", "d0")))
    h2, n1, g0, u0, gu0, wg = _ffn_fwd(h1, fn0, wg, 0, tma, _ag_comm(shard, wg, ("b_pw1", "b_pw2"), (T // tma) // 2))
    h3, aq, gq, dcq, sq, wg = _mixb_fwd(h2, vecs, bias1, wg, tm, _ag_comm(shard, wg, ("g1", "u1", "d1"), 1))
    dh4, n3, g1, u1, gu1, st_fin = _ffn_fwd(h3, fn1, wg, 1, tma, None, head=(tgt, fin))

    def by_dest(gw, key):
        return gw.reshape(NDEV, N_ROWS[key], D)

    gw_d1 = by_dest(_grad_w(gu1, dh4, FH, "grad_down1", tw), "d1")
    dh3, dg1, du1, st_f1 = _ffn_bwd_dx(dh4, h3, g1, u1, fn1, wg, 1, tma, None)
    gw_g1 = by_dest(_grad_w(dg1, n3, FH, "grad_gate1", tw), "g1")
    gw_u1 = by_dest(_grad_w(du1, n3, FH, "grad_up1", tw), "u1")
    keys = ("g1", "u1", "d1")
    dh2, st_b, st_b1, gw_pw1, gw_pw2, land = _mixb_bwd(
        dh3, h2, aq, gq, dcq, sq, vecs, wg, tm, _rs_comm(_pair_reduce([gw_g1, gw_u1, gw_d1], keys), keys, None))
    gw_pw1, gw_pw2 = by_dest(gw_pw1, "b_pw1"), by_dest(gw_pw2, "b_pw2")
    gw_d0 = by_dest(_grad_w(gu0, dh2, FH, "grad_down0", tw), "d0")
    keys = ("b_pw1", "b_pw2", "d0")
    dh1, dg0, du0, st_f0, land = _ffn_bwd_dx(
        dh2, h1, g0, u0, fn0, wg, 0, tma, _rs_comm(_pair_reduce([gw_pw1, gw_pw2, gw_d0], keys), keys, land))
    gw_g0 = by_dest(_grad_w(dg0, n1, FH, "grad_gate0", tw), "g0")
    gw_u0 = by_dest(_grad_w(du0, n1, FH, "grad_up0", tw), "u0")
    keys = ("g0", "u0")
    dx, st_a, gw_in, gw_out, land = _mixa_bwd(
        dh1, xs, bq, cq, vq, ccq, yq, a_norm, cw_a, wg, tm,
        _rs_comm(_pair_reduce([gw_g0, gw_u0], keys), keys, land))
    gw_in, gw_out = by_dest(gw_in, "a_in"), by_dest(gw_out, "a_out")

    st_small = st_b.at[SM_A_CONV:SM_A_CONV + KA].set(st_a[1:1 + KA])
    sm_dest = st_small.reshape(SM_F32_ROWS, NDEV, 128).transpose(1, 0, 2)
    sm_dest = sm_dest.at[:, SM_B_PW1:SM_B_PW1 + 2, :].set(st_b1[0].reshape(NDEV, 2, 128))
    repl = jnp.concatenate([st_a[0:1], st_f0[0:1], st_f1[0:1], st_fin[0:1], st_fin[1:2],
                            jnp.zeros((3, D), F32)], axis=0)[None]
    last = ("a_out", "a_in")
    p_last, sm_land, rp_land = _pair_reduce([gw_out, gw_in], last, side=(sm_dest, repl))

    big = [("ffn_w_gate", ("g0", "g1"), False,
            rows_first(ffn_w_gate), rows_first(m_ffn_w_gate), rows_first(v_ffn_w_gate)),
           ("ffn_w_up", ("u0", "u1"), False, rows_first(ffn_w_up), rows_first(m_ffn_w_up), rows_first(v_ffn_w_up)),
           ("ffn_w_down", ("d0", "d1"), False, ffn_w_down, m_ffn_w_down, v_ffn_w_down),
           ("a_w_out", ("a_out",), False, a_w_out, m_a_w_out, v_a_w_out),
           ("b_w_pw1", ("b_pw1",), True, b_w_pw1, m_b_w_pw1, v_b_w_pw1),
           ("b_w_pw2", ("b_pw2",), False, b_w_pw2, m_b_w_pw2, v_b_w_pw2),
           ("a_w_in", ("a_in",), True, a_w_in, m_a_w_in, v_a_w_in)]
    res = {}
    for k, (name, keys, transposed, w, m, v) in enumerate(big):
        if k < 3:
            comm = _rs_comm(p_last, last, land, part=((0, 176), (176, 176), (352, 160))[k], body_reads_land=True)
            *res[name], land = _finish_weight(land, keys, transposed, w, m, v, name, comm)
        else:
            res[name] = _finish_weight(land, keys, transposed, w, m, v, name)
    for name in ("ffn_w_gate", "ffn_w_up"):
        res[name] = [rows_first(a) for a in res[name]]

    g_small, g_repl, loss8 = _sum_small(sm_land, rp_land)
    loss = loss8[0, 0]

    small_names = ["b_norm", "b_b_pw1", "b_conv", "b_b_conv", "b_ln_g", "b_ln_b", "b_b_pw2", "a_conv",
                   "a_norm", "ffn_norm", "final_norm"]
    triples = [(b_norm, m_b_norm, v_b_norm), (b_b_pw1, m_b_b_pw1, v_b_b_pw1), (b_conv, m_b_conv, v_b_conv),
               (b_b_conv, m_b_b_conv, v_b_b_conv), (b_ln_g, m_b_ln_g, v_b_ln_g), (b_ln_b, m_b_ln_b, v_b_ln_b),
               (b_b_pw2, m_b_b_pw2, v_b_b_pw2), (a_conv, m_a_conv, v_a_conv), (a_norm, m_a_norm, v_a_norm),
               (ffn_norm, m_ffn_norm, v_ffn_norm),
               (fin, m_final_norm.reshape(1, D), v_final_norm.reshape(1, D))]
    for name, quad in zip(small_names, _adamw_small(g_small, g_repl, triples)):
        res[name] = quad
    res["final_norm"] = tuple(a.reshape(D) for a in res["final_norm"])

    order = ["a_norm", "a_w_in", "a_conv", "a_w_out", "b_norm", "b_w_pw1", "b_b_pw1", "b_conv", "b_b_conv",
             "b_ln_g", "b_ln_b", "b_w_pw2", "b_b_pw2", "ffn_norm", "ffn_w_gate", "ffn_w_up", "ffn_w_down",
             "final_norm"]
    out = [loss, dx.reshape(1, T, D)]
    for j in range(4):
        out += [res[k][j] for k in order]
    return tuple(out)
```

```python
import functools

import jax
import jax.numpy as jnp
from jax import lax
from jax.experimental import pallas as pl
from jax.experimental.pallas import tpu as pltpu

F32 = jnp.float32
BF16 = jnp.bfloat16
MESH = pl.DeviceIdType.MESH

D = 1024
FF = 2816
FH = FF // 2
NDEV = 8
KA = 3
KB = 31
HALO_A = 8
HALO_B = 32
CONV_ROWS = 64
RMS_EPS = 1e-6
LN_EPS = 1e-5
LR, B1, B2, ADAM_EPS, WD, STEP = 0.001, 0.9, 0.999, 1e-08, 0.01, 10

TM = 256
TM_A = 512
TM_DW = 1024
VMEM_LIMIT = 62 * 1024 * 1024

N_ROWS = {"a_in": 384, "a_out": 128, "b_pw1": 256, "b_pw2": 128, "g0": 352, "u0": 352, "d0": 352,
          "g1": 352, "u1": 352, "d1": 352}
WEIGHT_KEYS = ("a_in", "a_out", "b_pw1", "b_pw2", "g0", "u0", "d0", "g1", "u1", "d1")
NCHIP = 4


def _offsets(order):
    off, o = {}, 0
    for k in order:
        off[k] = o
        o += N_ROWS[k]
    return off, o


PK_ORDER = ("a_in", "a_out", "g0", "u0", "d0", "b_pw1", "b_pw2", "g1", "u1", "d1")
PK_OFF, PK_ROWS = _offsets(PK_ORDER)
LD_ORDER = ("g1", "u1", "d1", "b_pw1", "b_pw2", "d0", "g0", "u0", "a_out", "a_in")
LD_OFF, N_WROWS = _offsets(LD_ORDER)


def _span(off, keys):
    return off[keys[0]], sum(N_ROWS[k] for k in keys)


SM_B_NORM, SM_B_PW1, SM_B_CONV, SM_B_BCONV, SM_LN_G, SM_LN_B, SM_B_PW2, SM_A_CONV = 0, 1, 3, 34, 35, 36, 37, 38
SM_USED = 41
SM_F32_ROWS = 64


def _pcall(body, **kw):
    return pl.pallas_call(body, **kw)


def _cparams(sem=None):
    return pltpu.CompilerParams(dimension_semantics=sem, vmem_limit_bytes=VMEM_LIMIT)


def _dot(a, b):
    return jnp.dot(a, b, preferred_element_type=F32)


def _dot_nt(a, b):
    return lax.dot_general(a, b, (((1,), (1,)), ((), ())), preferred_element_type=F32)


def _dot_tn(a, b):
    return lax.dot_general(a, b, (((0,), (0,)), ((), ())), preferred_element_type=F32)


def _sigmoid(v):
    return 1.0 / (1.0 + jnp.exp(-v))


def _rms_stat(x):
    return lax.rsqrt(jnp.mean(x * x, axis=-1, keepdims=True) + RMS_EPS)


def _rms_bwd(dn, x, r, gamma):
    dng = dn * gamma
    dx = r * dng - x * (r * r * r) * jnp.mean(dng * x, axis=-1, keepdims=True)
    return dx, jnp.sum(dn * x * r, axis=0, keepdims=True)


def _weight_loads(wg_ref, plan, sems):
    copies = []
    for j, (dst, key) in enumerate(plan):
        off, n = PK_OFF[key], N_ROWS[key]
        copies += [pltpu.make_async_copy(
            wg_ref.at[d, pl.ds(off, n), :], dst.at[pl.ds(d * n, n), :], sems.at[j * NDEV + d])
            for d in range(NDEV)]

    def load():
        for cp in copies:
            cp.start()
        for cp in copies:
            cp.wait()

    return load


_ANY = pl.BlockSpec(memory_space=pl.ANY)


def _row_spec(tm, width, rev_nt=None):
    if rev_nt is None:
        return pl.BlockSpec((tm, width), lambda i: (i, 0))
    return pl.BlockSpec((tm, width), lambda i: (rev_nt - 1 - i, 0))


def _full_spec(shape):
    return pl.BlockSpec(shape, lambda *_: (0,) * len(shape))


def _sds(shape, dtype):
    return jax.ShapeDtypeStruct(shape, dtype)


def _mesh_pos():
    return lax.axis_index("x"), lax.axis_index("y"), lax.axis_index("c")


def _lin(p):
    return 4 * p[0] + 2 * p[1] + p[2]


def _ag_exchange(src, slot, send_sems, recv_sems, local_sem):
    x, y, c = _mesh_pos()
    me, sibling = (x, y, c), (x, y, 1 - c)
    chips = [(1 - x, y), (x, 1 - y), (1 - x, 1 - y)]

    def copy(k, block, to, own=False):
        return pltpu.make_async_remote_copy(
            src_ref=src if own else slot(block), dst_ref=slot(block),
            send_sem=send_sems.at[k], recv_sem=recv_sems.at[k], device_id=to, device_id_type=MESH)

    mine = pltpu.make_async_copy(src, slot(me), local_sem)
    first = [copy(0, me, sibling, own=True)]
    first += [copy(1 + j, me, (*chip, c), own=True) for j, chip in enumerate(chips)]
    passed = [copy(4 + j, (*chip, c), sibling) for j, chip in enumerate(chips)]

    def start():
        mine.start()
        for cp in first:
            cp.start()

    def forward():
        for j, chip in enumerate(chips):
            copy(1 + j, (*chip, c), me).wait_recv()
            passed[j].start()

    def finish():
        copy(0, sibling, me).wait_recv()
        for j, chip in enumerate(chips):
            copy(4 + j, (*chip, 1 - c), me).wait_recv()
        for cp in first + passed:
            cp.wait_send()
        mine.wait()

    return start, forward, finish


def _chip_exchange(p_ref, land_ref, send_sems, recv_sems, keys, part):
    x, y, c = _mesh_pos()
    l0, rows = _span(LD_OFF, keys)
    p0, rows = (0, rows) if part is None else part
    dst = land_ref.at[2 * x + y, pl.ds(l0 + p0, rows), :]
    peers = [((x + (j >> 1)) % 2, (y + (j & 1)) % 2) for j in range(NCHIP)]

    def copy(j):
        tx, ty = peers[j]
        src = p_ref.at[2 * tx + ty, pl.ds(p0, rows), :]
        if j == 0:
            return pltpu.make_async_copy(src, dst, send_sems.at[0])
        return pltpu.make_async_remote_copy(
            src_ref=src, dst_ref=dst, send_sem=send_sems.at[j], recv_sem=recv_sems.at[j],
            device_id=(tx, ty, c), device_id_type=MESH)

    def start():
        for j in range(NCHIP):
            copy(j).start()

    def finish():
        for j in range(NCHIP):
            copy(j).wait()

    return start, finish


def _all_to_all_f32(src_for, dst_ref, send_sems, recv_sems):
    x, y, c = _mesh_pos()
    dst = dst_ref.at[_lin((x, y, c))]
    peers = [((x + ((j >> 2) & 1)) % 2, (y + ((j >> 1) & 1)) % 2, (c + (j & 1)) % 2) for j in range(NDEV)]

    def copy(j):
        if j == 0:
            return pltpu.make_async_copy(src_for(_lin(peers[0])), dst, send_sems.at[0])
        return pltpu.make_async_remote_copy(
            src_ref=src_for(_lin(peers[j])), dst_ref=dst, send_sem=send_sems.at[j], recv_sem=recv_sems.at[j],
            device_id=peers[j], device_id_type=MESH)

    def start():
        for j in range(NDEV):
            copy(j).start()

    def finish():
        for j in range(NDEV):
            copy(j).wait()

    return start, finish


class _Comm:
    def __init__(self, ins, alias_in, out_shape, scratch, make, gives_wg, middle_step=0):
        self.ins, self.alias_in, self.out_shape = ins, alias_in, out_shape
        self.scratch, self.make, self.gives_wg, self.middle_step = scratch, make, gives_wg, middle_step


def _ag_comm(shard, wg, keys, forward_step=0):
    def make(c_ins, c_out, sc):
        return _weights_exchange(c_ins[0], c_out, sc[0], sc[1], sc[2], keys)

    return _Comm([shard, wg], 1, _sds(wg.shape, BF16),
                 [pltpu.SemaphoreType.DMA((7,)), pltpu.SemaphoreType.DMA((7,)), pltpu.SemaphoreType.DMA],
                 make, True, forward_step)


def _weights_exchange(shard_ref, wg_ref, send_sems, recv_sems, local_sem, keys):
    r0, nr = _span(PK_OFF, keys)
    return _ag_exchange(shard_ref.at[pl.ds(r0, nr), :], lambda p: wg_ref.at[_lin(p), pl.ds(r0, nr), :],
                        send_sems, recv_sems, local_sem)


def _rs_comm(psum, keys, land, part=None, body_reads_land=False):
    def make(c_ins, c_out, sc):
        start, finish = _chip_exchange(c_ins[0], c_out, sc[0], sc[1], keys, part)
        return start, None, finish

    ins = [psum] + ([] if land is None else [land])
    return _Comm(ins, None if land is None else 1, _sds((NCHIP, N_WROWS, D), BF16),
                 [pltpu.SemaphoreType.DMA((NCHIP,)), pltpu.SemaphoreType.DMA((NCHIP,))], make, body_reads_land)


def _hosted_call(body, name, nt, in_specs, out_specs, out_shape, scratch, args, comm):
    if comm is None:
        return _pcall(body, name=name, grid=(nt,), in_specs=in_specs, out_specs=tuple(out_specs),
                      out_shape=tuple(out_shape), scratch_shapes=scratch,
                      compiler_params=_cparams(("arbitrary",)))(*args)
    n_in, n_out, n_sc, n_cin = len(in_specs), len(out_specs), len(scratch), len(comm.ins)

    def wrapped(*refs):
        ins = refs[:n_in]
        c_ins = refs[n_in:n_in + n_cin]
        outs = refs[n_in + n_cin:n_in + n_cin + n_out]
        c_out = refs[n_in + n_cin + n_out]
        sc = refs[n_in + n_cin + n_out + 1:n_in + n_cin + n_out + 1 + n_sc]
        c_sc = refs[n_in + n_cin + n_out + 1 + n_sc:]
        start, middle, finish = comm.make(c_ins, c_out, c_sc)
        pl.when(pl.program_id(0) == 0)(start)
        if comm.gives_wg:
            body(*ins, c_out, *outs, *sc)
        else:
            body(*ins, *outs, *sc)
        if middle is not None:
            pl.when(pl.program_id(0) == max(nt - 1 - comm.middle_step, 0))(middle)
        pl.when(pl.program_id(0) == nt - 1)(finish)

    aliases = {} if comm.alias_in is None else {n_in + comm.alias_in: n_out}
    res = _pcall(wrapped, name=name, grid=(nt,),
                 in_specs=list(in_specs) + [_ANY] * n_cin, out_specs=tuple(out_specs) + (_ANY,),
                 out_shape=tuple(out_shape) + (comm.out_shape,),
                 scratch_shapes=list(scratch) + list(comm.scratch),
                 input_output_aliases=aliases,
                 compiler_params=_cparams(("arbitrary",)))(*args, *comm.ins)
    return res


def _pack_shard(weights, smalls):
    plan = (("a_in", 0, 0, True), ("a_out", 1, 0, False), ("b_pw1", 2, 0, True), ("b_pw2", 3, 0, False),
            ("g0", 4, 0, False), ("u0", 5, 0, False), ("d0", 6, 0, False),
            ("g1", 4, 1, False), ("u1", 5, 1, False), ("d1", 6, 1, False))
    n_pad = 384

    def body(*refs):
        out, sm, pad = refs[-3:]
        for key, idx, layer, transposed in plan:
            n = N_ROWS[key]
            val = refs[idx][layer]
            if transposed:
                if n % 128:
                    pad[:, pl.ds(0, n)] = val
                    pad[:, pl.ds(n, n_pad - n)] = jnp.zeros((D, n_pad - n), F32)
                    val = pad[...]
                val = val.T[:n]
            out[pl.ds(PK_OFF[key], n), :] = val.astype(BF16)
        b_norm, b_b_pw1, b_conv, b_b_conv, b_ln_g, b_ln_b, b_b_pw2, a_conv = refs[len(weights):-3]
        sm[...] = jnp.zeros((SM_F32_ROWS, 128), F32)
        sm[pl.ds(SM_B_PW1, 1), :] = b_b_pw1[:, pl.ds(0, 128)]
        sm[pl.ds(SM_B_PW1 + 1, 1), :] = b_b_pw1[:, pl.ds(128, 128)]
        sm[pl.ds(SM_B_CONV, KB), :] = b_conv[0]
        sm[pl.ds(SM_A_CONV, KA), :] = a_conv[0]
        for row, ref in ((SM_B_NORM, b_norm), (SM_B_BCONV, b_b_conv), (SM_LN_G, b_ln_g), (SM_LN_B, b_ln_b),
                         (SM_B_PW2, b_b_pw2)):
            sm[pl.ds(row, 1), :] = ref[...]

    vm = pl.BlockSpec(memory_space=pltpu.VMEM)
    return _pcall(
        body, name="pack_shard",
        out_shape=(_sds((PK_ROWS, D), BF16), _sds((SM_F32_ROWS, 128), F32)),
        in_specs=[vm] * (len(weights) + len(smalls)), out_specs=(vm, vm),
        scratch_shapes=[pltpu.VMEM((D, n_pad), F32)],
        compiler_params=_cparams(),
    )(*weights, *smalls)


def _all_gather_first(shard, small):
    def body(x_ref, s_ref, wg_ref, sg_ref, send_w, recv_w, local_w, send_s, recv_s, local_s):
        start_w, forward_w, finish_w = _weights_exchange(x_ref, wg_ref, send_w, recv_w, local_w, ("a_in", "a_out"))
        start_s, forward_s, finish_s = _ag_exchange(s_ref, lambda p: sg_ref.at[_lin(p)], send_s, recv_s, local_s)
        start_s()
        start_w()
        forward_s()
        forward_w()
        finish_s()
        finish_w()

    sems = [pltpu.SemaphoreType.DMA((7,)), pltpu.SemaphoreType.DMA((7,)), pltpu.SemaphoreType.DMA]
    return _pcall(
        body, name="all_gather_first",
        out_shape=(_sds((NDEV, PK_ROWS, D), BF16), _sds((NDEV, SM_F32_ROWS, 128), F32)),
        in_specs=[_ANY, _ANY], out_specs=(_ANY, _ANY), scratch_shapes=sems + sems,
    )(shard, small)


def _pair_reduce(grads, keys, side=None, land=None):
    l0, rows = _span(LD_OFF, keys)
    nk = len(keys)
    n_side = 0 if side is None else 2
    n_land = 0 if land is None else 1

    def body(c_ref, *refs):
        mine = refs[:nk]
        whole = refs[nk:2 * nk]
        n_in = 2 * nk + n_side + n_land
        o_ref = refs[n_in]
        got, send_sem, recv_sems = refs[n_in + 1 + n_side:n_in + 4 + n_side]
        t = pl.program_id(0)
        x, y, c = _mesh_pos()
        sibling = (x, y, 1 - c)
        if side is not None:
            sm_ref, rp_ref = refs[2 * nk:2 * nk + 2]
            sm_land, rp_land = refs[n_in + 1:n_in + 3]
            s2, r2, s3, r3 = refs[n_in + 6:n_in + 10]
            start_s, finish_s = _all_to_all_f32(lambda d: sm_ref.at[d], sm_land, s2, r2)
            start_r, finish_r = _all_to_all_f32(lambda d: rp_ref.at[0], rp_land, s3, r3)

        @pl.when(t == 0)
        def _():
            if side is not None:
                start_s()
                start_r()
            for ref, key in zip(whole, keys):
                for d in range(NCHIP):
                    pltpu.make_async_remote_copy(
                        src_ref=ref.at[2 * d + (1 - c)], dst_ref=got.at[d, pl.ds(LD_OFF[key] - l0, N_ROWS[key]), :],
                        send_sem=send_sem, recv_sem=recv_sems.at[d], device_id=sibling, device_id_type=MESH).start()

        pltpu.make_async_remote_copy(src_ref=got.at[t], dst_ref=got.at[t], send_sem=send_sem,
                                     recv_sem=recv_sems.at[t], device_id=sibling, device_id_type=MESH).wait_recv()
        sums = o_ref if land is None else refs[-3].at[t]
        for ref, key in zip(mine, keys):
            sl = pl.ds(LD_OFF[key] - l0, N_ROWS[key])
            sums[sl, :] = (ref[...].astype(F32) + got[t, sl, :].astype(F32)).astype(BF16)

        if land is not None:
            psum, chip_send, chip_recv = refs[-3:]
            my_chip = 2 * x + y
            dst = o_ref.at[my_chip, pl.ds(l0, rows), :]
            hop = jnp.bitwise_xor(t, my_chip)

            @pl.when(t == my_chip)
            def _():
                pltpu.make_async_copy(psum.at[t], dst, chip_send.at[0]).start()

            @pl.when(t != my_chip)
            def _():
                pltpu.make_async_remote_copy(
                    src_ref=psum.at[t], dst_ref=dst, send_sem=chip_send.at[hop], recv_sem=chip_recv.at[hop],
                    device_id=(t // 2, t % 2, c), device_id_type=MESH).start()

        @pl.when(t == NCHIP - 1)
        def _():
            pltpu.make_async_remote_copy(src_ref=got, dst_ref=got, send_sem=send_sem, recv_sem=recv_sems.at[0],
                                         device_id=sibling, device_id_type=MESH).wait_send()
            if land is not None:
                one = o_ref.at[0, pl.ds(l0, rows), :]
                pltpu.make_async_copy(one, one, chip_send.at[0]).wait()
                for j in range(1, NCHIP):
                    pltpu.make_async_remote_copy(src_ref=one, dst_ref=one, send_sem=chip_send.at[j],
                                                 recv_sem=chip_recv.at[j], device_id=sibling, device_id_type=MESH).wait()
            if side is not None:
                finish_s()
                finish_r()

    if land is None:
        out_specs = [pl.BlockSpec((None, rows, D), lambda t, c: (t, 0, 0))]
        out_shape = [_sds((NCHIP, rows, D), BF16)]
    else:
        out_specs, out_shape = [_ANY], [_sds((NCHIP, N_WROWS, D), BF16)]
    scratch = [pltpu.VMEM((NCHIP, rows, D), BF16), pltpu.SemaphoreType.DMA, pltpu.SemaphoreType.DMA((NCHIP,))]
    if side is not None:
        out_specs += [_ANY, _ANY]
        out_shape += [_sds((NDEV, SM_F32_ROWS, 128), F32), _sds((NDEV, 8, D), F32)]
        scratch += [pltpu.SemaphoreType.DMA((NDEV,))] * 4
    if land is not None:
        scratch += [pltpu.VMEM((NCHIP, rows, D), BF16), pltpu.SemaphoreType.DMA((NCHIP,)),
                    pltpu.SemaphoreType.DMA((NCHIP,))]
    grid_spec = pltpu.PrefetchScalarGridSpec(
        num_scalar_prefetch=1, grid=(NCHIP,),
        in_specs=[pl.BlockSpec((None, N_ROWS[k], D), lambda t, c: (2 * t + c[0], 0, 0)) for k in keys]
        + [_ANY] * (nk + n_side + n_land),
        out_specs=tuple(out_specs), scratch_shapes=scratch)
    core = lax.axis_index("c").astype(jnp.int32).reshape(1)
    aliases = {} if land is None else {1 + 2 * nk + n_side: 0}
    res = _pcall(
        body, name="pair_reduce_" + keys[0], grid_spec=grid_spec, out_shape=tuple(out_shape),
        input_output_aliases=aliases, compiler_params=_cparams(("arbitrary",)),
    )(core, *grads, *grads, *(side or ()), *(() if land is None else (land,)))
    return res[0] if len(res) == 1 else res


def _mixa_fwd(x, gam, cw, wg, tm, comm):
    T = x.shape[0]

    def body(x_ref, gam_ref, cw_ref, wg_ref, h_ref, b_ref, c_ref, v_ref, cc_ref, y_ref,
             win, wout, buf, sems):
        first = pl.program_id(0) == 0
        load_w = _weight_loads(wg_ref, [(win, "a_in"), (wout, "a_out")], sems)

        @pl.when(first)
        def _():
            load_w()
            buf[pl.ds(0, HALO_A), :] = jnp.zeros((HALO_A, D), F32)

        xv = x_ref[...]
        nb = (xv * _rms_stat(xv) * gam_ref[...]).astype(BF16)
        bv = _dot_nt(nb, win[pl.ds(0, D), :])
        cval = _dot_nt(nb, win[pl.ds(D, D), :])
        vval = _dot_nt(nb, win[pl.ds(2 * D, D), :])
        cv = cval * vval
        buf[pl.ds(HALO_A, tm), :] = cv
        cc = cw_ref[pl.ds(KA - 1, 1), :] * cv
        for k in range(KA - 1):
            cc = cc + cw_ref[pl.ds(k, 1), :] * buf[pl.ds(HALO_A - (KA - 1) + k, tm), :]
        buf[pl.ds(0, HALO_A), :] = buf[pl.ds(tm, HALO_A), :]
        yb = (bv * cc).astype(BF16)
        b_ref[...] = bv.astype(BF16)
        c_ref[...] = cval.astype(BF16)
        v_ref[...] = vval.astype(BF16)
        cc_ref[...] = cc.astype(BF16)
        y_ref[...] = yb
        h_ref[...] = xv + _dot(yb, wout[...])

    act = _sds((T, D), BF16)
    return _hosted_call(
        body, "mixa_fwd", T // tm,
        in_specs=[_row_spec(tm, D), _full_spec((1, D)), _full_spec((KA, D))],
        out_specs=[_row_spec(tm, D) for _ in range(6)],
        out_shape=(_sds((T, D), F32),) + (act,) * 5,
        scratch=[pltpu.VMEM((3 * D, D), BF16), pltpu.VMEM((D, D), BF16),
                 pltpu.VMEM((HALO_A + tm, D), F32), pltpu.SemaphoreType.DMA((2 * NDEV,))],
        args=(x, gam, cw), comm=comm)


def _ffn_fwd(h, gam, wg, layer, tm, comm, head=None):
    T = h.shape[0]
    kg, ku, kd = "g%d" % layer, "u%d" % layer, "d%d" % layer
    n_head = 0 if head is None else 2

    def body(*refs):
        h_ref, gam_ref = refs[:2]
        wg_ref = refs[2 + n_head]
        o_ref, n_ref, g_ref, u_ref, gu_ref = refs[3 + n_head:8 + n_head]
        wgt, wut, wd, sems = refs[-4:]

        first = pl.program_id(0) == 0
        load_w = _weight_loads(wg_ref, [(wgt, kg), (wut, ku), (wd, kd)], sems)
        pl.when(first)(load_w)

        hv = h_ref[...]
        nb = (hv * _rms_stat(hv) * gam_ref[...]).astype(BF16)
        n_ref[...] = nb
        out = hv
        for f in range(2):
            cols = pl.ds(f * FH, FH)
            g = _dot_nt(nb, wgt[cols, :])
            u = _dot_nt(nb, wut[cols, :])
            gu = (g * _sigmoid(g) * u).astype(BF16)
            g_ref[:, cols] = g.astype(BF16)
            u_ref[:, cols] = u.astype(BF16)
            gu_ref[:, cols] = gu
            out = out + _dot(gu, wd[cols, :])
        if head is None:
            o_ref[...] = out
        else:
            t_ref, fin_ref, st_ref = refs[2], refs[3], refs[8 + n_head]

            @pl.when(pl.program_id(0) == 0)
            def _():
                st_ref[...] = jnp.zeros((8, D), F32)

            gamma = fin_ref[...]
            r = _rms_stat(out)
            err = out * r * gamma - t_ref[...]
            dx, dgam = _rms_bwd(err * (1.0 / D), out, r, gamma)
            o_ref[...] = dx
            st_ref[pl.ds(0, 1), :] += dgam
            st_ref[pl.ds(1, 1), :] += (0.5 / D) * jnp.sum(err * err, axis=0, keepdims=True)

    pre = _sds((T, FF), BF16)
    in_specs = [_row_spec(tm, D), _full_spec((1, D))]
    args = (h, gam)
    out_specs = [_row_spec(tm, D), _row_spec(tm, D), _row_spec(tm, FF), _row_spec(tm, FF), _row_spec(tm, FF)]
    out_shape = (_sds((T, D), F32), _sds((T, D), BF16), pre, pre, pre)
    if head is not None:
        in_specs, args = in_specs + [_row_spec(tm, D), _full_spec((1, D))], args + tuple(head)
        out_specs, out_shape = out_specs + [_full_spec((8, D))], out_shape + (_sds((8, D), F32),)
    if comm is None:
        in_specs, args = in_specs + [_ANY], args + (wg,)
    return _hosted_call(
        body, "ffn%d_fwd" % layer, T // tm,
        in_specs=in_specs, out_specs=out_specs, out_shape=out_shape,
        scratch=[pltpu.VMEM((FF, D), BF16)] * 3 + [pltpu.SemaphoreType.DMA((3 * NDEV,))],
        args=args, comm=comm)


def _shifted_copies(buf, shf, tm):
    for r in range(1, 8):
        shf[r - 1] = buf[pl.ds(r, tm + HALO_B - 8), :]


def _broadcast_taps(vec_ref, wb):
    for k in range(KB):
        wb[k] = jnp.broadcast_to(vec_ref[pl.ds(SM_B_CONV + k, 1), :], (8, D))
    wb[KB] = jnp.broadcast_to(vec_ref[pl.ds(SM_B_BCONV, 1), :], (8, D))


def _taps_by_shift_residue(taps):
    groups = {}
    for k, shift in taps:
        q, r = divmod(shift, 8)
        groups.setdefault(r, []).append((k, q))
    return sorted(groups.items())


def _window(buf, shf, base, r, q0, n_groups, lanes):
    rows = pl.ds(base + 8 * q0, 8 * n_groups)
    v = buf[rows, lanes] if r == 0 else shf[r - 1, rows, lanes]
    return [v[8 * i:8 * i + 8] for i in range(n_groups)]


def _long_conv(buf, shf, wb, out_ref, tm, taps, bias_row):
    n_acc = CONV_ROWS // 8
    groups = _taps_by_shift_residue(taps)
    for col in range(D // 128):
        lanes = pl.ds(128 * col, 128)

        def rows(i, carry, lanes=lanes):
            base = i * CONV_ROWS
            init = jnp.zeros((8, 128), F32) if bias_row is None else wb[bias_row, :, lanes]
            accs = [init] * n_acc
            for r, lst in groups:
                q0, q1 = min(q for _, q in lst), max(q for _, q in lst)
                win = _window(buf, shf, base, r, q0, n_acc + q1 - q0, lanes)
                for k, q in lst:
                    wk = wb[k, :, lanes]
                    accs = [acc + wk * win[h + q - q0] for h, acc in enumerate(accs)]
            for h, acc in enumerate(accs):
                out_ref[pl.ds(base + 8 * h, 8), lanes] = acc
            return carry

        for i in range(tm // CONV_ROWS):
            rows(i, 0)


def _long_conv_grad_taps(buf, shf, x_ref, wacc, tm):
    n_acc = CONV_ROWS // 8
    groups = _taps_by_shift_residue([(KB - 1 - j, j) for j in range(KB)])
    for col in range(D // 128):
        lanes = pl.ds(128 * col, 128)

        def rows(i, carry, lanes=lanes):
            base = i * CONV_ROWS
            xv = x_ref[pl.ds(base, CONV_ROWS), lanes]
            xs = [xv[8 * h:8 * h + 8] for h in range(n_acc)]
            for r, lst in groups:
                q0, q1 = min(q for _, q in lst), max(q for _, q in lst)
                win = _window(buf, shf, base, r, q0, n_acc + q1 - q0, lanes)
                for k, q in lst:
                    prod = [x * win[h + q - q0] for h, x in enumerate(xs)]
                    wacc[k, :, lanes] += functools.reduce(lambda a, b: a + b, prod)
            return carry

        for i in range(tm // CONV_ROWS):
            rows(i, 0)


def _ln_stats(dc):
    mu = jnp.mean(dc, axis=-1, keepdims=True)
    xc = dc - mu
    rstd = lax.rsqrt(jnp.mean(xc * xc, axis=-1, keepdims=True) + LN_EPS)
    return xc * rstd, rstd


def _mixb_fwd(h, vecs, bias1, wg, tm, comm):
    T = h.shape[0]

    def body(h_ref, vec_ref, b1_ref, wg_ref, o_ref, a_ref, g_ref, dc_ref, s_ref, w1, w2, buf, shf, wb, sems):
        first = pl.program_id(0) == 0
        load_w = _weight_loads(wg_ref, [(w1, "b_pw1"), (w2, "b_pw2")], sems)

        @pl.when(first)
        def _():
            load_w()
            buf[pl.ds(0, HALO_B), :] = jnp.zeros((HALO_B, D), F32)
            _broadcast_taps(vec_ref, wb)

        hv = h_ref[...]
        nb = (hv * _rms_stat(hv) * vec_ref[pl.ds(SM_B_NORM, 1), :]).astype(BF16)
        a = _dot_nt(nb, w1[pl.ds(0, D), :]) + b1_ref[:, pl.ds(0, D)]
        g = _dot_nt(nb, w1[pl.ds(D, D), :]) + b1_ref[:, pl.ds(D, D)]
        a_ref[...] = a.astype(BF16)
        g_ref[...] = g.astype(BF16)
        buf[pl.ds(HALO_B, tm), :] = a * _sigmoid(g)
        _shifted_copies(buf, shf, tm)

        _long_conv(buf, shf, wb, dc_ref, tm, [(k, HALO_B - (KB - 1) + k) for k in range(KB)], KB)
        buf[pl.ds(0, HALO_B), :] = buf[pl.ds(tm, HALO_B), :]
        xhat, _ = _ln_stats(dc_ref[...])
        ln = xhat * vec_ref[pl.ds(SM_LN_G, 1), :] + vec_ref[pl.ds(SM_LN_B, 1), :]
        s = (ln * _sigmoid(ln)).astype(BF16)
        s_ref[...] = s
        o_ref[...] = hv + _dot(s, w2[...]) + vec_ref[pl.ds(SM_B_PW2, 1), :]

    act = _sds((T, D), BF16)
    return _hosted_call(
        body, "mixb_fwd", T // tm,
        in_specs=[_row_spec(tm, D), _full_spec((SM_F32_ROWS, D)), _full_spec((1, 2 * D))],
        out_specs=[_row_spec(tm, D) for _ in range(5)],
        out_shape=(_sds((T, D), F32), act, act, _sds((T, D), F32), act),
        scratch=[pltpu.VMEM((2 * D, D), BF16), pltpu.VMEM((D, D), BF16),
                 pltpu.VMEM((HALO_B + tm, D), F32), pltpu.VMEM((7, HALO_B + tm - 8, D), F32),
                 pltpu.VMEM((KB + 1, 8, D), F32), pltpu.SemaphoreType.DMA((2 * NDEV,))],
        args=(h, vecs, bias1), comm=comm)


def _ffn_bwd_dx(dh, h, g, u, gam, wg, layer, tm, comm):
    T = h.shape[0]
    kg, ku, kd = "g%d" % layer, "u%d" % layer, "d%d" % layer

    def body(dh_ref, h_ref, g_ref, u_ref, gam_ref, wg_ref, o_ref, dg_ref, du_ref, st_ref, wgt, wut, wd, sems):
        first = pl.program_id(0) == 0
        load_w = _weight_loads(wg_ref, [(wd, kd), (wgt, kg), (wut, ku)], sems)

        @pl.when(first)
        def _():
            load_w()
            st_ref[...] = jnp.zeros((8, D), F32)

        dhv = dh_ref[...]
        dhb = dhv.astype(BF16)
        dn = jnp.zeros_like(dhv)
        for f in range(2):
            cols = pl.ds(f * FH, FH)
            dgu = _dot_nt(dhb, wd[cols, :])
            gv = g_ref[:, cols].astype(F32)
            sg = _sigmoid(gv)
            du = (dgu * gv * sg).astype(BF16)
            dg = (dgu * u_ref[:, cols].astype(F32) * (sg * (1.0 + gv * (1.0 - sg)))).astype(BF16)
            dg_ref[:, cols] = dg
            du_ref[:, cols] = du
            dn = dn + _dot(dg, wgt[cols, :]) + _dot(du, wut[cols, :])
        hv = h_ref[...]
        dx, dgam = _rms_bwd(dn, hv, _rms_stat(hv), gam_ref[...])
        o_ref[...] = dhv + dx
        st_ref[pl.ds(0, 1), :] += dgam

    pre = _sds((T, FF), BF16)
    return _hosted_call(
        body, "ffn%d_bwd_dx" % layer, T // tm,
        in_specs=[_row_spec(tm, D), _row_spec(tm, D), _row_spec(tm, FF), _row_spec(tm, FF),
                  _full_spec((1, D)), _ANY],
        out_specs=[_row_spec(tm, D), _row_spec(tm, FF), _row_spec(tm, FF), _full_spec((8, D))],
        out_shape=(_sds((T, D), F32), pre, pre, _sds((8, D), F32)),
        scratch=[pltpu.VMEM((FF, D), BF16)] * 3 + [pltpu.SemaphoreType.DMA((3 * NDEV,))],
        args=(dh, h, g, u, gam, wg), comm=comm)


def _grad_w(lhs, rhs, mc, name, tm):
    T, M = lhs.shape
    nt = T // tm

    def body(l_ref, r_ref, o_ref, acc):
        i = pl.program_id(1)

        @pl.when(i == 0)
        def _():
            acc[...] = jnp.zeros((mc, D), F32)

        acc[...] += _dot_tn(l_ref[...], r_ref[...].astype(BF16))

        @pl.when(i == nt - 1)
        def _():
            o_ref[...] = acc[...].astype(BF16)

    return _pcall(
        body, name=name, grid=(M // mc, nt),
        out_shape=_sds((M, D), BF16),
        in_specs=[pl.BlockSpec((tm, mc), lambda j, i: (i, j)), pl.BlockSpec((tm, D), lambda j, i: (i, 0))],
        out_specs=pl.BlockSpec((mc, D), lambda j, i: (j, 0)),
        scratch_shapes=[pltpu.VMEM((mc, D), F32)],
        compiler_params=_cparams(("arbitrary", "arbitrary")),
    )(lhs, rhs)


def _mixb_bwd(dh, h, a, g, dc, s, vecs, wg, tm, comm):
    T = h.shape[0]
    nt = T // tm

    def body(dh_ref, h_ref, a_ref, g_ref, dc_ref, s_ref, vec_ref, wg_ref, o_ref, st_ref, sb_ref, g1_ref, g2_ref,
             w1, w2, buf, shf, glu_s, dglu_s, wacc, wb, acc1, acc2, stage, sems, out_sems):
        first = pl.program_id(0) == 0
        load_w = _weight_loads(wg_ref, [(w2, "b_pw2"), (w1, "b_pw1")], sems)

        @pl.when(first)
        def _():
            load_w()
            buf[pl.ds(tm, HALO_B), :] = jnp.zeros((HALO_B, D), F32)
            wacc[...] = jnp.zeros((KB, 8, D), F32)
            _broadcast_taps(vec_ref, wb)
            st_ref[...] = jnp.zeros((SM_F32_ROWS, D), F32)
            sb_ref[...] = jnp.zeros((8, 2 * D), F32)
            acc1[...] = jnp.zeros((2 * D, D), F32)
            acc2[...] = jnp.zeros((D, D), F32)

        def acc(row, val):
            st_ref[pl.ds(row, 1), :] += jnp.sum(val, axis=0, keepdims=True)

        dhv = dh_ref[...]
        dhb = dhv.astype(BF16)
        acc(SM_B_PW2, dhv)
        acc2[...] += _dot_tn(s_ref[...], dhb)
        ds = _dot_nt(dhb, w2[...])
        xhat, rstd = _ln_stats(dc_ref[...])
        ln_g = vec_ref[pl.ds(SM_LN_G, 1), :]
        ln = xhat * ln_g + vec_ref[pl.ds(SM_LN_B, 1), :]
        sl = _sigmoid(ln)
        dln = ds * (sl * (1.0 + ln * (1.0 - sl)))
        acc(SM_LN_G, dln * xhat)
        acc(SM_LN_B, dln)
        dxh = dln * ln_g
        ddc = rstd * (dxh - jnp.mean(dxh, axis=-1, keepdims=True)
                      - xhat * jnp.mean(dxh * xhat, axis=-1, keepdims=True))
        acc(SM_B_BCONV, ddc)
        buf[pl.ds(0, tm), :] = ddc
        _shifted_copies(buf, shf, tm)
        av = a_ref[...].astype(F32)
        sg = _sigmoid(g_ref[...].astype(F32))
        glu_s[...] = av * sg

        _long_conv(buf, shf, wb, dglu_s, tm, [(KB - 1 - j, j) for j in range(KB)], None)
        _long_conv_grad_taps(buf, shf, glu_s, wacc, tm)
        buf[pl.ds(tm, HALO_B), :] = buf[pl.ds(0, HALO_B), :]
        dglu = dglu_s[...]
        da = dglu * sg
        dg = dglu * av * sg * (1.0 - sg)
        sb_ref[pl.ds(0, 1), pl.ds(0, D)] += jnp.sum(da, axis=0, keepdims=True)
        sb_ref[pl.ds(0, 1), pl.ds(D, D)] += jnp.sum(dg, axis=0, keepdims=True)
        dab, dgb = da.astype(BF16), dg.astype(BF16)
        dn = _dot(dab, w1[pl.ds(0, D), :]) + _dot(dgb, w1[pl.ds(D, D), :])
        hv = h_ref[...]
        r = _rms_stat(hv)
        gamma = vec_ref[pl.ds(SM_B_NORM, 1), :]
        nb = (hv * r * gamma).astype(BF16)
        acc1[pl.ds(0, D), :] += _dot_tn(dab, nb)
        acc1[pl.ds(D, D), :] += _dot_tn(dgb, nb)
        dx, dgam = _rms_bwd(dn, hv, r, gamma)
        o_ref[...] = dhv + dx
        st_ref[pl.ds(SM_B_NORM, 1), :] += dgam

        @pl.when(pl.program_id(0) == nt - 1)
        def _():
            st_ref[pl.ds(SM_B_CONV, KB), :] = jnp.sum(wacc[...], axis=1)
            blocks = [(acc1, 0, g1_ref, 0), (acc1, D, g1_ref, D), (acc2, 0, g2_ref, 0)]
            for k, (src, a0, dst, d0) in enumerate(blocks):
                stage[...] = src[pl.ds(a0, D), :].astype(BF16)
                cp = pltpu.make_async_copy(stage, dst.at[pl.ds(d0, D), :], out_sems.at[k])
                cp.start()
                cp.wait()

    rs = functools.partial(_row_spec, rev_nt=nt)
    return _hosted_call(
        body, "mixb_bwd", nt,
        in_specs=[rs(tm, D) for _ in range(6)] + [_full_spec((SM_F32_ROWS, D)), _ANY],
        out_specs=[rs(tm, D), _full_spec((SM_F32_ROWS, D)), _full_spec((8, 2 * D)), _ANY, _ANY],
        out_shape=(_sds((T, D), F32), _sds((SM_F32_ROWS, D), F32), _sds((8, 2 * D), F32),
                   _sds((2 * D, D), BF16), _sds((D, D), BF16)),
        scratch=[pltpu.VMEM((2 * D, D), BF16), pltpu.VMEM((D, D), BF16),
                 pltpu.VMEM((tm + HALO_B, D), F32), pltpu.VMEM((7, tm + HALO_B - 8, D), F32),
                 pltpu.VMEM((tm, D), F32), pltpu.VMEM((tm, D), F32), pltpu.VMEM((KB, 8, D), F32),
                 pltpu.VMEM((KB + 1, 8, D), F32), pltpu.VMEM((2 * D, D), F32), pltpu.VMEM((D, D), F32),
                 pltpu.VMEM((D, D), BF16), pltpu.SemaphoreType.DMA((2 * NDEV,)), pltpu.SemaphoreType.DMA((3,))],
        args=(dh, h, a, g, dc, s, vecs, wg), comm=comm)


def _mixa_bwd(dh, x, b, c, v, cc, y, gam, cw, wg, tm, comm):
    T = x.shape[0]
    nt = T // tm

    def body(dh_ref, x_ref, b_ref, c_ref, v_ref, cc_ref, y_ref, gam_ref, cw_ref, wg_ref,
             o_ref, st_ref, gin_ref, gout_ref, win, wout, buf, acc_in, acc_out, stage, sems, out_sems):
        first = pl.program_id(0) == 0
        load_w = _weight_loads(wg_ref, [(wout, "a_out"), (win, "a_in")], sems)

        @pl.when(first)
        def _():
            load_w()
            buf[pl.ds(tm, HALO_A), :] = jnp.zeros((HALO_A, D), F32)
            st_ref[...] = jnp.zeros((8, D), F32)
            acc_in[...] = jnp.zeros((3 * D, D), F32)
            acc_out[...] = jnp.zeros((D, D), F32)

        dhv = dh_ref[...]
        dhb = dhv.astype(BF16)
        dy = _dot_nt(dhb, wout[...])
        cval = c_ref[...].astype(F32)
        vval = v_ref[...].astype(F32)
        d_b = (dy * cc_ref[...].astype(F32)).astype(BF16)
        buf[pl.ds(0, tm), :] = dy * b_ref[...].astype(F32)
        cv = cval * vval
        dcv = jnp.zeros((tm, D), F32)
        for j in range(KA):
            sh = buf[pl.ds(j, tm), :]
            k = KA - 1 - j
            dcv = dcv + cw_ref[pl.ds(k, 1), :] * sh
            st_ref[pl.ds(1 + k, 1), :] += jnp.sum(cv * sh, axis=0, keepdims=True)
        buf[pl.ds(tm, HALO_A), :] = buf[pl.ds(0, HALO_A), :]
        d_c = (dcv * vval).astype(BF16)
        d_v = (dcv * cval).astype(BF16)
        xv = x_ref[...]
        r = _rms_stat(xv)
        gamma = gam_ref[...]
        nb = (xv * r * gamma).astype(BF16)
        for j, piece in enumerate((d_b, d_c, d_v)):
            acc_in[pl.ds(j * D, D), :] += _dot_tn(piece, nb)
        acc_out[...] += _dot_tn(y_ref[...], dhb)
        dn = _dot(d_b, win[pl.ds(0, D), :]) + _dot(d_c, win[pl.ds(D, D), :]) + _dot(d_v, win[pl.ds(2 * D, D), :])
        dx, dgam = _rms_bwd(dn, xv, r, gamma)
        o_ref[...] = dhv + dx
        st_ref[pl.ds(0, 1), :] += dgam

        @pl.when(pl.program_id(0) == nt - 1)
        def _():
            blocks = [(acc_in, j * D, gin_ref, j * D) for j in range(3)] + [(acc_out, 0, gout_ref, 0)]
            for k, (acc, a0, dst, d0) in enumerate(blocks):
                stage[...] = acc[pl.ds(a0, D), :].astype(BF16)
                cp = pltpu.make_async_copy(stage, dst.at[pl.ds(d0, D), :], out_sems.at[k])
                cp.start()
                cp.wait()

    rs = functools.partial(_row_spec, rev_nt=nt)
    return _hosted_call(
        body, "mixa_bwd", nt,
        in_specs=[rs(tm, D) for _ in range(7)] + [_full_spec((1, D)), _full_spec((KA, D)), _ANY],
        out_specs=[rs(tm, D), _full_spec((8, D)), _ANY, _ANY],
        out_shape=(_sds((T, D), F32), _sds((8, D), F32), _sds((3 * D, D), BF16), _sds((D, D), BF16)),
        scratch=[pltpu.VMEM((3 * D, D), BF16), pltpu.VMEM((D, D), BF16), pltpu.VMEM((tm + HALO_A, D), F32),
                 pltpu.VMEM((3 * D, D), F32), pltpu.VMEM((D, D), F32), pltpu.VMEM((D, D), BF16),
                 pltpu.SemaphoreType.DMA((2 * NDEV,)), pltpu.SemaphoreType.DMA((4,))],
        args=(dh, x, b, c, v, cc, y, gam, cw, wg), comm=comm)


def _sum_small(sm, rp):
    def body(sm_ref, rp_ref, osm, orp, oloss):
        a = sm_ref[0]
        b = rp_ref[0]
        for s in range(1, NDEV):
            a = a + sm_ref[s]
            b = b + rp_ref[s]
        osm[...] = a
        orp[...] = b
        oloss[...] = jnp.zeros((8, 128), F32) + jnp.sum(b[4:5, :], axis=-1, keepdims=True)

    return _pcall(
        body, name="sum_small_grads",
        out_shape=(_sds((SM_F32_ROWS, 128), F32), _sds((8, D), F32), _sds((8, 128), F32)),
        in_specs=[pl.BlockSpec(memory_space=pltpu.VMEM)] * 2,
        out_specs=tuple(pl.BlockSpec(memory_space=pltpu.VMEM) for _ in range(3)),
        compiler_params=_cparams(),
    )(sm, rp)


def _adam_math(w, g, m, v):
    mn = B1 * m + (1.0 - B1) * g
    vn = B2 * v + (1.0 - B2) * (g * g)
    m_hat = mn / (1.0 - B1 ** STEP)
    v_hat = vn / (1.0 - B2 ** STEP)
    return -LR * (m_hat / (jnp.sqrt(v_hat) + ADAM_EPS) + WD * w), mn, vn


def _finish_weight(land, keys, transposed, w, m, v, name, comm=None):
    layers, rows, cols = w.shape
    n = N_ROWS[keys[0]]
    n_pad = -(-n // 128) * 128 if transposed else n

    def body(w_ref, m_ref, v_ref, land_ref, og, od, om, ov, buf, sem):
        off = LD_OFF[keys[0]]
        if layers == 2:
            off = jnp.where(pl.program_id(0) == 0, off, LD_OFF[keys[1]])
        cp = pltpu.make_async_copy(land_ref.at[:, pl.ds(pl.multiple_of(off, 32), n), :], buf, sem)
        cp.start()
        cp.wait()
        g = buf[0].astype(F32)
        for s in range(1, NCHIP):
            g = g + buf[s].astype(F32)
        if transposed:
            if n_pad != n:
                g = jnp.concatenate([g, jnp.zeros((n_pad - n, D), F32)], axis=0)
            g = g.T[:, :n]
        d, mn, vn = _adam_math(w_ref[...], g, m_ref[...], v_ref[...])
        og[...] = g
        od[...] = d
        om[...] = mn
        ov[...] = vn

    spec = pl.BlockSpec((None, rows, cols), lambda l: (l, 0, 0))
    shp = _sds(w.shape, F32)
    in_specs, args = [spec, spec, spec], (w, m, v)
    if comm is None:
        in_specs, args = in_specs + [_ANY], args + (land,)
    return _hosted_call(
        body, "finish_" + name, layers,
        in_specs=in_specs, out_specs=[spec] * 4, out_shape=(shp,) * 4,
        scratch=[pltpu.VMEM((NCHIP, n, D), BF16), pltpu.SemaphoreType.DMA],
        args=args, comm=comm)


SMALL_PARAMS = (("sm", SM_B_NORM, 1), ("sm", SM_B_PW1, 2), ("sm", SM_B_CONV, KB), ("sm", SM_B_BCONV, 1),
                ("sm", SM_LN_G, 1), ("sm", SM_LN_B, 1), ("sm", SM_B_PW2, 1), ("sm", SM_A_CONV, KA),
                ("rp", 0, 1), ("rp", 1, 2), ("rp", 3, 1))


def _adamw_small(g_small, g_repl, triples):
    n = len(SMALL_PARAMS)

    def body(*refs):
        gs_ref, gr_ref = refs[0], refs[1]
        ins = refs[2:2 + 3 * n]
        outs = refs[2 + 3 * n:]
        for i, (src, r0, nr) in enumerate(SMALL_PARAMS):
            w_ref, m_ref, v_ref = ins[3 * i:3 * i + 3]
            g = (gs_ref if src == "sm" else gr_ref)[pl.ds(r0, nr), :]
            if i == 1:
                g = jnp.concatenate([g[0:1], g[1:2]], axis=1)
            lead = (0,) if len(w_ref.shape) == 3 else ()
            idx = lead + (slice(None), slice(None))
            vals = (g,) + _adam_math(w_ref[idx], g, m_ref[idx], v_ref[idx])
            for o_ref, val in zip(outs[4 * i:4 * i + 4], vals):
                o_ref[idx] = val

    flat = [a for t in triples for a in t]
    out_shape = tuple(_sds(t[0].shape, F32) for t in triples for _ in range(4))
    vm = pl.BlockSpec(memory_space=pltpu.VMEM)
    res = _pcall(
        body, name="adamw_small", out_shape=out_shape,
        in_specs=[vm] * (2 + len(flat)), out_specs=tuple(vm for _ in out_shape),
        compiler_params=_cparams(),
    )(g_small, g_repl, *flat)
    return [tuple(res[4 * i:4 * i + 4]) for i in range(n)]


def kernel(x, a_norm, a_w_in, a_conv, a_w_out, b_norm, b_w_pw1, b_b_pw1, b_conv, b_b_conv, b_ln_g, b_ln_b, b_w_pw2, b_b_pw2, ffn_norm, ffn_w_gate, ffn_w_up, ffn_w_down, final_norm, loss_target, m_a_norm, m_a_w_in, m_a_conv, m_a_w_out, m_b_norm, m_b_w_pw1, m_b_b_pw1, m_b_conv, m_b_b_conv, m_b_ln_g, m_b_ln_b, m_b_w_pw2, m_b_b_pw2, m_ffn_norm, m_ffn_w_gate, m_ffn_w_up, m_ffn_w_down, m_final_norm, v_a_norm, v_a_w_in, v_a_conv, v_a_w_out, v_b_norm, v_b_w_pw1, v_b_b_pw1, v_b_conv, v_b_b_conv, v_b_ln_g, v_b_ln_b, v_b_w_pw2, v_b_b_pw2, v_ffn_norm, v_ffn_w_gate, v_ffn_w_up, v_ffn_w_down, v_final_norm):
    T = x.shape[1]
    tm = min(TM, T)
    tma = min(TM_A, T)
    tw = min(TM_DW, T)
    xs = x.reshape(T, D)
    tgt = loss_target.reshape(T, D)

    def rows_first(a):
        return jnp.swapaxes(a, 1, 2)

    shard, small = _pack_shard(
        (a_w_in, a_w_out, b_w_pw1, b_w_pw2, rows_first(ffn_w_gate), rows_first(ffn_w_up), ffn_w_down),
        (b_norm, b_b_pw1, b_conv, b_b_conv, b_ln_g, b_ln_b, b_b_pw2, a_conv))
    wg, sm_all = _all_gather_first(shard, small)
    vecs = sm_all.transpose(1, 0, 2).reshape(SM_F32_ROWS, D)
    bias1 = sm_all[:, SM_B_PW1:SM_B_PW1 + 2, :].reshape(1, 2 * D)
    cw_a = vecs[SM_A_CONV:SM_A_CONV + KA]
    fn0, fn1 = ffn_norm[0:1], ffn_norm[1:2]
    fin = final_norm.reshape(1, D)

    h1, bq, cq, vq, ccq, yq, wg = _mixa_fwd(xs, a_norm, cw_a, wg, tma, _ag_comm(shard, wg, ("g0", "u0", "d0")))
    h2, n1, g0, u0, gu0, wg = _ffn_fwd(h1, fn0, wg, 0, tma, _ag_comm(shard, wg, ("b_pw1", "b_pw2"), (T // tma) // 2))
    h3, aq, gq, dcq, sq, wg = _mixb_fwd(h2, vecs, bias1, wg, tm, _ag_comm(shard, wg, ("g1", "u1", "d1"), 1))
    dh4, n3, g1, u1, gu1, st_fin = _ffn_fwd(h3, fn1, wg, 1, tma, None, head=(tgt, fin))

    def by_dest(gw, key):
        return gw.reshape(NDEV, N_ROWS[key], D)

    gw_d1 = by_dest(_grad_w(gu1, dh4, FH, "grad_down1", tw), "d1")
    dh3, dg1, du1, st_f1 = _ffn_bwd_dx(dh4, h3, g1, u1, fn1, wg, 1, tma, None)
    gw_g1 = by_dest(_grad_w(dg1, n3, FH, "grad_gate1", tw), "g1")
    gw_u1 = by_dest(_grad_w(du1, n3, FH, "grad_up1", tw), "u1")
    keys = ("g1", "u1", "d1")
    dh2, st_b, st_b1, gw_pw1, gw_pw2, land = _mixb_bwd(
        dh3, h2, aq, gq, dcq, sq, vecs, wg, tm, _rs_comm(_pair_reduce([gw_g1, gw_u1, gw_d1], keys), keys, None))
    gw_pw1, gw_pw2 = by_dest(gw_pw1, "b_pw1"), by_dest(gw_pw2, "b_pw2")
    gw_d0 = by_dest(_grad_w(gu0, dh2, FH, "grad_down0", tw), "d0")
    keys = ("b_pw1", "b_pw2", "d0")
    dh1, dg0, du0, st_f0, land = _ffn_bwd_dx(
        dh2, h1, g0, u0, fn0, wg, 0, tma, _rs_comm(_pair_reduce([gw_pw1, gw_pw2, gw_d0], keys), keys, land))
    gw_g0 = by_dest(_grad_w(dg0, n1, FH, "grad_gate0", tw), "g0")
    gw_u0 = by_dest(_grad_w(du0, n1, FH, "grad_up0", tw), "u0")
    keys = ("g0", "u0")
    dx, st_a, gw_in, gw_out, land = _mixa_bwd(
        dh1, xs, bq, cq, vq, ccq, yq, a_norm, cw_a, wg, tm,
        _rs_comm(_pair_reduce([gw_g0, gw_u0], keys), keys, land))
    gw_in, gw_out = by_dest(gw_in, "a_in"), by_dest(gw_out, "a_out")

    st_small = st_b.at[SM_A_CONV:SM_A_CONV + KA].set(st_a[1:1 + KA])
    sm_dest = st_small.reshape(SM_F32_ROWS, NDEV, 128).transpose(1, 0, 2)
    sm_dest = sm_dest.at[:, SM_B_PW1:SM_B_PW1 + 2, :].set(st_b1[0].reshape(NDEV, 2, 128))
    repl = jnp.concatenate([st_a[0:1], st_f0[0:1], st_f1[0:1], st_fin[0:1], st_fin[1:2],
                            jnp.zeros((3, D), F32)], axis=0)[None]
    last = ("a_out", "a_in")
    land, sm_land, rp_land = _pair_reduce([gw_out, gw_in], last, side=(sm_dest, repl), land=land)

    big = [("ffn_w_gate", ("g0", "g1"), False,
            rows_first(ffn_w_gate), rows_first(m_ffn_w_gate), rows_first(v_ffn_w_gate)),
           ("ffn_w_up", ("u0", "u1"), False, rows_first(ffn_w_up), rows_first(m_ffn_w_up), rows_first(v_ffn_w_up)),
           ("ffn_w_down", ("d0", "d1"), False, ffn_w_down, m_ffn_w_down, v_ffn_w_down),
           ("a_w_out", ("a_out",), False, a_w_out, m_a_w_out, v_a_w_out),
           ("b_w_pw1", ("b_pw1",), True, b_w_pw1, m_b_w_pw1, v_b_w_pw1),
           ("b_w_pw2", ("b_pw2",), False, b_w_pw2, m_b_w_pw2, v_b_w_pw2),
           ("a_w_in", ("a_in",), True, a_w_in, m_a_w_in, v_a_w_in)]
    res = {name: _finish_weight(land, keys, transposed, w, m, v, name) for name, keys, transposed, w, m, v in big}
    for name in ("ffn_w_gate", "ffn_w_up"):
        res[name] = [rows_first(a) for a in res[name]]

    g_small, g_repl, loss8 = _sum_small(sm_land, rp_land)
    loss = loss8[0, 0]

    small_names = ["b_norm", "b_b_pw1", "b_conv", "b_b_conv", "b_ln_g", "b_ln_b", "b_b_pw2", "a_conv",
                   "a_norm", "ffn_norm", "final_norm"]
    triples = [(b_norm, m_b_norm, v_b_norm), (b_b_pw1, m_b_b_pw1, v_b_b_pw1), (b_conv, m_b_conv, v_b_conv),
               (b_b_conv, m_b_b_conv, v_b_b_conv), (b_ln_g, m_b_ln_g, v_b_ln_g), (b_ln_b, m_b_ln_b, v_b_ln_b),
               (b_b_pw2, m_b_b_pw2, v_b_b_pw2), (a_conv, m_a_conv, v_a_conv), (a_norm, m_a_norm, v_a_norm),
               (ffn_norm, m_ffn_norm, v_ffn_norm),
               (fin, m_final_norm.reshape(1, D), v_final_norm.reshape(1, D))]
    for name, quad in zip(small_names, _adamw_small(g_small, g_repl, triples)):
        res[name] = quad
    res["final_norm"] = tuple(a.reshape(D) for a in res["final_norm"])

    order = ["a_norm", "a_w_in", "a_conv", "a_w_out", "b_norm", "b_w_pw1", "b_b_pw1", "b_conv", "b_b_conv",
             "b_ln_g", "b_ln_b", "b_w_pw2", "b_b_pw2", "ffn_norm", "ffn_w_gate", "ffn_w_up", "ffn_w_down",
             "final_norm"]
    out = [loss, dx.reshape(1, T, D)]
    for j in range(4):
        out += [res[k][j] for k in order]
    return tuple(out)
```

```python
import functools

import jax
import jax.numpy as jnp
from jax import lax
from jax.experimental import pallas as pl
from jax.experimental.pallas import tpu as pltpu

F32 = jnp.float32
BF16 = jnp.bfloat16
MESH = pl.DeviceIdType.MESH

D = 1024
FF = 2816
FH = FF // 2
NDEV = 8
KA = 3
KB = 31
HALO_A = 8
HALO_B = 32
CONV_ROWS = 64
RMS_EPS = 1e-6
LN_EPS = 1e-5
LR, B1, B2, ADAM_EPS, WD, STEP = 0.001, 0.9, 0.999, 1e-08, 0.01, 10

TM = 256
TM_A = 512
TM_DW = 2048
VMEM_LIMIT = 62 * 1024 * 1024

N_ROWS = {"a_in": 384, "a_out": 128, "b_pw1": 256, "b_pw2": 128, "g0": 352, "u0": 352, "d0": 352,
          "g1": 352, "u1": 352, "d1": 352}
NCHIP = 4


def _offsets(order):
    off, o = {}, 0
    for k in order:
        off[k] = o
        o += N_ROWS[k]
    return off, o


PK_ORDER = ("a_in", "a_out", "g0", "u0", "d0", "b_pw1", "b_pw2", "g1", "u1", "d1")
PK_OFF, PK_ROWS = _offsets(PK_ORDER)
LD_ORDER = ("g1", "u1", "d1", "b_pw1", "b_pw2", "d0", "g0", "u0", "a_out", "a_in")
LD_OFF, N_WROWS = _offsets(LD_ORDER)


def _span(off, keys):
    return off[keys[0]], sum(N_ROWS[k] for k in keys)


SM_B_NORM, SM_B_PW1, SM_B_CONV, SM_B_BCONV, SM_LN_G, SM_LN_B, SM_B_PW2, SM_A_CONV = 0, 1, 3, 34, 35, 36, 37, 38
SM_F32_ROWS = 64


def _pcall(body, **kw):
    return pl.pallas_call(body, **kw)


def _cparams(sem=None):
    return pltpu.CompilerParams(dimension_semantics=sem, vmem_limit_bytes=VMEM_LIMIT)


def _dot(a, b):
    return jnp.dot(a, b, preferred_element_type=F32)


def _dot_nt(a, b):
    return lax.dot_general(a, b, (((1,), (1,)), ((), ())), preferred_element_type=F32)


def _dot_tn(a, b):
    return lax.dot_general(a, b, (((0,), (0,)), ((), ())), preferred_element_type=F32)


def _sigmoid(v):
    return 1.0 / (1.0 + jnp.exp(-v))


def _rms_stat(x):
    return lax.rsqrt(jnp.mean(x * x, axis=-1, keepdims=True) + RMS_EPS)


def _rms_bwd(dn, x, r, gamma):
    dng = dn * gamma
    dx = r * dng - x * (r * r * r) * jnp.mean(dng * x, axis=-1, keepdims=True)
    return dx, jnp.sum(dn * x * r, axis=0, keepdims=True)


def _weight_loads(wg_ref, plan, sems):
    copies = []
    for j, (dst, key) in enumerate(plan):
        off, n = PK_OFF[key], N_ROWS[key]
        copies += [pltpu.make_async_copy(
            wg_ref.at[d, pl.ds(off, n), :], dst.at[pl.ds(d * n, n), :], sems.at[j * NDEV + d])
            for d in range(NDEV)]

    def load():
        for cp in copies:
            cp.start()
        for cp in copies:
            cp.wait()

    return load


_ANY = pl.BlockSpec(memory_space=pl.ANY)


def _row_spec(tm, width, rev_nt=None):
    if rev_nt is None:
        return pl.BlockSpec((tm, width), lambda i: (i, 0))
    return pl.BlockSpec((tm, width), lambda i: (rev_nt - 1 - i, 0))


def _full_spec(shape):
    return pl.BlockSpec(shape, lambda *_: (0,) * len(shape))


def _sds(shape, dtype):
    return jax.ShapeDtypeStruct(shape, dtype)


def _mesh_pos():
    return lax.axis_index("x"), lax.axis_index("y"), lax.axis_index("c")


def _lin(p):
    return 4 * p[0] + 2 * p[1] + p[2]


def _ag_exchange(src, slot, send_sems, recv_sems, local_sem):
    x, y, c = _mesh_pos()
    me, sibling = (x, y, c), (x, y, 1 - c)
    chips = [(1 - x, y), (x, 1 - y), (1 - x, 1 - y)]

    def copy(k, block, to, own=False):
        return pltpu.make_async_remote_copy(
            src_ref=src if own else slot(block), dst_ref=slot(block),
            send_sem=send_sems.at[k], recv_sem=recv_sems.at[k], device_id=to, device_id_type=MESH)

    mine = pltpu.make_async_copy(src, slot(me), local_sem)
    first = [copy(0, me, sibling, own=True)]
    first += [copy(1 + j, me, (*chip, c), own=True) for j, chip in enumerate(chips)]
    passed = [copy(4 + j, (*chip, c), sibling) for j, chip in enumerate(chips)]

    def start():
        mine.start()
        for cp in first:
            cp.start()

    def forward():
        for j, chip in enumerate(chips):
            copy(1 + j, (*chip, c), me).wait_recv()
            passed[j].start()

    def finish():
        copy(0, sibling, me).wait_recv()
        for j, chip in enumerate(chips):
            copy(4 + j, (*chip, 1 - c), me).wait_recv()
        for cp in first + passed:
            cp.wait_send()
        mine.wait()

    return start, forward, finish


def _chip_exchange(p_ref, land_ref, send_sems, recv_sems, keys):
    x, y, c = _mesh_pos()
    l0, rows = _span(LD_OFF, keys)
    dst = land_ref.at[2 * x + y, pl.ds(l0, rows), :]
    peers = [((x + (j >> 1)) % 2, (y + (j & 1)) % 2) for j in range(NCHIP)]

    def copy(j):
        tx, ty = peers[j]
        src = p_ref.at[2 * tx + ty]
        if j == 0:
            return pltpu.make_async_copy(src, dst, send_sems.at[0])
        return pltpu.make_async_remote_copy(
            src_ref=src, dst_ref=dst, send_sem=send_sems.at[j], recv_sem=recv_sems.at[j],
            device_id=(tx, ty, c), device_id_type=MESH)

    def start():
        for j in range(NCHIP):
            copy(j).start()

    def finish():
        for j in range(NCHIP):
            copy(j).wait()

    return start, finish


def _all_to_all_f32(src_for, dst_ref, send_sems, recv_sems):
    x, y, c = _mesh_pos()
    dst = dst_ref.at[_lin((x, y, c))]
    peers = [((x + ((j >> 2) & 1)) % 2, (y + ((j >> 1) & 1)) % 2, (c + (j & 1)) % 2) for j in range(NDEV)]

    def copy(j):
        if j == 0:
            return pltpu.make_async_copy(src_for(_lin(peers[0])), dst, send_sems.at[0])
        return pltpu.make_async_remote_copy(
            src_ref=src_for(_lin(peers[j])), dst_ref=dst, send_sem=send_sems.at[j], recv_sem=recv_sems.at[j],
            device_id=peers[j], device_id_type=MESH)

    def start():
        for j in range(NDEV):
            copy(j).start()

    def finish():
        for j in range(NDEV):
            copy(j).wait()

    return start, finish


class _Comm:
    def __init__(self, ins, alias_in, out_shape, scratch, make, gives_wg, middle_step=0):
        self.ins, self.alias_in, self.out_shape = ins, alias_in, out_shape
        self.scratch, self.make, self.gives_wg, self.middle_step = scratch, make, gives_wg, middle_step


def _ag_comm(shard, wg, keys, forward_step=0):
    def make(c_ins, c_out, sc):
        return _weights_exchange(c_ins[0], c_out, sc[0], sc[1], sc[2], keys)

    return _Comm([shard, wg], 1, _sds(wg.shape, BF16),
                 [pltpu.SemaphoreType.DMA((7,)), pltpu.SemaphoreType.DMA((7,)), pltpu.SemaphoreType.DMA],
                 make, True, forward_step)


def _weights_exchange(shard_ref, wg_ref, send_sems, recv_sems, local_sem, keys):
    r0, nr = _span(PK_OFF, keys)
    return _ag_exchange(shard_ref.at[pl.ds(r0, nr), :], lambda p: wg_ref.at[_lin(p), pl.ds(r0, nr), :],
                        send_sems, recv_sems, local_sem)


def _rs_comm(psum, keys, land):
    def make(c_ins, c_out, sc):
        start, finish = _chip_exchange(c_ins[0], c_out, sc[0], sc[1], keys)
        return start, None, finish

    ins = [psum] + ([] if land is None else [land])
    return _Comm(ins, None if land is None else 1, _sds((NCHIP, N_WROWS, D), BF16),
                 [pltpu.SemaphoreType.DMA((NCHIP,)), pltpu.SemaphoreType.DMA((NCHIP,))], make, False)


def _hosted_call(body, name, nt, in_specs, out_specs, out_shape, scratch, args, comm):
    if comm is None:
        return _pcall(body, name=name, grid=(nt,), in_specs=in_specs, out_specs=tuple(out_specs),
                      out_shape=tuple(out_shape), scratch_shapes=scratch,
                      compiler_params=_cparams(("arbitrary",)))(*args)
    n_in, n_out, n_sc, n_cin = len(in_specs), len(out_specs), len(scratch), len(comm.ins)

    def wrapped(*refs):
        ins = refs[:n_in]
        c_ins = refs[n_in:n_in + n_cin]
        outs = refs[n_in + n_cin:n_in + n_cin + n_out]
        c_out = refs[n_in + n_cin + n_out]
        sc = refs[n_in + n_cin + n_out + 1:n_in + n_cin + n_out + 1 + n_sc]
        c_sc = refs[n_in + n_cin + n_out + 1 + n_sc:]
        start, middle, finish = comm.make(c_ins, c_out, c_sc)
        pl.when(pl.program_id(0) == 0)(start)
        if comm.gives_wg:
            body(*ins, c_out, *outs, *sc)
        else:
            body(*ins, *outs, *sc)
        if middle is not None:
            pl.when(pl.program_id(0) == max(nt - 1 - comm.middle_step, 0))(middle)
        pl.when(pl.program_id(0) == nt - 1)(finish)

    aliases = {} if comm.alias_in is None else {n_in + comm.alias_in: n_out}
    res = _pcall(wrapped, name=name, grid=(nt,),
                 in_specs=list(in_specs) + [_ANY] * n_cin, out_specs=tuple(out_specs) + (_ANY,),
                 out_shape=tuple(out_shape) + (comm.out_shape,),
                 scratch_shapes=list(scratch) + list(comm.scratch),
                 input_output_aliases=aliases,
                 compiler_params=_cparams(("arbitrary",)))(*args, *comm.ins)
    return res


def _pack_shard(weights, smalls):
    plan = (("a_in", 0, 0, True), ("a_out", 1, 0, False), ("b_pw1", 2, 0, True), ("b_pw2", 3, 0, False),
            ("g0", 4, 0, False), ("u0", 5, 0, False), ("d0", 6, 0, False),
            ("g1", 4, 1, False), ("u1", 5, 1, False), ("d1", 6, 1, False))
    n_pad = 384

    def body(*refs):
        out, sm, pad = refs[-3:]
        for key, idx, layer, transposed in plan:
            n = N_ROWS[key]
            val = refs[idx][layer]
            if transposed:
                if n % 128:
                    pad[:, pl.ds(0, n)] = val
                    pad[:, pl.ds(n, n_pad - n)] = jnp.zeros((D, n_pad - n), F32)
                    val = pad[...]
                val = val.T[:n]
            out[pl.ds(PK_OFF[key], n), :] = val.astype(BF16)
        b_norm, b_b_pw1, b_conv, b_b_conv, b_ln_g, b_ln_b, b_b_pw2, a_conv = refs[len(weights):-3]
        sm[...] = jnp.zeros((SM_F32_ROWS, 128), F32)
        sm[pl.ds(SM_B_PW1, 1), :] = b_b_pw1[:, pl.ds(0, 128)]
        sm[pl.ds(SM_B_PW1 + 1, 1), :] = b_b_pw1[:, pl.ds(128, 128)]
        sm[pl.ds(SM_B_CONV, KB), :] = b_conv[0]
        sm[pl.ds(SM_A_CONV, KA), :] = a_conv[0]
        for row, ref in ((SM_B_NORM, b_norm), (SM_B_BCONV, b_b_conv), (SM_LN_G, b_ln_g), (SM_LN_B, b_ln_b),
                         (SM_B_PW2, b_b_pw2)):
            sm[pl.ds(row, 1), :] = ref[...]

    vm = pl.BlockSpec(memory_space=pltpu.VMEM)
    return _pcall(
        body, name="pack_shard",
        out_shape=(_sds((PK_ROWS, D), BF16), _sds((SM_F32_ROWS, 128), F32)),
        in_specs=[vm] * (len(weights) + len(smalls)), out_specs=(vm, vm),
        scratch_shapes=[pltpu.VMEM((D, n_pad), F32)],
        compiler_params=_cparams(),
    )(*weights, *smalls)


def _all_gather_first(shard, small):
    def body(x_ref, s_ref, wg_ref, sg_ref, send_w, recv_w, local_w, send_s, recv_s, local_s):
        start_w, forward_w, finish_w = _weights_exchange(x_ref, wg_ref, send_w, recv_w, local_w, ("a_in", "a_out"))
        start_s, forward_s, finish_s = _ag_exchange(s_ref, lambda p: sg_ref.at[_lin(p)], send_s, recv_s, local_s)
        start_s()
        start_w()
        forward_s()
        forward_w()
        finish_s()
        finish_w()

    sems = [pltpu.SemaphoreType.DMA((7,)), pltpu.SemaphoreType.DMA((7,)), pltpu.SemaphoreType.DMA]
    return _pcall(
        body, name="all_gather_first",
        out_shape=(_sds((NDEV, PK_ROWS, D), BF16), _sds((NDEV, SM_F32_ROWS, 128), F32)),
        in_specs=[_ANY, _ANY], out_specs=(_ANY, _ANY), scratch_shapes=sems + sems,
    )(shard, small)


def _pair_reduce(grads, keys, side=None, land=None):
    l0, rows = _span(LD_OFF, keys)
    nk = len(keys)
    n_side = 0 if side is None else 2
    n_land = 0 if land is None else 1

    def body(c_ref, *refs):
        mine = refs[:nk]
        whole = refs[nk:2 * nk]
        n_in = 2 * nk + n_side + n_land
        o_ref = refs[n_in]
        got, send_sem, recv_sems = refs[n_in + 1 + n_side:n_in + 4 + n_side]
        t = pl.program_id(0)
        x, y, c = _mesh_pos()
        sibling = (x, y, 1 - c)
        if side is not None:
            sm_ref, rp_ref = refs[2 * nk:2 * nk + 2]
            sm_land, rp_land = refs[n_in + 1:n_in + 3]
            s2, r2, s3, r3 = refs[n_in + 6:n_in + 10]
            start_s, finish_s = _all_to_all_f32(lambda d: sm_ref.at[d], sm_land, s2, r2)
            start_r, finish_r = _all_to_all_f32(lambda d: rp_ref.at[0], rp_land, s3, r3)

        @pl.when(t == 0)
        def _():
            if side is not None:
                start_s()
                start_r()
            for ref, key in zip(whole, keys):
                for d in range(NCHIP):
                    pltpu.make_async_remote_copy(
                        src_ref=ref.at[2 * d + (1 - c)], dst_ref=got.at[d, pl.ds(LD_OFF[key] - l0, N_ROWS[key]), :],
                        send_sem=send_sem, recv_sem=recv_sems.at[d], device_id=sibling, device_id_type=MESH).start()

        pltpu.make_async_remote_copy(src_ref=got.at[t], dst_ref=got.at[t], send_sem=send_sem,
                                     recv_sem=recv_sems.at[t], device_id=sibling, device_id_type=MESH).wait_recv()
        sums = o_ref if land is None else refs[-3].at[t]
        for ref, key in zip(mine, keys):
            sl = pl.ds(LD_OFF[key] - l0, N_ROWS[key])
            sums[sl, :] = (ref[...].astype(F32) + got[t, sl, :].astype(F32)).astype(BF16)

        if land is not None:
            psum, chip_send, chip_recv = refs[-3:]
            my_chip = 2 * x + y
            dst = o_ref.at[my_chip, pl.ds(l0, rows), :]
            hop = jnp.bitwise_xor(t, my_chip)

            @pl.when(t == my_chip)
            def _():
                pltpu.make_async_copy(psum.at[t], dst, chip_send.at[0]).start()

            @pl.when(t != my_chip)
            def _():
                pltpu.make_async_remote_copy(
                    src_ref=psum.at[t], dst_ref=dst, send_sem=chip_send.at[hop], recv_sem=chip_recv.at[hop],
                    device_id=(t // 2, t % 2, c), device_id_type=MESH).start()

        @pl.when(t == NCHIP - 1)
        def _():
            pltpu.make_async_remote_copy(src_ref=got, dst_ref=got, send_sem=send_sem, recv_sem=recv_sems.at[0],
                                         device_id=sibling, device_id_type=MESH).wait_send()
            if land is not None:
                one = o_ref.at[0, pl.ds(l0, rows), :]
                pltpu.make_async_copy(one, one, chip_send.at[0]).wait()
                for j in range(1, NCHIP):
                    pltpu.make_async_remote_copy(src_ref=one, dst_ref=one, send_sem=chip_send.at[j],
                                                 recv_sem=chip_recv.at[j], device_id=sibling, device_id_type=MESH).wait()
            if side is not None:
                finish_s()
                finish_r()

    if land is None:
        out_specs = [pl.BlockSpec((None, rows, D), lambda t, c: (t, 0, 0))]
        out_shape = [_sds((NCHIP, rows, D), BF16)]
    else:
        out_specs, out_shape = [_ANY], [_sds((NCHIP, N_WROWS, D), BF16)]
    scratch = [pltpu.VMEM((NCHIP, rows, D), BF16), pltpu.SemaphoreType.DMA, pltpu.SemaphoreType.DMA((NCHIP,))]
    if side is not None:
        out_specs += [_ANY, _ANY]
        out_shape += [_sds((NDEV, SM_F32_ROWS, 128), F32), _sds((NDEV, 8, D), F32)]
        scratch += [pltpu.SemaphoreType.DMA((NDEV,))] * 4
    if land is not None:
        scratch += [pltpu.VMEM((NCHIP, rows, D), BF16), pltpu.SemaphoreType.DMA((NCHIP,)),
                    pltpu.SemaphoreType.DMA((NCHIP,))]
    grid_spec = pltpu.PrefetchScalarGridSpec(
        num_scalar_prefetch=1, grid=(NCHIP,),
        in_specs=[pl.BlockSpec((None, N_ROWS[k], D), lambda t, c: (2 * t + c[0], 0, 0)) for k in keys]
        + [_ANY] * (nk + n_side + n_land),
        out_specs=tuple(out_specs), scratch_shapes=scratch)
    core = lax.axis_index("c").astype(jnp.int32).reshape(1)
    aliases = {} if land is None else {1 + 2 * nk + n_side: 0}
    res = _pcall(
        body, name="pair_reduce_" + keys[0], grid_spec=grid_spec, out_shape=tuple(out_shape),
        input_output_aliases=aliases, compiler_params=_cparams(("arbitrary",)),
    )(core, *grads, *grads, *(side or ()), *(() if land is None else (land,)))
    return res[0] if len(res) == 1 else res


def _mixa_fwd(x, gam, cw, wg, tm, comm):
    T = x.shape[0]

    def body(x_ref, gam_ref, cw_ref, wg_ref, h_ref, b_ref, c_ref, v_ref, cc_ref, y_ref,
             win, wout, buf, sems):
        first = pl.program_id(0) == 0
        load_w = _weight_loads(wg_ref, [(win, "a_in"), (wout, "a_out")], sems)

        @pl.when(first)
        def _():
            load_w()
            buf[pl.ds(0, HALO_A), :] = jnp.zeros((HALO_A, D), F32)

        xv = x_ref[...]
        nb = (xv * _rms_stat(xv) * gam_ref[...]).astype(BF16)
        bv = _dot_nt(nb, win[pl.ds(0, D), :])
        cval = _dot_nt(nb, win[pl.ds(D, D), :])
        vval = _dot_nt(nb, win[pl.ds(2 * D, D), :])
        cv = cval * vval
        buf[pl.ds(HALO_A, tm), :] = cv
        cc = cw_ref[pl.ds(KA - 1, 1), :] * cv
        for k in range(KA - 1):
            cc = cc + cw_ref[pl.ds(k, 1), :] * buf[pl.ds(HALO_A - (KA - 1) + k, tm), :]
        buf[pl.ds(0, HALO_A), :] = buf[pl.ds(tm, HALO_A), :]
        yb = (bv * cc).astype(BF16)
        b_ref[...] = bv.astype(BF16)
        c_ref[...] = cval.astype(BF16)
        v_ref[...] = vval.astype(BF16)
        cc_ref[...] = cc.astype(BF16)
        y_ref[...] = yb
        h_ref[...] = xv + _dot(yb, wout[...])

    act = _sds((T, D), BF16)
    return _hosted_call(
        body, "mixa_fwd", T // tm,
        in_specs=[_row_spec(tm, D), _full_spec((1, D)), _full_spec((KA, D))],
        out_specs=[_row_spec(tm, D) for _ in range(6)],
        out_shape=(_sds((T, D), F32),) + (act,) * 5,
        scratch=[pltpu.VMEM((3 * D, D), BF16), pltpu.VMEM((D, D), BF16),
                 pltpu.VMEM((HALO_A + tm, D), F32), pltpu.SemaphoreType.DMA((2 * NDEV,))],
        args=(x, gam, cw), comm=comm)


def _ffn_fwd(h, gam, wg, layer, tm, comm, head=None):
    T = h.shape[0]
    kg, ku, kd = "g%d" % layer, "u%d" % layer, "d%d" % layer
    n_head = 0 if head is None else 2

    def body(*refs):
        h_ref, gam_ref = refs[:2]
        wg_ref = refs[2 + n_head]
        o_ref, n_ref, g_ref, u_ref, gu_ref = refs[3 + n_head:8 + n_head]
        wgt, wut, wd, sems = refs[-4:]

        first = pl.program_id(0) == 0
        load_w = _weight_loads(wg_ref, [(wgt, kg), (wut, ku), (wd, kd)], sems)
        pl.when(first)(load_w)

        hv = h_ref[...]
        nb = (hv * _rms_stat(hv) * gam_ref[...]).astype(BF16)
        n_ref[...] = nb
        out = hv
        for f in range(2):
            cols = pl.ds(f * FH, FH)
            g = _dot_nt(nb, wgt[cols, :])
            u = _dot_nt(nb, wut[cols, :])
            gu = (g * _sigmoid(g) * u).astype(BF16)
            g_ref[:, cols] = g.astype(BF16)
            u_ref[:, cols] = u.astype(BF16)
            gu_ref[:, cols] = gu
            out = out + _dot(gu, wd[cols, :])
        if head is None:
            o_ref[...] = out
        else:
            t_ref, fin_ref, st_ref = refs[2], refs[3], refs[8 + n_head]

            @pl.when(pl.program_id(0) == 0)
            def _():
                st_ref[...] = jnp.zeros((8, D), F32)

            gamma = fin_ref[...]
            r = _rms_stat(out)
            err = out * r * gamma - t_ref[...]
            dx, dgam = _rms_bwd(err * (1.0 / D), out, r, gamma)
            o_ref[...] = dx
            st_ref[pl.ds(0, 1), :] += dgam
            st_ref[pl.ds(1, 1), :] += (0.5 / D) * jnp.sum(err * err, axis=0, keepdims=True)

    pre = _sds((T, FF), BF16)
    in_specs = [_row_spec(tm, D), _full_spec((1, D))]
    args = (h, gam)
    out_specs = [_row_spec(tm, D), _row_spec(tm, D), _row_spec(tm, FF), _row_spec(tm, FF), _row_spec(tm, FF)]
    out_shape = (_sds((T, D), F32), _sds((T, D), BF16), pre, pre, pre)
    if head is not None:
        in_specs, args = in_specs + [_row_spec(tm, D), _full_spec((1, D))], args + tuple(head)
        out_specs, out_shape = out_specs + [_full_spec((8, D))], out_shape + (_sds((8, D), F32),)
    if comm is None:
        in_specs, args = in_specs + [_ANY], args + (wg,)
    return _hosted_call(
        body, "ffn%d_fwd" % layer, T // tm,
        in_specs=in_specs, out_specs=out_specs, out_shape=out_shape,
        scratch=[pltpu.VMEM((FF, D), BF16)] * 3 + [pltpu.SemaphoreType.DMA((3 * NDEV,))],
        args=args, comm=comm)


def _shifted_copies(buf, shf, tm):
    for r in range(1, 8):
        shf[r - 1] = buf[pl.ds(r, tm + HALO_B - 8), :]


def _broadcast_taps(vec_ref, wb):
    for k in range(KB):
        wb[k] = jnp.broadcast_to(vec_ref[pl.ds(SM_B_CONV + k, 1), :], (8, D))
    wb[KB] = jnp.broadcast_to(vec_ref[pl.ds(SM_B_BCONV, 1), :], (8, D))


def _taps_by_shift_residue(taps):
    groups = {}
    for k, shift in taps:
        q, r = divmod(shift, 8)
        groups.setdefault(r, []).append((k, q))
    return sorted(groups.items())


def _window(buf, shf, base, r, q0, n_groups, lanes):
    rows = pl.ds(base + 8 * q0, 8 * n_groups)
    v = buf[rows, lanes] if r == 0 else shf[r - 1, rows, lanes]
    return [v[8 * i:8 * i + 8] for i in range(n_groups)]


def _long_conv(buf, shf, wb, out_ref, tm, taps, bias_row):
    n_acc = CONV_ROWS // 8
    groups = _taps_by_shift_residue(taps)
    for col in range(D // 128):
        lanes = pl.ds(128 * col, 128)

        def rows(i, carry, lanes=lanes):
            base = i * CONV_ROWS
            init = jnp.zeros((8, 128), F32) if bias_row is None else wb[bias_row, :, lanes]
            accs = [init] * n_acc
            for r, lst in groups:
                q0, q1 = min(q for _, q in lst), max(q for _, q in lst)
                win = _window(buf, shf, base, r, q0, n_acc + q1 - q0, lanes)
                for k, q in lst:
                    wk = wb[k, :, lanes]
                    accs = [acc + wk * win[h + q - q0] for h, acc in enumerate(accs)]
            for h, acc in enumerate(accs):
                out_ref[pl.ds(base + 8 * h, 8), lanes] = acc
            return carry

        for i in range(tm // CONV_ROWS):
            rows(i, 0)


def _long_conv_grad_taps(buf, shf, x_ref, wacc, tm):
    n_acc = CONV_ROWS // 8
    groups = _taps_by_shift_residue([(KB - 1 - j, j) for j in range(KB)])
    for col in range(D // 128):
        lanes = pl.ds(128 * col, 128)

        def rows(i, carry, lanes=lanes):
            base = i * CONV_ROWS
            xv = x_ref[pl.ds(base, CONV_ROWS), lanes]
            xs = [xv[8 * h:8 * h + 8] for h in range(n_acc)]
            for r, lst in groups:
                q0, q1 = min(q for _, q in lst), max(q for _, q in lst)
                win = _window(buf, shf, base, r, q0, n_acc + q1 - q0, lanes)
                for k, q in lst:
                    prod = [x * win[h + q - q0] for h, x in enumerate(xs)]
                    wacc[k, :, lanes] += functools.reduce(lambda a, b: a + b, prod)
            return carry

        for i in range(tm // CONV_ROWS):
            rows(i, 0)


def _ln_stats(dc):
    mu = jnp.mean(dc, axis=-1, keepdims=True)
    xc = dc - mu
    rstd = lax.rsqrt(jnp.mean(xc * xc, axis=-1, keepdims=True) + LN_EPS)
    return xc * rstd, rstd


def _mixb_fwd(h, vecs, bias1, wg, tm, comm):
    T = h.shape[0]

    def body(h_ref, vec_ref, b1_ref, wg_ref, o_ref, a_ref, g_ref, dc_ref, s_ref, w1, w2, buf, shf, wb, sems):
        first = pl.program_id(0) == 0
        load_w = _weight_loads(wg_ref, [(w1, "b_pw1"), (w2, "b_pw2")], sems)

        @pl.when(first)
        def _():
            load_w()
            buf[pl.ds(0, HALO_B), :] = jnp.zeros((HALO_B, D), F32)
            _broadcast_taps(vec_ref, wb)

        hv = h_ref[...]
        nb = (hv * _rms_stat(hv) * vec_ref[pl.ds(SM_B_NORM, 1), :]).astype(BF16)
        a = _dot_nt(nb, w1[pl.ds(0, D), :]) + b1_ref[:, pl.ds(0, D)]
        g = _dot_nt(nb, w1[pl.ds(D, D), :]) + b1_ref[:, pl.ds(D, D)]
        a_ref[...] = a.astype(BF16)
        g_ref[...] = g.astype(BF16)
        buf[pl.ds(HALO_B, tm), :] = a * _sigmoid(g)
        _shifted_copies(buf, shf, tm)

        _long_conv(buf, shf, wb, dc_ref, tm, [(k, HALO_B - (KB - 1) + k) for k in range(KB)], KB)
        buf[pl.ds(0, HALO_B), :] = buf[pl.ds(tm, HALO_B), :]
        xhat, _ = _ln_stats(dc_ref[...])
        ln = xhat * vec_ref[pl.ds(SM_LN_G, 1), :] + vec_ref[pl.ds(SM_LN_B, 1), :]
        s = (ln * _sigmoid(ln)).astype(BF16)
        s_ref[...] = s
        o_ref[...] = hv + _dot(s, w2[...]) + vec_ref[pl.ds(SM_B_PW2, 1), :]

    act = _sds((T, D), BF16)
    return _hosted_call(
        body, "mixb_fwd", T // tm,
        in_specs=[_row_spec(tm, D), _full_spec((SM_F32_ROWS, D)), _full_spec((1, 2 * D))],
        out_specs=[_row_spec(tm, D) for _ in range(5)],
        out_shape=(_sds((T, D), F32), act, act, _sds((T, D), F32), act),
        scratch=[pltpu.VMEM((2 * D, D), BF16), pltpu.VMEM((D, D), BF16),
                 pltpu.VMEM((HALO_B + tm, D), F32), pltpu.VMEM((7, HALO_B + tm - 8, D), F32),
                 pltpu.VMEM((KB + 1, 8, D), F32), pltpu.SemaphoreType.DMA((2 * NDEV,))],
        args=(h, vecs, bias1), comm=comm)


def _ffn_bwd_dx(dh, h, g, u, gam, wg, layer, tm, comm):
    T = h.shape[0]
    kg, ku, kd = "g%d" % layer, "u%d" % layer, "d%d" % layer

    def body(dh_ref, h_ref, g_ref, u_ref, gam_ref, wg_ref, o_ref, dg_ref, du_ref, st_ref, wgt, wut, wd, sems):
        first = pl.program_id(0) == 0
        load_w = _weight_loads(wg_ref, [(wd, kd), (wgt, kg), (wut, ku)], sems)

        @pl.when(first)
        def _():
            load_w()
            st_ref[...] = jnp.zeros((8, D), F32)

        dhv = dh_ref[...]
        dhb = dhv.astype(BF16)
        dn = jnp.zeros_like(dhv)
        for f in range(2):
            cols = pl.ds(f * FH, FH)
            dgu = _dot_nt(dhb, wd[cols, :])
            gv = g_ref[:, cols].astype(F32)
            sg = _sigmoid(gv)
            du = (dgu * gv * sg).astype(BF16)
            dg = (dgu * u_ref[:, cols].astype(F32) * (sg * (1.0 + gv * (1.0 - sg)))).astype(BF16)
            dg_ref[:, cols] = dg
            du_ref[:, cols] = du
            dn = dn + _dot(dg, wgt[cols, :]) + _dot(du, wut[cols, :])
        hv = h_ref[...]
        dx, dgam = _rms_bwd(dn, hv, _rms_stat(hv), gam_ref[...])
        o_ref[...] = dhv + dx
        st_ref[pl.ds(0, 1), :] += dgam

    pre = _sds((T, FF), BF16)
    return _hosted_call(
        body, "ffn%d_bwd_dx" % layer, T // tm,
        in_specs=[_row_spec(tm, D), _row_spec(tm, D), _row_spec(tm, FF), _row_spec(tm, FF),
                  _full_spec((1, D)), _ANY],
        out_specs=[_row_spec(tm, D), _row_spec(tm, FF), _row_spec(tm, FF), _full_spec((8, D))],
        out_shape=(_sds((T, D), F32), pre, pre, _sds((8, D), F32)),
        scratch=[pltpu.VMEM((FF, D), BF16)] * 3 + [pltpu.SemaphoreType.DMA((3 * NDEV,))],
        args=(dh, h, g, u, gam, wg), comm=comm)


def _grad_w(lhs, rhs, mc, name, tm):
    T, M = lhs.shape
    nt = T // tm

    def body(l_ref, r_ref, o_ref, acc):
        i = pl.program_id(1)

        @pl.when(i == 0)
        def _():
            acc[...] = jnp.zeros((mc, D), F32)

        acc[...] += _dot_tn(l_ref[...], r_ref[...].astype(BF16))

        @pl.when(i == nt - 1)
        def _():
            o_ref[...] = acc[...].astype(BF16)

    return _pcall(
        body, name=name, grid=(M // mc, nt),
        out_shape=_sds((M, D), BF16),
        in_specs=[pl.BlockSpec((tm, mc), lambda j, i: (i, j)), pl.BlockSpec((tm, D), lambda j, i: (i, 0))],
        out_specs=pl.BlockSpec((mc, D), lambda j, i: (j, 0)),
        scratch_shapes=[pltpu.VMEM((mc, D), F32)],
        compiler_params=_cparams(("arbitrary", "arbitrary")),
    )(lhs, rhs)


def _mixb_bwd(dh, h, a, g, dc, s, vecs, wg, tm, comm):
    T = h.shape[0]
    nt = T // tm

    def body(dh_ref, h_ref, a_ref, g_ref, dc_ref, s_ref, vec_ref, wg_ref, o_ref, st_ref, sb_ref, g1_ref, g2_ref,
             w1, w2, buf, shf, glu_s, dglu_s, wacc, wb, acc1, acc2, stage, sems, out_sems):
        first = pl.program_id(0) == 0
        load_w = _weight_loads(wg_ref, [(w2, "b_pw2"), (w1, "b_pw1")], sems)

        @pl.when(first)
        def _():
            load_w()
            buf[pl.ds(tm, HALO_B), :] = jnp.zeros((HALO_B, D), F32)
            wacc[...] = jnp.zeros((KB, 8, D), F32)
            _broadcast_taps(vec_ref, wb)
            st_ref[...] = jnp.zeros((SM_F32_ROWS, D), F32)
            sb_ref[...] = jnp.zeros((8, 2 * D), F32)
            acc1[...] = jnp.zeros((2 * D, D), F32)
            acc2[...] = jnp.zeros((D, D), F32)

        def acc(row, val):
            st_ref[pl.ds(row, 1), :] += jnp.sum(val, axis=0, keepdims=True)

        dhv = dh_ref[...]
        dhb = dhv.astype(BF16)
        acc(SM_B_PW2, dhv)
        acc2[...] += _dot_tn(s_ref[...], dhb)
        ds = _dot_nt(dhb, w2[...])
        xhat, rstd = _ln_stats(dc_ref[...])
        ln_g = vec_ref[pl.ds(SM_LN_G, 1), :]
        ln = xhat * ln_g + vec_ref[pl.ds(SM_LN_B, 1), :]
        sl = _sigmoid(ln)
        dln = ds * (sl * (1.0 + ln * (1.0 - sl)))
        acc(SM_LN_G, dln * xhat)
        acc(SM_LN_B, dln)
        dxh = dln * ln_g
        ddc = rstd * (dxh - jnp.mean(dxh, axis=-1, keepdims=True)
                      - xhat * jnp.mean(dxh * xhat, axis=-1, keepdims=True))
        acc(SM_B_BCONV, ddc)
        buf[pl.ds(0, tm), :] = ddc
        _shifted_copies(buf, shf, tm)
        av = a_ref[...].astype(F32)
        sg = _sigmoid(g_ref[...].astype(F32))
        glu_s[...] = av * sg

        _long_conv(buf, shf, wb, dglu_s, tm, [(KB - 1 - j, j) for j in range(KB)], None)
        _long_conv_grad_taps(buf, shf, glu_s, wacc, tm)
        buf[pl.ds(tm, HALO_B), :] = buf[pl.ds(0, HALO_B), :]
        dglu = dglu_s[...]
        da = dglu * sg
        dg = dglu * av * sg * (1.0 - sg)
        sb_ref[pl.ds(0, 1), pl.ds(0, D)] += jnp.sum(da, axis=0, keepdims=True)
        sb_ref[pl.ds(0, 1), pl.ds(D, D)] += jnp.sum(dg, axis=0, keepdims=True)
        dab, dgb = da.astype(BF16), dg.astype(BF16)
        dn = _dot(dab, w1[pl.ds(0, D), :]) + _dot(dgb, w1[pl.ds(D, D), :])
        hv = h_ref[...]
        r = _rms_stat(hv)
        gamma = vec_ref[pl.ds(SM_B_NORM, 1), :]
        nb = (hv * r * gamma).astype(BF16)
        acc1[pl.ds(0, D), :] += _dot_tn(dab, nb)
        acc1[pl.ds(D, D), :] += _dot_tn(dgb, nb)
        dx, dgam = _rms_bwd(dn, hv, r, gamma)
        o_ref[...] = dhv + dx
        st_ref[pl.ds(SM_B_NORM, 1), :] += dgam

        @pl.when(pl.program_id(0) == nt - 1)
        def _():
            st_ref[pl.ds(SM_B_CONV, KB), :] = jnp.sum(wacc[...], axis=1)
            blocks = [(acc1, 0, g1_ref, 0), (acc1, D, g1_ref, D), (acc2, 0, g2_ref, 0)]
            for k, (src, a0, dst, d0) in enumerate(blocks):
                stage[...] = src[pl.ds(a0, D), :].astype(BF16)
                cp = pltpu.make_async_copy(stage, dst.at[pl.ds(d0, D), :], out_sems.at[k])
                cp.start()
                cp.wait()

    rs = functools.partial(_row_spec, rev_nt=nt)
    return _hosted_call(
        body, "mixb_bwd", nt,
        in_specs=[rs(tm, D) for _ in range(6)] + [_full_spec((SM_F32_ROWS, D)), _ANY],
        out_specs=[rs(tm, D), _full_spec((SM_F32_ROWS, D)), _full_spec((8, 2 * D)), _ANY, _ANY],
        out_shape=(_sds((T, D), F32), _sds((SM_F32_ROWS, D), F32), _sds((8, 2 * D), F32),
                   _sds((2 * D, D), BF16), _sds((D, D), BF16)),
        scratch=[pltpu.VMEM((2 * D, D), BF16), pltpu.VMEM((D, D), BF16),
                 pltpu.VMEM((tm + HALO_B, D), F32), pltpu.VMEM((7, tm + HALO_B - 8, D), F32),
                 pltpu.VMEM((tm, D), F32), pltpu.VMEM((tm, D), F32), pltpu.VMEM((KB, 8, D), F32),
                 pltpu.VMEM((KB + 1, 8, D), F32), pltpu.VMEM((2 * D, D), F32), pltpu.VMEM((D, D), F32),
                 pltpu.VMEM((D, D), BF16), pltpu.SemaphoreType.DMA((2 * NDEV,)), pltpu.SemaphoreType.DMA((3,))],
        args=(dh, h, a, g, dc, s, vecs, wg), comm=comm)


def _mixa_bwd(dh, x, b, c, v, cc, y, gam, cw, wg, tm, comm):
    T = x.shape[0]
    nt = T // tm

    def body(dh_ref, x_ref, b_ref, c_ref, v_ref, cc_ref, y_ref, gam_ref, cw_ref, wg_ref,
             o_ref, st_ref, gin_ref, gout_ref, win, wout, buf, acc_in, acc_out, stage, sems, out_sems):
        first = pl.program_id(0) == 0
        load_w = _weight_loads(wg_ref, [(wout, "a_out"), (win, "a_in")], sems)

        @pl.when(first)
        def _():
            load_w()
            buf[pl.ds(tm, HALO_A), :] = jnp.zeros((HALO_A, D), F32)
            st_ref[...] = jnp.zeros((8, D), F32)
            acc_in[...] = jnp.zeros((3 * D, D), F32)
            acc_out[...] = jnp.zeros((D, D), F32)

        dhv = dh_ref[...]
        dhb = dhv.astype(BF16)
        dy = _dot_nt(dhb, wout[...])
        cval = c_ref[...].astype(F32)
        vval = v_ref[...].astype(F32)
        d_b = (dy * cc_ref[...].astype(F32)).astype(BF16)
        buf[pl.ds(0, tm), :] = dy * b_ref[...].astype(F32)
        cv = cval * vval
        dcv = jnp.zeros((tm, D), F32)
        for j in range(KA):
            sh = buf[pl.ds(j, tm), :]
            k = KA - 1 - j
            dcv = dcv + cw_ref[pl.ds(k, 1), :] * sh
            st_ref[pl.ds(1 + k, 1), :] += jnp.sum(cv * sh, axis=0, keepdims=True)
        buf[pl.ds(tm, HALO_A), :] = buf[pl.ds(0, HALO_A), :]
        d_c = (dcv * vval).astype(BF16)
        d_v = (dcv * cval).astype(BF16)
        xv = x_ref[...]
        r = _rms_stat(xv)
        gamma = gam_ref[...]
        nb = (xv * r * gamma).astype(BF16)
        for j, piece in enumerate((d_b, d_c, d_v)):
            acc_in[pl.ds(j * D, D), :] += _dot_tn(piece, nb)
        acc_out[...] += _dot_tn(y_ref[...], dhb)
        dn = _dot(d_b, win[pl.ds(0, D), :]) + _dot(d_c, win[pl.ds(D, D), :]) + _dot(d_v, win[pl.ds(2 * D, D), :])
        dx, dgam = _rms_bwd(dn, xv, r, gamma)
        o_ref[...] = dhv + dx
        st_ref[pl.ds(0, 1), :] += dgam

        @pl.when(pl.program_id(0) == nt - 1)
        def _():
            blocks = [(acc_in, j * D, gin_ref, j * D) for j in range(3)] + [(acc_out, 0, gout_ref, 0)]
            for k, (acc, a0, dst, d0) in enumerate(blocks):
                stage[...] = acc[pl.ds(a0, D), :].astype(BF16)
                cp = pltpu.make_async_copy(stage, dst.at[pl.ds(d0, D), :], out_sems.at[k])
                cp.start()
                cp.wait()

    rs = functools.partial(_row_spec, rev_nt=nt)
    return _hosted_call(
        body, "mixa_bwd", nt,
        in_specs=[rs(tm, D) for _ in range(7)] + [_full_spec((1, D)), _full_spec((KA, D)), _ANY],
        out_specs=[rs(tm, D), _full_spec((8, D)), _ANY, _ANY],
        out_shape=(_sds((T, D), F32), _sds((8, D), F32), _sds((3 * D, D), BF16), _sds((D, D), BF16)),
        scratch=[pltpu.VMEM((3 * D, D), BF16), pltpu.VMEM((D, D), BF16), pltpu.VMEM((tm + HALO_A, D), F32),
                 pltpu.VMEM((3 * D, D), F32), pltpu.VMEM((D, D), F32), pltpu.VMEM((D, D), BF16),
                 pltpu.SemaphoreType.DMA((2 * NDEV,)), pltpu.SemaphoreType.DMA((4,))],
        args=(dh, x, b, c, v, cc, y, gam, cw, wg), comm=comm)


def _sum_small(sm, rp):
    def body(sm_ref, rp_ref, osm, orp, oloss):
        a = sm_ref[0]
        b = rp_ref[0]
        for s in range(1, NDEV):
            a = a + sm_ref[s]
            b = b + rp_ref[s]
        osm[...] = a
        orp[...] = b
        oloss[...] = jnp.zeros((8, 128), F32) + jnp.sum(b[4:5, :], axis=-1, keepdims=True)

    return _pcall(
        body, name="sum_small_grads",
        out_shape=(_sds((SM_F32_ROWS, 128), F32), _sds((8, D), F32), _sds((8, 128), F32)),
        in_specs=[pl.BlockSpec(memory_space=pltpu.VMEM)] * 2,
        out_specs=tuple(pl.BlockSpec(memory_space=pltpu.VMEM) for _ in range(3)),
        compiler_params=_cparams(),
    )(sm, rp)


def _adam_math(w, g, m, v):
    mn = B1 * m + (1.0 - B1) * g
    vn = B2 * v + (1.0 - B2) * (g * g)
    m_hat = mn / (1.0 - B1 ** STEP)
    v_hat = vn / (1.0 - B2 ** STEP)
    return -LR * (m_hat / (jnp.sqrt(v_hat) + ADAM_EPS) + WD * w), mn, vn


def _finish_weight(land, keys, transposed, w, m, v, name):
    layers, rows, cols = w.shape
    n = N_ROWS[keys[0]]
    n_pad = -(-n // 128) * 128 if transposed else n

    def body(w_ref, m_ref, v_ref, land_ref, og, od, om, ov, buf, sem):
        off = LD_OFF[keys[0]]
        if layers == 2:
            off = jnp.where(pl.program_id(0) == 0, off, LD_OFF[keys[1]])
        cp = pltpu.make_async_copy(land_ref.at[:, pl.ds(pl.multiple_of(off, 32), n), :], buf, sem)
        cp.start()
        cp.wait()
        g = buf[0].astype(F32)
        for s in range(1, NCHIP):
            g = g + buf[s].astype(F32)
        if transposed:
            if n_pad != n:
                g = jnp.concatenate([g, jnp.zeros((n_pad - n, D), F32)], axis=0)
            g = g.T[:, :n]
        d, mn, vn = _adam_math(w_ref[...], g, m_ref[...], v_ref[...])
        og[...] = g
        od[...] = d
        om[...] = mn
        ov[...] = vn

    spec = pl.BlockSpec((None, rows, cols), lambda l: (l, 0, 0))
    shp = _sds(w.shape, F32)
    return _hosted_call(
        body, "finish_" + name, layers,
        in_specs=[spec, spec, spec, _ANY], out_specs=[spec] * 4, out_shape=(shp,) * 4,
        scratch=[pltpu.VMEM((NCHIP, n, D), BF16), pltpu.SemaphoreType.DMA],
        args=(w, m, v, land), comm=None)


SMALL_PARAMS = (("sm", SM_B_NORM, 1), ("sm", SM_B_PW1, 2), ("sm", SM_B_CONV, KB), ("sm", SM_B_BCONV, 1),
                ("sm", SM_LN_G, 1), ("sm", SM_LN_B, 1), ("sm", SM_B_PW2, 1), ("sm", SM_A_CONV, KA),
                ("rp", 0, 1), ("rp", 1, 2), ("rp", 3, 1))


def _adamw_small(g_small, g_repl, triples):
    n = len(SMALL_PARAMS)

    def body(*refs):
        gs_ref, gr_ref = refs[0], refs[1]
        ins = refs[2:2 + 3 * n]
        outs = refs[2 + 3 * n:]
        for i, (src, r0, nr) in enumerate(SMALL_PARAMS):
            w_ref, m_ref, v_ref = ins[3 * i:3 * i + 3]
            g = (gs_ref if src == "sm" else gr_ref)[pl.ds(r0, nr), :]
            if i == 1:
                g = jnp.concatenate([g[0:1], g[1:2]], axis=1)
            lead = (0,) if len(w_ref.shape) == 3 else ()
            idx = lead + (slice(None), slice(None))
            vals = (g,) + _adam_math(w_ref[idx], g, m_ref[idx], v_ref[idx])
            for o_ref, val in zip(outs[4 * i:4 * i + 4], vals):
                o_ref[idx] = val

    flat = [a for t in triples for a in t]
    out_shape = tuple(_sds(t[0].shape, F32) for t in triples for _ in range(4))
    vm = pl.BlockSpec(memory_space=pltpu.VMEM)
    res = _pcall(
        body, name="adamw_small", out_shape=out_shape,
        in_specs=[vm] * (2 + len(flat)), out_specs=tuple(vm for _ in out_shape),
        compiler_params=_cparams(),
    )(g_small, g_repl, *flat)
    return [tuple(res[4 * i:4 * i + 4]) for i in range(n)]


def kernel(x, a_norm, a_w_in, a_conv, a_w_out, b_norm, b_w_pw1, b_b_pw1, b_conv, b_b_conv, b_ln_g, b_ln_b, b_w_pw2, b_b_pw2, ffn_norm, ffn_w_gate, ffn_w_up, ffn_w_down, final_norm, loss_target, m_a_norm, m_a_w_in, m_a_conv, m_a_w_out, m_b_norm, m_b_w_pw1, m_b_b_pw1, m_b_conv, m_b_b_conv, m_b_ln_g, m_b_ln_b, m_b_w_pw2, m_b_b_pw2, m_ffn_norm, m_ffn_w_gate, m_ffn_w_up, m_ffn_w_down, m_final_norm, v_a_norm, v_a_w_in, v_a_conv, v_a_w_out, v_b_norm, v_b_w_pw1, v_b_b_pw1, v_b_conv, v_b_b_conv, v_b_ln_g, v_b_ln_b, v_b_w_pw2, v_b_b_pw2, v_ffn_norm, v_ffn_w_gate, v_ffn_w_up, v_ffn_w_down, v_final_norm):
    T = x.shape[1]
    tm = min(TM, T)
    tma = min(TM_A, T)
    tw = min(TM_DW, T)
    xs = x.reshape(T, D)
    tgt = loss_target.reshape(T, D)

    def rows_first(a):
        return jnp.swapaxes(a, 1, 2)

    shard, small = _pack_shard(
        (a_w_in, a_w_out, b_w_pw1, b_w_pw2, rows_first(ffn_w_gate), rows_first(ffn_w_up), ffn_w_down),
        (b_norm, b_b_pw1, b_conv, b_b_conv, b_ln_g, b_ln_b, b_b_pw2, a_conv))
    wg, sm_all = _all_gather_first(shard, small)
    vecs = sm_all.transpose(1, 0, 2).reshape(SM_F32_ROWS, D)
    bias1 = sm_all[:, SM_B_PW1:SM_B_PW1 + 2, :].reshape(1, 2 * D)
    cw_a = vecs[SM_A_CONV:SM_A_CONV + KA]
    fn0, fn1 = ffn_norm[0:1], ffn_norm[1:2]
    fin = final_norm.reshape(1, D)

    h1, bq, cq, vq, ccq, yq, wg = _mixa_fwd(xs, a_norm, cw_a, wg, tma, _ag_comm(shard, wg, ("g0", "u0", "d0")))
    h2, n1, g0, u0, gu0, wg = _ffn_fwd(h1, fn0, wg, 0, tma, _ag_comm(shard, wg, ("b_pw1", "b_pw2"), (T // tma) // 2))
    h3, aq, gq, dcq, sq, wg = _mixb_fwd(h2, vecs, bias1, wg, tm, _ag_comm(shard, wg, ("g1", "u1", "d1"), 1))
    dh4, n3, g1, u1, gu1, st_fin = _ffn_fwd(h3, fn1, wg, 1, tma, None, head=(tgt, fin))

    def by_dest(gw, key):
        return gw.reshape(NDEV, N_ROWS[key], D)

    gw_d1 = by_dest(_grad_w(gu1, dh4, FH, "grad_down1", tw), "d1")
    dh3, dg1, du1, st_f1 = _ffn_bwd_dx(dh4, h3, g1, u1, fn1, wg, 1, tma, None)
    gw_g1 = by_dest(_grad_w(dg1, n3, FH, "grad_gate1", tw), "g1")
    gw_u1 = by_dest(_grad_w(du1, n3, FH, "grad_up1", tw), "u1")
    keys = ("g1", "u1", "d1")
    dh2, st_b, st_b1, gw_pw1, gw_pw2, land = _mixb_bwd(
        dh3, h2, aq, gq, dcq, sq, vecs, wg, tm, _rs_comm(_pair_reduce([gw_g1, gw_u1, gw_d1], keys), keys, None))
    gw_pw1, gw_pw2 = by_dest(gw_pw1, "b_pw1"), by_dest(gw_pw2, "b_pw2")
    gw_d0 = by_dest(_grad_w(gu0, dh2, FH, "grad_down0", tw), "d0")
    keys = ("b_pw1", "b_pw2", "d0")
    dh1, dg0, du0, st_f0, land = _ffn_bwd_dx(
        dh2, h1, g0, u0, fn0, wg, 0, tma, _rs_comm(_pair_reduce([gw_pw1, gw_pw2, gw_d0], keys), keys, land))
    gw_g0 = by_dest(_grad_w(dg0, n1, FH, "grad_gate0", tw), "g0")
    gw_u0 = by_dest(_grad_w(du0, n1, FH, "grad_up0", tw), "u0")
    keys = ("g0", "u0")
    dx, st_a, gw_in, gw_out, land = _mixa_bwd(
        dh1, xs, bq, cq, vq, ccq, yq, a_norm, cw_a, wg, tm,
        _rs_comm(_pair_reduce([gw_g0, gw_u0], keys), keys, land))
    gw_in, gw_out = by_dest(gw_in, "a_in"), by_dest(gw_out, "a_out")

    st_small = st_b.at[SM_A_CONV:SM_A_CONV + KA].set(st_a[1:1 + KA])
    sm_dest = st_small.reshape(SM_F32_ROWS, NDEV, 128).transpose(1, 0, 2)
    sm_dest = sm_dest.at[:, SM_B_PW1:SM_B_PW1 + 2, :].set(st_b1[0].reshape(NDEV, 2, 128))
    repl = jnp.concatenate([st_a[0:1], st_f0[0:1], st_f1[0:1], st_fin[0:1], st_fin[1:2],
                            jnp.zeros((3, D), F32)], axis=0)[None]
    last = ("a_out", "a_in")
    land, sm_land, rp_land = _pair_reduce([gw_out, gw_in], last, side=(sm_dest, repl), land=land)

    big = [("ffn_w_gate", ("g0", "g1"), False,
            rows_first(ffn_w_gate), rows_first(m_ffn_w_gate), rows_first(v_ffn_w_gate)),
           ("ffn_w_up", ("u0", "u1"), False, rows_first(ffn_w_up), rows_first(m_ffn_w_up), rows_first(v_ffn_w_up)),
           ("ffn_w_down", ("d0", "d1"), False, ffn_w_down, m_ffn_w_down, v_ffn_w_down),
           ("a_w_out", ("a_out",), False, a_w_out, m_a_w_out, v_a_w_out),
           ("b_w_pw1", ("b_pw1",), True, b_w_pw1, m_b_w_pw1, v_b_w_pw1),
           ("b_w_pw2", ("b_pw2",), False, b_w_pw2, m_b_w_pw2, v_b_w_pw2),
           ("a_w_in", ("a_in",), True, a_w_in, m_a_w_in, v_a_w_in)]
    res = {name: _finish_weight(land, keys, transposed, w, m, v, name) for name, keys, transposed, w, m, v in big}
    for name in ("ffn_w_gate", "ffn_w_up"):
        res[name] = [rows_first(a) for a in res[name]]

    g_small, g_repl, loss8 = _sum_small(sm_land, rp_land)
    loss = loss8[0, 0]

    small_names = ["b_norm", "b_b_pw1", "b_conv", "b_b_conv", "b_ln_g", "b_ln_b", "b_b_pw2", "a_conv",
                   "a_norm", "ffn_norm", "final_norm"]
    triples = [(b_norm, m_b_norm, v_b_norm), (b_b_pw1, m_b_b_pw1, v_b_b_pw1), (b_conv, m_b_conv, v_b_conv),
               (b_b_conv, m_b_b_conv, v_b_b_conv), (b_ln_g, m_b_ln_g, v_b_ln_g), (b_ln_b, m_b_ln_b, v_b_ln_b),
               (b_b_pw2, m_b_b_pw2, v_b_b_pw2), (a_conv, m_a_conv, v_a_conv), (a_norm, m_a_norm, v_a_norm),
               (ffn_norm, m_ffn_norm, v_ffn_norm),
               (fin, m_final_norm.reshape(1, D), v_final_norm.reshape(1, D))]
    for name, quad in zip(small_names, _adamw_small(g_small, g_repl, triples)):
        res[name] = quad
    res["final_norm"] = tuple(a.reshape(D) for a in res["final_norm"])

    order = ["a_norm", "a_w_in", "a_conv", "a_w_out", "b_norm", "b_w_pw1", "b_b_pw1", "b_conv", "b_b_conv",
             "b_ln_g", "b_ln_b", "b_w_pw2", "b_b_pw2", "ffn_norm", "ffn_w_gate", "ffn_w_up", "ffn_w_down",
             "final_norm"]
    out = [loss, dx.reshape(1, T, D)]
    for j in range(4):
        out += [res[k][j] for k in order]
    return tuple(out)
```

```python
import functools

import jax
import jax.numpy as jnp
from jax import lax
from jax.experimental import pallas as pl
from jax.experimental.pallas import tpu as pltpu

F32 = jnp.float32
BF16 = jnp.bfloat16
MESH = pl.DeviceIdType.MESH

D = 1024
FF = 2816
FH = FF // 2
NDEV = 8
KA = 3
KB = 31
HALO_A = 8
HALO_B = 32
CONV_ROWS = 64
RMS_EPS = 1e-6
LN_EPS = 1e-5
LR, B1, B2, ADAM_EPS, WD, STEP = 0.001, 0.9, 0.999, 1e-08, 0.01, 10

TM = 256
TM_A = 512
TM_DW = 2048
VMEM_LIMIT = 62 * 1024 * 1024

N_ROWS = {"a_in": 384, "a_out": 128, "b_pw1": 256, "b_pw2": 128, "g0": 352, "u0": 352, "d0": 352,
          "g1": 352, "u1": 352, "d1": 352}
NCHIP = 4


def _offsets(order):
    off, o = {}, 0
    for k in order:
        off[k] = o
        o += N_ROWS[k]
    return off, o


PK_ORDER = ("a_in", "a_out", "g0", "u0", "d0", "b_pw1", "b_pw2", "g1", "u1", "d1")
PK_OFF, PK_ROWS = _offsets(PK_ORDER)
LD_ORDER = ("g1", "u1", "d1", "b_pw1", "b_pw2", "d0", "g0", "u0", "a_out", "a_in")
LD_OFF, N_WROWS = _offsets(LD_ORDER)


def _span(off, keys):
    return off[keys[0]], sum(N_ROWS[k] for k in keys)


SM_B_NORM, SM_B_PW1, SM_B_CONV, SM_B_BCONV, SM_LN_G, SM_LN_B, SM_B_PW2, SM_A_CONV = 0, 1, 3, 34, 35, 36, 37, 38
SM_F32_ROWS = 64


def _pcall(body, **kw):
    return pl.pallas_call(body, **kw)


def _cparams(sem=None):
    return pltpu.CompilerParams(dimension_semantics=sem, vmem_limit_bytes=VMEM_LIMIT)


def _dot(a, b):
    return jnp.dot(a, b, preferred_element_type=F32)


def _dot_nt(a, b):
    return lax.dot_general(a, b, (((1,), (1,)), ((), ())), preferred_element_type=F32)


def _dot_tn(a, b):
    return lax.dot_general(a, b, (((0,), (0,)), ((), ())), preferred_element_type=F32)


def _sigmoid(v):
    return 1.0 / (1.0 + jnp.exp(-v))


def _rms_stat(x):
    return lax.rsqrt(jnp.mean(x * x, axis=-1, keepdims=True) + RMS_EPS)


def _rms_bwd(dn, x, r, gamma):
    dng = dn * gamma
    dx = r * dng - x * (r * r * r) * jnp.mean(dng * x, axis=-1, keepdims=True)
    return dx, jnp.sum(dn * x * r, axis=0, keepdims=True)


def _weight_loads(wg_ref, plan, sems):
    copies = []
    for j, (dst, key) in enumerate(plan):
        off, n = PK_OFF[key], N_ROWS[key]
        copies += [pltpu.make_async_copy(
            wg_ref.at[d, pl.ds(off, n), :], dst.at[pl.ds(d * n, n), :], sems.at[j * NDEV + d])
            for d in range(NDEV)]

    def load():
        for cp in copies:
            cp.start()
        for cp in copies:
            cp.wait()

    return load


_ANY = pl.BlockSpec(memory_space=pl.ANY)


def _row_spec(tm, width, rev_nt=None):
    if rev_nt is None:
        return pl.BlockSpec((tm, width), lambda i: (i, 0))
    return pl.BlockSpec((tm, width), lambda i: (rev_nt - 1 - i, 0))


def _full_spec(shape):
    return pl.BlockSpec(shape, lambda *_: (0,) * len(shape))


def _sds(shape, dtype):
    return jax.ShapeDtypeStruct(shape, dtype)


def _mesh_pos():
    return lax.axis_index("x"), lax.axis_index("y"), lax.axis_index("c")


def _lin(p):
    return 4 * p[0] + 2 * p[1] + p[2]


def _ag_exchange(src, slot, send_sems, recv_sems, local_sem):
    x, y, c = _mesh_pos()
    me, sibling = (x, y, c), (x, y, 1 - c)
    chips = [(1 - x, y), (x, 1 - y), (1 - x, 1 - y)]

    def copy(k, block, to, own=False):
        return pltpu.make_async_remote_copy(
            src_ref=src if own else slot(block), dst_ref=slot(block),
            send_sem=send_sems.at[k], recv_sem=recv_sems.at[k], device_id=to, device_id_type=MESH)

    mine = pltpu.make_async_copy(src, slot(me), local_sem)
    first = [copy(0, me, sibling, own=True)]
    first += [copy(1 + j, me, (*chip, c), own=True) for j, chip in enumerate(chips)]
    passed = [copy(4 + j, (*chip, c), sibling) for j, chip in enumerate(chips)]

    def start():
        mine.start()
        for cp in first:
            cp.start()

    def forward():
        for j, chip in enumerate(chips):
            copy(1 + j, (*chip, c), me).wait_recv()
            passed[j].start()

    def finish():
        copy(0, sibling, me).wait_recv()
        for j, chip in enumerate(chips):
            copy(4 + j, (*chip, 1 - c), me).wait_recv()
        for cp in first + passed:
            cp.wait_send()
        mine.wait()

    return start, forward, finish


def _chip_exchange(p_ref, land_ref, send_sems, recv_sems, keys):
    x, y, c = _mesh_pos()
    l0, rows = _span(LD_OFF, keys)
    dst = land_ref.at[2 * x + y, pl.ds(l0, rows), :]
    peers = [((x + (j >> 1)) % 2, (y + (j & 1)) % 2) for j in range(NCHIP)]

    def copy(j):
        tx, ty = peers[j]
        src = p_ref.at[2 * tx + ty]
        if j == 0:
            return pltpu.make_async_copy(src, dst, send_sems.at[0])
        return pltpu.make_async_remote_copy(
            src_ref=src, dst_ref=dst, send_sem=send_sems.at[j], recv_sem=recv_sems.at[j],
            device_id=(tx, ty, c), device_id_type=MESH)

    def start():
        for j in range(NCHIP):
            copy(j).start()

    def finish():
        for j in range(NCHIP):
            copy(j).wait()

    return start, finish


def _all_to_all_f32(src_for, dst_ref, send_sems, recv_sems):
    x, y, c = _mesh_pos()
    dst = dst_ref.at[_lin((x, y, c))]
    peers = [((x + ((j >> 2) & 1)) % 2, (y + ((j >> 1) & 1)) % 2, (c + (j & 1)) % 2) for j in range(NDEV)]

    def copy(j):
        if j == 0:
            return pltpu.make_async_copy(src_for(_lin(peers[0])), dst, send_sems.at[0])
        return pltpu.make_async_remote_copy(
            src_ref=src_for(_lin(peers[j])), dst_ref=dst, send_sem=send_sems.at[j], recv_sem=recv_sems.at[j],
            device_id=peers[j], device_id_type=MESH)

    def start():
        for j in range(NDEV):
            copy(j).start()

    def finish():
        for j in range(NDEV):
            copy(j).wait()

    return start, finish


class _Comm:
    def __init__(self, ins, alias_in, out_shape, scratch, make, gives_wg, middle_step=0):
        self.ins, self.alias_in, self.out_shape = ins, alias_in, out_shape
        self.scratch, self.make, self.gives_wg, self.middle_step = scratch, make, gives_wg, middle_step


def _ag_comm(shard, wg, keys, forward_step=0):
    def make(c_ins, c_out, sc):
        return _weights_exchange(c_ins[0], c_out, sc[0], sc[1], sc[2], keys)

    return _Comm([shard, wg], 1, _sds(wg.shape, BF16),
                 [pltpu.SemaphoreType.DMA((7,)), pltpu.SemaphoreType.DMA((7,)), pltpu.SemaphoreType.DMA],
                 make, True, forward_step)


def _weights_exchange(shard_ref, wg_ref, send_sems, recv_sems, local_sem, keys):
    r0, nr = _span(PK_OFF, keys)
    return _ag_exchange(shard_ref.at[pl.ds(r0, nr), :], lambda p: wg_ref.at[_lin(p), pl.ds(r0, nr), :],
                        send_sems, recv_sems, local_sem)


def _rs_comm(psum, keys, land):
    def make(c_ins, c_out, sc):
        start, finish = _chip_exchange(c_ins[0], c_out, sc[0], sc[1], keys)
        return start, None, finish

    ins = [psum] + ([] if land is None else [land])
    return _Comm(ins, None if land is None else 1, _sds((NCHIP, N_WROWS, D), BF16),
                 [pltpu.SemaphoreType.DMA((NCHIP,)), pltpu.SemaphoreType.DMA((NCHIP,))], make, False)


def _hosted_call(body, name, nt, in_specs, out_specs, out_shape, scratch, args, comm):
    if comm is None:
        return _pcall(body, name=name, grid=(nt,), in_specs=in_specs, out_specs=tuple(out_specs),
                      out_shape=tuple(out_shape), scratch_shapes=scratch,
                      compiler_params=_cparams(("arbitrary",)))(*args)
    n_in, n_out, n_sc, n_cin = len(in_specs), len(out_specs), len(scratch), len(comm.ins)

    def wrapped(*refs):
        ins = refs[:n_in]
        c_ins = refs[n_in:n_in + n_cin]
        outs = refs[n_in + n_cin:n_in + n_cin + n_out]
        c_out = refs[n_in + n_cin + n_out]
        sc = refs[n_in + n_cin + n_out + 1:n_in + n_cin + n_out + 1 + n_sc]
        c_sc = refs[n_in + n_cin + n_out + 1 + n_sc:]
        start, middle, finish = comm.make(c_ins, c_out, c_sc)
        pl.when(pl.program_id(0) == 0)(start)
        if comm.gives_wg:
            body(*ins, c_out, *outs, *sc)
        else:
            body(*ins, *outs, *sc)
        if middle is not None:
            pl.when(pl.program_id(0) == max(nt - 1 - comm.middle_step, 0))(middle)
        pl.when(pl.program_id(0) == nt - 1)(finish)

    aliases = {} if comm.alias_in is None else {n_in + comm.alias_in: n_out}
    res = _pcall(wrapped, name=name, grid=(nt,),
                 in_specs=list(in_specs) + [_ANY] * n_cin, out_specs=tuple(out_specs) + (_ANY,),
                 out_shape=tuple(out_shape) + (comm.out_shape,),
                 scratch_shapes=list(scratch) + list(comm.scratch),
                 input_output_aliases=aliases,
                 compiler_params=_cparams(("arbitrary",)))(*args, *comm.ins)
    return res


def _pack_shard(weights, smalls):
    plan = (("a_in", 0, 0, True), ("a_out", 1, 0, False), ("b_pw1", 2, 0, True), ("b_pw2", 3, 0, False),
            ("g0", 4, 0, False), ("u0", 5, 0, False), ("d0", 6, 0, False),
            ("g1", 4, 1, False), ("u1", 5, 1, False), ("d1", 6, 1, False))
    n_pad = 384

    def body(*refs):
        out, sm, pad = refs[-3:]
        for key, idx, layer, transposed in plan:
            n = N_ROWS[key]
            val = refs[idx][layer]
            if transposed:
                if n % 128:
                    pad[:, pl.ds(0, n)] = val
                    pad[:, pl.ds(n, n_pad - n)] = jnp.zeros((D, n_pad - n), F32)
                    val = pad[...]
                val = val.T[:n]
            out[pl.ds(PK_OFF[key], n), :] = val.astype(BF16)
        b_norm, b_b_pw1, b_conv, b_b_conv, b_ln_g, b_ln_b, b_b_pw2, a_conv = refs[len(weights):-3]
        sm[...] = jnp.zeros((SM_F32_ROWS, 128), F32)
        sm[pl.ds(SM_B_PW1, 1), :] = b_b_pw1[:, pl.ds(0, 128)]
        sm[pl.ds(SM_B_PW1 + 1, 1), :] = b_b_pw1[:, pl.ds(128, 128)]
        sm[pl.ds(SM_B_CONV, KB), :] = b_conv[0]
        sm[pl.ds(SM_A_CONV, KA), :] = a_conv[0]
        for row, ref in ((SM_B_NORM, b_norm), (SM_B_BCONV, b_b_conv), (SM_LN_G, b_ln_g), (SM_LN_B, b_ln_b),
                         (SM_B_PW2, b_b_pw2)):
            sm[pl.ds(row, 1), :] = ref[...]

    vm = pl.BlockSpec(memory_space=pltpu.VMEM)
    return _pcall(
        body, name="pack_shard",
        out_shape=(_sds((PK_ROWS, D), BF16), _sds((SM_F32_ROWS, 128), F32)),
        in_specs=[vm] * (len(weights) + len(smalls)), out_specs=(vm, vm),
        scratch_shapes=[pltpu.VMEM((D, n_pad), F32)],
        compiler_params=_cparams(),
    )(*weights, *smalls)


def _all_gather_first(shard, small):
    def body(x_ref, s_ref, wg_ref, sg_ref, send_w, recv_w, local_w, send_s, recv_s, local_s):
        start_w, forward_w, finish_w = _weights_exchange(x_ref, wg_ref, send_w, recv_w, local_w, ("a_in", "a_out"))
        start_s, forward_s, finish_s = _ag_exchange(s_ref, lambda p: sg_ref.at[_lin(p)], send_s, recv_s, local_s)
        start_s()
        start_w()
        forward_s()
        forward_w()
        finish_s()
        finish_w()

    sems = [pltpu.SemaphoreType.DMA((7,)), pltpu.SemaphoreType.DMA((7,)), pltpu.SemaphoreType.DMA]
    return _pcall(
        body, name="all_gather_first",
        out_shape=(_sds((NDEV, PK_ROWS, D), BF16), _sds((NDEV, SM_F32_ROWS, 128), F32)),
        in_specs=[_ANY, _ANY], out_specs=(_ANY, _ANY), scratch_shapes=sems + sems,
    )(shard, small)


def _pair_reduce(grads, keys, side=None, land=None):
    l0, rows = _span(LD_OFF, keys)
    nk = len(keys)
    n_side = 0 if side is None else 2
    n_land = 0 if land is None else 1

    def body(c_ref, *refs):
        mine = refs[:nk]
        whole = refs[nk:2 * nk]
        n_in = 2 * nk + n_side + n_land
        o_ref = refs[n_in]
        got, send_sem, recv_sems = refs[n_in + 1 + n_side:n_in + 4 + n_side]
        t = pl.program_id(0)
        x, y, c = _mesh_pos()
        sibling = (x, y, 1 - c)
        if side is not None:
            sm_ref, rp_ref = refs[2 * nk:2 * nk + 2]
            sm_land, rp_land = refs[n_in + 1:n_in + 3]
            s2, r2, s3, r3 = refs[n_in + 6:n_in + 10]
            start_s, finish_s = _all_to_all_f32(lambda d: sm_ref.at[d], sm_land, s2, r2)
            start_r, finish_r = _all_to_all_f32(lambda d: rp_ref.at[0], rp_land, s3, r3)

        @pl.when(t == 0)
        def _():
            if side is not None:
                start_s()
                start_r()
            for ref, key in zip(whole, keys):
                for d in range(NCHIP):
                    pltpu.make_async_remote_copy(
                        src_ref=ref.at[2 * d + (1 - c)], dst_ref=got.at[d, pl.ds(LD_OFF[key] - l0, N_ROWS[key]), :],
                        send_sem=send_sem, recv_sem=recv_sems.at[d], device_id=sibling, device_id_type=MESH).start()

        pltpu.make_async_remote_copy(src_ref=got.at[t], dst_ref=got.at[t], send_sem=send_sem,
                                     recv_sem=recv_sems.at[t], device_id=sibling, device_id_type=MESH).wait_recv()
        sums = o_ref if land is None else refs[-3].at[t]
        for ref, key in zip(mine, keys):
            sl = pl.ds(LD_OFF[key] - l0, N_ROWS[key])
            sums[sl, :] = (ref[...].astype(F32) + got[t, sl, :].astype(F32)).astype(BF16)

        if land is not None:
            psum, chip_send, chip_recv = refs[-3:]
            my_chip = 2 * x + y
            dst = o_ref.at[my_chip, pl.ds(l0, rows), :]
            hop = jnp.bitwise_xor(t, my_chip)

            @pl.when(t == my_chip)
            def _():
                pltpu.make_async_copy(psum.at[t], dst, chip_send.at[0]).start()

            @pl.when(t != my_chip)
            def _():
                pltpu.make_async_remote_copy(
                    src_ref=psum.at[t], dst_ref=dst, send_sem=chip_send.at[hop], recv_sem=chip_recv.at[hop],
                    device_id=(t // 2, t % 2, c), device_id_type=MESH).start()

        @pl.when(t == NCHIP - 1)
        def _():
            pltpu.make_async_remote_copy(src_ref=got, dst_ref=got, send_sem=send_sem, recv_sem=recv_sems.at[0],
                                         device_id=sibling, device_id_type=MESH).wait_send()
            if land is not None:
                one = o_ref.at[0, pl.ds(l0, rows), :]
                pltpu.make_async_copy(one, one, chip_send.at[0]).wait()
                for j in range(1, NCHIP):
                    pltpu.make_async_remote_copy(src_ref=one, dst_ref=one, send_sem=chip_send.at[j],
                                                 recv_sem=chip_recv.at[j], device_id=sibling, device_id_type=MESH).wait()
            if side is not None:
                finish_s()
                finish_r()

    if land is None:
        out_specs = [pl.BlockSpec((None, rows, D), lambda t, c: (t, 0, 0))]
        out_shape = [_sds((NCHIP, rows, D), BF16)]
    else:
        out_specs, out_shape = [_ANY], [_sds((NCHIP, N_WROWS, D), BF16)]
    scratch = [pltpu.VMEM((NCHIP, rows, D), BF16), pltpu.SemaphoreType.DMA, pltpu.SemaphoreType.DMA((NCHIP,))]
    if side is not None:
        out_specs += [_ANY, _ANY]
        out_shape += [_sds((NDEV, SM_F32_ROWS, 128), F32), _sds((NDEV, 8, D), F32)]
        scratch += [pltpu.SemaphoreType.DMA((NDEV,))] * 4
    if land is not None:
        scratch += [pltpu.VMEM((NCHIP, rows, D), BF16), pltpu.SemaphoreType.DMA((NCHIP,)),
                    pltpu.SemaphoreType.DMA((NCHIP,))]
    grid_spec = pltpu.PrefetchScalarGridSpec(
        num_scalar_prefetch=1, grid=(NCHIP,),
        in_specs=[pl.BlockSpec((None, N_ROWS[k], D), lambda t, c: (2 * t + c[0], 0, 0)) for k in keys]
        + [_ANY] * (nk + n_side + n_land),
        out_specs=tuple(out_specs), scratch_shapes=scratch)
    core = lax.axis_index("c").astype(jnp.int32).reshape(1)
    aliases = {} if land is None else {1 + 2 * nk + n_side: 0}
    res = _pcall(
        body, name="pair_reduce_" + keys[0], grid_spec=grid_spec, out_shape=tuple(out_shape),
        input_output_aliases=aliases, compiler_params=_cparams(("arbitrary",)),
    )(core, *grads, *grads, *(side or ()), *(() if land is None else (land,)))
    return res[0] if len(res) == 1 else res


def _chip_start(psum):
    hbm = pl.BlockSpec(memory_space=pltpu.HBM)
    sem = pl.BlockSpec(memory_space=pltpu.SEMAPHORE)

    def body(p_ref, land_ref, s1, s2, s3, r1, r2, r3, p_thru, land_thru, token):
        x, y, c = _mesh_pos()
        for j, (send_sem, recv_sem) in enumerate(((s1, r1), (s2, r2), (s3, r3)), start=1):
            tx, ty = (x + (j >> 1)) % 2, (y + (j & 1)) % 2
            pltpu.make_async_remote_copy(
                src_ref=p_ref.at[2 * tx + ty], dst_ref=land_ref.at[2 * x + y], send_sem=send_sem, recv_sem=recv_sem,
                device_id=(tx, ty, c), device_id_type=MESH).start()
        token[...] = jnp.zeros_like(token)

    dma = pltpu.SemaphoreType.DMA(())
    buf = pltpu.HBM(psum.shape, psum.dtype)
    res = _pcall(
        body, name="chip_exchange_start",
        out_shape=(dma,) * 6 + (buf, buf, _sds((8, 128), F32)),
        in_specs=(hbm, hbm), out_specs=(sem,) * 6 + (hbm, hbm, pl.BlockSpec(memory_space=pltpu.VMEM)),
        input_output_aliases={0: 6, 1: 7},
        compiler_params=pltpu.CompilerParams(has_side_effects=pltpu.SideEffectType.DATAFLOW_SIDE_EFFECTING),
    )(pltpu.with_memory_space_constraint(psum, pltpu.HBM),
      pltpu.with_memory_space_constraint(lax.empty(psum.shape, psum.dtype), pltpu.HBM))
    return res[:6], res[6], res[7], res[8]


def _chip_wait(sems, p_thru, land_thru, after):
    hbm = pl.BlockSpec(memory_space=pltpu.HBM)
    sem = pl.BlockSpec(memory_space=pltpu.SEMAPHORE)

    def body(p_ref, land_ref, s1, s2, s3, r1, r2, r3, after_ref, p_dead, got_ref):
        x, y, c = _mesh_pos()
        for send_sem, recv_sem in ((s1, r1), (s2, r2), (s3, r3)):
            copy = pltpu.make_async_remote_copy(
                src_ref=p_ref.at[0], dst_ref=land_ref.at[0], send_sem=send_sem, recv_sem=recv_sem,
                device_id=(x, y, 1 - c), device_id_type=MESH)
            copy.wait_send()
            copy.wait_recv()

    buf = pltpu.HBM(p_thru.shape, p_thru.dtype)
    return _pcall(
        body, name="chip_exchange_wait", out_shape=(buf, buf),
        in_specs=(hbm, hbm) + (sem,) * 6 + (_ANY,), out_specs=(hbm, hbm),
        input_output_aliases={0: 0, 1: 1},
        compiler_params=pltpu.CompilerParams(has_side_effects=pltpu.SideEffectType.DATAFLOW_SIDE_EFFECTING),
    )(p_thru, land_thru, *sems, after)


def _own_slot(land, psum):
    def body(land_in, p_ref, land_ref, sem):
        x, y, _ = _mesh_pos()
        cp = pltpu.make_async_copy(p_ref.at[2 * x + y], land_ref.at[2 * x + y], sem)
        cp.start()
        cp.wait()

    return _pcall(
        body, name="own_slot", out_shape=_sds(land.shape, land.dtype),
        in_specs=[_ANY, _ANY], out_specs=_ANY, input_output_aliases={0: 0},
        scratch_shapes=[pltpu.SemaphoreType.DMA],
    )(land, psum)


def _mixa_fwd(x, gam, cw, wg, tm, comm):
    T = x.shape[0]

    def body(x_ref, gam_ref, cw_ref, wg_ref, h_ref, b_ref, c_ref, v_ref, cc_ref, y_ref,
             win, wout, buf, sems):
        first = pl.program_id(0) == 0
        load_w = _weight_loads(wg_ref, [(win, "a_in"), (wout, "a_out")], sems)

        @pl.when(first)
        def _():
            load_w()
            buf[pl.ds(0, HALO_A), :] = jnp.zeros((HALO_A, D), F32)

        xv = x_ref[...]
        nb = (xv * _rms_stat(xv) * gam_ref[...]).astype(BF16)
        bv = _dot_nt(nb, win[pl.ds(0, D), :])
        cval = _dot_nt(nb, win[pl.ds(D, D), :])
        vval = _dot_nt(nb, win[pl.ds(2 * D, D), :])
        cv = cval * vval
        buf[pl.ds(HALO_A, tm), :] = cv
        cc = cw_ref[pl.ds(KA - 1, 1), :] * cv
        for k in range(KA - 1):
            cc = cc + cw_ref[pl.ds(k, 1), :] * buf[pl.ds(HALO_A - (KA - 1) + k, tm), :]
        buf[pl.ds(0, HALO_A), :] = buf[pl.ds(tm, HALO_A), :]
        yb = (bv * cc).astype(BF16)
        b_ref[...] = bv.astype(BF16)
        c_ref[...] = cval.astype(BF16)
        v_ref[...] = vval.astype(BF16)
        cc_ref[...] = cc.astype(BF16)
        y_ref[...] = yb
        h_ref[...] = xv + _dot(yb, wout[...])

    act = _sds((T, D), BF16)
    return _hosted_call(
        body, "mixa_fwd", T // tm,
        in_specs=[_row_spec(tm, D), _full_spec((1, D)), _full_spec((KA, D))],
        out_specs=[_row_spec(tm, D) for _ in range(6)],
        out_shape=(_sds((T, D), F32),) + (act,) * 5,
        scratch=[pltpu.VMEM((3 * D, D), BF16), pltpu.VMEM((D, D), BF16),
                 pltpu.VMEM((HALO_A + tm, D), F32), pltpu.SemaphoreType.DMA((2 * NDEV,))],
        args=(x, gam, cw), comm=comm)


def _ffn_fwd(h, gam, wg, layer, tm, comm, head=None):
    T = h.shape[0]
    kg, ku, kd = "g%d" % layer, "u%d" % layer, "d%d" % layer
    n_head = 0 if head is None else 2

    def body(*refs):
        h_ref, gam_ref = refs[:2]
        wg_ref = refs[2 + n_head]
        o_ref, n_ref, g_ref, u_ref, gu_ref = refs[3 + n_head:8 + n_head]
        wgt, wut, wd, sems = refs[-4:]

        first = pl.program_id(0) == 0
        load_w = _weight_loads(wg_ref, [(wgt, kg), (wut, ku), (wd, kd)], sems)
        pl.when(first)(load_w)

        hv = h_ref[...]
        nb = (hv * _rms_stat(hv) * gam_ref[...]).astype(BF16)
        n_ref[...] = nb
        out = hv
        for f in range(2):
            cols = pl.ds(f * FH, FH)
            g = _dot_nt(nb, wgt[cols, :])
            u = _dot_nt(nb, wut[cols, :])
            gu = (g * _sigmoid(g) * u).astype(BF16)
            g_ref[:, cols] = g.astype(BF16)
            u_ref[:, cols] = u.astype(BF16)
            gu_ref[:, cols] = gu
            out = out + _dot(gu, wd[cols, :])
        if head is None:
            o_ref[...] = out
        else:
            t_ref, fin_ref, st_ref = refs[2], refs[3], refs[8 + n_head]

            @pl.when(pl.program_id(0) == 0)
            def _():
                st_ref[...] = jnp.zeros((8, D), F32)

            gamma = fin_ref[...]
            r = _rms_stat(out)
            err = out * r * gamma - t_ref[...]
            dx, dgam = _rms_bwd(err * (1.0 / D), out, r, gamma)
            o_ref[...] = dx
            st_ref[pl.ds(0, 1), :] += dgam
            st_ref[pl.ds(1, 1), :] += (0.5 / D) * jnp.sum(err * err, axis=0, keepdims=True)

    pre = _sds((T, FF), BF16)
    in_specs = [_row_spec(tm, D), _full_spec((1, D))]
    args = (h, gam)
    out_specs = [_row_spec(tm, D), _row_spec(tm, D), _row_spec(tm, FF), _row_spec(tm, FF), _row_spec(tm, FF)]
    out_shape = (_sds((T, D), F32), _sds((T, D), BF16), pre, pre, pre)
    if head is not None:
        in_specs, args = in_specs + [_row_spec(tm, D), _full_spec((1, D))], args + tuple(head)
        out_specs, out_shape = out_specs + [_full_spec((8, D))], out_shape + (_sds((8, D), F32),)
    if comm is None:
        in_specs, args = in_specs + [_ANY], args + (wg,)
    return _hosted_call(
        body, "ffn%d_fwd" % layer, T // tm,
        in_specs=in_specs, out_specs=out_specs, out_shape=out_shape,
        scratch=[pltpu.VMEM((FF, D), BF16)] * 3 + [pltpu.SemaphoreType.DMA((3 * NDEV,))],
        args=args, comm=comm)


def _shifted_copies(buf, shf, tm):
    for r in range(1, 8):
        shf[r - 1] = buf[pl.ds(r, tm + HALO_B - 8), :]


def _broadcast_taps(vec_ref, wb):
    for k in range(KB):
        wb[k] = jnp.broadcast_to(vec_ref[pl.ds(SM_B_CONV + k, 1), :], (8, D))
    wb[KB] = jnp.broadcast_to(vec_ref[pl.ds(SM_B_BCONV, 1), :], (8, D))


def _taps_by_shift_residue(taps):
    groups = {}
    for k, shift in taps:
        q, r = divmod(shift, 8)
        groups.setdefault(r, []).append((k, q))
    return sorted(groups.items())


def _window(buf, shf, base, r, q0, n_groups, lanes):
    rows = pl.ds(base + 8 * q0, 8 * n_groups)
    v = buf[rows, lanes] if r == 0 else shf[r - 1, rows, lanes]
    return [v[8 * i:8 * i + 8] for i in range(n_groups)]


def _long_conv(buf, shf, wb, out_ref, tm, taps, bias_row):
    n_acc = CONV_ROWS // 8
    groups = _taps_by_shift_residue(taps)
    for col in range(D // 128):
        lanes = pl.ds(128 * col, 128)

        def rows(i, carry, lanes=lanes):
            base = i * CONV_ROWS
            init = jnp.zeros((8, 128), F32) if bias_row is None else wb[bias_row, :, lanes]
            accs = [init] * n_acc
            for r, lst in groups:
                q0, q1 = min(q for _, q in lst), max(q for _, q in lst)
                win = _window(buf, shf, base, r, q0, n_acc + q1 - q0, lanes)
                for k, q in lst:
                    wk = wb[k, :, lanes]
                    accs = [acc + wk * win[h + q - q0] for h, acc in enumerate(accs)]
            for h, acc in enumerate(accs):
                out_ref[pl.ds(base + 8 * h, 8), lanes] = acc
            return carry

        for i in range(tm // CONV_ROWS):
            rows(i, 0)


def _long_conv_grad_taps(buf, shf, x_ref, wacc, tm):
    n_acc = CONV_ROWS // 8
    groups = _taps_by_shift_residue([(KB - 1 - j, j) for j in range(KB)])
    for col in range(D // 128):
        lanes = pl.ds(128 * col, 128)

        def rows(i, carry, lanes=lanes):
            base = i * CONV_ROWS
            xv = x_ref[pl.ds(base, CONV_ROWS), lanes]
            xs = [xv[8 * h:8 * h + 8] for h in range(n_acc)]
            for r, lst in groups:
                q0, q1 = min(q for _, q in lst), max(q for _, q in lst)
                win = _window(buf, shf, base, r, q0, n_acc + q1 - q0, lanes)
                for k, q in lst:
                    prod = [x * win[h + q - q0] for h, x in enumerate(xs)]
                    wacc[k, :, lanes] += functools.reduce(lambda a, b: a + b, prod)
            return carry

        for i in range(tm // CONV_ROWS):
            rows(i, 0)


def _ln_stats(dc):
    mu = jnp.mean(dc, axis=-1, keepdims=True)
    xc = dc - mu
    rstd = lax.rsqrt(jnp.mean(xc * xc, axis=-1, keepdims=True) + LN_EPS)
    return xc * rstd, rstd


def _mixb_fwd(h, vecs, bias1, wg, tm, comm):
    T = h.shape[0]

    def body(h_ref, vec_ref, b1_ref, wg_ref, o_ref, a_ref, g_ref, dc_ref, s_ref, w1, w2, buf, shf, wb, sems):
        first = pl.program_id(0) == 0
        load_w = _weight_loads(wg_ref, [(w1, "b_pw1"), (w2, "b_pw2")], sems)

        @pl.when(first)
        def _():
            load_w()
            buf[pl.ds(0, HALO_B), :] = jnp.zeros((HALO_B, D), F32)
            _broadcast_taps(vec_ref, wb)

        hv = h_ref[...]
        nb = (hv * _rms_stat(hv) * vec_ref[pl.ds(SM_B_NORM, 1), :]).astype(BF16)
        a = _dot_nt(nb, w1[pl.ds(0, D), :]) + b1_ref[:, pl.ds(0, D)]
        g = _dot_nt(nb, w1[pl.ds(D, D), :]) + b1_ref[:, pl.ds(D, D)]
        a_ref[...] = a.astype(BF16)
        g_ref[...] = g.astype(BF16)
        buf[pl.ds(HALO_B, tm), :] = a * _sigmoid(g)
        _shifted_copies(buf, shf, tm)

        _long_conv(buf, shf, wb, dc_ref, tm, [(k, HALO_B - (KB - 1) + k) for k in range(KB)], KB)
        buf[pl.ds(0, HALO_B), :] = buf[pl.ds(tm, HALO_B), :]
        xhat, _ = _ln_stats(dc_ref[...])
        ln = xhat * vec_ref[pl.ds(SM_LN_G, 1), :] + vec_ref[pl.ds(SM_LN_B, 1), :]
        s = (ln * _sigmoid(ln)).astype(BF16)
        s_ref[...] = s
        o_ref[...] = hv + _dot(s, w2[...]) + vec_ref[pl.ds(SM_B_PW2, 1), :]

    act = _sds((T, D), BF16)
    return _hosted_call(
        body, "mixb_fwd", T // tm,
        in_specs=[_row_spec(tm, D), _full_spec((SM_F32_ROWS, D)), _full_spec((1, 2 * D))],
        out_specs=[_row_spec(tm, D) for _ in range(5)],
        out_shape=(_sds((T, D), F32), act, act, _sds((T, D), F32), act),
        scratch=[pltpu.VMEM((2 * D, D), BF16), pltpu.VMEM((D, D), BF16),
                 pltpu.VMEM((HALO_B + tm, D), F32), pltpu.VMEM((7, HALO_B + tm - 8, D), F32),
                 pltpu.VMEM((KB + 1, 8, D), F32), pltpu.SemaphoreType.DMA((2 * NDEV,))],
        args=(h, vecs, bias1), comm=comm)


def _ffn_bwd_dx(dh, h, g, u, gam, wg, layer, tm, comm):
    T = h.shape[0]
    kg, ku, kd = "g%d" % layer, "u%d" % layer, "d%d" % layer

    def body(dh_ref, h_ref, g_ref, u_ref, gam_ref, wg_ref, o_ref, dg_ref, du_ref, st_ref, wgt, wut, wd, sems):
        first = pl.program_id(0) == 0
        load_w = _weight_loads(wg_ref, [(wd, kd), (wgt, kg), (wut, ku)], sems)

        @pl.when(first)
        def _():
            load_w()
            st_ref[...] = jnp.zeros((8, D), F32)

        dhv = dh_ref[...]
        dhb = dhv.astype(BF16)
        dn = jnp.zeros_like(dhv)
        for f in range(2):
            cols = pl.ds(f * FH, FH)
            dgu = _dot_nt(dhb, wd[cols, :])
            gv = g_ref[:, cols].astype(F32)
            sg = _sigmoid(gv)
            du = (dgu * gv * sg).astype(BF16)
            dg = (dgu * u_ref[:, cols].astype(F32) * (sg * (1.0 + gv * (1.0 - sg)))).astype(BF16)
            dg_ref[:, cols] = dg
            du_ref[:, cols] = du
            dn = dn + _dot(dg, wgt[cols, :]) + _dot(du, wut[cols, :])
        hv = h_ref[...]
        dx, dgam = _rms_bwd(dn, hv, _rms_stat(hv), gam_ref[...])
        o_ref[...] = dhv + dx
        st_ref[pl.ds(0, 1), :] += dgam

    pre = _sds((T, FF), BF16)
    return _hosted_call(
        body, "ffn%d_bwd_dx" % layer, T // tm,
        in_specs=[_row_spec(tm, D), _row_spec(tm, D), _row_spec(tm, FF), _row_spec(tm, FF),
                  _full_spec((1, D)), _ANY],
        out_specs=[_row_spec(tm, D), _row_spec(tm, FF), _row_spec(tm, FF), _full_spec((8, D))],
        out_shape=(_sds((T, D), F32), pre, pre, _sds((8, D), F32)),
        scratch=[pltpu.VMEM((FF, D), BF16)] * 3 + [pltpu.SemaphoreType.DMA((3 * NDEV,))],
        args=(dh, h, g, u, gam, wg), comm=comm)


def _grad_w(lhs, rhs, mc, name, tm):
    T, M = lhs.shape
    nt = T // tm

    def body(l_ref, r_ref, o_ref, acc):
        i = pl.program_id(1)

        @pl.when(i == 0)
        def _():
            acc[...] = jnp.zeros((mc, D), F32)

        acc[...] += _dot_tn(l_ref[...], r_ref[...].astype(BF16))

        @pl.when(i == nt - 1)
        def _():
            o_ref[...] = acc[...].astype(BF16)

    return _pcall(
        body, name=name, grid=(M // mc, nt),
        out_shape=_sds((M, D), BF16),
        in_specs=[pl.BlockSpec((tm, mc), lambda j, i: (i, j)), pl.BlockSpec((tm, D), lambda j, i: (i, 0))],
        out_specs=pl.BlockSpec((mc, D), lambda j, i: (j, 0)),
        scratch_shapes=[pltpu.VMEM((mc, D), F32)],
        compiler_params=_cparams(("arbitrary", "arbitrary")),
    )(lhs, rhs)


def _mixb_bwd(dh, h, a, g, dc, s, vecs, wg, tm, comm):
    T = h.shape[0]
    nt = T // tm

    def body(dh_ref, h_ref, a_ref, g_ref, dc_ref, s_ref, vec_ref, wg_ref, o_ref, st_ref, sb_ref, g1_ref, g2_ref,
             w1, w2, buf, shf, glu_s, dglu_s, wacc, wb, acc1, acc2, stage, sems, out_sems):
        first = pl.program_id(0) == 0
        load_w = _weight_loads(wg_ref, [(w2, "b_pw2"), (w1, "b_pw1")], sems)

        @pl.when(first)
        def _():
            load_w()
            buf[pl.ds(tm, HALO_B), :] = jnp.zeros((HALO_B, D), F32)
            wacc[...] = jnp.zeros((KB, 8, D), F32)
            _broadcast_taps(vec_ref, wb)
            st_ref[...] = jnp.zeros((SM_F32_ROWS, D), F32)
            sb_ref[...] = jnp.zeros((8, 2 * D), F32)
            acc1[...] = jnp.zeros((2 * D, D), F32)
            acc2[...] = jnp.zeros((D, D), F32)

        def acc(row, val):
            st_ref[pl.ds(row, 1), :] += jnp.sum(val, axis=0, keepdims=True)

        dhv = dh_ref[...]
        dhb = dhv.astype(BF16)
        acc(SM_B_PW2, dhv)
        acc2[...] += _dot_tn(s_ref[...], dhb)
        ds = _dot_nt(dhb, w2[...])
        xhat, rstd = _ln_stats(dc_ref[...])
        ln_g = vec_ref[pl.ds(SM_LN_G, 1), :]
        ln = xhat * ln_g + vec_ref[pl.ds(SM_LN_B, 1), :]
        sl = _sigmoid(ln)
        dln = ds * (sl * (1.0 + ln * (1.0 - sl)))
        acc(SM_LN_G, dln * xhat)
        acc(SM_LN_B, dln)
        dxh = dln * ln_g
        ddc = rstd * (dxh - jnp.mean(dxh, axis=-1, keepdims=True)
                      - xhat * jnp.mean(dxh * xhat, axis=-1, keepdims=True))
        acc(SM_B_BCONV, ddc)
        buf[pl.ds(0, tm), :] = ddc
        _shifted_copies(buf, shf, tm)
        av = a_ref[...].astype(F32)
        sg = _sigmoid(g_ref[...].astype(F32))
        glu_s[...] = av * sg

        _long_conv(buf, shf, wb, dglu_s, tm, [(KB - 1 - j, j) for j in range(KB)], None)
        _long_conv_grad_taps(buf, shf, glu_s, wacc, tm)
        buf[pl.ds(tm, HALO_B), :] = buf[pl.ds(0, HALO_B), :]
        dglu = dglu_s[...]
        da = dglu * sg
        dg = dglu * av * sg * (1.0 - sg)
        sb_ref[pl.ds(0, 1), pl.ds(0, D)] += jnp.sum(da, axis=0, keepdims=True)
        sb_ref[pl.ds(0, 1), pl.ds(D, D)] += jnp.sum(dg, axis=0, keepdims=True)
        dab, dgb = da.astype(BF16), dg.astype(BF16)
        dn = _dot(dab, w1[pl.ds(0, D), :]) + _dot(dgb, w1[pl.ds(D, D), :])
        hv = h_ref[...]
        r = _rms_stat(hv)
        gamma = vec_ref[pl.ds(SM_B_NORM, 1), :]
        nb = (hv * r * gamma).astype(BF16)
        acc1[pl.ds(0, D), :] += _dot_tn(dab, nb)
        acc1[pl.ds(D, D), :] += _dot_tn(dgb, nb)
        dx, dgam = _rms_bwd(dn, hv, r, gamma)
        o_ref[...] = dhv + dx
        st_ref[pl.ds(SM_B_NORM, 1), :] += dgam

        @pl.when(pl.program_id(0) == nt - 1)
        def _():
            st_ref[pl.ds(SM_B_CONV, KB), :] = jnp.sum(wacc[...], axis=1)
            blocks = [(acc1, 0, g1_ref, 0), (acc1, D, g1_ref, D), (acc2, 0, g2_ref, 0)]
            for k, (src, a0, dst, d0) in enumerate(blocks):
                stage[...] = src[pl.ds(a0, D), :].astype(BF16)
                cp = pltpu.make_async_copy(stage, dst.at[pl.ds(d0, D), :], out_sems.at[k])
                cp.start()
                cp.wait()

    rs = functools.partial(_row_spec, rev_nt=nt)
    return _hosted_call(
        body, "mixb_bwd", nt,
        in_specs=[rs(tm, D) for _ in range(6)] + [_full_spec((SM_F32_ROWS, D)), _ANY],
        out_specs=[rs(tm, D), _full_spec((SM_F32_ROWS, D)), _full_spec((8, 2 * D)), _ANY, _ANY],
        out_shape=(_sds((T, D), F32), _sds((SM_F32_ROWS, D), F32), _sds((8, 2 * D), F32),
                   _sds((2 * D, D), BF16), _sds((D, D), BF16)),
        scratch=[pltpu.VMEM((2 * D, D), BF16), pltpu.VMEM((D, D), BF16),
                 pltpu.VMEM((tm + HALO_B, D), F32), pltpu.VMEM((7, tm + HALO_B - 8, D), F32),
                 pltpu.VMEM((tm, D), F32), pltpu.VMEM((tm, D), F32), pltpu.VMEM((KB, 8, D), F32),
                 pltpu.VMEM((KB + 1, 8, D), F32), pltpu.VMEM((2 * D, D), F32), pltpu.VMEM((D, D), F32),
                 pltpu.VMEM((D, D), BF16), pltpu.SemaphoreType.DMA((2 * NDEV,)), pltpu.SemaphoreType.DMA((3,))],
        args=(dh, h, a, g, dc, s, vecs, wg), comm=comm)


def _mixa_bwd(dh, x, b, c, v, cc, y, gam, cw, wg, tm, comm):
    T = x.shape[0]
    nt = T // tm

    def body(dh_ref, x_ref, b_ref, c_ref, v_ref, cc_ref, y_ref, gam_ref, cw_ref, wg_ref,
             o_ref, st_ref, gin_ref, gout_ref, win, wout, buf, acc_in, acc_out, stage, sems, out_sems):
        first = pl.program_id(0) == 0
        load_w = _weight_loads(wg_ref, [(wout, "a_out"), (win, "a_in")], sems)

        @pl.when(first)
        def _():
            load_w()
            buf[pl.ds(tm, HALO_A), :] = jnp.zeros((HALO_A, D), F32)
            st_ref[...] = jnp.zeros((8, D), F32)
            acc_in[...] = jnp.zeros((3 * D, D), F32)
            acc_out[...] = jnp.zeros((D, D), F32)

        dhv = dh_ref[...]
        dhb = dhv.astype(BF16)
        dy = _dot_nt(dhb, wout[...])
        cval = c_ref[...].astype(F32)
        vval = v_ref[...].astype(F32)
        d_b = (dy * cc_ref[...].astype(F32)).astype(BF16)
        buf[pl.ds(0, tm), :] = dy * b_ref[...].astype(F32)
        cv = cval * vval
        dcv = jnp.zeros((tm, D), F32)
        for j in range(KA):
            sh = buf[pl.ds(j, tm), :]
            k = KA - 1 - j
            dcv = dcv + cw_ref[pl.ds(k, 1), :] * sh
            st_ref[pl.ds(1 + k, 1), :] += jnp.sum(cv * sh, axis=0, keepdims=True)
        buf[pl.ds(tm, HALO_A), :] = buf[pl.ds(0, HALO_A), :]
        d_c = (dcv * vval).astype(BF16)
        d_v = (dcv * cval).astype(BF16)
        xv = x_ref[...]
        r = _rms_stat(xv)
        gamma = gam_ref[...]
        nb = (xv * r * gamma).astype(BF16)
        for j, piece in enumerate((d_b, d_c, d_v)):
            acc_in[pl.ds(j * D, D), :] += _dot_tn(piece, nb)
        acc_out[...] += _dot_tn(y_ref[...], dhb)
        dn = _dot(d_b, win[pl.ds(0, D), :]) + _dot(d_c, win[pl.ds(D, D), :]) + _dot(d_v, win[pl.ds(2 * D, D), :])
        dx, dgam = _rms_bwd(dn, xv, r, gamma)
        o_ref[...] = dhv + dx
        st_ref[pl.ds(0, 1), :] += dgam

        @pl.when(pl.program_id(0) == nt - 1)
        def _():
            blocks = [(acc_in, j * D, gin_ref, j * D) for j in range(3)] + [(acc_out, 0, gout_ref, 0)]
            for k, (acc, a0, dst, d0) in enumerate(blocks):
                stage[...] = acc[pl.ds(a0, D), :].astype(BF16)
                cp = pltpu.make_async_copy(stage, dst.at[pl.ds(d0, D), :], out_sems.at[k])
                cp.start()
                cp.wait()

    rs = functools.partial(_row_spec, rev_nt=nt)
    return _hosted_call(
        body, "mixa_bwd", nt,
        in_specs=[rs(tm, D) for _ in range(7)] + [_full_spec((1, D)), _full_spec((KA, D)), _ANY],
        out_specs=[rs(tm, D), _full_spec((8, D)), _ANY, _ANY],
        out_shape=(_sds((T, D), F32), _sds((8, D), F32), _sds((3 * D, D), BF16), _sds((D, D), BF16)),
        scratch=[pltpu.VMEM((3 * D, D), BF16), pltpu.VMEM((D, D), BF16), pltpu.VMEM((tm + HALO_A, D), F32),
                 pltpu.VMEM((3 * D, D), F32), pltpu.VMEM((D, D), F32), pltpu.VMEM((D, D), BF16),
                 pltpu.SemaphoreType.DMA((2 * NDEV,)), pltpu.SemaphoreType.DMA((4,))],
        args=(dh, x, b, c, v, cc, y, gam, cw, wg), comm=comm)


def _sum_small(sm, rp):
    def body(sm_ref, rp_ref, osm, orp, oloss):
        a = sm_ref[0]
        b = rp_ref[0]
        for s in range(1, NDEV):
            a = a + sm_ref[s]
            b = b + rp_ref[s]
        osm[...] = a
        orp[...] = b
        oloss[...] = jnp.zeros((8, 128), F32) + jnp.sum(b[4:5, :], axis=-1, keepdims=True)

    return _pcall(
        body, name="sum_small_grads",
        out_shape=(_sds((SM_F32_ROWS, 128), F32), _sds((8, D), F32), _sds((8, 128), F32)),
        in_specs=[pl.BlockSpec(memory_space=pltpu.VMEM)] * 2,
        out_specs=tuple(pl.BlockSpec(memory_space=pltpu.VMEM) for _ in range(3)),
        compiler_params=_cparams(),
    )(sm, rp)


def _adam_math(w, g, m, v):
    mn = B1 * m + (1.0 - B1) * g
    vn = B2 * v + (1.0 - B2) * (g * g)
    m_hat = mn / (1.0 - B1 ** STEP)
    v_hat = vn / (1.0 - B2 ** STEP)
    return -LR * (m_hat / (jnp.sqrt(v_hat) + ADAM_EPS) + WD * w), mn, vn


def _finish_weight(land, keys, transposed, w, m, v, name, row0=0, after=None):
    layers, rows, cols = w.shape
    n = N_ROWS[keys[0]]
    n_pad = -(-n // 128) * 128 if transposed else n

    def body(w_ref, m_ref, v_ref, land_ref, *rest):
        og, od, om, ov, buf, sem = rest[-6:]
        off = LD_OFF[keys[0]] - row0
        if layers == 2:
            off = jnp.where(pl.program_id(0) == 0, off, LD_OFF[keys[1]] - row0)
        cp = pltpu.make_async_copy(land_ref.at[:, pl.ds(pl.multiple_of(off, 32), n), :], buf, sem)
        cp.start()
        cp.wait()
        g = buf[0].astype(F32)
        for s in range(1, NCHIP):
            g = g + buf[s].astype(F32)
        if transposed:
            if n_pad != n:
                g = jnp.concatenate([g, jnp.zeros((n_pad - n, D), F32)], axis=0)
            g = g.T[:, :n]
        d, mn, vn = _adam_math(w_ref[...], g, m_ref[...], v_ref[...])
        og[...] = g
        od[...] = d
        om[...] = mn
        ov[...] = vn

    spec = pl.BlockSpec((None, rows, cols), lambda l: (l, 0, 0))
    shp = _sds(w.shape, F32)
    return _hosted_call(
        body, "finish_" + name, layers,
        in_specs=[spec, spec, spec, _ANY] + ([] if after is None else [_ANY]), out_specs=[spec] * 4,
        out_shape=(shp,) * 4, scratch=[pltpu.VMEM((NCHIP, n, D), BF16), pltpu.SemaphoreType.DMA],
        args=(w, m, v, land) + (() if after is None else (after,)), comm=None)


SMALL_PARAMS = (("sm", SM_B_NORM, 1), ("sm", SM_B_PW1, 2), ("sm", SM_B_CONV, KB), ("sm", SM_B_BCONV, 1),
                ("sm", SM_LN_G, 1), ("sm", SM_LN_B, 1), ("sm", SM_B_PW2, 1), ("sm", SM_A_CONV, KA),
                ("rp", 0, 1), ("rp", 1, 2), ("rp", 3, 1))


def _adamw_small(g_small, g_repl, triples):
    n = len(SMALL_PARAMS)

    def body(*refs):
        gs_ref, gr_ref = refs[0], refs[1]
        ins = refs[2:2 + 3 * n]
        outs = refs[2 + 3 * n:]
        for i, (src, r0, nr) in enumerate(SMALL_PARAMS):
            w_ref, m_ref, v_ref = ins[3 * i:3 * i + 3]
            g = (gs_ref if src == "sm" else gr_ref)[pl.ds(r0, nr), :]
            if i == 1:
                g = jnp.concatenate([g[0:1], g[1:2]], axis=1)
            lead = (0,) if len(w_ref.shape) == 3 else ()
            idx = lead + (slice(None), slice(None))
            vals = (g,) + _adam_math(w_ref[idx], g, m_ref[idx], v_ref[idx])
            for o_ref, val in zip(outs[4 * i:4 * i + 4], vals):
                o_ref[idx] = val

    flat = [a for t in triples for a in t]
    out_shape = tuple(_sds(t[0].shape, F32) for t in triples for _ in range(4))
    vm = pl.BlockSpec(memory_space=pltpu.VMEM)
    res = _pcall(
        body, name="adamw_small", out_shape=out_shape,
        in_specs=[vm] * (2 + len(flat)), out_specs=tuple(vm for _ in out_shape),
        compiler_params=_cparams(),
    )(g_small, g_repl, *flat)
    return [tuple(res[4 * i:4 * i + 4]) for i in range(n)]


def kernel(x, a_norm, a_w_in, a_conv, a_w_out, b_norm, b_w_pw1, b_b_pw1, b_conv, b_b_conv, b_ln_g, b_ln_b, b_w_pw2, b_b_pw2, ffn_norm, ffn_w_gate, ffn_w_up, ffn_w_down, final_norm, loss_target, m_a_norm, m_a_w_in, m_a_conv, m_a_w_out, m_b_norm, m_b_w_pw1, m_b_b_pw1, m_b_conv, m_b_b_conv, m_b_ln_g, m_b_ln_b, m_b_w_pw2, m_b_b_pw2, m_ffn_norm, m_ffn_w_gate, m_ffn_w_up, m_ffn_w_down, m_final_norm, v_a_norm, v_a_w_in, v_a_conv, v_a_w_out, v_b_norm, v_b_w_pw1, v_b_b_pw1, v_b_conv, v_b_b_conv, v_b_ln_g, v_b_ln_b, v_b_w_pw2, v_b_b_pw2, v_ffn_norm, v_ffn_w_gate, v_ffn_w_up, v_ffn_w_down, v_final_norm):
    T = x.shape[1]
    tm = min(TM, T)
    tma = min(TM_A, T)
    tw = min(TM_DW, T)
    xs = x.reshape(T, D)
    tgt = loss_target.reshape(T, D)

    def rows_first(a):
        return jnp.swapaxes(a, 1, 2)

    shard, small = _pack_shard(
        (a_w_in, a_w_out, b_w_pw1, b_w_pw2, rows_first(ffn_w_gate), rows_first(ffn_w_up), ffn_w_down),
        (b_norm, b_b_pw1, b_conv, b_b_conv, b_ln_g, b_ln_b, b_b_pw2, a_conv))
    wg, sm_all = _all_gather_first(shard, small)
    vecs = sm_all.transpose(1, 0, 2).reshape(SM_F32_ROWS, D)
    bias1 = sm_all[:, SM_B_PW1:SM_B_PW1 + 2, :].reshape(1, 2 * D)
    cw_a = vecs[SM_A_CONV:SM_A_CONV + KA]
    fn0, fn1 = ffn_norm[0:1], ffn_norm[1:2]
    fin = final_norm.reshape(1, D)

    h1, bq, cq, vq, ccq, yq, wg = _mixa_fwd(xs, a_norm, cw_a, wg, tma, _ag_comm(shard, wg, ("g0", "u0", "d0")))
    h2, n1, g0, u0, gu0, wg = _ffn_fwd(h1, fn0, wg, 0, tma, _ag_comm(shard, wg, ("b_pw1", "b_pw2"), (T // tma) // 2))
    h3, aq, gq, dcq, sq, wg = _mixb_fwd(h2, vecs, bias1, wg, tm, _ag_comm(shard, wg, ("g1", "u1", "d1"), 1))
    dh4, n3, g1, u1, gu1, st_fin = _ffn_fwd(h3, fn1, wg, 1, tma, None, head=(tgt, fin))

    def by_dest(gw, key):
        return gw.reshape(NDEV, N_ROWS[key], D)

    gw_d1 = by_dest(_grad_w(gu1, dh4, FH, "grad_down1", tw), "d1")
    dh3, dg1, du1, st_f1 = _ffn_bwd_dx(dh4, h3, g1, u1, fn1, wg, 1, tma, None)
    gw_g1 = by_dest(_grad_w(dg1, n3, FH, "grad_gate1", tw), "g1")
    gw_u1 = by_dest(_grad_w(du1, n3, FH, "grad_up1", tw), "u1")
    keys = ("g1", "u1", "d1")
    dh2, st_b, st_b1, gw_pw1, gw_pw2, land = _mixb_bwd(
        dh3, h2, aq, gq, dcq, sq, vecs, wg, tm, _rs_comm(_pair_reduce([gw_g1, gw_u1, gw_d1], keys), keys, None))
    gw_pw1, gw_pw2 = by_dest(gw_pw1, "b_pw1"), by_dest(gw_pw2, "b_pw2")
    gw_d0 = by_dest(_grad_w(gu0, dh2, FH, "grad_down0", tw), "d0")
    keys = ("b_pw1", "b_pw2", "d0")
    dh1, dg0, du0, st_f0, land = _ffn_bwd_dx(
        dh2, h1, g0, u0, fn0, wg, 0, tma, _rs_comm(_pair_reduce([gw_pw1, gw_pw2, gw_d0], keys), keys, land))
    gw_g0 = by_dest(_grad_w(dg0, n1, FH, "grad_gate0", tw), "g0")
    gw_u0 = by_dest(_grad_w(du0, n1, FH, "grad_up0", tw), "u0")
    keys = ("g0", "u0")
    dx, st_a, gw_in, gw_out, land = _mixa_bwd(
        dh1, xs, bq, cq, vq, ccq, yq, a_norm, cw_a, wg, tm,
        _rs_comm(_pair_reduce([gw_g0, gw_u0], keys), keys, land))
    gw_in, gw_out = by_dest(gw_in, "a_in"), by_dest(gw_out, "a_out")

    st_small = st_b.at[SM_A_CONV:SM_A_CONV + KA].set(st_a[1:1 + KA])
    sm_dest = st_small.reshape(SM_F32_ROWS, NDEV, 128).transpose(1, 0, 2)
    sm_dest = sm_dest.at[:, SM_B_PW1:SM_B_PW1 + 2, :].set(st_b1[0].reshape(NDEV, 2, 128))
    repl = jnp.concatenate([st_a[0:1], st_f0[0:1], st_f1[0:1], st_fin[0:1], st_fin[1:2],
                            jnp.zeros((3, D), F32)], axis=0)[None]
    last = ("a_out", "a_in")
    p_last, sm_land, rp_land = _pair_reduce([gw_out, gw_in], last, side=(sm_dest, repl))
    sems, p_last, land_last, token = _chip_start(p_last)

    big = [("ffn_w_gate", ("g0", "g1"), False,
            rows_first(ffn_w_gate), rows_first(m_ffn_w_gate), rows_first(v_ffn_w_gate)),
           ("ffn_w_up", ("u0", "u1"), False, rows_first(ffn_w_up), rows_first(m_ffn_w_up), rows_first(v_ffn_w_up)),
           ("ffn_w_down", ("d0", "d1"), False, ffn_w_down, m_ffn_w_down, v_ffn_w_down),
           ("a_w_out", ("a_out",), False, a_w_out, m_a_w_out, v_a_w_out),
           ("b_w_pw1", ("b_pw1",), True, b_w_pw1, m_b_w_pw1, v_b_w_pw1),
           ("b_w_pw2", ("b_pw2",), False, b_w_pw2, m_b_w_pw2, v_b_w_pw2),
           ("a_w_in", ("a_in",), True, a_w_in, m_a_w_in, v_a_w_in)]
    res, after = {}, token
    for name, keys, transposed, w, m, v in big:
        if keys[0] not in last:
            res[name] = _finish_weight(land, keys, transposed, w, m, v, name, after=after)
            after = res[name][1]
    _, land_last = _chip_wait(sems, p_last, land_last, after)
    land_last = _own_slot(land_last, p_last)
    for name, keys, transposed, w, m, v in big:
        if keys[0] in last:
            res[name] = _finish_weight(land_last, keys, transposed, w, m, v, name, row0=LD_OFF[last[0]])
    for name in ("ffn_w_gate", "ffn_w_up"):
        res[name] = [rows_first(a) for a in res[name]]

    g_small, g_repl, loss8 = _sum_small(sm_land, rp_land)
    loss = loss8[0, 0]

    small_names = ["b_norm", "b_b_pw1", "b_conv", "b_b_conv", "b_ln_g", "b_ln_b", "b_b_pw2", "a_conv",
                   "a_norm", "ffn_norm", "final_norm"]
    triples = [(b_norm, m_b_norm, v_b_norm), (b_b_pw1, m_b_b_pw1, v_b_b_pw1), (b_conv, m_b_conv, v_b_conv),
               (b_b_conv, m_b_b_conv, v_b_b_conv), (b_ln_g, m_b_ln_g, v_b_ln_g), (b_ln_b, m_b_ln_b, v_b_ln_b),
               (b_b_pw2, m_b_b_pw2, v_b_b_pw2), (a_conv, m_a_conv, v_a_conv), (a_norm, m_a_norm, v_a_norm),
               (ffn_norm, m_ffn_norm, v_ffn_norm),
               (fin, m_final_norm.reshape(1, D), v_final_norm.reshape(1, D))]
    for name, quad in zip(small_names, _adamw_small(g_small, g_repl, triples)):
        res[name] = quad
    res["final_norm"] = tuple(a.reshape(D) for a in res["final_norm"])

    order = ["a_norm", "a_w_in", "a_conv", "a_w_out", "b_norm", "b_w_pw1", "b_b_pw1", "b_conv", "b_b_conv",
             "b_ln_g", "b_ln_b", "b_w_pw2", "b_b_pw2", "ffn_norm", "ffn_w_gate", "ffn_w_up", "ffn_w_down",
             "final_norm"]
    out = [loss, dx.reshape(1, T, D)]
    for j in range(4):
        out += [res[k][j] for k in order]
    return tuple(out)
```

```python
import functools

import jax
import jax.numpy as jnp
from jax import lax
from jax.experimental import pallas as pl
from jax.experimental.pallas import tpu as pltpu

F32 = jnp.float32
BF16 = jnp.bfloat16
MESH = pl.DeviceIdType.MESH

D = 1024
FF = 2816
FH = FF // 2
NDEV = 8
KA = 3
KB = 31
HALO_A = 8
HALO_B = 32
CONV_ROWS = 64
RMS_EPS = 1e-6
LN_EPS = 1e-5
LR, B1, B2, ADAM_EPS, WD, STEP = 0.001, 0.9, 0.999, 1e-08, 0.01, 10

TM = 256
TM_A = 512
TM_DW = 2048
VMEM_LIMIT = 62 * 1024 * 1024

N_ROWS = {"a_in": 384, "a_out": 128, "b_pw1": 256, "b_pw2": 128, "g0": 352, "u0": 352, "d0": 352,
          "g1": 352, "u1": 352, "d1": 352}
NCHIP = 4


def _offsets(order):
    off, o = {}, 0
    for k in order:
        off[k] = o
        o += N_ROWS[k]
    return off, o


PK_ORDER = ("a_in", "a_out", "g0", "u0", "d0", "b_pw1", "b_pw2", "g1", "u1", "d1")
PK_OFF, PK_ROWS = _offsets(PK_ORDER)
LD_ORDER = ("g1", "u1", "d1", "b_pw1", "b_pw2", "d0", "g0", "u0", "a_out", "a_in")
LD_OFF, N_WROWS = _offsets(LD_ORDER)


def _span(off, keys):
    return off[keys[0]], sum(N_ROWS[k] for k in keys)


SM_B_NORM, SM_B_PW1, SM_B_CONV, SM_B_BCONV, SM_LN_G, SM_LN_B, SM_B_PW2, SM_A_CONV = 0, 1, 3, 34, 35, 36, 37, 38
SM_F32_ROWS = 64


def _pcall(body, **kw):
    return pl.pallas_call(body, **kw)


def _cparams(sem=None):
    return pltpu.CompilerParams(dimension_semantics=sem, vmem_limit_bytes=VMEM_LIMIT)


def _dot(a, b):
    return jnp.dot(a, b, preferred_element_type=F32)


def _dot_nt(a, b):
    return lax.dot_general(a, b, (((1,), (1,)), ((), ())), preferred_element_type=F32)


def _dot_tn(a, b):
    return lax.dot_general(a, b, (((0,), (0,)), ((), ())), preferred_element_type=F32)


def _sigmoid(v):
    return 1.0 / (1.0 + jnp.exp(-v))


def _rms_stat(x):
    return lax.rsqrt(jnp.mean(x * x, axis=-1, keepdims=True) + RMS_EPS)


def _rms_bwd(dn, x, r, gamma):
    dng = dn * gamma
    dx = r * dng - x * (r * r * r) * jnp.mean(dng * x, axis=-1, keepdims=True)
    return dx, jnp.sum(dn * x * r, axis=0, keepdims=True)


def _weight_loads(wg_ref, plan, sems):
    copies = []
    for j, (dst, key) in enumerate(plan):
        off, n = PK_OFF[key], N_ROWS[key]
        copies += [pltpu.make_async_copy(
            wg_ref.at[d, pl.ds(off, n), :], dst.at[pl.ds(d * n, n), :], sems.at[j * NDEV + d])
            for d in range(NDEV)]

    def load():
        for cp in copies:
            cp.start()
        for cp in copies:
            cp.wait()

    return load


_ANY = pl.BlockSpec(memory_space=pl.ANY)


def _row_spec(tm, width, rev_nt=None):
    if rev_nt is None:
        return pl.BlockSpec((tm, width), lambda i: (i, 0))
    return pl.BlockSpec((tm, width), lambda i: (rev_nt - 1 - i, 0))


def _full_spec(shape):
    return pl.BlockSpec(shape, lambda *_: (0,) * len(shape))


def _sds(shape, dtype):
    return jax.ShapeDtypeStruct(shape, dtype)


def _mesh_pos():
    return lax.axis_index("x"), lax.axis_index("y"), lax.axis_index("c")


def _lin(p):
    return 4 * p[0] + 2 * p[1] + p[2]


def _ag_exchange(src, slot, send_sems, recv_sems, local_sem):
    x, y, c = _mesh_pos()
    me, sibling = (x, y, c), (x, y, 1 - c)
    chips = [(1 - x, y), (x, 1 - y), (1 - x, 1 - y)]

    def copy(k, block, to, own=False):
        return pltpu.make_async_remote_copy(
            src_ref=src if own else slot(block), dst_ref=slot(block),
            send_sem=send_sems.at[k], recv_sem=recv_sems.at[k], device_id=to, device_id_type=MESH)

    mine = pltpu.make_async_copy(src, slot(me), local_sem)
    first = [copy(0, me, sibling, own=True)]
    first += [copy(1 + j, me, (*chip, c), own=True) for j, chip in enumerate(chips)]
    passed = [copy(4 + j, (*chip, c), sibling) for j, chip in enumerate(chips)]

    def start():
        mine.start()
        for cp in first:
            cp.start()

    def forward():
        for j, chip in enumerate(chips):
            copy(1 + j, (*chip, c), me).wait_recv()
            passed[j].start()

    def finish():
        copy(0, sibling, me).wait_recv()
        for j, chip in enumerate(chips):
            copy(4 + j, (*chip, 1 - c), me).wait_recv()
        for cp in first + passed:
            cp.wait_send()
        mine.wait()

    return start, forward, finish


def _chip_exchange(p_ref, land_ref, send_sems, recv_sems, keys):
    x, y, c = _mesh_pos()
    l0, rows = _span(LD_OFF, keys)
    dst = land_ref.at[2 * x + y, pl.ds(l0, rows), :]
    peers = [((x + (j >> 1)) % 2, (y + (j & 1)) % 2) for j in range(NCHIP)]

    def copy(j):
        tx, ty = peers[j]
        src = p_ref.at[2 * tx + ty]
        if j == 0:
            return pltpu.make_async_copy(src, dst, send_sems.at[0])
        return pltpu.make_async_remote_copy(
            src_ref=src, dst_ref=dst, send_sem=send_sems.at[j], recv_sem=recv_sems.at[j],
            device_id=(tx, ty, c), device_id_type=MESH)

    def start():
        for j in range(NCHIP):
            copy(j).start()

    def finish():
        for j in range(NCHIP):
            copy(j).wait()

    return start, finish


def _all_to_all_f32(src_for, dst_ref, send_sems, recv_sems):
    x, y, c = _mesh_pos()
    dst = dst_ref.at[_lin((x, y, c))]
    peers = [((x + ((j >> 2) & 1)) % 2, (y + ((j >> 1) & 1)) % 2, (c + (j & 1)) % 2) for j in range(NDEV)]

    def copy(j):
        if j == 0:
            return pltpu.make_async_copy(src_for(_lin(peers[0])), dst, send_sems.at[0])
        return pltpu.make_async_remote_copy(
            src_ref=src_for(_lin(peers[j])), dst_ref=dst, send_sem=send_sems.at[j], recv_sem=recv_sems.at[j],
            device_id=peers[j], device_id_type=MESH)

    def start():
        for j in range(NDEV):
            copy(j).start()

    def finish():
        for j in range(NDEV):
            copy(j).wait()

    return start, finish


class _Comm:
    def __init__(self, ins, alias_in, out_shape, scratch, make, gives_wg, middle_step=0):
        self.ins, self.alias_in, self.out_shape = ins, alias_in, out_shape
        self.scratch, self.make, self.gives_wg, self.middle_step = scratch, make, gives_wg, middle_step


def _ag_comm(shard, wg, keys, forward_step=0):
    def make(c_ins, c_out, sc):
        return _weights_exchange(c_ins[0], c_out, sc[0], sc[1], sc[2], keys)

    return _Comm([shard, wg], 1, _sds(wg.shape, BF16),
                 [pltpu.SemaphoreType.DMA((7,)), pltpu.SemaphoreType.DMA((7,)), pltpu.SemaphoreType.DMA],
                 make, True, forward_step)


def _weights_exchange(shard_ref, wg_ref, send_sems, recv_sems, local_sem, keys):
    r0, nr = _span(PK_OFF, keys)
    return _ag_exchange(shard_ref.at[pl.ds(r0, nr), :], lambda p: wg_ref.at[_lin(p), pl.ds(r0, nr), :],
                        send_sems, recv_sems, local_sem)


def _rs_comm(psum, keys, land):
    def make(c_ins, c_out, sc):
        start, finish = _chip_exchange(c_ins[0], c_out, sc[0], sc[1], keys)
        return start, None, finish

    ins = [psum] + ([] if land is None else [land])
    return _Comm(ins, None if land is None else 1, _sds((NCHIP, N_WROWS, D), BF16),
                 [pltpu.SemaphoreType.DMA((NCHIP,)), pltpu.SemaphoreType.DMA((NCHIP,))], make, False)


def _hosted_call(body, name, nt, in_specs, out_specs, out_shape, scratch, args, comm):
    if comm is None:
        return _pcall(body, name=name, grid=(nt,), in_specs=in_specs, out_specs=tuple(out_specs),
                      out_shape=tuple(out_shape), scratch_shapes=scratch,
                      compiler_params=_cparams(("arbitrary",)))(*args)
    n_in, n_out, n_sc, n_cin = len(in_specs), len(out_specs), len(scratch), len(comm.ins)

    def wrapped(*refs):
        ins = refs[:n_in]
        c_ins = refs[n_in:n_in + n_cin]
        outs = refs[n_in + n_cin:n_in + n_cin + n_out]
        c_out = refs[n_in + n_cin + n_out]
        sc = refs[n_in + n_cin + n_out + 1:n_in + n_cin + n_out + 1 + n_sc]
        c_sc = refs[n_in + n_cin + n_out + 1 + n_sc:]
        start, middle, finish = comm.make(c_ins, c_out, c_sc)
        pl.when(pl.program_id(0) == 0)(start)
        if comm.gives_wg:
            body(*ins, c_out, *outs, *sc)
        else:
            body(*ins, *outs, *sc)
        if middle is not None:
            pl.when(pl.program_id(0) == max(nt - 1 - comm.middle_step, 0))(middle)
        pl.when(pl.program_id(0) == nt - 1)(finish)

    aliases = {} if comm.alias_in is None else {n_in + comm.alias_in: n_out}
    res = _pcall(wrapped, name=name, grid=(nt,),
                 in_specs=list(in_specs) + [_ANY] * n_cin, out_specs=tuple(out_specs) + (_ANY,),
                 out_shape=tuple(out_shape) + (comm.out_shape,),
                 scratch_shapes=list(scratch) + list(comm.scratch),
                 input_output_aliases=aliases,
                 compiler_params=_cparams(("arbitrary",)))(*args, *comm.ins)
    return res


def _pack_shard(weights, smalls):
    plan = (("a_in", 0, 0, True), ("a_out", 1, 0, False), ("b_pw1", 2, 0, True), ("b_pw2", 3, 0, False),
            ("g0", 4, 0, False), ("u0", 5, 0, False), ("d0", 6, 0, False),
            ("g1", 4, 1, False), ("u1", 5, 1, False), ("d1", 6, 1, False))
    n_pad = 384

    def body(*refs):
        out, sm, pad = refs[-3:]
        for key, idx, layer, transposed in plan:
            n = N_ROWS[key]
            val = refs[idx][layer]
            if transposed:
                if n % 128:
                    pad[:, pl.ds(0, n)] = val
                    pad[:, pl.ds(n, n_pad - n)] = jnp.zeros((D, n_pad - n), F32)
                    val = pad[...]
                val = val.T[:n]
            out[pl.ds(PK_OFF[key], n), :] = val.astype(BF16)
        b_norm, b_b_pw1, b_conv, b_b_conv, b_ln_g, b_ln_b, b_b_pw2, a_conv = refs[len(weights):-3]
        sm[...] = jnp.zeros((SM_F32_ROWS, 128), F32)
        sm[pl.ds(SM_B_PW1, 1), :] = b_b_pw1[:, pl.ds(0, 128)]
        sm[pl.ds(SM_B_PW1 + 1, 1), :] = b_b_pw1[:, pl.ds(128, 128)]
        sm[pl.ds(SM_B_CONV, KB), :] = b_conv[0]
        sm[pl.ds(SM_A_CONV, KA), :] = a_conv[0]
        for row, ref in ((SM_B_NORM, b_norm), (SM_B_BCONV, b_b_conv), (SM_LN_G, b_ln_g), (SM_LN_B, b_ln_b),
                         (SM_B_PW2, b_b_pw2)):
            sm[pl.ds(row, 1), :] = ref[...]

    vm = pl.BlockSpec(memory_space=pltpu.VMEM)
    return _pcall(
        body, name="pack_shard",
        out_shape=(_sds((PK_ROWS, D), BF16), _sds((SM_F32_ROWS, 128), F32)),
        in_specs=[vm] * (len(weights) + len(smalls)), out_specs=(vm, vm),
        scratch_shapes=[pltpu.VMEM((D, n_pad), F32)],
        compiler_params=_cparams(),
    )(*weights, *smalls)


def _all_gather_first(shard, small):
    def body(x_ref, s_ref, wg_ref, sg_ref, send_w, recv_w, local_w, send_s, recv_s, local_s):
        start_w, forward_w, finish_w = _weights_exchange(x_ref, wg_ref, send_w, recv_w, local_w, ("a_in", "a_out"))
        start_s, forward_s, finish_s = _ag_exchange(s_ref, lambda p: sg_ref.at[_lin(p)], send_s, recv_s, local_s)
        start_s()
        start_w()
        forward_s()
        forward_w()
        finish_s()
        finish_w()

    sems = [pltpu.SemaphoreType.DMA((7,)), pltpu.SemaphoreType.DMA((7,)), pltpu.SemaphoreType.DMA]
    return _pcall(
        body, name="all_gather_first",
        out_shape=(_sds((NDEV, PK_ROWS, D), BF16), _sds((NDEV, SM_F32_ROWS, 128), F32)),
        in_specs=[_ANY, _ANY], out_specs=(_ANY, _ANY), scratch_shapes=sems + sems,
    )(shard, small)


def _pair_reduce(grads, keys, side=None, land=None):
    l0, rows = _span(LD_OFF, keys)
    nk = len(keys)
    n_side = 0 if side is None else 2
    n_land = 0 if land is None else 1

    def body(c_ref, *refs):
        mine = refs[:nk]
        whole = refs[nk:2 * nk]
        n_in = 2 * nk + n_side + n_land
        o_ref = refs[n_in]
        got, send_sem, recv_sems = refs[n_in + 1 + n_side:n_in + 4 + n_side]
        t = pl.program_id(0)
        x, y, c = _mesh_pos()
        sibling = (x, y, 1 - c)
        if side is not None:
            sm_ref, rp_ref = refs[2 * nk:2 * nk + 2]
            sm_land, rp_land = refs[n_in + 1:n_in + 3]
            s2, r2, s3, r3 = refs[n_in + 6:n_in + 10]
            start_s, finish_s = _all_to_all_f32(lambda d: sm_ref.at[d], sm_land, s2, r2)
            start_r, finish_r = _all_to_all_f32(lambda d: rp_ref.at[0], rp_land, s3, r3)

        @pl.when(t == 0)
        def _():
            if side is not None:
                start_s()
                start_r()
            for ref, key in zip(whole, keys):
                for d in range(NCHIP):
                    pltpu.make_async_remote_copy(
                        src_ref=ref.at[2 * d + (1 - c)], dst_ref=got.at[d, pl.ds(LD_OFF[key] - l0, N_ROWS[key]), :],
                        send_sem=send_sem, recv_sem=recv_sems.at[d], device_id=sibling, device_id_type=MESH).start()

        pltpu.make_async_remote_copy(src_ref=got.at[t], dst_ref=got.at[t], send_sem=send_sem,
                                     recv_sem=recv_sems.at[t], device_id=sibling, device_id_type=MESH).wait_recv()
        sums = o_ref if land is None else refs[-3].at[t]
        for ref, key in zip(mine, keys):
            sl = pl.ds(LD_OFF[key] - l0, N_ROWS[key])
            sums[sl, :] = (ref[...].astype(F32) + got[t, sl, :].astype(F32)).astype(BF16)

        if land is not None:
            psum, chip_send, chip_recv = refs[-3:]
            my_chip = 2 * x + y
            dst = o_ref.at[my_chip, pl.ds(l0, rows), :]
            hop = jnp.bitwise_xor(t, my_chip)

            @pl.when(t == my_chip)
            def _():
                pltpu.make_async_copy(psum.at[t], dst, chip_send.at[0]).start()

            @pl.when(t != my_chip)
            def _():
                pltpu.make_async_remote_copy(
                    src_ref=psum.at[t], dst_ref=dst, send_sem=chip_send.at[hop], recv_sem=chip_recv.at[hop],
                    device_id=(t // 2, t % 2, c), device_id_type=MESH).start()

        @pl.when(t == NCHIP - 1)
        def _():
            pltpu.make_async_remote_copy(src_ref=got, dst_ref=got, send_sem=send_sem, recv_sem=recv_sems.at[0],
                                         device_id=sibling, device_id_type=MESH).wait_send()
            if land is not None:
                one = o_ref.at[0, pl.ds(l0, rows), :]
                pltpu.make_async_copy(one, one, chip_send.at[0]).wait()
                for j in range(1, NCHIP):
                    pltpu.make_async_remote_copy(src_ref=one, dst_ref=one, send_sem=chip_send.at[j],
                                                 recv_sem=chip_recv.at[j], device_id=sibling, device_id_type=MESH).wait()
            if side is not None:
                finish_s()
                finish_r()

    if land is None:
        out_specs = [pl.BlockSpec((None, rows, D), lambda t, c: (t, 0, 0))]
        out_shape = [_sds((NCHIP, rows, D), BF16)]
    else:
        out_specs, out_shape = [_ANY], [_sds((NCHIP, N_WROWS, D), BF16)]
    scratch = [pltpu.VMEM((NCHIP, rows, D), BF16), pltpu.SemaphoreType.DMA, pltpu.SemaphoreType.DMA((NCHIP,))]
    if side is not None:
        out_specs += [_ANY, _ANY]
        out_shape += [_sds((NDEV, SM_F32_ROWS, 128), F32), _sds((NDEV, 8, D), F32)]
        scratch += [pltpu.SemaphoreType.DMA((NDEV,))] * 4
    if land is not None:
        scratch += [pltpu.VMEM((NCHIP, rows, D), BF16), pltpu.SemaphoreType.DMA((NCHIP,)),
                    pltpu.SemaphoreType.DMA((NCHIP,))]
    grid_spec = pltpu.PrefetchScalarGridSpec(
        num_scalar_prefetch=1, grid=(NCHIP,),
        in_specs=[pl.BlockSpec((None, N_ROWS[k], D), lambda t, c: (2 * t + c[0], 0, 0)) for k in keys]
        + [_ANY] * (nk + n_side + n_land),
        out_specs=tuple(out_specs), scratch_shapes=scratch)
    core = lax.axis_index("c").astype(jnp.int32).reshape(1)
    aliases = {} if land is None else {1 + 2 * nk + n_side: 0}
    res = _pcall(
        body, name="pair_reduce_" + keys[0], grid_spec=grid_spec, out_shape=tuple(out_shape),
        input_output_aliases=aliases, compiler_params=_cparams(("arbitrary",)),
    )(core, *grads, *grads, *(side or ()), *(() if land is None else (land,)))
    return res[0] if len(res) == 1 else res


def _chip_start(psum, tag):
    hbm = pl.BlockSpec(memory_space=pltpu.HBM)
    sem = pl.BlockSpec(memory_space=pltpu.SEMAPHORE)

    def body(p_ref, land_ref, s1, s2, s3, r1, r2, r3, p_thru, land_thru, token):
        x, y, c = _mesh_pos()
        for j, (send_sem, recv_sem) in enumerate(((s1, r1), (s2, r2), (s3, r3)), start=1):
            tx, ty = (x + (j >> 1)) % 2, (y + (j & 1)) % 2
            pltpu.make_async_remote_copy(
                src_ref=p_ref.at[2 * tx + ty], dst_ref=land_ref.at[2 * x + y], send_sem=send_sem, recv_sem=recv_sem,
                device_id=(tx, ty, c), device_id_type=MESH).start()
        token[...] = jnp.zeros_like(token)

    dma = pltpu.SemaphoreType.DMA(())
    buf = pltpu.HBM(psum.shape, psum.dtype)
    res = _pcall(
        body, name="chip_exchange_start_" + tag,
        out_shape=(dma,) * 6 + (buf, buf, _sds((8, 128), F32)),
        in_specs=(hbm, hbm), out_specs=(sem,) * 6 + (hbm, hbm, pl.BlockSpec(memory_space=pltpu.VMEM)),
        input_output_aliases={0: 6, 1: 7},
        compiler_params=pltpu.CompilerParams(has_side_effects=pltpu.SideEffectType.DATAFLOW_SIDE_EFFECTING),
    )(pltpu.with_memory_space_constraint(psum, pltpu.HBM),
      pltpu.with_memory_space_constraint(lax.empty(psum.shape, psum.dtype), pltpu.HBM))
    return res[:6], res[6], res[7], res[8]


def _chip_wait(sems, p_thru, land_thru, after, tag):
    hbm = pl.BlockSpec(memory_space=pltpu.HBM)
    sem = pl.BlockSpec(memory_space=pltpu.SEMAPHORE)

    def body(p_ref, land_ref, s1, s2, s3, r1, r2, r3, after_ref, p_dead, got_ref):
        x, y, c = _mesh_pos()
        for send_sem, recv_sem in ((s1, r1), (s2, r2), (s3, r3)):
            copy = pltpu.make_async_remote_copy(
                src_ref=p_ref.at[0], dst_ref=land_ref.at[0], send_sem=send_sem, recv_sem=recv_sem,
                device_id=(x, y, 1 - c), device_id_type=MESH)
            copy.wait_send()
            copy.wait_recv()

    buf = pltpu.HBM(p_thru.shape, p_thru.dtype)
    return _pcall(
        body, name="chip_exchange_wait_" + tag, out_shape=(buf, buf),
        in_specs=(hbm, hbm) + (sem,) * 6 + (_ANY,), out_specs=(hbm, hbm),
        input_output_aliases={0: 0, 1: 1},
        compiler_params=pltpu.CompilerParams(has_side_effects=pltpu.SideEffectType.DATAFLOW_SIDE_EFFECTING),
    )(p_thru, land_thru, *sems, after)


def _own_slot(land, psum, tag):
    def body(land_in, p_ref, land_ref, sem):
        x, y, _ = _mesh_pos()
        cp = pltpu.make_async_copy(p_ref.at[2 * x + y], land_ref.at[2 * x + y], sem)
        cp.start()
        cp.wait()

    return _pcall(
        body, name="own_slot_" + tag, out_shape=_sds(land.shape, land.dtype),
        in_specs=[_ANY, _ANY], out_specs=_ANY, input_output_aliases={0: 0},
        scratch_shapes=[pltpu.SemaphoreType.DMA],
    )(land, psum)


def _mixa_fwd(x, gam, cw, wg, tm, comm):
    T = x.shape[0]

    def body(x_ref, gam_ref, cw_ref, wg_ref, h_ref, b_ref, c_ref, v_ref, cc_ref, y_ref,
             win, wout, buf, sems):
        first = pl.program_id(0) == 0
        load_w = _weight_loads(wg_ref, [(win, "a_in"), (wout, "a_out")], sems)

        @pl.when(first)
        def _():
            load_w()
            buf[pl.ds(0, HALO_A), :] = jnp.zeros((HALO_A, D), F32)

        xv = x_ref[...]
        nb = (xv * _rms_stat(xv) * gam_ref[...]).astype(BF16)
        bv = _dot_nt(nb, win[pl.ds(0, D), :])
        cval = _dot_nt(nb, win[pl.ds(D, D), :])
        vval = _dot_nt(nb, win[pl.ds(2 * D, D), :])
        cv = cval * vval
        buf[pl.ds(HALO_A, tm), :] = cv
        cc = cw_ref[pl.ds(KA - 1, 1), :] * cv
        for k in range(KA - 1):
            cc = cc + cw_ref[pl.ds(k, 1), :] * buf[pl.ds(HALO_A - (KA - 1) + k, tm), :]
        buf[pl.ds(0, HALO_A), :] = buf[pl.ds(tm, HALO_A), :]
        yb = (bv * cc).astype(BF16)
        b_ref[...] = bv.astype(BF16)
        c_ref[...] = cval.astype(BF16)
        v_ref[...] = vval.astype(BF16)
        cc_ref[...] = cc.astype(BF16)
        y_ref[...] = yb
        h_ref[...] = xv + _dot(yb, wout[...])

    act = _sds((T, D), BF16)
    return _hosted_call(
        body, "mixa_fwd", T // tm,
        in_specs=[_row_spec(tm, D), _full_spec((1, D)), _full_spec((KA, D))],
        out_specs=[_row_spec(tm, D) for _ in range(6)],
        out_shape=(_sds((T, D), F32),) + (act,) * 5,
        scratch=[pltpu.VMEM((3 * D, D), BF16), pltpu.VMEM((D, D), BF16),
                 pltpu.VMEM((HALO_A + tm, D), F32), pltpu.SemaphoreType.DMA((2 * NDEV,))],
        args=(x, gam, cw), comm=comm)


def _ffn_fwd(h, gam, wg, layer, tm, comm, head=None):
    T = h.shape[0]
    kg, ku, kd = "g%d" % layer, "u%d" % layer, "d%d" % layer
    n_head = 0 if head is None else 2

    def body(*refs):
        h_ref, gam_ref = refs[:2]
        wg_ref = refs[2 + n_head]
        o_ref, n_ref, g_ref, u_ref, gu_ref = refs[3 + n_head:8 + n_head]
        wgt, wut, wd, sems = refs[-4:]

        first = pl.program_id(0) == 0
        load_w = _weight_loads(wg_ref, [(wgt, kg), (wut, ku), (wd, kd)], sems)
        pl.when(first)(load_w)

        hv = h_ref[...]
        nb = (hv * _rms_stat(hv) * gam_ref[...]).astype(BF16)
        n_ref[...] = nb
        out = hv
        for f in range(2):
            cols = pl.ds(f * FH, FH)
            g = _dot_nt(nb, wgt[cols, :])
            u = _dot_nt(nb, wut[cols, :])
            gu = (g * _sigmoid(g) * u).astype(BF16)
            g_ref[:, cols] = g.astype(BF16)
            u_ref[:, cols] = u.astype(BF16)
            gu_ref[:, cols] = gu
            out = out + _dot(gu, wd[cols, :])
        if head is None:
            o_ref[...] = out
        else:
            t_ref, fin_ref, st_ref = refs[2], refs[3], refs[8 + n_head]

            @pl.when(pl.program_id(0) == 0)
            def _():
                st_ref[...] = jnp.zeros((8, D), F32)

            gamma = fin_ref[...]
            r = _rms_stat(out)
            err = out * r * gamma - t_ref[...]
            dx, dgam = _rms_bwd(err * (1.0 / D), out, r, gamma)
            o_ref[...] = dx
            st_ref[pl.ds(0, 1), :] += dgam
            st_ref[pl.ds(1, 1), :] += (0.5 / D) * jnp.sum(err * err, axis=0, keepdims=True)

    pre = _sds((T, FF), BF16)
    in_specs = [_row_spec(tm, D), _full_spec((1, D))]
    args = (h, gam)
    out_specs = [_row_spec(tm, D), _row_spec(tm, D), _row_spec(tm, FF), _row_spec(tm, FF), _row_spec(tm, FF)]
    out_shape = (_sds((T, D), F32), _sds((T, D), BF16), pre, pre, pre)
    if head is not None:
        in_specs, args = in_specs + [_row_spec(tm, D), _full_spec((1, D))], args + tuple(head)
        out_specs, out_shape = out_specs + [_full_spec((8, D))], out_shape + (_sds((8, D), F32),)
    if comm is None:
        in_specs, args = in_specs + [_ANY], args + (wg,)
    return _hosted_call(
        body, "ffn%d_fwd" % layer, T // tm,
        in_specs=in_specs, out_specs=out_specs, out_shape=out_shape,
        scratch=[pltpu.VMEM((FF, D), BF16)] * 3 + [pltpu.SemaphoreType.DMA((3 * NDEV,))],
        args=args, comm=comm)


def _shifted_copies(buf, shf, tm):
    for r in range(1, 8):
        shf[r - 1] = buf[pl.ds(r, tm + HALO_B - 8), :]


def _broadcast_taps(vec_ref, wb):
    for k in range(KB):
        wb[k] = jnp.broadcast_to(vec_ref[pl.ds(SM_B_CONV + k, 1), :], (8, D))
    wb[KB] = jnp.broadcast_to(vec_ref[pl.ds(SM_B_BCONV, 1), :], (8, D))


def _taps_by_shift_residue(taps):
    groups = {}
    for k, shift in taps:
        q, r = divmod(shift, 8)
        groups.setdefault(r, []).append((k, q))
    return sorted(groups.items())


def _window(buf, shf, base, r, q0, n_groups, lanes):
    rows = pl.ds(base + 8 * q0, 8 * n_groups)
    v = buf[rows, lanes] if r == 0 else shf[r - 1, rows, lanes]
    return [v[8 * i:8 * i + 8] for i in range(n_groups)]


def _long_conv(buf, shf, wb, out_ref, tm, taps, bias_row):
    n_acc = CONV_ROWS // 8
    groups = _taps_by_shift_residue(taps)
    for col in range(D // 128):
        lanes = pl.ds(128 * col, 128)

        def rows(i, carry, lanes=lanes):
            base = i * CONV_ROWS
            init = jnp.zeros((8, 128), F32) if bias_row is None else wb[bias_row, :, lanes]
            accs = [init] * n_acc
            for r, lst in groups:
                q0, q1 = min(q for _, q in lst), max(q for _, q in lst)
                win = _window(buf, shf, base, r, q0, n_acc + q1 - q0, lanes)
                for k, q in lst:
                    wk = wb[k, :, lanes]
                    accs = [acc + wk * win[h + q - q0] for h, acc in enumerate(accs)]
            for h, acc in enumerate(accs):
                out_ref[pl.ds(base + 8 * h, 8), lanes] = acc
            return carry

        for i in range(tm // CONV_ROWS):
            rows(i, 0)


def _long_conv_grad_taps(buf, shf, x_ref, wacc, tm):
    n_acc = CONV_ROWS // 8
    groups = _taps_by_shift_residue([(KB - 1 - j, j) for j in range(KB)])
    for col in range(D // 128):
        lanes = pl.ds(128 * col, 128)

        def rows(i, carry, lanes=lanes):
            base = i * CONV_ROWS
            xv = x_ref[pl.ds(base, CONV_ROWS), lanes]
            xs = [xv[8 * h:8 * h + 8] for h in range(n_acc)]
            for r, lst in groups:
                q0, q1 = min(q for _, q in lst), max(q for _, q in lst)
                win = _window(buf, shf, base, r, q0, n_acc + q1 - q0, lanes)
                for k, q in lst:
                    prod = [x * win[h + q - q0] for h, x in enumerate(xs)]
                    wacc[k, :, lanes] += functools.reduce(lambda a, b: a + b, prod)
            return carry

        for i in range(tm // CONV_ROWS):
            rows(i, 0)


def _ln_stats(dc):
    mu = jnp.mean(dc, axis=-1, keepdims=True)
    xc = dc - mu
    rstd = lax.rsqrt(jnp.mean(xc * xc, axis=-1, keepdims=True) + LN_EPS)
    return xc * rstd, rstd


def _mixb_fwd(h, vecs, bias1, wg, tm, comm):
    T = h.shape[0]

    def body(h_ref, vec_ref, b1_ref, wg_ref, o_ref, a_ref, g_ref, dc_ref, s_ref, w1, w2, buf, shf, wb, sems):
        first = pl.program_id(0) == 0
        load_w = _weight_loads(wg_ref, [(w1, "b_pw1"), (w2, "b_pw2")], sems)

        @pl.when(first)
        def _():
            load_w()
            buf[pl.ds(0, HALO_B), :] = jnp.zeros((HALO_B, D), F32)
            _broadcast_taps(vec_ref, wb)

        hv = h_ref[...]
        nb = (hv * _rms_stat(hv) * vec_ref[pl.ds(SM_B_NORM, 1), :]).astype(BF16)
        a = _dot_nt(nb, w1[pl.ds(0, D), :]) + b1_ref[:, pl.ds(0, D)]
        g = _dot_nt(nb, w1[pl.ds(D, D), :]) + b1_ref[:, pl.ds(D, D)]
        a_ref[...] = a.astype(BF16)
        g_ref[...] = g.astype(BF16)
        buf[pl.ds(HALO_B, tm), :] = a * _sigmoid(g)
        _shifted_copies(buf, shf, tm)

        _long_conv(buf, shf, wb, dc_ref, tm, [(k, HALO_B - (KB - 1) + k) for k in range(KB)], KB)
        buf[pl.ds(0, HALO_B), :] = buf[pl.ds(tm, HALO_B), :]
        xhat, _ = _ln_stats(dc_ref[...])
        ln = xhat * vec_ref[pl.ds(SM_LN_G, 1), :] + vec_ref[pl.ds(SM_LN_B, 1), :]
        s = (ln * _sigmoid(ln)).astype(BF16)
        s_ref[...] = s
        o_ref[...] = hv + _dot(s, w2[...]) + vec_ref[pl.ds(SM_B_PW2, 1), :]

    act = _sds((T, D), BF16)
    return _hosted_call(
        body, "mixb_fwd", T // tm,
        in_specs=[_row_spec(tm, D), _full_spec((SM_F32_ROWS, D)), _full_spec((1, 2 * D))],
        out_specs=[_row_spec(tm, D) for _ in range(5)],
        out_shape=(_sds((T, D), F32), act, act, _sds((T, D), F32), act),
        scratch=[pltpu.VMEM((2 * D, D), BF16), pltpu.VMEM((D, D), BF16),
                 pltpu.VMEM((HALO_B + tm, D), F32), pltpu.VMEM((7, HALO_B + tm - 8, D), F32),
                 pltpu.VMEM((KB + 1, 8, D), F32), pltpu.SemaphoreType.DMA((2 * NDEV,))],
        args=(h, vecs, bias1), comm=comm)


def _ffn_bwd_dx(dh, h, g, u, gam, wg, layer, tm, comm):
    T = h.shape[0]
    kg, ku, kd = "g%d" % layer, "u%d" % layer, "d%d" % layer

    def body(dh_ref, h_ref, g_ref, u_ref, gam_ref, wg_ref, o_ref, dg_ref, du_ref, st_ref, wgt, wut, wd, sems):
        first = pl.program_id(0) == 0
        load_w = _weight_loads(wg_ref, [(wd, kd), (wgt, kg), (wut, ku)], sems)

        @pl.when(first)
        def _():
            load_w()
            st_ref[...] = jnp.zeros((8, D), F32)

        dhv = dh_ref[...]
        dhb = dhv.astype(BF16)
        dn = jnp.zeros_like(dhv)
        for f in range(2):
            cols = pl.ds(f * FH, FH)
            dgu = _dot_nt(dhb, wd[cols, :])
            gv = g_ref[:, cols].astype(F32)
            sg = _sigmoid(gv)
            du = (dgu * gv * sg).astype(BF16)
            dg = (dgu * u_ref[:, cols].astype(F32) * (sg * (1.0 + gv * (1.0 - sg)))).astype(BF16)
            dg_ref[:, cols] = dg
            du_ref[:, cols] = du
            dn = dn + _dot(dg, wgt[cols, :]) + _dot(du, wut[cols, :])
        hv = h_ref[...]
        dx, dgam = _rms_bwd(dn, hv, _rms_stat(hv), gam_ref[...])
        o_ref[...] = dhv + dx
        st_ref[pl.ds(0, 1), :] += dgam

    pre = _sds((T, FF), BF16)
    return _hosted_call(
        body, "ffn%d_bwd_dx" % layer, T // tm,
        in_specs=[_row_spec(tm, D), _row_spec(tm, D), _row_spec(tm, FF), _row_spec(tm, FF),
                  _full_spec((1, D)), _ANY],
        out_specs=[_row_spec(tm, D), _row_spec(tm, FF), _row_spec(tm, FF), _full_spec((8, D))],
        out_shape=(_sds((T, D), F32), pre, pre, _sds((8, D), F32)),
        scratch=[pltpu.VMEM((FF, D), BF16)] * 3 + [pltpu.SemaphoreType.DMA((3 * NDEV,))],
        args=(dh, h, g, u, gam, wg), comm=comm)


def _grad_w(lhs, rhs, mc, name, tm):
    T, M = lhs.shape
    nt = T // tm

    def body(l_ref, r_ref, o_ref, acc):
        i = pl.program_id(1)

        @pl.when(i == 0)
        def _():
            acc[...] = jnp.zeros((mc, D), F32)

        acc[...] += _dot_tn(l_ref[...], r_ref[...].astype(BF16))

        @pl.when(i == nt - 1)
        def _():
            o_ref[...] = acc[...].astype(BF16)

    return _pcall(
        body, name=name, grid=(M // mc, nt),
        out_shape=_sds((M, D), BF16),
        in_specs=[pl.BlockSpec((tm, mc), lambda j, i: (i, j)), pl.BlockSpec((tm, D), lambda j, i: (i, 0))],
        out_specs=pl.BlockSpec((mc, D), lambda j, i: (j, 0)),
        scratch_shapes=[pltpu.VMEM((mc, D), F32)],
        compiler_params=_cparams(("arbitrary", "arbitrary")),
    )(lhs, rhs)


def _mixb_bwd(dh, h, a, g, dc, s, vecs, wg, tm, comm):
    T = h.shape[0]
    nt = T // tm

    def body(dh_ref, h_ref, a_ref, g_ref, dc_ref, s_ref, vec_ref, wg_ref, o_ref, st_ref, sb_ref, g1_ref, g2_ref,
             w1, w2, buf, shf, glu_s, dglu_s, wacc, wb, acc1, acc2, stage, sems, out_sems):
        first = pl.program_id(0) == 0
        load_w = _weight_loads(wg_ref, [(w2, "b_pw2"), (w1, "b_pw1")], sems)

        @pl.when(first)
        def _():
            load_w()
            buf[pl.ds(tm, HALO_B), :] = jnp.zeros((HALO_B, D), F32)
            wacc[...] = jnp.zeros((KB, 8, D), F32)
            _broadcast_taps(vec_ref, wb)
            st_ref[...] = jnp.zeros((SM_F32_ROWS, D), F32)
            sb_ref[...] = jnp.zeros((8, 2 * D), F32)
            acc1[...] = jnp.zeros((2 * D, D), F32)
            acc2[...] = jnp.zeros((D, D), F32)

        def acc(row, val):
            st_ref[pl.ds(row, 1), :] += jnp.sum(val, axis=0, keepdims=True)

        dhv = dh_ref[...]
        dhb = dhv.astype(BF16)
        acc(SM_B_PW2, dhv)
        acc2[...] += _dot_tn(s_ref[...], dhb)
        ds = _dot_nt(dhb, w2[...])
        xhat, rstd = _ln_stats(dc_ref[...])
        ln_g = vec_ref[pl.ds(SM_LN_G, 1), :]
        ln = xhat * ln_g + vec_ref[pl.ds(SM_LN_B, 1), :]
        sl = _sigmoid(ln)
        dln = ds * (sl * (1.0 + ln * (1.0 - sl)))
        acc(SM_LN_G, dln * xhat)
        acc(SM_LN_B, dln)
        dxh = dln * ln_g
        ddc = rstd * (dxh - jnp.mean(dxh, axis=-1, keepdims=True)
                      - xhat * jnp.mean(dxh * xhat, axis=-1, keepdims=True))
        acc(SM_B_BCONV, ddc)
        buf[pl.ds(0, tm), :] = ddc
        _shifted_copies(buf, shf, tm)
        av = a_ref[...].astype(F32)
        sg = _sigmoid(g_ref[...].astype(F32))
        glu_s[...] = av * sg

        _long_conv(buf, shf, wb, dglu_s, tm, [(KB - 1 - j, j) for j in range(KB)], None)
        _long_conv_grad_taps(buf, shf, glu_s, wacc, tm)
        buf[pl.ds(tm, HALO_B), :] = buf[pl.ds(0, HALO_B), :]
        dglu = dglu_s[...]
        da = dglu * sg
        dg = dglu * av * sg * (1.0 - sg)
        sb_ref[pl.ds(0, 1), pl.ds(0, D)] += jnp.sum(da, axis=0, keepdims=True)
        sb_ref[pl.ds(0, 1), pl.ds(D, D)] += jnp.sum(dg, axis=0, keepdims=True)
        dab, dgb = da.astype(BF16), dg.astype(BF16)
        dn = _dot(dab, w1[pl.ds(0, D), :]) + _dot(dgb, w1[pl.ds(D, D), :])
        hv = h_ref[...]
        r = _rms_stat(hv)
        gamma = vec_ref[pl.ds(SM_B_NORM, 1), :]
        nb = (hv * r * gamma).astype(BF16)
        acc1[pl.ds(0, D), :] += _dot_tn(dab, nb)
        acc1[pl.ds(D, D), :] += _dot_tn(dgb, nb)
        dx, dgam = _rms_bwd(dn, hv, r, gamma)
        o_ref[...] = dhv + dx
        st_ref[pl.ds(SM_B_NORM, 1), :] += dgam

        @pl.when(pl.program_id(0) == nt - 1)
        def _():
            st_ref[pl.ds(SM_B_CONV, KB), :] = jnp.sum(wacc[...], axis=1)
            blocks = [(acc1, 0, g1_ref, 0), (acc1, D, g1_ref, D), (acc2, 0, g2_ref, 0)]
            for k, (src, a0, dst, d0) in enumerate(blocks):
                stage[...] = src[pl.ds(a0, D), :].astype(BF16)
                cp = pltpu.make_async_copy(stage, dst.at[pl.ds(d0, D), :], out_sems.at[k])
                cp.start()
                cp.wait()

    rs = functools.partial(_row_spec, rev_nt=nt)
    return _hosted_call(
        body, "mixb_bwd", nt,
        in_specs=[rs(tm, D) for _ in range(6)] + [_full_spec((SM_F32_ROWS, D)), _ANY],
        out_specs=[rs(tm, D), _full_spec((SM_F32_ROWS, D)), _full_spec((8, 2 * D)), _ANY, _ANY],
        out_shape=(_sds((T, D), F32), _sds((SM_F32_ROWS, D), F32), _sds((8, 2 * D), F32),
                   _sds((2 * D, D), BF16), _sds((D, D), BF16)),
        scratch=[pltpu.VMEM((2 * D, D), BF16), pltpu.VMEM((D, D), BF16),
                 pltpu.VMEM((tm + HALO_B, D), F32), pltpu.VMEM((7, tm + HALO_B - 8, D), F32),
                 pltpu.VMEM((tm, D), F32), pltpu.VMEM((tm, D), F32), pltpu.VMEM((KB, 8, D), F32),
                 pltpu.VMEM((KB + 1, 8, D), F32), pltpu.VMEM((2 * D, D), F32), pltpu.VMEM((D, D), F32),
                 pltpu.VMEM((D, D), BF16), pltpu.SemaphoreType.DMA((2 * NDEV,)), pltpu.SemaphoreType.DMA((3,))],
        args=(dh, h, a, g, dc, s, vecs, wg), comm=comm)


def _mixa_bwd(dh, x, b, c, v, cc, y, gam, cw, wg, after, tm):
    T = x.shape[0]
    nt = T // tm

    def body(dh_ref, x_ref, b_ref, c_ref, v_ref, cc_ref, y_ref, gam_ref, cw_ref, wg_ref, after_ref,
             o_ref, st_ref, gin_ref, gout_ref, win, wout, buf, acc_in, acc_out, stage, sems, out_sems):
        first = pl.program_id(0) == 0
        load_w = _weight_loads(wg_ref, [(wout, "a_out"), (win, "a_in")], sems)

        @pl.when(first)
        def _():
            load_w()
            buf[pl.ds(tm, HALO_A), :] = jnp.zeros((HALO_A, D), F32)
            st_ref[...] = jnp.zeros((8, D), F32)
            acc_in[...] = jnp.zeros((3 * D, D), F32)
            acc_out[...] = jnp.zeros((D, D), F32)

        dhv = dh_ref[...]
        dhb = dhv.astype(BF16)
        dy = _dot_nt(dhb, wout[...])
        cval = c_ref[...].astype(F32)
        vval = v_ref[...].astype(F32)
        d_b = (dy * cc_ref[...].astype(F32)).astype(BF16)
        buf[pl.ds(0, tm), :] = dy * b_ref[...].astype(F32)
        cv = cval * vval
        dcv = jnp.zeros((tm, D), F32)
        for j in range(KA):
            sh = buf[pl.ds(j, tm), :]
            k = KA - 1 - j
            dcv = dcv + cw_ref[pl.ds(k, 1), :] * sh
            st_ref[pl.ds(1 + k, 1), :] += jnp.sum(cv * sh, axis=0, keepdims=True)
        buf[pl.ds(tm, HALO_A), :] = buf[pl.ds(0, HALO_A), :]
        d_c = (dcv * vval).astype(BF16)
        d_v = (dcv * cval).astype(BF16)
        xv = x_ref[...]
        r = _rms_stat(xv)
        gamma = gam_ref[...]
        nb = (xv * r * gamma).astype(BF16)
        for j, piece in enumerate((d_b, d_c, d_v)):
            acc_in[pl.ds(j * D, D), :] += _dot_tn(piece, nb)
        acc_out[...] += _dot_tn(y_ref[...], dhb)
        dn = _dot(d_b, win[pl.ds(0, D), :]) + _dot(d_c, win[pl.ds(D, D), :]) + _dot(d_v, win[pl.ds(2 * D, D), :])
        dx, dgam = _rms_bwd(dn, xv, r, gamma)
        o_ref[...] = dhv + dx
        st_ref[pl.ds(0, 1), :] += dgam

        @pl.when(pl.program_id(0) == nt - 1)
        def _():
            blocks = [(acc_in, j * D, gin_ref, j * D) for j in range(3)] + [(acc_out, 0, gout_ref, 0)]
            for k, (acc, a0, dst, d0) in enumerate(blocks):
                stage[...] = acc[pl.ds(a0, D), :].astype(BF16)
                cp = pltpu.make_async_copy(stage, dst.at[pl.ds(d0, D), :], out_sems.at[k])
                cp.start()
                cp.wait()

    rs = functools.partial(_row_spec, rev_nt=nt)
    return _hosted_call(
        body, "mixa_bwd", nt,
        in_specs=[rs(tm, D) for _ in range(7)] + [_full_spec((1, D)), _full_spec((KA, D)), _ANY, _ANY],
        out_specs=[rs(tm, D), _full_spec((8, D)), _ANY, _ANY],
        out_shape=(_sds((T, D), F32), _sds((8, D), F32), _sds((3 * D, D), BF16), _sds((D, D), BF16)),
        scratch=[pltpu.VMEM((3 * D, D), BF16), pltpu.VMEM((D, D), BF16), pltpu.VMEM((tm + HALO_A, D), F32),
                 pltpu.VMEM((3 * D, D), F32), pltpu.VMEM((D, D), F32), pltpu.VMEM((D, D), BF16),
                 pltpu.SemaphoreType.DMA((2 * NDEV,)), pltpu.SemaphoreType.DMA((4,))],
        args=(dh, x, b, c, v, cc, y, gam, cw, wg, after), comm=None)


def _sum_small(sm, rp):
    def body(sm_ref, rp_ref, osm, orp, oloss):
        a = sm_ref[0]
        b = rp_ref[0]
        for s in range(1, NDEV):
            a = a + sm_ref[s]
            b = b + rp_ref[s]
        osm[...] = a
        orp[...] = b
        oloss[...] = jnp.zeros((8, 128), F32) + jnp.sum(b[4:5, :], axis=-1, keepdims=True)

    return _pcall(
        body, name="sum_small_grads",
        out_shape=(_sds((SM_F32_ROWS, 128), F32), _sds((8, D), F32), _sds((8, 128), F32)),
        in_specs=[pl.BlockSpec(memory_space=pltpu.VMEM)] * 2,
        out_specs=tuple(pl.BlockSpec(memory_space=pltpu.VMEM) for _ in range(3)),
        compiler_params=_cparams(),
    )(sm, rp)


def _adam_math(w, g, m, v):
    mn = B1 * m + (1.0 - B1) * g
    vn = B2 * v + (1.0 - B2) * (g * g)
    m_hat = mn / (1.0 - B1 ** STEP)
    v_hat = vn / (1.0 - B2 ** STEP)
    return -LR * (m_hat / (jnp.sqrt(v_hat) + ADAM_EPS) + WD * w), mn, vn


def _finish_weight(lands, keys, transposed, w, m, v, name, after=None):
    layers, rows, cols = w.shape
    n = N_ROWS[keys[0]]
    n_pad = -(-n // 128) * 128 if transposed else n

    def body(w_ref, m_ref, v_ref, *rest):
        og, od, om, ov, buf, sem = rest[-6:]
        for l in range(layers):
            @pl.when(pl.program_id(0) == l)
            def _(land_ref=rest[l], off=LD_OFF[keys[l]] - lands[l][1]):
                cp = pltpu.make_async_copy(land_ref.at[:, pl.ds(off, n), :], buf, sem)
                cp.start()
                cp.wait()

        g = buf[0].astype(F32)
        for s in range(1, NCHIP):
            g = g + buf[s].astype(F32)
        if transposed:
            if n_pad != n:
                g = jnp.concatenate([g, jnp.zeros((n_pad - n, D), F32)], axis=0)
            g = g.T[:, :n]
        d, mn, vn = _adam_math(w_ref[...], g, m_ref[...], v_ref[...])
        og[...] = g
        od[...] = d
        om[...] = mn
        ov[...] = vn

    spec = pl.BlockSpec((None, rows, cols), lambda l: (l, 0, 0))
    shp = _sds(w.shape, F32)
    extra = [buf_ for buf_, _ in lands] + ([] if after is None else [after])
    return _hosted_call(
        body, "finish_" + name, layers,
        in_specs=[spec, spec, spec] + [_ANY] * len(extra), out_specs=[spec] * 4,
        out_shape=(shp,) * 4, scratch=[pltpu.VMEM((NCHIP, n, D), BF16), pltpu.SemaphoreType.DMA],
        args=(w, m, v, *extra), comm=None)


SMALL_PARAMS = (("sm", SM_B_NORM, 1), ("sm", SM_B_PW1, 2), ("sm", SM_B_CONV, KB), ("sm", SM_B_BCONV, 1),
                ("sm", SM_LN_G, 1), ("sm", SM_LN_B, 1), ("sm", SM_B_PW2, 1), ("sm", SM_A_CONV, KA),
                ("rp", 0, 1), ("rp", 1, 2), ("rp", 3, 1))


def _adamw_small(g_small, g_repl, triples):
    n = len(SMALL_PARAMS)

    def body(*refs):
        gs_ref, gr_ref = refs[0], refs[1]
        ins = refs[2:2 + 3 * n]
        outs = refs[2 + 3 * n:]
        for i, (src, r0, nr) in enumerate(SMALL_PARAMS):
            w_ref, m_ref, v_ref = ins[3 * i:3 * i + 3]
            g = (gs_ref if src == "sm" else gr_ref)[pl.ds(r0, nr), :]
            if i == 1:
                g = jnp.concatenate([g[0:1], g[1:2]], axis=1)
            lead = (0,) if len(w_ref.shape) == 3 else ()
            idx = lead + (slice(None), slice(None))
            vals = (g,) + _adam_math(w_ref[idx], g, m_ref[idx], v_ref[idx])
            for o_ref, val in zip(outs[4 * i:4 * i + 4], vals):
                o_ref[idx] = val

    flat = [a for t in triples for a in t]
    out_shape = tuple(_sds(t[0].shape, F32) for t in triples for _ in range(4))
    vm = pl.BlockSpec(memory_space=pltpu.VMEM)
    res = _pcall(
        body, name="adamw_small", out_shape=out_shape,
        in_specs=[vm] * (2 + len(flat)), out_specs=tuple(vm for _ in out_shape),
        compiler_params=_cparams(),
    )(g_small, g_repl, *flat)
    return [tuple(res[4 * i:4 * i + 4]) for i in range(n)]


def kernel(x, a_norm, a_w_in, a_conv, a_w_out, b_norm, b_w_pw1, b_b_pw1, b_conv, b_b_conv, b_ln_g, b_ln_b, b_w_pw2, b_b_pw2, ffn_norm, ffn_w_gate, ffn_w_up, ffn_w_down, final_norm, loss_target, m_a_norm, m_a_w_in, m_a_conv, m_a_w_out, m_b_norm, m_b_w_pw1, m_b_b_pw1, m_b_conv, m_b_b_conv, m_b_ln_g, m_b_ln_b, m_b_w_pw2, m_b_b_pw2, m_ffn_norm, m_ffn_w_gate, m_ffn_w_up, m_ffn_w_down, m_final_norm, v_a_norm, v_a_w_in, v_a_conv, v_a_w_out, v_b_norm, v_b_w_pw1, v_b_b_pw1, v_b_conv, v_b_b_conv, v_b_ln_g, v_b_ln_b, v_b_w_pw2, v_b_b_pw2, v_ffn_norm, v_ffn_w_gate, v_ffn_w_up, v_ffn_w_down, v_final_norm):
    T = x.shape[1]
    tm = min(TM, T)
    tma = min(TM_A, T)
    tw = min(TM_DW, T)
    xs = x.reshape(T, D)
    tgt = loss_target.reshape(T, D)

    def rows_first(a):
        return jnp.swapaxes(a, 1, 2)

    shard, small = _pack_shard(
        (a_w_in, a_w_out, b_w_pw1, b_w_pw2, rows_first(ffn_w_gate), rows_first(ffn_w_up), ffn_w_down),
        (b_norm, b_b_pw1, b_conv, b_b_conv, b_ln_g, b_ln_b, b_b_pw2, a_conv))
    wg, sm_all = _all_gather_first(shard, small)
    vecs = sm_all.transpose(1, 0, 2).reshape(SM_F32_ROWS, D)
    bias1 = sm_all[:, SM_B_PW1:SM_B_PW1 + 2, :].reshape(1, 2 * D)
    cw_a = vecs[SM_A_CONV:SM_A_CONV + KA]
    fn0, fn1 = ffn_norm[0:1], ffn_norm[1:2]
    fin = final_norm.reshape(1, D)

    h1, bq, cq, vq, ccq, yq, wg = _mixa_fwd(xs, a_norm, cw_a, wg, tma, _ag_comm(shard, wg, ("g0", "u0", "d0")))
    h2, n1, g0, u0, gu0, wg = _ffn_fwd(h1, fn0, wg, 0, tma, _ag_comm(shard, wg, ("b_pw1", "b_pw2"), (T // tma) // 2))
    h3, aq, gq, dcq, sq, wg = _mixb_fwd(h2, vecs, bias1, wg, tm, _ag_comm(shard, wg, ("g1", "u1", "d1"), 1))
    dh4, n3, g1, u1, gu1, st_fin = _ffn_fwd(h3, fn1, wg, 1, tma, None, head=(tgt, fin))

    def by_dest(gw, key):
        return gw.reshape(NDEV, N_ROWS[key], D)

    gw_d1 = by_dest(_grad_w(gu1, dh4, FH, "grad_down1", tw), "d1")
    dh3, dg1, du1, st_f1 = _ffn_bwd_dx(dh4, h3, g1, u1, fn1, wg, 1, tma, None)
    gw_g1 = by_dest(_grad_w(dg1, n3, FH, "grad_gate1", tw), "g1")
    gw_u1 = by_dest(_grad_w(du1, n3, FH, "grad_up1", tw), "u1")
    keys = ("g1", "u1", "d1")
    dh2, st_b, st_b1, gw_pw1, gw_pw2, land = _mixb_bwd(
        dh3, h2, aq, gq, dcq, sq, vecs, wg, tm, _rs_comm(_pair_reduce([gw_g1, gw_u1, gw_d1], keys), keys, None))
    gw_pw1, gw_pw2 = by_dest(gw_pw1, "b_pw1"), by_dest(gw_pw2, "b_pw2")
    gw_d0 = by_dest(_grad_w(gu0, dh2, FH, "grad_down0", tw), "d0")
    keys = ("b_pw1", "b_pw2", "d0")
    dh1, dg0, du0, st_f0, land = _ffn_bwd_dx(
        dh2, h1, g0, u0, fn0, wg, 0, tma, _rs_comm(_pair_reduce([gw_pw1, gw_pw2, gw_d0], keys), keys, land))
    gw_g0 = by_dest(_grad_w(dg0, n1, FH, "grad_gate0", tw), "g0")
    gw_u0 = by_dest(_grad_w(du0, n1, FH, "grad_up0", tw), "u0")
    third = ("g0", "u0")
    sems3, p_third, land_third, token = _chip_start(_pair_reduce([gw_g0, gw_u0], third), "third")
    dx, st_a, gw_in, gw_out = _mixa_bwd(dh1, xs, bq, cq, vq, ccq, yq, a_norm, cw_a, wg, token, tm)
    _, land_third = _chip_wait(sems3, p_third, land_third, st_a, "third")
    land_third = _own_slot(land_third, p_third, "third")
    gw_in, gw_out = by_dest(gw_in, "a_in"), by_dest(gw_out, "a_out")

    st_small = st_b.at[SM_A_CONV:SM_A_CONV + KA].set(st_a[1:1 + KA])
    sm_dest = st_small.reshape(SM_F32_ROWS, NDEV, 128).transpose(1, 0, 2)
    sm_dest = sm_dest.at[:, SM_B_PW1:SM_B_PW1 + 2, :].set(st_b1[0].reshape(NDEV, 2, 128))
    repl = jnp.concatenate([st_a[0:1], st_f0[0:1], st_f1[0:1], st_fin[0:1], st_fin[1:2],
                            jnp.zeros((3, D), F32)], axis=0)[None]
    last = ("a_out", "a_in")
    p_last, sm_land, rp_land = _pair_reduce([gw_out, gw_in], last, side=(sm_dest, repl))
    sems, p_last, land_last, token = _chip_start(p_last, "last")

    big = [("ffn_w_gate", ("g0", "g1"), False,
            rows_first(ffn_w_gate), rows_first(m_ffn_w_gate), rows_first(v_ffn_w_gate)),
           ("ffn_w_up", ("u0", "u1"), False, rows_first(ffn_w_up), rows_first(m_ffn_w_up), rows_first(v_ffn_w_up)),
           ("ffn_w_down", ("d0", "d1"), False, ffn_w_down, m_ffn_w_down, v_ffn_w_down),
           ("a_w_out", ("a_out",), False, a_w_out, m_a_w_out, v_a_w_out),
           ("b_w_pw1", ("b_pw1",), True, b_w_pw1, m_b_w_pw1, v_b_w_pw1),
           ("b_w_pw2", ("b_pw2",), False, b_w_pw2, m_b_w_pw2, v_b_w_pw2),
           ("a_w_in", ("a_in",), True, a_w_in, m_a_w_in, v_a_w_in)]
    def landing(key):
        if key in third:
            return land_third, LD_OFF[third[0]]
        return (land_last, LD_OFF[last[0]]) if key in last else (land, 0)

    res, after = {}, token
    for name, keys, transposed, w, m, v in big:
        if keys[0] not in last:
            res[name] = _finish_weight([landing(k) for k in keys], keys, transposed, w, m, v, name, after=after)
            after = res[name][1]
    _, land_last = _chip_wait(sems, p_last, land_last, after, "last")
    land_last = _own_slot(land_last, p_last, "last")
    for name, keys, transposed, w, m, v in big:
        if keys[0] in last:
            res[name] = _finish_weight([landing(k) for k in keys], keys, transposed, w, m, v, name)
    for name in ("ffn_w_gate", "ffn_w_up"):
        res[name] = [rows_first(a) for a in res[name]]

    g_small, g_repl, loss8 = _sum_small(sm_land, rp_land)
    loss = loss8[0, 0]

    small_names = ["b_norm", "b_b_pw1", "b_conv", "b_b_conv", "b_ln_g", "b_ln_b", "b_b_pw2", "a_conv",
                   "a_norm", "ffn_norm", "final_norm"]
    triples = [(b_norm, m_b_norm, v_b_norm), (b_b_pw1, m_b_b_pw1, v_b_b_pw1), (b_conv, m_b_conv, v_b_conv),
               (b_b_conv, m_b_b_conv, v_b_b_conv), (b_ln_g, m_b_ln_g, v_b_ln_g), (b_ln_b, m_b_ln_b, v_b_ln_b),
               (b_b_pw2, m_b_b_pw2, v_b_b_pw2), (a_conv, m_a_conv, v_a_conv), (a_norm, m_a_norm, v_a_norm),
               (ffn_norm, m_ffn_norm, v_ffn_norm),
               (fin, m_final_norm.reshape(1, D), v_final_norm.reshape(1, D))]
    for name, quad in zip(small_names, _adamw_small(g_small, g_repl, triples)):
        res[name] = quad
    res["final_norm"] = tuple(a.reshape(D) for a in res["final_norm"])

    order = ["a_norm", "a_w_in", "a_conv", "a_w_out", "b_norm", "b_w_pw1", "b_b_pw1", "b_conv", "b_b_conv",
             "b_ln_g", "b_ln_b", "b_w_pw2", "b_b_pw2", "ffn_norm", "ffn_w_gate", "ffn_w_up", "ffn_w_down",
             "final_norm"]
    out = [loss, dx.reshape(1, T, D)]
    for j in range(4):
        out += [res[k][j] for k in order]
    return tuple(out)
```

```python
import functools

import jax
import jax.numpy as jnp
from jax import lax
from jax.experimental import pallas as pl
from jax.experimental.pallas import tpu as pltpu

F32 = jnp.float32
BF16 = jnp.bfloat16
MESH = pl.DeviceIdType.MESH

D = 1024
FF = 2816
FH = FF // 2
NDEV = 8
KA = 3
KB = 31
HALO_A = 8
HALO_B = 32
CONV_ROWS = 64
RMS_EPS = 1e-6
LN_EPS = 1e-5
LR, B1, B2, ADAM_EPS, WD, STEP = 0.001, 0.9, 0.999, 1e-08, 0.01, 10

TM = 256
TM_A = 512
TM_DW = 2048
VMEM_LIMIT = 62 * 1024 * 1024
LOCAL_THREAD = 1

N_ROWS = {"a_in": 384, "a_out": 128, "b_pw1": 256, "b_pw2": 128, "g0": 352, "u0": 352, "d0": 352,
          "g1": 352, "u1": 352, "d1": 352}
NCHIP = 4


def _offsets(order):
    off, o = {}, 0
    for k in order:
        off[k] = o
        o += N_ROWS[k]
    return off, o


PK_ORDER = ("a_in", "a_out", "g0", "u0", "d0", "b_pw1", "b_pw2", "g1", "u1", "d1")
PK_OFF, PK_ROWS = _offsets(PK_ORDER)
LD_ORDER = ("g1", "u1", "d1", "b_pw1", "b_pw2", "d0", "g0", "u0", "a_out", "a_in")
LD_OFF, N_WROWS = _offsets(LD_ORDER)


def _span(off, keys):
    return off[keys[0]], sum(N_ROWS[k] for k in keys)


SM_B_NORM, SM_B_PW1, SM_B_CONV, SM_B_BCONV, SM_LN_G, SM_LN_B, SM_B_PW2, SM_A_CONV = 0, 1, 3, 34, 35, 36, 37, 38
SM_F32_ROWS = 64


def _pcall(body, **kw):
    return pl.pallas_call(body, **kw)


def _cparams(sem=None):
    return pltpu.CompilerParams(dimension_semantics=sem, vmem_limit_bytes=VMEM_LIMIT)


def _dot(a, b):
    return jnp.dot(a, b, preferred_element_type=F32)


def _dot_nt(a, b):
    return lax.dot_general(a, b, (((1,), (1,)), ((), ())), preferred_element_type=F32)


def _dot_tn(a, b):
    return lax.dot_general(a, b, (((0,), (0,)), ((), ())), preferred_element_type=F32)


def _sigmoid(v):
    return 1.0 / (1.0 + jnp.exp(-v))


def _rms_stat(x):
    return lax.rsqrt(jnp.mean(x * x, axis=-1, keepdims=True) + RMS_EPS)


def _rms_bwd(dn, x, r, gamma):
    dng = dn * gamma
    dx = r * dng - x * (r * r * r) * jnp.mean(dng * x, axis=-1, keepdims=True)
    return dx, jnp.sum(dn * x * r, axis=0, keepdims=True)


def _weight_loads(wg_ref, plan, sems):
    copies = []
    for j, (dst, key) in enumerate(plan):
        off, n = PK_OFF[key], N_ROWS[key]
        copies += [pltpu.make_async_copy(
            wg_ref.at[d, pl.ds(off, n), :], dst.at[pl.ds(d * n, n), :], sems.at[j * NDEV + d])
            for d in range(NDEV)]

    def load():
        for cp in copies:
            cp.start()
        for cp in copies:
            cp.wait()

    return load


_ANY = pl.BlockSpec(memory_space=pl.ANY)


def _row_spec(tm, width, rev_nt=None):
    if rev_nt is None:
        return pl.BlockSpec((tm, width), lambda i: (i, 0))
    return pl.BlockSpec((tm, width), lambda i: (rev_nt - 1 - i, 0))


def _full_spec(shape):
    return pl.BlockSpec(shape, lambda *_: (0,) * len(shape))


def _sds(shape, dtype):
    return jax.ShapeDtypeStruct(shape, dtype)


def _mesh_pos():
    return lax.axis_index("x"), lax.axis_index("y"), lax.axis_index("c")


def _lin(p):
    return 4 * p[0] + 2 * p[1] + p[2]


def _ag_exchange(src, slot, send_sems, recv_sems, local_sem):
    x, y, c = _mesh_pos()
    me, sibling = (x, y, c), (x, y, 1 - c)
    chips = [(1 - x, y), (x, 1 - y), (1 - x, 1 - y)]

    def copy(k, block, to, own=False):
        return pltpu.make_async_remote_copy(
            src_ref=src if own else slot(block), dst_ref=slot(block),
            send_sem=send_sems.at[k], recv_sem=recv_sems.at[k], device_id=to, device_id_type=MESH)

    mine = pltpu.make_async_copy(src, slot(me), local_sem)
    first = [copy(0, me, sibling, own=True)]
    first += [copy(1 + j, me, (*chip, c), own=True) for j, chip in enumerate(chips)]
    passed = [copy(4 + j, (*chip, c), sibling) for j, chip in enumerate(chips)]

    def start():
        mine.start()
        for cp in first:
            cp.start()

    def forward():
        for j, chip in enumerate(chips):
            copy(1 + j, (*chip, c), me).wait_recv()
            passed[j].start()

    def finish():
        copy(0, sibling, me).wait_recv()
        for j, chip in enumerate(chips):
            copy(4 + j, (*chip, 1 - c), me).wait_recv()
        for cp in first + passed:
            cp.wait_send()
        mine.wait()

    return start, forward, finish


def _chip_exchange(p_ref, land_ref, send_sems, recv_sems, keys):
    x, y, c = _mesh_pos()
    l0, rows = _span(LD_OFF, keys)
    dst = land_ref.at[2 * x + y, pl.ds(l0, rows), :]
    peers = [((x + (j >> 1)) % 2, (y + (j & 1)) % 2) for j in range(NCHIP)]

    def copy(j):
        tx, ty = peers[j]
        src = p_ref.at[2 * tx + ty]
        if j == 0:
            return pltpu.make_async_copy(src, dst, send_sems.at[0])
        return pltpu.make_async_remote_copy(
            src_ref=src, dst_ref=dst, send_sem=send_sems.at[j], recv_sem=recv_sems.at[j],
            device_id=(tx, ty, c), device_id_type=MESH)

    def start():
        for j in range(NCHIP):
            copy(j).start()

    def finish():
        for j in range(NCHIP):
            copy(j).wait()

    return start, finish


def _all_to_all_f32(src_for, dst_ref, send_sems, recv_sems):
    x, y, c = _mesh_pos()
    dst = dst_ref.at[_lin((x, y, c))]
    peers = [((x + ((j >> 2) & 1)) % 2, (y + ((j >> 1) & 1)) % 2, (c + (j & 1)) % 2) for j in range(NDEV)]

    def copy(j):
        if j == 0:
            return pltpu.make_async_copy(src_for(_lin(peers[0])), dst, send_sems.at[0])
        return pltpu.make_async_remote_copy(
            src_ref=src_for(_lin(peers[j])), dst_ref=dst, send_sem=send_sems.at[j], recv_sem=recv_sems.at[j],
            device_id=peers[j], device_id_type=MESH)

    def start():
        for j in range(NDEV):
            copy(j).start()

    def finish():
        for j in range(NDEV):
            copy(j).wait()

    return start, finish


class _Comm:
    def __init__(self, ins, alias_in, out_shape, scratch, make, gives_wg, middle_step=0):
        self.ins, self.alias_in, self.out_shape = ins, alias_in, out_shape
        self.scratch, self.make, self.gives_wg, self.middle_step = scratch, make, gives_wg, middle_step


def _ag_comm(shard, wg, keys, forward_step=0):
    def make(c_ins, c_out, sc):
        return _weights_exchange(c_ins[0], c_out, sc[0], sc[1], sc[2], keys)

    return _Comm([shard, wg], 1, _sds(wg.shape, BF16),
                 [pltpu.SemaphoreType.DMA((7,)), pltpu.SemaphoreType.DMA((7,)), pltpu.SemaphoreType.DMA],
                 make, True, forward_step)


def _weights_exchange(shard_ref, wg_ref, send_sems, recv_sems, local_sem, keys):
    r0, nr = _span(PK_OFF, keys)
    return _ag_exchange(shard_ref.at[pl.ds(r0, nr), :], lambda p: wg_ref.at[_lin(p), pl.ds(r0, nr), :],
                        send_sems, recv_sems, local_sem)


def _rs_comm(psum, keys, land):
    def make(c_ins, c_out, sc):
        start, finish = _chip_exchange(c_ins[0], c_out, sc[0], sc[1], keys)
        return start, None, finish

    ins = [psum] + ([] if land is None else [land])
    return _Comm(ins, None if land is None else 1, _sds((NCHIP, N_WROWS, D), BF16),
                 [pltpu.SemaphoreType.DMA((NCHIP,)), pltpu.SemaphoreType.DMA((NCHIP,))], make, False)


def _hosted_call(body, name, nt, in_specs, out_specs, out_shape, scratch, args, comm):
    if comm is None:
        return _pcall(body, name=name, grid=(nt,), in_specs=in_specs, out_specs=tuple(out_specs),
                      out_shape=tuple(out_shape), scratch_shapes=scratch,
                      compiler_params=_cparams(("arbitrary",)))(*args)
    n_in, n_out, n_sc, n_cin = len(in_specs), len(out_specs), len(scratch), len(comm.ins)

    def wrapped(*refs):
        ins = refs[:n_in]
        c_ins = refs[n_in:n_in + n_cin]
        outs = refs[n_in + n_cin:n_in + n_cin + n_out]
        c_out = refs[n_in + n_cin + n_out]
        sc = refs[n_in + n_cin + n_out + 1:n_in + n_cin + n_out + 1 + n_sc]
        c_sc = refs[n_in + n_cin + n_out + 1 + n_sc:]
        start, middle, finish = comm.make(c_ins, c_out, c_sc)
        pl.when(pl.program_id(0) == 0)(start)
        if comm.gives_wg:
            body(*ins, c_out, *outs, *sc)
        else:
            body(*ins, *outs, *sc)
        if middle is not None:
            pl.when(pl.program_id(0) == max(nt - 1 - comm.middle_step, 0))(middle)
        pl.when(pl.program_id(0) == nt - 1)(finish)

    aliases = {} if comm.alias_in is None else {n_in + comm.alias_in: n_out}
    res = _pcall(wrapped, name=name, grid=(nt,),
                 in_specs=list(in_specs) + [_ANY] * n_cin, out_specs=tuple(out_specs) + (_ANY,),
                 out_shape=tuple(out_shape) + (comm.out_shape,),
                 scratch_shapes=list(scratch) + list(comm.scratch),
                 input_output_aliases=aliases,
                 compiler_params=_cparams(("arbitrary",)))(*args, *comm.ins)
    return res


def _pack_shard(weights, smalls):
    plan = (("a_in", 0, 0, True), ("a_out", 1, 0, False), ("b_pw1", 2, 0, True), ("b_pw2", 3, 0, False),
            ("g0", 4, 0, False), ("u0", 5, 0, False), ("d0", 6, 0, False),
            ("g1", 4, 1, False), ("u1", 5, 1, False), ("d1", 6, 1, False))
    n_pad = 384

    def body(*refs):
        out, sm, pad = refs[-3:]
        for key, idx, layer, transposed in plan:
            n = N_ROWS[key]
            val = refs[idx][layer]
            if transposed:
                if n % 128:
                    pad[:, pl.ds(0, n)] = val
                    pad[:, pl.ds(n, n_pad - n)] = jnp.zeros((D, n_pad - n), F32)
                    val = pad[...]
                val = val.T[:n]
            out[pl.ds(PK_OFF[key], n), :] = val.astype(BF16)
        b_norm, b_b_pw1, b_conv, b_b_conv, b_ln_g, b_ln_b, b_b_pw2, a_conv = refs[len(weights):-3]
        sm[...] = jnp.zeros((SM_F32_ROWS, 128), F32)
        sm[pl.ds(SM_B_PW1, 1), :] = b_b_pw1[:, pl.ds(0, 128)]
        sm[pl.ds(SM_B_PW1 + 1, 1), :] = b_b_pw1[:, pl.ds(128, 128)]
        sm[pl.ds(SM_B_CONV, KB), :] = b_conv[0]
        sm[pl.ds(SM_A_CONV, KA), :] = a_conv[0]
        for row, ref in ((SM_B_NORM, b_norm), (SM_B_BCONV, b_b_conv), (SM_LN_G, b_ln_g), (SM_LN_B, b_ln_b),
                         (SM_B_PW2, b_b_pw2)):
            sm[pl.ds(row, 1), :] = ref[...]

    vm = pl.BlockSpec(memory_space=pltpu.VMEM)
    return _pcall(
        body, name="pack_shard",
        out_shape=(_sds((PK_ROWS, D), BF16), _sds((SM_F32_ROWS, 128), F32)),
        in_specs=[vm] * (len(weights) + len(smalls)), out_specs=(vm, vm),
        scratch_shapes=[pltpu.VMEM((D, n_pad), F32)],
        compiler_params=_cparams(),
    )(*weights, *smalls)


def _all_gather_first(shard, small):
    def body(x_ref, s_ref, wg_ref, sg_ref, send_w, recv_w, local_w, send_s, recv_s, local_s):
        start_w, forward_w, finish_w = _weights_exchange(x_ref, wg_ref, send_w, recv_w, local_w, ("a_in", "a_out"))
        start_s, forward_s, finish_s = _ag_exchange(s_ref, lambda p: sg_ref.at[_lin(p)], send_s, recv_s, local_s)
        start_s()
        start_w()
        forward_s()
        forward_w()
        finish_s()
        finish_w()

    sems = [pltpu.SemaphoreType.DMA((7,)), pltpu.SemaphoreType.DMA((7,)), pltpu.SemaphoreType.DMA]
    return _pcall(
        body, name="all_gather_first",
        out_shape=(_sds((NDEV, PK_ROWS, D), BF16), _sds((NDEV, SM_F32_ROWS, 128), F32)),
        in_specs=[_ANY, _ANY], out_specs=(_ANY, _ANY), scratch_shapes=sems + sems,
    )(shard, small)


def _pair_reduce(grads, keys, side=None, land=None):
    l0, rows = _span(LD_OFF, keys)
    nk = len(keys)
    n_side = 0 if side is None else 2
    n_land = 0 if land is None else 1

    def body(c_ref, *refs):
        mine = refs[:nk]
        whole = refs[nk:2 * nk]
        n_in = 2 * nk + n_side + n_land
        o_ref = refs[n_in]
        got, send_sem, recv_sems = refs[n_in + 1 + n_side:n_in + 4 + n_side]
        t = pl.program_id(0)
        x, y, c = _mesh_pos()
        sibling = (x, y, 1 - c)
        if side is not None:
            sm_ref, rp_ref = refs[2 * nk:2 * nk + 2]
            sm_land, rp_land = refs[n_in + 1:n_in + 3]
            s2, r2, s3, r3 = refs[n_in + 6:n_in + 10]
            start_s, finish_s = _all_to_all_f32(lambda d: sm_ref.at[d], sm_land, s2, r2)
            start_r, finish_r = _all_to_all_f32(lambda d: rp_ref.at[0], rp_land, s3, r3)

        @pl.when(t == 0)
        def _():
            if side is not None:
                start_s()
                start_r()
            for ref, key in zip(whole, keys):
                for d in range(NCHIP):
                    pltpu.make_async_remote_copy(
                        src_ref=ref.at[2 * d + (1 - c)], dst_ref=got.at[d, pl.ds(LD_OFF[key] - l0, N_ROWS[key]), :],
                        send_sem=send_sem, recv_sem=recv_sems.at[d], device_id=sibling, device_id_type=MESH).start()

        pltpu.make_async_remote_copy(src_ref=got.at[t], dst_ref=got.at[t], send_sem=send_sem,
                                     recv_sem=recv_sems.at[t], device_id=sibling, device_id_type=MESH).wait_recv()
        sums = o_ref if land is None else refs[-3].at[t]
        for ref, key in zip(mine, keys):
            sl = pl.ds(LD_OFF[key] - l0, N_ROWS[key])
            sums[sl, :] = (ref[...].astype(F32) + got[t, sl, :].astype(F32)).astype(BF16)

        if land is not None:
            psum, chip_send, chip_recv = refs[-3:]
            my_chip = 2 * x + y
            dst = o_ref.at[my_chip, pl.ds(l0, rows), :]
            hop = jnp.bitwise_xor(t, my_chip)

            @pl.when(t == my_chip)
            def _():
                pltpu.make_async_copy(psum.at[t], dst, chip_send.at[0]).start()

            @pl.when(t != my_chip)
            def _():
                pltpu.make_async_remote_copy(
                    src_ref=psum.at[t], dst_ref=dst, send_sem=chip_send.at[hop], recv_sem=chip_recv.at[hop],
                    device_id=(t // 2, t % 2, c), device_id_type=MESH).start()

        @pl.when(t == NCHIP - 1)
        def _():
            pltpu.make_async_remote_copy(src_ref=got, dst_ref=got, send_sem=send_sem, recv_sem=recv_sems.at[0],
                                         device_id=sibling, device_id_type=MESH).wait_send()
            if land is not None:
                one = o_ref.at[0, pl.ds(l0, rows), :]
                pltpu.make_async_copy(one, one, chip_send.at[0]).wait()
                for j in range(1, NCHIP):
                    pltpu.make_async_remote_copy(src_ref=one, dst_ref=one, send_sem=chip_send.at[j],
                                                 recv_sem=chip_recv.at[j], device_id=sibling, device_id_type=MESH).wait()
            if side is not None:
                finish_s()
                finish_r()

    if land is None:
        out_specs = [pl.BlockSpec((None, rows, D), lambda t, c: (t, 0, 0))]
        out_shape = [_sds((NCHIP, rows, D), BF16)]
    else:
        out_specs, out_shape = [_ANY], [_sds((NCHIP, N_WROWS, D), BF16)]
    scratch = [pltpu.VMEM((NCHIP, rows, D), BF16), pltpu.SemaphoreType.DMA, pltpu.SemaphoreType.DMA((NCHIP,))]
    if side is not None:
        out_specs += [_ANY, _ANY]
        out_shape += [_sds((NDEV, SM_F32_ROWS, 128), F32), _sds((NDEV, 8, D), F32)]
        scratch += [pltpu.SemaphoreType.DMA((NDEV,))] * 4
    if land is not None:
        scratch += [pltpu.VMEM((NCHIP, rows, D), BF16), pltpu.SemaphoreType.DMA((NCHIP,)),
                    pltpu.SemaphoreType.DMA((NCHIP,))]
    grid_spec = pltpu.PrefetchScalarGridSpec(
        num_scalar_prefetch=1, grid=(NCHIP,),
        in_specs=[pl.BlockSpec((None, N_ROWS[k], D), lambda t, c: (2 * t + c[0], 0, 0)) for k in keys]
        + [_ANY] * (nk + n_side + n_land),
        out_specs=tuple(out_specs), scratch_shapes=scratch)
    core = lax.axis_index("c").astype(jnp.int32).reshape(1)
    aliases = {} if land is None else {1 + 2 * nk + n_side: 0}
    res = _pcall(
        body, name="pair_reduce_" + keys[0], grid_spec=grid_spec, out_shape=tuple(out_shape),
        input_output_aliases=aliases, compiler_params=_cparams(("arbitrary",)),
    )(core, *grads, *grads, *(side or ()), *(() if land is None else (land,)))
    return res[0] if len(res) == 1 else res


def _chip_start(psum):
    hbm = pl.BlockSpec(memory_space=pltpu.HBM)
    sem = pl.BlockSpec(memory_space=pltpu.SEMAPHORE)

    def body(p_ref, land_ref, s1, s2, s3, r1, r2, r3, p_thru, land_thru, token):
        x, y, c = _mesh_pos()
        for j, (send_sem, recv_sem) in enumerate(((s1, r1), (s2, r2), (s3, r3)), start=1):
            tx, ty = (x + (j >> 1)) % 2, (y + (j & 1)) % 2
            pltpu.make_async_remote_copy(
                src_ref=p_ref.at[2 * tx + ty], dst_ref=land_ref.at[2 * x + y], send_sem=send_sem, recv_sem=recv_sem,
                device_id=(tx, ty, c), device_id_type=MESH).start()
        token[...] = jnp.zeros_like(token)

    dma = pltpu.SemaphoreType.DMA(())
    buf = pltpu.HBM(psum.shape, psum.dtype)
    res = _pcall(
        body, name="chip_exchange_start",
        out_shape=(dma,) * 6 + (buf, buf, _sds((8, 128), F32)),
        in_specs=(hbm, hbm), out_specs=(sem,) * 6 + (hbm, hbm, pl.BlockSpec(memory_space=pltpu.VMEM)),
        input_output_aliases={0: 6, 1: 7},
        compiler_params=pltpu.CompilerParams(has_side_effects=pltpu.SideEffectType.DATAFLOW_SIDE_EFFECTING),
    )(pltpu.with_memory_space_constraint(psum, pltpu.HBM),
      pltpu.with_memory_space_constraint(lax.empty(psum.shape, psum.dtype), pltpu.HBM))
    return res[:6], res[6], res[7], res[8]


def _chip_wait(sems, p_thru, land_thru, after):
    hbm = pl.BlockSpec(memory_space=pltpu.HBM)
    sem = pl.BlockSpec(memory_space=pltpu.SEMAPHORE)

    def body(p_ref, land_ref, s1, s2, s3, r1, r2, r3, after_ref, p_dead, got_ref):
        x, y, c = _mesh_pos()
        for send_sem, recv_sem in ((s1, r1), (s2, r2), (s3, r3)):
            copy = pltpu.make_async_remote_copy(
                src_ref=p_ref.at[0], dst_ref=land_ref.at[0], send_sem=send_sem, recv_sem=recv_sem,
                device_id=(x, y, 1 - c), device_id_type=MESH)
            copy.wait_send()
            copy.wait_recv()

    buf = pltpu.HBM(p_thru.shape, p_thru.dtype)
    return _pcall(
        body, name="chip_exchange_wait", out_shape=(buf, buf),
        in_specs=(hbm, hbm) + (sem,) * 6 + (_ANY,), out_specs=(hbm, hbm),
        input_output_aliases={0: 0, 1: 1},
        compiler_params=pltpu.CompilerParams(has_side_effects=pltpu.SideEffectType.DATAFLOW_SIDE_EFFECTING),
    )(p_thru, land_thru, *sems, after)


def _own_slot(land, psum):
    def body(land_in, p_ref, land_ref, sem):
        x, y, _ = _mesh_pos()
        cp = pltpu.make_async_copy(p_ref.at[2 * x + y], land_ref.at[2 * x + y], sem)
        cp.start()
        cp.wait()

    return _pcall(
        body, name="own_slot", out_shape=_sds(land.shape, land.dtype),
        in_specs=[_ANY, _ANY], out_specs=_ANY, input_output_aliases={0: 0},
        scratch_shapes=[pltpu.SemaphoreType.DMA],
    )(land, psum)


def _mixa_fwd(x, gam, cw, wg, tm, comm):
    T = x.shape[0]

    def body(x_ref, gam_ref, cw_ref, wg_ref, h_ref, b_ref, c_ref, v_ref, cc_ref, y_ref,
             win, wout, buf, sems):
        first = pl.program_id(0) == 0
        load_w = _weight_loads(wg_ref, [(win, "a_in"), (wout, "a_out")], sems)

        @pl.when(first)
        def _():
            load_w()
            buf[pl.ds(0, HALO_A), :] = jnp.zeros((HALO_A, D), F32)

        xv = x_ref[...]
        nb = (xv * _rms_stat(xv) * gam_ref[...]).astype(BF16)
        bv = _dot_nt(nb, win[pl.ds(0, D), :])
        cval = _dot_nt(nb, win[pl.ds(D, D), :])
        vval = _dot_nt(nb, win[pl.ds(2 * D, D), :])
        cv = cval * vval
        buf[pl.ds(HALO_A, tm), :] = cv
        cc = cw_ref[pl.ds(KA - 1, 1), :] * cv
        for k in range(KA - 1):
            cc = cc + cw_ref[pl.ds(k, 1), :] * buf[pl.ds(HALO_A - (KA - 1) + k, tm), :]
        buf[pl.ds(0, HALO_A), :] = buf[pl.ds(tm, HALO_A), :]
        yb = (bv * cc).astype(BF16)
        b_ref[...] = bv.astype(BF16)
        c_ref[...] = cval.astype(BF16)
        v_ref[...] = vval.astype(BF16)
        cc_ref[...] = cc.astype(BF16)
        y_ref[...] = yb
        h_ref[...] = xv + _dot(yb, wout[...])

    act = _sds((T, D), BF16)
    return _hosted_call(
        body, "mixa_fwd", T // tm,
        in_specs=[_row_spec(tm, D), _full_spec((1, D)), _full_spec((KA, D))],
        out_specs=[_row_spec(tm, D) for _ in range(6)],
        out_shape=(_sds((T, D), F32),) + (act,) * 5,
        scratch=[pltpu.VMEM((3 * D, D), BF16), pltpu.VMEM((D, D), BF16),
                 pltpu.VMEM((HALO_A + tm, D), F32), pltpu.SemaphoreType.DMA((2 * NDEV,))],
        args=(x, gam, cw), comm=comm)


def _ffn_fwd(h, gam, wg, layer, tm, comm, head=None):
    T = h.shape[0]
    kg, ku, kd = "g%d" % layer, "u%d" % layer, "d%d" % layer
    n_head = 0 if head is None else 2

    def body(*refs):
        h_ref, gam_ref = refs[:2]
        wg_ref = refs[2 + n_head]
        o_ref, n_ref, g_ref, u_ref, gu_ref = refs[3 + n_head:8 + n_head]
        wgt, wut, wd, sems = refs[-4:]

        first = pl.program_id(0) == 0
        load_w = _weight_loads(wg_ref, [(wgt, kg), (wut, ku), (wd, kd)], sems)
        pl.when(first)(load_w)

        hv = h_ref[...]
        nb = (hv * _rms_stat(hv) * gam_ref[...]).astype(BF16)
        n_ref[...] = nb
        out = hv
        for f in range(2):
            cols = pl.ds(f * FH, FH)
            g = _dot_nt(nb, wgt[cols, :])
            u = _dot_nt(nb, wut[cols, :])
            gu = (g * _sigmoid(g) * u).astype(BF16)
            g_ref[:, cols] = g.astype(BF16)
            u_ref[:, cols] = u.astype(BF16)
            gu_ref[:, cols] = gu
            out = out + _dot(gu, wd[cols, :])
        if head is None:
            o_ref[...] = out
        else:
            t_ref, fin_ref, st_ref = refs[2], refs[3], refs[8 + n_head]

            @pl.when(pl.program_id(0) == 0)
            def _():
                st_ref[...] = jnp.zeros((8, D), F32)

            gamma = fin_ref[...]
            r = _rms_stat(out)
            err = out * r * gamma - t_ref[...]
            dx, dgam = _rms_bwd(err * (1.0 / D), out, r, gamma)
            o_ref[...] = dx
            st_ref[pl.ds(0, 1), :] += dgam
            st_ref[pl.ds(1, 1), :] += (0.5 / D) * jnp.sum(err * err, axis=0, keepdims=True)

    pre = _sds((T, FF), BF16)
    in_specs = [_row_spec(tm, D), _full_spec((1, D))]
    args = (h, gam)
    out_specs = [_row_spec(tm, D), _row_spec(tm, D), _row_spec(tm, FF), _row_spec(tm, FF), _row_spec(tm, FF)]
    out_shape = (_sds((T, D), F32), _sds((T, D), BF16), pre, pre, pre)
    if head is not None:
        in_specs, args = in_specs + [_row_spec(tm, D), _full_spec((1, D))], args + tuple(head)
        out_specs, out_shape = out_specs + [_full_spec((8, D))], out_shape + (_sds((8, D), F32),)
    if comm is None:
        in_specs, args = in_specs + [_ANY], args + (wg,)
    return _hosted_call(
        body, "ffn%d_fwd" % layer, T // tm,
        in_specs=in_specs, out_specs=out_specs, out_shape=out_shape,
        scratch=[pltpu.VMEM((FF, D), BF16)] * 3 + [pltpu.SemaphoreType.DMA((3 * NDEV,))],
        args=args, comm=comm)


def _shifted_copies(buf, shf, tm):
    for r in range(1, 8):
        shf[r - 1] = buf[pl.ds(r, tm + HALO_B - 8), :]


def _broadcast_taps(vec_ref, wb):
    for k in range(KB):
        wb[k] = jnp.broadcast_to(vec_ref[pl.ds(SM_B_CONV + k, 1), :], (8, D))
    wb[KB] = jnp.broadcast_to(vec_ref[pl.ds(SM_B_BCONV, 1), :], (8, D))


def _taps_by_shift_residue(taps):
    groups = {}
    for k, shift in taps:
        q, r = divmod(shift, 8)
        groups.setdefault(r, []).append((k, q))
    return sorted(groups.items())


def _window(buf, shf, base, r, q0, n_groups, lanes):
    rows = pl.ds(base + 8 * q0, 8 * n_groups)
    v = buf[rows, lanes] if r == 0 else shf[r - 1, rows, lanes]
    return [v[8 * i:8 * i + 8] for i in range(n_groups)]


def _long_conv(buf, shf, wb, out_ref, tm, taps, bias_row):
    n_acc = CONV_ROWS // 8
    groups = _taps_by_shift_residue(taps)
    for col in range(D // 128):
        lanes = pl.ds(128 * col, 128)

        def rows(i, carry, lanes=lanes):
            base = i * CONV_ROWS
            init = jnp.zeros((8, 128), F32) if bias_row is None else wb[bias_row, :, lanes]
            accs = [init] * n_acc
            for r, lst in groups:
                q0, q1 = min(q for _, q in lst), max(q for _, q in lst)
                win = _window(buf, shf, base, r, q0, n_acc + q1 - q0, lanes)
                for k, q in lst:
                    wk = wb[k, :, lanes]
                    accs = [acc + wk * win[h + q - q0] for h, acc in enumerate(accs)]
            for h, acc in enumerate(accs):
                out_ref[pl.ds(base + 8 * h, 8), lanes] = acc
            return carry

        for i in range(tm // CONV_ROWS):
            rows(i, 0)


def _long_conv_grad_taps(buf, shf, x_ref, wacc, tm):
    n_acc = CONV_ROWS // 8
    groups = _taps_by_shift_residue([(KB - 1 - j, j) for j in range(KB)])
    for col in range(D // 128):
        lanes = pl.ds(128 * col, 128)

        def rows(i, carry, lanes=lanes):
            base = i * CONV_ROWS
            xv = x_ref[pl.ds(base, CONV_ROWS), lanes]
            xs = [xv[8 * h:8 * h + 8] for h in range(n_acc)]
            for r, lst in groups:
                q0, q1 = min(q for _, q in lst), max(q for _, q in lst)
                win = _window(buf, shf, base, r, q0, n_acc + q1 - q0, lanes)
                for k, q in lst:
                    prod = [x * win[h + q - q0] for h, x in enumerate(xs)]
                    wacc[k, :, lanes] += functools.reduce(lambda a, b: a + b, prod)
            return carry

        for i in range(tm // CONV_ROWS):
            rows(i, 0)


def _ln_stats(dc):
    mu = jnp.mean(dc, axis=-1, keepdims=True)
    xc = dc - mu
    rstd = lax.rsqrt(jnp.mean(xc * xc, axis=-1, keepdims=True) + LN_EPS)
    return xc * rstd, rstd


def _mixb_fwd(h, vecs, bias1, wg, tm, comm):
    T = h.shape[0]

    def body(h_ref, vec_ref, b1_ref, wg_ref, o_ref, a_ref, g_ref, dc_ref, s_ref, w1, w2, buf, shf, wb, sems):
        first = pl.program_id(0) == 0
        load_w = _weight_loads(wg_ref, [(w1, "b_pw1"), (w2, "b_pw2")], sems)

        @pl.when(first)
        def _():
            load_w()
            buf[pl.ds(0, HALO_B), :] = jnp.zeros((HALO_B, D), F32)
            _broadcast_taps(vec_ref, wb)

        hv = h_ref[...]
        nb = (hv * _rms_stat(hv) * vec_ref[pl.ds(SM_B_NORM, 1), :]).astype(BF16)
        a = _dot_nt(nb, w1[pl.ds(0, D), :]) + b1_ref[:, pl.ds(0, D)]
        g = _dot_nt(nb, w1[pl.ds(D, D), :]) + b1_ref[:, pl.ds(D, D)]
        a_ref[...] = a.astype(BF16)
        g_ref[...] = g.astype(BF16)
        buf[pl.ds(HALO_B, tm), :] = a * _sigmoid(g)
        _shifted_copies(buf, shf, tm)

        _long_conv(buf, shf, wb, dc_ref, tm, [(k, HALO_B - (KB - 1) + k) for k in range(KB)], KB)
        buf[pl.ds(0, HALO_B), :] = buf[pl.ds(tm, HALO_B), :]
        xhat, _ = _ln_stats(dc_ref[...])
        ln = xhat * vec_ref[pl.ds(SM_LN_G, 1), :] + vec_ref[pl.ds(SM_LN_B, 1), :]
        s = (ln * _sigmoid(ln)).astype(BF16)
        s_ref[...] = s
        o_ref[...] = hv + _dot(s, w2[...]) + vec_ref[pl.ds(SM_B_PW2, 1), :]

    act = _sds((T, D), BF16)
    return _hosted_call(
        body, "mixb_fwd", T // tm,
        in_specs=[_row_spec(tm, D), _full_spec((SM_F32_ROWS, D)), _full_spec((1, 2 * D))],
        out_specs=[_row_spec(tm, D) for _ in range(5)],
        out_shape=(_sds((T, D), F32), act, act, _sds((T, D), F32), act),
        scratch=[pltpu.VMEM((2 * D, D), BF16), pltpu.VMEM((D, D), BF16),
                 pltpu.VMEM((HALO_B + tm, D), F32), pltpu.VMEM((7, HALO_B + tm - 8, D), F32),
                 pltpu.VMEM((KB + 1, 8, D), F32), pltpu.SemaphoreType.DMA((2 * NDEV,))],
        args=(h, vecs, bias1), comm=comm)


def _ffn_bwd_dx(dh, h, g, u, gam, wg, layer, tm, comm):
    T = h.shape[0]
    kg, ku, kd = "g%d" % layer, "u%d" % layer, "d%d" % layer

    def body(dh_ref, h_ref, g_ref, u_ref, gam_ref, wg_ref, o_ref, dg_ref, du_ref, st_ref, wgt, wut, wd, sems):
        first = pl.program_id(0) == 0
        load_w = _weight_loads(wg_ref, [(wd, kd), (wgt, kg), (wut, ku)], sems)

        @pl.when(first)
        def _():
            load_w()
            st_ref[...] = jnp.zeros((8, D), F32)

        dhv = dh_ref[...]
        dhb = dhv.astype(BF16)
        dn = jnp.zeros_like(dhv)
        for f in range(2):
            cols = pl.ds(f * FH, FH)
            dgu = _dot_nt(dhb, wd[cols, :])
            gv = g_ref[:, cols].astype(F32)
            sg = _sigmoid(gv)
            du = (dgu * gv * sg).astype(BF16)
            dg = (dgu * u_ref[:, cols].astype(F32) * (sg * (1.0 + gv * (1.0 - sg)))).astype(BF16)
            dg_ref[:, cols] = dg
            du_ref[:, cols] = du
            dn = dn + _dot(dg, wgt[cols, :]) + _dot(du, wut[cols, :])
        hv = h_ref[...]
        dx, dgam = _rms_bwd(dn, hv, _rms_stat(hv), gam_ref[...])
        o_ref[...] = dhv + dx
        st_ref[pl.ds(0, 1), :] += dgam

    pre = _sds((T, FF), BF16)
    return _hosted_call(
        body, "ffn%d_bwd_dx" % layer, T // tm,
        in_specs=[_row_spec(tm, D), _row_spec(tm, D), _row_spec(tm, FF), _row_spec(tm, FF),
                  _full_spec((1, D)), _ANY],
        out_specs=[_row_spec(tm, D), _row_spec(tm, FF), _row_spec(tm, FF), _full_spec((8, D))],
        out_shape=(_sds((T, D), F32), pre, pre, _sds((8, D), F32)),
        scratch=[pltpu.VMEM((FF, D), BF16)] * 3 + [pltpu.SemaphoreType.DMA((3 * NDEV,))],
        args=(dh, h, g, u, gam, wg), comm=comm)


def _grad_w(lhs, rhs, mc, name, tm):
    T, M = lhs.shape
    nt = T // tm

    def body(l_ref, r_ref, o_ref, acc):
        i = pl.program_id(1)

        @pl.when(i == 0)
        def _():
            acc[...] = jnp.zeros((mc, D), F32)

        acc[...] += _dot_tn(l_ref[...], r_ref[...].astype(BF16))

        @pl.when(i == nt - 1)
        def _():
            o_ref[...] = acc[...].astype(BF16)

    return _pcall(
        body, name=name, grid=(M // mc, nt),
        out_shape=_sds((M, D), BF16),
        in_specs=[pl.BlockSpec((tm, mc), lambda j, i: (i, j)), pl.BlockSpec((tm, D), lambda j, i: (i, 0))],
        out_specs=pl.BlockSpec((mc, D), lambda j, i: (j, 0)),
        scratch_shapes=[pltpu.VMEM((mc, D), F32)],
        compiler_params=_cparams(("arbitrary", "arbitrary")),
    )(lhs, rhs)


def _mixb_bwd(dh, h, a, g, dc, s, vecs, wg, tm, comm):
    T = h.shape[0]
    nt = T // tm

    def body(dh_ref, h_ref, a_ref, g_ref, dc_ref, s_ref, vec_ref, wg_ref, o_ref, st_ref, sb_ref, g1_ref, g2_ref,
             w1, w2, buf, shf, glu_s, dglu_s, wacc, wb, acc1, acc2, stage, sems, out_sems):
        first = pl.program_id(0) == 0
        load_w = _weight_loads(wg_ref, [(w2, "b_pw2"), (w1, "b_pw1")], sems)

        @pl.when(first)
        def _():
            load_w()
            buf[pl.ds(tm, HALO_B), :] = jnp.zeros((HALO_B, D), F32)
            wacc[...] = jnp.zeros((KB, 8, D), F32)
            _broadcast_taps(vec_ref, wb)
            st_ref[...] = jnp.zeros((SM_F32_ROWS, D), F32)
            sb_ref[...] = jnp.zeros((8, 2 * D), F32)
            acc1[...] = jnp.zeros((2 * D, D), F32)
            acc2[...] = jnp.zeros((D, D), F32)

        def acc(row, val):
            st_ref[pl.ds(row, 1), :] += jnp.sum(val, axis=0, keepdims=True)

        dhv = dh_ref[...]
        dhb = dhv.astype(BF16)
        acc(SM_B_PW2, dhv)
        acc2[...] += _dot_tn(s_ref[...], dhb)
        ds = _dot_nt(dhb, w2[...])
        xhat, rstd = _ln_stats(dc_ref[...])
        ln_g = vec_ref[pl.ds(SM_LN_G, 1), :]
        ln = xhat * ln_g + vec_ref[pl.ds(SM_LN_B, 1), :]
        sl = _sigmoid(ln)
        dln = ds * (sl * (1.0 + ln * (1.0 - sl)))
        acc(SM_LN_G, dln * xhat)
        acc(SM_LN_B, dln)
        dxh = dln * ln_g
        ddc = rstd * (dxh - jnp.mean(dxh, axis=-1, keepdims=True)
                      - xhat * jnp.mean(dxh * xhat, axis=-1, keepdims=True))
        acc(SM_B_BCONV, ddc)
        buf[pl.ds(0, tm), :] = ddc
        _shifted_copies(buf, shf, tm)
        av = a_ref[...].astype(F32)
        sg = _sigmoid(g_ref[...].astype(F32))
        glu_s[...] = av * sg

        _long_conv(buf, shf, wb, dglu_s, tm, [(KB - 1 - j, j) for j in range(KB)], None)
        _long_conv_grad_taps(buf, shf, glu_s, wacc, tm)
        buf[pl.ds(tm, HALO_B), :] = buf[pl.ds(0, HALO_B), :]
        dglu = dglu_s[...]
        da = dglu * sg
        dg = dglu * av * sg * (1.0 - sg)
        sb_ref[pl.ds(0, 1), pl.ds(0, D)] += jnp.sum(da, axis=0, keepdims=True)
        sb_ref[pl.ds(0, 1), pl.ds(D, D)] += jnp.sum(dg, axis=0, keepdims=True)
        dab, dgb = da.astype(BF16), dg.astype(BF16)
        dn = _dot(dab, w1[pl.ds(0, D), :]) + _dot(dgb, w1[pl.ds(D, D), :])
        hv = h_ref[...]
        r = _rms_stat(hv)
        gamma = vec_ref[pl.ds(SM_B_NORM, 1), :]
        nb = (hv * r * gamma).astype(BF16)
        acc1[pl.ds(0, D), :] += _dot_tn(dab, nb)
        acc1[pl.ds(D, D), :] += _dot_tn(dgb, nb)
        dx, dgam = _rms_bwd(dn, hv, r, gamma)
        o_ref[...] = dhv + dx
        st_ref[pl.ds(SM_B_NORM, 1), :] += dgam

        @pl.when(pl.program_id(0) == nt - 1)
        def _():
            st_ref[pl.ds(SM_B_CONV, KB), :] = jnp.sum(wacc[...], axis=1)
            blocks = [(acc1, 0, g1_ref, 0), (acc1, D, g1_ref, D), (acc2, 0, g2_ref, 0)]
            for k, (src, a0, dst, d0) in enumerate(blocks):
                stage[...] = src[pl.ds(a0, D), :].astype(BF16)
                cp = pltpu.make_async_copy(stage, dst.at[pl.ds(d0, D), :], out_sems.at[k])
                cp.start()
                cp.wait()

    rs = functools.partial(_row_spec, rev_nt=nt)
    return _hosted_call(
        body, "mixb_bwd", nt,
        in_specs=[rs(tm, D) for _ in range(6)] + [_full_spec((SM_F32_ROWS, D)), _ANY],
        out_specs=[rs(tm, D), _full_spec((SM_F32_ROWS, D)), _full_spec((8, 2 * D)), _ANY, _ANY],
        out_shape=(_sds((T, D), F32), _sds((SM_F32_ROWS, D), F32), _sds((8, 2 * D), F32),
                   _sds((2 * D, D), BF16), _sds((D, D), BF16)),
        scratch=[pltpu.VMEM((2 * D, D), BF16), pltpu.VMEM((D, D), BF16),
                 pltpu.VMEM((tm + HALO_B, D), F32), pltpu.VMEM((7, tm + HALO_B - 8, D), F32),
                 pltpu.VMEM((tm, D), F32), pltpu.VMEM((tm, D), F32), pltpu.VMEM((KB, 8, D), F32),
                 pltpu.VMEM((KB + 1, 8, D), F32), pltpu.VMEM((2 * D, D), F32), pltpu.VMEM((D, D), F32),
                 pltpu.VMEM((D, D), BF16), pltpu.SemaphoreType.DMA((2 * NDEV,)), pltpu.SemaphoreType.DMA((3,))],
        args=(dh, h, a, g, dc, s, vecs, wg), comm=comm)


def _mixa_bwd(dh, x, b, c, v, cc, y, gam, cw, wg, tm, comm):
    T = x.shape[0]
    nt = T // tm

    def body(dh_ref, x_ref, b_ref, c_ref, v_ref, cc_ref, y_ref, gam_ref, cw_ref, wg_ref,
             o_ref, st_ref, gin_ref, gout_ref, win, wout, buf, acc_in, acc_out, stage, sems, out_sems):
        first = pl.program_id(0) == 0
        load_w = _weight_loads(wg_ref, [(wout, "a_out"), (win, "a_in")], sems)

        @pl.when(first)
        def _():
            load_w()
            buf[pl.ds(tm, HALO_A), :] = jnp.zeros((HALO_A, D), F32)
            st_ref[...] = jnp.zeros((8, D), F32)
            acc_in[...] = jnp.zeros((3 * D, D), F32)
            acc_out[...] = jnp.zeros((D, D), F32)

        dhv = dh_ref[...]
        dhb = dhv.astype(BF16)
        dy = _dot_nt(dhb, wout[...])
        cval = c_ref[...].astype(F32)
        vval = v_ref[...].astype(F32)
        d_b = (dy * cc_ref[...].astype(F32)).astype(BF16)
        buf[pl.ds(0, tm), :] = dy * b_ref[...].astype(F32)
        cv = cval * vval
        dcv = jnp.zeros((tm, D), F32)
        for j in range(KA):
            sh = buf[pl.ds(j, tm), :]
            k = KA - 1 - j
            dcv = dcv + cw_ref[pl.ds(k, 1), :] * sh
            st_ref[pl.ds(1 + k, 1), :] += jnp.sum(cv * sh, axis=0, keepdims=True)
        buf[pl.ds(tm, HALO_A), :] = buf[pl.ds(0, HALO_A), :]
        d_c = (dcv * vval).astype(BF16)
        d_v = (dcv * cval).astype(BF16)
        xv = x_ref[...]
        r = _rms_stat(xv)
        gamma = gam_ref[...]
        nb = (xv * r * gamma).astype(BF16)
        for j, piece in enumerate((d_b, d_c, d_v)):
            acc_in[pl.ds(j * D, D), :] += _dot_tn(piece, nb)
        acc_out[...] += _dot_tn(y_ref[...], dhb)
        dn = _dot(d_b, win[pl.ds(0, D), :]) + _dot(d_c, win[pl.ds(D, D), :]) + _dot(d_v, win[pl.ds(2 * D, D), :])
        dx, dgam = _rms_bwd(dn, xv, r, gamma)
        o_ref[...] = dhv + dx
        st_ref[pl.ds(0, 1), :] += dgam

        @pl.when(pl.program_id(0) == nt - 1)
        def _():
            blocks = [(acc_in, j * D, gin_ref, j * D) for j in range(3)] + [(acc_out, 0, gout_ref, 0)]
            for k, (acc, a0, dst, d0) in enumerate(blocks):
                stage[...] = acc[pl.ds(a0, D), :].astype(BF16)
                cp = pltpu.make_async_copy(stage, dst.at[pl.ds(d0, D), :], out_sems.at[k])
                cp.start()
                cp.wait()

    rs = functools.partial(_row_spec, rev_nt=nt)
    return _hosted_call(
        body, "mixa_bwd", nt,
        in_specs=[rs(tm, D) for _ in range(7)] + [_full_spec((1, D)), _full_spec((KA, D)), _ANY],
        out_specs=[rs(tm, D), _full_spec((8, D)), _ANY, _ANY],
        out_shape=(_sds((T, D), F32), _sds((8, D), F32), _sds((3 * D, D), BF16), _sds((D, D), BF16)),
        scratch=[pltpu.VMEM((3 * D, D), BF16), pltpu.VMEM((D, D), BF16), pltpu.VMEM((tm + HALO_A, D), F32),
                 pltpu.VMEM((3 * D, D), F32), pltpu.VMEM((D, D), F32), pltpu.VMEM((D, D), BF16),
                 pltpu.SemaphoreType.DMA((2 * NDEV,)), pltpu.SemaphoreType.DMA((4,))],
        args=(dh, x, b, c, v, cc, y, gam, cw, wg), comm=comm)


def _sum_small(sm, rp):
    def body(sm_ref, rp_ref, osm, orp, oloss):
        a = sm_ref[0]
        b = rp_ref[0]
        for s in range(1, NDEV):
            a = a + sm_ref[s]
            b = b + rp_ref[s]
        osm[...] = a
        orp[...] = b
        oloss[...] = jnp.zeros((8, 128), F32) + jnp.sum(b[4:5, :], axis=-1, keepdims=True)

    return _pcall(
        body, name="sum_small_grads",
        out_shape=(_sds((SM_F32_ROWS, 128), F32), _sds((8, D), F32), _sds((8, 128), F32)),
        in_specs=[pl.BlockSpec(memory_space=pltpu.VMEM)] * 2,
        out_specs=tuple(pl.BlockSpec(memory_space=pltpu.VMEM) for _ in range(3)),
        compiler_params=_cparams(),
    )(sm, rp)


def _adam_math(w, g, m, v):
    mn = B1 * m + (1.0 - B1) * g
    vn = B2 * v + (1.0 - B2) * (g * g)
    m_hat = mn / (1.0 - B1 ** STEP)
    v_hat = vn / (1.0 - B2 ** STEP)
    return -LR * (m_hat / (jnp.sqrt(v_hat) + ADAM_EPS) + WD * w), mn, vn


def _finish_weight(land, keys, transposed, w, m, v, name, row0=0, after=None):
    layers, rows, cols = w.shape
    n = N_ROWS[keys[0]]
    n_pad = -(-n // 128) * 128 if transposed else n

    def body(w_ref, m_ref, v_ref, land_ref, *rest):
        og, od, om, ov, buf, wv, mv, vv, sg, sd, sm, sv, sems = rest[-13:]
        for l in range(layers):
            off = LD_OFF[keys[l]] - row0
            loads = [pltpu.make_async_copy(land_ref.at[:, pl.ds(off, n), :], buf, sems.at[0])]
            loads += [pltpu.make_async_copy(src.at[l], dst, sems.at[1 + i])
                      for i, (src, dst) in enumerate(((w_ref, wv), (m_ref, mv), (v_ref, vv)))]
            for cp in loads:
                cp.start(priority=LOCAL_THREAD)
            for cp in loads:
                cp.wait()
            g = buf[0].astype(F32)
            for s in range(1, NCHIP):
                g = g + buf[s].astype(F32)
            if transposed:
                if n_pad != n:
                    g = jnp.concatenate([g, jnp.zeros((n_pad - n, D), F32)], axis=0)
                g = g.T[:, :n]
            sg[...] = g
            sd[...], sm[...], sv[...] = _adam_math(wv[...], g, mv[...], vv[...])
            stores = [pltpu.make_async_copy(src, dst.at[l], sems.at[4 + i])
                      for i, (src, dst) in enumerate(((sg, og), (sd, od), (sm, om), (sv, ov)))]
            for cp in stores:
                cp.start(priority=LOCAL_THREAD)
            for cp in stores:
                cp.wait()

    shp = _sds(w.shape, F32)
    tile = pltpu.VMEM((rows, cols), F32)
    args = (w, m, v, land) + (() if after is None else (after,))
    return _pcall(
        body, name="finish_" + name, out_shape=(shp,) * 4,
        in_specs=[_ANY] * len(args), out_specs=(_ANY,) * 4,
        scratch_shapes=[pltpu.VMEM((NCHIP, n, D), BF16)] + [tile] * 7 + [pltpu.SemaphoreType.DMA((8,))],
        compiler_params=_cparams(),
    )(*args)


SMALL_PARAMS = (("sm", SM_B_NORM, 1), ("sm", SM_B_PW1, 2), ("sm", SM_B_CONV, KB), ("sm", SM_B_BCONV, 1),
                ("sm", SM_LN_G, 1), ("sm", SM_LN_B, 1), ("sm", SM_B_PW2, 1), ("sm", SM_A_CONV, KA),
                ("rp", 0, 1), ("rp", 1, 2), ("rp", 3, 1))


def _adamw_small(g_small, g_repl, triples):
    n = len(SMALL_PARAMS)

    def body(*refs):
        gs_ref, gr_ref = refs[0], refs[1]
        ins = refs[2:2 + 3 * n]
        outs = refs[2 + 3 * n:]
        for i, (src, r0, nr) in enumerate(SMALL_PARAMS):
            w_ref, m_ref, v_ref = ins[3 * i:3 * i + 3]
            g = (gs_ref if src == "sm" else gr_ref)[pl.ds(r0, nr), :]
            if i == 1:
                g = jnp.concatenate([g[0:1], g[1:2]], axis=1)
            lead = (0,) if len(w_ref.shape) == 3 else ()
            idx = lead + (slice(None), slice(None))
            vals = (g,) + _adam_math(w_ref[idx], g, m_ref[idx], v_ref[idx])
            for o_ref, val in zip(outs[4 * i:4 * i + 4], vals):
                o_ref[idx] = val

    flat = [a for t in triples for a in t]
    out_shape = tuple(_sds(t[0].shape, F32) for t in triples for _ in range(4))
    vm = pl.BlockSpec(memory_space=pltpu.VMEM)
    res = _pcall(
        body, name="adamw_small", out_shape=out_shape,
        in_specs=[vm] * (2 + len(flat)), out_specs=tuple(vm for _ in out_shape),
        compiler_params=_cparams(),
    )(g_small, g_repl, *flat)
    return [tuple(res[4 * i:4 * i + 4]) for i in range(n)]


def kernel(x, a_norm, a_w_in, a_conv, a_w_out, b_norm, b_w_pw1, b_b_pw1, b_conv, b_b_conv, b_ln_g, b_ln_b, b_w_pw2, b_b_pw2, ffn_norm, ffn_w_gate, ffn_w_up, ffn_w_down, final_norm, loss_target, m_a_norm, m_a_w_in, m_a_conv, m_a_w_out, m_b_norm, m_b_w_pw1, m_b_b_pw1, m_b_conv, m_b_b_conv, m_b_ln_g, m_b_ln_b, m_b_w_pw2, m_b_b_pw2, m_ffn_norm, m_ffn_w_gate, m_ffn_w_up, m_ffn_w_down, m_final_norm, v_a_norm, v_a_w_in, v_a_conv, v_a_w_out, v_b_norm, v_b_w_pw1, v_b_b_pw1, v_b_conv, v_b_b_conv, v_b_ln_g, v_b_ln_b, v_b_w_pw2, v_b_b_pw2, v_ffn_norm, v_ffn_w_gate, v_ffn_w_up, v_ffn_w_down, v_final_norm):
    T = x.shape[1]
    tm = min(TM, T)
    tma = min(TM_A, T)
    tw = min(TM_DW, T)
    xs = x.reshape(T, D)
    tgt = loss_target.reshape(T, D)

    def rows_first(a):
        return jnp.swapaxes(a, 1, 2)

    shard, small = _pack_shard(
        (a_w_in, a_w_out, b_w_pw1, b_w_pw2, rows_first(ffn_w_gate), rows_first(ffn_w_up), ffn_w_down),
        (b_norm, b_b_pw1, b_conv, b_b_conv, b_ln_g, b_ln_b, b_b_pw2, a_conv))
    wg, sm_all = _all_gather_first(shard, small)
    vecs = sm_all.transpose(1, 0, 2).reshape(SM_F32_ROWS, D)
    bias1 = sm_all[:, SM_B_PW1:SM_B_PW1 + 2, :].reshape(1, 2 * D)
    cw_a = vecs[SM_A_CONV:SM_A_CONV + KA]
    fn0, fn1 = ffn_norm[0:1], ffn_norm[1:2]
    fin = final_norm.reshape(1, D)

    h1, bq, cq, vq, ccq, yq, wg = _mixa_fwd(xs, a_norm, cw_a, wg, tma, _ag_comm(shard, wg, ("g0", "u0", "d0")))
    h2, n1, g0, u0, gu0, wg = _ffn_fwd(h1, fn0, wg, 0, tma, _ag_comm(shard, wg, ("b_pw1", "b_pw2"), (T // tma) // 2))
    h3, aq, gq, dcq, sq, wg = _mixb_fwd(h2, vecs, bias1, wg, tm, _ag_comm(shard, wg, ("g1", "u1", "d1"), 1))
    dh4, n3, g1, u1, gu1, st_fin = _ffn_fwd(h3, fn1, wg, 1, tma, None, head=(tgt, fin))

    def by_dest(gw, key):
        return gw.reshape(NDEV, N_ROWS[key], D)

    gw_d1 = by_dest(_grad_w(gu1, dh4, FH, "grad_down1", tw), "d1")
    dh3, dg1, du1, st_f1 = _ffn_bwd_dx(dh4, h3, g1, u1, fn1, wg, 1, tma, None)
    gw_g1 = by_dest(_grad_w(dg1, n3, FH, "grad_gate1", tw), "g1")
    gw_u1 = by_dest(_grad_w(du1, n3, FH, "grad_up1", tw), "u1")
    keys = ("g1", "u1", "d1")
    dh2, st_b, st_b1, gw_pw1, gw_pw2, land = _mixb_bwd(
        dh3, h2, aq, gq, dcq, sq, vecs, wg, tm, _rs_comm(_pair_reduce([gw_g1, gw_u1, gw_d1], keys), keys, None))
    gw_pw1, gw_pw2 = by_dest(gw_pw1, "b_pw1"), by_dest(gw_pw2, "b_pw2")
    gw_d0 = by_dest(_grad_w(gu0, dh2, FH, "grad_down0", tw), "d0")
    keys = ("b_pw1", "b_pw2", "d0")
    dh1, dg0, du0, st_f0, land = _ffn_bwd_dx(
        dh2, h1, g0, u0, fn0, wg, 0, tma, _rs_comm(_pair_reduce([gw_pw1, gw_pw2, gw_d0], keys), keys, land))
    gw_g0 = by_dest(_grad_w(dg0, n1, FH, "grad_gate0", tw), "g0")
    gw_u0 = by_dest(_grad_w(du0, n1, FH, "grad_up0", tw), "u0")
    keys = ("g0", "u0")
    dx, st_a, gw_in, gw_out, land = _mixa_bwd(
        dh1, xs, bq, cq, vq, ccq, yq, a_norm, cw_a, wg, tm,
        _rs_comm(_pair_reduce([gw_g0, gw_u0], keys), keys, land))
    gw_in, gw_out = by_dest(gw_in, "a_in"), by_dest(gw_out, "a_out")

    st_small = st_b.at[SM_A_CONV:SM_A_CONV + KA].set(st_a[1:1 + KA])
    sm_dest = st_small.reshape(SM_F32_ROWS, NDEV, 128).transpose(1, 0, 2)
    sm_dest = sm_dest.at[:, SM_B_PW1:SM_B_PW1 + 2, :].set(st_b1[0].reshape(NDEV, 2, 128))
    repl = jnp.concatenate([st_a[0:1], st_f0[0:1], st_f1[0:1], st_fin[0:1], st_fin[1:2],
                            jnp.zeros((3, D), F32)], axis=0)[None]
    last = ("a_out", "a_in")
    p_last, sm_land, rp_land = _pair_reduce([gw_out, gw_in], last, side=(sm_dest, repl))
    sems, p_last, land_last, token = _chip_start(p_last)

    big = [("ffn_w_gate", ("g0", "g1"), False,
            rows_first(ffn_w_gate), rows_first(m_ffn_w_gate), rows_first(v_ffn_w_gate)),
           ("ffn_w_up", ("u0", "u1"), False, rows_first(ffn_w_up), rows_first(m_ffn_w_up), rows_first(v_ffn_w_up)),
           ("ffn_w_down", ("d0", "d1"), False, ffn_w_down, m_ffn_w_down, v_ffn_w_down),
           ("a_w_out", ("a_out",), False, a_w_out, m_a_w_out, v_a_w_out),
           ("b_w_pw1", ("b_pw1",), True, b_w_pw1, m_b_w_pw1, v_b_w_pw1),
           ("b_w_pw2", ("b_pw2",), False, b_w_pw2, m_b_w_pw2, v_b_w_pw2),
           ("a_w_in", ("a_in",), True, a_w_in, m_a_w_in, v_a_w_in)]
    res, after = {}, token
    for name, keys, transposed, w, m, v in big:
        if keys[0] not in last:
            res[name] = _finish_weight(land, keys, transposed, w, m, v, name, after=after)
            after = res[name][1]
    _, land_last = _chip_wait(sems, p_last, land_last, after)
    land_last = _own_slot(land_last, p_last)
    for name, keys, transposed, w, m, v in big:
        if keys[0] in last:
            res[name] = _finish_weight(land_last, keys, transposed, w, m, v, name, row0=LD_OFF[last[0]])
    for name in ("ffn_w_gate", "ffn_w_up"):
        res[name] = [rows_first(a) for a in res[name]]

    g_small, g_repl, loss8 = _sum_small(sm_land, rp_land)
    loss = loss8[0, 0]

    small_names = ["b_norm", "b_b_pw1", "b_conv", "b_b_conv", "b_ln_g", "b_ln_b", "b_b_pw2", "a_conv",
                   "a_norm", "ffn_norm", "final_norm"]
    triples = [(b_norm, m_b_norm, v_b_norm), (b_b_pw1, m_b_b_pw1, v_b_b_pw1), (b_conv, m_b_conv, v_b_conv),
               (b_b_conv, m_b_b_conv, v_b_b_conv), (b_ln_g, m_b_ln_g, v_b_ln_g), (b_ln_b, m_b_ln_b, v_b_ln_b),
               (b_b_pw2, m_b_b_pw2, v_b_b_pw2), (a_conv, m_a_conv, v_a_conv), (a_norm, m_a_norm, v_a_norm),
               (ffn_norm, m_ffn_norm, v_ffn_norm),
               (fin, m_final_norm.reshape(1, D), v_final_norm.reshape(1, D))]
    for name, quad in zip(small_names, _adamw_small(g_small, g_repl, triples)):
        res[name] = quad
    res["final_norm"] = tuple(a.reshape(D) for a in res["final_norm"])

    order = ["a_norm", "a_w_in", "a_conv", "a_w_out", "b_norm", "b_w_pw1", "b_b_pw1", "b_conv", "b_b_conv",
             "b_ln_g", "b_ln_b", "b_w_pw2", "b_b_pw2", "ffn_norm", "ffn_w_gate", "ffn_w_up", "ffn_w_down",
             "final_norm"]
    out = [loss, dx.reshape(1, T, D)]
    for j in range(4):
        out += [res[k][j] for k in order]
    return tuple(out)
```

```python
import functools

import jax
import jax.numpy as jnp
from jax import lax
from jax.experimental import pallas as pl
from jax.experimental.pallas import tpu as pltpu

F32 = jnp.float32
BF16 = jnp.bfloat16
MESH = pl.DeviceIdType.MESH

D = 1024
FF = 2816
FH = FF // 2
NDEV = 8
KA = 3
KB = 31
HALO_A = 8
HALO_B = 32
CONV_ROWS = 64
RMS_EPS = 1e-6
LN_EPS = 1e-5
LR, B1, B2, ADAM_EPS, WD, STEP = 0.001, 0.9, 0.999, 1e-08, 0.01, 10

TM = 256
TM_A = 512
TM_DW = 2048
VMEM_LIMIT = 62 * 1024 * 1024
LOCAL_THREAD = 1

N_ROWS = {"a_in": 384, "a_out": 128, "b_pw1": 256, "b_pw2": 128, "g0": 352, "u0": 352, "d0": 352,
          "g1": 352, "u1": 352, "d1": 352}
NCHIP = 4


def _offsets(order):
    off, o = {}, 0
    for k in order:
        off[k] = o
        o += N_ROWS[k]
    return off, o


PK_ORDER = ("a_in", "a_out", "g0", "u0", "d0", "b_pw1", "b_pw2", "g1", "u1", "d1")
PK_OFF, PK_ROWS = _offsets(PK_ORDER)
LD_ORDER = ("g1", "u1", "d1", "b_pw1", "b_pw2", "d0", "g0", "u0", "a_out", "a_in")
LD_OFF, N_WROWS = _offsets(LD_ORDER)


def _span(off, keys):
    return off[keys[0]], sum(N_ROWS[k] for k in keys)


SM_B_NORM, SM_B_PW1, SM_B_CONV, SM_B_BCONV, SM_LN_G, SM_LN_B, SM_B_PW2, SM_A_CONV = 0, 1, 3, 34, 35, 36, 37, 38
SM_F32_ROWS = 64


def _pcall(body, **kw):
    return pl.pallas_call(body, **kw)


def _cparams(sem=None):
    return pltpu.CompilerParams(dimension_semantics=sem, vmem_limit_bytes=VMEM_LIMIT)


def _dot(a, b):
    return jnp.dot(a, b, preferred_element_type=F32)


def _dot_nt(a, b):
    return lax.dot_general(a, b, (((1,), (1,)), ((), ())), preferred_element_type=F32)


def _dot_tn(a, b):
    return lax.dot_general(a, b, (((0,), (0,)), ((), ())), preferred_element_type=F32)


def _sigmoid(v):
    return 1.0 / (1.0 + jnp.exp(-v))


def _rms_stat(x):
    return lax.rsqrt(jnp.mean(x * x, axis=-1, keepdims=True) + RMS_EPS)


def _rms_bwd(dn, x, r, gamma):
    dng = dn * gamma
    dx = r * dng - x * (r * r * r) * jnp.mean(dng * x, axis=-1, keepdims=True)
    return dx, jnp.sum(dn * x * r, axis=0, keepdims=True)


def _weight_loads(wg_ref, plan, sems):
    copies = []
    for j, (dst, key) in enumerate(plan):
        off, n = PK_OFF[key], N_ROWS[key]
        copies += [pltpu.make_async_copy(
            wg_ref.at[d, pl.ds(off, n), :], dst.at[pl.ds(d * n, n), :], sems.at[j * NDEV + d])
            for d in range(NDEV)]

    def load():
        for k, cp in enumerate(copies):
            cp.start(priority=k % 2)
        for cp in copies:
            cp.wait()

    return load


_ANY = pl.BlockSpec(memory_space=pl.ANY)


def _row_spec(tm, width, rev_nt=None):
    if rev_nt is None:
        return pl.BlockSpec((tm, width), lambda i: (i, 0))
    return pl.BlockSpec((tm, width), lambda i: (rev_nt - 1 - i, 0))


def _full_spec(shape):
    return pl.BlockSpec(shape, lambda *_: (0,) * len(shape))


def _sds(shape, dtype):
    return jax.ShapeDtypeStruct(shape, dtype)


def _mesh_pos():
    return lax.axis_index("x"), lax.axis_index("y"), lax.axis_index("c")


def _lin(p):
    return 4 * p[0] + 2 * p[1] + p[2]


def _ag_exchange(src, slot, send_sems, recv_sems, local_sem):
    x, y, c = _mesh_pos()
    me, sibling = (x, y, c), (x, y, 1 - c)
    chips = [(1 - x, y), (x, 1 - y), (1 - x, 1 - y)]

    def copy(k, block, to, own=False):
        return pltpu.make_async_remote_copy(
            src_ref=src if own else slot(block), dst_ref=slot(block),
            send_sem=send_sems.at[k], recv_sem=recv_sems.at[k], device_id=to, device_id_type=MESH)

    mine = pltpu.make_async_copy(src, slot(me), local_sem)
    first = [copy(0, me, sibling, own=True)]
    first += [copy(1 + j, me, (*chip, c), own=True) for j, chip in enumerate(chips)]
    passed = [copy(4 + j, (*chip, c), sibling) for j, chip in enumerate(chips)]

    def start():
        mine.start()
        for cp in first:
            cp.start()

    def forward():
        for j, chip in enumerate(chips):
            copy(1 + j, (*chip, c), me).wait_recv()
            passed[j].start()

    def finish():
        copy(0, sibling, me).wait_recv()
        for j, chip in enumerate(chips):
            copy(4 + j, (*chip, 1 - c), me).wait_recv()
        for cp in first + passed:
            cp.wait_send()
        mine.wait()

    return start, forward, finish


def _chip_exchange(p_ref, land_ref, send_sems, recv_sems, keys):
    x, y, c = _mesh_pos()
    l0, rows = _span(LD_OFF, keys)
    dst = land_ref.at[2 * x + y, pl.ds(l0, rows), :]
    peers = [((x + (j >> 1)) % 2, (y + (j & 1)) % 2) for j in range(NCHIP)]

    def copy(j):
        tx, ty = peers[j]
        src = p_ref.at[2 * tx + ty]
        if j == 0:
            return pltpu.make_async_copy(src, dst, send_sems.at[0])
        return pltpu.make_async_remote_copy(
            src_ref=src, dst_ref=dst, send_sem=send_sems.at[j], recv_sem=recv_sems.at[j],
            device_id=(tx, ty, c), device_id_type=MESH)

    def start():
        for j in range(NCHIP):
            copy(j).start()

    def finish():
        for j in range(NCHIP):
            copy(j).wait()

    return start, finish


def _all_to_all_f32(src_for, dst_ref, send_sems, recv_sems):
    x, y, c = _mesh_pos()
    dst = dst_ref.at[_lin((x, y, c))]
    peers = [((x + ((j >> 2) & 1)) % 2, (y + ((j >> 1) & 1)) % 2, (c + (j & 1)) % 2) for j in range(NDEV)]

    def copy(j):
        if j == 0:
            return pltpu.make_async_copy(src_for(_lin(peers[0])), dst, send_sems.at[0])
        return pltpu.make_async_remote_copy(
            src_ref=src_for(_lin(peers[j])), dst_ref=dst, send_sem=send_sems.at[j], recv_sem=recv_sems.at[j],
            device_id=peers[j], device_id_type=MESH)

    def start():
        for j in range(NDEV):
            copy(j).start()

    def finish():
        for j in range(NDEV):
            copy(j).wait()

    return start, finish


class _Comm:
    def __init__(self, ins, alias_in, out_shape, scratch, make, gives_wg, middle_step=0):
        self.ins, self.alias_in, self.out_shape = ins, alias_in, out_shape
        self.scratch, self.make, self.gives_wg, self.middle_step = scratch, make, gives_wg, middle_step


def _ag_comm(shard, wg, keys, forward_step=0):
    def make(c_ins, c_out, sc):
        return _weights_exchange(c_ins[0], c_out, sc[0], sc[1], sc[2], keys)

    return _Comm([shard, wg], 1, _sds(wg.shape, BF16),
                 [pltpu.SemaphoreType.DMA((7,)), pltpu.SemaphoreType.DMA((7,)), pltpu.SemaphoreType.DMA],
                 make, True, forward_step)


def _weights_exchange(shard_ref, wg_ref, send_sems, recv_sems, local_sem, keys):
    r0, nr = _span(PK_OFF, keys)
    return _ag_exchange(shard_ref.at[pl.ds(r0, nr), :], lambda p: wg_ref.at[_lin(p), pl.ds(r0, nr), :],
                        send_sems, recv_sems, local_sem)


def _rs_comm(psum, keys, land):
    def make(c_ins, c_out, sc):
        start, finish = _chip_exchange(c_ins[0], c_out, sc[0], sc[1], keys)
        return start, None, finish

    ins = [psum] + ([] if land is None else [land])
    return _Comm(ins, None if land is None else 1, _sds((NCHIP, N_WROWS, D), BF16),
                 [pltpu.SemaphoreType.DMA((NCHIP,)), pltpu.SemaphoreType.DMA((NCHIP,))], make, False)


def _hosted_call(body, name, nt, in_specs, out_specs, out_shape, scratch, args, comm):
    if comm is None:
        return _pcall(body, name=name, grid=(nt,), in_specs=in_specs, out_specs=tuple(out_specs),
                      out_shape=tuple(out_shape), scratch_shapes=scratch,
                      compiler_params=_cparams(("arbitrary",)))(*args)
    n_in, n_out, n_sc, n_cin = len(in_specs), len(out_specs), len(scratch), len(comm.ins)

    def wrapped(*refs):
        ins = refs[:n_in]
        c_ins = refs[n_in:n_in + n_cin]
        outs = refs[n_in + n_cin:n_in + n_cin + n_out]
        c_out = refs[n_in + n_cin + n_out]
        sc = refs[n_in + n_cin + n_out + 1:n_in + n_cin + n_out + 1 + n_sc]
        c_sc = refs[n_in + n_cin + n_out + 1 + n_sc:]
        start, middle, finish = comm.make(c_ins, c_out, c_sc)
        pl.when(pl.program_id(0) == 0)(start)
        if comm.gives_wg:
            body(*ins, c_out, *outs, *sc)
        else:
            body(*ins, *outs, *sc)
        if middle is not None:
            pl.when(pl.program_id(0) == max(nt - 1 - comm.middle_step, 0))(middle)
        pl.when(pl.program_id(0) == nt - 1)(finish)

    aliases = {} if comm.alias_in is None else {n_in + comm.alias_in: n_out}
    res = _pcall(wrapped, name=name, grid=(nt,),
                 in_specs=list(in_specs) + [_ANY] * n_cin, out_specs=tuple(out_specs) + (_ANY,),
                 out_shape=tuple(out_shape) + (comm.out_shape,),
                 scratch_shapes=list(scratch) + list(comm.scratch),
                 input_output_aliases=aliases,
                 compiler_params=_cparams(("arbitrary",)))(*args, *comm.ins)
    return res


def _pack_shard(weights, smalls):
    plan = (("a_in", 0, 0, True), ("a_out", 1, 0, False), ("b_pw1", 2, 0, True), ("b_pw2", 3, 0, False),
            ("g0", 4, 0, False), ("u0", 5, 0, False), ("d0", 6, 0, False),
            ("g1", 4, 1, False), ("u1", 5, 1, False), ("d1", 6, 1, False))
    n_pad = 384

    def body(*refs):
        out, sm, pad = refs[-3:]
        for key, idx, layer, transposed in plan:
            n = N_ROWS[key]
            val = refs[idx][layer]
            if transposed:
                if n % 128:
                    pad[:, pl.ds(0, n)] = val
                    pad[:, pl.ds(n, n_pad - n)] = jnp.zeros((D, n_pad - n), F32)
                    val = pad[...]
                val = val.T[:n]
            out[pl.ds(PK_OFF[key], n), :] = val.astype(BF16)
        b_norm, b_b_pw1, b_conv, b_b_conv, b_ln_g, b_ln_b, b_b_pw2, a_conv = refs[len(weights):-3]
        sm[...] = jnp.zeros((SM_F32_ROWS, 128), F32)
        sm[pl.ds(SM_B_PW1, 1), :] = b_b_pw1[:, pl.ds(0, 128)]
        sm[pl.ds(SM_B_PW1 + 1, 1), :] = b_b_pw1[:, pl.ds(128, 128)]
        sm[pl.ds(SM_B_CONV, KB), :] = b_conv[0]
        sm[pl.ds(SM_A_CONV, KA), :] = a_conv[0]
        for row, ref in ((SM_B_NORM, b_norm), (SM_B_BCONV, b_b_conv), (SM_LN_G, b_ln_g), (SM_LN_B, b_ln_b),
                         (SM_B_PW2, b_b_pw2)):
            sm[pl.ds(row, 1), :] = ref[...]

    vm = pl.BlockSpec(memory_space=pltpu.VMEM)
    return _pcall(
        body, name="pack_shard",
        out_shape=(_sds((PK_ROWS, D), BF16), _sds((SM_F32_ROWS, 128), F32)),
        in_specs=[vm] * (len(weights) + len(smalls)), out_specs=(vm, vm),
        scratch_shapes=[pltpu.VMEM((D, n_pad), F32)],
        compiler_params=_cparams(),
    )(*weights, *smalls)


def _all_gather_first(shard, small):
    def body(x_ref, s_ref, wg_ref, sg_ref, send_w, recv_w, local_w, send_s, recv_s, local_s):
        start_w, forward_w, finish_w = _weights_exchange(x_ref, wg_ref, send_w, recv_w, local_w, ("a_in", "a_out"))
        start_s, forward_s, finish_s = _ag_exchange(s_ref, lambda p: sg_ref.at[_lin(p)], send_s, recv_s, local_s)
        start_s()
        start_w()
        forward_s()
        forward_w()
        finish_s()
        finish_w()

    sems = [pltpu.SemaphoreType.DMA((7,)), pltpu.SemaphoreType.DMA((7,)), pltpu.SemaphoreType.DMA]
    return _pcall(
        body, name="all_gather_first",
        out_shape=(_sds((NDEV, PK_ROWS, D), BF16), _sds((NDEV, SM_F32_ROWS, 128), F32)),
        in_specs=[_ANY, _ANY], out_specs=(_ANY, _ANY), scratch_shapes=sems + sems,
    )(shard, small)


def _pair_reduce(grads, keys, side=None, land=None):
    l0, rows = _span(LD_OFF, keys)
    nk = len(keys)
    n_side = 0 if side is None else 2
    n_land = 0 if land is None else 1

    def body(c_ref, *refs):
        mine = refs[:nk]
        whole = refs[nk:2 * nk]
        n_in = 2 * nk + n_side + n_land
        o_ref = refs[n_in]
        got, send_sem, recv_sems = refs[n_in + 1 + n_side:n_in + 4 + n_side]
        t = pl.program_id(0)
        x, y, c = _mesh_pos()
        sibling = (x, y, 1 - c)
        if side is not None:
            sm_ref, rp_ref = refs[2 * nk:2 * nk + 2]
            sm_land, rp_land = refs[n_in + 1:n_in + 3]
            s2, r2, s3, r3 = refs[n_in + 6:n_in + 10]
            start_s, finish_s = _all_to_all_f32(lambda d: sm_ref.at[d], sm_land, s2, r2)
            start_r, finish_r = _all_to_all_f32(lambda d: rp_ref.at[0], rp_land, s3, r3)

        @pl.when(t == 0)
        def _():
            if side is not None:
                start_s()
                start_r()
            for ref, key in zip(whole, keys):
                for d in range(NCHIP):
                    pltpu.make_async_remote_copy(
                        src_ref=ref.at[2 * d + (1 - c)], dst_ref=got.at[d, pl.ds(LD_OFF[key] - l0, N_ROWS[key]), :],
                        send_sem=send_sem, recv_sem=recv_sems.at[d], device_id=sibling, device_id_type=MESH).start()

        pltpu.make_async_remote_copy(src_ref=got.at[t], dst_ref=got.at[t], send_sem=send_sem,
                                     recv_sem=recv_sems.at[t], device_id=sibling, device_id_type=MESH).wait_recv()
        sums = o_ref if land is None else refs[-3].at[t]
        for ref, key in zip(mine, keys):
            sl = pl.ds(LD_OFF[key] - l0, N_ROWS[key])
            sums[sl, :] = (ref[...].astype(F32) + got[t, sl, :].astype(F32)).astype(BF16)

        if land is not None:
            psum, chip_send, chip_recv = refs[-3:]
            my_chip = 2 * x + y
            dst = o_ref.at[my_chip, pl.ds(l0, rows), :]
            hop = jnp.bitwise_xor(t, my_chip)

            @pl.when(t == my_chip)
            def _():
                pltpu.make_async_copy(psum.at[t], dst, chip_send.at[0]).start()

            @pl.when(t != my_chip)
            def _():
                pltpu.make_async_remote_copy(
                    src_ref=psum.at[t], dst_ref=dst, send_sem=chip_send.at[hop], recv_sem=chip_recv.at[hop],
                    device_id=(t // 2, t % 2, c), device_id_type=MESH).start()

        @pl.when(t == NCHIP - 1)
        def _():
            pltpu.make_async_remote_copy(src_ref=got, dst_ref=got, send_sem=send_sem, recv_sem=recv_sems.at[0],
                                         device_id=sibling, device_id_type=MESH).wait_send()
            if land is not None:
                one = o_ref.at[0, pl.ds(l0, rows), :]
                pltpu.make_async_copy(one, one, chip_send.at[0]).wait()
                for j in range(1, NCHIP):
                    pltpu.make_async_remote_copy(src_ref=one, dst_ref=one, send_sem=chip_send.at[j],
                                                 recv_sem=chip_recv.at[j], device_id=sibling, device_id_type=MESH).wait()
            if side is not None:
                finish_s()
                finish_r()

    if land is None:
        out_specs = [pl.BlockSpec((None, rows, D), lambda t, c: (t, 0, 0))]
        out_shape = [_sds((NCHIP, rows, D), BF16)]
    else:
        out_specs, out_shape = [_ANY], [_sds((NCHIP, N_WROWS, D), BF16)]
    scratch = [pltpu.VMEM((NCHIP, rows, D), BF16), pltpu.SemaphoreType.DMA, pltpu.SemaphoreType.DMA((NCHIP,))]
    if side is not None:
        out_specs += [_ANY, _ANY]
        out_shape += [_sds((NDEV, SM_F32_ROWS, 128), F32), _sds((NDEV, 8, D), F32)]
        scratch += [pltpu.SemaphoreType.DMA((NDEV,))] * 4
    if land is not None:
        scratch += [pltpu.VMEM((NCHIP, rows, D), BF16), pltpu.SemaphoreType.DMA((NCHIP,)),
                    pltpu.SemaphoreType.DMA((NCHIP,))]
    grid_spec = pltpu.PrefetchScalarGridSpec(
        num_scalar_prefetch=1, grid=(NCHIP,),
        in_specs=[pl.BlockSpec((None, N_ROWS[k], D), lambda t, c: (2 * t + c[0], 0, 0)) for k in keys]
        + [_ANY] * (nk + n_side + n_land),
        out_specs=tuple(out_specs), scratch_shapes=scratch)
    core = lax.axis_index("c").astype(jnp.int32).reshape(1)
    aliases = {} if land is None else {1 + 2 * nk + n_side: 0}
    res = _pcall(
        body, name="pair_reduce_" + keys[0], grid_spec=grid_spec, out_shape=tuple(out_shape),
        input_output_aliases=aliases, compiler_params=_cparams(("arbitrary",)),
    )(core, *grads, *grads, *(side or ()), *(() if land is None else (land,)))
    return res[0] if len(res) == 1 else res


def _chip_start(psum):
    hbm = pl.BlockSpec(memory_space=pltpu.HBM)
    sem = pl.BlockSpec(memory_space=pltpu.SEMAPHORE)

    def body(p_ref, land_ref, s1, s2, s3, r1, r2, r3, p_thru, land_thru, token):
        x, y, c = _mesh_pos()
        for j, (send_sem, recv_sem) in enumerate(((s1, r1), (s2, r2), (s3, r3)), start=1):
            tx, ty = (x + (j >> 1)) % 2, (y + (j & 1)) % 2
            pltpu.make_async_remote_copy(
                src_ref=p_ref.at[2 * tx + ty], dst_ref=land_ref.at[2 * x + y], send_sem=send_sem, recv_sem=recv_sem,
                device_id=(tx, ty, c), device_id_type=MESH).start()
        token[...] = jnp.zeros_like(token)

    dma = pltpu.SemaphoreType.DMA(())
    buf = pltpu.HBM(psum.shape, psum.dtype)
    res = _pcall(
        body, name="chip_exchange_start",
        out_shape=(dma,) * 6 + (buf, buf, _sds((8, 128), F32)),
        in_specs=(hbm, hbm), out_specs=(sem,) * 6 + (hbm, hbm, pl.BlockSpec(memory_space=pltpu.VMEM)),
        input_output_aliases={0: 6, 1: 7},
        compiler_params=pltpu.CompilerParams(has_side_effects=pltpu.SideEffectType.DATAFLOW_SIDE_EFFECTING),
    )(pltpu.with_memory_space_constraint(psum, pltpu.HBM),
      pltpu.with_memory_space_constraint(lax.empty(psum.shape, psum.dtype), pltpu.HBM))
    return res[:6], res[6], res[7], res[8]


def _chip_wait(sems, p_thru, land_thru, after):
    hbm = pl.BlockSpec(memory_space=pltpu.HBM)
    sem = pl.BlockSpec(memory_space=pltpu.SEMAPHORE)

    def body(p_ref, land_ref, s1, s2, s3, r1, r2, r3, after_ref, p_dead, got_ref):
        x, y, c = _mesh_pos()
        for send_sem, recv_sem in ((s1, r1), (s2, r2), (s3, r3)):
            copy = pltpu.make_async_remote_copy(
                src_ref=p_ref.at[0], dst_ref=land_ref.at[0], send_sem=send_sem, recv_sem=recv_sem,
                device_id=(x, y, 1 - c), device_id_type=MESH)
            copy.wait_send()
            copy.wait_recv()

    buf = pltpu.HBM(p_thru.shape, p_thru.dtype)
    return _pcall(
        body, name="chip_exchange_wait", out_shape=(buf, buf),
        in_specs=(hbm, hbm) + (sem,) * 6 + (_ANY,), out_specs=(hbm, hbm),
        input_output_aliases={0: 0, 1: 1},
        compiler_params=pltpu.CompilerParams(has_side_effects=pltpu.SideEffectType.DATAFLOW_SIDE_EFFECTING),
    )(p_thru, land_thru, *sems, after)


def _own_slot(land, psum):
    def body(land_in, p_ref, land_ref, sem):
        x, y, _ = _mesh_pos()
        cp = pltpu.make_async_copy(p_ref.at[2 * x + y], land_ref.at[2 * x + y], sem)
        cp.start()
        cp.wait()

    return _pcall(
        body, name="own_slot", out_shape=_sds(land.shape, land.dtype),
        in_specs=[_ANY, _ANY], out_specs=_ANY, input_output_aliases={0: 0},
        scratch_shapes=[pltpu.SemaphoreType.DMA],
    )(land, psum)


def _mixa_fwd(x, gam, cw, wg, tm, comm):
    T = x.shape[0]

    def body(x_ref, gam_ref, cw_ref, wg_ref, h_ref, b_ref, c_ref, v_ref, cc_ref, y_ref,
             win, wout, buf, sems):
        first = pl.program_id(0) == 0
        load_w = _weight_loads(wg_ref, [(win, "a_in"), (wout, "a_out")], sems)

        @pl.when(first)
        def _():
            load_w()
            buf[pl.ds(0, HALO_A), :] = jnp.zeros((HALO_A, D), F32)

        xv = x_ref[...]
        nb = (xv * _rms_stat(xv) * gam_ref[...]).astype(BF16)
        bv = _dot_nt(nb, win[pl.ds(0, D), :])
        cval = _dot_nt(nb, win[pl.ds(D, D), :])
        vval = _dot_nt(nb, win[pl.ds(2 * D, D), :])
        cv = cval * vval
        buf[pl.ds(HALO_A, tm), :] = cv
        cc = cw_ref[pl.ds(KA - 1, 1), :] * cv
        for k in range(KA - 1):
            cc = cc + cw_ref[pl.ds(k, 1), :] * buf[pl.ds(HALO_A - (KA - 1) + k, tm), :]
        buf[pl.ds(0, HALO_A), :] = buf[pl.ds(tm, HALO_A), :]
        yb = (bv * cc).astype(BF16)
        b_ref[...] = bv.astype(BF16)
        c_ref[...] = cval.astype(BF16)
        v_ref[...] = vval.astype(BF16)
        cc_ref[...] = cc.astype(BF16)
        y_ref[...] = yb
        h_ref[...] = xv + _dot(yb, wout[...])

    act = _sds((T, D), BF16)
    return _hosted_call(
        body, "mixa_fwd", T // tm,
        in_specs=[_row_spec(tm, D), _full_spec((1, D)), _full_spec((KA, D))],
        out_specs=[_row_spec(tm, D) for _ in range(6)],
        out_shape=(_sds((T, D), F32),) + (act,) * 5,
        scratch=[pltpu.VMEM((3 * D, D), BF16), pltpu.VMEM((D, D), BF16),
                 pltpu.VMEM((HALO_A + tm, D), F32), pltpu.SemaphoreType.DMA((2 * NDEV,))],
        args=(x, gam, cw), comm=comm)


def _ffn_fwd(h, gam, wg, layer, tm, comm, head=None):
    T = h.shape[0]
    kg, ku, kd = "g%d" % layer, "u%d" % layer, "d%d" % layer
    n_head = 0 if head is None else 2

    def body(*refs):
        h_ref, gam_ref = refs[:2]
        wg_ref = refs[2 + n_head]
        o_ref, n_ref, g_ref, u_ref, gu_ref = refs[3 + n_head:8 + n_head]
        wgt, wut, wd, sems = refs[-4:]

        first = pl.program_id(0) == 0
        load_w = _weight_loads(wg_ref, [(wgt, kg), (wut, ku), (wd, kd)], sems)
        pl.when(first)(load_w)

        hv = h_ref[...]
        nb = (hv * _rms_stat(hv) * gam_ref[...]).astype(BF16)
        n_ref[...] = nb
        out = hv
        for f in range(2):
            cols = pl.ds(f * FH, FH)
            g = _dot_nt(nb, wgt[cols, :])
            u = _dot_nt(nb, wut[cols, :])
            gu = (g * _sigmoid(g) * u).astype(BF16)
            g_ref[:, cols] = g.astype(BF16)
            u_ref[:, cols] = u.astype(BF16)
            gu_ref[:, cols] = gu
            out = out + _dot(gu, wd[cols, :])
        if head is None:
            o_ref[...] = out
        else:
            t_ref, fin_ref, st_ref = refs[2], refs[3], refs[8 + n_head]

            @pl.when(pl.program_id(0) == 0)
            def _():
                st_ref[...] = jnp.zeros((8, D), F32)

            gamma = fin_ref[...]
            r = _rms_stat(out)
            err = out * r * gamma - t_ref[...]
            dx, dgam = _rms_bwd(err * (1.0 / D), out, r, gamma)
            o_ref[...] = dx
            st_ref[pl.ds(0, 1), :] += dgam
            st_ref[pl.ds(1, 1), :] += (0.5 / D) * jnp.sum(err * err, axis=0, keepdims=True)

    pre = _sds((T, FF), BF16)
    in_specs = [_row_spec(tm, D), _full_spec((1, D))]
    args = (h, gam)
    out_specs = [_row_spec(tm, D), _row_spec(tm, D), _row_spec(tm, FF), _row_spec(tm, FF), _row_spec(tm, FF)]
    out_shape = (_sds((T, D), F32), _sds((T, D), BF16), pre, pre, pre)
    if head is not None:
        in_specs, args = in_specs + [_row_spec(tm, D), _full_spec((1, D))], args + tuple(head)
        out_specs, out_shape = out_specs + [_full_spec((8, D))], out_shape + (_sds((8, D), F32),)
    if comm is None:
        in_specs, args = in_specs + [_ANY], args + (wg,)
    return _hosted_call(
        body, "ffn%d_fwd" % layer, T // tm,
        in_specs=in_specs, out_specs=out_specs, out_shape=out_shape,
        scratch=[pltpu.VMEM((FF, D), BF16)] * 3 + [pltpu.SemaphoreType.DMA((3 * NDEV,))],
        args=args, comm=comm)


def _shifted_copies(buf, shf, tm):
    for r in range(1, 8):
        shf[r - 1] = buf[pl.ds(r, tm + HALO_B - 8), :]


def _broadcast_taps(vec_ref, wb):
    for k in range(KB):
        wb[k] = jnp.broadcast_to(vec_ref[pl.ds(SM_B_CONV + k, 1), :], (8, D))
    wb[KB] = jnp.broadcast_to(vec_ref[pl.ds(SM_B_BCONV, 1), :], (8, D))


def _taps_by_shift_residue(taps):
    groups = {}
    for k, shift in taps:
        q, r = divmod(shift, 8)
        groups.setdefault(r, []).append((k, q))
    return sorted(groups.items())


def _window(buf, shf, base, r, q0, n_groups, lanes):
    rows = pl.ds(base + 8 * q0, 8 * n_groups)
    v = buf[rows, lanes] if r == 0 else shf[r - 1, rows, lanes]
    return [v[8 * i:8 * i + 8] for i in range(n_groups)]


def _long_conv(buf, shf, wb, out_ref, tm, taps, bias_row):
    n_acc = CONV_ROWS // 8
    groups = _taps_by_shift_residue(taps)
    for col in range(D // 128):
        lanes = pl.ds(128 * col, 128)

        def rows(i, carry, lanes=lanes):
            base = i * CONV_ROWS
            init = jnp.zeros((8, 128), F32) if bias_row is None else wb[bias_row, :, lanes]
            accs = [init] * n_acc
            for r, lst in groups:
                q0, q1 = min(q for _, q in lst), max(q for _, q in lst)
                win = _window(buf, shf, base, r, q0, n_acc + q1 - q0, lanes)
                for k, q in lst:
                    wk = wb[k, :, lanes]
                    accs = [acc + wk * win[h + q - q0] for h, acc in enumerate(accs)]
            for h, acc in enumerate(accs):
                out_ref[pl.ds(base + 8 * h, 8), lanes] = acc
            return carry

        for i in range(tm // CONV_ROWS):
            rows(i, 0)


def _long_conv_grad_taps(buf, shf, x_ref, wacc, tm):
    n_acc = CONV_ROWS // 8
    groups = _taps_by_shift_residue([(KB - 1 - j, j) for j in range(KB)])
    for col in range(D // 128):
        lanes = pl.ds(128 * col, 128)

        def rows(i, carry, lanes=lanes):
            base = i * CONV_ROWS
            xv = x_ref[pl.ds(base, CONV_ROWS), lanes]
            xs = [xv[8 * h:8 * h + 8] for h in range(n_acc)]
            for r, lst in groups:
                q0, q1 = min(q for _, q in lst), max(q for _, q in lst)
                win = _window(buf, shf, base, r, q0, n_acc + q1 - q0, lanes)
                for k, q in lst:
                    prod = [x * win[h + q - q0] for h, x in enumerate(xs)]
                    wacc[k, :, lanes] += functools.reduce(lambda a, b: a + b, prod)
            return carry

        for i in range(tm // CONV_ROWS):
            rows(i, 0)


def _ln_stats(dc):
    mu = jnp.mean(dc, axis=-1, keepdims=True)
    xc = dc - mu
    rstd = lax.rsqrt(jnp.mean(xc * xc, axis=-1, keepdims=True) + LN_EPS)
    return xc * rstd, rstd


def _mixb_fwd(h, vecs, bias1, wg, tm, comm):
    T = h.shape[0]

    def body(h_ref, vec_ref, b1_ref, wg_ref, o_ref, a_ref, g_ref, dc_ref, s_ref, w1, w2, buf, shf, wb, sems):
        first = pl.program_id(0) == 0
        load_w = _weight_loads(wg_ref, [(w1, "b_pw1"), (w2, "b_pw2")], sems)

        @pl.when(first)
        def _():
            load_w()
            buf[pl.ds(0, HALO_B), :] = jnp.zeros((HALO_B, D), F32)
            _broadcast_taps(vec_ref, wb)

        hv = h_ref[...]
        nb = (hv * _rms_stat(hv) * vec_ref[pl.ds(SM_B_NORM, 1), :]).astype(BF16)
        a = _dot_nt(nb, w1[pl.ds(0, D), :]) + b1_ref[:, pl.ds(0, D)]
        g = _dot_nt(nb, w1[pl.ds(D, D), :]) + b1_ref[:, pl.ds(D, D)]
        a_ref[...] = a.astype(BF16)
        g_ref[...] = g.astype(BF16)
        buf[pl.ds(HALO_B, tm), :] = a * _sigmoid(g)
        _shifted_copies(buf, shf, tm)

        _long_conv(buf, shf, wb, dc_ref, tm, [(k, HALO_B - (KB - 1) + k) for k in range(KB)], KB)
        buf[pl.ds(0, HALO_B), :] = buf[pl.ds(tm, HALO_B), :]
        xhat, _ = _ln_stats(dc_ref[...])
        ln = xhat * vec_ref[pl.ds(SM_LN_G, 1), :] + vec_ref[pl.ds(SM_LN_B, 1), :]
        s = (ln * _sigmoid(ln)).astype(BF16)
        s_ref[...] = s
        o_ref[...] = hv + _dot(s, w2[...]) + vec_ref[pl.ds(SM_B_PW2, 1), :]

    act = _sds((T, D), BF16)
    return _hosted_call(
        body, "mixb_fwd", T // tm,
        in_specs=[_row_spec(tm, D), _full_spec((SM_F32_ROWS, D)), _full_spec((1, 2 * D))],
        out_specs=[_row_spec(tm, D) for _ in range(5)],
        out_shape=(_sds((T, D), F32), act, act, _sds((T, D), F32), act),
        scratch=[pltpu.VMEM((2 * D, D), BF16), pltpu.VMEM((D, D), BF16),
                 pltpu.VMEM((HALO_B + tm, D), F32), pltpu.VMEM((7, HALO_B + tm - 8, D), F32),
                 pltpu.VMEM((KB + 1, 8, D), F32), pltpu.SemaphoreType.DMA((2 * NDEV,))],
        args=(h, vecs, bias1), comm=comm)


def _ffn_bwd_dx(dh, h, g, u, gam, wg, layer, tm, comm):
    T = h.shape[0]
    kg, ku, kd = "g%d" % layer, "u%d" % layer, "d%d" % layer

    def body(dh_ref, h_ref, g_ref, u_ref, gam_ref, wg_ref, o_ref, dg_ref, du_ref, st_ref, wgt, wut, wd, sems):
        first = pl.program_id(0) == 0
        load_w = _weight_loads(wg_ref, [(wd, kd), (wgt, kg), (wut, ku)], sems)

        @pl.when(first)
        def _():
            load_w()
            st_ref[...] = jnp.zeros((8, D), F32)

        dhv = dh_ref[...]
        dhb = dhv.astype(BF16)
        dn = jnp.zeros_like(dhv)
        for f in range(2):
            cols = pl.ds(f * FH, FH)
            dgu = _dot_nt(dhb, wd[cols, :])
            gv = g_ref[:, cols].astype(F32)
            sg = _sigmoid(gv)
            du = (dgu * gv * sg).astype(BF16)
            dg = (dgu * u_ref[:, cols].astype(F32) * (sg * (1.0 + gv * (1.0 - sg)))).astype(BF16)
            dg_ref[:, cols] = dg
            du_ref[:, cols] = du
            dn = dn + _dot(dg, wgt[cols, :]) + _dot(du, wut[cols, :])
        hv = h_ref[...]
        dx, dgam = _rms_bwd(dn, hv, _rms_stat(hv), gam_ref[...])
        o_ref[...] = dhv + dx
        st_ref[pl.ds(0, 1), :] += dgam

    pre = _sds((T, FF), BF16)
    return _hosted_call(
        body, "ffn%d_bwd_dx" % layer, T // tm,
        in_specs=[_row_spec(tm, D), _row_spec(tm, D), _row_spec(tm, FF), _row_spec(tm, FF),
                  _full_spec((1, D)), _ANY],
        out_specs=[_row_spec(tm, D), _row_spec(tm, FF), _row_spec(tm, FF), _full_spec((8, D))],
        out_shape=(_sds((T, D), F32), pre, pre, _sds((8, D), F32)),
        scratch=[pltpu.VMEM((FF, D), BF16)] * 3 + [pltpu.SemaphoreType.DMA((3 * NDEV,))],
        args=(dh, h, g, u, gam, wg), comm=comm)


def _grad_w(lhs, rhs, mc, name, tm):
    T, M = lhs.shape
    nt = T // tm

    def body(l_ref, r_ref, o_ref, acc):
        i = pl.program_id(1)

        @pl.when(i == 0)
        def _():
            acc[...] = jnp.zeros((mc, D), F32)

        acc[...] += _dot_tn(l_ref[...], r_ref[...].astype(BF16))

        @pl.when(i == nt - 1)
        def _():
            o_ref[...] = acc[...].astype(BF16)

    return _pcall(
        body, name=name, grid=(M // mc, nt),
        out_shape=_sds((M, D), BF16),
        in_specs=[pl.BlockSpec((tm, mc), lambda j, i: (i, j)), pl.BlockSpec((tm, D), lambda j, i: (i, 0))],
        out_specs=pl.BlockSpec((mc, D), lambda j, i: (j, 0)),
        scratch_shapes=[pltpu.VMEM((mc, D), F32)],
        compiler_params=_cparams(("arbitrary", "arbitrary")),
    )(lhs, rhs)


def _mixb_bwd(dh, h, a, g, dc, s, vecs, wg, tm, comm):
    T = h.shape[0]
    nt = T // tm

    def body(dh_ref, h_ref, a_ref, g_ref, dc_ref, s_ref, vec_ref, wg_ref, o_ref, st_ref, sb_ref, g1_ref, g2_ref,
             w1, w2, buf, shf, glu_s, dglu_s, wacc, wb, acc1, acc2, stage, sems, out_sems):
        first = pl.program_id(0) == 0
        load_w = _weight_loads(wg_ref, [(w2, "b_pw2"), (w1, "b_pw1")], sems)

        @pl.when(first)
        def _():
            load_w()
            buf[pl.ds(tm, HALO_B), :] = jnp.zeros((HALO_B, D), F32)
            wacc[...] = jnp.zeros((KB, 8, D), F32)
            _broadcast_taps(vec_ref, wb)
            st_ref[...] = jnp.zeros((SM_F32_ROWS, D), F32)
            sb_ref[...] = jnp.zeros((8, 2 * D), F32)
            acc1[...] = jnp.zeros((2 * D, D), F32)
            acc2[...] = jnp.zeros((D, D), F32)

        def acc(row, val):
            st_ref[pl.ds(row, 1), :] += jnp.sum(val, axis=0, keepdims=True)

        dhv = dh_ref[...]
        dhb = dhv.astype(BF16)
        acc(SM_B_PW2, dhv)
        acc2[...] += _dot_tn(s_ref[...], dhb)
        ds = _dot_nt(dhb, w2[...])
        xhat, rstd = _ln_stats(dc_ref[...])
        ln_g = vec_ref[pl.ds(SM_LN_G, 1), :]
        ln = xhat * ln_g + vec_ref[pl.ds(SM_LN_B, 1), :]
        sl = _sigmoid(ln)
        dln = ds * (sl * (1.0 + ln * (1.0 - sl)))
        acc(SM_LN_G, dln * xhat)
        acc(SM_LN_B, dln)
        dxh = dln * ln_g
        ddc = rstd * (dxh - jnp.mean(dxh, axis=-1, keepdims=True)
                      - xhat * jnp.mean(dxh * xhat, axis=-1, keepdims=True))
        acc(SM_B_BCONV, ddc)
        buf[pl.ds(0, tm), :] = ddc
        _shifted_copies(buf, shf, tm)
        av = a_ref[...].astype(F32)
        sg = _sigmoid(g_ref[...].astype(F32))
        glu_s[...] = av * sg

        _long_conv(buf, shf, wb, dglu_s, tm, [(KB - 1 - j, j) for j in range(KB)], None)
        _long_conv_grad_taps(buf, shf, glu_s, wacc, tm)
        buf[pl.ds(tm, HALO_B), :] = buf[pl.ds(0, HALO_B), :]
        dglu = dglu_s[...]
        da = dglu * sg
        dg = dglu * av * sg * (1.0 - sg)
        sb_ref[pl.ds(0, 1), pl.ds(0, D)] += jnp.sum(da, axis=0, keepdims=True)
        sb_ref[pl.ds(0, 1), pl.ds(D, D)] += jnp.sum(dg, axis=0, keepdims=True)
        dab, dgb = da.astype(BF16), dg.astype(BF16)
        dn = _dot(dab, w1[pl.ds(0, D), :]) + _dot(dgb, w1[pl.ds(D, D), :])
        hv = h_ref[...]
        r = _rms_stat(hv)
        gamma = vec_ref[pl.ds(SM_B_NORM, 1), :]
        nb = (hv * r * gamma).astype(BF16)
        acc1[pl.ds(0, D), :] += _dot_tn(dab, nb)
        acc1[pl.ds(D, D), :] += _dot_tn(dgb, nb)
        dx, dgam = _rms_bwd(dn, hv, r, gamma)
        o_ref[...] = dhv + dx
        st_ref[pl.ds(SM_B_NORM, 1), :] += dgam

        @pl.when(pl.program_id(0) == nt - 1)
        def _():
            st_ref[pl.ds(SM_B_CONV, KB), :] = jnp.sum(wacc[...], axis=1)
            blocks = [(acc1, 0, g1_ref, 0), (acc1, D, g1_ref, D), (acc2, 0, g2_ref, 0)]
            for k, (src, a0, dst, d0) in enumerate(blocks):
                stage[...] = src[pl.ds(a0, D), :].astype(BF16)
                cp = pltpu.make_async_copy(stage, dst.at[pl.ds(d0, D), :], out_sems.at[k])
                cp.start()
                cp.wait()

    rs = functools.partial(_row_spec, rev_nt=nt)
    return _hosted_call(
        body, "mixb_bwd", nt,
        in_specs=[rs(tm, D) for _ in range(6)] + [_full_spec((SM_F32_ROWS, D)), _ANY],
        out_specs=[rs(tm, D), _full_spec((SM_F32_ROWS, D)), _full_spec((8, 2 * D)), _ANY, _ANY],
        out_shape=(_sds((T, D), F32), _sds((SM_F32_ROWS, D), F32), _sds((8, 2 * D), F32),
                   _sds((2 * D, D), BF16), _sds((D, D), BF16)),
        scratch=[pltpu.VMEM((2 * D, D), BF16), pltpu.VMEM((D, D), BF16),
                 pltpu.VMEM((tm + HALO_B, D), F32), pltpu.VMEM((7, tm + HALO_B - 8, D), F32),
                 pltpu.VMEM((tm, D), F32), pltpu.VMEM((tm, D), F32), pltpu.VMEM((KB, 8, D), F32),
                 pltpu.VMEM((KB + 1, 8, D), F32), pltpu.VMEM((2 * D, D), F32), pltpu.VMEM((D, D), F32),
                 pltpu.VMEM((D, D), BF16), pltpu.SemaphoreType.DMA((2 * NDEV,)), pltpu.SemaphoreType.DMA((3,))],
        args=(dh, h, a, g, dc, s, vecs, wg), comm=comm)


def _mixa_bwd(dh, x, b, c, v, cc, y, gam, cw, wg, tm, comm):
    T = x.shape[0]
    nt = T // tm

    def body(dh_ref, x_ref, b_ref, c_ref, v_ref, cc_ref, y_ref, gam_ref, cw_ref, wg_ref,
             o_ref, st_ref, gin_ref, gout_ref, win, wout, buf, acc_in, acc_out, stage, sems, out_sems):
        first = pl.program_id(0) == 0
        load_w = _weight_loads(wg_ref, [(wout, "a_out"), (win, "a_in")], sems)

        @pl.when(first)
        def _():
            load_w()
            buf[pl.ds(tm, HALO_A), :] = jnp.zeros((HALO_A, D), F32)
            st_ref[...] = jnp.zeros((8, D), F32)
            acc_in[...] = jnp.zeros((3 * D, D), F32)
            acc_out[...] = jnp.zeros((D, D), F32)

        dhv = dh_ref[...]
        dhb = dhv.astype(BF16)
        dy = _dot_nt(dhb, wout[...])
        cval = c_ref[...].astype(F32)
        vval = v_ref[...].astype(F32)
        d_b = (dy * cc_ref[...].astype(F32)).astype(BF16)
        buf[pl.ds(0, tm), :] = dy * b_ref[...].astype(F32)
        cv = cval * vval
        dcv = jnp.zeros((tm, D), F32)
        for j in range(KA):
            sh = buf[pl.ds(j, tm), :]
            k = KA - 1 - j
            dcv = dcv + cw_ref[pl.ds(k, 1), :] * sh
            st_ref[pl.ds(1 + k, 1), :] += jnp.sum(cv * sh, axis=0, keepdims=True)
        buf[pl.ds(tm, HALO_A), :] = buf[pl.ds(0, HALO_A), :]
        d_c = (dcv * vval).astype(BF16)
        d_v = (dcv * cval).astype(BF16)
        xv = x_ref[...]
        r = _rms_stat(xv)
        gamma = gam_ref[...]
        nb = (xv * r * gamma).astype(BF16)
        for j, piece in enumerate((d_b, d_c, d_v)):
            acc_in[pl.ds(j * D, D), :] += _dot_tn(piece, nb)
        acc_out[...] += _dot_tn(y_ref[...], dhb)
        dn = _dot(d_b, win[pl.ds(0, D), :]) + _dot(d_c, win[pl.ds(D, D), :]) + _dot(d_v, win[pl.ds(2 * D, D), :])
        dx, dgam = _rms_bwd(dn, xv, r, gamma)
        o_ref[...] = dhv + dx
        st_ref[pl.ds(0, 1), :] += dgam

        @pl.when(pl.program_id(0) == nt - 1)
        def _():
            blocks = [(acc_in, j * D, gin_ref, j * D) for j in range(3)] + [(acc_out, 0, gout_ref, 0)]
            for k, (acc, a0, dst, d0) in enumerate(blocks):
                stage[...] = acc[pl.ds(a0, D), :].astype(BF16)
                cp = pltpu.make_async_copy(stage, dst.at[pl.ds(d0, D), :], out_sems.at[k])
                cp.start()
                cp.wait()

    rs = functools.partial(_row_spec, rev_nt=nt)
    return _hosted_call(
        body, "mixa_bwd", nt,
        in_specs=[rs(tm, D) for _ in range(7)] + [_full_spec((1, D)), _full_spec((KA, D)), _ANY],
        out_specs=[rs(tm, D), _full_spec((8, D)), _ANY, _ANY],
        out_shape=(_sds((T, D), F32), _sds((8, D), F32), _sds((3 * D, D), BF16), _sds((D, D), BF16)),
        scratch=[pltpu.VMEM((3 * D, D), BF16), pltpu.VMEM((D, D), BF16), pltpu.VMEM((tm + HALO_A, D), F32),
                 pltpu.VMEM((3 * D, D), F32), pltpu.VMEM((D, D), F32), pltpu.VMEM((D, D), BF16),
                 pltpu.SemaphoreType.DMA((2 * NDEV,)), pltpu.SemaphoreType.DMA((4,))],
        args=(dh, x, b, c, v, cc, y, gam, cw, wg), comm=comm)


def _sum_small(sm, rp):
    def body(sm_ref, rp_ref, osm, orp, oloss):
        a = sm_ref[0]
        b = rp_ref[0]
        for s in range(1, NDEV):
            a = a + sm_ref[s]
            b = b + rp_ref[s]
        osm[...] = a
        orp[...] = b
        oloss[...] = jnp.zeros((8, 128), F32) + jnp.sum(b[4:5, :], axis=-1, keepdims=True)

    return _pcall(
        body, name="sum_small_grads",
        out_shape=(_sds((SM_F32_ROWS, 128), F32), _sds((8, D), F32), _sds((8, 128), F32)),
        in_specs=[pl.BlockSpec(memory_space=pltpu.VMEM)] * 2,
        out_specs=tuple(pl.BlockSpec(memory_space=pltpu.VMEM) for _ in range(3)),
        compiler_params=_cparams(),
    )(sm, rp)


def _adam_math(w, g, m, v):
    mn = B1 * m + (1.0 - B1) * g
    vn = B2 * v + (1.0 - B2) * (g * g)
    m_hat = mn / (1.0 - B1 ** STEP)
    v_hat = vn / (1.0 - B2 ** STEP)
    return -LR * (m_hat / (jnp.sqrt(v_hat) + ADAM_EPS) + WD * w), mn, vn


def _finish_weight(land, keys, transposed, w, m, v, name, row0=0, after=None):
    layers, rows, cols = w.shape
    n = N_ROWS[keys[0]]
    n_pad = -(-n // 128) * 128 if transposed else n

    def body(w_ref, m_ref, v_ref, land_ref, *rest):
        og, od, om, ov, buf, wv, mv, vv, sg, sd, sm, sv, sems = rest[-13:]
        for l in range(layers):
            off = LD_OFF[keys[l]] - row0
            loads = [pltpu.make_async_copy(land_ref.at[:, pl.ds(off, n), :], buf, sems.at[0])]
            loads += [pltpu.make_async_copy(src.at[l], dst, sems.at[1 + i])
                      for i, (src, dst) in enumerate(((w_ref, wv), (m_ref, mv), (v_ref, vv)))]
            for cp in loads:
                cp.start(priority=LOCAL_THREAD)
            for cp in loads:
                cp.wait()
            g = buf[0].astype(F32)
            for s in range(1, NCHIP):
                g = g + buf[s].astype(F32)
            if transposed:
                if n_pad != n:
                    g = jnp.concatenate([g, jnp.zeros((n_pad - n, D), F32)], axis=0)
                g = g.T[:, :n]
            sg[...] = g
            sd[...], sm[...], sv[...] = _adam_math(wv[...], g, mv[...], vv[...])
            stores = [pltpu.make_async_copy(src, dst.at[l], sems.at[4 + i])
                      for i, (src, dst) in enumerate(((sg, og), (sd, od), (sm, om), (sv, ov)))]
            for cp in stores:
                cp.start(priority=LOCAL_THREAD)
            for cp in stores:
                cp.wait()

    shp = _sds(w.shape, F32)
    tile = pltpu.VMEM((rows, cols), F32)
    args = (w, m, v, land) + (() if after is None else (after,))
    return _pcall(
        body, name="finish_" + name, out_shape=(shp,) * 4,
        in_specs=[_ANY] * len(args), out_specs=(_ANY,) * 4,
        scratch_shapes=[pltpu.VMEM((NCHIP, n, D), BF16)] + [tile] * 7 + [pltpu.SemaphoreType.DMA((8,))],
        compiler_params=_cparams(),
    )(*args)


SMALL_PARAMS = (("sm", SM_B_NORM, 1), ("sm", SM_B_PW1, 2), ("sm", SM_B_CONV, KB), ("sm", SM_B_BCONV, 1),
                ("sm", SM_LN_G, 1), ("sm", SM_LN_B, 1), ("sm", SM_B_PW2, 1), ("sm", SM_A_CONV, KA),
                ("rp", 0, 1), ("rp", 1, 2), ("rp", 3, 1))


def _adamw_small(g_small, g_repl, triples):
    n = len(SMALL_PARAMS)

    def body(*refs):
        gs_ref, gr_ref = refs[0], refs[1]
        ins = refs[2:2 + 3 * n]
        outs = refs[2 + 3 * n:]
        for i, (src, r0, nr) in enumerate(SMALL_PARAMS):
            w_ref, m_ref, v_ref = ins[3 * i:3 * i + 3]
            g = (gs_ref if src == "sm" else gr_ref)[pl.ds(r0, nr), :]
            if i == 1:
                g = jnp.concatenate([g[0:1], g[1:2]], axis=1)
            lead = (0,) if len(w_ref.shape) == 3 else ()
            idx = lead + (slice(None), slice(None))
            vals = (g,) + _adam_math(w_ref[idx], g, m_ref[idx], v_ref[idx])
            for o_ref, val in zip(outs[4 * i:4 * i + 4], vals):
                o_ref[idx] = val

    flat = [a for t in triples for a in t]
    out_shape = tuple(_sds(t[0].shape, F32) for t in triples for _ in range(4))
    vm = pl.BlockSpec(memory_space=pltpu.VMEM)
    res = _pcall(
        body, name="adamw_small", out_shape=out_shape,
        in_specs=[vm] * (2 + len(flat)), out_specs=tuple(vm for _ in out_shape),
        compiler_params=_cparams(),
    )(g_small, g_repl, *flat)
    return [tuple(res[4 * i:4 * i + 4]) for i in range(n)]


def kernel(x, a_norm, a_w_in, a_conv, a_w_out, b_norm, b_w_pw1, b_b_pw1, b_conv, b_b_conv, b_ln_g, b_ln_b, b_w_pw2, b_b_pw2, ffn_norm, ffn_w_gate, ffn_w_up, ffn_w_down, final_norm, loss_target, m_a_norm, m_a_w_in, m_a_conv, m_a_w_out, m_b_norm, m_b_w_pw1, m_b_b_pw1, m_b_conv, m_b_b_conv, m_b_ln_g, m_b_ln_b, m_b_w_pw2, m_b_b_pw2, m_ffn_norm, m_ffn_w_gate, m_ffn_w_up, m_ffn_w_down, m_final_norm, v_a_norm, v_a_w_in, v_a_conv, v_a_w_out, v_b_norm, v_b_w_pw1, v_b_b_pw1, v_b_conv, v_b_b_conv, v_b_ln_g, v_b_ln_b, v_b_w_pw2, v_b_b_pw2, v_ffn_norm, v_ffn_w_gate, v_ffn_w_up, v_ffn_w_down, v_final_norm):
    T = x.shape[1]
    tm = min(TM, T)
    tma = min(TM_A, T)
    tw = min(TM_DW, T)
    xs = x.reshape(T, D)
    tgt = loss_target.reshape(T, D)

    def rows_first(a):
        return jnp.swapaxes(a, 1, 2)

    shard, small = _pack_shard(
        (a_w_in, a_w_out, b_w_pw1, b_w_pw2, rows_first(ffn_w_gate), rows_first(ffn_w_up), ffn_w_down),
        (b_norm, b_b_pw1, b_conv, b_b_conv, b_ln_g, b_ln_b, b_b_pw2, a_conv))
    wg, sm_all = _all_gather_first(shard, small)
    vecs = sm_all.transpose(1, 0, 2).reshape(SM_F32_ROWS, D)
    bias1 = sm_all[:, SM_B_PW1:SM_B_PW1 + 2, :].reshape(1, 2 * D)
    cw_a = vecs[SM_A_CONV:SM_A_CONV + KA]
    fn0, fn1 = ffn_norm[0:1], ffn_norm[1:2]
    fin = final_norm.reshape(1, D)

    h1, bq, cq, vq, ccq, yq, wg = _mixa_fwd(xs, a_norm, cw_a, wg, tma, _ag_comm(shard, wg, ("g0", "u0", "d0")))
    h2, n1, g0, u0, gu0, wg = _ffn_fwd(h1, fn0, wg, 0, tma, _ag_comm(shard, wg, ("b_pw1", "b_pw2"), (T // tma) // 2))
    h3, aq, gq, dcq, sq, wg = _mixb_fwd(h2, vecs, bias1, wg, tm, _ag_comm(shard, wg, ("g1", "u1", "d1"), 1))
    dh4, n3, g1, u1, gu1, st_fin = _ffn_fwd(h3, fn1, wg, 1, tma, None, head=(tgt, fin))

    def by_dest(gw, key):
        return gw.reshape(NDEV, N_ROWS[key], D)

    gw_d1 = by_dest(_grad_w(gu1, dh4, FH, "grad_down1", tw), "d1")
    dh3, dg1, du1, st_f1 = _ffn_bwd_dx(dh4, h3, g1, u1, fn1, wg, 1, tma, None)
    gw_g1 = by_dest(_grad_w(dg1, n3, FH, "grad_gate1", tw), "g1")
    gw_u1 = by_dest(_grad_w(du1, n3, FH, "grad_up1", tw), "u1")
    keys = ("g1", "u1", "d1")
    dh2, st_b, st_b1, gw_pw1, gw_pw2, land = _mixb_bwd(
        dh3, h2, aq, gq, dcq, sq, vecs, wg, tm, _rs_comm(_pair_reduce([gw_g1, gw_u1, gw_d1], keys), keys, None))
    gw_pw1, gw_pw2 = by_dest(gw_pw1, "b_pw1"), by_dest(gw_pw2, "b_pw2")
    gw_d0 = by_dest(_grad_w(gu0, dh2, FH, "grad_down0", tw), "d0")
    keys = ("b_pw1", "b_pw2", "d0")
    dh1, dg0, du0, st_f0, land = _ffn_bwd_dx(
        dh2, h1, g0, u0, fn0, wg, 0, tma, _rs_comm(_pair_reduce([gw_pw1, gw_pw2, gw_d0], keys), keys, land))
    gw_g0 = by_dest(_grad_w(dg0, n1, FH, "grad_gate0", tw), "g0")
    gw_u0 = by_dest(_grad_w(du0, n1, FH, "grad_up0", tw), "u0")
    keys = ("g0", "u0")
    dx, st_a, gw_in, gw_out, land = _mixa_bwd(
        dh1, xs, bq, cq, vq, ccq, yq, a_norm, cw_a, wg, tm,
        _rs_comm(_pair_reduce([gw_g0, gw_u0], keys), keys, land))
    gw_in, gw_out = by_dest(gw_in, "a_in"), by_dest(gw_out, "a_out")

    st_small = st_b.at[SM_A_CONV:SM_A_CONV + KA].set(st_a[1:1 + KA])
    sm_dest = st_small.reshape(SM_F32_ROWS, NDEV, 128).transpose(1, 0, 2)
    sm_dest = sm_dest.at[:, SM_B_PW1:SM_B_PW1 + 2, :].set(st_b1[0].reshape(NDEV, 2, 128))
    repl = jnp.concatenate([st_a[0:1], st_f0[0:1], st_f1[0:1], st_fin[0:1], st_fin[1:2],
                            jnp.zeros((3, D), F32)], axis=0)[None]
    last = ("a_out", "a_in")
    p_last, sm_land, rp_land = _pair_reduce([gw_out, gw_in], last, side=(sm_dest, repl))
    sems, p_last, land_last, token = _chip_start(p_last)

    big = [("ffn_w_gate", ("g0", "g1"), False,
            rows_first(ffn_w_gate), rows_first(m_ffn_w_gate), rows_first(v_ffn_w_gate)),
           ("ffn_w_up", ("u0", "u1"), False, rows_first(ffn_w_up), rows_first(m_ffn_w_up), rows_first(v_ffn_w_up)),
           ("ffn_w_down", ("d0", "d1"), False, ffn_w_down, m_ffn_w_down, v_ffn_w_down),
           ("a_w_out", ("a_out",), False, a_w_out, m_a_w_out, v_a_w_out),
           ("b_w_pw1", ("b_pw1",), True, b_w_pw1, m_b_w_pw1, v_b_w_pw1),
           ("b_w_pw2", ("b_pw2",), False, b_w_pw2, m_b_w_pw2, v_b_w_pw2),
           ("a_w_in", ("a_in",), True, a_w_in, m_a_w_in, v_a_w_in)]
    res, after = {}, token
    for name, keys, transposed, w, m, v in big:
        if keys[0] not in last:
            res[name] = _finish_weight(land, keys, transposed, w, m, v, name, after=after)
            after = res[name][1]
    _, land_last = _chip_wait(sems, p_last, land_last, after)
    land_last = _own_slot(land_last, p_last)
    for name, keys, transposed, w, m, v in big:
        if keys[0] in last:
            res[name] = _finish_weight(land_last, keys, transposed, w, m, v, name, row0=LD_OFF[last[0]])
    for name in ("ffn_w_gate", "ffn_w_up"):
        res[name] = [rows_first(a) for a in res[name]]

    g_small, g_repl, loss8 = _sum_small(sm_land, rp_land)
    loss = loss8[0, 0]

    small_names = ["b_norm", "b_b_pw1", "b_conv", "b_b_conv", "b_ln_g", "b_ln_b", "b_b_pw2", "a_conv",
                   "a_norm", "ffn_norm", "final_norm"]
    triples = [(b_norm, m_b_norm, v_b_norm), (b_b_pw1, m_b_b_pw1, v_b_b_pw1), (b_conv, m_b_conv, v_b_conv),
               (b_b_conv, m_b_b_conv, v_b_b_conv), (b_ln_g, m_b_ln_g, v_b_ln_g), (b_ln_b, m_b_ln_b, v_b_ln_b),
               (b_b_pw2, m_b_b_pw2, v_b_b_pw2), (a_conv, m_a_conv, v_a_conv), (a_norm, m_a_norm, v_a_norm),
               (ffn_norm, m_ffn_norm, v_ffn_norm),
               (fin, m_final_norm.reshape(1, D), v_final_norm.reshape(1, D))]
    for name, quad in zip(small_names, _adamw_small(g_small, g_repl, triples)):
        res[name] = quad
    res["final_norm"] = tuple(a.reshape(D) for a in res["final_norm"])

    order = ["a_norm", "a_w_in", "a_conv", "a_w_out", "b_norm", "b_w_pw1", "b_b_pw1", "b_conv", "b_b_conv",
             "b_ln_g", "b_ln_b", "b_w_pw2", "b_b_pw2", "ffn_norm", "ffn_w_gate", "ffn_w_up", "ffn_w_down",
             "final_norm"]
    out = [loss, dx.reshape(1, T, D)]
    for j in range(4):
        out += [res[k][j] for k in order]
    return tuple(out)
```

```python
import functools

import jax
import jax.numpy as jnp
from jax import lax
from jax.experimental import pallas as pl
from jax.experimental.pallas import tpu as pltpu

F32 = jnp.float32
BF16 = jnp.bfloat16
MESH = pl.DeviceIdType.MESH

D = 1024
FF = 2816
FH = FF // 2
NDEV = 8
KA = 3
KB = 31
HALO_A = 8
HALO_B = 32
CONV_ROWS = 64
RMS_EPS = 1e-6
LN_EPS = 1e-5
LR, B1, B2, ADAM_EPS, WD, STEP = 0.001, 0.9, 0.999, 1e-08, 0.01, 10

TM = 256
TM_A = 512
TM_DW = 2048
VMEM_LIMIT = 62 * 1024 * 1024
LOCAL_THREAD = 1

N_ROWS = {"a_in": 384, "a_out": 128, "b_pw1": 256, "b_pw2": 128, "g0": 352, "u0": 352, "d0": 352,
          "g1": 352, "u1": 352, "d1": 352}
NCHIP = 4


def _offsets(order):
    off, o = {}, 0
    for k in order:
        off[k] = o
        o += N_ROWS[k]
    return off, o


PK_ORDER = ("a_in", "a_out", "g0", "u0", "d0", "b_pw1", "b_pw2", "g1", "u1", "d1")
PK_OFF, PK_ROWS = _offsets(PK_ORDER)
LD_ORDER = ("g1", "u1", "d1", "b_pw1", "b_pw2", "d0", "g0", "u0", "a_out", "a_in")
LD_OFF, N_WROWS = _offsets(LD_ORDER)


def _span(off, keys):
    return off[keys[0]], sum(N_ROWS[k] for k in keys)


SM_B_NORM, SM_B_PW1, SM_B_CONV, SM_B_BCONV, SM_LN_G, SM_LN_B, SM_B_PW2, SM_A_CONV = 0, 1, 3, 34, 35, 36, 37, 38
SM_F32_ROWS = 64


def _pcall(body, **kw):
    return pl.pallas_call(body, **kw)


def _cparams(sem=None):
    return pltpu.CompilerParams(dimension_semantics=sem, vmem_limit_bytes=VMEM_LIMIT)


def _dot(a, b):
    return jnp.dot(a, b, preferred_element_type=F32)


def _dot_nt(a, b):
    return lax.dot_general(a, b, (((1,), (1,)), ((), ())), preferred_element_type=F32)


def _dot_tn(a, b):
    return lax.dot_general(a, b, (((0,), (0,)), ((), ())), preferred_element_type=F32)


def _sigmoid(v):
    return 1.0 / (1.0 + jnp.exp(-v))


def _rms_stat(x):
    return lax.rsqrt(jnp.mean(x * x, axis=-1, keepdims=True) + RMS_EPS)


def _rms_bwd(dn, x, r, gamma):
    dng = dn * gamma
    dx = r * dng - x * (r * r * r) * jnp.mean(dng * x, axis=-1, keepdims=True)
    return dx, jnp.sum(dn * x * r, axis=0, keepdims=True)


def _weight_loads(wg_ref, plan, sems):
    copies = []
    for j, (dst, key) in enumerate(plan):
        off, n = PK_OFF[key], N_ROWS[key]
        copies += [pltpu.make_async_copy(
            wg_ref.at[d, pl.ds(off, n), :], dst.at[pl.ds(d * n, n), :], sems.at[j * NDEV + d])
            for d in range(NDEV)]

    def load():
        for cp in copies:
            cp.start()
        for cp in copies:
            cp.wait()

    return load


_ANY = pl.BlockSpec(memory_space=pl.ANY)


def _row_spec(tm, width, rev_nt=None):
    if rev_nt is None:
        return pl.BlockSpec((tm, width), lambda i: (i, 0))
    return pl.BlockSpec((tm, width), lambda i: (rev_nt - 1 - i, 0))


def _full_spec(shape):
    return pl.BlockSpec(shape, lambda *_: (0,) * len(shape))


def _sds(shape, dtype):
    return jax.ShapeDtypeStruct(shape, dtype)


def _mesh_pos():
    return lax.axis_index("x"), lax.axis_index("y"), lax.axis_index("c")


def _lin(p):
    return 4 * p[0] + 2 * p[1] + p[2]


def _ag_exchange(src, slot, send_sems, recv_sems, local_sem):
    x, y, c = _mesh_pos()
    me, sibling = (x, y, c), (x, y, 1 - c)
    chips = [(1 - x, y), (x, 1 - y), (1 - x, 1 - y)]

    def copy(k, block, to, own=False):
        return pltpu.make_async_remote_copy(
            src_ref=src if own else slot(block), dst_ref=slot(block),
            send_sem=send_sems.at[k], recv_sem=recv_sems.at[k], device_id=to, device_id_type=MESH)

    mine = pltpu.make_async_copy(src, slot(me), local_sem)
    first = [copy(0, me, sibling, own=True)]
    first += [copy(1 + j, me, (*chip, c), own=True) for j, chip in enumerate(chips)]
    passed = [copy(4 + j, (*chip, c), sibling) for j, chip in enumerate(chips)]

    def start():
        mine.start()
        for cp in first:
            cp.start()

    def forward():
        for j, chip in enumerate(chips):
            copy(1 + j, (*chip, c), me).wait_recv()
            passed[j].start()

    def finish():
        copy(0, sibling, me).wait_recv()
        for j, chip in enumerate(chips):
            copy(4 + j, (*chip, 1 - c), me).wait_recv()
        for cp in first + passed:
            cp.wait_send()
        mine.wait()

    return start, forward, finish


def _chip_exchange(p_ref, land_ref, send_sems, recv_sems, keys):
    x, y, c = _mesh_pos()
    l0, rows = _span(LD_OFF, keys)
    dst = land_ref.at[2 * x + y, pl.ds(l0, rows), :]
    peers = [((x + (j >> 1)) % 2, (y + (j & 1)) % 2) for j in range(NCHIP)]

    def copy(j):
        tx, ty = peers[j]
        src = p_ref.at[2 * tx + ty]
        if j == 0:
            return pltpu.make_async_copy(src, dst, send_sems.at[0])
        return pltpu.make_async_remote_copy(
            src_ref=src, dst_ref=dst, send_sem=send_sems.at[j], recv_sem=recv_sems.at[j],
            device_id=(tx, ty, c), device_id_type=MESH)

    def start():
        for j in range(NCHIP):
            copy(j).start()

    def finish():
        for j in range(NCHIP):
            copy(j).wait()

    return start, finish


def _all_to_all_f32(src_for, dst_ref, send_sems, recv_sems):
    x, y, c = _mesh_pos()
    dst = dst_ref.at[_lin((x, y, c))]
    peers = [((x + ((j >> 2) & 1)) % 2, (y + ((j >> 1) & 1)) % 2, (c + (j & 1)) % 2) for j in range(NDEV)]

    def copy(j):
        if j == 0:
            return pltpu.make_async_copy(src_for(_lin(peers[0])), dst, send_sems.at[0])
        return pltpu.make_async_remote_copy(
            src_ref=src_for(_lin(peers[j])), dst_ref=dst, send_sem=send_sems.at[j], recv_sem=recv_sems.at[j],
            device_id=peers[j], device_id_type=MESH)

    def start():
        for j in range(NDEV):
            copy(j).start()

    def finish():
        for j in range(NDEV):
            copy(j).wait()

    return start, finish


class _Comm:
    def __init__(self, ins, alias_in, out_shape, scratch, make, gives_wg, middle_step=0):
        self.ins, self.alias_in, self.out_shape = ins, alias_in, out_shape
        self.scratch, self.make, self.gives_wg, self.middle_step = scratch, make, gives_wg, middle_step


def _ag_comm(shard, wg, keys, forward_step=0):
    def make(c_ins, c_out, sc):
        return _weights_exchange(c_ins[0], c_out, sc[0], sc[1], sc[2], keys)

    return _Comm([shard, wg], 1, _sds(wg.shape, BF16),
                 [pltpu.SemaphoreType.DMA((7,)), pltpu.SemaphoreType.DMA((7,)), pltpu.SemaphoreType.DMA],
                 make, True, forward_step)


def _weights_exchange(shard_ref, wg_ref, send_sems, recv_sems, local_sem, keys):
    r0, nr = _span(PK_OFF, keys)
    return _ag_exchange(shard_ref.at[pl.ds(r0, nr), :], lambda p: wg_ref.at[_lin(p), pl.ds(r0, nr), :],
                        send_sems, recv_sems, local_sem)


def _rs_comm(psum, keys, land):
    def make(c_ins, c_out, sc):
        start, finish = _chip_exchange(c_ins[0], c_out, sc[0], sc[1], keys)
        return start, None, finish

    ins = [psum] + ([] if land is None else [land])
    return _Comm(ins, None if land is None else 1, _sds((NCHIP, N_WROWS, D), BF16),
                 [pltpu.SemaphoreType.DMA((NCHIP,)), pltpu.SemaphoreType.DMA((NCHIP,))], make, False)


def _hosted_call(body, name, nt, in_specs, out_specs, out_shape, scratch, args, comm):
    if comm is None:
        return _pcall(body, name=name, grid=(nt,), in_specs=in_specs, out_specs=tuple(out_specs),
                      out_shape=tuple(out_shape), scratch_shapes=scratch,
                      compiler_params=_cparams(("arbitrary",)))(*args)
    n_in, n_out, n_sc, n_cin = len(in_specs), len(out_specs), len(scratch), len(comm.ins)

    def wrapped(*refs):
        ins = refs[:n_in]
        c_ins = refs[n_in:n_in + n_cin]
        outs = refs[n_in + n_cin:n_in + n_cin + n_out]
        c_out = refs[n_in + n_cin + n_out]
        sc = refs[n_in + n_cin + n_out + 1:n_in + n_cin + n_out + 1 + n_sc]
        c_sc = refs[n_in + n_cin + n_out + 1 + n_sc:]
        start, middle, finish = comm.make(c_ins, c_out, c_sc)
        pl.when(pl.program_id(0) == 0)(start)
        if comm.gives_wg:
            body(*ins, c_out, *outs, *sc)
        else:
            body(*ins, *outs, *sc)
        if middle is not None:
            pl.when(pl.program_id(0) == max(nt - 1 - comm.middle_step, 0))(middle)
        pl.when(pl.program_id(0) == nt - 1)(finish)

    aliases = {} if comm.alias_in is None else {n_in + comm.alias_in: n_out}
    res = _pcall(wrapped, name=name, grid=(nt,),
                 in_specs=list(in_specs) + [_ANY] * n_cin, out_specs=tuple(out_specs) + (_ANY,),
                 out_shape=tuple(out_shape) + (comm.out_shape,),
                 scratch_shapes=list(scratch) + list(comm.scratch),
                 input_output_aliases=aliases,
                 compiler_params=_cparams(("arbitrary",)))(*args, *comm.ins)
    return res


def _pack_shard(weights, smalls):
    plan = (("a_in", 0, 0, True), ("a_out", 1, 0, False), ("b_pw1", 2, 0, True), ("b_pw2", 3, 0, False),
            ("g0", 4, 0, False), ("u0", 5, 0, False), ("d0", 6, 0, False),
            ("g1", 4, 1, False), ("u1", 5, 1, False), ("d1", 6, 1, False))
    n_pad = 384

    def body(*refs):
        out, sm, pad = refs[-3:]
        for key, idx, layer, transposed in plan:
            n = N_ROWS[key]
            val = refs[idx][layer]
            if transposed:
                if n % 128:
                    pad[:, pl.ds(0, n)] = val
                    pad[:, pl.ds(n, n_pad - n)] = jnp.zeros((D, n_pad - n), F32)
                    val = pad[...]
                val = val.T[:n]
            out[pl.ds(PK_OFF[key], n), :] = val.astype(BF16)
        b_norm, b_b_pw1, b_conv, b_b_conv, b_ln_g, b_ln_b, b_b_pw2, a_conv = refs[len(weights):-3]
        sm[...] = jnp.zeros((SM_F32_ROWS, 128), F32)
        sm[pl.ds(SM_B_PW1, 1), :] = b_b_pw1[:, pl.ds(0, 128)]
        sm[pl.ds(SM_B_PW1 + 1, 1), :] = b_b_pw1[:, pl.ds(128, 128)]
        sm[pl.ds(SM_B_CONV, KB), :] = b_conv[0]
        sm[pl.ds(SM_A_CONV, KA), :] = a_conv[0]
        for row, ref in ((SM_B_NORM, b_norm), (SM_B_BCONV, b_b_conv), (SM_LN_G, b_ln_g), (SM_LN_B, b_ln_b),
                         (SM_B_PW2, b_b_pw2)):
            sm[pl.ds(row, 1), :] = ref[...]

    vm = pl.BlockSpec(memory_space=pltpu.VMEM)
    return _pcall(
        body, name="pack_shard",
        out_shape=(_sds((PK_ROWS, D), BF16), _sds((SM_F32_ROWS, 128), F32)),
        in_specs=[vm] * (len(weights) + len(smalls)), out_specs=(vm, vm),
        scratch_shapes=[pltpu.VMEM((D, n_pad), F32)],
        compiler_params=_cparams(),
    )(*weights, *smalls)


def _all_gather_first(shard, small):
    def body(x_ref, s_ref, wg_ref, sg_ref, send_w, recv_w, local_w, send_s, recv_s, local_s):
        start_w, forward_w, finish_w = _weights_exchange(x_ref, wg_ref, send_w, recv_w, local_w, ("a_in", "a_out"))
        start_s, forward_s, finish_s = _ag_exchange(s_ref, lambda p: sg_ref.at[_lin(p)], send_s, recv_s, local_s)
        start_s()
        start_w()
        forward_s()
        forward_w()
        finish_s()
        finish_w()

    sems = [pltpu.SemaphoreType.DMA((7,)), pltpu.SemaphoreType.DMA((7,)), pltpu.SemaphoreType.DMA]
    return _pcall(
        body, name="all_gather_first",
        out_shape=(_sds((NDEV, PK_ROWS, D), BF16), _sds((NDEV, SM_F32_ROWS, 128), F32)),
        in_specs=[_ANY, _ANY], out_specs=(_ANY, _ANY), scratch_shapes=sems + sems,
    )(shard, small)


def _pair_reduce(grads, keys, side=None, land=None):
    l0, rows = _span(LD_OFF, keys)
    nk = len(keys)
    n_side = 0 if side is None else 2
    n_land = 0 if land is None else 1

    def body(c_ref, *refs):
        mine = refs[:nk]
        whole = refs[nk:2 * nk]
        n_in = 2 * nk + n_side + n_land
        o_ref = refs[n_in]
        got, send_sem, recv_sems = refs[n_in + 1 + n_side:n_in + 4 + n_side]
        t = pl.program_id(0)
        x, y, c = _mesh_pos()
        sibling = (x, y, 1 - c)
        if side is not None:
            sm_ref, rp_ref = refs[2 * nk:2 * nk + 2]
            sm_land, rp_land = refs[n_in + 1:n_in + 3]
            s2, r2, s3, r3 = refs[n_in + 6:n_in + 10]
            start_s, finish_s = _all_to_all_f32(lambda d: sm_ref.at[d], sm_land, s2, r2)
            start_r, finish_r = _all_to_all_f32(lambda d: rp_ref.at[0], rp_land, s3, r3)

        @pl.when(t == 0)
        def _():
            if side is not None:
                start_s()
                start_r()
            for ref, key in zip(whole, keys):
                for d in range(NCHIP):
                    pltpu.make_async_remote_copy(
                        src_ref=ref.at[2 * d + (1 - c)], dst_ref=got.at[d, pl.ds(LD_OFF[key] - l0, N_ROWS[key]), :],
                        send_sem=send_sem, recv_sem=recv_sems.at[d], device_id=sibling, device_id_type=MESH).start()

        pltpu.make_async_remote_copy(src_ref=got.at[t], dst_ref=got.at[t], send_sem=send_sem,
                                     recv_sem=recv_sems.at[t], device_id=sibling, device_id_type=MESH).wait_recv()
        sums = o_ref if land is None else refs[-3].at[t]
        for ref, key in zip(mine, keys):
            sl = pl.ds(LD_OFF[key] - l0, N_ROWS[key])
            sums[sl, :] = (ref[...].astype(F32) + got[t, sl, :].astype(F32)).astype(BF16)

        if land is not None:
            psum, chip_send, chip_recv = refs[-3:]
            my_chip = 2 * x + y
            dst = o_ref.at[my_chip, pl.ds(l0, rows), :]
            hop = jnp.bitwise_xor(t, my_chip)

            @pl.when(t == my_chip)
            def _():
                pltpu.make_async_copy(psum.at[t], dst, chip_send.at[0]).start()

            @pl.when(t != my_chip)
            def _():
                pltpu.make_async_remote_copy(
                    src_ref=psum.at[t], dst_ref=dst, send_sem=chip_send.at[hop], recv_sem=chip_recv.at[hop],
                    device_id=(t // 2, t % 2, c), device_id_type=MESH).start()

        @pl.when(t == NCHIP - 1)
        def _():
            pltpu.make_async_remote_copy(src_ref=got, dst_ref=got, send_sem=send_sem, recv_sem=recv_sems.at[0],
                                         device_id=sibling, device_id_type=MESH).wait_send()
            if land is not None:
                one = o_ref.at[0, pl.ds(l0, rows), :]
                pltpu.make_async_copy(one, one, chip_send.at[0]).wait()
                for j in range(1, NCHIP):
                    pltpu.make_async_remote_copy(src_ref=one, dst_ref=one, send_sem=chip_send.at[j],
                                                 recv_sem=chip_recv.at[j], device_id=sibling, device_id_type=MESH).wait()
            if side is not None:
                finish_s()
                finish_r()

    if land is None:
        out_specs = [pl.BlockSpec((None, rows, D), lambda t, c: (t, 0, 0))]
        out_shape = [_sds((NCHIP, rows, D), BF16)]
    else:
        out_specs, out_shape = [_ANY], [_sds((NCHIP, N_WROWS, D), BF16)]
    scratch = [pltpu.VMEM((NCHIP, rows, D), BF16), pltpu.SemaphoreType.DMA, pltpu.SemaphoreType.DMA((NCHIP,))]
    if side is not None:
        out_specs += [_ANY, _ANY]
        out_shape += [_sds((NDEV, SM_F32_ROWS, 128), F32), _sds((NDEV, 8, D), F32)]
        scratch += [pltpu.SemaphoreType.DMA((NDEV,))] * 4
    if land is not None:
        scratch += [pltpu.VMEM((NCHIP, rows, D), BF16), pltpu.SemaphoreType.DMA((NCHIP,)),
                    pltpu.SemaphoreType.DMA((NCHIP,))]
    grid_spec = pltpu.PrefetchScalarGridSpec(
        num_scalar_prefetch=1, grid=(NCHIP,),
        in_specs=[pl.BlockSpec((None, N_ROWS[k], D), lambda t, c: (2 * t + c[0], 0, 0)) for k in keys]
        + [_ANY] * (nk + n_side + n_land),
        out_specs=tuple(out_specs), scratch_shapes=scratch)
    core = lax.axis_index("c").astype(jnp.int32).reshape(1)
    aliases = {} if land is None else {1 + 2 * nk + n_side: 0}
    res = _pcall(
        body, name="pair_reduce_" + keys[0], grid_spec=grid_spec, out_shape=tuple(out_shape),
        input_output_aliases=aliases, compiler_params=_cparams(("arbitrary",)),
    )(core, *grads, *grads, *(side or ()), *(() if land is None else (land,)))
    return res[0] if len(res) == 1 else res


def _chip_start(psum):
    hbm = pl.BlockSpec(memory_space=pltpu.HBM)
    sem = pl.BlockSpec(memory_space=pltpu.SEMAPHORE)

    def body(p_ref, land_ref, s1, s2, s3, r1, r2, r3, p_thru, land_thru, token):
        x, y, c = _mesh_pos()
        for j, (send_sem, recv_sem) in enumerate(((s1, r1), (s2, r2), (s3, r3)), start=1):
            tx, ty = (x + (j >> 1)) % 2, (y + (j & 1)) % 2
            pltpu.make_async_remote_copy(
                src_ref=p_ref.at[2 * tx + ty], dst_ref=land_ref.at[2 * x + y], send_sem=send_sem, recv_sem=recv_sem,
                device_id=(tx, ty, c), device_id_type=MESH).start()
        token[...] = jnp.zeros_like(token)

    dma = pltpu.SemaphoreType.DMA(())
    buf = pltpu.HBM(psum.shape, psum.dtype)
    res = _pcall(
        body, name="chip_exchange_start",
        out_shape=(dma,) * 6 + (buf, buf, _sds((8, 128), F32)),
        in_specs=(hbm, hbm), out_specs=(sem,) * 6 + (hbm, hbm, pl.BlockSpec(memory_space=pltpu.VMEM)),
        input_output_aliases={0: 6, 1: 7},
        compiler_params=pltpu.CompilerParams(has_side_effects=pltpu.SideEffectType.DATAFLOW_SIDE_EFFECTING),
    )(pltpu.with_memory_space_constraint(psum, pltpu.HBM),
      pltpu.with_memory_space_constraint(lax.empty(psum.shape, psum.dtype), pltpu.HBM))
    return res[:6], res[6], res[7], res[8]


def _chip_wait(sems, p_thru, land_thru, after):
    hbm = pl.BlockSpec(memory_space=pltpu.HBM)
    sem = pl.BlockSpec(memory_space=pltpu.SEMAPHORE)

    def body(p_ref, land_ref, s1, s2, s3, r1, r2, r3, after_ref, p_dead, got_ref):
        x, y, c = _mesh_pos()
        for send_sem, recv_sem in ((s1, r1), (s2, r2), (s3, r3)):
            copy = pltpu.make_async_remote_copy(
                src_ref=p_ref.at[0], dst_ref=land_ref.at[0], send_sem=send_sem, recv_sem=recv_sem,
                device_id=(x, y, 1 - c), device_id_type=MESH)
            copy.wait_send()
            copy.wait_recv()

    buf = pltpu.HBM(p_thru.shape, p_thru.dtype)
    return _pcall(
        body, name="chip_exchange_wait", out_shape=(buf, buf),
        in_specs=(hbm, hbm) + (sem,) * 6 + (_ANY,), out_specs=(hbm, hbm),
        input_output_aliases={0: 0, 1: 1},
        compiler_params=pltpu.CompilerParams(has_side_effects=pltpu.SideEffectType.DATAFLOW_SIDE_EFFECTING),
    )(p_thru, land_thru, *sems, after)


def _own_slot(land, psum):
    def body(land_in, p_ref, land_ref, sem):
        x, y, _ = _mesh_pos()
        cp = pltpu.make_async_copy(p_ref.at[2 * x + y], land_ref.at[2 * x + y], sem)
        cp.start()
        cp.wait()

    return _pcall(
        body, name="own_slot", out_shape=_sds(land.shape, land.dtype),
        in_specs=[_ANY, _ANY], out_specs=_ANY, input_output_aliases={0: 0},
        scratch_shapes=[pltpu.SemaphoreType.DMA],
    )(land, psum)


def _mixa_fwd(x, gam, cw, wg, tm, comm):
    T = x.shape[0]

    def body(x_ref, gam_ref, cw_ref, wg_ref, h_ref, b_ref, c_ref, v_ref, cc_ref, y_ref,
             win, wout, buf, sems):
        first = pl.program_id(0) == 0
        load_w = _weight_loads(wg_ref, [(win, "a_in"), (wout, "a_out")], sems)

        @pl.when(first)
        def _():
            load_w()
            buf[pl.ds(0, HALO_A), :] = jnp.zeros((HALO_A, D), F32)

        xv = x_ref[...]
        nb = (xv * _rms_stat(xv) * gam_ref[...]).astype(BF16)
        bv = _dot_nt(nb, win[pl.ds(0, D), :])
        cval = _dot_nt(nb, win[pl.ds(D, D), :])
        vval = _dot_nt(nb, win[pl.ds(2 * D, D), :])
        cv = cval * vval
        buf[pl.ds(HALO_A, tm), :] = cv
        cc = cw_ref[pl.ds(KA - 1, 1), :] * cv
        for k in range(KA - 1):
            cc = cc + cw_ref[pl.ds(k, 1), :] * buf[pl.ds(HALO_A - (KA - 1) + k, tm), :]
        buf[pl.ds(0, HALO_A), :] = buf[pl.ds(tm, HALO_A), :]
        yb = (bv * cc).astype(BF16)
        b_ref[...] = bv.astype(BF16)
        c_ref[...] = cval.astype(BF16)
        v_ref[...] = vval.astype(BF16)
        cc_ref[...] = cc.astype(BF16)
        y_ref[...] = yb
        h_ref[...] = xv + _dot(yb, wout[...])

    act = _sds((T, D), BF16)
    return _hosted_call(
        body, "mixa_fwd", T // tm,
        in_specs=[_row_spec(tm, D), _full_spec((1, D)), _full_spec((KA, D))],
        out_specs=[_row_spec(tm, D) for _ in range(6)],
        out_shape=(_sds((T, D), F32),) + (act,) * 5,
        scratch=[pltpu.VMEM((3 * D, D), BF16), pltpu.VMEM((D, D), BF16),
                 pltpu.VMEM((HALO_A + tm, D), F32), pltpu.SemaphoreType.DMA((2 * NDEV,))],
        args=(x, gam, cw), comm=comm)


def _ffn_fwd(h, gam, wg, layer, tm, comm, head=None):
    T = h.shape[0]
    kg, ku, kd = "g%d" % layer, "u%d" % layer, "d%d" % layer
    n_head = 0 if head is None else 2

    def body(*refs):
        h_ref, gam_ref = refs[:2]
        wg_ref = refs[2 + n_head]
        o_ref, n_ref, g_ref, u_ref, gu_ref = refs[3 + n_head:8 + n_head]
        wgt, wut, wd, sems = refs[-4:]

        first = pl.program_id(0) == 0
        load_w = _weight_loads(wg_ref, [(wgt, kg), (wut, ku), (wd, kd)], sems)
        pl.when(first)(load_w)

        hv = h_ref[...]
        nb = (hv * _rms_stat(hv) * gam_ref[...]).astype(BF16)
        n_ref[...] = nb
        out = hv
        for f in range(2):
            cols = pl.ds(f * FH, FH)
            g = _dot_nt(nb, wgt[cols, :])
            u = _dot_nt(nb, wut[cols, :])
            gu = (g * _sigmoid(g) * u).astype(BF16)
            g_ref[:, cols] = g.astype(BF16)
            u_ref[:, cols] = u.astype(BF16)
            gu_ref[:, cols] = gu
            out = out + _dot(gu, wd[cols, :])
        if head is None:
            o_ref[...] = out
        else:
            t_ref, fin_ref, st_ref = refs[2], refs[3], refs[8 + n_head]

            @pl.when(pl.program_id(0) == 0)
            def _():
                st_ref[...] = jnp.zeros((8, D), F32)

            gamma = fin_ref[...]
            r = _rms_stat(out)
            err = out * r * gamma - t_ref[...]
            dx, dgam = _rms_bwd(err * (1.0 / D), out, r, gamma)
            o_ref[...] = dx
            st_ref[pl.ds(0, 1), :] += dgam
            st_ref[pl.ds(1, 1), :] += (0.5 / D) * jnp.sum(err * err, axis=0, keepdims=True)

    pre = _sds((T, FF), BF16)
    in_specs = [_row_spec(tm, D), _full_spec((1, D))]
    args = (h, gam)
    out_specs = [_row_spec(tm, D), _row_spec(tm, D), _row_spec(tm, FF), _row_spec(tm, FF), _row_spec(tm, FF)]
    out_shape = (_sds((T, D), F32), _sds((T, D), BF16), pre, pre, pre)
    if head is not None:
        in_specs, args = in_specs + [_row_spec(tm, D), _full_spec((1, D))], args + tuple(head)
        out_specs, out_shape = out_specs + [_full_spec((8, D))], out_shape + (_sds((8, D), F32),)
    if comm is None:
        in_specs, args = in_specs + [_ANY], args + (wg,)
    return _hosted_call(
        body, "ffn%d_fwd" % layer, T // tm,
        in_specs=in_specs, out_specs=out_specs, out_shape=out_shape,
        scratch=[pltpu.VMEM((FF, D), BF16)] * 3 + [pltpu.SemaphoreType.DMA((3 * NDEV,))],
        args=args, comm=comm)


def _shifted_copies(buf, shf, tm):
    for r in range(1, 8):
        shf[r - 1] = buf[pl.ds(r, tm + HALO_B - 8), :]


def _broadcast_taps(vec_ref, wb):
    for k in range(KB):
        wb[k] = jnp.broadcast_to(vec_ref[pl.ds(SM_B_CONV + k, 1), :], (8, D))
    wb[KB] = jnp.broadcast_to(vec_ref[pl.ds(SM_B_BCONV, 1), :], (8, D))


def _taps_by_shift_residue(taps):
    groups = {}
    for k, shift in taps:
        q, r = divmod(shift, 8)
        groups.setdefault(r, []).append((k, q))
    return sorted(groups.items())


def _window(buf, shf, base, r, q0, n_groups, lanes):
    rows = pl.ds(base + 8 * q0, 8 * n_groups)
    v = buf[rows, lanes] if r == 0 else shf[r - 1, rows, lanes]
    return [v[8 * i:8 * i + 8] for i in range(n_groups)]


def _long_conv(buf, shf, wb, out_ref, tm, taps, bias_row):
    n_acc = CONV_ROWS // 8
    groups = _taps_by_shift_residue(taps)
    for col in range(D // 128):
        lanes = pl.ds(128 * col, 128)

        def rows(i, carry, lanes=lanes):
            base = i * CONV_ROWS
            init = jnp.zeros((8, 128), F32) if bias_row is None else wb[bias_row, :, lanes]
            accs = [init] * n_acc
            for r, lst in groups:
                q0, q1 = min(q for _, q in lst), max(q for _, q in lst)
                win = _window(buf, shf, base, r, q0, n_acc + q1 - q0, lanes)
                for k, q in lst:
                    wk = wb[k, :, lanes]
                    accs = [acc + wk * win[h + q - q0] for h, acc in enumerate(accs)]
            for h, acc in enumerate(accs):
                out_ref[pl.ds(base + 8 * h, 8), lanes] = acc
            return carry

        for i in range(tm // CONV_ROWS):
            rows(i, 0)


def _long_conv_grad_taps(buf, shf, x_ref, wacc, tm):
    n_acc = CONV_ROWS // 8
    groups = _taps_by_shift_residue([(KB - 1 - j, j) for j in range(KB)])
    for col in range(D // 128):
        lanes = pl.ds(128 * col, 128)

        def rows(i, carry, lanes=lanes):
            base = i * CONV_ROWS
            xv = x_ref[pl.ds(base, CONV_ROWS), lanes]
            xs = [xv[8 * h:8 * h + 8] for h in range(n_acc)]
            for r, lst in groups:
                q0, q1 = min(q for _, q in lst), max(q for _, q in lst)
                win = _window(buf, shf, base, r, q0, n_acc + q1 - q0, lanes)
                for k, q in lst:
                    prod = [x * win[h + q - q0] for h, x in enumerate(xs)]
                    wacc[k, :, lanes] += functools.reduce(lambda a, b: a + b, prod)
            return carry

        for i in range(tm // CONV_ROWS):
            rows(i, 0)


def _ln_stats(dc):
    mu = jnp.mean(dc, axis=-1, keepdims=True)
    xc = dc - mu
    rstd = lax.rsqrt(jnp.mean(xc * xc, axis=-1, keepdims=True) + LN_EPS)
    return xc * rstd, rstd


def _mixb_fwd(h, vecs, bias1, wg, tm, comm):
    T = h.shape[0]

    def body(h_ref, vec_ref, b1_ref, wg_ref, o_ref, a_ref, g_ref, dc_ref, s_ref, w1, w2, buf, shf, wb, sems):
        first = pl.program_id(0) == 0
        load_w = _weight_loads(wg_ref, [(w1, "b_pw1"), (w2, "b_pw2")], sems)

        @pl.when(first)
        def _():
            load_w()
            buf[pl.ds(0, HALO_B), :] = jnp.zeros((HALO_B, D), F32)
            _broadcast_taps(vec_ref, wb)

        hv = h_ref[...]
        nb = (hv * _rms_stat(hv) * vec_ref[pl.ds(SM_B_NORM, 1), :]).astype(BF16)
        a = _dot_nt(nb, w1[pl.ds(0, D), :]) + b1_ref[:, pl.ds(0, D)]
        g = _dot_nt(nb, w1[pl.ds(D, D), :]) + b1_ref[:, pl.ds(D, D)]
        a_ref[...] = a.astype(BF16)
        g_ref[...] = g.astype(BF16)
        buf[pl.ds(HALO_B, tm), :] = a * _sigmoid(g)
        _shifted_copies(buf, shf, tm)

        _long_conv(buf, shf, wb, dc_ref, tm, [(k, HALO_B - (KB - 1) + k) for k in range(KB)], KB)
        buf[pl.ds(0, HALO_B), :] = buf[pl.ds(tm, HALO_B), :]
        xhat, _ = _ln_stats(dc_ref[...])
        ln = xhat * vec_ref[pl.ds(SM_LN_G, 1), :] + vec_ref[pl.ds(SM_LN_B, 1), :]
        s = (ln * _sigmoid(ln)).astype(BF16)
        s_ref[...] = s
        o_ref[...] = hv + _dot(s, w2[...]) + vec_ref[pl.ds(SM_B_PW2, 1), :]

    act = _sds((T, D), BF16)
    return _hosted_call(
        body, "mixb_fwd", T // tm,
        in_specs=[_row_spec(tm, D), _full_spec((SM_F32_ROWS, D)), _full_spec((1, 2 * D))],
        out_specs=[_row_spec(tm, D) for _ in range(5)],
        out_shape=(_sds((T, D), F32), act, act, _sds((T, D), F32), act),
        scratch=[pltpu.VMEM((2 * D, D), BF16), pltpu.VMEM((D, D), BF16),
                 pltpu.VMEM((HALO_B + tm, D), F32), pltpu.VMEM((7, HALO_B + tm - 8, D), F32),
                 pltpu.VMEM((KB + 1, 8, D), F32), pltpu.SemaphoreType.DMA((2 * NDEV,))],
        args=(h, vecs, bias1), comm=comm)


def _ffn_bwd_dx(dh, h, g, u, gam, wg, layer, tm, comm):
    T = h.shape[0]
    kg, ku, kd = "g%d" % layer, "u%d" % layer, "d%d" % layer

    def body(dh_ref, h_ref, g_ref, u_ref, gam_ref, wg_ref, o_ref, dg_ref, du_ref, st_ref, wgt, wut, wd, sems):
        first = pl.program_id(0) == 0
        load_w = _weight_loads(wg_ref, [(wd, kd), (wgt, kg), (wut, ku)], sems)

        @pl.when(first)
        def _():
            load_w()
            st_ref[...] = jnp.zeros((8, D), F32)

        dhv = dh_ref[...]
        dhb = dhv.astype(BF16)
        dn = jnp.zeros_like(dhv)
        for f in range(2):
            cols = pl.ds(f * FH, FH)
            dgu = _dot_nt(dhb, wd[cols, :])
            gv = g_ref[:, cols].astype(F32)
            sg = _sigmoid(gv)
            du = (dgu * gv * sg).astype(BF16)
            dg = (dgu * u_ref[:, cols].astype(F32) * (sg * (1.0 + gv * (1.0 - sg)))).astype(BF16)
            dg_ref[:, cols] = dg
            du_ref[:, cols] = du
            dn = dn + _dot(dg, wgt[cols, :]) + _dot(du, wut[cols, :])
        hv = h_ref[...]
        dx, dgam = _rms_bwd(dn, hv, _rms_stat(hv), gam_ref[...])
        o_ref[...] = dhv + dx
        st_ref[pl.ds(0, 1), :] += dgam

    pre = _sds((T, FF), BF16)
    return _hosted_call(
        body, "ffn%d_bwd_dx" % layer, T // tm,
        in_specs=[_row_spec(tm, D), _row_spec(tm, D), _row_spec(tm, FF), _row_spec(tm, FF),
                  _full_spec((1, D)), _ANY],
        out_specs=[_row_spec(tm, D), _row_spec(tm, FF), _row_spec(tm, FF), _full_spec((8, D))],
        out_shape=(_sds((T, D), F32), pre, pre, _sds((8, D), F32)),
        scratch=[pltpu.VMEM((FF, D), BF16)] * 3 + [pltpu.SemaphoreType.DMA((3 * NDEV,))],
        args=(dh, h, g, u, gam, wg), comm=comm)


def _grad_w(lhs, rhs, mc, name, tm):
    T, M = lhs.shape
    nt = T // tm

    def body(l_ref, r_ref, o_ref, acc):
        i = pl.program_id(1)

        @pl.when(i == 0)
        def _():
            acc[...] = jnp.zeros((mc, D), F32)

        acc[...] += _dot_tn(l_ref[...], r_ref[...].astype(BF16))

        @pl.when(i == nt - 1)
        def _():
            o_ref[...] = acc[...].astype(BF16)

    return _pcall(
        body, name=name, grid=(M // mc, nt),
        out_shape=_sds((M, D), BF16),
        in_specs=[pl.BlockSpec((tm, mc), lambda j, i: (i, j)), pl.BlockSpec((tm, D), lambda j, i: (i, 0))],
        out_specs=pl.BlockSpec((mc, D), lambda j, i: (j, 0)),
        scratch_shapes=[pltpu.VMEM((mc, D), F32)],
        compiler_params=_cparams(("arbitrary", "arbitrary")),
    )(lhs, rhs)


def _grad_w_pair(lhs_a, lhs_b, rhs, mc, name, tm):
    T, M = lhs_a.shape
    nt = T // tm

    def body(a_ref, b_ref, r_ref, oa_ref, ob_ref, acc_a, acc_b):
        i = pl.program_id(1)

        @pl.when(i == 0)
        def _():
            acc_a[...] = jnp.zeros((mc, D), F32)
            acc_b[...] = jnp.zeros((mc, D), F32)

        rb = r_ref[...].astype(BF16)
        acc_a[...] += _dot_tn(a_ref[...], rb)
        acc_b[...] += _dot_tn(b_ref[...], rb)

        @pl.when(i == nt - 1)
        def _():
            oa_ref[...] = acc_a[...].astype(BF16)
            ob_ref[...] = acc_b[...].astype(BF16)

    lspec = pl.BlockSpec((tm, mc), lambda j, i: (i, j))
    ospec = pl.BlockSpec((mc, D), lambda j, i: (j, 0))
    return _pcall(
        body, name=name, grid=(M // mc, nt),
        out_shape=(_sds((M, D), BF16), _sds((M, D), BF16)),
        in_specs=[lspec, lspec, pl.BlockSpec((tm, D), lambda j, i: (i, 0))],
        out_specs=(ospec, ospec),
        scratch_shapes=[pltpu.VMEM((mc, D), F32), pltpu.VMEM((mc, D), F32)],
        compiler_params=_cparams(("arbitrary", "arbitrary")),
    )(lhs_a, lhs_b, rhs)


def _mixb_bwd(dh, h, a, g, dc, s, vecs, wg, tm, comm):
    T = h.shape[0]
    nt = T // tm

    def body(dh_ref, h_ref, a_ref, g_ref, dc_ref, s_ref, vec_ref, wg_ref, o_ref, st_ref, sb_ref, g1_ref, g2_ref,
             w1, w2, buf, shf, glu_s, dglu_s, wacc, wb, acc1, acc2, stage, sems, out_sems):
        first = pl.program_id(0) == 0
        load_w = _weight_loads(wg_ref, [(w2, "b_pw2"), (w1, "b_pw1")], sems)

        @pl.when(first)
        def _():
            load_w()
            buf[pl.ds(tm, HALO_B), :] = jnp.zeros((HALO_B, D), F32)
            wacc[...] = jnp.zeros((KB, 8, D), F32)
            _broadcast_taps(vec_ref, wb)
            st_ref[...] = jnp.zeros((SM_F32_ROWS, D), F32)
            sb_ref[...] = jnp.zeros((8, 2 * D), F32)
            acc1[...] = jnp.zeros((2 * D, D), F32)
            acc2[...] = jnp.zeros((D, D), F32)

        def acc(row, val):
            st_ref[pl.ds(row, 1), :] += jnp.sum(val, axis=0, keepdims=True)

        dhv = dh_ref[...]
        dhb = dhv.astype(BF16)
        acc(SM_B_PW2, dhv)
        acc2[...] += _dot_tn(s_ref[...], dhb)
        ds = _dot_nt(dhb, w2[...])
        xhat, rstd = _ln_stats(dc_ref[...])
        ln_g = vec_ref[pl.ds(SM_LN_G, 1), :]
        ln = xhat * ln_g + vec_ref[pl.ds(SM_LN_B, 1), :]
        sl = _sigmoid(ln)
        dln = ds * (sl * (1.0 + ln * (1.0 - sl)))
        acc(SM_LN_G, dln * xhat)
        acc(SM_LN_B, dln)
        dxh = dln * ln_g
        ddc = rstd * (dxh - jnp.mean(dxh, axis=-1, keepdims=True)
                      - xhat * jnp.mean(dxh * xhat, axis=-1, keepdims=True))
        acc(SM_B_BCONV, ddc)
        buf[pl.ds(0, tm), :] = ddc
        _shifted_copies(buf, shf, tm)
        av = a_ref[...].astype(F32)
        sg = _sigmoid(g_ref[...].astype(F32))
        glu_s[...] = av * sg

        _long_conv(buf, shf, wb, dglu_s, tm, [(KB - 1 - j, j) for j in range(KB)], None)
        _long_conv_grad_taps(buf, shf, glu_s, wacc, tm)
        buf[pl.ds(tm, HALO_B), :] = buf[pl.ds(0, HALO_B), :]
        dglu = dglu_s[...]
        da = dglu * sg
        dg = dglu * av * sg * (1.0 - sg)
        sb_ref[pl.ds(0, 1), pl.ds(0, D)] += jnp.sum(da, axis=0, keepdims=True)
        sb_ref[pl.ds(0, 1), pl.ds(D, D)] += jnp.sum(dg, axis=0, keepdims=True)
        dab, dgb = da.astype(BF16), dg.astype(BF16)
        dn = _dot(dab, w1[pl.ds(0, D), :]) + _dot(dgb, w1[pl.ds(D, D), :])
        hv = h_ref[...]
        r = _rms_stat(hv)
        gamma = vec_ref[pl.ds(SM_B_NORM, 1), :]
        nb = (hv * r * gamma).astype(BF16)
        acc1[pl.ds(0, D), :] += _dot_tn(dab, nb)
        acc1[pl.ds(D, D), :] += _dot_tn(dgb, nb)
        dx, dgam = _rms_bwd(dn, hv, r, gamma)
        o_ref[...] = dhv + dx
        st_ref[pl.ds(SM_B_NORM, 1), :] += dgam

        @pl.when(pl.program_id(0) == nt - 1)
        def _():
            st_ref[pl.ds(SM_B_CONV, KB), :] = jnp.sum(wacc[...], axis=1)
            blocks = [(acc1, 0, g1_ref, 0), (acc1, D, g1_ref, D), (acc2, 0, g2_ref, 0)]
            for k, (src, a0, dst, d0) in enumerate(blocks):
                stage[...] = src[pl.ds(a0, D), :].astype(BF16)
                cp = pltpu.make_async_copy(stage, dst.at[pl.ds(d0, D), :], out_sems.at[k])
                cp.start()
                cp.wait()

    rs = functools.partial(_row_spec, rev_nt=nt)
    return _hosted_call(
        body, "mixb_bwd", nt,
        in_specs=[rs(tm, D) for _ in range(6)] + [_full_spec((SM_F32_ROWS, D)), _ANY],
        out_specs=[rs(tm, D), _full_spec((SM_F32_ROWS, D)), _full_spec((8, 2 * D)), _ANY, _ANY],
        out_shape=(_sds((T, D), F32), _sds((SM_F32_ROWS, D), F32), _sds((8, 2 * D), F32),
                   _sds((2 * D, D), BF16), _sds((D, D), BF16)),
        scratch=[pltpu.VMEM((2 * D, D), BF16), pltpu.VMEM((D, D), BF16),
                 pltpu.VMEM((tm + HALO_B, D), F32), pltpu.VMEM((7, tm + HALO_B - 8, D), F32),
                 pltpu.VMEM((tm, D), F32), pltpu.VMEM((tm, D), F32), pltpu.VMEM((KB, 8, D), F32),
                 pltpu.VMEM((KB + 1, 8, D), F32), pltpu.VMEM((2 * D, D), F32), pltpu.VMEM((D, D), F32),
                 pltpu.VMEM((D, D), BF16), pltpu.SemaphoreType.DMA((2 * NDEV,)), pltpu.SemaphoreType.DMA((3,))],
        args=(dh, h, a, g, dc, s, vecs, wg), comm=comm)


def _mixa_bwd(dh, x, b, c, v, cc, y, gam, cw, wg, tm, comm):
    T = x.shape[0]
    nt = T // tm

    def body(dh_ref, x_ref, b_ref, c_ref, v_ref, cc_ref, y_ref, gam_ref, cw_ref, wg_ref,
             o_ref, st_ref, gin_ref, gout_ref, win, wout, buf, acc_in, acc_out, stage, sems, out_sems):
        first = pl.program_id(0) == 0
        load_w = _weight_loads(wg_ref, [(wout, "a_out"), (win, "a_in")], sems)

        @pl.when(first)
        def _():
            load_w()
            buf[pl.ds(tm, HALO_A), :] = jnp.zeros((HALO_A, D), F32)
            st_ref[...] = jnp.zeros((8, D), F32)
            acc_in[...] = jnp.zeros((3 * D, D), F32)
            acc_out[...] = jnp.zeros((D, D), F32)

        dhv = dh_ref[...]
        dhb = dhv.astype(BF16)
        dy = _dot_nt(dhb, wout[...])
        cval = c_ref[...].astype(F32)
        vval = v_ref[...].astype(F32)
        d_b = (dy * cc_ref[...].astype(F32)).astype(BF16)
        buf[pl.ds(0, tm), :] = dy * b_ref[...].astype(F32)
        cv = cval * vval
        dcv = jnp.zeros((tm, D), F32)
        for j in range(KA):
            sh = buf[pl.ds(j, tm), :]
            k = KA - 1 - j
            dcv = dcv + cw_ref[pl.ds(k, 1), :] * sh
            st_ref[pl.ds(1 + k, 1), :] += jnp.sum(cv * sh, axis=0, keepdims=True)
        buf[pl.ds(tm, HALO_A), :] = buf[pl.ds(0, HALO_A), :]
        d_c = (dcv * vval).astype(BF16)
        d_v = (dcv * cval).astype(BF16)
        xv = x_ref[...]
        r = _rms_stat(xv)
        gamma = gam_ref[...]
        nb = (xv * r * gamma).astype(BF16)
        for j, piece in enumerate((d_b, d_c, d_v)):
            acc_in[pl.ds(j * D, D), :] += _dot_tn(piece, nb)
        acc_out[...] += _dot_tn(y_ref[...], dhb)
        dn = _dot(d_b, win[pl.ds(0, D), :]) + _dot(d_c, win[pl.ds(D, D), :]) + _dot(d_v, win[pl.ds(2 * D, D), :])
        dx, dgam = _rms_bwd(dn, xv, r, gamma)
        o_ref[...] = dhv + dx
        st_ref[pl.ds(0, 1), :] += dgam

        @pl.when(pl.program_id(0) == nt - 1)
        def _():
            blocks = [(acc_in, j * D, gin_ref, j * D) for j in range(3)] + [(acc_out, 0, gout_ref, 0)]
            for k, (acc, a0, dst, d0) in enumerate(blocks):
                stage[...] = acc[pl.ds(a0, D), :].astype(BF16)
                cp = pltpu.make_async_copy(stage, dst.at[pl.ds(d0, D), :], out_sems.at[k])
                cp.start()
                cp.wait()

    rs = functools.partial(_row_spec, rev_nt=nt)
    return _hosted_call(
        body, "mixa_bwd", nt,
        in_specs=[rs(tm, D) for _ in range(7)] + [_full_spec((1, D)), _full_spec((KA, D)), _ANY],
        out_specs=[rs(tm, D), _full_spec((8, D)), _ANY, _ANY],
        out_shape=(_sds((T, D), F32), _sds((8, D), F32), _sds((3 * D, D), BF16), _sds((D, D), BF16)),
        scratch=[pltpu.VMEM((3 * D, D), BF16), pltpu.VMEM((D, D), BF16), pltpu.VMEM((tm + HALO_A, D), F32),
                 pltpu.VMEM((3 * D, D), F32), pltpu.VMEM((D, D), F32), pltpu.VMEM((D, D), BF16),
                 pltpu.SemaphoreType.DMA((2 * NDEV,)), pltpu.SemaphoreType.DMA((4,))],
        args=(dh, x, b, c, v, cc, y, gam, cw, wg), comm=comm)


def _sum_small(sm, rp):
    def body(sm_ref, rp_ref, osm, orp, oloss):
        a = sm_ref[0]
        b = rp_ref[0]
        for s in range(1, NDEV):
            a = a + sm_ref[s]
            b = b + rp_ref[s]
        osm[...] = a
        orp[...] = b
        oloss[...] = jnp.zeros((8, 128), F32) + jnp.sum(b[4:5, :], axis=-1, keepdims=True)

    return _pcall(
        body, name="sum_small_grads",
        out_shape=(_sds((SM_F32_ROWS, 128), F32), _sds((8, D), F32), _sds((8, 128), F32)),
        in_specs=[pl.BlockSpec(memory_space=pltpu.VMEM)] * 2,
        out_specs=tuple(pl.BlockSpec(memory_space=pltpu.VMEM) for _ in range(3)),
        compiler_params=_cparams(),
    )(sm, rp)


def _adam_math(w, g, m, v):
    mn = B1 * m + (1.0 - B1) * g
    vn = B2 * v + (1.0 - B2) * (g * g)
    m_hat = mn / (1.0 - B1 ** STEP)
    v_hat = vn / (1.0 - B2 ** STEP)
    return -LR * (m_hat / (jnp.sqrt(v_hat) + ADAM_EPS) + WD * w), mn, vn


def _finish_weight(land, keys, transposed, w, m, v, name, row0=0, after=None):
    layers, rows, cols = w.shape
    n = N_ROWS[keys[0]]
    n_pad = -(-n // 128) * 128 if transposed else n

    def body(w_ref, m_ref, v_ref, land_ref, *rest):
        og, od, om, ov, buf, wv, mv, vv, sg, sd, sm, sv, sems = rest[-13:]
        for l in range(layers):
            off = LD_OFF[keys[l]] - row0
            loads = [pltpu.make_async_copy(land_ref.at[:, pl.ds(off, n), :], buf, sems.at[0])]
            loads += [pltpu.make_async_copy(src.at[l], dst, sems.at[1 + i])
                      for i, (src, dst) in enumerate(((w_ref, wv), (m_ref, mv), (v_ref, vv)))]
            for cp in loads:
                cp.start(priority=LOCAL_THREAD)
            for cp in loads:
                cp.wait()
            g = buf[0].astype(F32)
            for s in range(1, NCHIP):
                g = g + buf[s].astype(F32)
            if transposed:
                if n_pad != n:
                    g = jnp.concatenate([g, jnp.zeros((n_pad - n, D), F32)], axis=0)
                g = g.T[:, :n]
            sg[...] = g
            sd[...], sm[...], sv[...] = _adam_math(wv[...], g, mv[...], vv[...])
            stores = [pltpu.make_async_copy(src, dst.at[l], sems.at[4 + i])
                      for i, (src, dst) in enumerate(((sg, og), (sd, od), (sm, om), (sv, ov)))]
            for cp in stores:
                cp.start(priority=LOCAL_THREAD)
            for cp in stores:
                cp.wait()

    shp = _sds(w.shape, F32)
    tile = pltpu.VMEM((rows, cols), F32)
    args = (w, m, v, land) + (() if after is None else (after,))
    return _pcall(
        body, name="finish_" + name, out_shape=(shp,) * 4,
        in_specs=[_ANY] * len(args), out_specs=(_ANY,) * 4,
        scratch_shapes=[pltpu.VMEM((NCHIP, n, D), BF16)] + [tile] * 7 + [pltpu.SemaphoreType.DMA((8,))],
        compiler_params=_cparams(),
    )(*args)


SMALL_PARAMS = (("sm", SM_B_NORM, 1), ("sm", SM_B_PW1, 2), ("sm", SM_B_CONV, KB), ("sm", SM_B_BCONV, 1),
                ("sm", SM_LN_G, 1), ("sm", SM_LN_B, 1), ("sm", SM_B_PW2, 1), ("sm", SM_A_CONV, KA),
                ("rp", 0, 1), ("rp", 1, 2), ("rp", 3, 1))


def _adamw_small(g_small, g_repl, triples):
    n = len(SMALL_PARAMS)

    def body(*refs):
        gs_ref, gr_ref = refs[0], refs[1]
        ins = refs[2:2 + 3 * n]
        outs = refs[2 + 3 * n:]
        for i, (src, r0, nr) in enumerate(SMALL_PARAMS):
            w_ref, m_ref, v_ref = ins[3 * i:3 * i + 3]
            g = (gs_ref if src == "sm" else gr_ref)[pl.ds(r0, nr), :]
            if i == 1:
                g = jnp.concatenate([g[0:1], g[1:2]], axis=1)
            lead = (0,) if len(w_ref.shape) == 3 else ()
            idx = lead + (slice(None), slice(None))
            vals = (g,) + _adam_math(w_ref[idx], g, m_ref[idx], v_ref[idx])
            for o_ref, val in zip(outs[4 * i:4 * i + 4], vals):
                o_ref[idx] = val

    flat = [a for t in triples for a in t]
    out_shape = tuple(_sds(t[0].shape, F32) for t in triples for _ in range(4))
    vm = pl.BlockSpec(memory_space=pltpu.VMEM)
    res = _pcall(
        body, name="adamw_small", out_shape=out_shape,
        in_specs=[vm] * (2 + len(flat)), out_specs=tuple(vm for _ in out_shape),
        compiler_params=_cparams(),
    )(g_small, g_repl, *flat)
    return [tuple(res[4 * i:4 * i + 4]) for i in range(n)]


def kernel(x, a_norm, a_w_in, a_conv, a_w_out, b_norm, b_w_pw1, b_b_pw1, b_conv, b_b_conv, b_ln_g, b_ln_b, b_w_pw2, b_b_pw2, ffn_norm, ffn_w_gate, ffn_w_up, ffn_w_down, final_norm, loss_target, m_a_norm, m_a_w_in, m_a_conv, m_a_w_out, m_b_norm, m_b_w_pw1, m_b_b_pw1, m_b_conv, m_b_b_conv, m_b_ln_g, m_b_ln_b, m_b_w_pw2, m_b_b_pw2, m_ffn_norm, m_ffn_w_gate, m_ffn_w_up, m_ffn_w_down, m_final_norm, v_a_norm, v_a_w_in, v_a_conv, v_a_w_out, v_b_norm, v_b_w_pw1, v_b_b_pw1, v_b_conv, v_b_b_conv, v_b_ln_g, v_b_ln_b, v_b_w_pw2, v_b_b_pw2, v_ffn_norm, v_ffn_w_gate, v_ffn_w_up, v_ffn_w_down, v_final_norm):
    T = x.shape[1]
    tm = min(TM, T)
    tma = min(TM_A, T)
    tw = min(TM_DW, T)
    xs = x.reshape(T, D)
    tgt = loss_target.reshape(T, D)

    def rows_first(a):
        return jnp.swapaxes(a, 1, 2)

    shard, small = _pack_shard(
        (a_w_in, a_w_out, b_w_pw1, b_w_pw2, rows_first(ffn_w_gate), rows_first(ffn_w_up), ffn_w_down),
        (b_norm, b_b_pw1, b_conv, b_b_conv, b_ln_g, b_ln_b, b_b_pw2, a_conv))
    wg, sm_all = _all_gather_first(shard, small)
    vecs = sm_all.transpose(1, 0, 2).reshape(SM_F32_ROWS, D)
    bias1 = sm_all[:, SM_B_PW1:SM_B_PW1 + 2, :].reshape(1, 2 * D)
    cw_a = vecs[SM_A_CONV:SM_A_CONV + KA]
    fn0, fn1 = ffn_norm[0:1], ffn_norm[1:2]
    fin = final_norm.reshape(1, D)

    h1, bq, cq, vq, ccq, yq, wg = _mixa_fwd(xs, a_norm, cw_a, wg, tma, _ag_comm(shard, wg, ("g0", "u0", "d0")))
    h2, n1, g0, u0, gu0, wg = _ffn_fwd(h1, fn0, wg, 0, tma, _ag_comm(shard, wg, ("b_pw1", "b_pw2"), (T // tma) // 2))
    h3, aq, gq, dcq, sq, wg = _mixb_fwd(h2, vecs, bias1, wg, tm, _ag_comm(shard, wg, ("g1", "u1", "d1"), 1))
    dh4, n3, g1, u1, gu1, st_fin = _ffn_fwd(h3, fn1, wg, 1, tma, None, head=(tgt, fin))

    def by_dest(gw, key):
        return gw.reshape(NDEV, N_ROWS[key], D)

    gw_d1 = by_dest(_grad_w(gu1, dh4, FH, "grad_down1", tw), "d1")
    dh3, dg1, du1, st_f1 = _ffn_bwd_dx(dh4, h3, g1, u1, fn1, wg, 1, tma, None)
    gw_g1, gw_u1 = _grad_w_pair(dg1, du1, n3, FH, "grad_gate_up1", tw)
    gw_g1, gw_u1 = by_dest(gw_g1, "g1"), by_dest(gw_u1, "u1")
    keys = ("g1", "u1", "d1")
    dh2, st_b, st_b1, gw_pw1, gw_pw2, land = _mixb_bwd(
        dh3, h2, aq, gq, dcq, sq, vecs, wg, tm, _rs_comm(_pair_reduce([gw_g1, gw_u1, gw_d1], keys), keys, None))
    gw_pw1, gw_pw2 = by_dest(gw_pw1, "b_pw1"), by_dest(gw_pw2, "b_pw2")
    gw_d0 = by_dest(_grad_w(gu0, dh2, FH, "grad_down0", tw), "d0")
    keys = ("b_pw1", "b_pw2", "d0")
    dh1, dg0, du0, st_f0, land = _ffn_bwd_dx(
        dh2, h1, g0, u0, fn0, wg, 0, tma, _rs_comm(_pair_reduce([gw_pw1, gw_pw2, gw_d0], keys), keys, land))
    gw_g0, gw_u0 = _grad_w_pair(dg0, du0, n1, FH, "grad_gate_up0", tw)
    gw_g0, gw_u0 = by_dest(gw_g0, "g0"), by_dest(gw_u0, "u0")
    keys = ("g0", "u0")
    dx, st_a, gw_in, gw_out, land = _mixa_bwd(
        dh1, xs, bq, cq, vq, ccq, yq, a_norm, cw_a, wg, tm,
        _rs_comm(_pair_reduce([gw_g0, gw_u0], keys), keys, land))
    gw_in, gw_out = by_dest(gw_in, "a_in"), by_dest(gw_out, "a_out")

    st_small = st_b.at[SM_A_CONV:SM_A_CONV + KA].set(st_a[1:1 + KA])
    sm_dest = st_small.reshape(SM_F32_ROWS, NDEV, 128).transpose(1, 0, 2)
    sm_dest = sm_dest.at[:, SM_B_PW1:SM_B_PW1 + 2, :].set(st_b1[0].reshape(NDEV, 2, 128))
    repl = jnp.concatenate([st_a[0:1], st_f0[0:1], st_f1[0:1], st_fin[0:1], st_fin[1:2],
                            jnp.zeros((3, D), F32)], axis=0)[None]
    last = ("a_out", "a_in")
    p_last, sm_land, rp_land = _pair_reduce([gw_out, gw_in], last, side=(sm_dest, repl))
    sems, p_last, land_last, token = _chip_start(p_last)

    big = [("ffn_w_gate", ("g0", "g1"), False,
            rows_first(ffn_w_gate), rows_first(m_ffn_w_gate), rows_first(v_ffn_w_gate)),
           ("ffn_w_up", ("u0", "u1"), False, rows_first(ffn_w_up), rows_first(m_ffn_w_up), rows_first(v_ffn_w_up)),
           ("ffn_w_down", ("d0", "d1"), False, ffn_w_down, m_ffn_w_down, v_ffn_w_down),
           ("a_w_out", ("a_out",), False, a_w_out, m_a_w_out, v_a_w_out),
           ("b_w_pw1", ("b_pw1",), True, b_w_pw1, m_b_w_pw1, v_b_w_pw1),
           ("b_w_pw2", ("b_pw2",), False, b_w_pw2, m_b_w_pw2, v_b_w_pw2),
           ("a_w_in", ("a_in",), True, a_w_in, m_a_w_in, v_a_w_in)]
    res, after = {}, token
    for name, keys, transposed, w, m, v in big:
        if keys[0] not in last:
            res[name] = _finish_weight(land, keys, transposed, w, m, v, name, after=after)
            after = res[name][1]
    _, land_last = _chip_wait(sems, p_last, land_last, after)
    land_last = _own_slot(land_last, p_last)
    for name, keys, transposed, w, m, v in big:
        if keys[0] in last:
            res[name] = _finish_weight(land_last, keys, transposed, w, m, v, name, row0=LD_OFF[last[0]])
    for name in ("ffn_w_gate", "ffn_w_up"):
        res[name] = [rows_first(a) for a in res[name]]

    g_small, g_repl, loss8 = _sum_small(sm_land, rp_land)
    loss = loss8[0, 0]

    small_names = ["b_norm", "b_b_pw1", "b_conv", "b_b_conv", "b_ln_g", "b_ln_b", "b_b_pw2", "a_conv",
                   "a_norm", "ffn_norm", "final_norm"]
    triples = [(b_norm, m_b_norm, v_b_norm), (b_b_pw1, m_b_b_pw1, v_b_b_pw1), (b_conv, m_b_conv, v_b_conv),
               (b_b_conv, m_b_b_conv, v_b_b_conv), (b_ln_g, m_b_ln_g, v_b_ln_g), (b_ln_b, m_b_ln_b, v_b_ln_b),
               (b_b_pw2, m_b_b_pw2, v_b_b_pw2), (a_conv, m_a_conv, v_a_conv), (a_norm, m_a_norm, v_a_norm),
               (ffn_norm, m_ffn_norm, v_ffn_norm),
               (fin, m_final_norm.reshape(1, D), v_final_norm.reshape(1, D))]
    for name, quad in zip(small_names, _adamw_small(g_small, g_repl, triples)):
        res[name] = quad
    res["final_norm"] = tuple(a.reshape(D) for a in res["final_norm"])

    order = ["a_norm", "a_w_in", "a_conv", "a_w_out", "b_norm", "b_w_pw1", "b_b_pw1", "b_conv", "b_b_conv",
             "b_ln_g", "b_ln_b", "b_w_pw2", "b_b_pw2", "ffn_norm", "ffn_w_gate", "ffn_w_up", "ffn_w_down",
             "final_norm"]
    out = [loss, dx.reshape(1, T, D)]
    for j in range(4):
        out += [res[k][j] for k in order]
    return tuple(out)
```
